```python
import math
import jax, jax.numpy as jnp
from jax import lax
import numpy as np

D_MODEL = 1024
BATCH = 4
SEQ = 4096
DEPTH = 2

GRID_W = 64
CTX_LEN = 256
N_EVEN = (DEPTH + 1) // 2
N_ODD = DEPTH // 2

C_HYENA = D_MODEL // 2
C_NA = D_MODEL - C_HYENA
NA_HEAD_DIM = 32
NA_HEADS = C_NA // NA_HEAD_DIM
NA_WIN_R = 8
NA_WIN_C = 16
NA_QB = 16
NA_BAND = NA_QB + NA_WIN_C
HYENA_ORDER = 2
HYENA_SHORT = 3
HYENA_EMB = 33
HYENA_BANDS = (HYENA_EMB - 1) // 2
HYENA_FILTER_HID = 64
HYENA_FAST_DECAY = 0.3
HYENA_SLOW_DECAY = 1.5
HYENA_TARGET = 1e-2
F_GROUPS = 4
D_FF = 2816
N_EXPERTS = 8
TOP_K = 2
D_FF_EXPERT = 3584
N_MOD = 6
EPS = 1e-6
NEG_INF = -1e30

kernel_name = "hybrid_hyena_natten_fnet_moe_dit"


def rmsnorm(x, g):
    x32 = x.astype(jnp.float32)
    y = x32 * lax.rsqrt(jnp.mean(x32 * x32, axis=-1, keepdims=True) + EPS)
    return (y * g.astype(jnp.float32)).astype(x.dtype)


def modulate(h, shift, scale):
    return h * (1 + scale) + shift


def ada_params(cvec, w, b):
    return jnp.split(jax.nn.silu(cvec) @ w + b, N_MOD, axis=-1)


def swiglu(h, w1, w3, w2):
    return (jax.nn.silu(h @ w1) * (h @ w3)) @ w2


def moe_swiglu(h, w_router, b_router, w1, w3, w2):
    logits = (h @ w_router).astype(jnp.float32) + b_router.astype(jnp.float32)
    top_val, top_idx = lax.top_k(logits, TOP_K)
    top_w = jax.nn.softmax(top_val, axis=-1)
    gate = jnp.einsum('blk,blke->ble', top_w,
                      jax.nn.one_hot(top_idx, N_EXPERTS, dtype=jnp.float32)).astype(h.dtype)
    out = jnp.zeros_like(h)
    for e in range(N_EXPERTS):
        out = out + gate[..., e:e + 1] * swiglu(h, w1[e], w3[e], w2[e])
    return out


def hyena_filters(L, w1, b1, w2, b2, w3, freq):
    f32 = jnp.float32
    w1, b1, w2, b2, w3, freq = (a.astype(f32) for a in (w1, b1, w2, b2, w3, freq))
    t = jnp.linspace(0.0, 1.0, L, dtype=f32)[:, None]
    bands = jnp.linspace(1e-4, HYENA_BANDS - 1, HYENA_BANDS, dtype=f32)
    ang = (2.0 * math.pi / L) * jnp.arange(L, dtype=f32)[:, None] * bands[None, :]
    feats = jnp.concatenate([t, jnp.cos(ang), -jnp.sin(ang)], axis=-1)
    h = jnp.sin(freq[0] * (feats @ w1 + b1))
    h = jnp.sin(freq[1] * (h @ w2 + b2))
    h = (h @ w3).reshape(L, HYENA_ORDER, 2, C_HYENA)
    deltas = jnp.abs(jnp.linspace(math.log(HYENA_TARGET) / HYENA_SLOW_DECAY,
                                  math.log(HYENA_TARGET) / HYENA_FAST_DECAY, C_HYENA, dtype=f32))
    h = h * jnp.exp(-t * deltas[None, :])[:, None, None, :]
    k = jnp.concatenate([h[:, :, 0], jnp.zeros((1, HYENA_ORDER, C_HYENA), f32), h[:0:-1, :, 1]], axis=0)
    return k / (jnp.sum(jnp.abs(k), axis=0, keepdims=True) + EPS)


def long_conv(u, k, skip):
    L = u.shape[1]
    u32 = u.astype(jnp.float32)
    U = jnp.fft.rfft(u32, n=2 * L, axis=1)
    K = jnp.fft.rfft(k, axis=0)
    y = jnp.fft.irfft(U * K[None], n=2 * L, axis=1)[:, :L]
    return (y + u32 * skip.astype(jnp.float32)).astype(u.dtype)


def hyena_mixer(z, short_w, short_b, f_w1, f_b1, f_w2, f_b2, f_w3, f_freq, skip):
    L = z.shape[1]
    zp = jnp.pad(z, ((0, 0), (1, 1), (0, 0)))
    z = zp[:, :-2] * short_w[0] + zp[:, 1:-1] * short_w[1] + zp[:, 2:] * short_w[2] + short_b
    v, x1, x2 = jnp.split(z, 3, axis=-1)
    filt = hyena_filters(L, f_w1, f_b1, f_w2, f_b2, f_w3, f_freq)
    y = x1 * long_conv(v, filt[:, 0], skip[0])
    y = x2 * long_conv(y, filt[:, 1], skip[1])
    return y


def to_heads(t):
    B, L, _ = t.shape
    return t.reshape(B, L, NA_HEADS, NA_HEAD_DIM).transpose(0, 2, 1, 3)


def to_grid(t, rows):
    B = t.shape[0]
    return t.reshape(B, rows, GRID_W, NA_HEADS, NA_HEAD_DIM).transpose(0, 3, 1, 2, 4)


def context_attention(q, k, v):
    s = jnp.einsum('bhqd,bhkd->bhqk', q, k).astype(jnp.float32) * (NA_HEAD_DIM ** -0.5)
    p = jax.nn.softmax(s, axis=-1).astype(v.dtype)
    o = jnp.einsum('bhqk,bhkd->bhqd', p, v)
    B, H, L, dh = o.shape
    return o.transpose(0, 2, 1, 3).reshape(B, L, H * dh)


def neighborhood_attention(q, k, v, k_ctx, v_ctx, rpb):
    B, H, rows, W, dh = q.shape
    win_r = min(NA_WIN_R, rows)
    ncb = W // NA_QB
    scale = dh ** -0.5
    qcol = jnp.arange(W).reshape(ncb, NA_QB)
    c_start = jnp.clip(qcol - NA_WIN_C // 2, 0, W - NA_WIN_C)
    band_start = jnp.clip(jnp.arange(ncb) * NA_QB - NA_WIN_C // 2, 0, W - NA_BAND)
    band_cols = band_start[:, None] + jnp.arange(NA_BAND)
    kcol = band_cols[:, None, :]
    col_ok = (kcol >= c_start[..., None]) & (kcol < c_start[..., None] + NA_WIN_C)
    col_idx = jnp.clip(kcol - qcol[..., None] + NA_WIN_C - 1, 0, 2 * NA_WIN_C - 2)
    rpb32 = rpb.astype(jnp.float32)
    n_loc = win_r * NA_BAND

    def one_row(r):
        r0 = jnp.clip(r - win_r // 2, 0, rows - win_r)
        q_r = lax.dynamic_index_in_dim(q, r, axis=2, keepdims=False).reshape(B, H, ncb, NA_QB, dh)
        k_r = jnp.take(lax.dynamic_slice_in_dim(k, r0, win_r, axis=2), band_cols, axis=3)
        v_r = jnp.take(lax.dynamic_slice_in_dim(v, r0, win_r, axis=2), band_cols, axis=3)
        row_idx = r0 + jnp.arange(win_r) - r + NA_WIN_R - 1
        bias = rpb32[:, row_idx[None, None, :, None], col_idx[:, :, None, :]]
        s_loc = jnp.einsum('bhjqd,bhrjkd->bhjqrk', q_r, k_r).astype(jnp.float32) * scale + bias
        s_loc = jnp.where(col_ok[:, :, None, :], s_loc, NEG_INF)
        s_ctx = jnp.einsum('bhjqd,bhcd->bhjqc', q_r, k_ctx).astype(jnp.float32) * scale
        logits = jnp.concatenate([s_loc.reshape(B, H, ncb, NA_QB, n_loc), s_ctx], axis=-1)
        p = jax.nn.softmax(logits, axis=-1).astype(v.dtype)
        p_loc = p[..., :n_loc].reshape(B, H, ncb, NA_QB, win_r, NA_BAND)
        o = (jnp.einsum('bhjqrk,bhrjkd->bhjqd', p_loc, v_r)
             + jnp.einsum('bhjqc,bhcd->bhjqd', p[..., n_loc:], v_ctx))
        return o.reshape(B, H, W, dh)

    out = lax.map(one_row, jnp.arange(rows))
    return out.transpose(1, 0, 3, 2, 4).reshape(B, rows * W, H * dh)


def fourier_mix(h):
    B, L, D = h.shape
    hg = h.astype(jnp.float32).reshape(B, L, F_GROUPS, D // F_GROUPS)
    y = jnp.fft.fftn(hg, axes=(1, 3), norm='ortho').real
    return y.reshape(B, L, D).astype(h.dtype)


def setup_inputs(seed: int = 0) -> dict:
    key = jax.random.key(seed)
    ks = iter(jax.random.split(key, 32))
    f32 = jnp.float32
    D = D_MODEL

    def dense(shape, fan_in):
        return jax.random.normal(next(ks), shape, f32) * (fan_in ** -0.5)

    def small(shape, s):
        return jax.random.normal(next(ks), shape, f32) * s

    return {
        'x': small((BATCH, SEQ, D), 1.0),
        'c': small((BATCH, D), 1.0),
        'ctx': small((BATCH, CTX_LEN, D), 1.0),
        'c_ctx': small((D,), 1.0),
        'w_ada': dense((DEPTH, D, N_MOD * D), D),
        'b_ada': small((DEPTH, N_MOD * D), 0.02),
        'norm_g': 1.0 + small((DEPTH, 2, D), 0.02),
        'w_in': dense((N_EVEN, D, 3 * C_HYENA + 3 * C_NA), D),
        'hy_short_w': dense((N_EVEN, HYENA_SHORT, 3 * C_HYENA), HYENA_SHORT),
        'hy_short_b': small((N_EVEN, 3 * C_HYENA), 0.02),
        'hy_f_w1': dense((N_EVEN, HYENA_EMB, HYENA_FILTER_HID), HYENA_EMB),
        'hy_f_b1': small((N_EVEN, HYENA_FILTER_HID), 0.02),
        'hy_f_w2': dense((N_EVEN, HYENA_FILTER_HID, HYENA_FILTER_HID), HYENA_FILTER_HID),
        'hy_f_b2': small((N_EVEN, HYENA_FILTER_HID), 0.02),
        'hy_f_w3': dense((N_EVEN, HYENA_FILTER_HID, HYENA_ORDER * 2 * C_HYENA), HYENA_FILTER_HID),
        'hy_f_freq': 1.0 + small((N_EVEN, 2, HYENA_FILTER_HID), 0.02),
        'hy_skip': small((N_EVEN, HYENA_ORDER, C_HYENA), 0.5),
        'na_rpb': small((N_EVEN, NA_HEADS, 2 * NA_WIN_R - 1, 2 * NA_WIN_C - 1), 0.1),
        'w_mix_out': dense((N_EVEN, D, D), D),
        'ffn_w1': dense((N_EVEN, D, D_FF), D),
        'ffn_w3': dense((N_EVEN, D, D_FF), D),
        'ffn_w2': dense((N_EVEN, D_FF, D), D_FF),
        'w_fourier': dense((N_ODD, D, D), D),
        'w_router': dense((N_ODD, D, N_EXPERTS), D),
        'b_router': small((N_ODD, N_EXPERTS), 0.01),
        'moe_w1': dense((N_ODD, N_EXPERTS, D, D_FF_EXPERT), D),
        'moe_w3': dense((N_ODD, N_EXPERTS, D, D_FF_EXPERT), D),
        'moe_w2': dense((N_ODD, N_EXPERTS, D_FF_EXPERT, D), D_FF_EXPERT),
        'final_g': 1.0 + small((D,), 0.02),
    }


def reference(x, c, ctx, c_ctx, w_ada, b_ada, norm_g, w_in, hy_short_w, hy_short_b,
              hy_f_w1, hy_f_b1, hy_f_w2, hy_f_b2, hy_f_w3, hy_f_freq, hy_skip, na_rpb,
              w_mix_out, ffn_w1, ffn_w3, ffn_w2, w_fourier, w_router, b_router,
              moe_w1, moe_w3, moe_w2, final_g):
    rows = x.shape[1] // GRID_W
    kv_off = 3 * C_HYENA + C_NA
    xc = ctx
    for layer in range(DEPTH):
        even = layer % 2 == 0
        update_ctx = any(l % 2 == 0 for l in range(layer + 1, DEPTH))
        g_mix, g_ffn = norm_g[layer, 0], norm_g[layer, 1]
        sh1, sc1, gt1, sh2, sc2, gt2 = ada_params(c[:, None, :], w_ada[layer], b_ada[layer])
        if even or update_ctx:
            csh1, csc1, cgt1, csh2, csc2, cgt2 = ada_params(c_ctx[None, None, :], w_ada[layer], b_ada[layer])
        if even:
            i = layer // 2
            hy = (hy_short_w[i], hy_short_b[i], hy_f_w1[i], hy_f_b1[i], hy_f_w2[i], hy_f_b2[i],
                  hy_f_w3[i], hy_f_freq[i], hy_skip[i])
            h = modulate(rmsnorm(x, g_mix), sh1, sc1)
            z = h @ w_in[i]
            q, k, v = jnp.split(z[..., 3 * C_HYENA:], 3, axis=-1)
            hc = modulate(rmsnorm(xc, g_mix), csh1, csc1)
            if update_ctx:
                zc = hc @ w_in[i]
                kvc = zc[..., kv_off:]
            else:
                kvc = hc @ w_in[i][:, kv_off:]
            kc, vc = jnp.split(kvc, 2, axis=-1)
            kc, vc = to_heads(kc), to_heads(vc)
            y_na = neighborhood_attention(to_grid(q, rows), to_grid(k, rows), to_grid(v, rows), kc, vc, na_rpb[i])
            y_hy = hyena_mixer(z[..., :3 * C_HYENA], *hy)
            x = x + gt1 * (jnp.concatenate([y_hy, y_na], axis=-1) @ w_mix_out[i])
            if update_ctx:
                yc_na = context_attention(to_heads(zc[..., 3 * C_HYENA:kv_off]), kc, vc)
                yc_hy = hyena_mixer(zc[..., :3 * C_HYENA], *hy)
                xc = xc + cgt1 * (jnp.concatenate([yc_hy, yc_na], axis=-1) @ w_mix_out[i])
            x = x + gt2 * swiglu(modulate(rmsnorm(x, g_ffn), sh2, sc2), ffn_w1[i], ffn_w3[i], ffn_w2[i])
            if update_ctx:
                xc = xc + cgt2 * swiglu(modulate(rmsnorm(xc, g_ffn), csh2, csc2), ffn_w1[i], ffn_w3[i], ffn_w2[i])
        else:
            j = layer // 2
            x = x + gt1 * (fourier_mix(modulate(rmsnorm(x, g_mix), sh1, sc1)) @ w_fourier[j])
            x = x + gt2 * moe_swiglu(modulate(rmsnorm(x, g_ffn), sh2, sc2), w_router[j], b_router[j],
                                     moe_w1[j], moe_w3[j], moe_w2[j])
            if update_ctx:
                xc = xc + cgt1 * (fourier_mix(modulate(rmsnorm(xc, g_mix), csh1, csc1)) @ w_fourier[j])
                xc = xc + cgt2 * moe_swiglu(modulate(rmsnorm(xc, g_ffn), csh2, csc2), w_router[j], b_router[j],
                                            moe_w1[j], moe_w3[j], moe_w2[j])
    return rmsnorm(x, final_g)
```

```python
import functools
import math

import numpy as np
import jax
import jax.numpy as jnp
from jax import lax
from jax.experimental import pallas as pl
from jax.experimental.pallas import tpu as pltpu

F32 = jnp.float32
BF16 = jnp.bfloat16
HIGHEST = lax.Precision.HIGHEST

GRID_W = 64
NA_HEAD_DIM = 32
NA_WIN_R = 8
NA_WIN_C = 16
HYENA_EMB = 33
HYENA_BANDS = (HYENA_EMB - 1) // 2
HYENA_FAST_DECAY = 0.3
HYENA_SLOW_DECAY = 1.5
HYENA_TARGET = 1e-2
F_GROUPS = 4
N_MOD = 6
EPS = 1e-6
NEG_INF = -1e30

FFT_A = 64
FFT_R = 128
FFT_KA = FFT_A // 2 + 1
FFT_KA_PAD = 40
FM_A = 64

VMEM_LIMIT = 48 * 1024 * 1024


def _cparams(sem):
    return pltpu.CompilerParams(dimension_semantics=sem, vmem_limit_bytes=VMEM_LIMIT)


def _dot(a, b):
    return jnp.dot(a, b, preferred_element_type=F32)


def _norm_mod(x, g, shift, scale):
    ms = jnp.mean(x * x, axis=-1, keepdims=True)
    y = x * lax.rsqrt(ms + EPS) * g
    return y * (1.0 + scale) + shift


def _ada_body(c_ref, w_ref, b_ref, o_ref):
    cv = c_ref[...]
    s = cv * jax.nn.sigmoid(cv)
    o_ref[0] = jnp.dot(s, w_ref[0], precision=HIGHEST, preferred_element_type=F32) + b_ref[0]


def _ada(cvec, w_ada, b_ada):
    depth, d, n = w_ada.shape
    rows = cvec.shape[0]
    bn = n // 4
    return pl.pallas_call(
        _ada_body,
        out_shape=jax.ShapeDtypeStruct((depth, rows, n), F32),
        grid=(depth, n // bn),
        in_specs=[pl.BlockSpec((rows, d), lambda l, j: (0, 0)),
                  pl.BlockSpec((1, d, bn), lambda l, j: (l, 0, j)),
                  pl.BlockSpec((1, 1, bn), lambda l, j: (l, 0, j))],
        out_specs=pl.BlockSpec((1, rows, bn), lambda l, j: (l, 0, j)),
        compiler_params=_cparams(("parallel", "parallel")),
        name="ada",
    )(cvec, w_ada, b_ada.reshape(depth, 1, n))


def _inproj_body(x_ref, xp_ref, xn_ref, g_ref, sh_ref, sc_ref, why_ref, wqkv_ref, sw_ref, sb_ref,
                 v_ref, x1_ref, x2_ref, q_ref, k_ref, va_ref, *, n_tiles, q_scale, c_hy, c_na):
    i = pl.program_id(1)
    g, sh, sc = g_ref[...], sh_ref[0], sc_ref[0]
    h = _norm_mod(x_ref[0], g, sh, sc).astype(BF16)
    zh = _dot(h, why_ref[...])
    hp = _norm_mod(xp_ref[0], g, sh, sc).astype(BF16)
    hn = _norm_mod(xn_ref[0], g, sh, sc).astype(BF16)
    zp = _dot(hp, why_ref[...])[7:8]
    zn = _dot(hn, why_ref[...])[0:1]
    zp = jnp.where(i > 0, zp, 0.0)
    zn = jnp.where(i < n_tiles - 1, zn, 0.0)
    tm = zh.shape[0]
    row = lax.broadcasted_iota(jnp.int32, zh.shape, 0)
    z_m1 = jnp.where(row == 0, zp, pltpu.roll(zh, 1, 0))
    z_p1 = jnp.where(row == tm - 1, zn, pltpu.roll(zh, tm - 1, 0))
    sw = sw_ref[...]
    zc = z_m1 * sw[0:1] + zh * sw[1:2] + z_p1 * sw[2:3] + sb_ref[...]
    v_ref[0] = zc[:, 0:c_hy].astype(BF16)
    x1_ref[0] = zc[:, c_hy:2 * c_hy].astype(BF16)
    x2_ref[0] = zc[:, 2 * c_hy:3 * c_hy].astype(BF16)
    zq = _dot(h, wqkv_ref[...])
    q_ref[0] = (zq[:, 0:c_na] * q_scale).astype(BF16)
    k_ref[0] = zq[:, c_na:2 * c_na].astype(BF16)
    va_ref[0] = zq[:, 2 * c_na:3 * c_na].astype(BF16)


def _inproj(x, g, shift, scale, w_hy, w_qkv, short_w, short_b, tm=512):
    b, s, d = x.shape
    c_hy = w_hy.shape[1] // 3
    c_na = w_qkv.shape[1] // 3
    n_tiles = s // tm
    r8 = tm // 8
    body = functools.partial(_inproj_body, n_tiles=n_tiles, q_scale=NA_HEAD_DIM ** -0.5,
                             c_hy=c_hy, c_na=c_na)
    tok = lambda c: pl.BlockSpec((1, tm, c), lambda bi, i: (bi, i, 0))
    full2 = lambda a: pl.BlockSpec(a.shape, lambda bi, i: (0, 0))
    per_b = pl.BlockSpec((1, 1, d), lambda bi, i: (bi, 0, 0))
    return pl.pallas_call(
        body,
        out_shape=[jax.ShapeDtypeStruct((b, s, c_hy), BF16)] * 3 + [jax.ShapeDtypeStruct((b, s, c_na), BF16)] * 3,
        grid=(b, n_tiles),
        in_specs=[tok(d),
                  pl.BlockSpec((1, 8, d), lambda bi, i: (bi, jnp.maximum(i * r8 - 1, 0), 0)),
                  pl.BlockSpec((1, 8, d), lambda bi, i: (bi, jnp.minimum((i + 1) * r8, s // 8 - 1), 0)),
                  full2(g), per_b, per_b, full2(w_hy), full2(w_qkv), full2(short_w), full2(short_b)],
        out_specs=[tok(c_hy)] * 3 + [tok(c_na)] * 3,
        compiler_params=_cparams(("parallel", "parallel")),
        name="inproj",
    )(x, x, x, g, shift, scale, w_hy, w_qkv, short_w, short_b)


def _ctxkv_body(x_ref, g_ref, sh_ref, sc_ref, w_ref, k_ref, v_ref, *, c_na):
    h = _norm_mod(x_ref[0], g_ref[...], sh_ref[0], sc_ref[0]).astype(BF16)
    z = _dot(h, w_ref[...])
    k_ref[0] = z[:, 0:c_na].astype(BF16)
    v_ref[0] = z[:, c_na:2 * c_na].astype(BF16)


def _ctxkv(ctx, g, shift, scale, w_kv):
    b, n, d = ctx.shape
    c_na = w_kv.shape[1] // 2
    one = pl.BlockSpec((1, 1, d), lambda bi: (0, 0, 0))
    return pl.pallas_call(
        functools.partial(_ctxkv_body, c_na=c_na),
        out_shape=[jax.ShapeDtypeStruct((b, n, c_na), BF16)] * 2,
        grid=(b,),
        in_specs=[pl.BlockSpec((1, n, d), lambda bi: (bi, 0, 0)),
                  pl.BlockSpec(g.shape, lambda bi: (0, 0)), one, one,
                  pl.BlockSpec(w_kv.shape, lambda bi: (0, 0))],
        out_specs=[pl.BlockSpec((1, n, c_na), lambda bi: (bi, 0, 0))] * 2,
        compiler_params=_cparams(("parallel",)),
        name="ctxkv",
    )(ctx, g, shift, scale, w_kv)


def _na_bias_table(rpb):
    w = GRID_W
    col = np.arange(w)[:, None]
    kc = np.arange(w)[None, :]
    c_start = np.clip(col - NA_WIN_C // 2, 0, w - NA_WIN_C)
    valid = (kc >= c_start) & (kc < c_start + NA_WIN_C)
    cidx = np.clip(kc - col + NA_WIN_C - 1, 0, 2 * NA_WIN_C - 2)
    off = np.arange(NA_WIN_R)[:, None]
    j = np.arange(NA_WIN_R)[None, :]
    ridx = j - off + NA_WIN_R - 1
    tab = rpb.astype(F32)[:, ridx[:, :, None, None], cidx[None, None, :, :]]
    tab = jnp.where(valid[None, None, None], tab, NEG_INF)
    h = rpb.shape[0]
    return tab.transpose(0, 1, 3, 2, 4).reshape(h, NA_WIN_R, w, NA_WIN_R * w)


def _natt_body(q_ref, k_ref, v_ref, kc_ref, vc_ref, bias_ref, o_ref, *, rows, heads_per_blk):
    w = GRID_W
    nloc = NA_WIN_R * w
    lane = lax.broadcasted_iota(jnp.int32, (1, heads_per_blk * NA_HEAD_DIM), 1)
    kcx = kc_ref[0]
    vcx = vc_ref[0]
    nt = (((1,), (1,)), ((), ()))

    def row_step(r, carry):
        r0 = jnp.clip(r - NA_WIN_R // 2, 0, rows - NA_WIN_R)
        off = r - r0
        qs = q_ref[0, pl.ds(pl.multiple_of(r * w, w), w), :]
        kw = k_ref[0, pl.ds(pl.multiple_of(r0 * w, w), nloc), :]
        vw = v_ref[0, pl.ds(pl.multiple_of(r0 * w, w), nloc), :]
        acc = jnp.zeros((w, heads_per_blk * NA_HEAD_DIM), F32)
        for hh in range(heads_per_blk):
            in_head = (lane >= NA_HEAD_DIM * hh) & (lane < NA_HEAD_DIM * (hh + 1))
            qh = jnp.where(in_head, qs, jnp.zeros_like(qs))
            s_loc = lax.dot_general(qh, kw, nt, preferred_element_type=F32) + bias_ref[hh, off]
            s_ctx = lax.dot_general(qh, kcx, nt, preferred_element_type=F32)
            m = jnp.maximum(jnp.max(s_loc, axis=-1, keepdims=True), jnp.max(s_ctx, axis=-1, keepdims=True))
            p_loc = jnp.exp(s_loc - m)
            p_ctx = jnp.exp(s_ctx - m)
            den = jnp.sum(p_loc, axis=-1, keepdims=True) + jnp.sum(p_ctx, axis=-1, keepdims=True)
            o = _dot(p_loc.astype(BF16), vw) + _dot(p_ctx.astype(BF16), vcx)
            acc = acc + jnp.where(in_head, o / den, 0.0)
        o_ref[0, pl.ds(pl.multiple_of(r * w, w), w), :] = acc.astype(BF16)
        return carry

    lax.fori_loop(0, rows, row_step, 0)


def _natt(q, k, v, kc, vc, bias):
    b, s, c = q.shape
    nctx = kc.shape[1]
    hpb = 4
    lw = hpb * NA_HEAD_DIM
    rows = s // GRID_W
    seq = pl.BlockSpec((1, s, lw), lambda bi, g: (bi, 0, g))
    cx = pl.BlockSpec((1, nctx, lw), lambda bi, g: (bi, 0, g))
    return pl.pallas_call(
        functools.partial(_natt_body, rows=rows, heads_per_blk=hpb),
        out_shape=jax.ShapeDtypeStruct((b, s, c), BF16),
        grid=(b, c // lw),
        in_specs=[seq, seq, seq, cx, cx,
                  pl.BlockSpec((hpb,) + bias.shape[1:], lambda bi, g: (g, 0, 0, 0))],
        out_specs=seq,
        compiler_params=_cparams(("parallel", "parallel")),
        name="natt",
    )(q, k, v, kc, vc, bias)


def _hyena_feats(seq_len):
    t = jnp.linspace(0.0, 1.0, seq_len, dtype=F32)[:, None]
    bands = jnp.linspace(1e-4, HYENA_BANDS - 1, HYENA_BANDS, dtype=F32)
    ang = (2.0 * math.pi / seq_len) * jnp.arange(seq_len, dtype=F32)[:, None] * bands[None, :]
    feats = jnp.concatenate([t, jnp.cos(ang), -jnp.sin(ang)], axis=-1)
    return jnp.pad(feats, ((0, 0), (0, 128 - HYENA_EMB)))


def _filt_body(feat_ref, w1_ref, b1_ref, w2_ref, b2_ref, w3_ref, fr_ref, dl_ref, o_ref, l1_ref):
    j = pl.program_id(0)
    hp = functools.partial(jnp.dot, precision=HIGHEST, preferred_element_type=F32)
    feats = feat_ref[...]
    fr = fr_ref[...]
    h = jnp.sin(fr[0:1] * (hp(feats, w1_ref[...]) + b1_ref[...]))
    h = jnp.sin(fr[1:2] * (hp(h, w2_ref[...]) + b2_ref[...]))
    hc = hp(h, w3_ref[...])
    t = feats[:, 0:1]
    hc = hc * jnp.exp(-t * dl_ref[...])
    row = lax.broadcasted_iota(jnp.int32, hc.shape, 0)
    hc = jnp.where((row == 0) & (j % 2 == 1), 0.0, hc)
    l1_ref[0] = jnp.sum(jnp.abs(hc), axis=0, keepdims=True)
    o_ref[0] = hc.astype(BF16)


def _hyena_filter_taps(seq_len, f_w1, f_b1, f_w2, f_b2, f_w3, f_freq, c_hy):
    feats = _hyena_feats(seq_len)
    hid = f_w1.shape[1]
    w1 = jnp.pad(f_w1.astype(F32), ((0, 128 - HYENA_EMB), (0, 0)))
    deltas = jnp.abs(jnp.linspace(math.log(HYENA_TARGET) / HYENA_SLOW_DECAY,
                                  math.log(HYENA_TARGET) / HYENA_FAST_DECAY, c_hy, dtype=F32))[None, :]
    nblk = f_w3.shape[1] // c_hy
    c0 = lambda a: pl.BlockSpec(a.shape, lambda j: (0, 0))
    b1, b2 = f_b1.reshape(1, hid), f_b2.reshape(1, hid)
    return pl.pallas_call(
        _filt_body,
        out_shape=[jax.ShapeDtypeStruct((nblk, seq_len, c_hy), BF16),
                   jax.ShapeDtypeStruct((nblk, 1, c_hy), F32)],
        grid=(nblk,),
        in_specs=[c0(feats), c0(w1), c0(b1), c0(f_w2), c0(b2),
                  pl.BlockSpec((hid, c_hy), lambda j: (0, j)), c0(f_freq), c0(deltas)],
        out_specs=[pl.BlockSpec((1, seq_len, c_hy), lambda j: (j, 0, 0)),
                   pl.BlockSpec((1, 1, c_hy), lambda j: (j, 0, 0))],
        compiler_params=_cparams(("parallel",)),
        name="hyena_filter",
    )(feats, w1, b1, f_w2, b2, f_w3, f_freq, deltas)


def _conv_dft_constants():
    a_half = FFT_A // 2
    n = FFT_A * FFT_R
    ka = np.arange(FFT_KA)[:, None]
    a = np.arange(a_half)[None, :]
    ph = 2.0 * np.pi * ka * a / FFT_A
    m_fwd = np.zeros((2 * FFT_KA_PAD, a_half))
    m_fwd[:FFT_KA] = np.cos(ph)
    m_fwd[FFT_KA_PAD:FFT_KA_PAD + FFT_KA] = -np.sin(ph)
    wgt = np.where((ka == 0) | (ka == FFT_A // 2), 1.0, 2.0)
    m_inv = np.zeros((a_half, 2 * FFT_KA_PAD))
    m_inv[:, :FFT_KA] = (wgt * np.cos(ph)).T / n
    m_inv[:, FFT_KA_PAD:FFT_KA_PAD + FFT_KA] = (-wgt * np.sin(ph)).T / n
    kb = np.arange(FFT_R)[None, :, None]
    b = np.arange(FFT_R)[None, None, :]
    kaa = np.arange(FFT_KA)[:, None, None]
    th = 2.0 * np.pi * (b * kb / FFT_R + b * kaa / n)
    gr, gi = np.cos(th), -np.sin(th)
    g2 = np.zeros((FFT_KA_PAD, 2 * FFT_R, 2 * FFT_R))
    g2[:FFT_KA] = np.block([[gr, -gi], [gi, gr]])
    grt, git = gr.transpose(0, 2, 1), gi.transpose(0, 2, 1)
    g2h = np.zeros_like(g2)
    g2h[:FFT_KA] = np.block([[grt, git], [-git, grt]])
    as_bf = lambda m: jnp.asarray(m, dtype=BF16)
    return as_bf(m_fwd), as_bf(m_inv), as_bf(g2), as_bf(g2h)


def _fwd1_body(m_ref, u_ref, o_ref):
    res = _dot(m_ref[...], u_ref[0])
    o_ref[0, 0] = res[0:FFT_KA_PAD]
    o_ref[0, 1] = res[FFT_KA_PAD:2 * FFT_KA_PAD]


def _conv_fwd1(u, m_fwd, cb=8192):
    n, seq, c = u.shape
    a_half = FFT_A // 2
    cols = seq * c // a_half
    uv = u.reshape(n, a_half, cols)
    return pl.pallas_call(
        _fwd1_body,
        out_shape=jax.ShapeDtypeStruct((n, 2, FFT_KA_PAD, cols), F32),
        grid=(n, cols // cb),
        in_specs=[pl.BlockSpec(m_fwd.shape, lambda i, j: (0, 0)),
                  pl.BlockSpec((1, a_half, cb), lambda i, j: (i, 0, j))],
        out_specs=pl.BlockSpec((1, 2, FFT_KA_PAD, cb), lambda i, j: (i, 0, 0, j)),
        compiler_params=_cparams(("parallel", "parallel")),
        name="conv_fwd1",
    )(m_fwd, uv)


def _fwd2f_body(sf_ref, sb_ref, g_ref, l1_ref, kf_ref):
    o = pl.program_id(0)
    r2 = 2 * FFT_R
    c = sf_ref.shape[-1]
    g = g_ref[0]
    xf = _dot(g, sf_ref[0, :, 0].reshape(r2, c).astype(BF16))
    xb = _dot(g, sb_ref[0, :, 0].reshape(r2, c).astype(BF16))
    inv = 1.0 / (l1_ref[2 * o] + l1_ref[2 * o + 1] + EPS)
    kf_ref[0, 0, 0:FFT_R] = (xf[0:FFT_R] + xb[0:FFT_R]) * inv
    kf_ref[0, 0, FFT_R:r2] = (xf[FFT_R:r2] - xb[FFT_R:r2]) * inv


def _filter_spectrum(s_filt, l1, g2, c):
    n_ord = s_filt.shape[0] // 2
    sv = s_filt.reshape(2 * n_ord, 2, FFT_KA_PAD, FFT_R, c)
    r2 = 2 * FFT_R
    return pl.pallas_call(
        _fwd2f_body,
        out_shape=jax.ShapeDtypeStruct((n_ord, FFT_KA_PAD, r2, c), F32),
        grid=(n_ord, FFT_KA_PAD),
        in_specs=[pl.BlockSpec((1, 2, 1, FFT_R, c), lambda o, ka: (2 * o, 0, ka, 0, 0)),
                  pl.BlockSpec((1, 2, 1, FFT_R, c), lambda o, ka: (2 * o + 1, 0, ka, 0, 0)),
                  pl.BlockSpec((1, r2, r2), lambda o, ka: (ka, 0, 0)),
                  pl.BlockSpec(l1.shape, lambda o, ka: (0, 0, 0))],
        out_specs=pl.BlockSpec((1, 1, r2, c), lambda o, ka: (o, ka, 0, 0)),
        compiler_params=_cparams(("parallel", "parallel")),
        name="filter_spectrum",
    )(sv, sv, g2, l1)


def _mid_body(s_ref, g_ref, gh_ref, kf_ref, t_ref):
    r2 = 2 * FFT_R
    c = s_ref.shape[-1]
    x = _dot(g_ref[0], s_ref[0, :, 0].reshape(r2, c).astype(BF16))
    xr, xi = x[0:FFT_R], x[FFT_R:r2]
    kr, ki = kf_ref[0, 0, 0:FFT_R], kf_ref[0, 0, FFT_R:r2]
    y = jnp.concatenate([xr * kr - xi * ki, xr * ki + xi * kr], axis=0).astype(BF16)
    t = _dot(gh_ref[0], y)
    t_ref[0, 0, 0] = t[0:FFT_R]
    t_ref[0, 1, 0] = t[FFT_R:r2]


def _conv_mid(s, kf, order, g2, g2h, c):
    n = s.shape[0]
    sv = s.reshape(n, 2, FFT_KA_PAD, FFT_R, c)
    r2 = 2 * FFT_R
    blk = pl.BlockSpec((1, 2, 1, FFT_R, c), lambda i, ka: (i, 0, ka, 0, 0))
    gspec = pl.BlockSpec((1, r2, r2), lambda i, ka: (ka, 0, 0))
    out = pl.pallas_call(
        _mid_body,
        out_shape=jax.ShapeDtypeStruct(sv.shape, F32),
        grid=(n, FFT_KA_PAD),
        in_specs=[blk, gspec, gspec,
                  pl.BlockSpec((1, 1, r2, c), lambda i, ka: (order, ka, 0, 0))],
        out_specs=blk,
        compiler_params=_cparams(("parallel", "parallel")),
        name="conv_mid",
    )(sv, g2, g2h, kf)
    return out.reshape(s.shape)


def _inv1_body(m_ref, t_ref, u_ref, xg_ref, sk_ref, o_ref):
    cb = t_ref.shape[-1]
    t2 = t_ref[0].reshape(2 * FFT_KA_PAD, cb).astype(BF16)
    y = _dot(m_ref[...], t2)
    u = u_ref[0].astype(F32)
    o_ref[0] = (xg_ref[0].astype(F32) * (y + u * sk_ref[...])).astype(BF16)


def _conv_inv1(t, u, xg, skip, m_inv, cb=8192):
    n, seq, c = u.shape
    a_half = FFT_A // 2
    cols = seq * c // a_half
    sk = jnp.tile(skip.astype(F32).reshape(1, c), (1, cols // c))
    uspec = pl.BlockSpec((1, a_half, cb), lambda i, j: (i, 0, j))
    out = pl.pallas_call(
        _inv1_body,
        out_shape=jax.ShapeDtypeStruct((n, a_half, cols), BF16),
        grid=(n, cols // cb),
        in_specs=[pl.BlockSpec(m_inv.shape, lambda i, j: (0, 0)),
                  pl.BlockSpec((1, 2, FFT_KA_PAD, cb), lambda i, j: (i, 0, 0, j)),
                  uspec, uspec,
                  pl.BlockSpec((1, cb), lambda i, j: (0, j))],
        out_specs=uspec,
        compiler_params=_cparams(("parallel", "parallel")),
        name="conv_inv1",
    )(m_inv, t, u.reshape(n, a_half, cols), xg.reshape(n, a_half, cols), sk)
    return out.reshape(n, seq, c)


def _hyena(v, x1, x2, f_w1, f_b1, f_w2, f_b2, f_w3, f_freq, skip):
    _, seq, c = v.shape
    assert 2 * seq == FFT_A * FFT_R
    m_fwd, m_inv, g2, g2h = _conv_dft_constants()
    taps, l1 = _hyena_filter_taps(seq, f_w1, f_b1, f_w2, f_b2, f_w3, f_freq, c)
    kf = _filter_spectrum(_conv_fwd1(taps, m_fwd), l1, g2, c)
    y = v
    for order, xg in enumerate((x1, x2)):
        t = _conv_mid(_conv_fwd1(y, m_fwd), kf, order, g2, g2h, c)
        y = _conv_inv1(t, y, xg, skip[order], m_inv)
    return y


def _mixout_body(x_ref, a1_ref, a2_ref, w_ref, gt_ref, o_ref):
    c1 = a1_ref.shape[-1]
    y = _dot(a1_ref[0], w_ref[0:c1]) + _dot(a2_ref[0], w_ref[c1:])
    o_ref[0] = x_ref[0] + gt_ref[0] * y


def _mixout(x, a1, a2, w, gate, tm=512):
    b, s, d = x.shape
    tok = lambda c: pl.BlockSpec((1, tm, c), lambda bi, i: (bi, i, 0))
    return pl.pallas_call(
        _mixout_body,
        out_shape=jax.ShapeDtypeStruct(x.shape, F32),
        grid=(b, s // tm),
        in_specs=[tok(d), tok(a1.shape[-1]), tok(a2.shape[-1]),
                  pl.BlockSpec(w.shape, lambda bi, i: (0, 0)),
                  pl.BlockSpec((1, 1, d), lambda bi, i: (bi, 0, 0))],
        out_specs=tok(d),
        compiler_params=_cparams(("parallel", "parallel")),
        name="mixout",
    )(x, a1, a2, w, gate)


def _resid_matmul_body(x_ref, a_ref, w_ref, gt_ref, o_ref):
    o_ref[0] = x_ref[0] + gt_ref[0] * _dot(a_ref[0].astype(BF16), w_ref[...])


def _resid_matmul(x, a, w, gate, tm=512):
    b, s, d = x.shape
    tok = lambda c: pl.BlockSpec((1, tm, c), lambda bi, i: (bi, i, 0))
    return pl.pallas_call(
        _resid_matmul_body,
        out_shape=jax.ShapeDtypeStruct(x.shape, F32),
        grid=(b, s // tm),
        in_specs=[tok(d), tok(a.shape[-1]),
                  pl.BlockSpec(w.shape, lambda bi, i: (0, 0)),
                  pl.BlockSpec((1, 1, d), lambda bi, i: (bi, 0, 0))],
        out_specs=tok(d),
        compiler_params=_cparams(("parallel", "parallel")),
        name="resid_matmul",
    )(x, a, w, gate)


def _ffn_body(x_ref, g_ref, sh_ref, sc_ref, gt_ref, w1_ref, w3_ref, w2_ref, o_ref, h_scr, acc_scr):
    j = pl.program_id(2)

    @pl.when(j == 0)
    def _():
        h_scr[...] = _norm_mod(x_ref[0], g_ref[...], sh_ref[0], sc_ref[0]).astype(BF16)
        acc_scr[...] = jnp.zeros_like(acc_scr)

    h = h_scr[...]
    a = _dot(h, w1_ref[...])
    u = (a * jax.nn.sigmoid(a) * _dot(h, w3_ref[...])).astype(BF16)
    acc_scr[...] += _dot(u, w2_ref[...])

    @pl.when(j == pl.num_programs(2) - 1)
    def _():
        o_ref[0] = x_ref[0] + gt_ref[0] * acc_scr[...]


def _ffn(x, g, shift, scale, gate, w1, w3, w2, tm=512, fb=1408):
    b, s, d = x.shape
    f = w1.shape[1]
    tok = pl.BlockSpec((1, tm, d), lambda bi, i, j: (bi, i, 0))
    per_b = pl.BlockSpec((1, 1, d), lambda bi, i, j: (bi, 0, 0))
    return pl.pallas_call(
        _ffn_body,
        out_shape=jax.ShapeDtypeStruct(x.shape, F32),
        grid=(b, s // tm, f // fb),
        in_specs=[tok, pl.BlockSpec(g.shape, lambda bi, i, j: (0, 0)), per_b, per_b, per_b,
                  pl.BlockSpec((d, fb), lambda bi, i, j: (0, j)),
                  pl.BlockSpec((d, fb), lambda bi, i, j: (0, j)),
                  pl.BlockSpec((fb, d), lambda bi, i, j: (j, 0))],
        out_specs=tok,
        scratch_shapes=[pltpu.VMEM((tm, d), BF16), pltpu.VMEM((tm, d), F32)],
        compiler_params=_cparams(("parallel", "parallel", "arbitrary")),
        name="ffn",
    )(x, g, shift, scale, gate, w1, w3, w2)


def _fm_constants(cg):
    j = np.arange(cg)[:, None]
    m = np.arange(cg)[None, :]
    ph = 2.0 * np.pi * j * m / cg
    w_cs = np.concatenate([np.cos(ph), np.sin(ph)], axis=1)
    d = np.arange(FM_A)[:, None]
    a = np.arange(FM_A)[None, :]
    ph = 2.0 * np.pi * d * a / FM_A
    fr, fi = np.cos(ph), -np.sin(ph)
    m1 = np.block([[fr, fi], [fi, -fr]])
    n = FM_A * FM_A
    dd = np.arange(FM_A)[:, None, None]
    c = np.arange(FM_A)[None, :, None]
    b = np.arange(FM_A)[None, None, :]
    th = 2.0 * np.pi * (b * c / FM_A + b * dd / n)
    gcat = np.concatenate([np.cos(th), np.sin(th)], axis=2)
    as_bf = lambda x: jnp.asarray(x, dtype=BF16)
    return as_bf(w_cs), as_bf(m1), as_bf(gcat)


def _fm_chan_body(x_ref, g_ref, sh_ref, sc_ref, w_ref, o_ref, *, cg):
    h = _norm_mod(x_ref[0], g_ref[...], sh_ref[0], sc_ref[0]).astype(BF16)
    for grp in range(h.shape[-1] // cg):
        pq = _dot(h[:, grp * cg:(grp + 1) * cg], w_ref[...])
        o_ref[0, 0, :, grp * cg:(grp + 1) * cg] = pq[:, 0:cg].astype(BF16)
        o_ref[0, 1, :, grp * cg:(grp + 1) * cg] = pq[:, cg:2 * cg].astype(BF16)


def _fm_chan(x, g, shift, scale, w_cs, tm=512):
    b, s, d = x.shape
    cg = w_cs.shape[0]
    per_b = pl.BlockSpec((1, 1, d), lambda bi, i: (bi, 0, 0))
    return pl.pallas_call(
        functools.partial(_fm_chan_body, cg=cg),
        out_shape=jax.ShapeDtypeStruct((b, 2, s, d), BF16),
        grid=(b, s // tm),
        in_specs=[pl.BlockSpec((1, tm, d), lambda bi, i: (bi, i, 0)),
                  pl.BlockSpec(g.shape, lambda bi, i: (0, 0)), per_b, per_b,
                  pl.BlockSpec(w_cs.shape, lambda bi, i: (0, 0))],
        out_specs=pl.BlockSpec((1, 2, tm, d), lambda bi, i: (bi, 0, i, 0)),
        compiler_params=_cparams(("parallel", "parallel")),
        name="fm_chan",
    )(x, g, shift, scale, w_cs)


def _fm_s1_body(m_ref, pq_ref, o_ref):
    o_ref[0] = _dot(m_ref[...], pq_ref[0]).astype(BF16)


def _fm_stage1(pq, m1, cb=8192):
    b, _, s, d = pq.shape
    cols = s * d // FM_A
    blk = pl.BlockSpec((1, 2 * FM_A, cb), lambda bi, j: (bi, 0, j))
    return pl.pallas_call(
        _fm_s1_body,
        out_shape=jax.ShapeDtypeStruct((b, 2 * FM_A, cols), BF16),
        grid=(b, cols // cb),
        in_specs=[pl.BlockSpec(m1.shape, lambda bi, j: (0, 0)), blk],
        out_specs=blk,
        compiler_params=_cparams(("parallel", "parallel")),
        name="fm_stage1",
    )(m1, pq.reshape(b, 2 * FM_A, cols))


def _fm_s2_body(s_ref, g_ref, o_ref, *, dblk, scale):
    for i in range(dblk):
        s2 = jnp.concatenate([s_ref[0, 0, i], s_ref[0, 1, i]], axis=0)
        o_ref[:, i, :] = _dot(g_ref[i], s2) * scale


def _fm_stage2(s1, gcat, seq, d, dblk=8):
    b = s1.shape[0]
    sv = s1.reshape(b, 2, FM_A, FM_A, d)
    scale = 1.0 / math.sqrt(seq * (d // F_GROUPS))
    out = pl.pallas_call(
        functools.partial(_fm_s2_body, dblk=dblk, scale=scale),
        out_shape=jax.ShapeDtypeStruct((b * FM_A, dblk * (FM_A // dblk), d), F32),
        grid=(b, FM_A // dblk),
        in_specs=[pl.BlockSpec((1, 2, dblk, FM_A, d), lambda bi, j: (bi, 0, j, 0, 0)),
                  pl.BlockSpec((dblk, FM_A, 2 * FM_A), lambda bi, j: (j, 0, 0))],
        out_specs=pl.BlockSpec((FM_A, dblk, d), lambda bi, j: (bi, j, 0)),
        compiler_params=_cparams(("parallel", "parallel")),
        name="fm_stage2",
    )(sv, gcat)
    return out.reshape(b, seq, d)


def _fourier_mix(x, g, shift, scale):
    b, s, d = x.shape
    assert s == FM_A * FM_A
    w_cs, m1, gcat = _fm_constants(d // F_GROUPS)
    pq = _fm_chan(x, g, shift, scale, w_cs)
    return _fm_stage2(_fm_stage1(pq, m1), gcat, s, d)


def _router_body(x_ref, g_ref, sh_ref, sc_ref, wr_ref, br_ref, h_ref, gate_ref):
    h = _norm_mod(x_ref[0], g_ref[...], sh_ref[0], sc_ref[0])
    h_ref[0] = h.astype(BF16)
    logits = jnp.dot(h, wr_ref[...], precision=HIGHEST, preferred_element_type=F32) + br_ref[...]
    lane = lax.broadcasted_iota(jnp.int32, logits.shape, 1)
    nl = logits.shape[-1]
    m1 = jnp.max(logits, axis=-1, keepdims=True)
    i1 = jnp.min(jnp.where(logits == m1, lane, nl), axis=-1, keepdims=True)
    rest = jnp.where(lane == i1, -3.0e38, logits)
    m2 = jnp.max(rest, axis=-1, keepdims=True)
    i2 = jnp.min(jnp.where(rest == m2, lane, nl), axis=-1, keepdims=True)
    e = jnp.exp(m2 - m1)
    gate_ref[0] = jnp.where(lane == i1, 1.0 / (1.0 + e), 0.0) + jnp.where(lane == i2, e / (1.0 + e), 0.0)


def _router(x, g, shift, scale, w_router, b_router, tm=512):
    b, s, d = x.shape
    ne = w_router.shape[1]
    wr = jnp.pad(w_router.astype(F32), ((0, 0), (0, 128 - ne)))
    br = jnp.pad(b_router.astype(F32).reshape(1, ne), ((0, 0), (0, 128 - ne)), constant_values=NEG_INF)
    per_b = pl.BlockSpec((1, 1, d), lambda bi, i: (bi, 0, 0))
    tok = lambda c: pl.BlockSpec((1, tm, c), lambda bi, i: (bi, i, 0))
    return pl.pallas_call(
        _router_body,
        out_shape=[jax.ShapeDtypeStruct((b, s, d), BF16), jax.ShapeDtypeStruct((b, s, 128), F32)],
        grid=(b, s // tm),
        in_specs=[tok(d), pl.BlockSpec(g.shape, lambda bi, i: (0, 0)), per_b, per_b,
                  pl.BlockSpec(wr.shape, lambda bi, i: (0, 0)), pl.BlockSpec(br.shape, lambda bi, i: (0, 0))],
        out_specs=[tok(d), tok(128)],
        compiler_params=_cparams(("parallel", "parallel")),
        name="router",
    )(x, g, shift, scale, wr, br)


def _moe_body(x_ref, h_ref, gate_ref, gt_ref, fg_ref, w1_ref, w3_ref, w2_ref, o_ref, acc_scr):
    e = pl.program_id(2)
    j = pl.program_id(3)

    @pl.when((e == 0) & (j == 0))
    def _():
        acc_scr[...] = jnp.zeros_like(acc_scr)

    h = h_ref[0]
    gate = gate_ref[0]
    lane = lax.broadcasted_iota(jnp.int32, gate.shape, 1)
    ge = jnp.sum(jnp.where(lane == e, gate, 0.0), axis=-1, keepdims=True)
    a = _dot(h, w1_ref[0])
    u = (a * jax.nn.sigmoid(a) * _dot(h, w3_ref[0]) * ge).astype(BF16)
    acc_scr[...] += _dot(u, w2_ref[0])

    @pl.when((e == pl.num_programs(2) - 1) & (j == pl.num_programs(3) - 1))
    def _():
        xo = x_ref[0] + gt_ref[0] * acc_scr[...]
        ms = jnp.mean(xo * xo, axis=-1, keepdims=True)
        o_ref[0] = xo * lax.rsqrt(ms + EPS) * fg_ref[...]


def _moe(x, h, gate, gt, final_g, w1, w3, w2, tm=1024, fb=512):
    b, s, d = x.shape
    ne, _, f = w1.shape
    tok = lambda c: pl.BlockSpec((1, tm, c), lambda bi, i, e, j: (bi, i, 0))
    return pl.pallas_call(
        _moe_body,
        out_shape=jax.ShapeDtypeStruct(x.shape, F32),
        grid=(b, s // tm, ne, f // fb),
        in_specs=[tok(d), tok(d), tok(128),
                  pl.BlockSpec((1, 1, d), lambda bi, i, e, j: (bi, 0, 0)),
                  pl.BlockSpec(final_g.shape, lambda bi, i, e, j: (0, 0)),
                  pl.BlockSpec((1, d, fb), lambda bi, i, e, j: (e, 0, j)),
                  pl.BlockSpec((1, d, fb), lambda bi, i, e, j: (e, 0, j)),
                  pl.BlockSpec((1, fb, d), lambda bi, i, e, j: (e, j, 0))],
        out_specs=tok(d),
        scratch_shapes=[pltpu.VMEM((tm, d), F32)],
        compiler_params=_cparams(("parallel", "parallel", "arbitrary", "arbitrary")),
        name="moe",
    )(x, h, gate, gt, final_g, w1, w3, w2)


def kernel(x, c, ctx, c_ctx, w_ada, b_ada, norm_g, w_in, hy_short_w, hy_short_b, hy_f_w1, hy_f_b1, hy_f_w2, hy_f_b2, hy_f_w3, hy_f_freq, hy_skip, na_rpb, w_mix_out, ffn_w1, ffn_w3, ffn_w2, w_fourier, w_router, b_router, moe_w1, moe_w3, moe_w2, final_g):
    b, s, d = x.shape
    depth = w_ada.shape[0]
    assert depth == 2, "layer 0 mixes with Hyena/attention, layer 1 with Fourier/MoE"
    c_hy = hy_skip.shape[-1]
    c_na = d - c_hy

    cvec = jnp.concatenate([c, c_ctx[None, :], jnp.zeros((8 - b - 1, d), F32)], axis=0)
    mods = _ada(cvec, w_ada, b_ada)

    def mod(layer, idx, ctx_row=False):
        m = mods[layer, :, idx * d:(idx + 1) * d]
        return m[b:b + 1, None, :] if ctx_row else m[0:b, None, :]

    row = lambda a: a.reshape(1, -1)

    w_in0 = w_in[0].astype(BF16)
    w_hy, w_qkv = w_in0[:, 0:3 * c_hy], w_in0[:, 3 * c_hy:]
    v, x1, x2, q, k, va = _inproj(x, row(norm_g[0, 0]), mod(0, 0), mod(0, 1), w_hy, w_qkv,
                                  hy_short_w[0], row(hy_short_b[0]))
    kc, vc = _ctxkv(ctx, row(norm_g[0, 0]), mod(0, 0, True), mod(0, 1, True), w_qkv[:, c_na:])
    y_na = _natt(q, k, va, kc, vc, _na_bias_table(na_rpb[0]))
    y_hy = _hyena(v, x1, x2, hy_f_w1[0], hy_f_b1[0], hy_f_w2[0], hy_f_b2[0], hy_f_w3[0],
                  hy_f_freq[0], hy_skip[0])
    x = _mixout(x, y_hy, y_na, w_mix_out[0].astype(BF16), mod(0, 2))
    x = _ffn(x, row(norm_g[0, 1]), mod(0, 3), mod(0, 4), mod(0, 5),
             ffn_w1[0].astype(BF16), ffn_w3[0].astype(BF16), ffn_w2[0].astype(BF16))

    y_f = _fourier_mix(x, row(norm_g[1, 0]), mod(1, 0), mod(1, 1))
    x = _resid_matmul(x, y_f, w_fourier[0].astype(BF16), mod(1, 2))
    h, gate = _router(x, row(norm_g[1, 1]), mod(1, 3), mod(1, 4), w_router[0], b_router[0])
    return _moe(x, h, gate, mod(1, 5), row(final_g),
                moe_w1[0].astype(BF16), moe_w3[0].astype(BF16), moe_w2[0].astype(BF16))
```

```python
import functools
import math

import numpy as np
import jax
import jax.numpy as jnp
from jax import lax
from jax.experimental import pallas as pl
from jax.experimental.pallas import tpu as pltpu

F32 = jnp.float32
BF16 = jnp.bfloat16
HIGHEST = lax.Precision.HIGHEST

GRID_W = 64
NA_HEAD_DIM = 32
NA_WIN_R = 8
NA_WIN_C = 16
HYENA_EMB = 33
HYENA_BANDS = (HYENA_EMB - 1) // 2
HYENA_FAST_DECAY = 0.3
HYENA_SLOW_DECAY = 1.5
HYENA_TARGET = 1e-2
F_GROUPS = 4
N_MOD = 6
EPS = 1e-6
NEG_INF = -1e30

FFT_A = 64
FFT_R = 128
FFT_KA = FFT_A // 2 + 1
FFT_KA_PAD = 40
FM_A = 64

VMEM_LIMIT = 48 * 1024 * 1024


def _cparams(sem):
    return pltpu.CompilerParams(dimension_semantics=sem, vmem_limit_bytes=VMEM_LIMIT)


def _dot(a, b):
    return jnp.dot(a, b, preferred_element_type=F32)


def _mxu_const(m):
    return jnp.asarray(m, dtype=F32).astype(BF16)


def _norm_mod(x, g, shift, scale):
    ms = jnp.mean(x * x, axis=-1, keepdims=True)
    y = x * lax.rsqrt(ms + EPS) * g
    return y * (1.0 + scale) + shift


def _ada_body(c_ref, w_ref, b_ref, o_ref):
    cv = c_ref[...]
    s = cv * jax.nn.sigmoid(cv)
    o_ref[0] = jnp.dot(s, w_ref[0], precision=HIGHEST, preferred_element_type=F32) + b_ref[0]


def _ada(cvec, w_ada, b_ada):
    depth, d, n = w_ada.shape
    rows = cvec.shape[0]
    bn = n // 4
    return pl.pallas_call(
        _ada_body,
        out_shape=jax.ShapeDtypeStruct((depth, rows, n), F32),
        grid=(depth, n // bn),
        in_specs=[pl.BlockSpec((rows, d), lambda l, j: (0, 0)),
                  pl.BlockSpec((1, d, bn), lambda l, j: (l, 0, j)),
                  pl.BlockSpec((1, 1, bn), lambda l, j: (l, 0, j))],
        out_specs=pl.BlockSpec((1, rows, bn), lambda l, j: (l, 0, j)),
        compiler_params=_cparams(("parallel", "parallel")),
        name="ada",
    )(cvec, w_ada, b_ada.reshape(depth, 1, n))


def _inproj_body(x_ref, xp_ref, xn_ref, g_ref, sh_ref, sc_ref, why_ref, wqkv_ref, sw_ref, sb_ref,
                 v_ref, x1_ref, x2_ref, q_ref, k_ref, va_ref, *, n_tiles, q_scale, c_hy, c_na):
    i = pl.program_id(1)
    g, sh, sc = g_ref[...], sh_ref[0], sc_ref[0]
    h = _norm_mod(x_ref[0], g, sh, sc).astype(BF16)
    zh = _dot(h, why_ref[...])
    hp = _norm_mod(xp_ref[0], g, sh, sc).astype(BF16)
    hn = _norm_mod(xn_ref[0], g, sh, sc).astype(BF16)
    zp = _dot(hp, why_ref[...])[7:8]
    zn = _dot(hn, why_ref[...])[0:1]
    zp = jnp.where(i > 0, zp, 0.0)
    zn = jnp.where(i < n_tiles - 1, zn, 0.0)
    tm = zh.shape[0]
    row = lax.broadcasted_iota(jnp.int32, zh.shape, 0)
    z_m1 = jnp.where(row == 0, zp, pltpu.roll(zh, 1, 0))
    z_p1 = jnp.where(row == tm - 1, zn, pltpu.roll(zh, tm - 1, 0))
    sw = sw_ref[...]
    zc = z_m1 * sw[0:1] + zh * sw[1:2] + z_p1 * sw[2:3] + sb_ref[...]
    v_ref[0] = zc[:, 0:c_hy].astype(BF16)
    x1_ref[0] = zc[:, c_hy:2 * c_hy].astype(BF16)
    x2_ref[0] = zc[:, 2 * c_hy:3 * c_hy].astype(BF16)
    zq = _dot(h, wqkv_ref[...])
    q_ref[0] = (zq[:, 0:c_na] * q_scale).astype(BF16)
    k_ref[0] = zq[:, c_na:2 * c_na].astype(BF16)
    va_ref[0] = zq[:, 2 * c_na:3 * c_na].astype(BF16)


def _inproj(x, g, shift, scale, w_hy, w_qkv, short_w, short_b, tm=512):
    b, s, d = x.shape
    c_hy = w_hy.shape[1] // 3
    c_na = w_qkv.shape[1] // 3
    n_tiles = s // tm
    r8 = tm // 8
    body = functools.partial(_inproj_body, n_tiles=n_tiles, q_scale=NA_HEAD_DIM ** -0.5,
                             c_hy=c_hy, c_na=c_na)
    tok = lambda c: pl.BlockSpec((1, tm, c), lambda bi, i: (bi, i, 0))
    full2 = lambda a: pl.BlockSpec(a.shape, lambda bi, i: (0, 0))
    per_b = pl.BlockSpec((1, 1, d), lambda bi, i: (bi, 0, 0))
    return pl.pallas_call(
        body,
        out_shape=[jax.ShapeDtypeStruct((b, s, c_hy), BF16)] * 3 + [jax.ShapeDtypeStruct((b, s, c_na), BF16)] * 3,
        grid=(b, n_tiles),
        in_specs=[tok(d),
                  pl.BlockSpec((1, 8, d), lambda bi, i: (bi, jnp.maximum(i * r8 - 1, 0), 0)),
                  pl.BlockSpec((1, 8, d), lambda bi, i: (bi, jnp.minimum((i + 1) * r8, s // 8 - 1), 0)),
                  full2(g), per_b, per_b, full2(w_hy), full2(w_qkv), full2(short_w), full2(short_b)],
        out_specs=[tok(c_hy)] * 3 + [tok(c_na)] * 3,
        compiler_params=_cparams(("parallel", "parallel")),
        name="inproj",
    )(x, x, x, g, shift, scale, w_hy, w_qkv, short_w, short_b)


def _ctxkv_body(x_ref, g_ref, sh_ref, sc_ref, w_ref, k_ref, v_ref, *, c_na):
    h = _norm_mod(x_ref[0], g_ref[...], sh_ref[0], sc_ref[0]).astype(BF16)
    z = _dot(h, w_ref[...])
    k_ref[0] = z[:, 0:c_na].astype(BF16)
    v_ref[0] = z[:, c_na:2 * c_na].astype(BF16)


def _ctxkv(ctx, g, shift, scale, w_kv):
    b, n, d = ctx.shape
    c_na = w_kv.shape[1] // 2
    one = pl.BlockSpec((1, 1, d), lambda bi: (0, 0, 0))
    return pl.pallas_call(
        functools.partial(_ctxkv_body, c_na=c_na),
        out_shape=[jax.ShapeDtypeStruct((b, n, c_na), BF16)] * 2,
        grid=(b,),
        in_specs=[pl.BlockSpec((1, n, d), lambda bi: (bi, 0, 0)),
                  pl.BlockSpec(g.shape, lambda bi: (0, 0)), one, one,
                  pl.BlockSpec(w_kv.shape, lambda bi: (0, 0))],
        out_specs=[pl.BlockSpec((1, n, c_na), lambda bi: (bi, 0, 0))] * 2,
        compiler_params=_cparams(("parallel",)),
        name="ctxkv",
    )(ctx, g, shift, scale, w_kv)


NA_HEADS_PER_BLK = 4


def _na_bias_body(r_ref, e_ref, ok_ref, o_ref):
    t = jnp.dot(r_ref[...], e_ref[...], precision=HIGHEST, preferred_element_type=F32)
    o_ref[...] = jnp.where(ok_ref[...] > 0.5, t, NEG_INF)


def _na_bias_table(rpb):
    w = GRID_W
    h, nr, nc = rpb.shape
    col = np.arange(w)[:, None]
    kc = np.arange(w)[None, :]
    c_start = np.clip(col - NA_WIN_C // 2, 0, w - NA_WIN_C)
    valid = ((kc >= c_start) & (kc < c_start + NA_WIN_C)).reshape(1, w * w)
    expand = (np.arange(32)[:, None, None] == (kc - col + NA_WIN_C - 1)[None]).reshape(32, w * w)
    rp = jnp.pad(rpb.astype(F32).reshape(h * nr, nc), ((0, 0), (0, 32 - nc)))
    full = lambda a: pl.BlockSpec(a.shape, lambda: (0,) * a.ndim)
    expand = jnp.asarray(expand, dtype=F32)
    ok = jnp.asarray(valid, dtype=F32)
    toep = pl.pallas_call(
        _na_bias_body,
        out_shape=jax.ShapeDtypeStruct((h * nr, w * w), F32),
        in_specs=[full(rp), full(expand), full(ok)],
        out_specs=pl.BlockSpec((h * nr, w * w), lambda: (0, 0)),
        name="na_bias",
    )(rp, expand, ok)
    t2 = toep.reshape(h, nr, w, w).transpose(0, 2, 1, 3).reshape(h, w, nr * w)
    slabs = jnp.stack([t2[:, :, (NA_WIN_R - 1 - off) * w:(2 * NA_WIN_R - 1 - off) * w]
                       for off in range(NA_WIN_R)], axis=1)
    hpb = NA_HEADS_PER_BLK
    slabs = slabs.reshape(h // hpb, hpb, NA_WIN_R, w, NA_WIN_R * w).transpose(0, 2, 1, 3, 4)
    return slabs.reshape(h // hpb, NA_WIN_R, hpb * w, NA_WIN_R * w)


def _natt_body(q_ref, k_ref, v_ref, kc_ref, vc_ref, bias_ref, o_ref, *, rows):
    w = GRID_W
    hpb = NA_HEADS_PER_BLK
    nloc = NA_WIN_R * w
    lane = lax.broadcasted_iota(jnp.int32, (1, hpb * NA_HEAD_DIM), 1)
    in_head = [(lane >= NA_HEAD_DIM * hh) & (lane < NA_HEAD_DIM * (hh + 1)) for hh in range(hpb)]
    kcx = kc_ref[0]
    vcx = vc_ref[0]
    nt = (((1,), (1,)), ((), ()))

    def one_row(r):
        r0 = jnp.clip(r - NA_WIN_R // 2, 0, rows - NA_WIN_R)
        off = r - r0
        qs = q_ref[0, pl.ds(pl.multiple_of(r * w, w), w), :]
        kw = k_ref[0, pl.ds(pl.multiple_of(r0 * w, w), nloc), :]
        vw = v_ref[0, pl.ds(pl.multiple_of(r0 * w, w), nloc), :]
        zero = jnp.zeros_like(qs)
        qst = jnp.concatenate([jnp.where(m, qs, zero) for m in in_head], axis=0)
        s_loc = lax.dot_general(qst, kw, nt, preferred_element_type=F32) + bias_ref[0, off]
        s_ctx = lax.dot_general(qst, kcx, nt, preferred_element_type=F32)
        m = jnp.maximum(jnp.max(s_loc, axis=-1, keepdims=True), jnp.max(s_ctx, axis=-1, keepdims=True))
        p_loc = jnp.exp(s_loc - m)
        p_ctx = jnp.exp(s_ctx - m)
        den = jnp.sum(p_loc, axis=-1, keepdims=True) + jnp.sum(p_ctx, axis=-1, keepdims=True)
        o = (_dot(p_loc.astype(BF16), vw) + _dot(p_ctx.astype(BF16), vcx)) * (1.0 / den)
        acc = jnp.where(in_head[0], o[0:w], 0.0)
        for hh in range(1, hpb):
            acc = acc + jnp.where(in_head[hh], o[hh * w:(hh + 1) * w], 0.0)
        o_ref[0, pl.ds(pl.multiple_of(r * w, w), w), :] = acc.astype(BF16)

    def row_pair(i, carry):
        one_row(2 * i)
        one_row(2 * i + 1)
        return carry

    lax.fori_loop(0, rows // 2, row_pair, 0)


def _natt(q, k, v, kc, vc, bias):
    b, s, c = q.shape
    nctx = kc.shape[1]
    lw = NA_HEADS_PER_BLK * NA_HEAD_DIM
    rows = s // GRID_W
    seq = pl.BlockSpec((1, s, lw), lambda bi, g: (bi, 0, g))
    cx = pl.BlockSpec((1, nctx, lw), lambda bi, g: (bi, 0, g))
    return pl.pallas_call(
        functools.partial(_natt_body, rows=rows),
        out_shape=jax.ShapeDtypeStruct((b, s, c), BF16),
        grid=(b, c // lw),
        in_specs=[seq, seq, seq, cx, cx,
                  pl.BlockSpec((1,) + bias.shape[1:], lambda bi, g: (g, 0, 0, 0))],
        out_specs=seq,
        compiler_params=_cparams(("parallel", "parallel")),
        name="natt",
    )(q, k, v, kc, vc, bias)


def _hyena_feats(seq_len):
    t = jnp.linspace(0.0, 1.0, seq_len, dtype=F32)[:, None]
    bands = jnp.linspace(1e-4, HYENA_BANDS - 1, HYENA_BANDS, dtype=F32)
    ang = (2.0 * math.pi / seq_len) * jnp.arange(seq_len, dtype=F32)[:, None] * bands[None, :]
    feats = jnp.concatenate([t, jnp.cos(ang), -jnp.sin(ang)], axis=-1)
    return jnp.pad(feats, ((0, 0), (0, 128 - HYENA_EMB)))


def _filt_body(feat_ref, w1_ref, b1_ref, w2_ref, b2_ref, w3_ref, fr_ref, dl_ref, o_ref, l1_ref):
    j = pl.program_id(0)
    hp = functools.partial(jnp.dot, precision=HIGHEST, preferred_element_type=F32)
    feats = feat_ref[...]
    fr = fr_ref[...]
    h = jnp.sin(fr[0:1] * (hp(feats, w1_ref[...]) + b1_ref[...]))
    h = jnp.sin(fr[1:2] * (hp(h, w2_ref[...]) + b2_ref[...]))
    hc = hp(h, w3_ref[...])
    t = feats[:, 0:1]
    hc = hc * jnp.exp(-t * dl_ref[...])
    row = lax.broadcasted_iota(jnp.int32, hc.shape, 0)
    hc = jnp.where((row == 0) & (j % 2 == 1), 0.0, hc)
    l1_ref[0] = jnp.sum(jnp.abs(hc), axis=0, keepdims=True)
    o_ref[0] = hc.astype(BF16)


def _hyena_filter_taps(seq_len, f_w1, f_b1, f_w2, f_b2, f_w3, f_freq, c_hy):
    feats = _hyena_feats(seq_len)
    hid = f_w1.shape[1]
    w1 = jnp.pad(f_w1.astype(F32), ((0, 128 - HYENA_EMB), (0, 0)))
    deltas = jnp.abs(jnp.linspace(math.log(HYENA_TARGET) / HYENA_SLOW_DECAY,
                                  math.log(HYENA_TARGET) / HYENA_FAST_DECAY, c_hy, dtype=F32))[None, :]
    nblk = f_w3.shape[1] // c_hy
    c0 = lambda a: pl.BlockSpec(a.shape, lambda j: (0, 0))
    b1, b2 = f_b1.reshape(1, hid), f_b2.reshape(1, hid)
    return pl.pallas_call(
        _filt_body,
        out_shape=[jax.ShapeDtypeStruct((nblk, seq_len, c_hy), BF16),
                   jax.ShapeDtypeStruct((nblk, 1, c_hy), F32)],
        grid=(nblk,),
        in_specs=[c0(feats), c0(w1), c0(b1), c0(f_w2), c0(b2),
                  pl.BlockSpec((hid, c_hy), lambda j: (0, j)), c0(f_freq), c0(deltas)],
        out_specs=[pl.BlockSpec((1, seq_len, c_hy), lambda j: (j, 0, 0)),
                   pl.BlockSpec((1, 1, c_hy), lambda j: (j, 0, 0))],
        compiler_params=_cparams(("parallel",)),
        name="hyena_filter",
    )(feats, w1, b1, f_w2, b2, f_w3, f_freq, deltas)


def _conv_dft_constants():
    a_half = FFT_A // 2
    n = FFT_A * FFT_R
    ka = np.arange(FFT_KA)[:, None]
    a = np.arange(a_half)[None, :]
    ph = 2.0 * np.pi * ka * a / FFT_A
    m_fwd = np.zeros((2 * FFT_KA_PAD, a_half))
    m_fwd[:FFT_KA] = np.cos(ph)
    m_fwd[FFT_KA_PAD:FFT_KA_PAD + FFT_KA] = -np.sin(ph)
    wgt = np.where((ka == 0) | (ka == FFT_A // 2), 1.0, 2.0)
    m_inv = np.zeros((a_half, 2 * FFT_KA_PAD))
    m_inv[:, :FFT_KA] = (wgt * np.cos(ph)).T / n
    m_inv[:, FFT_KA_PAD:FFT_KA_PAD + FFT_KA] = (-wgt * np.sin(ph)).T / n
    kb = np.arange(FFT_R)[None, :, None]
    b = np.arange(FFT_R)[None, None, :]
    kaa = np.arange(FFT_KA)[:, None, None]
    th = 2.0 * np.pi * (b * kb / FFT_R + b * kaa / n)
    gr, gi = np.cos(th), -np.sin(th)
    g2 = np.zeros((FFT_KA_PAD, 2 * FFT_R, 2 * FFT_R))
    g2[:FFT_KA] = np.block([[gr, -gi], [gi, gr]])
    grt, git = gr.transpose(0, 2, 1), gi.transpose(0, 2, 1)
    g2h = np.zeros_like(g2)
    g2h[:FFT_KA] = np.block([[grt, git], [-git, grt]])
    return _mxu_const(m_fwd), _mxu_const(m_inv), _mxu_const(g2), _mxu_const(g2h)


def _fwd1_body(m_ref, u_ref, o_ref):
    res = _dot(m_ref[...], u_ref[0])
    o_ref[0, 0] = res[0:FFT_KA_PAD]
    o_ref[0, 1] = res[FFT_KA_PAD:2 * FFT_KA_PAD]


def _conv_fwd1(u, m_fwd, cb=8192):
    n, seq, c = u.shape
    a_half = FFT_A // 2
    cols = seq * c // a_half
    uv = u.reshape(n, a_half, cols)
    return pl.pallas_call(
        _fwd1_body,
        out_shape=jax.ShapeDtypeStruct((n, 2, FFT_KA_PAD, cols), F32),
        grid=(n, cols // cb),
        in_specs=[pl.BlockSpec(m_fwd.shape, lambda i, j: (0, 0)),
                  pl.BlockSpec((1, a_half, cb), lambda i, j: (i, 0, j))],
        out_specs=pl.BlockSpec((1, 2, FFT_KA_PAD, cb), lambda i, j: (i, 0, 0, j)),
        compiler_params=_cparams(("parallel", "parallel")),
        name="conv_fwd1",
    )(m_fwd, uv)


def _fwd2f_body(sf_ref, sb_ref, g_ref, l1_ref, kf_ref):
    o = pl.program_id(0)
    r2 = 2 * FFT_R
    c = sf_ref.shape[-1]
    g = g_ref[0]
    xf = _dot(g, sf_ref[0, :, 0].reshape(r2, c).astype(BF16))
    xb = _dot(g, sb_ref[0, :, 0].reshape(r2, c).astype(BF16))
    inv = 1.0 / (l1_ref[2 * o] + l1_ref[2 * o + 1] + EPS)
    kf_ref[0, 0, 0:FFT_R] = (xf[0:FFT_R] + xb[0:FFT_R]) * inv
    kf_ref[0, 0, FFT_R:r2] = (xf[FFT_R:r2] - xb[FFT_R:r2]) * inv


def _filter_spectrum(s_filt, l1, g2, c):
    n_ord = s_filt.shape[0] // 2
    sv = s_filt.reshape(2 * n_ord, 2, FFT_KA_PAD, FFT_R, c)
    r2 = 2 * FFT_R
    return pl.pallas_call(
        _fwd2f_body,
        out_shape=jax.ShapeDtypeStruct((n_ord, FFT_KA_PAD, r2, c), F32),
        grid=(n_ord, FFT_KA_PAD),
        in_specs=[pl.BlockSpec((1, 2, 1, FFT_R, c), lambda o, ka: (2 * o, 0, ka, 0, 0)),
                  pl.BlockSpec((1, 2, 1, FFT_R, c), lambda o, ka: (2 * o + 1, 0, ka, 0, 0)),
                  pl.BlockSpec((1, r2, r2), lambda o, ka: (ka, 0, 0)),
                  pl.BlockSpec(l1.shape, lambda o, ka: (0, 0, 0))],
        out_specs=pl.BlockSpec((1, 1, r2, c), lambda o, ka: (o, ka, 0, 0)),
        compiler_params=_cparams(("parallel", "parallel")),
        name="filter_spectrum",
    )(sv, sv, g2, l1)


def _mid_body(s_ref, g_ref, gh_ref, kf_ref, t_ref):
    r2 = 2 * FFT_R
    c = s_ref.shape[-1]
    x = _dot(g_ref[0], s_ref[0, :, 0].reshape(r2, c).astype(BF16))
    xr, xi = x[0:FFT_R], x[FFT_R:r2]
    kr, ki = kf_ref[0, 0, 0:FFT_R], kf_ref[0, 0, FFT_R:r2]
    y = jnp.concatenate([xr * kr - xi * ki, xr * ki + xi * kr], axis=0).astype(BF16)
    t = _dot(gh_ref[0], y)
    t_ref[0, 0, 0] = t[0:FFT_R]
    t_ref[0, 1, 0] = t[FFT_R:r2]


def _conv_mid(s, kf, order, g2, g2h, c):
    n = s.shape[0]
    sv = s.reshape(n, 2, FFT_KA_PAD, FFT_R, c)
    r2 = 2 * FFT_R
    blk = pl.BlockSpec((1, 2, 1, FFT_R, c), lambda i, ka: (i, 0, ka, 0, 0))
    gspec = pl.BlockSpec((1, r2, r2), lambda i, ka: (ka, 0, 0))
    out = pl.pallas_call(
        _mid_body,
        out_shape=jax.ShapeDtypeStruct(sv.shape, F32),
        grid=(n, FFT_KA_PAD),
        in_specs=[blk, gspec, gspec,
                  pl.BlockSpec((1, 1, r2, c), lambda i, ka: (order, ka, 0, 0))],
        out_specs=blk,
        compiler_params=_cparams(("parallel", "parallel")),
        name="conv_mid",
    )(sv, g2, g2h, kf)
    return out.reshape(s.shape)


def _inv1_body(m_ref, t_ref, u_ref, xg_ref, sk_ref, o_ref):
    cb = t_ref.shape[-1]
    t2 = t_ref[0].reshape(2 * FFT_KA_PAD, cb).astype(BF16)
    y = _dot(m_ref[...], t2)
    u = u_ref[0].astype(F32)
    o_ref[0] = (xg_ref[0].astype(F32) * (y + u * sk_ref[...])).astype(BF16)


def _conv_inv1(t, u, xg, skip, m_inv, cb=8192):
    n, seq, c = u.shape
    a_half = FFT_A // 2
    cols = seq * c // a_half
    sk = jnp.tile(skip.astype(F32).reshape(1, c), (1, cols // c))
    uspec = pl.BlockSpec((1, a_half, cb), lambda i, j: (i, 0, j))
    out = pl.pallas_call(
        _inv1_body,
        out_shape=jax.ShapeDtypeStruct((n, a_half, cols), BF16),
        grid=(n, cols // cb),
        in_specs=[pl.BlockSpec(m_inv.shape, lambda i, j: (0, 0)),
                  pl.BlockSpec((1, 2, FFT_KA_PAD, cb), lambda i, j: (i, 0, 0, j)),
                  uspec, uspec,
                  pl.BlockSpec((1, cb), lambda i, j: (0, j))],
        out_specs=uspec,
        compiler_params=_cparams(("parallel", "parallel")),
        name="conv_inv1",
    )(m_inv, t, u.reshape(n, a_half, cols), xg.reshape(n, a_half, cols), sk)
    return out.reshape(n, seq, c)


def _hyena(v, x1, x2, f_w1, f_b1, f_w2, f_b2, f_w3, f_freq, skip):
    _, seq, c = v.shape
    assert 2 * seq == FFT_A * FFT_R
    m_fwd, m_inv, g2, g2h = _conv_dft_constants()
    taps, l1 = _hyena_filter_taps(seq, f_w1, f_b1, f_w2, f_b2, f_w3, f_freq, c)
    kf = _filter_spectrum(_conv_fwd1(taps, m_fwd), l1, g2, c)
    y = v
    for order, xg in enumerate((x1, x2)):
        t = _conv_mid(_conv_fwd1(y, m_fwd), kf, order, g2, g2h, c)
        y = _conv_inv1(t, y, xg, skip[order], m_inv)
    return y


def _mixout_body(x_ref, a1_ref, a2_ref, w_ref, gt_ref, o_ref):
    c1 = a1_ref.shape[-1]
    y = _dot(a1_ref[0], w_ref[0:c1]) + _dot(a2_ref[0], w_ref[c1:])
    o_ref[0] = x_ref[0] + gt_ref[0] * y


def _mixout(x, a1, a2, w, gate, tm=512):
    b, s, d = x.shape
    tok = lambda c: pl.BlockSpec((1, tm, c), lambda bi, i: (bi, i, 0))
    return pl.pallas_call(
        _mixout_body,
        out_shape=jax.ShapeDtypeStruct(x.shape, F32),
        grid=(b, s // tm),
        in_specs=[tok(d), tok(a1.shape[-1]), tok(a2.shape[-1]),
                  pl.BlockSpec(w.shape, lambda bi, i: (0, 0)),
                  pl.BlockSpec((1, 1, d), lambda bi, i: (bi, 0, 0))],
        out_specs=tok(d),
        compiler_params=_cparams(("parallel", "parallel")),
        name="mixout",
    )(x, a1, a2, w, gate)


def _resid_matmul_body(x_ref, a_ref, w_ref, gt_ref, o_ref):
    o_ref[0] = x_ref[0] + gt_ref[0] * _dot(a_ref[0].astype(BF16), w_ref[...])


def _resid_matmul(x, a, w, gate, tm=512):
    b, s, d = x.shape
    tok = lambda c: pl.BlockSpec((1, tm, c), lambda bi, i: (bi, i, 0))
    return pl.pallas_call(
        _resid_matmul_body,
        out_shape=jax.ShapeDtypeStruct(x.shape, F32),
        grid=(b, s // tm),
        in_specs=[tok(d), tok(a.shape[-1]),
                  pl.BlockSpec(w.shape, lambda bi, i: (0, 0)),
                  pl.BlockSpec((1, 1, d), lambda bi, i: (bi, 0, 0))],
        out_specs=tok(d),
        compiler_params=_cparams(("parallel", "parallel")),
        name="resid_matmul",
    )(x, a, w, gate)


def _ffn_body(x_ref, g_ref, sh_ref, sc_ref, gt_ref, w1_ref, w3_ref, w2_ref, o_ref, h_scr, acc_scr):
    j = pl.program_id(2)

    @pl.when(j == 0)
    def _():
        h_scr[...] = _norm_mod(x_ref[0], g_ref[...], sh_ref[0], sc_ref[0]).astype(BF16)
        acc_scr[...] = jnp.zeros_like(acc_scr)

    h = h_scr[...]
    a = _dot(h, w1_ref[...])
    u = (a * jax.nn.sigmoid(a) * _dot(h, w3_ref[...])).astype(BF16)
    acc_scr[...] += _dot(u, w2_ref[...])

    @pl.when(j == pl.num_programs(2) - 1)
    def _():
        o_ref[0] = x_ref[0] + gt_ref[0] * acc_scr[...]


def _ffn(x, g, shift, scale, gate, w1, w3, w2, tm=512, fb=1408):
    b, s, d = x.shape
    f = w1.shape[1]
    tok = pl.BlockSpec((1, tm, d), lambda bi, i, j: (bi, i, 0))
    per_b = pl.BlockSpec((1, 1, d), lambda bi, i, j: (bi, 0, 0))
    return pl.pallas_call(
        _ffn_body,
        out_shape=jax.ShapeDtypeStruct(x.shape, F32),
        grid=(b, s // tm, f // fb),
        in_specs=[tok, pl.BlockSpec(g.shape, lambda bi, i, j: (0, 0)), per_b, per_b, per_b,
                  pl.BlockSpec((d, fb), lambda bi, i, j: (0, j)),
                  pl.BlockSpec((d, fb), lambda bi, i, j: (0, j)),
                  pl.BlockSpec((fb, d), lambda bi, i, j: (j, 0))],
        out_specs=tok,
        scratch_shapes=[pltpu.VMEM((tm, d), BF16), pltpu.VMEM((tm, d), F32)],
        compiler_params=_cparams(("parallel", "parallel", "arbitrary")),
        name="ffn",
    )(x, g, shift, scale, gate, w1, w3, w2)


def _fm_constants(cg):
    j = np.arange(cg)[:, None]
    m = np.arange(cg)[None, :]
    ph = 2.0 * np.pi * j * m / cg
    w_cs = np.concatenate([np.cos(ph), np.sin(ph)], axis=1)
    d = np.arange(FM_A)[:, None]
    a = np.arange(FM_A)[None, :]
    ph = 2.0 * np.pi * d * a / FM_A
    fr, fi = np.cos(ph), -np.sin(ph)
    m1 = np.block([[fr, fi], [fi, -fr]])
    n = FM_A * FM_A
    dd = np.arange(FM_A)[:, None, None]
    c = np.arange(FM_A)[None, :, None]
    b = np.arange(FM_A)[None, None, :]
    th = 2.0 * np.pi * (b * c / FM_A + b * dd / n)
    gcat = np.concatenate([np.cos(th), np.sin(th)], axis=2)
    return _mxu_const(w_cs), _mxu_const(m1), _mxu_const(gcat)


def _fm_chan_body(x_ref, g_ref, sh_ref, sc_ref, w_ref, o_ref, *, cg):
    h = _norm_mod(x_ref[0], g_ref[...], sh_ref[0], sc_ref[0]).astype(BF16)
    for grp in range(h.shape[-1] // cg):
        pq = _dot(h[:, grp * cg:(grp + 1) * cg], w_ref[...])
        o_ref[0, 0, :, grp * cg:(grp + 1) * cg] = pq[:, 0:cg].astype(BF16)
        o_ref[0, 1, :, grp * cg:(grp + 1) * cg] = pq[:, cg:2 * cg].astype(BF16)


def _fm_chan(x, g, shift, scale, w_cs, tm=512):
    b, s, d = x.shape
    cg = w_cs.shape[0]
    per_b = pl.BlockSpec((1, 1, d), lambda bi, i: (bi, 0, 0))
    return pl.pallas_call(
        functools.partial(_fm_chan_body, cg=cg),
        out_shape=jax.ShapeDtypeStruct((b, 2, s, d), BF16),
        grid=(b, s // tm),
        in_specs=[pl.BlockSpec((1, tm, d), lambda bi, i: (bi, i, 0)),
                  pl.BlockSpec(g.shape, lambda bi, i: (0, 0)), per_b, per_b,
                  pl.BlockSpec(w_cs.shape, lambda bi, i: (0, 0))],
        out_specs=pl.BlockSpec((1, 2, tm, d), lambda bi, i: (bi, 0, i, 0)),
        compiler_params=_cparams(("parallel", "parallel")),
        name="fm_chan",
    )(x, g, shift, scale, w_cs)


def _fm_s1_body(m_ref, pq_ref, o_ref):
    o_ref[0] = _dot(m_ref[...], pq_ref[0]).astype(BF16)


def _fm_stage1(pq, m1, cb=8192):
    b, _, s, d = pq.shape
    cols = s * d // FM_A
    blk = pl.BlockSpec((1, 2 * FM_A, cb), lambda bi, j: (bi, 0, j))
    return pl.pallas_call(
        _fm_s1_body,
        out_shape=jax.ShapeDtypeStruct((b, 2 * FM_A, cols), BF16),
        grid=(b, cols // cb),
        in_specs=[pl.BlockSpec(m1.shape, lambda bi, j: (0, 0)), blk],
        out_specs=blk,
        compiler_params=_cparams(("parallel", "parallel")),
        name="fm_stage1",
    )(m1, pq.reshape(b, 2 * FM_A, cols))


def _fm_s2_body(s_ref, g_ref, o_ref, *, dblk, scale):
    for i in range(dblk):
        s2 = jnp.concatenate([s_ref[0, 0, i], s_ref[0, 1, i]], axis=0)
        o_ref[:, i, :] = _dot(g_ref[i], s2) * scale


def _fm_stage2(s1, gcat, seq, d, dblk=8):
    b = s1.shape[0]
    sv = s1.reshape(b, 2, FM_A, FM_A, d)
    scale = 1.0 / math.sqrt(seq * (d // F_GROUPS))
    out = pl.pallas_call(
        functools.partial(_fm_s2_body, dblk=dblk, scale=scale),
        out_shape=jax.ShapeDtypeStruct((b * FM_A, dblk * (FM_A // dblk), d), F32),
        grid=(b, FM_A // dblk),
        in_specs=[pl.BlockSpec((1, 2, dblk, FM_A, d), lambda bi, j: (bi, 0, j, 0, 0)),
                  pl.BlockSpec((dblk, FM_A, 2 * FM_A), lambda bi, j: (j, 0, 0))],
        out_specs=pl.BlockSpec((FM_A, dblk, d), lambda bi, j: (bi, j, 0)),
        compiler_params=_cparams(("parallel", "parallel")),
        name="fm_stage2",
    )(sv, gcat)
    return out.reshape(b, seq, d)


def _fourier_mix(x, g, shift, scale):
    b, s, d = x.shape
    assert s == FM_A * FM_A
    w_cs, m1, gcat = _fm_constants(d // F_GROUPS)
    pq = _fm_chan(x, g, shift, scale, w_cs)
    return _fm_stage2(_fm_stage1(pq, m1), gcat, s, d)


def _router_body(x_ref, g_ref, sh_ref, sc_ref, wr_ref, br_ref, h_ref, gate_ref):
    h = _norm_mod(x_ref[0], g_ref[...], sh_ref[0], sc_ref[0])
    h_ref[0] = h.astype(BF16)
    logits = jnp.dot(h, wr_ref[...], precision=HIGHEST, preferred_element_type=F32) + br_ref[...]
    lane = lax.broadcasted_iota(jnp.int32, logits.shape, 1)
    nl = logits.shape[-1]
    m1 = jnp.max(logits, axis=-1, keepdims=True)
    i1 = jnp.min(jnp.where(logits == m1, lane, nl), axis=-1, keepdims=True)
    rest = jnp.where(lane == i1, -3.0e38, logits)
    m2 = jnp.max(rest, axis=-1, keepdims=True)
    i2 = jnp.min(jnp.where(rest == m2, lane, nl), axis=-1, keepdims=True)
    e = jnp.exp(m2 - m1)
    gate_ref[0] = jnp.where(lane == i1, 1.0 / (1.0 + e), 0.0) + jnp.where(lane == i2, e / (1.0 + e), 0.0)


def _router(x, g, shift, scale, w_router, b_router, tm=512):
    b, s, d = x.shape
    ne = w_router.shape[1]
    wr = jnp.pad(w_router.astype(F32), ((0, 0), (0, 128 - ne)))
    br = jnp.pad(b_router.astype(F32).reshape(1, ne), ((0, 0), (0, 128 - ne)), constant_values=NEG_INF)
    per_b = pl.BlockSpec((1, 1, d), lambda bi, i: (bi, 0, 0))
    tok = lambda c: pl.BlockSpec((1, tm, c), lambda bi, i: (bi, i, 0))
    return pl.pallas_call(
        _router_body,
        out_shape=[jax.ShapeDtypeStruct((b, s, d), BF16), jax.ShapeDtypeStruct((b, s, 128), F32)],
        grid=(b, s // tm),
        in_specs=[tok(d), pl.BlockSpec(g.shape, lambda bi, i: (0, 0)), per_b, per_b,
                  pl.BlockSpec(wr.shape, lambda bi, i: (0, 0)), pl.BlockSpec(br.shape, lambda bi, i: (0, 0))],
        out_specs=[tok(d), tok(128)],
        compiler_params=_cparams(("parallel", "parallel")),
        name="router",
    )(x, g, shift, scale, wr, br)


def _moe_body(x_ref, h_ref, gate_ref, gt_ref, fg_ref, w1_ref, w3_ref, w2_ref, o_ref, acc_scr):
    e = pl.program_id(2)
    j = pl.program_id(3)

    @pl.when((e == 0) & (j == 0))
    def _():
        acc_scr[...] = jnp.zeros_like(acc_scr)

    h = h_ref[0]
    gate = gate_ref[0]
    lane = lax.broadcasted_iota(jnp.int32, gate.shape, 1)
    ge = jnp.sum(jnp.where(lane == e, gate, 0.0), axis=-1, keepdims=True)
    a = _dot(h, w1_ref[0])
    u = (a * jax.nn.sigmoid(a) * _dot(h, w3_ref[0]) * ge).astype(BF16)
    acc_scr[...] += _dot(u, w2_ref[0])

    @pl.when((e == pl.num_programs(2) - 1) & (j == pl.num_programs(3) - 1))
    def _():
        xo = x_ref[0] + gt_ref[0] * acc_scr[...]
        ms = jnp.mean(xo * xo, axis=-1, keepdims=True)
        o_ref[0] = xo * lax.rsqrt(ms + EPS) * fg_ref[...]


def _moe(x, h, gate, gt, final_g, w1, w3, w2, tm=1024, fb=512):
    b, s, d = x.shape
    ne, _, f = w1.shape
    tok = lambda c: pl.BlockSpec((1, tm, c), lambda bi, i, e, j: (bi, i, 0))
    return pl.pallas_call(
        _moe_body,
        out_shape=jax.ShapeDtypeStruct(x.shape, F32),
        grid=(b, s // tm, ne, f // fb),
        in_specs=[tok(d), tok(d), tok(128),
                  pl.BlockSpec((1, 1, d), lambda bi, i, e, j: (bi, 0, 0)),
                  pl.BlockSpec(final_g.shape, lambda bi, i, e, j: (0, 0)),
                  pl.BlockSpec((1, d, fb), lambda bi, i, e, j: (e, 0, j)),
                  pl.BlockSpec((1, d, fb), lambda bi, i, e, j: (e, 0, j)),
                  pl.BlockSpec((1, fb, d), lambda bi, i, e, j: (e, j, 0))],
        out_specs=tok(d),
        scratch_shapes=[pltpu.VMEM((tm, d), F32)],
        compiler_params=_cparams(("parallel", "parallel", "arbitrary", "arbitrary")),
        name="moe",
    )(x, h, gate, gt, final_g, w1, w3, w2)


def kernel(x, c, ctx, c_ctx, w_ada, b_ada, norm_g, w_in, hy_short_w, hy_short_b, hy_f_w1, hy_f_b1, hy_f_w2, hy_f_b2, hy_f_w3, hy_f_freq, hy_skip, na_rpb, w_mix_out, ffn_w1, ffn_w3, ffn_w2, w_fourier, w_router, b_router, moe_w1, moe_w3, moe_w2, final_g):
    b, s, d = x.shape
    depth = w_ada.shape[0]
    assert depth == 2, "layer 0 mixes with Hyena/attention, layer 1 with Fourier/MoE"
    c_hy = hy_skip.shape[-1]
    c_na = d - c_hy

    cvec = jnp.concatenate([c, c_ctx[None, :], jnp.zeros((8 - b - 1, d), F32)], axis=0)
    mods = _ada(cvec, w_ada, b_ada)

    def mod(layer, idx, ctx_row=False):
        m = mods[layer, :, idx * d:(idx + 1) * d]
        return m[b:b + 1, None, :] if ctx_row else m[0:b, None, :]

    row = lambda a: a.reshape(1, -1)

    w_in0 = w_in[0].astype(BF16)
    w_hy, w_qkv = w_in0[:, 0:3 * c_hy], w_in0[:, 3 * c_hy:]
    v, x1, x2, q, k, va = _inproj(x, row(norm_g[0, 0]), mod(0, 0), mod(0, 1), w_hy, w_qkv,
                                  hy_short_w[0], row(hy_short_b[0]))
    kc, vc = _ctxkv(ctx, row(norm_g[0, 0]), mod(0, 0, True), mod(0, 1, True), w_qkv[:, c_na:])
    y_na = _natt(q, k, va, kc, vc, _na_bias_table(na_rpb[0]))
    y_hy = _hyena(v, x1, x2, hy_f_w1[0], hy_f_b1[0], hy_f_w2[0], hy_f_b2[0], hy_f_w3[0],
                  hy_f_freq[0], hy_skip[0])
    x = _mixout(x, y_hy, y_na, w_mix_out[0].astype(BF16), mod(0, 2))
    x = _ffn(x, row(norm_g[0, 1]), mod(0, 3), mod(0, 4), mod(0, 5),
             ffn_w1[0].astype(BF16), ffn_w3[0].astype(BF16), ffn_w2[0].astype(BF16))

    y_f = _fourier_mix(x, row(norm_g[1, 0]), mod(1, 0), mod(1, 1))
    x = _resid_matmul(x, y_f, w_fourier[0].astype(BF16), mod(1, 2))
    h, gate = _router(x, row(norm_g[1, 1]), mod(1, 3), mod(1, 4), w_router[0], b_router[0])
    return _moe(x, h, gate, mod(1, 5), row(final_g),
                moe_w1[0].astype(BF16), moe_w3[0].astype(BF16), moe_w2[0].astype(BF16))
```

```python
import functools
import math

import numpy as np
import jax
import jax.numpy as jnp
from jax import lax
from jax.experimental import pallas as pl
from jax.experimental.pallas import tpu as pltpu

F32 = jnp.float32
BF16 = jnp.bfloat16
HIGHEST = lax.Precision.HIGHEST

GRID_W = 64
NA_HEAD_DIM = 32
NA_WIN_R = 8
NA_WIN_C = 16
HYENA_EMB = 33
HYENA_BANDS = (HYENA_EMB - 1) // 2
HYENA_FAST_DECAY = 0.3
HYENA_SLOW_DECAY = 1.5
HYENA_TARGET = 1e-2
F_GROUPS = 4
N_MOD = 6
EPS = 1e-6
NEG_INF = -1e30

FFT_A = 64
FFT_R = 128
FFT_KA = FFT_A // 2 + 1
FFT_KA_PAD = 40
FM_A = 64

VMEM_LIMIT = 48 * 1024 * 1024


def _cparams(sem):
    return pltpu.CompilerParams(dimension_semantics=sem, vmem_limit_bytes=VMEM_LIMIT)


def _dot(a, b):
    return jnp.dot(a, b, preferred_element_type=F32)


def _mxu_const(m):
    return jnp.asarray(m, dtype=F32).astype(BF16)


def _norm_mod(x, g, shift, scale):
    ms = jnp.mean(x * x, axis=-1, keepdims=True)
    y = x * lax.rsqrt(ms + EPS) * g
    return y * (1.0 + scale) + shift


def _ada_body(c_ref, w_ref, b_ref, o_ref):
    cv = c_ref[...]
    s = cv * jax.nn.sigmoid(cv)
    o_ref[0] = jnp.dot(s, w_ref[0], precision=HIGHEST, preferred_element_type=F32) + b_ref[0]


def _ada(cvec, w_ada, b_ada):
    depth, d, n = w_ada.shape
    rows = cvec.shape[0]
    bn = n // 4
    return pl.pallas_call(
        _ada_body,
        out_shape=jax.ShapeDtypeStruct((depth, rows, n), F32),
        grid=(depth, n // bn),
        in_specs=[pl.BlockSpec((rows, d), lambda l, j: (0, 0)),
                  pl.BlockSpec((1, d, bn), lambda l, j: (l, 0, j)),
                  pl.BlockSpec((1, 1, bn), lambda l, j: (l, 0, j))],
        out_specs=pl.BlockSpec((1, rows, bn), lambda l, j: (l, 0, j)),
        compiler_params=_cparams(("parallel", "parallel")),
        name="ada",
    )(cvec, w_ada, b_ada.reshape(depth, 1, n))


def _inproj_body(x_ref, xp_ref, xn_ref, g_ref, sh_ref, sc_ref, why_ref, wqkv_ref, sw_ref, sb_ref,
                 v_ref, x1_ref, x2_ref, q_ref, k_ref, va_ref, *, n_tiles, q_scale, c_hy, c_na):
    i = pl.program_id(1)
    g, sh, sc = g_ref[...], sh_ref[0], sc_ref[0]
    h = _norm_mod(x_ref[0], g, sh, sc).astype(BF16)
    zh = _dot(h, why_ref[...])
    hp = _norm_mod(xp_ref[0], g, sh, sc).astype(BF16)
    hn = _norm_mod(xn_ref[0], g, sh, sc).astype(BF16)
    zp = _dot(hp, why_ref[...])[7:8]
    zn = _dot(hn, why_ref[...])[0:1]
    zp = jnp.where(i > 0, zp, 0.0)
    zn = jnp.where(i < n_tiles - 1, zn, 0.0)
    tm = zh.shape[0]
    row = lax.broadcasted_iota(jnp.int32, zh.shape, 0)
    z_m1 = jnp.where(row == 0, zp, pltpu.roll(zh, 1, 0))
    z_p1 = jnp.where(row == tm - 1, zn, pltpu.roll(zh, tm - 1, 0))
    sw = sw_ref[...]
    zc = z_m1 * sw[0:1] + zh * sw[1:2] + z_p1 * sw[2:3] + sb_ref[...]
    v_ref[0] = zc[:, 0:c_hy].astype(BF16)
    x1_ref[0] = zc[:, c_hy:2 * c_hy].astype(BF16)
    x2_ref[0] = zc[:, 2 * c_hy:3 * c_hy].astype(BF16)
    zq = _dot(h, wqkv_ref[...])
    q_ref[0] = (zq[:, 0:c_na] * q_scale).astype(BF16)
    k_ref[0] = zq[:, c_na:2 * c_na].astype(BF16)
    va_ref[0] = zq[:, 2 * c_na:3 * c_na].astype(BF16)


def _inproj(x, g, shift, scale, w_hy, w_qkv, short_w, short_b, tm=512):
    b, s, d = x.shape
    c_hy = w_hy.shape[1] // 3
    c_na = w_qkv.shape[1] // 3
    n_tiles = s // tm
    r8 = tm // 8
    body = functools.partial(_inproj_body, n_tiles=n_tiles, q_scale=NA_HEAD_DIM ** -0.5,
                             c_hy=c_hy, c_na=c_na)
    tok = lambda c: pl.BlockSpec((1, tm, c), lambda bi, i: (bi, i, 0))
    full2 = lambda a: pl.BlockSpec(a.shape, lambda bi, i: (0, 0))
    per_b = pl.BlockSpec((1, 1, d), lambda bi, i: (bi, 0, 0))
    return pl.pallas_call(
        body,
        out_shape=[jax.ShapeDtypeStruct((b, s, c_hy), BF16)] * 3 + [jax.ShapeDtypeStruct((b, s, c_na), BF16)] * 3,
        grid=(b, n_tiles),
        in_specs=[tok(d),
                  pl.BlockSpec((1, 8, d), lambda bi, i: (bi, jnp.maximum(i * r8 - 1, 0), 0)),
                  pl.BlockSpec((1, 8, d), lambda bi, i: (bi, jnp.minimum((i + 1) * r8, s // 8 - 1), 0)),
                  full2(g), per_b, per_b, full2(w_hy), full2(w_qkv), full2(short_w), full2(short_b)],
        out_specs=[tok(c_hy)] * 3 + [tok(c_na)] * 3,
        compiler_params=_cparams(("parallel", "parallel")),
        name="inproj",
    )(x, x, x, g, shift, scale, w_hy, w_qkv, short_w, short_b)


def _ctxkv_body(x_ref, g_ref, sh_ref, sc_ref, w_ref, k_ref, v_ref, *, c_na):
    h = _norm_mod(x_ref[0], g_ref[...], sh_ref[0], sc_ref[0]).astype(BF16)
    z = _dot(h, w_ref[...])
    k_ref[0] = z[:, 0:c_na].astype(BF16)
    v_ref[0] = z[:, c_na:2 * c_na].astype(BF16)


def _ctxkv(ctx, g, shift, scale, w_kv):
    b, n, d = ctx.shape
    c_na = w_kv.shape[1] // 2
    one = pl.BlockSpec((1, 1, d), lambda bi: (0, 0, 0))
    return pl.pallas_call(
        functools.partial(_ctxkv_body, c_na=c_na),
        out_shape=[jax.ShapeDtypeStruct((b, n, c_na), BF16)] * 2,
        grid=(b,),
        in_specs=[pl.BlockSpec((1, n, d), lambda bi: (bi, 0, 0)),
                  pl.BlockSpec(g.shape, lambda bi: (0, 0)), one, one,
                  pl.BlockSpec(w_kv.shape, lambda bi: (0, 0))],
        out_specs=[pl.BlockSpec((1, n, c_na), lambda bi: (bi, 0, 0))] * 2,
        compiler_params=_cparams(("parallel",)),
        name="ctxkv",
    )(ctx, g, shift, scale, w_kv)


NA_HEADS_PER_BLK = 4


def _na_bias_body(r_ref, e_ref, ok_ref, o_ref):
    t = jnp.dot(r_ref[...], e_ref[...], precision=HIGHEST, preferred_element_type=F32)
    o_ref[...] = jnp.where(ok_ref[...] > 0.5, t, NEG_INF)


def _na_bias_table(rpb):
    w = GRID_W
    h, nr, nc = rpb.shape
    col = np.arange(w)[:, None]
    kc = np.arange(w)[None, :]
    c_start = np.clip(col - NA_WIN_C // 2, 0, w - NA_WIN_C)
    valid = ((kc >= c_start) & (kc < c_start + NA_WIN_C)).reshape(1, w * w)
    expand = (np.arange(32)[:, None, None] == (kc - col + NA_WIN_C - 1)[None]).reshape(32, w * w)
    rp = jnp.pad(rpb.astype(F32).reshape(h * nr, nc), ((0, 0), (0, 32 - nc)))
    full = lambda a: pl.BlockSpec(a.shape, lambda: (0,) * a.ndim)
    expand = jnp.asarray(expand, dtype=F32)
    ok = jnp.asarray(valid, dtype=F32)
    toep = pl.pallas_call(
        _na_bias_body,
        out_shape=jax.ShapeDtypeStruct((h * nr, w * w), F32),
        in_specs=[full(rp), full(expand), full(ok)],
        out_specs=pl.BlockSpec((h * nr, w * w), lambda: (0, 0)),
        name="na_bias",
    )(rp, expand, ok)
    t2 = toep.reshape(h, nr, w, w).transpose(0, 2, 1, 3).reshape(h, w, nr * w)
    slabs = jnp.stack([t2[:, :, (NA_WIN_R - 1 - off) * w:(2 * NA_WIN_R - 1 - off) * w]
                       for off in range(NA_WIN_R)], axis=1)
    hpb = NA_HEADS_PER_BLK
    slabs = slabs.reshape(h // hpb, hpb, NA_WIN_R, w, NA_WIN_R * w).transpose(0, 2, 1, 3, 4)
    return slabs.reshape(h // hpb, NA_WIN_R, hpb * w, NA_WIN_R * w)


def _natt_body(q_ref, k_ref, v_ref, kc_ref, vc_ref, bias_ref, o_ref, *, rows):
    w = GRID_W
    hpb = NA_HEADS_PER_BLK
    nloc = NA_WIN_R * w
    lane = lax.broadcasted_iota(jnp.int32, (1, hpb * NA_HEAD_DIM), 1)
    in_head = [(lane >= NA_HEAD_DIM * hh) & (lane < NA_HEAD_DIM * (hh + 1)) for hh in range(hpb)]
    kcx = kc_ref[0]
    vcx = vc_ref[0]
    nt = (((1,), (1,)), ((), ()))

    def one_row(r):
        r0 = jnp.clip(r - NA_WIN_R // 2, 0, rows - NA_WIN_R)
        off = r - r0
        qs = q_ref[0, pl.ds(pl.multiple_of(r * w, w), w), :]
        kw = k_ref[0, pl.ds(pl.multiple_of(r0 * w, w), nloc), :]
        vw = v_ref[0, pl.ds(pl.multiple_of(r0 * w, w), nloc), :]
        zero = jnp.zeros_like(qs)
        qst = jnp.concatenate([jnp.where(m, qs, zero) for m in in_head], axis=0)
        s_loc = lax.dot_general(qst, kw, nt, preferred_element_type=F32) + bias_ref[0, off]
        s_ctx = lax.dot_general(qst, kcx, nt, preferred_element_type=F32)
        m = jnp.maximum(jnp.max(s_loc, axis=-1, keepdims=True), jnp.max(s_ctx, axis=-1, keepdims=True))
        p_loc = jnp.exp(s_loc - m)
        p_ctx = jnp.exp(s_ctx - m)
        den = jnp.sum(p_loc, axis=-1, keepdims=True) + jnp.sum(p_ctx, axis=-1, keepdims=True)
        o = (_dot(p_loc.astype(BF16), vw) + _dot(p_ctx.astype(BF16), vcx)) * (1.0 / den)
        acc = jnp.where(in_head[0], o[0:w], 0.0)
        for hh in range(1, hpb):
            acc = acc + jnp.where(in_head[hh], o[hh * w:(hh + 1) * w], 0.0)
        o_ref[0, pl.ds(pl.multiple_of(r * w, w), w), :] = acc.astype(BF16)

    def row_pair(i, carry):
        one_row(2 * i)
        one_row(2 * i + 1)
        return carry

    lax.fori_loop(0, rows // 2, row_pair, 0)


def _natt(q, k, v, kc, vc, bias):
    b, s, c = q.shape
    nctx = kc.shape[1]
    lw = NA_HEADS_PER_BLK * NA_HEAD_DIM
    rows = s // GRID_W
    seq = pl.BlockSpec((1, s, lw), lambda bi, g: (bi, 0, g))
    cx = pl.BlockSpec((1, nctx, lw), lambda bi, g: (bi, 0, g))
    return pl.pallas_call(
        functools.partial(_natt_body, rows=rows),
        out_shape=jax.ShapeDtypeStruct((b, s, c), BF16),
        grid=(b, c // lw),
        in_specs=[seq, seq, seq, cx, cx,
                  pl.BlockSpec((1,) + bias.shape[1:], lambda bi, g: (g, 0, 0, 0))],
        out_specs=seq,
        compiler_params=_cparams(("parallel", "parallel")),
        name="natt",
    )(q, k, v, kc, vc, bias)


def _hyena_feats(seq_len):
    t = jnp.linspace(0.0, 1.0, seq_len, dtype=F32)[:, None]
    bands = jnp.linspace(1e-4, HYENA_BANDS - 1, HYENA_BANDS, dtype=F32)
    ang = (2.0 * math.pi / seq_len) * jnp.arange(seq_len, dtype=F32)[:, None] * bands[None, :]
    feats = jnp.concatenate([t, jnp.cos(ang), -jnp.sin(ang)], axis=-1)
    return jnp.pad(feats, ((0, 0), (0, 128 - HYENA_EMB)))


def _filt_body(feat_ref, w1_ref, b1_ref, w2_ref, b2_ref, w3_ref, fr_ref, dl_ref, o_ref, l1_ref):
    j = pl.program_id(0)
    hp = functools.partial(jnp.dot, precision=HIGHEST, preferred_element_type=F32)
    feats = feat_ref[...]
    fr = fr_ref[...]
    h = jnp.sin(fr[0:1] * (hp(feats, w1_ref[...]) + b1_ref[...]))
    h = jnp.sin(fr[1:2] * (hp(h, w2_ref[...]) + b2_ref[...]))
    hc = hp(h, w3_ref[...])
    t = feats[:, 0:1]
    hc = hc * jnp.exp(-t * dl_ref[...])
    row = lax.broadcasted_iota(jnp.int32, hc.shape, 0)
    hc = jnp.where((row == 0) & (j % 2 == 1), 0.0, hc)
    l1_ref[0] = jnp.sum(jnp.abs(hc), axis=0, keepdims=True)
    o_ref[0] = hc.astype(BF16)


def _hyena_filter_taps(seq_len, f_w1, f_b1, f_w2, f_b2, f_w3, f_freq, c_hy):
    feats = _hyena_feats(seq_len)
    hid = f_w1.shape[1]
    w1 = jnp.pad(f_w1.astype(F32), ((0, 128 - HYENA_EMB), (0, 0)))
    deltas = jnp.abs(jnp.linspace(math.log(HYENA_TARGET) / HYENA_SLOW_DECAY,
                                  math.log(HYENA_TARGET) / HYENA_FAST_DECAY, c_hy, dtype=F32))[None, :]
    nblk = f_w3.shape[1] // c_hy
    c0 = lambda a: pl.BlockSpec(a.shape, lambda j: (0, 0))
    b1, b2 = f_b1.reshape(1, hid), f_b2.reshape(1, hid)
    return pl.pallas_call(
        _filt_body,
        out_shape=[jax.ShapeDtypeStruct((nblk, seq_len, c_hy), BF16),
                   jax.ShapeDtypeStruct((nblk, 1, c_hy), F32)],
        grid=(nblk,),
        in_specs=[c0(feats), c0(w1), c0(b1), c0(f_w2), c0(b2),
                  pl.BlockSpec((hid, c_hy), lambda j: (0, j)), c0(f_freq), c0(deltas)],
        out_specs=[pl.BlockSpec((1, seq_len, c_hy), lambda j: (j, 0, 0)),
                   pl.BlockSpec((1, 1, c_hy), lambda j: (j, 0, 0))],
        compiler_params=_cparams(("parallel",)),
        name="hyena_filter",
    )(feats, w1, b1, f_w2, b2, f_w3, f_freq, deltas)


def _conv_dft_constants():
    a_half = FFT_A // 2
    n = FFT_A * FFT_R
    ka = np.arange(FFT_KA)[:, None]
    a = np.arange(a_half)[None, :]
    ph = 2.0 * np.pi * ka * a / FFT_A
    m_fwd = np.zeros((2 * FFT_KA_PAD, a_half))
    m_fwd[:FFT_KA] = np.cos(ph)
    m_fwd[FFT_KA_PAD:FFT_KA_PAD + FFT_KA] = -np.sin(ph)
    wgt = np.where((ka == 0) | (ka == FFT_A // 2), 1.0, 2.0)
    m_inv = np.zeros((a_half, 2 * FFT_KA_PAD))
    m_inv[:, :FFT_KA] = (wgt * np.cos(ph)).T / n
    m_inv[:, FFT_KA_PAD:FFT_KA_PAD + FFT_KA] = (-wgt * np.sin(ph)).T / n
    kb = np.arange(FFT_R)[None, :, None]
    b = np.arange(FFT_R)[None, None, :]
    kaa = np.arange(FFT_KA)[:, None, None]
    th = 2.0 * np.pi * (b * kb / FFT_R + b * kaa / n)
    gr, gi = np.cos(th), -np.sin(th)
    g2 = np.zeros((FFT_KA_PAD, 2 * FFT_R, 2 * FFT_R))
    g2[:FFT_KA] = np.block([[gr, -gi], [gi, gr]])
    grt, git = gr.transpose(0, 2, 1), gi.transpose(0, 2, 1)
    g2h = np.zeros_like(g2)
    g2h[:FFT_KA] = np.block([[grt, git], [-git, grt]])
    return _mxu_const(m_fwd), _mxu_const(m_inv), _mxu_const(g2), _mxu_const(g2h)


def _fwd1_body(m_ref, u_ref, o_ref):
    res = _dot(m_ref[...], u_ref[0])
    o_ref[0, 0] = res[0:FFT_KA_PAD]
    o_ref[0, 1] = res[FFT_KA_PAD:2 * FFT_KA_PAD]


def _conv_fwd1(u, m_fwd, cb=8192):
    n, seq, c = u.shape
    a_half = FFT_A // 2
    cols = seq * c // a_half
    uv = u.reshape(n, a_half, cols)
    return pl.pallas_call(
        _fwd1_body,
        out_shape=jax.ShapeDtypeStruct((n, 2, FFT_KA_PAD, cols), F32),
        grid=(n, cols // cb),
        in_specs=[pl.BlockSpec(m_fwd.shape, lambda i, j: (0, 0)),
                  pl.BlockSpec((1, a_half, cb), lambda i, j: (i, 0, j))],
        out_specs=pl.BlockSpec((1, 2, FFT_KA_PAD, cb), lambda i, j: (i, 0, 0, j)),
        compiler_params=_cparams(("parallel", "parallel")),
        name="conv_fwd1",
    )(m_fwd, uv)


def _fwd2f_body(sf_ref, sb_ref, g_ref, l1_ref, kf_ref):
    o = pl.program_id(0)
    r2 = 2 * FFT_R
    c = sf_ref.shape[-1]
    g = g_ref[0]
    xf = _dot(g, sf_ref[0, :, 0].reshape(r2, c).astype(BF16))
    xb = _dot(g, sb_ref[0, :, 0].reshape(r2, c).astype(BF16))
    inv = 1.0 / (l1_ref[2 * o] + l1_ref[2 * o + 1] + EPS)
    kf_ref[0, 0, 0:FFT_R] = (xf[0:FFT_R] + xb[0:FFT_R]) * inv
    kf_ref[0, 0, FFT_R:r2] = (xf[FFT_R:r2] - xb[FFT_R:r2]) * inv


def _filter_spectrum(s_filt, l1, g2, c):
    n_ord = s_filt.shape[0] // 2
    sv = s_filt.reshape(2 * n_ord, 2, FFT_KA_PAD, FFT_R, c)
    r2 = 2 * FFT_R
    return pl.pallas_call(
        _fwd2f_body,
        out_shape=jax.ShapeDtypeStruct((n_ord, FFT_KA_PAD, r2, c), F32),
        grid=(n_ord, FFT_KA_PAD),
        in_specs=[pl.BlockSpec((1, 2, 1, FFT_R, c), lambda o, ka: (2 * o, 0, ka, 0, 0)),
                  pl.BlockSpec((1, 2, 1, FFT_R, c), lambda o, ka: (2 * o + 1, 0, ka, 0, 0)),
                  pl.BlockSpec((1, r2, r2), lambda o, ka: (ka, 0, 0)),
                  pl.BlockSpec(l1.shape, lambda o, ka: (0, 0, 0))],
        out_specs=pl.BlockSpec((1, 1, r2, c), lambda o, ka: (o, ka, 0, 0)),
        compiler_params=_cparams(("parallel", "parallel")),
        name="filter_spectrum",
    )(sv, sv, g2, l1)


def _mid_body(s_ref, g_ref, gh_ref, kf_ref, t_ref):
    r2 = 2 * FFT_R
    c = s_ref.shape[-1]
    x = _dot(g_ref[0], s_ref[0, :, 0].reshape(r2, c).astype(BF16))
    xr, xi = x[0:FFT_R], x[FFT_R:r2]
    kr, ki = kf_ref[0, 0, 0:FFT_R], kf_ref[0, 0, FFT_R:r2]
    y = jnp.concatenate([xr * kr - xi * ki, xr * ki + xi * kr], axis=0).astype(BF16)
    t = _dot(gh_ref[0], y)
    t_ref[0, 0, 0] = t[0:FFT_R]
    t_ref[0, 1, 0] = t[FFT_R:r2]


def _conv_mid(s, kf, order, g2, g2h, c):
    n = s.shape[0]
    sv = s.reshape(n, 2, FFT_KA_PAD, FFT_R, c)
    r2 = 2 * FFT_R
    blk = pl.BlockSpec((1, 2, 1, FFT_R, c), lambda i, ka: (i, 0, ka, 0, 0))
    gspec = pl.BlockSpec((1, r2, r2), lambda i, ka: (ka, 0, 0))
    out = pl.pallas_call(
        _mid_body,
        out_shape=jax.ShapeDtypeStruct(sv.shape, F32),
        grid=(n, FFT_KA_PAD),
        in_specs=[blk, gspec, gspec,
                  pl.BlockSpec((1, 1, r2, c), lambda i, ka: (order, ka, 0, 0))],
        out_specs=blk,
        compiler_params=_cparams(("parallel", "parallel")),
        name="conv_mid",
    )(sv, g2, g2h, kf)
    return out.reshape(s.shape)


def _inv1_body(m_ref, t_ref, u_ref, xg_ref, sk_ref, o_ref):
    cb = t_ref.shape[-1]
    t2 = t_ref[0].reshape(2 * FFT_KA_PAD, cb).astype(BF16)
    y = _dot(m_ref[...], t2)
    u = u_ref[0].astype(F32)
    o_ref[0] = (xg_ref[0].astype(F32) * (y + u * sk_ref[...])).astype(BF16)


def _conv_inv1(t, u, xg, skip, m_inv, cb=8192):
    n, seq, c = u.shape
    a_half = FFT_A // 2
    cols = seq * c // a_half
    sk = jnp.tile(skip.astype(F32).reshape(1, c), (1, cols // c))
    uspec = pl.BlockSpec((1, a_half, cb), lambda i, j: (i, 0, j))
    out = pl.pallas_call(
        _inv1_body,
        out_shape=jax.ShapeDtypeStruct((n, a_half, cols), BF16),
        grid=(n, cols // cb),
        in_specs=[pl.BlockSpec(m_inv.shape, lambda i, j: (0, 0)),
                  pl.BlockSpec((1, 2, FFT_KA_PAD, cb), lambda i, j: (i, 0, 0, j)),
                  uspec, uspec,
                  pl.BlockSpec((1, cb), lambda i, j: (0, j))],
        out_specs=uspec,
        compiler_params=_cparams(("parallel", "parallel")),
        name="conv_inv1",
    )(m_inv, t, u.reshape(n, a_half, cols), xg.reshape(n, a_half, cols), sk)
    return out.reshape(n, seq, c)


def _hyena(v, x1, x2, f_w1, f_b1, f_w2, f_b2, f_w3, f_freq, skip):
    _, seq, c = v.shape
    assert 2 * seq == FFT_A * FFT_R
    m_fwd, m_inv, g2, g2h = _conv_dft_constants()
    taps, l1 = _hyena_filter_taps(seq, f_w1, f_b1, f_w2, f_b2, f_w3, f_freq, c)
    kf = _filter_spectrum(_conv_fwd1(taps, m_fwd), l1, g2, c)
    y = v
    for order, xg in enumerate((x1, x2)):
        t = _conv_mid(_conv_fwd1(y, m_fwd), kf, order, g2, g2h, c)
        y = _conv_inv1(t, y, xg, skip[order], m_inv)
    return y


def _mixout_body(x_ref, a1_ref, a2_ref, w_ref, gt_ref, o_ref):
    c1 = a1_ref.shape[-1]
    y = _dot(a1_ref[0], w_ref[0:c1]) + _dot(a2_ref[0], w_ref[c1:])
    o_ref[0] = x_ref[0] + gt_ref[0] * y


def _mixout(x, a1, a2, w, gate, tm=512):
    b, s, d = x.shape
    tok = lambda c: pl.BlockSpec((1, tm, c), lambda bi, i: (bi, i, 0))
    return pl.pallas_call(
        _mixout_body,
        out_shape=jax.ShapeDtypeStruct(x.shape, F32),
        grid=(b, s // tm),
        in_specs=[tok(d), tok(a1.shape[-1]), tok(a2.shape[-1]),
                  pl.BlockSpec(w.shape, lambda bi, i: (0, 0)),
                  pl.BlockSpec((1, 1, d), lambda bi, i: (bi, 0, 0))],
        out_specs=tok(d),
        compiler_params=_cparams(("parallel", "parallel")),
        name="mixout",
    )(x, a1, a2, w, gate)


def _resid_matmul_body(x_ref, a_ref, w_ref, gt_ref, o_ref):
    o_ref[0] = x_ref[0] + gt_ref[0] * _dot(a_ref[0].astype(BF16), w_ref[...])


def _resid_matmul(x, a, w, gate, tm=512):
    b, s, d = x.shape
    tok = lambda c: pl.BlockSpec((1, tm, c), lambda bi, i: (bi, i, 0))
    return pl.pallas_call(
        _resid_matmul_body,
        out_shape=jax.ShapeDtypeStruct(x.shape, F32),
        grid=(b, s // tm),
        in_specs=[tok(d), tok(a.shape[-1]),
                  pl.BlockSpec(w.shape, lambda bi, i: (0, 0)),
                  pl.BlockSpec((1, 1, d), lambda bi, i: (bi, 0, 0))],
        out_specs=tok(d),
        compiler_params=_cparams(("parallel", "parallel")),
        name="resid_matmul",
    )(x, a, w, gate)


def _ffn_body(x_ref, g_ref, sh_ref, sc_ref, gt_ref, w1_ref, w3_ref, w2_ref, o_ref, h_scr, acc_scr):
    j = pl.program_id(2)

    @pl.when(j == 0)
    def _():
        h_scr[...] = _norm_mod(x_ref[0], g_ref[...], sh_ref[0], sc_ref[0]).astype(BF16)
        acc_scr[...] = jnp.zeros_like(acc_scr)

    h = h_scr[...]
    a = _dot(h, w1_ref[...])
    u = (a * jax.nn.sigmoid(a) * _dot(h, w3_ref[...])).astype(BF16)
    acc_scr[...] += _dot(u, w2_ref[...])

    @pl.when(j == pl.num_programs(2) - 1)
    def _():
        o_ref[0] = x_ref[0] + gt_ref[0] * acc_scr[...]


def _ffn(x, g, shift, scale, gate, w1, w3, w2, tm=512, fb=1408):
    b, s, d = x.shape
    f = w1.shape[1]
    tok = pl.BlockSpec((1, tm, d), lambda bi, i, j: (bi, i, 0))
    per_b = pl.BlockSpec((1, 1, d), lambda bi, i, j: (bi, 0, 0))
    return pl.pallas_call(
        _ffn_body,
        out_shape=jax.ShapeDtypeStruct(x.shape, F32),
        grid=(b, s // tm, f // fb),
        in_specs=[tok, pl.BlockSpec(g.shape, lambda bi, i, j: (0, 0)), per_b, per_b, per_b,
                  pl.BlockSpec((d, fb), lambda bi, i, j: (0, j)),
                  pl.BlockSpec((d, fb), lambda bi, i, j: (0, j)),
                  pl.BlockSpec((fb, d), lambda bi, i, j: (j, 0))],
        out_specs=tok,
        scratch_shapes=[pltpu.VMEM((tm, d), BF16), pltpu.VMEM((tm, d), F32)],
        compiler_params=_cparams(("parallel", "parallel", "arbitrary")),
        name="ffn",
    )(x, g, shift, scale, gate, w1, w3, w2)


def _fm_constants(cg):
    j = np.arange(cg)[:, None]
    m = np.arange(cg)[None, :]
    ph = 2.0 * np.pi * j * m / cg
    w_cs = np.concatenate([np.cos(ph), np.sin(ph)], axis=1)
    d = np.arange(FM_A)[:, None]
    a = np.arange(FM_A)[None, :]
    ph = 2.0 * np.pi * d * a / FM_A
    fr, fi = np.cos(ph), -np.sin(ph)
    m1 = np.block([[fr, fi], [fi, -fr]])
    n = FM_A * FM_A
    dd = np.arange(FM_A)[:, None, None]
    c = np.arange(FM_A)[None, :, None]
    b = np.arange(FM_A)[None, None, :]
    th = 2.0 * np.pi * (b * c / FM_A + b * dd / n)
    gcat = np.concatenate([np.cos(th), np.sin(th)], axis=2)
    return _mxu_const(w_cs), _mxu_const(m1), _mxu_const(gcat)


def _fm_chan_body(x_ref, g_ref, sh_ref, sc_ref, w_ref, o_ref, *, cg):
    h = _norm_mod(x_ref[0], g_ref[...], sh_ref[0], sc_ref[0]).astype(BF16)
    for grp in range(h.shape[-1] // cg):
        pq = _dot(h[:, grp * cg:(grp + 1) * cg], w_ref[...])
        o_ref[0, 0, :, grp * cg:(grp + 1) * cg] = pq[:, 0:cg].astype(BF16)
        o_ref[0, 1, :, grp * cg:(grp + 1) * cg] = pq[:, cg:2 * cg].astype(BF16)


def _fm_chan(x, g, shift, scale, w_cs, tm=512):
    b, s, d = x.shape
    cg = w_cs.shape[0]
    per_b = pl.BlockSpec((1, 1, d), lambda bi, i: (bi, 0, 0))
    return pl.pallas_call(
        functools.partial(_fm_chan_body, cg=cg),
        out_shape=jax.ShapeDtypeStruct((b, 2, s, d), BF16),
        grid=(b, s // tm),
        in_specs=[pl.BlockSpec((1, tm, d), lambda bi, i: (bi, i, 0)),
                  pl.BlockSpec(g.shape, lambda bi, i: (0, 0)), per_b, per_b,
                  pl.BlockSpec(w_cs.shape, lambda bi, i: (0, 0))],
        out_specs=pl.BlockSpec((1, 2, tm, d), lambda bi, i: (bi, 0, i, 0)),
        compiler_params=_cparams(("parallel", "parallel")),
        name="fm_chan",
    )(x, g, shift, scale, w_cs)


def _fm_s1_body(m_ref, pq_ref, o_ref):
    o_ref[0] = _dot(m_ref[...], pq_ref[0]).astype(BF16)


def _fm_stage1(pq, m1, cb=8192):
    b, _, s, d = pq.shape
    cols = s * d // FM_A
    blk = pl.BlockSpec((1, 2 * FM_A, cb), lambda bi, j: (bi, 0, j))
    return pl.pallas_call(
        _fm_s1_body,
        out_shape=jax.ShapeDtypeStruct((b, 2 * FM_A, cols), BF16),
        grid=(b, cols // cb),
        in_specs=[pl.BlockSpec(m1.shape, lambda bi, j: (0, 0)), blk],
        out_specs=blk,
        compiler_params=_cparams(("parallel", "parallel")),
        name="fm_stage1",
    )(m1, pq.reshape(b, 2 * FM_A, cols))


def _fm_s2_body(s_ref, g_ref, o_ref, *, dblk, scale):
    for i in range(dblk):
        s2 = jnp.concatenate([s_ref[0, 0, i], s_ref[0, 1, i]], axis=0)
        o_ref[:, i, :] = _dot(g_ref[i], s2) * scale


def _fm_stage2(s1, gcat, seq, d, dblk=8):
    b = s1.shape[0]
    sv = s1.reshape(b, 2, FM_A, FM_A, d)
    scale = 1.0 / math.sqrt(seq * (d // F_GROUPS))
    out = pl.pallas_call(
        functools.partial(_fm_s2_body, dblk=dblk, scale=scale),
        out_shape=jax.ShapeDtypeStruct((b * FM_A, dblk * (FM_A // dblk), d), F32),
        grid=(b, FM_A // dblk),
        in_specs=[pl.BlockSpec((1, 2, dblk, FM_A, d), lambda bi, j: (bi, 0, j, 0, 0)),
                  pl.BlockSpec((dblk, FM_A, 2 * FM_A), lambda bi, j: (j, 0, 0))],
        out_specs=pl.BlockSpec((FM_A, dblk, d), lambda bi, j: (bi, j, 0)),
        compiler_params=_cparams(("parallel", "parallel")),
        name="fm_stage2",
    )(sv, gcat)
    return out.reshape(b, seq, d)


def _fourier_mix(x, g, shift, scale):
    b, s, d = x.shape
    assert s == FM_A * FM_A
    w_cs, m1, gcat = _fm_constants(d // F_GROUPS)
    pq = _fm_chan(x, g, shift, scale, w_cs)
    return _fm_stage2(_fm_stage1(pq, m1), gcat, s, d)


LANES = 128
MOE_TM = 1024
DMA_WINDOW = 32


def _router_body(x_ref, g_ref, sh_ref, sc_ref, wr_ref, br_ref, h_ref, meta_ref, gw_ref, cnt_ref, carry):
    i = pl.program_id(0)

    @pl.when(i == 0)
    def _():
        carry[...] = jnp.zeros_like(carry)

    h = _norm_mod(x_ref[...], g_ref[...], sh_ref[0], sc_ref[0])
    for sl in range(h.shape[-1] // LANES):
        h_ref[:, sl, :] = h[:, sl * LANES:(sl + 1) * LANES]
    logits = jnp.dot(h, wr_ref[...], precision=HIGHEST, preferred_element_type=F32) + br_ref[...]
    lane = lax.broadcasted_iota(jnp.int32, logits.shape, 1)
    nl = logits.shape[-1]
    m1 = jnp.max(logits, axis=-1, keepdims=True)
    i1 = jnp.min(jnp.where(logits == m1, lane, nl), axis=-1, keepdims=True)
    rest = jnp.where(lane == i1, -3.0e38, logits)
    m2 = jnp.max(rest, axis=-1, keepdims=True)
    i2 = jnp.min(jnp.where(rest == m2, lane, nl), axis=-1, keepdims=True)
    e = jnp.exp(m2 - m1)
    gw_ref[...] = jnp.where(lane == 0, 1.0 / (1.0 + e), jnp.where(lane == 1, e / (1.0 + e), 0.0))
    onehot = jnp.where((lane == i1) | (lane == i2), 1.0, 0.0)
    tm = onehot.shape[0]
    earlier = lax.broadcasted_iota(jnp.int32, (tm, tm), 0) > lax.broadcasted_iota(jnp.int32, (tm, tm), 1)
    excl = _dot(jnp.where(earlier, 1.0, 0.0).astype(BF16), onehot.astype(BF16)) + carry[...]
    r1 = jnp.sum(jnp.where(lane == i1, excl, 0.0), axis=-1, keepdims=True).astype(jnp.int32)
    r2 = jnp.sum(jnp.where(lane == i2, excl, 0.0), axis=-1, keepdims=True).astype(jnp.int32)
    meta_ref[...] = jnp.where(lane == 0, i1, jnp.where(lane == 1, i2, jnp.where(lane == 2, r1, jnp.where(lane == 3, r2, 0))))
    carry[...] = carry[...] + jnp.sum(onehot, axis=0, keepdims=True)
    cnt_ref[...] = carry[...]


def _router(x, g, shift, scale, w_router, b_router, tm=512):
    b, s, d = x.shape
    t = b * s
    ne = w_router.shape[1]
    wr = jnp.pad(w_router.astype(F32), ((0, 0), (0, LANES - ne)))
    br = jnp.pad(b_router.astype(F32).reshape(1, ne), ((0, 0), (0, LANES - ne)), constant_values=NEG_INF)
    spt = s // tm
    per_b = pl.BlockSpec((1, 1, d), lambda i: (i // spt, 0, 0))
    const = lambda a: pl.BlockSpec(a.shape, lambda i: (0, 0))
    return pl.pallas_call(
        _router_body,
        out_shape=[jax.ShapeDtypeStruct((t, d // LANES, LANES), F32),
                   jax.ShapeDtypeStruct((t, LANES), jnp.int32),
                   jax.ShapeDtypeStruct((t, LANES), F32),
                   jax.ShapeDtypeStruct((1, LANES), F32)],
        grid=(t // tm,),
        in_specs=[pl.BlockSpec((tm, d), lambda i: (i, 0)), const(g), per_b, per_b, const(wr), const(br)],
        out_specs=[pl.BlockSpec((tm, d // LANES, LANES), lambda i: (i, 0, 0)),
                   pl.BlockSpec((tm, LANES), lambda i: (i, 0)),
                   pl.BlockSpec((tm, LANES), lambda i: (i, 0)),
                   pl.BlockSpec((1, LANES), lambda i: (0, 0))],
        scratch_shapes=[pltpu.VMEM((1, LANES), F32)],
        compiler_params=_cparams(("arbitrary",)),
        name="router",
    )(x.reshape(t, d), g, shift, scale, wr, br)


def _moe_plan(meta, counts, ne, tm):
    i1, i2, r1, r2 = meta[:, 0], meta[:, 1], meta[:, 2], meta[:, 3]
    cnt = counts[0, :ne].astype(jnp.int32)
    padded = ((cnt + tm - 1) // tm) * tm
    ends = jnp.cumsum(padded)
    offs = ends - padded
    pick = lambda idx: sum(jnp.where(idx == e, offs[e], 0) for e in range(ne))
    pos = jnp.concatenate([pick(i1) + r1, pick(i2) + r2]).astype(jnp.int32)
    n_tiles = (2 * meta.shape[0]) // tm + ne
    n_used = (ends[ne - 1] // tm).astype(jnp.int32)
    tile_start = jnp.minimum(jnp.arange(n_tiles, dtype=jnp.int32), n_used - 1) * tm
    tile_expert = jnp.sum(tile_start[:, None] >= ends[None, :], axis=1).astype(jnp.int32)
    return pos, offs + cnt, padded - cnt, tile_expert, n_used.reshape(1)


def _windowed_copies(n, start_copy, wait_one, per_iter):
    def body(i, carry):
        @pl.when(i >= DMA_WINDOW)
        def _():
            for _ in range(per_iter):
                wait_one()
        start_copy(i)
        return carry

    lax.fori_loop(0, n, body, 0)

    def drain(i, carry):
        for _ in range(per_iter):
            wait_one()
        return carry

    lax.fori_loop(0, jnp.minimum(n, DMA_WINDOW), drain, 0)


def _dispatch_body(pos_ref, pad_start_ref, pad_n_ref, h_hbm, xs_hbm, sem, *, n_tok, ne):
    copy = lambda src, dst: pltpu.make_async_copy(h_hbm.at[src], xs_hbm.at[dst], sem)
    wait_one = lambda: copy(0, 0).wait()

    def start_token(t):
        copy(t, pos_ref[t]).start()
        copy(t, pos_ref[n_tok + t]).start()

    _windowed_copies(n_tok, start_token, wait_one, 2)
    for e in range(ne):
        first = pad_start_ref[e]
        _windowed_copies(pad_n_ref[e], lambda r: copy(0, first + r).start(), wait_one, 1)


def _moe_dispatch(h3, pos, pad_start, pad_n, n_rows):
    n_tok = h3.shape[0]
    ne = pad_start.shape[0]
    return pl.pallas_call(
        functools.partial(_dispatch_body, n_tok=n_tok, ne=ne),
        out_shape=jax.ShapeDtypeStruct((n_rows,) + h3.shape[1:], h3.dtype),
        grid_spec=pltpu.PrefetchScalarGridSpec(
            num_scalar_prefetch=3, grid=(1,),
            in_specs=[pl.BlockSpec(memory_space=pl.ANY)],
            out_specs=pl.BlockSpec(memory_space=pl.ANY),
            scratch_shapes=[pltpu.SemaphoreType.DMA(())]),
        compiler_params=_cparams(("arbitrary",)),
        name="moe_dispatch",
    )(pos, pad_start, pad_n, h3)


def _combine_body(pos_ref, y_hbm, yg_hbm, sem, *, n_tok):
    copy = lambda src, dst: pltpu.make_async_copy(y_hbm.at[src], yg_hbm.at[dst], sem)
    wait_one = lambda: copy(0, 0).wait()

    def start_token(t):
        copy(pos_ref[t], t).start()
        copy(pos_ref[n_tok + t], n_tok + t).start()

    _windowed_copies(n_tok, start_token, wait_one, 2)


def _moe_combine_gather(y, pos):
    n_tok = pos.shape[0] // 2
    return pl.pallas_call(
        functools.partial(_combine_body, n_tok=n_tok),
        out_shape=jax.ShapeDtypeStruct((2 * n_tok,) + y.shape[1:], y.dtype),
        grid_spec=pltpu.PrefetchScalarGridSpec(
            num_scalar_prefetch=1, grid=(1,),
            in_specs=[pl.BlockSpec(memory_space=pl.ANY)],
            out_specs=pl.BlockSpec(memory_space=pl.ANY),
            scratch_shapes=[pltpu.SemaphoreType.DMA(())]),
        compiler_params=_cparams(("arbitrary",)),
        name="moe_combine_gather",
    )(pos, y)


def _moe_grouped_body(te_ref, nu_ref, xs_ref, w1_ref, w3_ref, w2_ref, y_ref, xb_scr, acc_scr):
    i = pl.program_id(0)
    j = pl.program_id(1)
    nsl = xs_ref.shape[1]

    @pl.when(i < nu_ref[0])
    def _():
        @pl.when(j == 0)
        def _():
            xb_scr[...] = jnp.concatenate([xs_ref[:, sl, :] for sl in range(nsl)], axis=1).astype(BF16)
            acc_scr[...] = jnp.zeros_like(acc_scr)

        h = xb_scr[...]
        a = _dot(h, w1_ref[0])
        u = (a * jax.nn.sigmoid(a) * _dot(h, w3_ref[0])).astype(BF16)
        acc_scr[...] += _dot(u, w2_ref[0])

        @pl.when(j == pl.num_programs(1) - 1)
        def _():
            for sl in range(nsl):
                y_ref[:, sl, :] = acc_scr[:, sl * LANES:(sl + 1) * LANES]


def _moe_grouped(xs, tile_expert, n_used, w1, w3, w2, tm, fb=512):
    n_rows, nsl, _ = xs.shape
    ne, d, f = w1.shape
    nj = f // fb
    row_tile = lambda i, j, te, nu: (jnp.minimum(i, nu[0] - 1), 0, 0)
    jj = lambda i, j, nu: jnp.where(i < nu[0], j, nj - 1)
    return pl.pallas_call(
        _moe_grouped_body,
        out_shape=jax.ShapeDtypeStruct(xs.shape, F32),
        grid_spec=pltpu.PrefetchScalarGridSpec(
            num_scalar_prefetch=2, grid=(n_rows // tm, nj),
            in_specs=[pl.BlockSpec((tm, nsl, LANES), row_tile),
                      pl.BlockSpec((1, d, fb), lambda i, j, te, nu: (te[i], 0, jj(i, j, nu))),
                      pl.BlockSpec((1, d, fb), lambda i, j, te, nu: (te[i], 0, jj(i, j, nu))),
                      pl.BlockSpec((1, fb, d), lambda i, j, te, nu: (te[i], jj(i, j, nu), 0))],
            out_specs=pl.BlockSpec((tm, nsl, LANES), row_tile),
            scratch_shapes=[pltpu.VMEM((tm, d), BF16), pltpu.VMEM((tm, d), F32)]),
        compiler_params=_cparams(("arbitrary", "arbitrary")),
        name="moe_grouped",
    )(tile_expert, n_used, xs, w1, w3, w2)


def _moe_final_body(x_ref, yg_ref, gw_ref, gt_ref, fg_ref, o_ref):
    gw = gw_ref[...]
    w1, w2 = gw[:, 0:1], gw[:, 1:2]
    y = jnp.concatenate([w1 * yg_ref[0, :, sl, :] + w2 * yg_ref[1, :, sl, :]
                         for sl in range(yg_ref.shape[2])], axis=1)
    xo = x_ref[0] + gt_ref[0] * y
    ms = jnp.mean(xo * xo, axis=-1, keepdims=True)
    o_ref[0] = xo * lax.rsqrt(ms + EPS) * fg_ref[...]


def _moe_final(x, yg, gw, gt, final_g, tm=512):
    b, s, d = x.shape
    spt = s // tm
    nsl = d // LANES
    return pl.pallas_call(
        _moe_final_body,
        out_shape=jax.ShapeDtypeStruct(x.shape, F32),
        grid=(b, spt),
        in_specs=[pl.BlockSpec((1, tm, d), lambda bi, i: (bi, i, 0)),
                  pl.BlockSpec((2, tm, nsl, LANES), lambda bi, i: (0, bi * spt + i, 0, 0)),
                  pl.BlockSpec((tm, LANES), lambda bi, i: (bi * spt + i, 0)),
                  pl.BlockSpec((1, 1, d), lambda bi, i: (bi, 0, 0)),
                  pl.BlockSpec(final_g.shape, lambda bi, i: (0, 0))],
        out_specs=pl.BlockSpec((1, tm, d), lambda bi, i: (bi, i, 0)),
        compiler_params=_cparams(("parallel", "parallel")),
        name="moe_final",
    )(x, yg.reshape(2, b * s, nsl, LANES), gw, gt, final_g)


def _moe_routed(x, g, shift, scale, gt, final_g, w_router, b_router, w1, w3, w2):
    ne = w1.shape[0]
    tm = MOE_TM
    h3, meta, gw, counts = _router(x, g, shift, scale, w_router, b_router)
    pos, pad_start, pad_n, tile_expert, n_used = _moe_plan(meta, counts, ne, tm)
    n_rows = (2 * h3.shape[0] // tm + ne) * tm
    xs = _moe_dispatch(h3, pos, pad_start, pad_n, n_rows)
    y = _moe_grouped(xs, tile_expert, n_used, w1, w3, w2, tm)
    yg = _moe_combine_gather(y, pos)
    return _moe_final(x, yg, gw, gt, final_g)


def kernel(x, c, ctx, c_ctx, w_ada, b_ada, norm_g, w_in, hy_short_w, hy_short_b, hy_f_w1, hy_f_b1, hy_f_w2, hy_f_b2, hy_f_w3, hy_f_freq, hy_skip, na_rpb, w_mix_out, ffn_w1, ffn_w3, ffn_w2, w_fourier, w_router, b_router, moe_w1, moe_w3, moe_w2, final_g):
    b, s, d = x.shape
    depth = w_ada.shape[0]
    assert depth == 2, "layer 0 mixes with Hyena/attention, layer 1 with Fourier/MoE"
    c_hy = hy_skip.shape[-1]
    c_na = d - c_hy

    cvec = jnp.concatenate([c, c_ctx[None, :], jnp.zeros((8 - b - 1, d), F32)], axis=0)
    mods = _ada(cvec, w_ada, b_ada)

    def mod(layer, idx, ctx_row=False):
        m = mods[layer, :, idx * d:(idx + 1) * d]
        return m[b:b + 1, None, :] if ctx_row else m[0:b, None, :]

    row = lambda a: a.reshape(1, -1)

    w_in0 = w_in[0].astype(BF16)
    w_hy, w_qkv = w_in0[:, 0:3 * c_hy], w_in0[:, 3 * c_hy:]
    v, x1, x2, q, k, va = _inproj(x, row(norm_g[0, 0]), mod(0, 0), mod(0, 1), w_hy, w_qkv,
                                  hy_short_w[0], row(hy_short_b[0]))
    kc, vc = _ctxkv(ctx, row(norm_g[0, 0]), mod(0, 0, True), mod(0, 1, True), w_qkv[:, c_na:])
    y_na = _natt(q, k, va, kc, vc, _na_bias_table(na_rpb[0]))
    y_hy = _hyena(v, x1, x2, hy_f_w1[0], hy_f_b1[0], hy_f_w2[0], hy_f_b2[0], hy_f_w3[0],
                  hy_f_freq[0], hy_skip[0])
    x = _mixout(x, y_hy, y_na, w_mix_out[0].astype(BF16), mod(0, 2))
    x = _ffn(x, row(norm_g[0, 1]), mod(0, 3), mod(0, 4), mod(0, 5),
             ffn_w1[0].astype(BF16), ffn_w3[0].astype(BF16), ffn_w2[0].astype(BF16))

    y_f = _fourier_mix(x, row(norm_g[1, 0]), mod(1, 0), mod(1, 1))
    x = _resid_matmul(x, y_f, w_fourier[0].astype(BF16), mod(1, 2))
    return _moe_routed(x, row(norm_g[1, 1]), mod(1, 3), mod(1, 4), mod(1, 5), row(final_g),
                       w_router[0], b_router[0],
                       moe_w1[0].astype(BF16), moe_w3[0].astype(BF16), moe_w2[0].astype(BF16))
```

```python
import functools
import math

import numpy as np
import jax
import jax.numpy as jnp
from jax import lax
from jax.experimental import pallas as pl
from jax.experimental.pallas import tpu as pltpu

F32 = jnp.float32
BF16 = jnp.bfloat16
HIGHEST = lax.Precision.HIGHEST

GRID_W = 64
NA_HEAD_DIM = 32
NA_WIN_R = 8
NA_WIN_C = 16
HYENA_EMB = 33
HYENA_BANDS = (HYENA_EMB - 1) // 2
HYENA_FAST_DECAY = 0.3
HYENA_SLOW_DECAY = 1.5
HYENA_TARGET = 1e-2
F_GROUPS = 4
N_MOD = 6
EPS = 1e-6
NEG_INF = -1e30

FFT_A = 64
FFT_R = 128
FFT_KA = FFT_A // 2 + 1
FFT_KA_PAD = 40
FM_A = 64

VMEM_LIMIT = 48 * 1024 * 1024


def _cparams(sem):
    return pltpu.CompilerParams(dimension_semantics=sem, vmem_limit_bytes=VMEM_LIMIT)


def _dot(a, b):
    return jnp.dot(a, b, preferred_element_type=F32)


def _mxu_const(m):
    return jnp.asarray(m, dtype=F32).astype(BF16)


def _norm_mod(x, g, shift, scale):
    ms = jnp.mean(x * x, axis=-1, keepdims=True)
    y = x * lax.rsqrt(ms + EPS) * g
    return y * (1.0 + scale) + shift


def _ada_body(c_ref, w_ref, b_ref, o_ref):
    cv = c_ref[...]
    s = cv * jax.nn.sigmoid(cv)
    o_ref[0] = jnp.dot(s, w_ref[0], precision=HIGHEST, preferred_element_type=F32) + b_ref[0]


def _ada(cvec, w_ada, b_ada):
    depth, d, n = w_ada.shape
    rows = cvec.shape[0]
    bn = n // 4
    return pl.pallas_call(
        _ada_body,
        out_shape=jax.ShapeDtypeStruct((depth, rows, n), F32),
        grid=(depth, n // bn),
        in_specs=[pl.BlockSpec((rows, d), lambda l, j: (0, 0)),
                  pl.BlockSpec((1, d, bn), lambda l, j: (l, 0, j)),
                  pl.BlockSpec((1, 1, bn), lambda l, j: (l, 0, j))],
        out_specs=pl.BlockSpec((1, rows, bn), lambda l, j: (l, 0, j)),
        compiler_params=_cparams(("parallel", "parallel")),
        name="ada",
    )(cvec, w_ada, b_ada.reshape(depth, 1, n))


def _inproj_body(x_ref, xp_ref, xn_ref, g_ref, sh_ref, sc_ref, why_ref, wqkv_ref, sw_ref, sb_ref,
                 v_ref, x1_ref, x2_ref, q_ref, k_ref, va_ref, *, n_tiles, q_scale, c_hy, c_na):
    i = pl.program_id(1)
    g, sh, sc = g_ref[...], sh_ref[0], sc_ref[0]
    h = _norm_mod(x_ref[0], g, sh, sc).astype(BF16)
    zh = _dot(h, why_ref[...])
    hp = _norm_mod(xp_ref[0], g, sh, sc).astype(BF16)
    hn = _norm_mod(xn_ref[0], g, sh, sc).astype(BF16)
    zp = _dot(hp, why_ref[...])[7:8]
    zn = _dot(hn, why_ref[...])[0:1]
    zp = jnp.where(i > 0, zp, 0.0)
    zn = jnp.where(i < n_tiles - 1, zn, 0.0)
    tm = zh.shape[0]
    row = lax.broadcasted_iota(jnp.int32, zh.shape, 0)
    z_m1 = jnp.where(row == 0, zp, pltpu.roll(zh, 1, 0))
    z_p1 = jnp.where(row == tm - 1, zn, pltpu.roll(zh, tm - 1, 0))
    sw = sw_ref[...]
    zc = z_m1 * sw[0:1] + zh * sw[1:2] + z_p1 * sw[2:3] + sb_ref[...]
    v_ref[0] = zc[:, 0:c_hy].astype(BF16)
    x1_ref[0] = zc[:, c_hy:2 * c_hy].astype(BF16)
    x2_ref[0] = zc[:, 2 * c_hy:3 * c_hy].astype(BF16)
    zq = _dot(h, wqkv_ref[...])
    q_ref[0] = (zq[:, 0:c_na] * q_scale).astype(BF16)
    k_ref[0] = zq[:, c_na:2 * c_na].astype(BF16)
    va_ref[0] = zq[:, 2 * c_na:3 * c_na].astype(BF16)


def _inproj(x, g, shift, scale, w_hy, w_qkv, short_w, short_b, tm=512):
    b, s, d = x.shape
    c_hy = w_hy.shape[1] // 3
    c_na = w_qkv.shape[1] // 3
    n_tiles = s // tm
    r8 = tm // 8
    body = functools.partial(_inproj_body, n_tiles=n_tiles, q_scale=NA_HEAD_DIM ** -0.5,
                             c_hy=c_hy, c_na=c_na)
    tok = lambda c: pl.BlockSpec((1, tm, c), lambda bi, i: (bi, i, 0))
    full2 = lambda a: pl.BlockSpec(a.shape, lambda bi, i: (0, 0))
    per_b = pl.BlockSpec((1, 1, d), lambda bi, i: (bi, 0, 0))
    return pl.pallas_call(
        body,
        out_shape=[jax.ShapeDtypeStruct((b, s, c_hy), BF16)] * 3 + [jax.ShapeDtypeStruct((b, s, c_na), BF16)] * 3,
        grid=(b, n_tiles),
        in_specs=[tok(d),
                  pl.BlockSpec((1, 8, d), lambda bi, i: (bi, jnp.maximum(i * r8 - 1, 0), 0)),
                  pl.BlockSpec((1, 8, d), lambda bi, i: (bi, jnp.minimum((i + 1) * r8, s // 8 - 1), 0)),
                  full2(g), per_b, per_b, full2(w_hy), full2(w_qkv), full2(short_w), full2(short_b)],
        out_specs=[tok(c_hy)] * 3 + [tok(c_na)] * 3,
        compiler_params=_cparams(("parallel", "parallel")),
        name="inproj",
    )(x, x, x, g, shift, scale, w_hy, w_qkv, short_w, short_b)


def _ctxkv_body(x_ref, g_ref, sh_ref, sc_ref, w_ref, k_ref, v_ref, *, c_na):
    h = _norm_mod(x_ref[0], g_ref[...], sh_ref[0], sc_ref[0]).astype(BF16)
    z = _dot(h, w_ref[...])
    k_ref[0] = z[:, 0:c_na].astype(BF16)
    v_ref[0] = z[:, c_na:2 * c_na].astype(BF16)


def _ctxkv(ctx, g, shift, scale, w_kv):
    b, n, d = ctx.shape
    c_na = w_kv.shape[1] // 2
    one = pl.BlockSpec((1, 1, d), lambda bi: (0, 0, 0))
    return pl.pallas_call(
        functools.partial(_ctxkv_body, c_na=c_na),
        out_shape=[jax.ShapeDtypeStruct((b, n, c_na), BF16)] * 2,
        grid=(b,),
        in_specs=[pl.BlockSpec((1, n, d), lambda bi: (bi, 0, 0)),
                  pl.BlockSpec(g.shape, lambda bi: (0, 0)), one, one,
                  pl.BlockSpec(w_kv.shape, lambda bi: (0, 0))],
        out_specs=[pl.BlockSpec((1, n, c_na), lambda bi: (bi, 0, 0))] * 2,
        compiler_params=_cparams(("parallel",)),
        name="ctxkv",
    )(ctx, g, shift, scale, w_kv)


NA_HEADS_PER_BLK = 4


def _na_bias_body(r_ref, e_ref, ok_ref, o_ref):
    t = jnp.dot(r_ref[...], e_ref[...], precision=HIGHEST, preferred_element_type=F32)
    o_ref[...] = jnp.where(ok_ref[...] > 0.5, t, NEG_INF)


def _na_bias_table(rpb):
    w = GRID_W
    h, nr, nc = rpb.shape
    col = np.arange(w)[:, None]
    kc = np.arange(w)[None, :]
    c_start = np.clip(col - NA_WIN_C // 2, 0, w - NA_WIN_C)
    valid = ((kc >= c_start) & (kc < c_start + NA_WIN_C)).reshape(1, w * w)
    expand = (np.arange(32)[:, None, None] == (kc - col + NA_WIN_C - 1)[None]).reshape(32, w * w)
    rp = jnp.pad(rpb.astype(F32).reshape(h * nr, nc), ((0, 0), (0, 32 - nc)))
    full = lambda a: pl.BlockSpec(a.shape, lambda: (0,) * a.ndim)
    expand = jnp.asarray(expand, dtype=F32)
    ok = jnp.asarray(valid, dtype=F32)
    toep = pl.pallas_call(
        _na_bias_body,
        out_shape=jax.ShapeDtypeStruct((h * nr, w * w), F32),
        in_specs=[full(rp), full(expand), full(ok)],
        out_specs=pl.BlockSpec((h * nr, w * w), lambda: (0, 0)),
        name="na_bias",
    )(rp, expand, ok)
    t2 = toep.reshape(h, nr, w, w).transpose(0, 2, 1, 3).reshape(h, w, nr * w)
    slabs = jnp.stack([t2[:, :, (NA_WIN_R - 1 - off) * w:(2 * NA_WIN_R - 1 - off) * w]
                       for off in range(NA_WIN_R)], axis=1)
    hpb = NA_HEADS_PER_BLK
    slabs = slabs.reshape(h // hpb, hpb, NA_WIN_R, w, NA_WIN_R * w).transpose(0, 2, 1, 3, 4)
    return slabs.reshape(h // hpb, NA_WIN_R, hpb * w, NA_WIN_R * w)


def _natt_body(q_ref, k_ref, v_ref, kc_ref, vc_ref, bias_ref, o_ref, *, rows):
    w = GRID_W
    hpb = NA_HEADS_PER_BLK
    nloc = NA_WIN_R * w
    lane = lax.broadcasted_iota(jnp.int32, (1, hpb * NA_HEAD_DIM), 1)
    in_head = [(lane >= NA_HEAD_DIM * hh) & (lane < NA_HEAD_DIM * (hh + 1)) for hh in range(hpb)]
    kcx = kc_ref[0]
    vcx = vc_ref[0]
    nt = (((1,), (1,)), ((), ()))

    def one_row(r):
        r0 = jnp.clip(r - NA_WIN_R // 2, 0, rows - NA_WIN_R)
        off = r - r0
        qs = q_ref[0, pl.ds(pl.multiple_of(r * w, w), w), :]
        kw = k_ref[0, pl.ds(pl.multiple_of(r0 * w, w), nloc), :]
        vw = v_ref[0, pl.ds(pl.multiple_of(r0 * w, w), nloc), :]
        zero = jnp.zeros_like(qs)
        qst = jnp.concatenate([jnp.where(m, qs, zero) for m in in_head], axis=0)
        s_loc = lax.dot_general(qst, kw, nt, preferred_element_type=F32) + bias_ref[0, off]
        s_ctx = lax.dot_general(qst, kcx, nt, preferred_element_type=F32)
        m = jnp.maximum(jnp.max(s_loc, axis=-1, keepdims=True), jnp.max(s_ctx, axis=-1, keepdims=True))
        p_loc = jnp.exp(s_loc - m)
        p_ctx = jnp.exp(s_ctx - m)
        den = jnp.sum(p_loc, axis=-1, keepdims=True) + jnp.sum(p_ctx, axis=-1, keepdims=True)
        o = (_dot(p_loc.astype(BF16), vw) + _dot(p_ctx.astype(BF16), vcx)) * (1.0 / den)
        acc = jnp.where(in_head[0], o[0:w], 0.0)
        for hh in range(1, hpb):
            acc = acc + jnp.where(in_head[hh], o[hh * w:(hh + 1) * w], 0.0)
        o_ref[0, pl.ds(pl.multiple_of(r * w, w), w), :] = acc.astype(BF16)

    def row_pair(i, carry):
        one_row(2 * i)
        one_row(2 * i + 1)
        return carry

    lax.fori_loop(0, rows // 2, row_pair, 0)


def _natt(q, k, v, kc, vc, bias):
    b, s, c = q.shape
    nctx = kc.shape[1]
    lw = NA_HEADS_PER_BLK * NA_HEAD_DIM
    rows = s // GRID_W
    seq = pl.BlockSpec((1, s, lw), lambda bi, g: (bi, 0, g))
    cx = pl.BlockSpec((1, nctx, lw), lambda bi, g: (bi, 0, g))
    return pl.pallas_call(
        functools.partial(_natt_body, rows=rows),
        out_shape=jax.ShapeDtypeStruct((b, s, c), BF16),
        grid=(b, c // lw),
        in_specs=[seq, seq, seq, cx, cx,
                  pl.BlockSpec((1,) + bias.shape[1:], lambda bi, g: (g, 0, 0, 0))],
        out_specs=seq,
        compiler_params=_cparams(("parallel", "parallel")),
        name="natt",
    )(q, k, v, kc, vc, bias)


def _hyena_feats(seq_len):
    t = jnp.linspace(0.0, 1.0, seq_len, dtype=F32)[:, None]
    bands = jnp.linspace(1e-4, HYENA_BANDS - 1, HYENA_BANDS, dtype=F32)
    ang = (2.0 * math.pi / seq_len) * jnp.arange(seq_len, dtype=F32)[:, None] * bands[None, :]
    feats = jnp.concatenate([t, jnp.cos(ang), -jnp.sin(ang)], axis=-1)
    return jnp.pad(feats, ((0, 0), (0, 128 - HYENA_EMB)))


def _filt_body(feat_ref, w1_ref, b1_ref, w2_ref, b2_ref, w3_ref, fr_ref, dl_ref, o_ref, l1_ref):
    j = pl.program_id(0)
    hp = functools.partial(jnp.dot, precision=HIGHEST, preferred_element_type=F32)
    feats = feat_ref[...]
    fr = fr_ref[...]
    h = jnp.sin(fr[0:1] * (hp(feats, w1_ref[...]) + b1_ref[...]))
    h = jnp.sin(fr[1:2] * (hp(h, w2_ref[...]) + b2_ref[...]))
    hc = hp(h, w3_ref[...])
    t = feats[:, 0:1]
    hc = hc * jnp.exp(-t * dl_ref[...])
    row = lax.broadcasted_iota(jnp.int32, hc.shape, 0)
    hc = jnp.where((row == 0) & (j % 2 == 1), 0.0, hc)
    l1_ref[0] = jnp.sum(jnp.abs(hc), axis=0, keepdims=True)
    o_ref[0] = hc.astype(BF16)


def _hyena_filter_taps(seq_len, f_w1, f_b1, f_w2, f_b2, f_w3, f_freq, c_hy):
    feats = _hyena_feats(seq_len)
    hid = f_w1.shape[1]
    w1 = jnp.pad(f_w1.astype(F32), ((0, 128 - HYENA_EMB), (0, 0)))
    deltas = jnp.abs(jnp.linspace(math.log(HYENA_TARGET) / HYENA_SLOW_DECAY,
                                  math.log(HYENA_TARGET) / HYENA_FAST_DECAY, c_hy, dtype=F32))[None, :]
    nblk = f_w3.shape[1] // c_hy
    c0 = lambda a: pl.BlockSpec(a.shape, lambda j: (0, 0))
    b1, b2 = f_b1.reshape(1, hid), f_b2.reshape(1, hid)
    return pl.pallas_call(
        _filt_body,
        out_shape=[jax.ShapeDtypeStruct((nblk, seq_len, c_hy), BF16),
                   jax.ShapeDtypeStruct((nblk, 1, c_hy), F32)],
        grid=(nblk,),
        in_specs=[c0(feats), c0(w1), c0(b1), c0(f_w2), c0(b2),
                  pl.BlockSpec((hid, c_hy), lambda j: (0, j)), c0(f_freq), c0(deltas)],
        out_specs=[pl.BlockSpec((1, seq_len, c_hy), lambda j: (j, 0, 0)),
                   pl.BlockSpec((1, 1, c_hy), lambda j: (j, 0, 0))],
        compiler_params=_cparams(("parallel",)),
        name="hyena_filter",
    )(feats, w1, b1, f_w2, b2, f_w3, f_freq, deltas)


def _conv_dft_constants():
    a_half = FFT_A // 2
    n = FFT_A * FFT_R
    ka = np.arange(FFT_KA)[:, None]
    a = np.arange(a_half)[None, :]
    ph = 2.0 * np.pi * ka * a / FFT_A
    m_fwd = np.zeros((2 * FFT_KA_PAD, a_half))
    m_fwd[:FFT_KA] = np.cos(ph)
    m_fwd[FFT_KA_PAD:FFT_KA_PAD + FFT_KA] = -np.sin(ph)
    wgt = np.where((ka == 0) | (ka == FFT_A // 2), 1.0, 2.0)
    m_inv = np.zeros((a_half, 2 * FFT_KA_PAD))
    m_inv[:, :FFT_KA] = (wgt * np.cos(ph)).T / n
    m_inv[:, FFT_KA_PAD:FFT_KA_PAD + FFT_KA] = (-wgt * np.sin(ph)).T / n
    kb = np.arange(FFT_R)[None, :, None]
    b = np.arange(FFT_R)[None, None, :]
    kaa = np.arange(FFT_KA)[:, None, None]
    th = 2.0 * np.pi * (b * kb / FFT_R + b * kaa / n)
    gr, gi = np.cos(th), -np.sin(th)
    g2 = np.zeros((FFT_KA_PAD, 2 * FFT_R, 2 * FFT_R))
    g2[:FFT_KA] = np.block([[gr, -gi], [gi, gr]])
    grt, git = gr.transpose(0, 2, 1), gi.transpose(0, 2, 1)
    g2h = np.zeros_like(g2)
    g2h[:FFT_KA] = np.block([[grt, git], [-git, grt]])
    return _mxu_const(m_fwd), _mxu_const(m_inv), _mxu_const(g2), _mxu_const(g2h)


def _fwd1_body(m_ref, u_ref, o_ref):
    res = _dot(m_ref[...], u_ref[0])
    o_ref[0, 0] = res[0:FFT_KA_PAD]
    o_ref[0, 1] = res[FFT_KA_PAD:2 * FFT_KA_PAD]


def _conv_fwd1(u, m_fwd, cb=8192):
    n, seq, c = u.shape
    a_half = FFT_A // 2
    cols = seq * c // a_half
    uv = u.reshape(n, a_half, cols)
    return pl.pallas_call(
        _fwd1_body,
        out_shape=jax.ShapeDtypeStruct((n, 2, FFT_KA_PAD, cols), F32),
        grid=(n, cols // cb),
        in_specs=[pl.BlockSpec(m_fwd.shape, lambda i, j: (0, 0)),
                  pl.BlockSpec((1, a_half, cb), lambda i, j: (i, 0, j))],
        out_specs=pl.BlockSpec((1, 2, FFT_KA_PAD, cb), lambda i, j: (i, 0, 0, j)),
        compiler_params=_cparams(("parallel", "parallel")),
        name="conv_fwd1",
    )(m_fwd, uv)


def _fwd2f_body(sf_ref, sb_ref, g_ref, l1_ref, kf_ref):
    o = pl.program_id(0)
    r2 = 2 * FFT_R
    c = sf_ref.shape[-1]
    g = g_ref[0]
    xf = _dot(g, sf_ref[0, :, 0].reshape(r2, c).astype(BF16))
    xb = _dot(g, sb_ref[0, :, 0].reshape(r2, c).astype(BF16))
    inv = 1.0 / (l1_ref[2 * o] + l1_ref[2 * o + 1] + EPS)
    kf_ref[0, 0, 0:FFT_R] = (xf[0:FFT_R] + xb[0:FFT_R]) * inv
    kf_ref[0, 0, FFT_R:r2] = (xf[FFT_R:r2] - xb[FFT_R:r2]) * inv


def _filter_spectrum(s_filt, l1, g2, c):
    n_ord = s_filt.shape[0] // 2
    sv = s_filt.reshape(2 * n_ord, 2, FFT_KA_PAD, FFT_R, c)
    r2 = 2 * FFT_R
    return pl.pallas_call(
        _fwd2f_body,
        out_shape=jax.ShapeDtypeStruct((n_ord, FFT_KA_PAD, r2, c), F32),
        grid=(n_ord, FFT_KA_PAD),
        in_specs=[pl.BlockSpec((1, 2, 1, FFT_R, c), lambda o, ka: (2 * o, 0, ka, 0, 0)),
                  pl.BlockSpec((1, 2, 1, FFT_R, c), lambda o, ka: (2 * o + 1, 0, ka, 0, 0)),
                  pl.BlockSpec((1, r2, r2), lambda o, ka: (ka, 0, 0)),
                  pl.BlockSpec(l1.shape, lambda o, ka: (0, 0, 0))],
        out_specs=pl.BlockSpec((1, 1, r2, c), lambda o, ka: (o, ka, 0, 0)),
        compiler_params=_cparams(("parallel", "parallel")),
        name="filter_spectrum",
    )(sv, sv, g2, l1)


def _mid_body(s_ref, g_ref, gh_ref, kf_ref, t_ref):
    r2 = 2 * FFT_R
    c = s_ref.shape[-1]
    x = _dot(g_ref[0], s_ref[0, :, 0].reshape(r2, c).astype(BF16))
    xr, xi = x[0:FFT_R], x[FFT_R:r2]
    kr, ki = kf_ref[0, 0, 0:FFT_R], kf_ref[0, 0, FFT_R:r2]
    y = jnp.concatenate([xr * kr - xi * ki, xr * ki + xi * kr], axis=0).astype(BF16)
    t = _dot(gh_ref[0], y)
    t_ref[0, 0, 0] = t[0:FFT_R]
    t_ref[0, 1, 0] = t[FFT_R:r2]


def _conv_mid(s, kf, order, g2, g2h, c):
    n = s.shape[0]
    sv = s.reshape(n, 2, FFT_KA_PAD, FFT_R, c)
    r2 = 2 * FFT_R
    blk = pl.BlockSpec((1, 2, 1, FFT_R, c), lambda i, ka: (i, 0, ka, 0, 0))
    gspec = pl.BlockSpec((1, r2, r2), lambda i, ka: (ka, 0, 0))
    out = pl.pallas_call(
        _mid_body,
        out_shape=jax.ShapeDtypeStruct(sv.shape, F32),
        grid=(n, FFT_KA_PAD),
        in_specs=[blk, gspec, gspec,
                  pl.BlockSpec((1, 1, r2, c), lambda i, ka: (order, ka, 0, 0))],
        out_specs=blk,
        compiler_params=_cparams(("parallel", "parallel")),
        name="conv_mid",
    )(sv, g2, g2h, kf)
    return out.reshape(s.shape)


def _inv1_body(m_ref, t_ref, u_ref, xg_ref, sk_ref, o_ref):
    cb = t_ref.shape[-1]
    t2 = t_ref[0].reshape(2 * FFT_KA_PAD, cb).astype(BF16)
    y = _dot(m_ref[...], t2)
    u = u_ref[0].astype(F32)
    o_ref[0] = (xg_ref[0].astype(F32) * (y + u * sk_ref[...])).astype(BF16)


def _conv_inv1(t, u, xg, skip, m_inv, cb=8192):
    n, seq, c = u.shape
    a_half = FFT_A // 2
    cols = seq * c // a_half
    sk = jnp.tile(skip.astype(F32).reshape(1, c), (1, cols // c))
    uspec = pl.BlockSpec((1, a_half, cb), lambda i, j: (i, 0, j))
    out = pl.pallas_call(
        _inv1_body,
        out_shape=jax.ShapeDtypeStruct((n, a_half, cols), BF16),
        grid=(n, cols // cb),
        in_specs=[pl.BlockSpec(m_inv.shape, lambda i, j: (0, 0)),
                  pl.BlockSpec((1, 2, FFT_KA_PAD, cb), lambda i, j: (i, 0, 0, j)),
                  uspec, uspec,
                  pl.BlockSpec((1, cb), lambda i, j: (0, j))],
        out_specs=uspec,
        compiler_params=_cparams(("parallel", "parallel")),
        name="conv_inv1",
    )(m_inv, t, u.reshape(n, a_half, cols), xg.reshape(n, a_half, cols), sk)
    return out.reshape(n, seq, c)


def _hyena(v, x1, x2, f_w1, f_b1, f_w2, f_b2, f_w3, f_freq, skip):
    _, seq, c = v.shape
    assert 2 * seq == FFT_A * FFT_R
    m_fwd, m_inv, g2, g2h = _conv_dft_constants()
    taps, l1 = _hyena_filter_taps(seq, f_w1, f_b1, f_w2, f_b2, f_w3, f_freq, c)
    kf = _filter_spectrum(_conv_fwd1(taps, m_fwd), l1, g2, c)
    y = v
    for order, xg in enumerate((x1, x2)):
        t = _conv_mid(_conv_fwd1(y, m_fwd), kf, order, g2, g2h, c)
        y = _conv_inv1(t, y, xg, skip[order], m_inv)
    return y


def _mixout_body(x_ref, a1_ref, a2_ref, w_ref, gt_ref, o_ref):
    c1 = a1_ref.shape[-1]
    y = _dot(a1_ref[0], w_ref[0:c1]) + _dot(a2_ref[0], w_ref[c1:])
    o_ref[0] = x_ref[0] + gt_ref[0] * y


def _mixout(x, a1, a2, w, gate, tm=512):
    b, s, d = x.shape
    tok = lambda c: pl.BlockSpec((1, tm, c), lambda bi, i: (bi, i, 0))
    return pl.pallas_call(
        _mixout_body,
        out_shape=jax.ShapeDtypeStruct(x.shape, F32),
        grid=(b, s // tm),
        in_specs=[tok(d), tok(a1.shape[-1]), tok(a2.shape[-1]),
                  pl.BlockSpec(w.shape, lambda bi, i: (0, 0)),
                  pl.BlockSpec((1, 1, d), lambda bi, i: (bi, 0, 0))],
        out_specs=tok(d),
        compiler_params=_cparams(("parallel", "parallel")),
        name="mixout",
    )(x, a1, a2, w, gate)


def _resid_matmul_body(x_ref, a_ref, w_ref, gt_ref, o_ref):
    o_ref[0] = x_ref[0] + gt_ref[0] * _dot(a_ref[0].astype(BF16), w_ref[...])


def _resid_matmul(x, a, w, gate, tm=512):
    b, s, d = x.shape
    tok = lambda c: pl.BlockSpec((1, tm, c), lambda bi, i: (bi, i, 0))
    return pl.pallas_call(
        _resid_matmul_body,
        out_shape=jax.ShapeDtypeStruct(x.shape, F32),
        grid=(b, s // tm),
        in_specs=[tok(d), tok(a.shape[-1]),
                  pl.BlockSpec(w.shape, lambda bi, i: (0, 0)),
                  pl.BlockSpec((1, 1, d), lambda bi, i: (bi, 0, 0))],
        out_specs=tok(d),
        compiler_params=_cparams(("parallel", "parallel")),
        name="resid_matmul",
    )(x, a, w, gate)


def _ffn_body(x_ref, g_ref, sh_ref, sc_ref, gt_ref, w1_ref, w3_ref, w2_ref, o_ref, h_scr, acc_scr):
    j = pl.program_id(2)

    @pl.when(j == 0)
    def _():
        h_scr[...] = _norm_mod(x_ref[0], g_ref[...], sh_ref[0], sc_ref[0]).astype(BF16)
        acc_scr[...] = jnp.zeros_like(acc_scr)

    h = h_scr[...]
    a = _dot(h, w1_ref[...])
    u = (a * jax.nn.sigmoid(a) * _dot(h, w3_ref[...])).astype(BF16)
    acc_scr[...] += _dot(u, w2_ref[...])

    @pl.when(j == pl.num_programs(2) - 1)
    def _():
        o_ref[0] = x_ref[0] + gt_ref[0] * acc_scr[...]


def _ffn(x, g, shift, scale, gate, w1, w3, w2, tm=512, fb=1408):
    b, s, d = x.shape
    f = w1.shape[1]
    tok = pl.BlockSpec((1, tm, d), lambda bi, i, j: (bi, i, 0))
    per_b = pl.BlockSpec((1, 1, d), lambda bi, i, j: (bi, 0, 0))
    return pl.pallas_call(
        _ffn_body,
        out_shape=jax.ShapeDtypeStruct(x.shape, F32),
        grid=(b, s // tm, f // fb),
        in_specs=[tok, pl.BlockSpec(g.shape, lambda bi, i, j: (0, 0)), per_b, per_b, per_b,
                  pl.BlockSpec((d, fb), lambda bi, i, j: (0, j)),
                  pl.BlockSpec((d, fb), lambda bi, i, j: (0, j)),
                  pl.BlockSpec((fb, d), lambda bi, i, j: (j, 0))],
        out_specs=tok,
        scratch_shapes=[pltpu.VMEM((tm, d), BF16), pltpu.VMEM((tm, d), F32)],
        compiler_params=_cparams(("parallel", "parallel", "arbitrary")),
        name="ffn",
    )(x, g, shift, scale, gate, w1, w3, w2)


def _fm_constants(cg):
    j = np.arange(cg)[:, None]
    m = np.arange(cg)[None, :]
    ph = 2.0 * np.pi * j * m / cg
    w_cs = np.concatenate([np.cos(ph), np.sin(ph)], axis=1)
    d = np.arange(FM_A)[:, None]
    a = np.arange(FM_A)[None, :]
    ph = 2.0 * np.pi * d * a / FM_A
    fr, fi = np.cos(ph), -np.sin(ph)
    m1 = np.block([[fr, fi], [fi, -fr]])
    n = FM_A * FM_A
    dd = np.arange(FM_A)[:, None, None]
    c = np.arange(FM_A)[None, :, None]
    b = np.arange(FM_A)[None, None, :]
    th = 2.0 * np.pi * (b * c / FM_A + b * dd / n)
    gcat = np.concatenate([np.cos(th), np.sin(th)], axis=2)
    return _mxu_const(w_cs), _mxu_const(m1), _mxu_const(gcat)


def _fm_chan_body(x_ref, g_ref, sh_ref, sc_ref, w_ref, o_ref, *, cg):
    h = _norm_mod(x_ref[0], g_ref[...], sh_ref[0], sc_ref[0]).astype(BF16)
    for grp in range(h.shape[-1] // cg):
        pq = _dot(h[:, grp * cg:(grp + 1) * cg], w_ref[...])
        o_ref[0, 0, :, grp * cg:(grp + 1) * cg] = pq[:, 0:cg].astype(BF16)
        o_ref[0, 1, :, grp * cg:(grp + 1) * cg] = pq[:, cg:2 * cg].astype(BF16)


def _fm_chan(x, g, shift, scale, w_cs, tm=512):
    b, s, d = x.shape
    cg = w_cs.shape[0]
    per_b = pl.BlockSpec((1, 1, d), lambda bi, i: (bi, 0, 0))
    return pl.pallas_call(
        functools.partial(_fm_chan_body, cg=cg),
        out_shape=jax.ShapeDtypeStruct((b, 2, s, d), BF16),
        grid=(b, s // tm),
        in_specs=[pl.BlockSpec((1, tm, d), lambda bi, i: (bi, i, 0)),
                  pl.BlockSpec(g.shape, lambda bi, i: (0, 0)), per_b, per_b,
                  pl.BlockSpec(w_cs.shape, lambda bi, i: (0, 0))],
        out_specs=pl.BlockSpec((1, 2, tm, d), lambda bi, i: (bi, 0, i, 0)),
        compiler_params=_cparams(("parallel", "parallel")),
        name="fm_chan",
    )(x, g, shift, scale, w_cs)


def _fm_s1_body(m_ref, pq_ref, o_ref):
    o_ref[0] = _dot(m_ref[...], pq_ref[0]).astype(BF16)


def _fm_stage1(pq, m1, cb=8192):
    b, _, s, d = pq.shape
    cols = s * d // FM_A
    blk = pl.BlockSpec((1, 2 * FM_A, cb), lambda bi, j: (bi, 0, j))
    return pl.pallas_call(
        _fm_s1_body,
        out_shape=jax.ShapeDtypeStruct((b, 2 * FM_A, cols), BF16),
        grid=(b, cols // cb),
        in_specs=[pl.BlockSpec(m1.shape, lambda bi, j: (0, 0)), blk],
        out_specs=blk,
        compiler_params=_cparams(("parallel", "parallel")),
        name="fm_stage1",
    )(m1, pq.reshape(b, 2 * FM_A, cols))


def _fm_s2_body(s_ref, g_ref, o_ref, *, dblk, scale):
    for i in range(dblk):
        s2 = jnp.concatenate([s_ref[0, 0, i], s_ref[0, 1, i]], axis=0)
        o_ref[:, i, :] = _dot(g_ref[i], s2) * scale


def _fm_stage2(s1, gcat, seq, d, dblk=8):
    b = s1.shape[0]
    sv = s1.reshape(b, 2, FM_A, FM_A, d)
    scale = 1.0 / math.sqrt(seq * (d // F_GROUPS))
    out = pl.pallas_call(
        functools.partial(_fm_s2_body, dblk=dblk, scale=scale),
        out_shape=jax.ShapeDtypeStruct((b * FM_A, dblk * (FM_A // dblk), d), F32),
        grid=(b, FM_A // dblk),
        in_specs=[pl.BlockSpec((1, 2, dblk, FM_A, d), lambda bi, j: (bi, 0, j, 0, 0)),
                  pl.BlockSpec((dblk, FM_A, 2 * FM_A), lambda bi, j: (j, 0, 0))],
        out_specs=pl.BlockSpec((FM_A, dblk, d), lambda bi, j: (bi, j, 0)),
        compiler_params=_cparams(("parallel", "parallel")),
        name="fm_stage2",
    )(sv, gcat)
    return out.reshape(b, seq, d)


def _fourier_mix(x, g, shift, scale):
    b, s, d = x.shape
    assert s == FM_A * FM_A
    w_cs, m1, gcat = _fm_constants(d // F_GROUPS)
    pq = _fm_chan(x, g, shift, scale, w_cs)
    return _fm_stage2(_fm_stage1(pq, m1), gcat, s, d)


LANES = 128
MOE_TM = 1024
DMA_WINDOW = 128


def _router_body(x_ref, g_ref, sh_ref, sc_ref, wr_ref, br_ref, h_ref, meta_ref, gw_ref, cnt_ref, carry):
    i = pl.program_id(0)

    @pl.when(i == 0)
    def _():
        carry[...] = jnp.zeros_like(carry)

    h = _norm_mod(x_ref[...], g_ref[...], sh_ref[0], sc_ref[0])
    for sl in range(h.shape[-1] // LANES):
        h_ref[:, sl, :] = h[:, sl * LANES:(sl + 1) * LANES]
    logits = jnp.dot(h, wr_ref[...], precision=HIGHEST, preferred_element_type=F32) + br_ref[...]
    lane = lax.broadcasted_iota(jnp.int32, logits.shape, 1)
    nl = logits.shape[-1]
    m1 = jnp.max(logits, axis=-1, keepdims=True)
    i1 = jnp.min(jnp.where(logits == m1, lane, nl), axis=-1, keepdims=True)
    rest = jnp.where(lane == i1, -3.0e38, logits)
    m2 = jnp.max(rest, axis=-1, keepdims=True)
    i2 = jnp.min(jnp.where(rest == m2, lane, nl), axis=-1, keepdims=True)
    e = jnp.exp(m2 - m1)
    gw_ref[...] = jnp.where(lane == 0, 1.0 / (1.0 + e), jnp.where(lane == 1, e / (1.0 + e), 0.0))
    onehot = jnp.where((lane == i1) | (lane == i2), 1.0, 0.0)
    tm = onehot.shape[0]
    earlier = lax.broadcasted_iota(jnp.int32, (tm, tm), 0) > lax.broadcasted_iota(jnp.int32, (tm, tm), 1)
    excl = _dot(jnp.where(earlier, 1.0, 0.0).astype(BF16), onehot.astype(BF16)) + carry[...]
    r1 = jnp.sum(jnp.where(lane == i1, excl, 0.0), axis=-1, keepdims=True).astype(jnp.int32)
    r2 = jnp.sum(jnp.where(lane == i2, excl, 0.0), axis=-1, keepdims=True).astype(jnp.int32)
    meta_ref[...] = jnp.where(lane == 0, i1, jnp.where(lane == 1, i2, jnp.where(lane == 2, r1, jnp.where(lane == 3, r2, 0))))
    carry[...] = carry[...] + jnp.sum(onehot, axis=0, keepdims=True)
    cnt_ref[...] = carry[...]


def _router(x, g, shift, scale, w_router, b_router, tm=512):
    b, s, d = x.shape
    t = b * s
    ne = w_router.shape[1]
    wr = jnp.pad(w_router.astype(F32), ((0, 0), (0, LANES - ne)))
    br = jnp.pad(b_router.astype(F32).reshape(1, ne), ((0, 0), (0, LANES - ne)), constant_values=NEG_INF)
    spt = s // tm
    per_b = pl.BlockSpec((1, 1, d), lambda i: (i // spt, 0, 0))
    const = lambda a: pl.BlockSpec(a.shape, lambda i: (0, 0))
    return pl.pallas_call(
        _router_body,
        out_shape=[jax.ShapeDtypeStruct((t, d // LANES, LANES), F32),
                   jax.ShapeDtypeStruct((t, LANES), jnp.int32),
                   jax.ShapeDtypeStruct((t, LANES), F32),
                   jax.ShapeDtypeStruct((1, LANES), F32)],
        grid=(t // tm,),
        in_specs=[pl.BlockSpec((tm, d), lambda i: (i, 0)), const(g), per_b, per_b, const(wr), const(br)],
        out_specs=[pl.BlockSpec((tm, d // LANES, LANES), lambda i: (i, 0, 0)),
                   pl.BlockSpec((tm, LANES), lambda i: (i, 0)),
                   pl.BlockSpec((tm, LANES), lambda i: (i, 0)),
                   pl.BlockSpec((1, LANES), lambda i: (0, 0))],
        scratch_shapes=[pltpu.VMEM((1, LANES), F32)],
        compiler_params=_cparams(("arbitrary",)),
        name="router",
    )(x.reshape(t, d), g, shift, scale, wr, br)


def _moe_plan(meta, counts, ne, tm):
    i1, i2, r1, r2 = meta[:, 0], meta[:, 1], meta[:, 2], meta[:, 3]
    cnt = counts[0, :ne].astype(jnp.int32)
    padded = ((cnt + tm - 1) // tm) * tm
    ends = jnp.cumsum(padded)
    offs = ends - padded
    pick = lambda idx: sum(jnp.where(idx == e, offs[e], 0) for e in range(ne))
    pos = jnp.concatenate([pick(i1) + r1, pick(i2) + r2]).astype(jnp.int32)
    n_tiles = (2 * meta.shape[0]) // tm + ne
    n_used = (ends[ne - 1] // tm).astype(jnp.int32)
    tile_start = jnp.minimum(jnp.arange(n_tiles, dtype=jnp.int32), n_used - 1) * tm
    tile_expert = jnp.sum(tile_start[:, None] >= ends[None, :], axis=1).astype(jnp.int32)
    return pos, offs + cnt, padded - cnt, tile_expert, n_used.reshape(1)


def _windowed_copies(n, start_copy, wait_one, per_iter):
    def body(i, carry):
        @pl.when(i >= DMA_WINDOW)
        def _():
            for _ in range(per_iter):
                wait_one()
        start_copy(i)
        return carry

    lax.fori_loop(0, n, body, 0)

    def drain(i, carry):
        for _ in range(per_iter):
            wait_one()
        return carry

    lax.fori_loop(0, jnp.minimum(n, DMA_WINDOW), drain, 0)


def _dispatch_body(pos_ref, pad_start_ref, pad_n_ref, h_ref, xs_hbm, sem, *, n_tok, ne):
    i = pl.program_id(0)
    td = h_ref.shape[0]
    base = i * td
    copy = lambda src, dst: pltpu.make_async_copy(h_ref.at[src], xs_hbm.at[dst], sem)
    wait_one = lambda: copy(0, 0).wait()

    def start_token(r):
        copy(r, pos_ref[base + r]).start()
        copy(r, pos_ref[n_tok + base + r]).start()

    _windowed_copies(td, start_token, wait_one, 2)

    @pl.when(i == 0)
    def _():
        for e in range(ne):
            first = pad_start_ref[e]
            _windowed_copies(pad_n_ref[e], lambda r: copy(0, first + r).start(), wait_one, 1)


def _moe_dispatch(h3, pos, pad_start, pad_n, n_rows, td=512):
    n_tok = h3.shape[0]
    ne = pad_start.shape[0]
    return pl.pallas_call(
        functools.partial(_dispatch_body, n_tok=n_tok, ne=ne),
        out_shape=jax.ShapeDtypeStruct((n_rows,) + h3.shape[1:], h3.dtype),
        grid_spec=pltpu.PrefetchScalarGridSpec(
            num_scalar_prefetch=3, grid=(n_tok // td,),
            in_specs=[pl.BlockSpec((td,) + h3.shape[1:], lambda i, p, ps, pn: (i, 0, 0))],
            out_specs=pl.BlockSpec(memory_space=pl.ANY),
            scratch_shapes=[pltpu.SemaphoreType.DMA(())]),
        compiler_params=_cparams(("arbitrary",)),
        name="moe_dispatch",
    )(pos, pad_start, pad_n, h3)


def _moe_grouped_body(te_ref, nu_ref, xs_ref, w1_ref, w3_ref, w2_ref, y_ref, xb_scr, acc_scr):
    i = pl.program_id(0)
    j = pl.program_id(1)
    nsl = xs_ref.shape[1]

    @pl.when(i < nu_ref[0])
    def _():
        @pl.when(j == 0)
        def _():
            xb_scr[...] = jnp.concatenate([xs_ref[:, sl, :] for sl in range(nsl)], axis=1).astype(BF16)
            acc_scr[...] = jnp.zeros_like(acc_scr)

        h = xb_scr[...]
        a = _dot(h, w1_ref[0])
        u = (a * jax.nn.sigmoid(a) * _dot(h, w3_ref[0])).astype(BF16)
        acc_scr[...] += _dot(u, w2_ref[0])

        @pl.when(j == pl.num_programs(1) - 1)
        def _():
            for sl in range(nsl):
                y_ref[:, sl, :] = acc_scr[:, sl * LANES:(sl + 1) * LANES]


def _moe_grouped(xs, tile_expert, n_used, w1, w3, w2, tm, fb=512):
    n_rows, nsl, _ = xs.shape
    ne, d, f = w1.shape
    nj = f // fb
    row_tile = lambda i, j, te, nu: (jnp.minimum(i, nu[0] - 1), 0, 0)
    jj = lambda i, j, nu: jnp.where(i < nu[0], j, nj - 1)
    return pl.pallas_call(
        _moe_grouped_body,
        out_shape=jax.ShapeDtypeStruct(xs.shape, F32),
        grid_spec=pltpu.PrefetchScalarGridSpec(
            num_scalar_prefetch=2, grid=(n_rows // tm, nj),
            in_specs=[pl.BlockSpec((tm, nsl, LANES), row_tile),
                      pl.BlockSpec((1, d, fb), lambda i, j, te, nu: (te[i], 0, jj(i, j, nu))),
                      pl.BlockSpec((1, d, fb), lambda i, j, te, nu: (te[i], 0, jj(i, j, nu))),
                      pl.BlockSpec((1, fb, d), lambda i, j, te, nu: (te[i], jj(i, j, nu), 0))],
            out_specs=pl.BlockSpec((tm, nsl, LANES), row_tile),
            scratch_shapes=[pltpu.VMEM((tm, d), BF16), pltpu.VMEM((tm, d), F32)]),
        compiler_params=_cparams(("arbitrary", "arbitrary")),
        name="moe_grouped",
    )(tile_expert, n_used, xs, w1, w3, w2)


def _moe_final_body(pos_ref, x_ref, y_hbm, gw_ref, gt_ref, fg_ref, o_ref, yg_scr, sem, *, n_tok):
    i = pl.program_id(0)
    tc = x_ref.shape[0]
    base = i * tc
    copy = lambda src, k, r: pltpu.make_async_copy(y_hbm.at[src], yg_scr.at[k, r], sem)
    wait_one = lambda: copy(0, 0, 0).wait()

    def start_token(r):
        copy(pos_ref[base + r], 0, r).start()
        copy(pos_ref[n_tok + base + r], 1, r).start()

    _windowed_copies(tc, start_token, wait_one, 2)
    gw = gw_ref[...]
    w1, w2 = gw[:, 0:1], gw[:, 1:2]
    y = jnp.concatenate([w1 * yg_scr[0, :, sl, :] + w2 * yg_scr[1, :, sl, :]
                         for sl in range(yg_scr.shape[2])], axis=1)
    xo = x_ref[...] + gt_ref[0] * y
    ms = jnp.mean(xo * xo, axis=-1, keepdims=True)
    o_ref[...] = xo * lax.rsqrt(ms + EPS) * fg_ref[...]


def _moe_final(x, y, pos, gw, gt, final_g, tc=256):
    b, s, d = x.shape
    t = b * s
    spt = s // tc
    nsl = d // LANES
    out = pl.pallas_call(
        functools.partial(_moe_final_body, n_tok=t),
        out_shape=jax.ShapeDtypeStruct((t, d), F32),
        grid_spec=pltpu.PrefetchScalarGridSpec(
            num_scalar_prefetch=1, grid=(t // tc,),
            in_specs=[pl.BlockSpec((tc, d), lambda i, p: (i, 0)),
                      pl.BlockSpec(memory_space=pl.ANY),
                      pl.BlockSpec((tc, LANES), lambda i, p: (i, 0)),
                      pl.BlockSpec((1, 1, d), lambda i, p: (i // spt, 0, 0)),
                      pl.BlockSpec(final_g.shape, lambda i, p: (0, 0))],
            out_specs=pl.BlockSpec((tc, d), lambda i, p: (i, 0)),
            scratch_shapes=[pltpu.VMEM((2, tc, nsl, LANES), F32), pltpu.SemaphoreType.DMA(())]),
        compiler_params=_cparams(("arbitrary",)),
        name="moe_final",
    )(pos, x.reshape(t, d), y, gw, gt, final_g)
    return out.reshape(b, s, d)


def _moe_routed(x, g, shift, scale, gt, final_g, w_router, b_router, w1, w3, w2):
    ne = w1.shape[0]
    tm = MOE_TM
    h3, meta, gw, counts = _router(x, g, shift, scale, w_router, b_router)
    pos, pad_start, pad_n, tile_expert, n_used = _moe_plan(meta, counts, ne, tm)
    n_rows = (2 * h3.shape[0] // tm + ne) * tm
    xs = _moe_dispatch(h3, pos, pad_start, pad_n, n_rows)
    y = _moe_grouped(xs, tile_expert, n_used, w1, w3, w2, tm)
    return _moe_final(x, y, pos, gw, gt, final_g)


def kernel(x, c, ctx, c_ctx, w_ada, b_ada, norm_g, w_in, hy_short_w, hy_short_b, hy_f_w1, hy_f_b1, hy_f_w2, hy_f_b2, hy_f_w3, hy_f_freq, hy_skip, na_rpb, w_mix_out, ffn_w1, ffn_w3, ffn_w2, w_fourier, w_router, b_router, moe_w1, moe_w3, moe_w2, final_g):
    b, s, d = x.shape
    depth = w_ada.shape[0]
    assert depth == 2, "layer 0 mixes with Hyena/attention, layer 1 with Fourier/MoE"
    c_hy = hy_skip.shape[-1]
    c_na = d - c_hy

    cvec = jnp.concatenate([c, c_ctx[None, :], jnp.zeros((8 - b - 1, d), F32)], axis=0)
    mods = _ada(cvec, w_ada, b_ada)

    def mod(layer, idx, ctx_row=False):
        m = mods[layer, :, idx * d:(idx + 1) * d]
        return m[b:b + 1, None, :] if ctx_row else m[0:b, None, :]

    row = lambda a: a.reshape(1, -1)

    w_in0 = w_in[0].astype(BF16)
    w_hy, w_qkv = w_in0[:, 0:3 * c_hy], w_in0[:, 3 * c_hy:]
    v, x1, x2, q, k, va = _inproj(x, row(norm_g[0, 0]), mod(0, 0), mod(0, 1), w_hy, w_qkv,
                                  hy_short_w[0], row(hy_short_b[0]))
    kc, vc = _ctxkv(ctx, row(norm_g[0, 0]), mod(0, 0, True), mod(0, 1, True), w_qkv[:, c_na:])
    y_na = _natt(q, k, va, kc, vc, _na_bias_table(na_rpb[0]))
    y_hy = _hyena(v, x1, x2, hy_f_w1[0], hy_f_b1[0], hy_f_w2[0], hy_f_b2[0], hy_f_w3[0],
                  hy_f_freq[0], hy_skip[0])
    x = _mixout(x, y_hy, y_na, w_mix_out[0].astype(BF16), mod(0, 2))
    x = _ffn(x, row(norm_g[0, 1]), mod(0, 3), mod(0, 4), mod(0, 5),
             ffn_w1[0].astype(BF16), ffn_w3[0].astype(BF16), ffn_w2[0].astype(BF16))

    y_f = _fourier_mix(x, row(norm_g[1, 0]), mod(1, 0), mod(1, 1))
    x = _resid_matmul(x, y_f, w_fourier[0].astype(BF16), mod(1, 2))
    return _moe_routed(x, row(norm_g[1, 1]), mod(1, 3), mod(1, 4), mod(1, 5), row(final_g),
                       w_router[0], b_router[0],
                       moe_w1[0].astype(BF16), moe_w3[0].astype(BF16), moe_w2[0].astype(BF16))
```

```python
import functools
import math

import numpy as np
import jax
import jax.numpy as jnp
from jax import lax
from jax.experimental import pallas as pl
from jax.experimental.pallas import tpu as pltpu

F32 = jnp.float32
BF16 = jnp.bfloat16
HIGHEST = lax.Precision.HIGHEST

GRID_W = 64
NA_HEAD_DIM = 32
NA_WIN_R = 8
NA_WIN_C = 16
HYENA_EMB = 33
HYENA_BANDS = (HYENA_EMB - 1) // 2
HYENA_FAST_DECAY = 0.3
HYENA_SLOW_DECAY = 1.5
HYENA_TARGET = 1e-2
F_GROUPS = 4
N_MOD = 6
EPS = 1e-6
NEG_INF = -1e30

FFT_A = 64
FFT_R = 128
FFT_KA = FFT_A // 2 + 1
FFT_KA_PAD = 40
FM_A = 64

VMEM_LIMIT = 48 * 1024 * 1024


def _cparams(sem):
    return pltpu.CompilerParams(dimension_semantics=sem, vmem_limit_bytes=VMEM_LIMIT)


def _dot(a, b):
    return jnp.dot(a, b, preferred_element_type=F32)


def _mxu_const(m):
    return jnp.asarray(m, dtype=F32).astype(BF16)


def _norm_mod(x, g, shift, scale):
    ms = jnp.mean(x * x, axis=-1, keepdims=True)
    y = x * lax.rsqrt(ms + EPS) * g
    return y * (1.0 + scale) + shift


def _ada_body(c_ref, w_ref, b_ref, o_ref):
    cv = c_ref[...]
    s = cv * jax.nn.sigmoid(cv)
    o_ref[0] = jnp.dot(s, w_ref[0], precision=HIGHEST, preferred_element_type=F32) + b_ref[0]


def _ada(cvec, w_ada, b_ada):
    depth, d, n = w_ada.shape
    rows = cvec.shape[0]
    bn = n // 4
    return pl.pallas_call(
        _ada_body,
        out_shape=jax.ShapeDtypeStruct((depth, rows, n), F32),
        grid=(depth, n // bn),
        in_specs=[pl.BlockSpec((rows, d), lambda l, j: (0, 0)),
                  pl.BlockSpec((1, d, bn), lambda l, j: (l, 0, j)),
                  pl.BlockSpec((1, 1, bn), lambda l, j: (l, 0, j))],
        out_specs=pl.BlockSpec((1, rows, bn), lambda l, j: (l, 0, j)),
        compiler_params=_cparams(("parallel", "parallel")),
        name="ada",
    )(cvec, w_ada, b_ada.reshape(depth, 1, n))


def _inproj_body(x_ref, xp_ref, xn_ref, g_ref, sh_ref, sc_ref, why_ref, wqkv_ref, sw_ref, sb_ref,
                 v_ref, x1_ref, x2_ref, q_ref, k_ref, va_ref, *, n_tiles, q_scale, c_hy, c_na):
    i = pl.program_id(1)
    g, sh, sc = g_ref[...], sh_ref[0], sc_ref[0]
    h = _norm_mod(x_ref[0], g, sh, sc).astype(BF16)
    zh = _dot(h, why_ref[...])
    hp = _norm_mod(xp_ref[0], g, sh, sc).astype(BF16)
    hn = _norm_mod(xn_ref[0], g, sh, sc).astype(BF16)
    zp = _dot(hp, why_ref[...])[7:8]
    zn = _dot(hn, why_ref[...])[0:1]
    zp = jnp.where(i > 0, zp, 0.0)
    zn = jnp.where(i < n_tiles - 1, zn, 0.0)
    tm = zh.shape[0]
    row = lax.broadcasted_iota(jnp.int32, zh.shape, 0)
    z_m1 = jnp.where(row == 0, zp, pltpu.roll(zh, 1, 0))
    z_p1 = jnp.where(row == tm - 1, zn, pltpu.roll(zh, tm - 1, 0))
    sw = sw_ref[...]
    zc = z_m1 * sw[0:1] + zh * sw[1:2] + z_p1 * sw[2:3] + sb_ref[...]
    v_ref[0] = zc[:, 0:c_hy].astype(BF16)
    x1_ref[0] = zc[:, c_hy:2 * c_hy].astype(BF16)
    x2_ref[0] = zc[:, 2 * c_hy:3 * c_hy].astype(BF16)
    zq = _dot(h, wqkv_ref[...])
    q_ref[0] = (zq[:, 0:c_na] * q_scale).astype(BF16)
    k_ref[0] = zq[:, c_na:2 * c_na].astype(BF16)
    va_ref[0] = zq[:, 2 * c_na:3 * c_na].astype(BF16)


def _inproj(x, g, shift, scale, w_hy, w_qkv, short_w, short_b, tm=512):
    b, s, d = x.shape
    c_hy = w_hy.shape[1] // 3
    c_na = w_qkv.shape[1] // 3
    n_tiles = s // tm
    r8 = tm // 8
    body = functools.partial(_inproj_body, n_tiles=n_tiles, q_scale=NA_HEAD_DIM ** -0.5,
                             c_hy=c_hy, c_na=c_na)
    tok = lambda c: pl.BlockSpec((1, tm, c), lambda bi, i: (bi, i, 0))
    full2 = lambda a: pl.BlockSpec(a.shape, lambda bi, i: (0, 0))
    per_b = pl.BlockSpec((1, 1, d), lambda bi, i: (bi, 0, 0))
    return pl.pallas_call(
        body,
        out_shape=[jax.ShapeDtypeStruct((b, s, c_hy), BF16)] * 3 + [jax.ShapeDtypeStruct((b, s, c_na), BF16)] * 3,
        grid=(b, n_tiles),
        in_specs=[tok(d),
                  pl.BlockSpec((1, 8, d), lambda bi, i: (bi, jnp.maximum(i * r8 - 1, 0), 0)),
                  pl.BlockSpec((1, 8, d), lambda bi, i: (bi, jnp.minimum((i + 1) * r8, s // 8 - 1), 0)),
                  full2(g), per_b, per_b, full2(w_hy), full2(w_qkv), full2(short_w), full2(short_b)],
        out_specs=[tok(c_hy)] * 3 + [tok(c_na)] * 3,
        compiler_params=_cparams(("parallel", "parallel")),
        name="inproj",
    )(x, x, x, g, shift, scale, w_hy, w_qkv, short_w, short_b)


def _ctxkv_body(x_ref, g_ref, sh_ref, sc_ref, w_ref, k_ref, v_ref, *, c_na):
    h = _norm_mod(x_ref[0], g_ref[...], sh_ref[0], sc_ref[0]).astype(BF16)
    z = _dot(h, w_ref[...])
    k_ref[0] = z[:, 0:c_na].astype(BF16)
    v_ref[0] = z[:, c_na:2 * c_na].astype(BF16)


def _ctxkv(ctx, g, shift, scale, w_kv):
    b, n, d = ctx.shape
    c_na = w_kv.shape[1] // 2
    one = pl.BlockSpec((1, 1, d), lambda bi: (0, 0, 0))
    return pl.pallas_call(
        functools.partial(_ctxkv_body, c_na=c_na),
        out_shape=[jax.ShapeDtypeStruct((b, n, c_na), BF16)] * 2,
        grid=(b,),
        in_specs=[pl.BlockSpec((1, n, d), lambda bi: (bi, 0, 0)),
                  pl.BlockSpec(g.shape, lambda bi: (0, 0)), one, one,
                  pl.BlockSpec(w_kv.shape, lambda bi: (0, 0))],
        out_specs=[pl.BlockSpec((1, n, c_na), lambda bi: (bi, 0, 0))] * 2,
        compiler_params=_cparams(("parallel",)),
        name="ctxkv",
    )(ctx, g, shift, scale, w_kv)


NA_HEADS_PER_BLK = 4


def _na_bias_body(r_ref, e_ref, ok_ref, o_ref):
    t = jnp.dot(r_ref[...], e_ref[...], precision=HIGHEST, preferred_element_type=F32)
    o_ref[...] = jnp.where(ok_ref[...] > 0.5, t, NEG_INF)


def _na_bias_table(rpb):
    w = GRID_W
    h, nr, nc = rpb.shape
    col = np.arange(w)[:, None]
    kc = np.arange(w)[None, :]
    c_start = np.clip(col - NA_WIN_C // 2, 0, w - NA_WIN_C)
    valid = ((kc >= c_start) & (kc < c_start + NA_WIN_C)).reshape(1, w * w)
    expand = (np.arange(32)[:, None, None] == (kc - col + NA_WIN_C - 1)[None]).reshape(32, w * w)
    rp = jnp.pad(rpb.astype(F32).reshape(h * nr, nc), ((0, 0), (0, 32 - nc)))
    full = lambda a: pl.BlockSpec(a.shape, lambda: (0,) * a.ndim)
    expand = jnp.asarray(expand, dtype=F32)
    ok = jnp.asarray(valid, dtype=F32)
    toep = pl.pallas_call(
        _na_bias_body,
        out_shape=jax.ShapeDtypeStruct((h * nr, w * w), F32),
        in_specs=[full(rp), full(expand), full(ok)],
        out_specs=pl.BlockSpec((h * nr, w * w), lambda: (0, 0)),
        name="na_bias",
    )(rp, expand, ok)
    t2 = toep.reshape(h, nr, w, w).transpose(0, 2, 1, 3).reshape(h, w, nr * w)
    slabs = jnp.stack([t2[:, :, (NA_WIN_R - 1 - off) * w:(2 * NA_WIN_R - 1 - off) * w]
                       for off in range(NA_WIN_R)], axis=1)
    hpb = NA_HEADS_PER_BLK
    slabs = slabs.reshape(h // hpb, hpb, NA_WIN_R, w, NA_WIN_R * w).transpose(0, 2, 1, 3, 4)
    return slabs.reshape(h // hpb, NA_WIN_R, hpb * w, NA_WIN_R * w)


def _natt_body(q_ref, k_ref, v_ref, kc_ref, vc_ref, bias_ref, o_ref, *, rows):
    w = GRID_W
    hpb = NA_HEADS_PER_BLK
    nloc = NA_WIN_R * w
    lane = lax.broadcasted_iota(jnp.int32, (1, hpb * NA_HEAD_DIM), 1)
    in_head = [(lane >= NA_HEAD_DIM * hh) & (lane < NA_HEAD_DIM * (hh + 1)) for hh in range(hpb)]
    kcx = kc_ref[0]
    vcx = vc_ref[0]
    nt = (((1,), (1,)), ((), ()))

    def one_row(r):
        r0 = jnp.clip(r - NA_WIN_R // 2, 0, rows - NA_WIN_R)
        off = r - r0
        qs = q_ref[0, pl.ds(pl.multiple_of(r * w, w), w), :]
        kw = k_ref[0, pl.ds(pl.multiple_of(r0 * w, w), nloc), :]
        vw = v_ref[0, pl.ds(pl.multiple_of(r0 * w, w), nloc), :]
        zero = jnp.zeros_like(qs)
        qst = jnp.concatenate([jnp.where(m, qs, zero) for m in in_head], axis=0)
        s_loc = lax.dot_general(qst, kw, nt, preferred_element_type=F32) + bias_ref[0, off]
        s_ctx = lax.dot_general(qst, kcx, nt, preferred_element_type=F32)
        m = jnp.maximum(jnp.max(s_loc, axis=-1, keepdims=True), jnp.max(s_ctx, axis=-1, keepdims=True))
        p_loc = jnp.exp(s_loc - m)
        p_ctx = jnp.exp(s_ctx - m)
        den = jnp.sum(p_loc, axis=-1, keepdims=True) + jnp.sum(p_ctx, axis=-1, keepdims=True)
        o = (_dot(p_loc.astype(BF16), vw) + _dot(p_ctx.astype(BF16), vcx)) * (1.0 / den)
        acc = jnp.where(in_head[0], o[0:w], 0.0)
        for hh in range(1, hpb):
            acc = acc + jnp.where(in_head[hh], o[hh * w:(hh + 1) * w], 0.0)
        o_ref[0, pl.ds(pl.multiple_of(r * w, w), w), :] = acc.astype(BF16)

    def row_pair(i, carry):
        one_row(2 * i)
        one_row(2 * i + 1)
        return carry

    lax.fori_loop(0, rows // 2, row_pair, 0)


def _natt(q, k, v, kc, vc, bias):
    b, s, c = q.shape
    nctx = kc.shape[1]
    lw = NA_HEADS_PER_BLK * NA_HEAD_DIM
    rows = s // GRID_W
    seq = pl.BlockSpec((1, s, lw), lambda bi, g: (bi, 0, g))
    cx = pl.BlockSpec((1, nctx, lw), lambda bi, g: (bi, 0, g))
    return pl.pallas_call(
        functools.partial(_natt_body, rows=rows),
        out_shape=jax.ShapeDtypeStruct((b, s, c), BF16),
        grid=(b, c // lw),
        in_specs=[seq, seq, seq, cx, cx,
                  pl.BlockSpec((1,) + bias.shape[1:], lambda bi, g: (g, 0, 0, 0))],
        out_specs=seq,
        compiler_params=_cparams(("parallel", "parallel")),
        name="natt",
    )(q, k, v, kc, vc, bias)


def _hyena_feats(seq_len):
    t = jnp.linspace(0.0, 1.0, seq_len, dtype=F32)[:, None]
    bands = jnp.linspace(1e-4, HYENA_BANDS - 1, HYENA_BANDS, dtype=F32)
    ang = (2.0 * math.pi / seq_len) * jnp.arange(seq_len, dtype=F32)[:, None] * bands[None, :]
    feats = jnp.concatenate([t, jnp.cos(ang), -jnp.sin(ang)], axis=-1)
    return jnp.pad(feats, ((0, 0), (0, 128 - HYENA_EMB)))


def _filt_body(feat_ref, w1_ref, b1_ref, w2_ref, b2_ref, w3_ref, fr_ref, dl_ref, o_ref, l1_ref):
    j = pl.program_id(0)
    hp = functools.partial(jnp.dot, precision=HIGHEST, preferred_element_type=F32)
    feats = feat_ref[...]
    fr = fr_ref[...]
    h = jnp.sin(fr[0:1] * (hp(feats, w1_ref[...]) + b1_ref[...]))
    h = jnp.sin(fr[1:2] * (hp(h, w2_ref[...]) + b2_ref[...]))
    hc = hp(h, w3_ref[...])
    t = feats[:, 0:1]
    hc = hc * jnp.exp(-t * dl_ref[...])
    row = lax.broadcasted_iota(jnp.int32, hc.shape, 0)
    hc = jnp.where((row == 0) & (j % 2 == 1), 0.0, hc)
    l1_ref[0] = jnp.sum(jnp.abs(hc), axis=0, keepdims=True)
    o_ref[0] = hc.astype(BF16)


def _hyena_filter_taps(seq_len, f_w1, f_b1, f_w2, f_b2, f_w3, f_freq, c_hy):
    feats = _hyena_feats(seq_len)
    hid = f_w1.shape[1]
    w1 = jnp.pad(f_w1.astype(F32), ((0, 128 - HYENA_EMB), (0, 0)))
    deltas = jnp.abs(jnp.linspace(math.log(HYENA_TARGET) / HYENA_SLOW_DECAY,
                                  math.log(HYENA_TARGET) / HYENA_FAST_DECAY, c_hy, dtype=F32))[None, :]
    nblk = f_w3.shape[1] // c_hy
    c0 = lambda a: pl.BlockSpec(a.shape, lambda j: (0, 0))
    b1, b2 = f_b1.reshape(1, hid), f_b2.reshape(1, hid)
    return pl.pallas_call(
        _filt_body,
        out_shape=[jax.ShapeDtypeStruct((nblk, seq_len, c_hy), BF16),
                   jax.ShapeDtypeStruct((nblk, 1, c_hy), F32)],
        grid=(nblk,),
        in_specs=[c0(feats), c0(w1), c0(b1), c0(f_w2), c0(b2),
                  pl.BlockSpec((hid, c_hy), lambda j: (0, j)), c0(f_freq), c0(deltas)],
        out_specs=[pl.BlockSpec((1, seq_len, c_hy), lambda j: (j, 0, 0)),
                   pl.BlockSpec((1, 1, c_hy), lambda j: (j, 0, 0))],
        compiler_params=_cparams(("parallel",)),
        name="hyena_filter",
    )(feats, w1, b1, f_w2, b2, f_w3, f_freq, deltas)


def _conv_dft_constants():
    a_half = FFT_A // 2
    n = FFT_A * FFT_R
    ka = np.arange(FFT_KA)[:, None]
    a = np.arange(a_half)[None, :]
    ph = 2.0 * np.pi * ka * a / FFT_A
    m_fwd = np.zeros((2 * FFT_KA_PAD, a_half))
    m_fwd[:FFT_KA] = np.cos(ph)
    m_fwd[FFT_KA_PAD:FFT_KA_PAD + FFT_KA] = -np.sin(ph)
    wgt = np.where((ka == 0) | (ka == FFT_A // 2), 1.0, 2.0)
    m_inv = np.zeros((a_half, 2 * FFT_KA_PAD))
    m_inv[:, :FFT_KA] = (wgt * np.cos(ph)).T / n
    m_inv[:, FFT_KA_PAD:FFT_KA_PAD + FFT_KA] = (-wgt * np.sin(ph)).T / n
    kb = np.arange(FFT_R)[None, :, None]
    b = np.arange(FFT_R)[None, None, :]
    kaa = np.arange(FFT_KA)[:, None, None]
    th = 2.0 * np.pi * (b * kb / FFT_R + b * kaa / n)
    gr, gi = np.cos(th), -np.sin(th)
    g2 = np.zeros((FFT_KA_PAD, 2 * FFT_R, 2 * FFT_R))
    g2[:FFT_KA] = np.block([[gr, -gi], [gi, gr]])
    grt, git = gr.transpose(0, 2, 1), gi.transpose(0, 2, 1)
    g2h = np.zeros_like(g2)
    g2h[:FFT_KA] = np.block([[grt, git], [-git, grt]])
    return _mxu_const(m_fwd), _mxu_const(m_inv), _mxu_const(g2), _mxu_const(g2h)


def _fwd1_body(m_ref, u_ref, o_ref):
    res = _dot(m_ref[...], u_ref[0])
    o_ref[0, 0] = res[0:FFT_KA_PAD]
    o_ref[0, 1] = res[FFT_KA_PAD:2 * FFT_KA_PAD]


def _conv_fwd1(u, m_fwd, cb=8192):
    n, seq, c = u.shape
    a_half = FFT_A // 2
    cols = seq * c // a_half
    uv = u.reshape(n, a_half, cols)
    return pl.pallas_call(
        _fwd1_body,
        out_shape=jax.ShapeDtypeStruct((n, 2, FFT_KA_PAD, cols), F32),
        grid=(n, cols // cb),
        in_specs=[pl.BlockSpec(m_fwd.shape, lambda i, j: (0, 0)),
                  pl.BlockSpec((1, a_half, cb), lambda i, j: (i, 0, j))],
        out_specs=pl.BlockSpec((1, 2, FFT_KA_PAD, cb), lambda i, j: (i, 0, 0, j)),
        compiler_params=_cparams(("parallel", "parallel")),
        name="conv_fwd1",
    )(m_fwd, uv)


FFT_KB = 8


def _rows_to_slabs(src_ref, dst_scr, c):
    for part in range(2):
        for b in range(FFT_R):
            dst_scr[part, :, b, :] = src_ref[0, part, :, b * c:(b + 1) * c]


def _slabs_to_rows(src_scr, dst_ref, c):
    for part in range(2):
        for b in range(FFT_R):
            dst_ref[0, part, :, b * c:(b + 1) * c] = src_scr[part, :, b, :]


def _slab(scr, i):
    return jnp.concatenate([scr[0, i], scr[1, i]], axis=0).astype(BF16)


def _fwd2f_body(sf_ref, sb_ref, g_ref, l1_ref, kf_ref, f3, b3):
    o = pl.program_id(0)
    j = pl.program_id(1)
    r2 = 2 * FFT_R
    c = kf_ref.shape[-1]
    _rows_to_slabs(sf_ref, f3, c)
    _rows_to_slabs(sb_ref, b3, c)
    inv = 1.0 / (l1_ref[2 * o] + l1_ref[2 * o + 1] + EPS)
    for i in range(FFT_KB):
        @pl.when(j * FFT_KB + i < FFT_KA)
        def _():
            xf = _dot(g_ref[i], _slab(f3, i))
            xb = _dot(g_ref[i], _slab(b3, i))
            kf_ref[0, i, 0:FFT_R] = (xf[0:FFT_R] + xb[0:FFT_R]) * inv
            kf_ref[0, i, FFT_R:r2] = (xf[FFT_R:r2] - xb[FFT_R:r2]) * inv

        @pl.when(j * FFT_KB + i >= FFT_KA)
        def _():
            kf_ref[0, i] = jnp.zeros((r2, c), F32)


def _filter_spectrum(s_filt, l1, g2, c):
    n_ord = s_filt.shape[0] // 2
    cols = s_filt.shape[-1]
    r2 = 2 * FFT_R
    return pl.pallas_call(
        _fwd2f_body,
        out_shape=jax.ShapeDtypeStruct((n_ord, FFT_KA_PAD, r2, c), F32),
        grid=(n_ord, FFT_KA_PAD // FFT_KB),
        in_specs=[pl.BlockSpec((1, 2, FFT_KB, cols), lambda o, j: (2 * o, 0, j, 0)),
                  pl.BlockSpec((1, 2, FFT_KB, cols), lambda o, j: (2 * o + 1, 0, j, 0)),
                  pl.BlockSpec((FFT_KB, r2, r2), lambda o, j: (j, 0, 0)),
                  pl.BlockSpec(l1.shape, lambda o, j: (0, 0, 0))],
        out_specs=pl.BlockSpec((1, FFT_KB, r2, c), lambda o, j: (o, j, 0, 0)),
        scratch_shapes=[pltpu.VMEM((2, FFT_KB, FFT_R, c), F32)] * 2,
        compiler_params=_cparams(("parallel", "parallel")),
        name="filter_spectrum",
    )(s_filt, s_filt, g2, l1)


def _mid_body(s_ref, g_ref, gh_ref, kf_ref, t_ref, s3, t3):
    j = pl.program_id(1)
    r2 = 2 * FFT_R
    c = kf_ref.shape[-1]
    _rows_to_slabs(s_ref, s3, c)
    for i in range(FFT_KB):
        @pl.when(j * FFT_KB + i < FFT_KA)
        def _():
            x = _dot(g_ref[i], _slab(s3, i))
            xr, xi = x[0:FFT_R], x[FFT_R:r2]
            kr, ki = kf_ref[0, i, 0:FFT_R], kf_ref[0, i, FFT_R:r2]
            y = jnp.concatenate([xr * kr - xi * ki, xr * ki + xi * kr], axis=0).astype(BF16)
            t = _dot(gh_ref[i], y)
            t3[0, i] = t[0:FFT_R]
            t3[1, i] = t[FFT_R:r2]

        @pl.when(j * FFT_KB + i >= FFT_KA)
        def _():
            t3[0, i] = jnp.zeros((FFT_R, c), F32)
            t3[1, i] = jnp.zeros((FFT_R, c), F32)

    _slabs_to_rows(t3, t_ref, c)


def _conv_mid(s, kf, order, g2, g2h, c):
    n, _, _, cols = s.shape
    r2 = 2 * FFT_R
    blk = pl.BlockSpec((1, 2, FFT_KB, cols), lambda i, j: (i, 0, j, 0))
    gspec = pl.BlockSpec((FFT_KB, r2, r2), lambda i, j: (j, 0, 0))
    return pl.pallas_call(
        _mid_body,
        out_shape=jax.ShapeDtypeStruct(s.shape, F32),
        grid=(n, FFT_KA_PAD // FFT_KB),
        in_specs=[blk, gspec, gspec,
                  pl.BlockSpec((1, FFT_KB, r2, c), lambda i, j: (order, j, 0, 0))],
        out_specs=blk,
        scratch_shapes=[pltpu.VMEM((2, FFT_KB, FFT_R, c), F32)] * 2,
        compiler_params=_cparams(("parallel", "parallel")),
        name="conv_mid",
    )(s, g2, g2h, kf)


def _inv1_body(m_ref, t_ref, u_ref, xg_ref, sk_ref, o_ref):
    cb = t_ref.shape[-1]
    t2 = t_ref[0].reshape(2 * FFT_KA_PAD, cb).astype(BF16)
    y = _dot(m_ref[...], t2)
    u = u_ref[0].astype(F32)
    o_ref[0] = (xg_ref[0].astype(F32) * (y + u * sk_ref[...])).astype(BF16)


def _conv_inv1(t, u, xg, skip, m_inv, cb=8192):
    n, seq, c = u.shape
    a_half = FFT_A // 2
    cols = seq * c // a_half
    sk = jnp.tile(skip.astype(F32).reshape(1, c), (1, cols // c))
    uspec = pl.BlockSpec((1, a_half, cb), lambda i, j: (i, 0, j))
    out = pl.pallas_call(
        _inv1_body,
        out_shape=jax.ShapeDtypeStruct((n, a_half, cols), BF16),
        grid=(n, cols // cb),
        in_specs=[pl.BlockSpec(m_inv.shape, lambda i, j: (0, 0)),
                  pl.BlockSpec((1, 2, FFT_KA_PAD, cb), lambda i, j: (i, 0, 0, j)),
                  uspec, uspec,
                  pl.BlockSpec((1, cb), lambda i, j: (0, j))],
        out_specs=uspec,
        compiler_params=_cparams(("parallel", "parallel")),
        name="conv_inv1",
    )(m_inv, t, u.reshape(n, a_half, cols), xg.reshape(n, a_half, cols), sk)
    return out.reshape(n, seq, c)


def _hyena(v, x1, x2, f_w1, f_b1, f_w2, f_b2, f_w3, f_freq, skip):
    _, seq, c = v.shape
    assert 2 * seq == FFT_A * FFT_R
    m_fwd, m_inv, g2, g2h = _conv_dft_constants()
    taps, l1 = _hyena_filter_taps(seq, f_w1, f_b1, f_w2, f_b2, f_w3, f_freq, c)
    kf = _filter_spectrum(_conv_fwd1(taps, m_fwd), l1, g2, c)
    y = v
    for order, xg in enumerate((x1, x2)):
        t = _conv_mid(_conv_fwd1(y, m_fwd), kf, order, g2, g2h, c)
        y = _conv_inv1(t, y, xg, skip[order], m_inv)
    return y


def _mixout_body(x_ref, a1_ref, a2_ref, w_ref, gt_ref, o_ref):
    c1 = a1_ref.shape[-1]
    y = _dot(a1_ref[0], w_ref[0:c1]) + _dot(a2_ref[0], w_ref[c1:])
    o_ref[0] = x_ref[0] + gt_ref[0] * y


def _mixout(x, a1, a2, w, gate, tm=512):
    b, s, d = x.shape
    tok = lambda c: pl.BlockSpec((1, tm, c), lambda bi, i: (bi, i, 0))
    return pl.pallas_call(
        _mixout_body,
        out_shape=jax.ShapeDtypeStruct(x.shape, F32),
        grid=(b, s // tm),
        in_specs=[tok(d), tok(a1.shape[-1]), tok(a2.shape[-1]),
                  pl.BlockSpec(w.shape, lambda bi, i: (0, 0)),
                  pl.BlockSpec((1, 1, d), lambda bi, i: (bi, 0, 0))],
        out_specs=tok(d),
        compiler_params=_cparams(("parallel", "parallel")),
        name="mixout",
    )(x, a1, a2, w, gate)


def _resid_matmul_body(x_ref, a_ref, w_ref, gt_ref, o_ref):
    o_ref[0] = x_ref[0] + gt_ref[0] * _dot(a_ref[0].astype(BF16), w_ref[...])


def _resid_matmul(x, a, w, gate, tm=512):
    b, s, d = x.shape
    tok = lambda c: pl.BlockSpec((1, tm, c), lambda bi, i: (bi, i, 0))
    return pl.pallas_call(
        _resid_matmul_body,
        out_shape=jax.ShapeDtypeStruct(x.shape, F32),
        grid=(b, s // tm),
        in_specs=[tok(d), tok(a.shape[-1]),
                  pl.BlockSpec(w.shape, lambda bi, i: (0, 0)),
                  pl.BlockSpec((1, 1, d), lambda bi, i: (bi, 0, 0))],
        out_specs=tok(d),
        compiler_params=_cparams(("parallel", "parallel")),
        name="resid_matmul",
    )(x, a, w, gate)


def _ffn_body(x_ref, g_ref, sh_ref, sc_ref, gt_ref, w1_ref, w3_ref, w2_ref, o_ref, h_scr, acc_scr):
    j = pl.program_id(2)

    @pl.when(j == 0)
    def _():
        h_scr[...] = _norm_mod(x_ref[0], g_ref[...], sh_ref[0], sc_ref[0]).astype(BF16)
        acc_scr[...] = jnp.zeros_like(acc_scr)

    h = h_scr[...]
    a = _dot(h, w1_ref[...])
    u = (a * jax.nn.sigmoid(a) * _dot(h, w3_ref[...])).astype(BF16)
    acc_scr[...] += _dot(u, w2_ref[...])

    @pl.when(j == pl.num_programs(2) - 1)
    def _():
        o_ref[0] = x_ref[0] + gt_ref[0] * acc_scr[...]


def _ffn(x, g, shift, scale, gate, w1, w3, w2, tm=512, fb=1408):
    b, s, d = x.shape
    f = w1.shape[1]
    tok = pl.BlockSpec((1, tm, d), lambda bi, i, j: (bi, i, 0))
    per_b = pl.BlockSpec((1, 1, d), lambda bi, i, j: (bi, 0, 0))
    return pl.pallas_call(
        _ffn_body,
        out_shape=jax.ShapeDtypeStruct(x.shape, F32),
        grid=(b, s // tm, f // fb),
        in_specs=[tok, pl.BlockSpec(g.shape, lambda bi, i, j: (0, 0)), per_b, per_b, per_b,
                  pl.BlockSpec((d, fb), lambda bi, i, j: (0, j)),
                  pl.BlockSpec((d, fb), lambda bi, i, j: (0, j)),
                  pl.BlockSpec((fb, d), lambda bi, i, j: (j, 0))],
        out_specs=tok,
        scratch_shapes=[pltpu.VMEM((tm, d), BF16), pltpu.VMEM((tm, d), F32)],
        compiler_params=_cparams(("parallel", "parallel", "arbitrary")),
        name="ffn",
    )(x, g, shift, scale, gate, w1, w3, w2)


def _fm_constants(cg):
    j = np.arange(cg)[:, None]
    m = np.arange(cg)[None, :]
    ph = 2.0 * np.pi * j * m / cg
    w_cs = np.concatenate([np.cos(ph), np.sin(ph)], axis=1)
    d = np.arange(FM_A)[:, None]
    a = np.arange(FM_A)[None, :]
    ph = 2.0 * np.pi * d * a / FM_A
    fr, fi = np.cos(ph), -np.sin(ph)
    m1 = np.block([[fr, fi], [fi, -fr]])
    n = FM_A * FM_A
    dd = np.arange(FM_A)[:, None, None]
    c = np.arange(FM_A)[None, :, None]
    b = np.arange(FM_A)[None, None, :]
    th = 2.0 * np.pi * (b * c / FM_A + b * dd / n)
    gcat = np.concatenate([np.cos(th), np.sin(th)], axis=2)
    return _mxu_const(w_cs), _mxu_const(m1), _mxu_const(gcat)


def _fm_chan_body(x_ref, g_ref, sh_ref, sc_ref, w_ref, o_ref, *, cg):
    h = _norm_mod(x_ref[0], g_ref[...], sh_ref[0], sc_ref[0]).astype(BF16)
    for grp in range(h.shape[-1] // cg):
        pq = _dot(h[:, grp * cg:(grp + 1) * cg], w_ref[...])
        o_ref[0, 0, :, grp * cg:(grp + 1) * cg] = pq[:, 0:cg].astype(BF16)
        o_ref[0, 1, :, grp * cg:(grp + 1) * cg] = pq[:, cg:2 * cg].astype(BF16)


def _fm_chan(x, g, shift, scale, w_cs, tm=512):
    b, s, d = x.shape
    cg = w_cs.shape[0]
    per_b = pl.BlockSpec((1, 1, d), lambda bi, i: (bi, 0, 0))
    return pl.pallas_call(
        functools.partial(_fm_chan_body, cg=cg),
        out_shape=jax.ShapeDtypeStruct((b, 2, s, d), BF16),
        grid=(b, s // tm),
        in_specs=[pl.BlockSpec((1, tm, d), lambda bi, i: (bi, i, 0)),
                  pl.BlockSpec(g.shape, lambda bi, i: (0, 0)), per_b, per_b,
                  pl.BlockSpec(w_cs.shape, lambda bi, i: (0, 0))],
        out_specs=pl.BlockSpec((1, 2, tm, d), lambda bi, i: (bi, 0, i, 0)),
        compiler_params=_cparams(("parallel", "parallel")),
        name="fm_chan",
    )(x, g, shift, scale, w_cs)


def _fm_s1_body(m_ref, pq_ref, o_ref):
    o_ref[0] = _dot(m_ref[...], pq_ref[0]).astype(BF16)


def _fm_stage1(pq, m1, cb=8192):
    b, _, s, d = pq.shape
    cols = s * d // FM_A
    blk = pl.BlockSpec((1, 2 * FM_A, cb), lambda bi, j: (bi, 0, j))
    return pl.pallas_call(
        _fm_s1_body,
        out_shape=jax.ShapeDtypeStruct((b, 2 * FM_A, cols), BF16),
        grid=(b, cols // cb),
        in_specs=[pl.BlockSpec(m1.shape, lambda bi, j: (0, 0)), blk],
        out_specs=blk,
        compiler_params=_cparams(("parallel", "parallel")),
        name="fm_stage1",
    )(m1, pq.reshape(b, 2 * FM_A, cols))


def _fm_s2_body(s_ref, g_ref, o_ref, *, dblk, scale):
    for i in range(dblk):
        s2 = jnp.concatenate([s_ref[0, 0, i], s_ref[0, 1, i]], axis=0)
        o_ref[:, i, :] = _dot(g_ref[i], s2) * scale


def _fm_stage2(s1, gcat, seq, d, dblk=8):
    b = s1.shape[0]
    sv = s1.reshape(b, 2, FM_A, FM_A, d)
    scale = 1.0 / math.sqrt(seq * (d // F_GROUPS))
    out = pl.pallas_call(
        functools.partial(_fm_s2_body, dblk=dblk, scale=scale),
        out_shape=jax.ShapeDtypeStruct((b * FM_A, dblk * (FM_A // dblk), d), F32),
        grid=(b, FM_A // dblk),
        in_specs=[pl.BlockSpec((1, 2, dblk, FM_A, d), lambda bi, j: (bi, 0, j, 0, 0)),
                  pl.BlockSpec((dblk, FM_A, 2 * FM_A), lambda bi, j: (j, 0, 0))],
        out_specs=pl.BlockSpec((FM_A, dblk, d), lambda bi, j: (bi, j, 0)),
        compiler_params=_cparams(("parallel", "parallel")),
        name="fm_stage2",
    )(sv, gcat)
    return out.reshape(b, seq, d)


def _fourier_mix(x, g, shift, scale):
    b, s, d = x.shape
    assert s == FM_A * FM_A
    w_cs, m1, gcat = _fm_constants(d // F_GROUPS)
    pq = _fm_chan(x, g, shift, scale, w_cs)
    return _fm_stage2(_fm_stage1(pq, m1), gcat, s, d)


LANES = 128
MOE_TM = 1024
DMA_WINDOW = 128


def _router_body(x_ref, g_ref, sh_ref, sc_ref, wr_ref, br_ref, h_ref, meta_ref, gw_ref, cnt_ref, carry):
    i = pl.program_id(0)

    @pl.when(i == 0)
    def _():
        carry[...] = jnp.zeros_like(carry)

    h = _norm_mod(x_ref[...], g_ref[...], sh_ref[0], sc_ref[0])
    for sl in range(h.shape[-1] // LANES):
        h_ref[:, sl, :] = h[:, sl * LANES:(sl + 1) * LANES]
    logits = jnp.dot(h, wr_ref[...], precision=HIGHEST, preferred_element_type=F32) + br_ref[...]
    lane = lax.broadcasted_iota(jnp.int32, logits.shape, 1)
    nl = logits.shape[-1]
    m1 = jnp.max(logits, axis=-1, keepdims=True)
    i1 = jnp.min(jnp.where(logits == m1, lane, nl), axis=-1, keepdims=True)
    rest = jnp.where(lane == i1, -3.0e38, logits)
    m2 = jnp.max(rest, axis=-1, keepdims=True)
    i2 = jnp.min(jnp.where(rest == m2, lane, nl), axis=-1, keepdims=True)
    e = jnp.exp(m2 - m1)
    gw_ref[...] = jnp.where(lane == 0, 1.0 / (1.0 + e), jnp.where(lane == 1, e / (1.0 + e), 0.0))
    onehot = jnp.where((lane == i1) | (lane == i2), 1.0, 0.0)
    tm = onehot.shape[0]
    earlier = lax.broadcasted_iota(jnp.int32, (tm, tm), 0) > lax.broadcasted_iota(jnp.int32, (tm, tm), 1)
    excl = _dot(jnp.where(earlier, 1.0, 0.0).astype(BF16), onehot.astype(BF16)) + carry[...]
    r1 = jnp.sum(jnp.where(lane == i1, excl, 0.0), axis=-1, keepdims=True).astype(jnp.int32)
    r2 = jnp.sum(jnp.where(lane == i2, excl, 0.0), axis=-1, keepdims=True).astype(jnp.int32)
    meta_ref[...] = jnp.where(lane == 0, i1, jnp.where(lane == 1, i2, jnp.where(lane == 2, r1, jnp.where(lane == 3, r2, 0))))
    carry[...] = carry[...] + jnp.sum(onehot, axis=0, keepdims=True)
    cnt_ref[...] = carry[...]


def _router(x, g, shift, scale, w_router, b_router, tm=512):
    b, s, d = x.shape
    t = b * s
    ne = w_router.shape[1]
    wr = jnp.pad(w_router.astype(F32), ((0, 0), (0, LANES - ne)))
    br = jnp.pad(b_router.astype(F32).reshape(1, ne), ((0, 0), (0, LANES - ne)), constant_values=NEG_INF)
    spt = s // tm
    per_b = pl.BlockSpec((1, 1, d), lambda i: (i // spt, 0, 0))
    const = lambda a: pl.BlockSpec(a.shape, lambda i: (0, 0))
    return pl.pallas_call(
        _router_body,
        out_shape=[jax.ShapeDtypeStruct((t, d // LANES, LANES), F32),
                   jax.ShapeDtypeStruct((t, LANES), jnp.int32),
                   jax.ShapeDtypeStruct((t, LANES), F32),
                   jax.ShapeDtypeStruct((1, LANES), F32)],
        grid=(t // tm,),
        in_specs=[pl.BlockSpec((tm, d), lambda i: (i, 0)), const(g), per_b, per_b, const(wr), const(br)],
        out_specs=[pl.BlockSpec((tm, d // LANES, LANES), lambda i: (i, 0, 0)),
                   pl.BlockSpec((tm, LANES), lambda i: (i, 0)),
                   pl.BlockSpec((tm, LANES), lambda i: (i, 0)),
                   pl.BlockSpec((1, LANES), lambda i: (0, 0))],
        scratch_shapes=[pltpu.VMEM((1, LANES), F32)],
        compiler_params=_cparams(("arbitrary",)),
        name="router",
    )(x.reshape(t, d), g, shift, scale, wr, br)


def _moe_plan(meta, counts, ne, tm):
    i1, i2, r1, r2 = meta[:, 0], meta[:, 1], meta[:, 2], meta[:, 3]
    cnt = counts[0, :ne].astype(jnp.int32)
    padded = ((cnt + tm - 1) // tm) * tm
    ends = jnp.cumsum(padded)
    offs = ends - padded
    pick = lambda idx: sum(jnp.where(idx == e, offs[e], 0) for e in range(ne))
    pos = jnp.concatenate([pick(i1) + r1, pick(i2) + r2]).astype(jnp.int32)
    n_tiles = (2 * meta.shape[0]) // tm + ne
    n_used = (ends[ne - 1] // tm).astype(jnp.int32)
    tile_start = jnp.minimum(jnp.arange(n_tiles, dtype=jnp.int32), n_used - 1) * tm
    tile_expert = jnp.sum(tile_start[:, None] >= ends[None, :], axis=1).astype(jnp.int32)
    return pos, offs + cnt, padded - cnt, tile_expert, n_used.reshape(1)


def _windowed_copies(n, start_copy, wait_one, per_iter):
    def body(i, carry):
        @pl.when(i >= DMA_WINDOW)
        def _():
            for _ in range(per_iter):
                wait_one()
        start_copy(i)
        return carry

    lax.fori_loop(0, n, body, 0)

    def drain(i, carry):
        for _ in range(per_iter):
            wait_one()
        return carry

    lax.fori_loop(0, jnp.minimum(n, DMA_WINDOW), drain, 0)


def _dispatch_body(pos_ref, pad_start_ref, pad_n_ref, h_ref, xs_hbm, sem, *, n_tok, ne):
    i = pl.program_id(0)
    td = h_ref.shape[0]
    base = i * td
    copy = lambda src, dst: pltpu.make_async_copy(h_ref.at[src], xs_hbm.at[dst], sem)
    wait_one = lambda: copy(0, 0).wait()

    def start_token(r, carry):
        copy(r, pos_ref[base + r]).start()
        copy(r, pos_ref[n_tok + base + r]).start()
        return carry

    lax.fori_loop(0, td, start_token, 0, unroll=8)
    whole_tile = pltpu.make_async_copy(h_ref, xs_hbm.at[pl.ds(0, td)], sem)
    whole_tile.wait()
    whole_tile.wait()

    @pl.when(i == 0)
    def _():
        for e in range(ne):
            first = pad_start_ref[e]
            _windowed_copies(pad_n_ref[e], lambda r: copy(0, first + r).start(), wait_one, 1)


def _moe_dispatch(h3, pos, pad_start, pad_n, n_rows, td=1024):
    n_tok = h3.shape[0]
    ne = pad_start.shape[0]
    return pl.pallas_call(
        functools.partial(_dispatch_body, n_tok=n_tok, ne=ne),
        out_shape=jax.ShapeDtypeStruct((n_rows,) + h3.shape[1:], h3.dtype),
        grid_spec=pltpu.PrefetchScalarGridSpec(
            num_scalar_prefetch=3, grid=(n_tok // td,),
            in_specs=[pl.BlockSpec((td,) + h3.shape[1:], lambda i, p, ps, pn: (i, 0, 0))],
            out_specs=pl.BlockSpec(memory_space=pl.ANY),
            scratch_shapes=[pltpu.SemaphoreType.DMA(())]),
        compiler_params=_cparams(("arbitrary",)),
        name="moe_dispatch",
    )(pos, pad_start, pad_n, h3)


def _moe_grouped_body(te_ref, nu_ref, xs_ref, w1_ref, w3_ref, w2_ref, y_ref, xb_scr, acc_scr):
    i = pl.program_id(0)
    j = pl.program_id(1)
    nsl = xs_ref.shape[1]

    @pl.when(i < nu_ref[0])
    def _():
        @pl.when(j == 0)
        def _():
            xb_scr[...] = jnp.concatenate([xs_ref[:, sl, :] for sl in range(nsl)], axis=1).astype(BF16)
            acc_scr[...] = jnp.zeros_like(acc_scr)

        h = xb_scr[...]
        a = _dot(h, w1_ref[0].astype(BF16))
        u = (a * jax.nn.sigmoid(a) * _dot(h, w3_ref[0].astype(BF16))).astype(BF16)
        acc_scr[...] += _dot(u, w2_ref[0].astype(BF16))

        @pl.when(j == pl.num_programs(1) - 1)
        def _():
            for sl in range(nsl):
                y_ref[:, sl, :] = acc_scr[:, sl * LANES:(sl + 1) * LANES]


def _moe_grouped(xs, tile_expert, n_used, w1, w3, w2, tm, fb=512):
    n_rows, nsl, _ = xs.shape
    ne, d, f = w1.shape
    nj = f // fb
    row_tile = lambda i, j, te, nu: (jnp.minimum(i, nu[0] - 1), 0, 0)
    jj = lambda i, j, nu: jnp.where(i < nu[0], j, nj - 1)
    return pl.pallas_call(
        _moe_grouped_body,
        out_shape=jax.ShapeDtypeStruct(xs.shape, F32),
        grid_spec=pltpu.PrefetchScalarGridSpec(
            num_scalar_prefetch=2, grid=(n_rows // tm, nj),
            in_specs=[pl.BlockSpec((tm, nsl, LANES), row_tile),
                      pl.BlockSpec((1, d, fb), lambda i, j, te, nu: (te[i], 0, jj(i, j, nu))),
                      pl.BlockSpec((1, d, fb), lambda i, j, te, nu: (te[i], 0, jj(i, j, nu))),
                      pl.BlockSpec((1, fb, d), lambda i, j, te, nu: (te[i], jj(i, j, nu), 0))],
            out_specs=pl.BlockSpec((tm, nsl, LANES), row_tile),
            scratch_shapes=[pltpu.VMEM((tm, d), BF16), pltpu.VMEM((tm, d), F32)]),
        compiler_params=_cparams(("arbitrary", "arbitrary")),
        name="moe_grouped",
    )(tile_expert, n_used, xs, w1, w3, w2)


def _moe_final_body(pos_ref, x_ref, y_hbm, gw_ref, gt_ref, fg_ref, o_ref, yg_scr, sem, *, n_tok):
    i = pl.program_id(0)
    tc = x_ref.shape[0]
    slot = i % 2

    def gather_tile(step, into):
        base = step * tc

        def start_token(r, carry):
            pltpu.make_async_copy(y_hbm.at[pos_ref[base + r]], yg_scr.at[into, r], sem.at[into]).start()
            pltpu.make_async_copy(y_hbm.at[pos_ref[n_tok + base + r]], yg_scr.at[into, tc + r], sem.at[into]).start()
            return carry

        lax.fori_loop(0, tc, start_token, 0, unroll=8)

    @pl.when(i == 0)
    def _():
        gather_tile(0, 0)

    @pl.when(i + 1 < pl.num_programs(0))
    def _():
        gather_tile(i + 1, 1 - slot)

    pltpu.make_async_copy(y_hbm.at[pl.ds(0, 2 * tc)], yg_scr.at[slot], sem.at[slot]).wait()
    gw = gw_ref[...]
    w1, w2 = gw[:, 0:1], gw[:, 1:2]
    y = jnp.concatenate([w1 * yg_scr[slot, pl.ds(0, tc), sl, :] + w2 * yg_scr[slot, pl.ds(tc, tc), sl, :]
                         for sl in range(yg_scr.shape[2])], axis=1)
    xo = x_ref[...] + gt_ref[0] * y
    ms = jnp.mean(xo * xo, axis=-1, keepdims=True)
    o_ref[...] = xo * lax.rsqrt(ms + EPS) * fg_ref[...]


def _moe_final(x, y, pos, gw, gt, final_g, tc=256):
    b, s, d = x.shape
    t = b * s
    spt = s // tc
    nsl = d // LANES
    out = pl.pallas_call(
        functools.partial(_moe_final_body, n_tok=t),
        out_shape=jax.ShapeDtypeStruct((t, d), F32),
        grid_spec=pltpu.PrefetchScalarGridSpec(
            num_scalar_prefetch=1, grid=(t // tc,),
            in_specs=[pl.BlockSpec((tc, d), lambda i, p: (i, 0)),
                      pl.BlockSpec(memory_space=pl.ANY),
                      pl.BlockSpec((tc, LANES), lambda i, p: (i, 0)),
                      pl.BlockSpec((1, 1, d), lambda i, p: (i // spt, 0, 0)),
                      pl.BlockSpec(final_g.shape, lambda i, p: (0, 0))],
            out_specs=pl.BlockSpec((tc, d), lambda i, p: (i, 0)),
            scratch_shapes=[pltpu.VMEM((2, 2 * tc, nsl, LANES), F32), pltpu.SemaphoreType.DMA((2,))]),
        compiler_params=_cparams(("arbitrary",)),
        name="moe_final",
    )(pos, x.reshape(t, d), y, gw, gt, final_g)
    return out.reshape(b, s, d)


def _moe_routed(x, g, shift, scale, gt, final_g, w_router, b_router, w1, w3, w2):
    ne = w1.shape[0]
    tm = MOE_TM
    h3, meta, gw, counts = _router(x, g, shift, scale, w_router, b_router)
    pos, pad_start, pad_n, tile_expert, n_used = _moe_plan(meta, counts, ne, tm)
    n_rows = (2 * h3.shape[0] // tm + ne) * tm
    xs = _moe_dispatch(h3, pos, pad_start, pad_n, n_rows)
    y = _moe_grouped(xs, tile_expert, n_used, w1, w3, w2, tm)
    return _moe_final(x, y, pos, gw, gt, final_g)


def kernel(x, c, ctx, c_ctx, w_ada, b_ada, norm_g, w_in, hy_short_w, hy_short_b, hy_f_w1, hy_f_b1, hy_f_w2, hy_f_b2, hy_f_w3, hy_f_freq, hy_skip, na_rpb, w_mix_out, ffn_w1, ffn_w3, ffn_w2, w_fourier, w_router, b_router, moe_w1, moe_w3, moe_w2, final_g):
    b, s, d = x.shape
    depth = w_ada.shape[0]
    assert depth == 2, "layer 0 mixes with Hyena/attention, layer 1 with Fourier/MoE"
    c_hy = hy_skip.shape[-1]
    c_na = d - c_hy

    cvec = jnp.concatenate([c, c_ctx[None, :], jnp.zeros((8 - b - 1, d), F32)], axis=0)
    mods = _ada(cvec, w_ada, b_ada)

    def mod(layer, idx, ctx_row=False):
        m = mods[layer, :, idx * d:(idx + 1) * d]
        return m[b:b + 1, None, :] if ctx_row else m[0:b, None, :]

    row = lambda a: a.reshape(1, -1)

    w_in0 = w_in[0].astype(BF16)
    w_hy, w_qkv = w_in0[:, 0:3 * c_hy], w_in0[:, 3 * c_hy:]
    v, x1, x2, q, k, va = _inproj(x, row(norm_g[0, 0]), mod(0, 0), mod(0, 1), w_hy, w_qkv,
                                  hy_short_w[0], row(hy_short_b[0]))
    kc, vc = _ctxkv(ctx, row(norm_g[0, 0]), mod(0, 0, True), mod(0, 1, True), w_qkv[:, c_na:])
    y_na = _natt(q, k, va, kc, vc, _na_bias_table(na_rpb[0]))
    y_hy = _hyena(v, x1, x2, hy_f_w1[0], hy_f_b1[0], hy_f_w2[0], hy_f_b2[0], hy_f_w3[0],
                  hy_f_freq[0], hy_skip[0])
    x = _mixout(x, y_hy, y_na, w_mix_out[0].astype(BF16), mod(0, 2))
    x = _ffn(x, row(norm_g[0, 1]), mod(0, 3), mod(0, 4), mod(0, 5),
             ffn_w1[0].astype(BF16), ffn_w3[0].astype(BF16), ffn_w2[0].astype(BF16))

    y_f = _fourier_mix(x, row(norm_g[1, 0]), mod(1, 0), mod(1, 1))
    x = _resid_matmul(x, y_f, w_fourier[0].astype(BF16), mod(1, 2))
    return _moe_routed(x, row(norm_g[1, 1]), mod(1, 3), mod(1, 4), mod(1, 5), row(final_g),
                       w_router[0], b_router[0],
                       moe_w1[0], moe_w3[0], moe_w2[0])
```

```python
import functools
import math

import numpy as np
import jax
import jax.numpy as jnp
from jax import lax
from jax.experimental import pallas as pl
from jax.experimental.pallas import tpu as pltpu

F32 = jnp.float32
BF16 = jnp.bfloat16
HIGHEST = lax.Precision.HIGHEST

GRID_W = 64
NA_HEAD_DIM = 32
NA_WIN_R = 8
NA_WIN_C = 16
HYENA_EMB = 33
HYENA_BANDS = (HYENA_EMB - 1) // 2
HYENA_FAST_DECAY = 0.3
HYENA_SLOW_DECAY = 1.5
HYENA_TARGET = 1e-2
F_GROUPS = 4
N_MOD = 6
EPS = 1e-6
NEG_INF = -1e30

FFT_A = 64
FFT_R = 128
FFT_KA = FFT_A // 2 + 1
FFT_KA_PAD = 40
FM_A = 64

VMEM_LIMIT = 48 * 1024 * 1024


def _cparams(sem):
    return pltpu.CompilerParams(dimension_semantics=sem, vmem_limit_bytes=VMEM_LIMIT)


def _dot(a, b):
    return jnp.dot(a, b, preferred_element_type=F32)


def _mxu_const(m):
    return jnp.asarray(m, dtype=F32).astype(BF16)


def _norm_mod(x, g, shift, scale):
    ms = jnp.mean(x * x, axis=-1, keepdims=True)
    y = x * lax.rsqrt(ms + EPS) * g
    return y * (1.0 + scale) + shift


def _ada_body(c_ref, w_ref, b_ref, o_ref):
    cv = c_ref[...]
    s = cv * jax.nn.sigmoid(cv)
    o_ref[0] = jnp.dot(s, w_ref[0], precision=HIGHEST, preferred_element_type=F32) + b_ref[0]


def _ada(cvec, w_ada, b_ada):
    depth, d, n = w_ada.shape
    rows = cvec.shape[0]
    bn = n // 4
    return pl.pallas_call(
        _ada_body,
        out_shape=jax.ShapeDtypeStruct((depth, rows, n), F32),
        grid=(depth, n // bn),
        in_specs=[pl.BlockSpec((rows, d), lambda l, j: (0, 0)),
                  pl.BlockSpec((1, d, bn), lambda l, j: (l, 0, j)),
                  pl.BlockSpec((1, 1, bn), lambda l, j: (l, 0, j))],
        out_specs=pl.BlockSpec((1, rows, bn), lambda l, j: (l, 0, j)),
        compiler_params=_cparams(("parallel", "parallel")),
        name="ada",
    )(cvec, w_ada, b_ada.reshape(depth, 1, n))


def _inproj_body(x_ref, xp_ref, xn_ref, g_ref, sh_ref, sc_ref, why_ref, wqkv_ref, sw_ref, sb_ref,
                 v_ref, x1_ref, x2_ref, q_ref, k_ref, va_ref, *, n_tiles, q_scale, c_hy, c_na):
    i = pl.program_id(1)
    g, sh, sc = g_ref[...], sh_ref[0], sc_ref[0]
    h = _norm_mod(x_ref[0], g, sh, sc).astype(BF16)
    zh = _dot(h, why_ref[...])
    hp = _norm_mod(xp_ref[0], g, sh, sc).astype(BF16)
    hn = _norm_mod(xn_ref[0], g, sh, sc).astype(BF16)
    zp = _dot(hp, why_ref[...])[7:8]
    zn = _dot(hn, why_ref[...])[0:1]
    zp = jnp.where(i > 0, zp, 0.0)
    zn = jnp.where(i < n_tiles - 1, zn, 0.0)
    tm = zh.shape[0]
    row = lax.broadcasted_iota(jnp.int32, zh.shape, 0)
    z_m1 = jnp.where(row == 0, zp, pltpu.roll(zh, 1, 0))
    z_p1 = jnp.where(row == tm - 1, zn, pltpu.roll(zh, tm - 1, 0))
    sw = sw_ref[...]
    zc = z_m1 * sw[0:1] + zh * sw[1:2] + z_p1 * sw[2:3] + sb_ref[...]
    v_ref[0] = zc[:, 0:c_hy].astype(BF16)
    x1_ref[0] = zc[:, c_hy:2 * c_hy].astype(BF16)
    x2_ref[0] = zc[:, 2 * c_hy:3 * c_hy].astype(BF16)
    zq = _dot(h, wqkv_ref[...])
    q_ref[0] = (zq[:, 0:c_na] * q_scale).astype(BF16)
    k_ref[0] = zq[:, c_na:2 * c_na].astype(BF16)
    va_ref[0] = zq[:, 2 * c_na:3 * c_na].astype(BF16)


def _inproj(x, g, shift, scale, w_hy, w_qkv, short_w, short_b, tm=512):
    b, s, d = x.shape
    c_hy = w_hy.shape[1] // 3
    c_na = w_qkv.shape[1] // 3
    n_tiles = s // tm
    r8 = tm // 8
    body = functools.partial(_inproj_body, n_tiles=n_tiles, q_scale=NA_HEAD_DIM ** -0.5,
                             c_hy=c_hy, c_na=c_na)
    tok = lambda c: pl.BlockSpec((1, tm, c), lambda bi, i: (bi, i, 0))
    full2 = lambda a: pl.BlockSpec(a.shape, lambda bi, i: (0, 0))
    per_b = pl.BlockSpec((1, 1, d), lambda bi, i: (bi, 0, 0))
    return pl.pallas_call(
        body,
        out_shape=[jax.ShapeDtypeStruct((b, s, c_hy), BF16)] * 3 + [jax.ShapeDtypeStruct((b, s, c_na), BF16)] * 3,
        grid=(b, n_tiles),
        in_specs=[tok(d),
                  pl.BlockSpec((1, 8, d), lambda bi, i: (bi, jnp.maximum(i * r8 - 1, 0), 0)),
                  pl.BlockSpec((1, 8, d), lambda bi, i: (bi, jnp.minimum((i + 1) * r8, s // 8 - 1), 0)),
                  full2(g), per_b, per_b, full2(w_hy), full2(w_qkv), full2(short_w), full2(short_b)],
        out_specs=[tok(c_hy)] * 3 + [tok(c_na)] * 3,
        compiler_params=_cparams(("parallel", "parallel")),
        name="inproj",
    )(x, x, x, g, shift, scale, w_hy, w_qkv, short_w, short_b)


def _ctxkv_body(x_ref, g_ref, sh_ref, sc_ref, w_ref, k_ref, v_ref, *, c_na):
    h = _norm_mod(x_ref[0], g_ref[...], sh_ref[0], sc_ref[0]).astype(BF16)
    z = _dot(h, w_ref[...])
    k_ref[0] = z[:, 0:c_na].astype(BF16)
    v_ref[0] = z[:, c_na:2 * c_na].astype(BF16)


def _ctxkv(ctx, g, shift, scale, w_kv):
    b, n, d = ctx.shape
    c_na = w_kv.shape[1] // 2
    one = pl.BlockSpec((1, 1, d), lambda bi: (0, 0, 0))
    return pl.pallas_call(
        functools.partial(_ctxkv_body, c_na=c_na),
        out_shape=[jax.ShapeDtypeStruct((b, n, c_na), BF16)] * 2,
        grid=(b,),
        in_specs=[pl.BlockSpec((1, n, d), lambda bi: (bi, 0, 0)),
                  pl.BlockSpec(g.shape, lambda bi: (0, 0)), one, one,
                  pl.BlockSpec(w_kv.shape, lambda bi: (0, 0))],
        out_specs=[pl.BlockSpec((1, n, c_na), lambda bi: (bi, 0, 0))] * 2,
        compiler_params=_cparams(("parallel",)),
        name="ctxkv",
    )(ctx, g, shift, scale, w_kv)


NA_HEADS_PER_BLK = 4


def _na_bias_body(r_ref, e_ref, ok_ref, o_ref):
    t = jnp.dot(r_ref[...], e_ref[...], precision=HIGHEST, preferred_element_type=F32)
    o_ref[...] = jnp.where(ok_ref[...] > 0.5, t, NEG_INF)


def _na_bias_table(rpb):
    w = GRID_W
    h, nr, nc = rpb.shape
    col = np.arange(w)[:, None]
    kc = np.arange(w)[None, :]
    c_start = np.clip(col - NA_WIN_C // 2, 0, w - NA_WIN_C)
    valid = ((kc >= c_start) & (kc < c_start + NA_WIN_C)).reshape(1, w * w)
    expand = (np.arange(32)[:, None, None] == (kc - col + NA_WIN_C - 1)[None]).reshape(32, w * w)
    rp = jnp.pad(rpb.astype(F32).reshape(h * nr, nc), ((0, 0), (0, 32 - nc)))
    full = lambda a: pl.BlockSpec(a.shape, lambda: (0,) * a.ndim)
    expand = jnp.asarray(expand, dtype=F32)
    ok = jnp.asarray(valid, dtype=F32)
    toep = pl.pallas_call(
        _na_bias_body,
        out_shape=jax.ShapeDtypeStruct((h * nr, w * w), F32),
        in_specs=[full(rp), full(expand), full(ok)],
        out_specs=pl.BlockSpec((h * nr, w * w), lambda: (0, 0)),
        name="na_bias",
    )(rp, expand, ok)
    t2 = toep.reshape(h, nr, w, w).transpose(0, 2, 1, 3).reshape(h, w, nr * w)
    slabs = jnp.stack([t2[:, :, (NA_WIN_R - 1 - off) * w:(2 * NA_WIN_R - 1 - off) * w]
                       for off in range(NA_WIN_R)], axis=1)
    hpb = NA_HEADS_PER_BLK
    slabs = slabs.reshape(h // hpb, hpb, NA_WIN_R, w, NA_WIN_R * w).transpose(0, 2, 1, 3, 4)
    return slabs.reshape(h // hpb, NA_WIN_R, hpb * w, NA_WIN_R * w)


def _natt_body(q_ref, k_ref, v_ref, kc_ref, vc_ref, bias_ref, o_ref, *, rows):
    w = GRID_W
    hpb = NA_HEADS_PER_BLK
    nloc = NA_WIN_R * w
    lane = lax.broadcasted_iota(jnp.int32, (1, hpb * NA_HEAD_DIM), 1)
    in_head = [(lane >= NA_HEAD_DIM * hh) & (lane < NA_HEAD_DIM * (hh + 1)) for hh in range(hpb)]
    kcx = kc_ref[0]
    vcx = vc_ref[0]
    nt = (((1,), (1,)), ((), ()))

    def one_row(r):
        r0 = jnp.clip(r - NA_WIN_R // 2, 0, rows - NA_WIN_R)
        off = r - r0
        qs = q_ref[0, pl.ds(pl.multiple_of(r * w, w), w), :]
        kw = k_ref[0, pl.ds(pl.multiple_of(r0 * w, w), nloc), :]
        vw = v_ref[0, pl.ds(pl.multiple_of(r0 * w, w), nloc), :]
        zero = jnp.zeros_like(qs)
        qst = jnp.concatenate([jnp.where(m, qs, zero) for m in in_head], axis=0)
        s_loc = lax.dot_general(qst, kw, nt, preferred_element_type=F32) + bias_ref[0, off]
        s_ctx = lax.dot_general(qst, kcx, nt, preferred_element_type=F32)
        m = jnp.maximum(jnp.max(s_loc, axis=-1, keepdims=True), jnp.max(s_ctx, axis=-1, keepdims=True))
        p_loc = jnp.exp(s_loc - m)
        p_ctx = jnp.exp(s_ctx - m)
        den = jnp.sum(p_loc, axis=-1, keepdims=True) + jnp.sum(p_ctx, axis=-1, keepdims=True)
        o = (_dot(p_loc.astype(BF16), vw) + _dot(p_ctx.astype(BF16), vcx)) * (1.0 / den)
        acc = jnp.where(in_head[0], o[0:w], 0.0)
        for hh in range(1, hpb):
            acc = acc + jnp.where(in_head[hh], o[hh * w:(hh + 1) * w], 0.0)
        o_ref[0, pl.ds(pl.multiple_of(r * w, w), w), :] = acc.astype(BF16)

    def row_pair(i, carry):
        one_row(2 * i)
        one_row(2 * i + 1)
        return carry

    lax.fori_loop(0, rows // 2, row_pair, 0)


def _natt(q, k, v, kc, vc, bias):
    b, s, c = q.shape
    nctx = kc.shape[1]
    lw = NA_HEADS_PER_BLK * NA_HEAD_DIM
    rows = s // GRID_W
    seq = pl.BlockSpec((1, s, lw), lambda bi, g: (bi, 0, g))
    cx = pl.BlockSpec((1, nctx, lw), lambda bi, g: (bi, 0, g))
    return pl.pallas_call(
        functools.partial(_natt_body, rows=rows),
        out_shape=jax.ShapeDtypeStruct((b, s, c), BF16),
        grid=(b, c // lw),
        in_specs=[seq, seq, seq, cx, cx,
                  pl.BlockSpec((1,) + bias.shape[1:], lambda bi, g: (g, 0, 0, 0))],
        out_specs=seq,
        compiler_params=_cparams(("parallel", "parallel")),
        name="natt",
    )(q, k, v, kc, vc, bias)


def _hyena_feats(seq_len):
    t = jnp.linspace(0.0, 1.0, seq_len, dtype=F32)[:, None]
    bands = jnp.linspace(1e-4, HYENA_BANDS - 1, HYENA_BANDS, dtype=F32)
    ang = (2.0 * math.pi / seq_len) * jnp.arange(seq_len, dtype=F32)[:, None] * bands[None, :]
    feats = jnp.concatenate([t, jnp.cos(ang), -jnp.sin(ang)], axis=-1)
    return jnp.pad(feats, ((0, 0), (0, 128 - HYENA_EMB)))


def _filt_body(feat_ref, w1_ref, b1_ref, w2_ref, b2_ref, w3_ref, fr_ref, dl_ref, o_ref, l1_ref, h_scr):
    j = pl.program_id(0)
    hp = functools.partial(jnp.dot, precision=HIGHEST, preferred_element_type=F32)
    feats = feat_ref[...]

    @pl.when(j == 0)
    def _():
        fr = fr_ref[...]
        h = jnp.sin(fr[0:1] * (hp(feats, w1_ref[...]) + b1_ref[...]))
        h_scr[...] = jnp.sin(fr[1:2] * (hp(h, w2_ref[...]) + b2_ref[...]))

    hc = hp(h_scr[...], w3_ref[...])
    t = feats[:, 0:1]
    hc = hc * jnp.exp(-t * dl_ref[...])
    row = lax.broadcasted_iota(jnp.int32, hc.shape, 0)
    hc = jnp.where((row == 0) & (j % 2 == 1), 0.0, hc)
    l1_ref[0] = jnp.sum(jnp.abs(hc), axis=0, keepdims=True)
    o_ref[0] = hc.astype(BF16)


def _hyena_filter_taps(seq_len, f_w1, f_b1, f_w2, f_b2, f_w3, f_freq, c_hy):
    feats = _hyena_feats(seq_len)
    hid = f_w1.shape[1]
    w1 = jnp.pad(f_w1.astype(F32), ((0, 128 - HYENA_EMB), (0, 0)))
    deltas = jnp.abs(jnp.linspace(math.log(HYENA_TARGET) / HYENA_SLOW_DECAY,
                                  math.log(HYENA_TARGET) / HYENA_FAST_DECAY, c_hy, dtype=F32))[None, :]
    nblk = f_w3.shape[1] // c_hy
    c0 = lambda a: pl.BlockSpec(a.shape, lambda j: (0, 0))
    b1, b2 = f_b1.reshape(1, hid), f_b2.reshape(1, hid)
    return pl.pallas_call(
        _filt_body,
        out_shape=[jax.ShapeDtypeStruct((nblk, seq_len, c_hy), BF16),
                   jax.ShapeDtypeStruct((nblk, 1, c_hy), F32)],
        grid=(nblk,),
        in_specs=[c0(feats), c0(w1), c0(b1), c0(f_w2), c0(b2),
                  pl.BlockSpec((hid, c_hy), lambda j: (0, j)), c0(f_freq), c0(deltas)],
        out_specs=[pl.BlockSpec((1, seq_len, c_hy), lambda j: (j, 0, 0)),
                   pl.BlockSpec((1, 1, c_hy), lambda j: (j, 0, 0))],
        scratch_shapes=[pltpu.VMEM((seq_len, hid), F32)],
        compiler_params=_cparams(("arbitrary",)),
        name="hyena_filter",
    )(feats, w1, b1, f_w2, b2, f_w3, f_freq, deltas)


def _conv_dft_constants():
    a_half = FFT_A // 2
    n = FFT_A * FFT_R
    ka = np.arange(FFT_KA)[:, None]
    a = np.arange(a_half)[None, :]
    ph = 2.0 * np.pi * ka * a / FFT_A
    m_fwd = np.zeros((2 * FFT_KA_PAD, a_half))
    m_fwd[:FFT_KA] = np.cos(ph)
    m_fwd[FFT_KA_PAD:FFT_KA_PAD + FFT_KA] = -np.sin(ph)
    wgt = np.where((ka == 0) | (ka == FFT_A // 2), 1.0, 2.0)
    m_inv = np.zeros((a_half, 2 * FFT_KA_PAD))
    m_inv[:, :FFT_KA] = (wgt * np.cos(ph)).T / n
    m_inv[:, FFT_KA_PAD:FFT_KA_PAD + FFT_KA] = (-wgt * np.sin(ph)).T / n
    kb = np.arange(FFT_R)[None, :, None]
    b = np.arange(FFT_R)[None, None, :]
    kaa = np.arange(FFT_KA)[:, None, None]
    th = 2.0 * np.pi * (b * kb / FFT_R + b * kaa / n)
    gr, gi = np.cos(th), -np.sin(th)
    g2 = np.zeros((FFT_KA_PAD, 2 * FFT_R, 2 * FFT_R))
    g2[:FFT_KA] = np.block([[gr, -gi], [gi, gr]])
    grt, git = gr.transpose(0, 2, 1), gi.transpose(0, 2, 1)
    g2h = np.zeros_like(g2)
    g2h[:FFT_KA] = np.block([[grt, git], [-git, grt]])
    return _mxu_const(m_fwd), _mxu_const(m_inv), _mxu_const(g2), _mxu_const(g2h)


def _fwd1_body(m_ref, u_ref, o_ref):
    res = _dot(m_ref[...], u_ref[0])
    o_ref[0, 0] = res[0:FFT_KA_PAD]
    o_ref[0, 1] = res[FFT_KA_PAD:2 * FFT_KA_PAD]


def _conv_fwd1(u, m_fwd, cb=8192):
    n, seq, c = u.shape
    a_half = FFT_A // 2
    cols = seq * c // a_half
    uv = u.reshape(n, a_half, cols)
    return pl.pallas_call(
        _fwd1_body,
        out_shape=jax.ShapeDtypeStruct((n, 2, FFT_KA_PAD, cols), F32),
        grid=(n, cols // cb),
        in_specs=[pl.BlockSpec(m_fwd.shape, lambda i, j: (0, 0)),
                  pl.BlockSpec((1, a_half, cb), lambda i, j: (i, 0, j))],
        out_specs=pl.BlockSpec((1, 2, FFT_KA_PAD, cb), lambda i, j: (i, 0, 0, j)),
        compiler_params=_cparams(("parallel", "parallel")),
        name="conv_fwd1",
    )(m_fwd, uv)


FFT_KB = 8


def _rows_to_slabs(src_ref, dst_scr, c):
    for part in range(2):
        for b in range(FFT_R):
            dst_scr[part, :, b, :] = src_ref[0, part, :, b * c:(b + 1) * c]


def _slabs_to_rows(src_scr, dst_ref, c):
    for part in range(2):
        for b in range(FFT_R):
            dst_ref[0, part, :, b * c:(b + 1) * c] = src_scr[part, :, b, :]


def _slab(scr, i):
    return jnp.concatenate([scr[0, i], scr[1, i]], axis=0).astype(BF16)


def _fwd2f_body(sf_ref, sb_ref, g_ref, l1_ref, kf_ref, f3, b3):
    o = pl.program_id(0)
    j = pl.program_id(1)
    r2 = 2 * FFT_R
    c = kf_ref.shape[-1]
    _rows_to_slabs(sf_ref, f3, c)
    _rows_to_slabs(sb_ref, b3, c)
    inv = 1.0 / (l1_ref[2 * o] + l1_ref[2 * o + 1] + EPS)
    for i in range(FFT_KB):
        @pl.when(j * FFT_KB + i < FFT_KA)
        def _():
            xf = _dot(g_ref[i], _slab(f3, i))
            xb = _dot(g_ref[i], _slab(b3, i))
            kf_ref[0, i, 0:FFT_R] = (xf[0:FFT_R] + xb[0:FFT_R]) * inv
            kf_ref[0, i, FFT_R:r2] = (xf[FFT_R:r2] - xb[FFT_R:r2]) * inv

        @pl.when(j * FFT_KB + i >= FFT_KA)
        def _():
            kf_ref[0, i] = jnp.zeros((r2, c), F32)


def _filter_spectrum(s_filt, l1, g2, c):
    n_ord = s_filt.shape[0] // 2
    cols = s_filt.shape[-1]
    r2 = 2 * FFT_R
    return pl.pallas_call(
        _fwd2f_body,
        out_shape=jax.ShapeDtypeStruct((n_ord, FFT_KA_PAD, r2, c), F32),
        grid=(n_ord, FFT_KA_PAD // FFT_KB),
        in_specs=[pl.BlockSpec((1, 2, FFT_KB, cols), lambda o, j: (2 * o, 0, j, 0)),
                  pl.BlockSpec((1, 2, FFT_KB, cols), lambda o, j: (2 * o + 1, 0, j, 0)),
                  pl.BlockSpec((FFT_KB, r2, r2), lambda o, j: (j, 0, 0)),
                  pl.BlockSpec(l1.shape, lambda o, j: (0, 0, 0))],
        out_specs=pl.BlockSpec((1, FFT_KB, r2, c), lambda o, j: (o, j, 0, 0)),
        scratch_shapes=[pltpu.VMEM((2, FFT_KB, FFT_R, c), F32)] * 2,
        compiler_params=_cparams(("parallel", "parallel")),
        name="filter_spectrum",
    )(s_filt, s_filt, g2, l1)


def _mid_body(s_ref, g_ref, gh_ref, kf_ref, t_ref, s3, t3):
    j = pl.program_id(1)
    r2 = 2 * FFT_R
    c = kf_ref.shape[-1]
    _rows_to_slabs(s_ref, s3, c)
    for i in range(FFT_KB):
        @pl.when(j * FFT_KB + i < FFT_KA)
        def _():
            x = _dot(g_ref[i], _slab(s3, i))
            xr, xi = x[0:FFT_R], x[FFT_R:r2]
            kr, ki = kf_ref[0, i, 0:FFT_R], kf_ref[0, i, FFT_R:r2]
            y = jnp.concatenate([xr * kr - xi * ki, xr * ki + xi * kr], axis=0).astype(BF16)
            t = _dot(gh_ref[i], y)
            t3[0, i] = t[0:FFT_R]
            t3[1, i] = t[FFT_R:r2]

        @pl.when(j * FFT_KB + i >= FFT_KA)
        def _():
            t3[0, i] = jnp.zeros((FFT_R, c), F32)
            t3[1, i] = jnp.zeros((FFT_R, c), F32)

    _slabs_to_rows(t3, t_ref, c)


def _conv_mid(s, kf, order, g2, g2h, c):
    n, _, _, cols = s.shape
    r2 = 2 * FFT_R
    blk = pl.BlockSpec((1, 2, FFT_KB, cols), lambda i, j: (i, 0, j, 0))
    gspec = pl.BlockSpec((FFT_KB, r2, r2), lambda i, j: (j, 0, 0))
    return pl.pallas_call(
        _mid_body,
        out_shape=jax.ShapeDtypeStruct(s.shape, F32),
        grid=(n, FFT_KA_PAD // FFT_KB),
        in_specs=[blk, gspec, gspec,
                  pl.BlockSpec((1, FFT_KB, r2, c), lambda i, j: (order, j, 0, 0))],
        out_specs=blk,
        scratch_shapes=[pltpu.VMEM((2, FFT_KB, FFT_R, c), F32)] * 2,
        compiler_params=_cparams(("parallel", "parallel")),
        name="conv_mid",
    )(s, g2, g2h, kf)


def _inv1_body(m_ref, t_ref, u_ref, xg_ref, sk_ref, o_ref):
    cb = t_ref.shape[-1]
    t2 = t_ref[0].reshape(2 * FFT_KA_PAD, cb).astype(BF16)
    y = _dot(m_ref[...], t2)
    u = u_ref[0].astype(F32)
    o_ref[0] = (xg_ref[0].astype(F32) * (y + u * sk_ref[...])).astype(BF16)


def _conv_inv1(t, u, xg, skip, m_inv, cb=8192):
    n, seq, c = u.shape
    a_half = FFT_A // 2
    cols = seq * c // a_half
    sk = jnp.tile(skip.astype(F32).reshape(1, c), (1, cols // c))
    uspec = pl.BlockSpec((1, a_half, cb), lambda i, j: (i, 0, j))
    out = pl.pallas_call(
        _inv1_body,
        out_shape=jax.ShapeDtypeStruct((n, a_half, cols), BF16),
        grid=(n, cols // cb),
        in_specs=[pl.BlockSpec(m_inv.shape, lambda i, j: (0, 0)),
                  pl.BlockSpec((1, 2, FFT_KA_PAD, cb), lambda i, j: (i, 0, 0, j)),
                  uspec, uspec,
                  pl.BlockSpec((1, cb), lambda i, j: (0, j))],
        out_specs=uspec,
        compiler_params=_cparams(("parallel", "parallel")),
        name="conv_inv1",
    )(m_inv, t, u.reshape(n, a_half, cols), xg.reshape(n, a_half, cols), sk)
    return out.reshape(n, seq, c)


def _hyena(v, x1, x2, f_w1, f_b1, f_w2, f_b2, f_w3, f_freq, skip):
    _, seq, c = v.shape
    assert 2 * seq == FFT_A * FFT_R
    m_fwd, m_inv, g2, g2h = _conv_dft_constants()
    taps, l1 = _hyena_filter_taps(seq, f_w1, f_b1, f_w2, f_b2, f_w3, f_freq, c)
    kf = _filter_spectrum(_conv_fwd1(taps, m_fwd), l1, g2, c)
    y = v
    for order, xg in enumerate((x1, x2)):
        t = _conv_mid(_conv_fwd1(y, m_fwd), kf, order, g2, g2h, c)
        y = _conv_inv1(t, y, xg, skip[order], m_inv)
    return y


def _mix_ffn_body(x_ref, a1_ref, a2_ref, wm_ref, gm_ref, g_ref, sh_ref, sc_ref, gt_ref,
                  w1_ref, w3_ref, w2_ref, o_ref, x_scr, h_scr, acc_scr):
    j = pl.program_id(2)

    @pl.when(j == 0)
    def _():
        c1 = a1_ref.shape[-1]
        mixed = _dot(a1_ref[0], wm_ref[0:c1]) + _dot(a2_ref[0], wm_ref[c1:])
        xm = x_ref[0] + gm_ref[0] * mixed
        x_scr[...] = xm
        h_scr[...] = _norm_mod(xm, g_ref[...], sh_ref[0], sc_ref[0]).astype(BF16)
        acc_scr[...] = jnp.zeros_like(acc_scr)

    h = h_scr[...]
    a = _dot(h, w1_ref[...])
    u = (a * jax.nn.sigmoid(a) * _dot(h, w3_ref[...])).astype(BF16)
    acc_scr[...] += _dot(u, w2_ref[...])

    @pl.when(j == pl.num_programs(2) - 1)
    def _():
        o_ref[0] = x_scr[...] + gt_ref[0] * acc_scr[...]


def _mix_ffn(x, a1, a2, w_mix, gate_mix, g, shift, scale, gate, w1, w3, w2, tm=512, fb=1408):
    b, s, d = x.shape
    f = w1.shape[1]
    tok = lambda c: pl.BlockSpec((1, tm, c), lambda bi, i, j: (bi, i, 0))
    per_b = pl.BlockSpec((1, 1, d), lambda bi, i, j: (bi, 0, 0))
    const = lambda a: pl.BlockSpec(a.shape, lambda bi, i, j: (0, 0))
    return pl.pallas_call(
        _mix_ffn_body,
        out_shape=jax.ShapeDtypeStruct(x.shape, F32),
        grid=(b, s // tm, f // fb),
        in_specs=[tok(d), tok(a1.shape[-1]), tok(a2.shape[-1]), const(w_mix), per_b,
                  const(g), per_b, per_b, per_b,
                  pl.BlockSpec((d, fb), lambda bi, i, j: (0, j)),
                  pl.BlockSpec((d, fb), lambda bi, i, j: (0, j)),
                  pl.BlockSpec((fb, d), lambda bi, i, j: (j, 0))],
        out_specs=tok(d),
        scratch_shapes=[pltpu.VMEM((tm, d), F32), pltpu.VMEM((tm, d), BF16), pltpu.VMEM((tm, d), F32)],
        compiler_params=_cparams(("parallel", "parallel", "arbitrary")),
        name="mix_ffn",
    )(x, a1, a2, w_mix, gate_mix, g, shift, scale, gate, w1, w3, w2)


def _fm_constants(cg):
    j = np.arange(cg)[:, None]
    m = np.arange(cg)[None, :]
    ph = 2.0 * np.pi * j * m / cg
    w_cs = np.concatenate([np.cos(ph), np.sin(ph)], axis=1)
    d = np.arange(FM_A)[:, None]
    a = np.arange(FM_A)[None, :]
    ph = 2.0 * np.pi * d * a / FM_A
    fr, fi = np.cos(ph), -np.sin(ph)
    m1 = np.block([[fr, fi], [fi, -fr]])
    n = FM_A * FM_A
    dd = np.arange(FM_A)[:, None, None]
    c = np.arange(FM_A)[None, :, None]
    b = np.arange(FM_A)[None, None, :]
    th = 2.0 * np.pi * (b * c / FM_A + b * dd / n)
    gcat = np.concatenate([np.cos(th), np.sin(th)], axis=2)
    return _mxu_const(w_cs), _mxu_const(m1), _mxu_const(gcat)


def _fm_chan_body(x_ref, g_ref, sh_ref, sc_ref, w_ref, o_ref, *, cg):
    h = _norm_mod(x_ref[0], g_ref[...], sh_ref[0], sc_ref[0]).astype(BF16)
    for grp in range(h.shape[-1] // cg):
        pq = _dot(h[:, grp * cg:(grp + 1) * cg], w_ref[...])
        o_ref[0, 0, :, grp * cg:(grp + 1) * cg] = pq[:, 0:cg].astype(BF16)
        o_ref[0, 1, :, grp * cg:(grp + 1) * cg] = pq[:, cg:2 * cg].astype(BF16)


def _fm_chan(x, g, shift, scale, w_cs, tm=512):
    b, s, d = x.shape
    cg = w_cs.shape[0]
    per_b = pl.BlockSpec((1, 1, d), lambda bi, i: (bi, 0, 0))
    return pl.pallas_call(
        functools.partial(_fm_chan_body, cg=cg),
        out_shape=jax.ShapeDtypeStruct((b, 2, s, d), BF16),
        grid=(b, s // tm),
        in_specs=[pl.BlockSpec((1, tm, d), lambda bi, i: (bi, i, 0)),
                  pl.BlockSpec(g.shape, lambda bi, i: (0, 0)), per_b, per_b,
                  pl.BlockSpec(w_cs.shape, lambda bi, i: (0, 0))],
        out_specs=pl.BlockSpec((1, 2, tm, d), lambda bi, i: (bi, 0, i, 0)),
        compiler_params=_cparams(("parallel", "parallel")),
        name="fm_chan",
    )(x, g, shift, scale, w_cs)


def _fm_s1_body(m_ref, pq_ref, o_ref):
    o_ref[0] = _dot(m_ref[...], pq_ref[0]).astype(BF16)


def _fm_stage1(pq, m1, cb=8192):
    b, _, s, d = pq.shape
    cols = s * d // FM_A
    blk = pl.BlockSpec((1, 2 * FM_A, cb), lambda bi, j: (bi, 0, j))
    return pl.pallas_call(
        _fm_s1_body,
        out_shape=jax.ShapeDtypeStruct((b, 2 * FM_A, cols), BF16),
        grid=(b, cols // cb),
        in_specs=[pl.BlockSpec(m1.shape, lambda bi, j: (0, 0)), blk],
        out_specs=blk,
        compiler_params=_cparams(("parallel", "parallel")),
        name="fm_stage1",
    )(m1, pq.reshape(b, 2 * FM_A, cols))


def _fm_s2_body(s_ref, g_ref, o_ref, *, dblk, scale):
    for i in range(dblk):
        s2 = jnp.concatenate([s_ref[0, 0, i], s_ref[0, 1, i]], axis=0)
        o_ref[:, i, :] = _dot(g_ref[i], s2) * scale


def _fm_stage2(s1, gcat, seq, d, dblk=8):
    b = s1.shape[0]
    sv = s1.reshape(b, 2, FM_A, FM_A, d)
    scale = 1.0 / math.sqrt(seq * (d // F_GROUPS))
    out = pl.pallas_call(
        functools.partial(_fm_s2_body, dblk=dblk, scale=scale),
        out_shape=jax.ShapeDtypeStruct((b * FM_A, dblk * (FM_A // dblk), d), F32),
        grid=(b, FM_A // dblk),
        in_specs=[pl.BlockSpec((1, 2, dblk, FM_A, d), lambda bi, j: (bi, 0, j, 0, 0)),
                  pl.BlockSpec((dblk, FM_A, 2 * FM_A), lambda bi, j: (j, 0, 0))],
        out_specs=pl.BlockSpec((FM_A, dblk, d), lambda bi, j: (bi, j, 0)),
        compiler_params=_cparams(("parallel", "parallel")),
        name="fm_stage2",
    )(sv, gcat)
    return out.reshape(b, seq, d)


def _fourier_mix(x, g, shift, scale):
    b, s, d = x.shape
    assert s == FM_A * FM_A
    w_cs, m1, gcat = _fm_constants(d // F_GROUPS)
    pq = _fm_chan(x, g, shift, scale, w_cs)
    return _fm_stage2(_fm_stage1(pq, m1), gcat, s, d)


LANES = 128
MOE_TM = 1024
DMA_WINDOW = 128


def _router_body(x_ref, yf_ref, wf_ref, gf_ref, g_ref, sh_ref, sc_ref, wr_ref, br_ref,
                 xo_ref, h_ref, meta_ref, gw_ref, cnt_ref, carry):
    i = pl.program_id(0)

    @pl.when(i == 0)
    def _():
        carry[...] = jnp.zeros_like(carry)

    xm = x_ref[...] + gf_ref[0] * _dot(yf_ref[...].astype(BF16), wf_ref[...])
    xo_ref[...] = xm
    h = _norm_mod(xm, g_ref[...], sh_ref[0], sc_ref[0])
    for sl in range(h.shape[-1] // LANES):
        h_ref[:, sl, :] = h[:, sl * LANES:(sl + 1) * LANES]
    logits = jnp.dot(h, wr_ref[...], precision=HIGHEST, preferred_element_type=F32) + br_ref[...]
    lane = lax.broadcasted_iota(jnp.int32, logits.shape, 1)
    nl = logits.shape[-1]
    m1 = jnp.max(logits, axis=-1, keepdims=True)
    i1 = jnp.min(jnp.where(logits == m1, lane, nl), axis=-1, keepdims=True)
    rest = jnp.where(lane == i1, -3.0e38, logits)
    m2 = jnp.max(rest, axis=-1, keepdims=True)
    i2 = jnp.min(jnp.where(rest == m2, lane, nl), axis=-1, keepdims=True)
    e = jnp.exp(m2 - m1)
    gw_ref[...] = jnp.where(lane == 0, 1.0 / (1.0 + e), jnp.where(lane == 1, e / (1.0 + e), 0.0))
    onehot = jnp.where((lane == i1) | (lane == i2), 1.0, 0.0)
    tm = onehot.shape[0]
    earlier = lax.broadcasted_iota(jnp.int32, (tm, tm), 0) > lax.broadcasted_iota(jnp.int32, (tm, tm), 1)
    excl = _dot(jnp.where(earlier, 1.0, 0.0).astype(BF16), onehot.astype(BF16)) + carry[...]
    r1 = jnp.sum(jnp.where(lane == i1, excl, 0.0), axis=-1, keepdims=True).astype(jnp.int32)
    r2 = jnp.sum(jnp.where(lane == i2, excl, 0.0), axis=-1, keepdims=True).astype(jnp.int32)
    meta_ref[...] = jnp.where(lane == 0, i1, jnp.where(lane == 1, i2, jnp.where(lane == 2, r1, jnp.where(lane == 3, r2, 0))))
    carry[...] = carry[...] + jnp.sum(onehot, axis=0, keepdims=True)
    cnt_ref[...] = carry[...]


def _router(x, y_f, w_f, gate_f, g, shift, scale, w_router, b_router, tm=512):
    b, s, d = x.shape
    t = b * s
    ne = w_router.shape[1]
    wr = jnp.pad(w_router.astype(F32), ((0, 0), (0, LANES - ne)))
    br = jnp.pad(b_router.astype(F32).reshape(1, ne), ((0, 0), (0, LANES - ne)), constant_values=NEG_INF)
    spt = s // tm
    per_b = pl.BlockSpec((1, 1, d), lambda i: (i // spt, 0, 0))
    const = lambda a: pl.BlockSpec(a.shape, lambda i: (0, 0))
    tok = pl.BlockSpec((tm, d), lambda i: (i, 0))
    return pl.pallas_call(
        _router_body,
        out_shape=[jax.ShapeDtypeStruct((t, d), F32),
                   jax.ShapeDtypeStruct((t, d // LANES, LANES), F32),
                   jax.ShapeDtypeStruct((t, LANES), jnp.int32),
                   jax.ShapeDtypeStruct((t, LANES), F32),
                   jax.ShapeDtypeStruct((1, LANES), F32)],
        grid=(t // tm,),
        in_specs=[tok, tok, const(w_f), per_b, const(g), per_b, per_b, const(wr), const(br)],
        out_specs=[tok,
                   pl.BlockSpec((tm, d // LANES, LANES), lambda i: (i, 0, 0)),
                   pl.BlockSpec((tm, LANES), lambda i: (i, 0)),
                   pl.BlockSpec((tm, LANES), lambda i: (i, 0)),
                   pl.BlockSpec((1, LANES), lambda i: (0, 0))],
        scratch_shapes=[pltpu.VMEM((1, LANES), F32)],
        compiler_params=_cparams(("arbitrary",)),
        name="router",
    )(x.reshape(t, d), y_f.reshape(t, d), w_f, gate_f, g, shift, scale, wr, br)


def _moe_plan(meta, counts, ne, tm):
    i1, i2, r1, r2 = meta[:, 0], meta[:, 1], meta[:, 2], meta[:, 3]
    cnt = counts[0, :ne].astype(jnp.int32)
    padded = ((cnt + tm - 1) // tm) * tm
    ends = jnp.cumsum(padded)
    offs = ends - padded
    pick = lambda idx: sum(jnp.where(idx == e, offs[e], 0) for e in range(ne))
    pos = jnp.concatenate([pick(i1) + r1, pick(i2) + r2]).astype(jnp.int32)
    n_tiles = (2 * meta.shape[0]) // tm + ne
    n_used = (ends[ne - 1] // tm).astype(jnp.int32)
    tile_start = jnp.minimum(jnp.arange(n_tiles, dtype=jnp.int32), n_used - 1) * tm
    tile_expert = jnp.sum(tile_start[:, None] >= ends[None, :], axis=1).astype(jnp.int32)
    return pos, offs + cnt, padded - cnt, tile_expert, n_used.reshape(1)


def _windowed_copies(n, start_copy, wait_one, per_iter):
    def body(i, carry):
        @pl.when(i >= DMA_WINDOW)
        def _():
            for _ in range(per_iter):
                wait_one()
        start_copy(i)
        return carry

    lax.fori_loop(0, n, body, 0)

    def drain(i, carry):
        for _ in range(per_iter):
            wait_one()
        return carry

    lax.fori_loop(0, jnp.minimum(n, DMA_WINDOW), drain, 0)


def _dispatch_body(pos_ref, pad_start_ref, pad_n_ref, h_ref, xs_hbm, sem, *, n_tok, ne):
    i = pl.program_id(0)
    td = h_ref.shape[0]
    base = i * td
    copy = lambda src, dst: pltpu.make_async_copy(h_ref.at[src], xs_hbm.at[dst], sem)
    wait_one = lambda: copy(0, 0).wait()

    def start_token(r, carry):
        copy(r, pos_ref[base + r]).start()
        copy(r, pos_ref[n_tok + base + r]).start()
        return carry

    lax.fori_loop(0, td, start_token, 0, unroll=8)
    whole_tile = pltpu.make_async_copy(h_ref, xs_hbm.at[pl.ds(0, td)], sem)
    whole_tile.wait()
    whole_tile.wait()

    @pl.when(i == 0)
    def _():
        for e in range(ne):
            first = pad_start_ref[e]
            _windowed_copies(pad_n_ref[e], lambda r: copy(0, first + r).start(), wait_one, 1)


def _moe_dispatch(h3, pos, pad_start, pad_n, n_rows, td=1024):
    n_tok = h3.shape[0]
    ne = pad_start.shape[0]
    return pl.pallas_call(
        functools.partial(_dispatch_body, n_tok=n_tok, ne=ne),
        out_shape=jax.ShapeDtypeStruct((n_rows,) + h3.shape[1:], h3.dtype),
        grid_spec=pltpu.PrefetchScalarGridSpec(
            num_scalar_prefetch=3, grid=(n_tok // td,),
            in_specs=[pl.BlockSpec((td,) + h3.shape[1:], lambda i, p, ps, pn: (i, 0, 0))],
            out_specs=pl.BlockSpec(memory_space=pl.ANY),
            scratch_shapes=[pltpu.SemaphoreType.DMA(())]),
        compiler_params=_cparams(("arbitrary",)),
        name="moe_dispatch",
    )(pos, pad_start, pad_n, h3)


def _moe_grouped_body(te_ref, nu_ref, xs_ref, w1_ref, w3_ref, w2_ref, y_ref, xb_scr, acc_scr):
    i = pl.program_id(0)
    j = pl.program_id(1)
    nsl = xs_ref.shape[1]

    @pl.when(i < nu_ref[0])
    def _():
        @pl.when(j == 0)
        def _():
            xb_scr[...] = jnp.concatenate([xs_ref[:, sl, :] for sl in range(nsl)], axis=1).astype(BF16)
            acc_scr[...] = jnp.zeros_like(acc_scr)

        h = xb_scr[...]
        a = _dot(h, w1_ref[0].astype(BF16))
        u = (a * jax.nn.sigmoid(a) * _dot(h, w3_ref[0].astype(BF16))).astype(BF16)
        acc_scr[...] += _dot(u, w2_ref[0].astype(BF16))

        @pl.when(j == pl.num_programs(1) - 1)
        def _():
            for sl in range(nsl):
                y_ref[:, sl, :] = acc_scr[:, sl * LANES:(sl + 1) * LANES]


def _moe_grouped(xs, tile_expert, n_used, w1, w3, w2, tm, fb=512):
    n_rows, nsl, _ = xs.shape
    ne, d, f = w1.shape
    nj = f // fb
    row_tile = lambda i, j, te, nu: (jnp.maximum(jnp.minimum(i, nu[0] - 1), 0), 0, 0)
    jj = lambda i, j, nu: jnp.where(i < nu[0], j, nj - 1)
    return pl.pallas_call(
        _moe_grouped_body,
        out_shape=jax.ShapeDtypeStruct(xs.shape, F32),
        grid_spec=pltpu.PrefetchScalarGridSpec(
            num_scalar_prefetch=2, grid=(n_rows // tm, nj),
            in_specs=[pl.BlockSpec((tm, nsl, LANES), row_tile),
                      pl.BlockSpec((1, d, fb), lambda i, j, te, nu: (te[i], 0, jj(i, j, nu))),
                      pl.BlockSpec((1, d, fb), lambda i, j, te, nu: (te[i], 0, jj(i, j, nu))),
                      pl.BlockSpec((1, fb, d), lambda i, j, te, nu: (te[i], jj(i, j, nu), 0))],
            out_specs=pl.BlockSpec((tm, nsl, LANES), row_tile),
            scratch_shapes=[pltpu.VMEM((tm, d), BF16), pltpu.VMEM((tm, d), F32)]),
        compiler_params=_cparams(("arbitrary", "arbitrary")),
        name="moe_grouped",
    )(tile_expert, n_used, xs, w1, w3, w2)


def _moe_final_body(pos_ref, x_ref, y_hbm, gw_ref, gt_ref, fg_ref, o_ref, yg_scr, sem, *, n_tok):
    i = pl.program_id(0)
    tc = x_ref.shape[0]
    slot = i % 2

    def gather_tile(step, into):
        base = step * tc

        def start_token(r, carry):
            pltpu.make_async_copy(y_hbm.at[pos_ref[base + r]], yg_scr.at[into, r], sem.at[into]).start()
            pltpu.make_async_copy(y_hbm.at[pos_ref[n_tok + base + r]], yg_scr.at[into, tc + r], sem.at[into]).start()
            return carry

        lax.fori_loop(0, tc, start_token, 0, unroll=8)

    @pl.when(i == 0)
    def _():
        gather_tile(0, 0)

    @pl.when(i + 1 < pl.num_programs(0))
    def _():
        gather_tile(i + 1, 1 - slot)

    pltpu.make_async_copy(y_hbm.at[pl.ds(0, 2 * tc)], yg_scr.at[slot], sem.at[slot]).wait()
    gw = gw_ref[...]
    w1, w2 = gw[:, 0:1], gw[:, 1:2]
    y = jnp.concatenate([w1 * yg_scr[slot, pl.ds(0, tc), sl, :] + w2 * yg_scr[slot, pl.ds(tc, tc), sl, :]
                         for sl in range(yg_scr.shape[2])], axis=1)
    xo = x_ref[...] + gt_ref[0] * y
    ms = jnp.mean(xo * xo, axis=-1, keepdims=True)
    o_ref[...] = xo * lax.rsqrt(ms + EPS) * fg_ref[...]


def _moe_final(x, y, pos, gw, gt, final_g, tc=256):
    b, s, d = x.shape
    t = b * s
    spt = s // tc
    nsl = d // LANES
    out = pl.pallas_call(
        functools.partial(_moe_final_body, n_tok=t),
        out_shape=jax.ShapeDtypeStruct((t, d), F32),
        grid_spec=pltpu.PrefetchScalarGridSpec(
            num_scalar_prefetch=1, grid=(t // tc,),
            in_specs=[pl.BlockSpec((tc, d), lambda i, p: (i, 0)),
                      pl.BlockSpec(memory_space=pl.ANY),
                      pl.BlockSpec((tc, LANES), lambda i, p: (i, 0)),
                      pl.BlockSpec((1, 1, d), lambda i, p: (i // spt, 0, 0)),
                      pl.BlockSpec(final_g.shape, lambda i, p: (0, 0))],
            out_specs=pl.BlockSpec((tc, d), lambda i, p: (i, 0)),
            scratch_shapes=[pltpu.VMEM((2, 2 * tc, nsl, LANES), F32), pltpu.SemaphoreType.DMA((2,))]),
        compiler_params=_cparams(("arbitrary",)),
        name="moe_final",
    )(pos, x.reshape(t, d), y, gw, gt, final_g)
    return out.reshape(b, s, d)


def _moe_routed(x, y_f, w_f, gate_f, g, shift, scale, gt, final_g, w_router, b_router, w1, w3, w2):
    ne = w1.shape[0]
    tm = MOE_TM
    x1, h3, meta, gw, counts = _router(x, y_f, w_f, gate_f, g, shift, scale, w_router, b_router)
    pos, pad_start, pad_n, tile_expert, n_used = _moe_plan(meta, counts, ne, tm)
    n_rows = (2 * h3.shape[0] // tm + ne) * tm
    xs = _moe_dispatch(h3, pos, pad_start, pad_n, n_rows)
    y = _moe_grouped(xs, tile_expert, n_used, w1, w3, w2, tm)
    return _moe_final(x1.reshape(x.shape), y, pos, gw, gt, final_g)


def kernel(x, c, ctx, c_ctx, w_ada, b_ada, norm_g, w_in, hy_short_w, hy_short_b, hy_f_w1, hy_f_b1, hy_f_w2, hy_f_b2, hy_f_w3, hy_f_freq, hy_skip, na_rpb, w_mix_out, ffn_w1, ffn_w3, ffn_w2, w_fourier, w_router, b_router, moe_w1, moe_w3, moe_w2, final_g):
    b, s, d = x.shape
    depth = w_ada.shape[0]
    assert depth == 2, "layer 0 mixes with Hyena/attention, layer 1 with Fourier/MoE"
    c_hy = hy_skip.shape[-1]
    c_na = d - c_hy

    cvec = jnp.concatenate([c, c_ctx[None, :], jnp.zeros((8 - b - 1, d), F32)], axis=0)
    mods = _ada(cvec, w_ada, b_ada)

    def mod(layer, idx, ctx_row=False):
        m = mods[layer, :, idx * d:(idx + 1) * d]
        return m[b:b + 1, None, :] if ctx_row else m[0:b, None, :]

    row = lambda a: a.reshape(1, -1)

    w_in0 = w_in[0].astype(BF16)
    w_hy, w_qkv = w_in0[:, 0:3 * c_hy], w_in0[:, 3 * c_hy:]
    v, x1, x2, q, k, va = _inproj(x, row(norm_g[0, 0]), mod(0, 0), mod(0, 1), w_hy, w_qkv,
                                  hy_short_w[0], row(hy_short_b[0]))
    kc, vc = _ctxkv(ctx, row(norm_g[0, 0]), mod(0, 0, True), mod(0, 1, True), w_qkv[:, c_na:])
    y_na = _natt(q, k, va, kc, vc, _na_bias_table(na_rpb[0]))
    y_hy = _hyena(v, x1, x2, hy_f_w1[0], hy_f_b1[0], hy_f_w2[0], hy_f_b2[0], hy_f_w3[0],
                  hy_f_freq[0], hy_skip[0])
    x = _mix_ffn(x, y_hy, y_na, w_mix_out[0].astype(BF16), mod(0, 2),
                 row(norm_g[0, 1]), mod(0, 3), mod(0, 4), mod(0, 5),
                 ffn_w1[0].astype(BF16), ffn_w3[0].astype(BF16), ffn_w2[0].astype(BF16))

    y_f = _fourier_mix(x, row(norm_g[1, 0]), mod(1, 0), mod(1, 1))
    return _moe_routed(x, y_f, w_fourier[0].astype(BF16), mod(1, 2),
                       row(norm_g[1, 1]), mod(1, 3), mod(1, 4), mod(1, 5), row(final_g),
                       w_router[0], b_router[0],
                       moe_w1[0], moe_w3[0], moe_w2[0])
```

```python
import functools
import math

import numpy as np
import jax
import jax.numpy as jnp
from jax import lax
from jax.experimental import pallas as pl
from jax.experimental.pallas import tpu as pltpu

F32 = jnp.float32
BF16 = jnp.bfloat16
HIGHEST = lax.Precision.HIGHEST

GRID_W = 64
NA_HEAD_DIM = 32
NA_WIN_R = 8
NA_WIN_C = 16
HYENA_EMB = 33
HYENA_BANDS = (HYENA_EMB - 1) // 2
HYENA_FAST_DECAY = 0.3
HYENA_SLOW_DECAY = 1.5
HYENA_TARGET = 1e-2
F_GROUPS = 4
N_MOD = 6
EPS = 1e-6
NEG_INF = -1e30

FFT_A = 64
FFT_R = 128
FFT_KA = FFT_A // 2 + 1
FFT_KA_PAD = 40
FM_A = 64

VMEM_LIMIT = 48 * 1024 * 1024


def _cparams(sem):
    return pltpu.CompilerParams(dimension_semantics=sem, vmem_limit_bytes=VMEM_LIMIT)


def _dot(a, b):
    return jnp.dot(a, b, preferred_element_type=F32)


def _mxu_const(m):
    return jnp.asarray(m, dtype=F32).astype(BF16)


def _norm_mod(x, g, shift, scale):
    ms = jnp.mean(x * x, axis=-1, keepdims=True)
    y = x * lax.rsqrt(ms + EPS) * g
    return y * (1.0 + scale) + shift


def _ada_body(c_ref, w_ref, b_ref, o_ref):
    cv = c_ref[...]
    s = cv * jax.nn.sigmoid(cv)
    o_ref[0] = jnp.dot(s, w_ref[0], precision=HIGHEST, preferred_element_type=F32) + b_ref[0]


def _ada(cvec, w_ada, b_ada):
    depth, d, n = w_ada.shape
    rows = cvec.shape[0]
    bn = n // 4
    return pl.pallas_call(
        _ada_body,
        out_shape=jax.ShapeDtypeStruct((depth, rows, n), F32),
        grid=(depth, n // bn),
        in_specs=[pl.BlockSpec((rows, d), lambda l, j: (0, 0)),
                  pl.BlockSpec((1, d, bn), lambda l, j: (l, 0, j)),
                  pl.BlockSpec((1, 1, bn), lambda l, j: (l, 0, j))],
        out_specs=pl.BlockSpec((1, rows, bn), lambda l, j: (l, 0, j)),
        compiler_params=_cparams(("parallel", "parallel")),
        name="ada",
    )(cvec, w_ada, b_ada.reshape(depth, 1, n))


def _inproj_body(x_ref, xp_ref, xn_ref, g_ref, sh_ref, sc_ref, why_ref, wqkv_ref, sw_ref, sb_ref,
                 v_ref, x1_ref, x2_ref, q_ref, k_ref, va_ref, *, n_tiles, q_scale, c_hy, c_na):
    i = pl.program_id(1)
    g, sh, sc = g_ref[...], sh_ref[0], sc_ref[0]
    h = _norm_mod(x_ref[0], g, sh, sc).astype(BF16)
    zh = _dot(h, why_ref[...])
    hp = _norm_mod(xp_ref[0], g, sh, sc).astype(BF16)
    hn = _norm_mod(xn_ref[0], g, sh, sc).astype(BF16)
    zp = _dot(hp, why_ref[...])[7:8]
    zn = _dot(hn, why_ref[...])[0:1]
    zp = jnp.where(i > 0, zp, 0.0)
    zn = jnp.where(i < n_tiles - 1, zn, 0.0)
    tm = zh.shape[0]
    row = lax.broadcasted_iota(jnp.int32, zh.shape, 0)
    z_m1 = jnp.where(row == 0, zp, pltpu.roll(zh, 1, 0))
    z_p1 = jnp.where(row == tm - 1, zn, pltpu.roll(zh, tm - 1, 0))
    sw = sw_ref[...]
    zc = z_m1 * sw[0:1] + zh * sw[1:2] + z_p1 * sw[2:3] + sb_ref[...]
    v_ref[0] = zc[:, 0:c_hy].astype(BF16)
    x1_ref[0] = zc[:, c_hy:2 * c_hy].astype(BF16)
    x2_ref[0] = zc[:, 2 * c_hy:3 * c_hy].astype(BF16)
    zq = _dot(h, wqkv_ref[...])
    q_ref[0] = (zq[:, 0:c_na] * q_scale).astype(BF16)
    k_ref[0] = zq[:, c_na:2 * c_na].astype(BF16)
    va_ref[0] = zq[:, 2 * c_na:3 * c_na].astype(BF16)


def _inproj(x, g, shift, scale, w_hy, w_qkv, short_w, short_b, tm=512):
    b, s, d = x.shape
    c_hy = w_hy.shape[1] // 3
    c_na = w_qkv.shape[1] // 3
    n_tiles = s // tm
    r8 = tm // 8
    body = functools.partial(_inproj_body, n_tiles=n_tiles, q_scale=NA_HEAD_DIM ** -0.5,
                             c_hy=c_hy, c_na=c_na)
    tok = lambda c: pl.BlockSpec((1, tm, c), lambda bi, i: (bi, i, 0))
    full2 = lambda a: pl.BlockSpec(a.shape, lambda bi, i: (0, 0))
    per_b = pl.BlockSpec((1, 1, d), lambda bi, i: (bi, 0, 0))
    return pl.pallas_call(
        body,
        out_shape=[jax.ShapeDtypeStruct((b, s, c_hy), BF16)] * 3 + [jax.ShapeDtypeStruct((b, s, c_na), BF16)] * 3,
        grid=(b, n_tiles),
        in_specs=[tok(d),
                  pl.BlockSpec((1, 8, d), lambda bi, i: (bi, jnp.maximum(i * r8 - 1, 0), 0)),
                  pl.BlockSpec((1, 8, d), lambda bi, i: (bi, jnp.minimum((i + 1) * r8, s // 8 - 1), 0)),
                  full2(g), per_b, per_b, full2(w_hy), full2(w_qkv), full2(short_w), full2(short_b)],
        out_specs=[tok(c_hy)] * 3 + [tok(c_na)] * 3,
        compiler_params=_cparams(("parallel", "parallel")),
        name="inproj",
    )(x, x, x, g, shift, scale, w_hy, w_qkv, short_w, short_b)


def _ctxkv_body(x_ref, g_ref, sh_ref, sc_ref, w_ref, k_ref, v_ref, *, c_na):
    h = _norm_mod(x_ref[0], g_ref[...], sh_ref[0], sc_ref[0]).astype(BF16)
    z = _dot(h, w_ref[...])
    k_ref[0] = z[:, 0:c_na].astype(BF16)
    v_ref[0] = z[:, c_na:2 * c_na].astype(BF16)


def _ctxkv(ctx, g, shift, scale, w_kv):
    b, n, d = ctx.shape
    c_na = w_kv.shape[1] // 2
    one = pl.BlockSpec((1, 1, d), lambda bi: (0, 0, 0))
    return pl.pallas_call(
        functools.partial(_ctxkv_body, c_na=c_na),
        out_shape=[jax.ShapeDtypeStruct((b, n, c_na), BF16)] * 2,
        grid=(b,),
        in_specs=[pl.BlockSpec((1, n, d), lambda bi: (bi, 0, 0)),
                  pl.BlockSpec(g.shape, lambda bi: (0, 0)), one, one,
                  pl.BlockSpec(w_kv.shape, lambda bi: (0, 0))],
        out_specs=[pl.BlockSpec((1, n, c_na), lambda bi: (bi, 0, 0))] * 2,
        compiler_params=_cparams(("parallel",)),
        name="ctxkv",
    )(ctx, g, shift, scale, w_kv)


NA_HEADS_PER_BLK = 4


def _na_bias_body(r_ref, e_ref, ok_ref, o_ref):
    t = jnp.dot(r_ref[...], e_ref[...], precision=HIGHEST, preferred_element_type=F32)
    o_ref[...] = jnp.where(ok_ref[...] > 0.5, t, NEG_INF)


def _na_bias_table(rpb):
    w = GRID_W
    h, nr, nc = rpb.shape
    col = np.arange(w)[:, None]
    kc = np.arange(w)[None, :]
    c_start = np.clip(col - NA_WIN_C // 2, 0, w - NA_WIN_C)
    valid = ((kc >= c_start) & (kc < c_start + NA_WIN_C)).reshape(1, w * w)
    expand = (np.arange(32)[:, None, None] == (kc - col + NA_WIN_C - 1)[None]).reshape(32, w * w)
    rp = jnp.pad(rpb.astype(F32).reshape(h * nr, nc), ((0, 0), (0, 32 - nc)))
    full = lambda a: pl.BlockSpec(a.shape, lambda: (0,) * a.ndim)
    expand = jnp.asarray(expand, dtype=F32)
    ok = jnp.asarray(valid, dtype=F32)
    toep = pl.pallas_call(
        _na_bias_body,
        out_shape=jax.ShapeDtypeStruct((h * nr, w * w), F32),
        in_specs=[full(rp), full(expand), full(ok)],
        out_specs=pl.BlockSpec((h * nr, w * w), lambda: (0, 0)),
        name="na_bias",
    )(rp, expand, ok)
    t2 = toep.reshape(h, nr, w, w).transpose(0, 2, 1, 3).reshape(h, w, nr * w)
    slabs = jnp.stack([t2[:, :, (NA_WIN_R - 1 - off) * w:(2 * NA_WIN_R - 1 - off) * w]
                       for off in range(NA_WIN_R)], axis=1)
    hpb = NA_HEADS_PER_BLK
    slabs = slabs.reshape(h // hpb, hpb, NA_WIN_R, w, NA_WIN_R * w).transpose(0, 2, 1, 3, 4)
    return slabs.reshape(h // hpb, NA_WIN_R, hpb * w, NA_WIN_R * w)


def _natt_body(q_ref, k_ref, v_ref, kc_ref, vc_ref, bias_ref, o_ref, *, rows):
    w = GRID_W
    hpb = NA_HEADS_PER_BLK
    nloc = NA_WIN_R * w
    lane = lax.broadcasted_iota(jnp.int32, (1, hpb * NA_HEAD_DIM), 1)
    in_head = [(lane >= NA_HEAD_DIM * hh) & (lane < NA_HEAD_DIM * (hh + 1)) for hh in range(hpb)]
    kcx = kc_ref[0]
    vcx = vc_ref[0]
    nt = (((1,), (1,)), ((), ()))

    def one_row(r):
        r0 = jnp.clip(r - NA_WIN_R // 2, 0, rows - NA_WIN_R)
        off = r - r0
        qs = q_ref[0, pl.ds(pl.multiple_of(r * w, w), w), :]
        kw = k_ref[0, pl.ds(pl.multiple_of(r0 * w, w), nloc), :]
        vw = v_ref[0, pl.ds(pl.multiple_of(r0 * w, w), nloc), :]
        zero = jnp.zeros_like(qs)
        qst = jnp.concatenate([jnp.where(m, qs, zero) for m in in_head], axis=0)
        s_loc = lax.dot_general(qst, kw, nt, preferred_element_type=F32) + bias_ref[0, off]
        s_ctx = lax.dot_general(qst, kcx, nt, preferred_element_type=F32)
        m = jnp.maximum(jnp.max(s_loc, axis=-1, keepdims=True), jnp.max(s_ctx, axis=-1, keepdims=True))
        p_loc = jnp.exp(s_loc - m)
        p_ctx = jnp.exp(s_ctx - m)
        den = jnp.sum(p_loc, axis=-1, keepdims=True) + jnp.sum(p_ctx, axis=-1, keepdims=True)
        o = (_dot(p_loc.astype(BF16), vw) + _dot(p_ctx.astype(BF16), vcx)) * (1.0 / den)
        acc = jnp.where(in_head[0], o[0:w], 0.0)
        for hh in range(1, hpb):
            acc = acc + jnp.where(in_head[hh], o[hh * w:(hh + 1) * w], 0.0)
        o_ref[0, pl.ds(pl.multiple_of(r * w, w), w), :] = acc.astype(BF16)

    def row_pair(i, carry):
        one_row(2 * i)
        one_row(2 * i + 1)
        return carry

    lax.fori_loop(0, rows // 2, row_pair, 0)


def _natt(q, k, v, kc, vc, bias):
    b, s, c = q.shape
    nctx = kc.shape[1]
    lw = NA_HEADS_PER_BLK * NA_HEAD_DIM
    rows = s // GRID_W
    seq = pl.BlockSpec((1, s, lw), lambda bi, g: (bi, 0, g))
    cx = pl.BlockSpec((1, nctx, lw), lambda bi, g: (bi, 0, g))
    return pl.pallas_call(
        functools.partial(_natt_body, rows=rows),
        out_shape=jax.ShapeDtypeStruct((b, s, c), BF16),
        grid=(b, c // lw),
        in_specs=[seq, seq, seq, cx, cx,
                  pl.BlockSpec((1,) + bias.shape[1:], lambda bi, g: (g, 0, 0, 0))],
        out_specs=seq,
        compiler_params=_cparams(("parallel", "parallel")),
        name="natt",
    )(q, k, v, kc, vc, bias)


def _hyena_feats(seq_len):
    t = jnp.linspace(0.0, 1.0, seq_len, dtype=F32)[:, None]
    bands = jnp.linspace(1e-4, HYENA_BANDS - 1, HYENA_BANDS, dtype=F32)
    ang = (2.0 * math.pi / seq_len) * jnp.arange(seq_len, dtype=F32)[:, None] * bands[None, :]
    feats = jnp.concatenate([t, jnp.cos(ang), -jnp.sin(ang)], axis=-1)
    return jnp.pad(feats, ((0, 0), (0, 128 - HYENA_EMB)))


def _filt_body(feat_ref, w1_ref, b1_ref, w2_ref, b2_ref, w3_ref, fr_ref, dl_ref, o_ref, l1_ref, h_scr):
    j = pl.program_id(0)
    hp = functools.partial(jnp.dot, precision=HIGHEST, preferred_element_type=F32)
    feats = feat_ref[...]

    @pl.when(j == 0)
    def _():
        fr = fr_ref[...]
        h = jnp.sin(fr[0:1] * (hp(feats, w1_ref[...]) + b1_ref[...]))
        h_scr[...] = jnp.sin(fr[1:2] * (hp(h, w2_ref[...]) + b2_ref[...]))

    hc = hp(h_scr[...], w3_ref[...])
    t = feats[:, 0:1]
    hc = hc * jnp.exp(-t * dl_ref[...])
    row = lax.broadcasted_iota(jnp.int32, hc.shape, 0)
    hc = jnp.where((row == 0) & (j % 2 == 1), 0.0, hc)
    l1_ref[0] = jnp.sum(jnp.abs(hc), axis=0, keepdims=True)
    o_ref[0] = hc.astype(BF16)


def _hyena_filter_taps(seq_len, f_w1, f_b1, f_w2, f_b2, f_w3, f_freq, c_hy):
    feats = _hyena_feats(seq_len)
    hid = f_w1.shape[1]
    w1 = jnp.pad(f_w1.astype(F32), ((0, 128 - HYENA_EMB), (0, 0)))
    deltas = jnp.abs(jnp.linspace(math.log(HYENA_TARGET) / HYENA_SLOW_DECAY,
                                  math.log(HYENA_TARGET) / HYENA_FAST_DECAY, c_hy, dtype=F32))[None, :]
    nblk = f_w3.shape[1] // c_hy
    c0 = lambda a: pl.BlockSpec(a.shape, lambda j: (0, 0))
    b1, b2 = f_b1.reshape(1, hid), f_b2.reshape(1, hid)
    return pl.pallas_call(
        _filt_body,
        out_shape=[jax.ShapeDtypeStruct((nblk, seq_len, c_hy), BF16),
                   jax.ShapeDtypeStruct((nblk, 1, c_hy), F32)],
        grid=(nblk,),
        in_specs=[c0(feats), c0(w1), c0(b1), c0(f_w2), c0(b2),
                  pl.BlockSpec((hid, c_hy), lambda j: (0, j)), c0(f_freq), c0(deltas)],
        out_specs=[pl.BlockSpec((1, seq_len, c_hy), lambda j: (j, 0, 0)),
                   pl.BlockSpec((1, 1, c_hy), lambda j: (j, 0, 0))],
        scratch_shapes=[pltpu.VMEM((seq_len, hid), F32)],
        compiler_params=_cparams(("arbitrary",)),
        name="hyena_filter",
    )(feats, w1, b1, f_w2, b2, f_w3, f_freq, deltas)


def _conv_dft_constants():
    a_half = FFT_A // 2
    n = FFT_A * FFT_R
    ka = np.arange(FFT_KA)[:, None]
    a = np.arange(a_half)[None, :]
    ph = 2.0 * np.pi * ka * a / FFT_A
    m_fwd = np.zeros((2 * FFT_KA_PAD, a_half))
    m_fwd[:FFT_KA] = np.cos(ph)
    m_fwd[FFT_KA_PAD:FFT_KA_PAD + FFT_KA] = -np.sin(ph)
    wgt = np.where((ka == 0) | (ka == FFT_A // 2), 1.0, 2.0)
    m_inv = np.zeros((a_half, 2 * FFT_KA_PAD))
    m_inv[:, :FFT_KA] = (wgt * np.cos(ph)).T / n
    m_inv[:, FFT_KA_PAD:FFT_KA_PAD + FFT_KA] = (-wgt * np.sin(ph)).T / n
    kb = np.arange(FFT_R)[None, :, None]
    b = np.arange(FFT_R)[None, None, :]
    kaa = np.arange(FFT_KA)[:, None, None]
    th = 2.0 * np.pi * (b * kb / FFT_R + b * kaa / n)
    gr, gi = np.cos(th), -np.sin(th)
    g2 = np.zeros((FFT_KA_PAD, 2 * FFT_R, 2 * FFT_R))
    g2[:FFT_KA] = np.block([[gr, -gi], [gi, gr]])
    grt, git = gr.transpose(0, 2, 1), gi.transpose(0, 2, 1)
    g2h = np.zeros_like(g2)
    g2h[:FFT_KA] = np.block([[grt, git], [-git, grt]])
    return _mxu_const(m_fwd), _mxu_const(m_inv), _mxu_const(g2), _mxu_const(g2h)


def _fwd1_body(m_ref, u_ref, o_ref):
    res = _dot(m_ref[...], u_ref[0])
    o_ref[0, 0] = res[0:FFT_KA_PAD]
    o_ref[0, 1] = res[FFT_KA_PAD:2 * FFT_KA_PAD]


def _conv_fwd1(u, m_fwd, cb=8192):
    n, seq, c = u.shape
    a_half = FFT_A // 2
    cols = seq * c // a_half
    uv = u.reshape(n, a_half, cols)
    return pl.pallas_call(
        _fwd1_body,
        out_shape=jax.ShapeDtypeStruct((n, 2, FFT_KA_PAD, cols), F32),
        grid=(n, cols // cb),
        in_specs=[pl.BlockSpec(m_fwd.shape, lambda i, j: (0, 0)),
                  pl.BlockSpec((1, a_half, cb), lambda i, j: (i, 0, j))],
        out_specs=pl.BlockSpec((1, 2, FFT_KA_PAD, cb), lambda i, j: (i, 0, 0, j)),
        compiler_params=_cparams(("parallel", "parallel")),
        name="conv_fwd1",
    )(m_fwd, uv)


FFT_KB = 8


def _rows_to_slabs(src_ref, dst_scr, c):
    for part in range(2):
        for b in range(FFT_R):
            dst_scr[part, :, b, :] = src_ref[0, part, :, b * c:(b + 1) * c]


def _slabs_to_rows(src_scr, dst_ref, c):
    for part in range(2):
        for b in range(FFT_R):
            dst_ref[0, part, :, b * c:(b + 1) * c] = src_scr[part, :, b, :]


def _slab(scr, i):
    return jnp.concatenate([scr[0, i], scr[1, i]], axis=0).astype(BF16)


def _fwd2f_body(sf_ref, sb_ref, g_ref, l1_ref, kf_ref, f3, b3):
    o = pl.program_id(0)
    j = pl.program_id(1)
    r2 = 2 * FFT_R
    c = kf_ref.shape[-1]
    _rows_to_slabs(sf_ref, f3, c)
    _rows_to_slabs(sb_ref, b3, c)
    inv = 1.0 / (l1_ref[2 * o] + l1_ref[2 * o + 1] + EPS)
    for i in range(FFT_KB):
        @pl.when(j * FFT_KB + i < FFT_KA)
        def _():
            xf = _dot(g_ref[i], _slab(f3, i))
            xb = _dot(g_ref[i], _slab(b3, i))
            kf_ref[0, i, 0:FFT_R] = (xf[0:FFT_R] + xb[0:FFT_R]) * inv
            kf_ref[0, i, FFT_R:r2] = (xf[FFT_R:r2] - xb[FFT_R:r2]) * inv

        @pl.when(j * FFT_KB + i >= FFT_KA)
        def _():
            kf_ref[0, i] = jnp.zeros((r2, c), F32)


def _filter_spectrum(s_filt, l1, g2, c):
    n_ord = s_filt.shape[0] // 2
    cols = s_filt.shape[-1]
    r2 = 2 * FFT_R
    return pl.pallas_call(
        _fwd2f_body,
        out_shape=jax.ShapeDtypeStruct((n_ord, FFT_KA_PAD, r2, c), F32),
        grid=(n_ord, FFT_KA_PAD // FFT_KB),
        in_specs=[pl.BlockSpec((1, 2, FFT_KB, cols), lambda o, j: (2 * o, 0, j, 0)),
                  pl.BlockSpec((1, 2, FFT_KB, cols), lambda o, j: (2 * o + 1, 0, j, 0)),
                  pl.BlockSpec((FFT_KB, r2, r2), lambda o, j: (j, 0, 0)),
                  pl.BlockSpec(l1.shape, lambda o, j: (0, 0, 0))],
        out_specs=pl.BlockSpec((1, FFT_KB, r2, c), lambda o, j: (o, j, 0, 0)),
        scratch_shapes=[pltpu.VMEM((2, FFT_KB, FFT_R, c), F32)] * 2,
        compiler_params=_cparams(("parallel", "parallel")),
        name="filter_spectrum",
    )(s_filt, s_filt, g2, l1)


def _mid_body(s_ref, g_ref, gh_ref, kf_ref, t_ref, s3, t3):
    j = pl.program_id(1)
    r2 = 2 * FFT_R
    c = kf_ref.shape[-1]
    _rows_to_slabs(s_ref, s3, c)
    for i in range(FFT_KB):
        @pl.when(j * FFT_KB + i < FFT_KA)
        def _():
            x = _dot(g_ref[i], _slab(s3, i))
            xr, xi = x[0:FFT_R], x[FFT_R:r2]
            kr, ki = kf_ref[0, i, 0:FFT_R], kf_ref[0, i, FFT_R:r2]
            y = jnp.concatenate([xr * kr - xi * ki, xr * ki + xi * kr], axis=0).astype(BF16)
            t = _dot(gh_ref[i], y)
            t3[0, i] = t[0:FFT_R]
            t3[1, i] = t[FFT_R:r2]

        @pl.when(j * FFT_KB + i >= FFT_KA)
        def _():
            t3[0, i] = jnp.zeros((FFT_R, c), F32)
            t3[1, i] = jnp.zeros((FFT_R, c), F32)

    _slabs_to_rows(t3, t_ref, c)


def _conv_mid(s, kf, order, g2, g2h, c):
    n, _, _, cols = s.shape
    r2 = 2 * FFT_R
    blk = pl.BlockSpec((1, 2, FFT_KB, cols), lambda i, j: (i, 0, j, 0))
    gspec = pl.BlockSpec((FFT_KB, r2, r2), lambda i, j: (j, 0, 0))
    return pl.pallas_call(
        _mid_body,
        out_shape=jax.ShapeDtypeStruct(s.shape, F32),
        grid=(n, FFT_KA_PAD // FFT_KB),
        in_specs=[blk, gspec, gspec,
                  pl.BlockSpec((1, FFT_KB, r2, c), lambda i, j: (order, j, 0, 0))],
        out_specs=blk,
        scratch_shapes=[pltpu.VMEM((2, FFT_KB, FFT_R, c), F32)] * 2,
        compiler_params=_cparams(("parallel", "parallel")),
        name="conv_mid",
    )(s, g2, g2h, kf)


def _inv1_body(m_ref, t_ref, u_ref, xg_ref, sk_ref, o_ref):
    cb = t_ref.shape[-1]
    t2 = t_ref[0].reshape(2 * FFT_KA_PAD, cb).astype(BF16)
    y = _dot(m_ref[...], t2)
    u = u_ref[0].astype(F32)
    o_ref[0] = (xg_ref[0].astype(F32) * (y + u * sk_ref[...])).astype(BF16)


def _conv_inv1(t, u, xg, skip, m_inv, cb=8192):
    n, seq, c = u.shape
    a_half = FFT_A // 2
    cols = seq * c // a_half
    sk = jnp.tile(skip.astype(F32).reshape(1, c), (1, cols // c))
    uspec = pl.BlockSpec((1, a_half, cb), lambda i, j: (i, 0, j))
    out = pl.pallas_call(
        _inv1_body,
        out_shape=jax.ShapeDtypeStruct((n, a_half, cols), BF16),
        grid=(n, cols // cb),
        in_specs=[pl.BlockSpec(m_inv.shape, lambda i, j: (0, 0)),
                  pl.BlockSpec((1, 2, FFT_KA_PAD, cb), lambda i, j: (i, 0, 0, j)),
                  uspec, uspec,
                  pl.BlockSpec((1, cb), lambda i, j: (0, j))],
        out_specs=uspec,
        compiler_params=_cparams(("parallel", "parallel")),
        name="conv_inv1",
    )(m_inv, t, u.reshape(n, a_half, cols), xg.reshape(n, a_half, cols), sk)
    return out.reshape(n, seq, c)


def _hyena(v, x1, x2, f_w1, f_b1, f_w2, f_b2, f_w3, f_freq, skip):
    _, seq, c = v.shape
    assert 2 * seq == FFT_A * FFT_R
    m_fwd, m_inv, g2, g2h = _conv_dft_constants()
    taps, l1 = _hyena_filter_taps(seq, f_w1, f_b1, f_w2, f_b2, f_w3, f_freq, c)
    kf = _filter_spectrum(_conv_fwd1(taps, m_fwd), l1, g2, c)
    y = v
    for order, xg in enumerate((x1, x2)):
        t = _conv_mid(_conv_fwd1(y, m_fwd), kf, order, g2, g2h, c)
        y = _conv_inv1(t, y, xg, skip[order], m_inv)
    return y


def _mix_ffn_body(x_ref, a1_ref, a2_ref, wm_ref, gm_ref, g_ref, sh_ref, sc_ref, gt_ref,
                  w1_ref, w3_ref, w2_ref, o_ref, x_scr, h_scr, acc_scr):
    j = pl.program_id(2)

    @pl.when(j == 0)
    def _():
        c1 = a1_ref.shape[-1]
        mixed = _dot(a1_ref[0], wm_ref[0:c1]) + _dot(a2_ref[0], wm_ref[c1:])
        xm = x_ref[0] + gm_ref[0] * mixed
        x_scr[...] = xm
        h_scr[...] = _norm_mod(xm, g_ref[...], sh_ref[0], sc_ref[0]).astype(BF16)
        acc_scr[...] = jnp.zeros_like(acc_scr)

    h = h_scr[...]
    a = _dot(h, w1_ref[...])
    u = (a * jax.nn.sigmoid(a) * _dot(h, w3_ref[...])).astype(BF16)
    acc_scr[...] += _dot(u, w2_ref[...])

    @pl.when(j == pl.num_programs(2) - 1)
    def _():
        o_ref[0] = x_scr[...] + gt_ref[0] * acc_scr[...]


def _mix_ffn(x, a1, a2, w_mix, gate_mix, g, shift, scale, gate, w1, w3, w2, tm=512, fb=1408):
    b, s, d = x.shape
    f = w1.shape[1]
    tok = lambda c: pl.BlockSpec((1, tm, c), lambda bi, i, j: (bi, i, 0))
    per_b = pl.BlockSpec((1, 1, d), lambda bi, i, j: (bi, 0, 0))
    const = lambda a: pl.BlockSpec(a.shape, lambda bi, i, j: (0, 0))
    return pl.pallas_call(
        _mix_ffn_body,
        out_shape=jax.ShapeDtypeStruct(x.shape, F32),
        grid=(b, s // tm, f // fb),
        in_specs=[tok(d), tok(a1.shape[-1]), tok(a2.shape[-1]), const(w_mix), per_b,
                  const(g), per_b, per_b, per_b,
                  pl.BlockSpec((d, fb), lambda bi, i, j: (0, j)),
                  pl.BlockSpec((d, fb), lambda bi, i, j: (0, j)),
                  pl.BlockSpec((fb, d), lambda bi, i, j: (j, 0))],
        out_specs=tok(d),
        scratch_shapes=[pltpu.VMEM((tm, d), F32), pltpu.VMEM((tm, d), BF16), pltpu.VMEM((tm, d), F32)],
        compiler_params=_cparams(("parallel", "parallel", "arbitrary")),
        name="mix_ffn",
    )(x, a1, a2, w_mix, gate_mix, g, shift, scale, gate, w1, w3, w2)


def _fm_constants(cg):
    j = np.arange(cg)[:, None]
    m = np.arange(cg)[None, :]
    ph = 2.0 * np.pi * j * m / cg
    w_cs = np.concatenate([np.cos(ph), np.sin(ph)], axis=1)
    d = np.arange(FM_A)[:, None]
    a = np.arange(FM_A)[None, :]
    ph = 2.0 * np.pi * d * a / FM_A
    fr, fi = np.cos(ph), -np.sin(ph)
    m1 = np.block([[fr, fi], [fi, -fr]])
    n = FM_A * FM_A
    dd = np.arange(FM_A)[:, None, None]
    c = np.arange(FM_A)[None, :, None]
    b = np.arange(FM_A)[None, None, :]
    th = 2.0 * np.pi * (b * c / FM_A + b * dd / n)
    gcat = np.concatenate([np.cos(th), np.sin(th)], axis=2)
    return _mxu_const(w_cs), _mxu_const(m1), _mxu_const(gcat)


def _fm_chan_body(x_ref, g_ref, sh_ref, sc_ref, w_ref, o_ref, *, cg):
    h = _norm_mod(x_ref[0], g_ref[...], sh_ref[0], sc_ref[0]).astype(BF16)
    for grp in range(h.shape[-1] // cg):
        pq = _dot(h[:, grp * cg:(grp + 1) * cg], w_ref[...])
        o_ref[0, 0, :, grp * cg:(grp + 1) * cg] = pq[:, 0:cg].astype(BF16)
        o_ref[0, 1, :, grp * cg:(grp + 1) * cg] = pq[:, cg:2 * cg].astype(BF16)


def _fm_chan(x, g, shift, scale, w_cs, tm=512):
    b, s, d = x.shape
    cg = w_cs.shape[0]
    per_b = pl.BlockSpec((1, 1, d), lambda bi, i: (bi, 0, 0))
    return pl.pallas_call(
        functools.partial(_fm_chan_body, cg=cg),
        out_shape=jax.ShapeDtypeStruct((b, 2, s, d), BF16),
        grid=(b, s // tm),
        in_specs=[pl.BlockSpec((1, tm, d), lambda bi, i: (bi, i, 0)),
                  pl.BlockSpec(g.shape, lambda bi, i: (0, 0)), per_b, per_b,
                  pl.BlockSpec(w_cs.shape, lambda bi, i: (0, 0))],
        out_specs=pl.BlockSpec((1, 2, tm, d), lambda bi, i: (bi, 0, i, 0)),
        compiler_params=_cparams(("parallel", "parallel")),
        name="fm_chan",
    )(x, g, shift, scale, w_cs)


def _fm_s1_body(m_ref, pq_ref, o_ref):
    o_ref[0] = _dot(m_ref[...], pq_ref[0]).astype(BF16)


def _fm_stage1(pq, m1, cb=8192):
    b, _, s, d = pq.shape
    cols = s * d // FM_A
    blk = pl.BlockSpec((1, 2 * FM_A, cb), lambda bi, j: (bi, 0, j))
    return pl.pallas_call(
        _fm_s1_body,
        out_shape=jax.ShapeDtypeStruct((b, 2 * FM_A, cols), BF16),
        grid=(b, cols // cb),
        in_specs=[pl.BlockSpec(m1.shape, lambda bi, j: (0, 0)), blk],
        out_specs=blk,
        compiler_params=_cparams(("parallel", "parallel")),
        name="fm_stage1",
    )(m1, pq.reshape(b, 2 * FM_A, cols))


def _fm_s2_body(s_ref, g_ref, o_ref, *, dblk, scale):
    for i in range(dblk):
        s2 = jnp.concatenate([s_ref[0, 0, i], s_ref[0, 1, i]], axis=0)
        o_ref[:, i, :] = _dot(g_ref[i], s2) * scale


def _fm_stage2(s1, gcat, seq, d, dblk=8):
    b = s1.shape[0]
    sv = s1.reshape(b, 2, FM_A, FM_A, d)
    scale = 1.0 / math.sqrt(seq * (d // F_GROUPS))
    out = pl.pallas_call(
        functools.partial(_fm_s2_body, dblk=dblk, scale=scale),
        out_shape=jax.ShapeDtypeStruct((b * FM_A, dblk * (FM_A // dblk), d), F32),
        grid=(b, FM_A // dblk),
        in_specs=[pl.BlockSpec((1, 2, dblk, FM_A, d), lambda bi, j: (bi, 0, j, 0, 0)),
                  pl.BlockSpec((dblk, FM_A, 2 * FM_A), lambda bi, j: (j, 0, 0))],
        out_specs=pl.BlockSpec((FM_A, dblk, d), lambda bi, j: (bi, j, 0)),
        compiler_params=_cparams(("parallel", "parallel")),
        name="fm_stage2",
    )(sv, gcat)
    return out.reshape(b, seq, d)


def _fourier_mix(x, g, shift, scale):
    b, s, d = x.shape
    assert s == FM_A * FM_A
    w_cs, m1, gcat = _fm_constants(d // F_GROUPS)
    pq = _fm_chan(x, g, shift, scale, w_cs)
    return _fm_stage2(_fm_stage1(pq, m1), gcat, s, d)


LANES = 128
ROW_SL = 8
MOE_TM = 1024
DMA_WINDOW = 128


def _router_body(x_ref, yf_ref, wf_ref, gf_ref, g_ref, sh_ref, sc_ref, wr_ref, br_ref,
                 xo_ref, h_ref, meta_ref, gw_ref, cnt_ref, carry):
    i = pl.program_id(0)

    @pl.when(i == 0)
    def _():
        carry[...] = jnp.zeros_like(carry)

    xm = x_ref[...] + gf_ref[0] * _dot(yf_ref[...].astype(BF16), wf_ref[...])
    xo_ref[...] = xm
    h = _norm_mod(xm, g_ref[...], sh_ref[0], sc_ref[0])
    _rows_to_tiles(h_ref, h)
    h_hi = h.astype(BF16)
    h_lo = (h - h_hi.astype(F32)).astype(BF16)
    by_hi = _dot(h_hi, wr_ref[...])
    logits = by_hi[:, 0:LANES] + by_hi[:, LANES:] + _dot(h_lo, wr_ref[:, 0:LANES]) + br_ref[...]
    lane = lax.broadcasted_iota(jnp.int32, logits.shape, 1)
    nl = logits.shape[-1]
    m1 = jnp.max(logits, axis=-1, keepdims=True)
    i1 = jnp.min(jnp.where(logits == m1, lane, nl), axis=-1, keepdims=True)
    rest = jnp.where(lane == i1, -3.0e38, logits)
    m2 = jnp.max(rest, axis=-1, keepdims=True)
    i2 = jnp.min(jnp.where(rest == m2, lane, nl), axis=-1, keepdims=True)
    e = jnp.exp(m2 - m1)
    gw_ref[...] = jnp.where(lane == 0, 1.0 / (1.0 + e), jnp.where(lane == 1, e / (1.0 + e), 0.0))
    onehot = jnp.where((lane == i1) | (lane == i2), 1.0, 0.0)
    tm = onehot.shape[0]
    earlier = lax.broadcasted_iota(jnp.int32, (tm, tm), 0) > lax.broadcasted_iota(jnp.int32, (tm, tm), 1)
    excl = _dot(jnp.where(earlier, 1.0, 0.0).astype(BF16), onehot.astype(BF16)) + carry[...]
    r1 = jnp.sum(jnp.where(lane == i1, excl, 0.0), axis=-1, keepdims=True).astype(jnp.int32)
    r2 = jnp.sum(jnp.where(lane == i2, excl, 0.0), axis=-1, keepdims=True).astype(jnp.int32)
    meta_ref[...] = jnp.where(lane == 0, i1, jnp.where(lane == 1, i2, jnp.where(lane == 2, r1, jnp.where(lane == 3, r2, 0))))
    carry[...] = carry[...] + jnp.sum(onehot, axis=0, keepdims=True)
    cnt_ref[...] = carry[...]


def _router(x, y_f, w_f, gate_f, g, shift, scale, w_router, b_router, tm=512):
    b, s, d = x.shape
    t = b * s
    ne = w_router.shape[1]
    wr = jnp.pad(w_router.astype(F32), ((0, 0), (0, LANES - ne)))
    wr_hi = wr.astype(BF16)
    wr = jnp.concatenate([wr_hi, (wr - wr_hi.astype(F32)).astype(BF16)], axis=1)
    br = jnp.pad(b_router.astype(F32).reshape(1, ne), ((0, 0), (0, LANES - ne)), constant_values=NEG_INF)
    spt = s // tm
    per_b = pl.BlockSpec((1, 1, d), lambda i: (i // spt, 0, 0))
    const = lambda a: pl.BlockSpec(a.shape, lambda i: (0, 0))
    tok = pl.BlockSpec((tm, d), lambda i: (i, 0))
    return pl.pallas_call(
        _router_body,
        out_shape=[jax.ShapeDtypeStruct((t, d), F32),
                   jax.ShapeDtypeStruct((t * ROW_SL, LANES), F32),
                   jax.ShapeDtypeStruct((t, LANES), jnp.int32),
                   jax.ShapeDtypeStruct((t, LANES), F32),
                   jax.ShapeDtypeStruct((1, LANES), F32)],
        grid=(t // tm,),
        in_specs=[tok, tok, const(w_f), per_b, const(g), per_b, per_b, const(wr), const(br)],
        out_specs=[tok,
                   pl.BlockSpec((tm * ROW_SL, LANES), lambda i: (i, 0)),
                   pl.BlockSpec((tm, LANES), lambda i: (i, 0)),
                   pl.BlockSpec((tm, LANES), lambda i: (i, 0)),
                   pl.BlockSpec((1, LANES), lambda i: (0, 0))],
        scratch_shapes=[pltpu.VMEM((1, LANES), F32)],
        compiler_params=_cparams(("arbitrary",)),
        name="router",
    )(x.reshape(t, d), y_f.reshape(t, d), w_f, gate_f, g, shift, scale, wr, br)


def _moe_plan(meta, counts, ne, tm):
    i1, i2, r1, r2 = meta[:, 0], meta[:, 1], meta[:, 2], meta[:, 3]
    cnt = counts[0, :ne].astype(jnp.int32)
    padded = ((cnt + tm - 1) // tm) * tm
    ends = jnp.cumsum(padded)
    offs = ends - padded
    pick = lambda idx: sum(jnp.where(idx == e, offs[e], 0) for e in range(ne))
    pos = jnp.concatenate([pick(i1) + r1, pick(i2) + r2]).astype(jnp.int32)
    n_tiles = (2 * meta.shape[0]) // tm + ne
    n_used = (ends[ne - 1] // tm).astype(jnp.int32)
    tile_start = jnp.minimum(jnp.arange(n_tiles, dtype=jnp.int32), n_used - 1) * tm
    tile_expert = jnp.sum(tile_start[:, None] >= ends[None, :], axis=1).astype(jnp.int32)
    return pos, offs + cnt, padded - cnt, tile_expert, n_used.reshape(1)


def _windowed_copies(n, start_copy, wait_one, per_iter):
    def body(i, carry):
        @pl.when(i >= DMA_WINDOW)
        def _():
            for _ in range(per_iter):
                wait_one()
        start_copy(i)
        return carry

    lax.fori_loop(0, n, body, 0)

    def drain(i, carry):
        for _ in range(per_iter):
            wait_one()
        return carry

    lax.fori_loop(0, jnp.minimum(n, DMA_WINDOW), drain, 0)


def _tile_of(ref, row):
    return ref.at[pl.ds(pl.multiple_of(row * ROW_SL, ROW_SL), ROW_SL)]


def _tiles_to_rows(ref, n, first=0):
    return jnp.concatenate([ref[pl.ds(first * ROW_SL + sl, n, stride=ROW_SL), :] for sl in range(ROW_SL)], axis=1)


def _rows_to_tiles(ref, val):
    n = val.shape[0]
    for sl in range(ROW_SL):
        ref[pl.ds(sl, n, stride=ROW_SL), :] = val[:, sl * LANES:(sl + 1) * LANES]


def _dispatch_body(pos_ref, pad_start_ref, pad_n_ref, h_ref, xs_hbm, sem, *, n_tok, ne):
    i = pl.program_id(0)
    td = h_ref.shape[0] // ROW_SL
    base = i * td
    copy = lambda src, dst: pltpu.make_async_copy(_tile_of(h_ref, src), _tile_of(xs_hbm, dst), sem)
    wait_one = lambda: copy(0, 0).wait()

    def start_token(r, carry):
        copy(r, pos_ref[base + r]).start()
        copy(r, pos_ref[n_tok + base + r]).start()
        return carry

    lax.fori_loop(0, td, start_token, 0, unroll=8)
    whole_tile = pltpu.make_async_copy(h_ref, xs_hbm.at[pl.ds(0, td * ROW_SL)], sem)
    whole_tile.wait()
    whole_tile.wait()

    @pl.when(i == 0)
    def _():
        for e in range(ne):
            first = pad_start_ref[e]
            _windowed_copies(pad_n_ref[e], lambda r: copy(0, first + r).start(), wait_one, 1)


def _moe_dispatch(h3, pos, pad_start, pad_n, n_rows, td=1024):
    n_tok = h3.shape[0] // ROW_SL
    ne = pad_start.shape[0]
    return pl.pallas_call(
        functools.partial(_dispatch_body, n_tok=n_tok, ne=ne),
        out_shape=jax.ShapeDtypeStruct((n_rows * ROW_SL, LANES), h3.dtype),
        grid_spec=pltpu.PrefetchScalarGridSpec(
            num_scalar_prefetch=3, grid=(n_tok // td,),
            in_specs=[pl.BlockSpec((td * ROW_SL, LANES), lambda i, p, ps, pn: (i, 0))],
            out_specs=pl.BlockSpec(memory_space=pl.ANY),
            scratch_shapes=[pltpu.SemaphoreType.DMA(())]),
        compiler_params=_cparams(("arbitrary",)),
        name="moe_dispatch",
    )(pos, pad_start, pad_n, h3)


def _moe_grouped_body(te_ref, nu_ref, xs_ref, w1_ref, w3_ref, w2_ref, y_ref, xb_scr, acc_scr):
    i = pl.program_id(0)
    j = pl.program_id(1)
    tm = xb_scr.shape[0]

    @pl.when(i < nu_ref[0])
    def _():
        @pl.when(j == 0)
        def _():
            xb_scr[...] = _tiles_to_rows(xs_ref, tm).astype(BF16)

        h = xb_scr[...]
        a = _dot(h, w1_ref[0].astype(BF16))
        u = (a * jax.nn.sigmoid(a) * _dot(h, w3_ref[0].astype(BF16))).astype(BF16)
        part = _dot(u, w2_ref[0].astype(BF16))

        @pl.when(j == 0)
        def _():
            acc_scr[...] = part

        @pl.when(j > 0)
        def _():
            acc_scr[...] += part

        @pl.when(j == pl.num_programs(1) - 1)
        def _():
            _rows_to_tiles(y_ref, acc_scr[...])


def _moe_grouped(xs, tile_expert, n_used, w1, w3, w2, tm, fb=512):
    ne, d, f = w1.shape
    n_rows = xs.shape[0] // ROW_SL
    nj = f // fb
    row_tile = lambda i, j, te, nu: (jnp.maximum(jnp.minimum(i, nu[0] - 1), 0), 0)
    jj = lambda i, j, nu: jnp.where(i < nu[0], j, nj - 1)
    return pl.pallas_call(
        _moe_grouped_body,
        out_shape=jax.ShapeDtypeStruct(xs.shape, F32),
        grid_spec=pltpu.PrefetchScalarGridSpec(
            num_scalar_prefetch=2, grid=(n_rows // tm, nj),
            in_specs=[pl.BlockSpec((tm * ROW_SL, LANES), row_tile),
                      pl.BlockSpec((1, d, fb), lambda i, j, te, nu: (te[i], 0, jj(i, j, nu))),
                      pl.BlockSpec((1, d, fb), lambda i, j, te, nu: (te[i], 0, jj(i, j, nu))),
                      pl.BlockSpec((1, fb, d), lambda i, j, te, nu: (te[i], jj(i, j, nu), 0))],
            out_specs=pl.BlockSpec((tm * ROW_SL, LANES), row_tile),
            scratch_shapes=[pltpu.VMEM((tm, d), BF16), pltpu.VMEM((tm, d), F32)]),
        compiler_params=_cparams(("arbitrary", "arbitrary")),
        name="moe_grouped",
    )(tile_expert, n_used, xs, w1, w3, w2)


def _moe_final_body(pos_ref, x_ref, y_hbm, gw_ref, gt_ref, fg_ref, o_ref, yg_scr, sem, *, n_tok):
    i = pl.program_id(0)
    tc = x_ref.shape[0]
    slot = i % 2

    def gather_tile(step, into):
        base = step * tc

        def start_token(r, carry):
            dst = yg_scr.at[into]
            pltpu.make_async_copy(_tile_of(y_hbm, pos_ref[base + r]), _tile_of(dst, r), sem.at[into]).start()
            pltpu.make_async_copy(_tile_of(y_hbm, pos_ref[n_tok + base + r]), _tile_of(dst, tc + r),
                                  sem.at[into]).start()
            return carry

        lax.fori_loop(0, tc, start_token, 0, unroll=8)

    @pl.when(i == 0)
    def _():
        gather_tile(0, 0)

    @pl.when(i + 1 < pl.num_programs(0))
    def _():
        gather_tile(i + 1, 1 - slot)

    pltpu.make_async_copy(y_hbm.at[pl.ds(0, 2 * tc * ROW_SL)], yg_scr.at[slot], sem.at[slot]).wait()
    gw = gw_ref[...]
    rows = yg_scr.at[slot]
    y = gw[:, 0:1] * _tiles_to_rows(rows, tc) + gw[:, 1:2] * _tiles_to_rows(rows, tc, first=tc)
    xo = x_ref[...] + gt_ref[0] * y
    ms = jnp.mean(xo * xo, axis=-1, keepdims=True)
    o_ref[...] = xo * lax.rsqrt(ms + EPS) * fg_ref[...]


def _moe_final(x, y, pos, gw, gt, final_g, tc=256):
    b, s, d = x.shape
    t = b * s
    spt = s // tc
    nsl = d // LANES
    out = pl.pallas_call(
        functools.partial(_moe_final_body, n_tok=t),
        out_shape=jax.ShapeDtypeStruct((t, d), F32),
        grid_spec=pltpu.PrefetchScalarGridSpec(
            num_scalar_prefetch=1, grid=(t // tc,),
            in_specs=[pl.BlockSpec((tc, d), lambda i, p: (i, 0)),
                      pl.BlockSpec(memory_space=pl.ANY),
                      pl.BlockSpec((tc, LANES), lambda i, p: (i, 0)),
                      pl.BlockSpec((1, 1, d), lambda i, p: (i // spt, 0, 0)),
                      pl.BlockSpec(final_g.shape, lambda i, p: (0, 0))],
            out_specs=pl.BlockSpec((tc, d), lambda i, p: (i, 0)),
            scratch_shapes=[pltpu.VMEM((2, 2 * tc * ROW_SL, LANES), F32), pltpu.SemaphoreType.DMA((2,))]),
        compiler_params=_cparams(("arbitrary",)),
        name="moe_final",
    )(pos, x.reshape(t, d), y, gw, gt, final_g)
    return out.reshape(b, s, d)


def _moe_routed(x, y_f, w_f, gate_f, g, shift, scale, gt, final_g, w_router, b_router, w1, w3, w2):
    ne = w1.shape[0]
    tm = MOE_TM
    x1, h3, meta, gw, counts = _router(x, y_f, w_f, gate_f, g, shift, scale, w_router, b_router)
    pos, pad_start, pad_n, tile_expert, n_used = _moe_plan(meta, counts, ne, tm)
    assert x.shape[-1] == ROW_SL * LANES
    n_rows = (2 * (h3.shape[0] // ROW_SL) // tm + ne) * tm
    xs = _moe_dispatch(h3, pos, pad_start, pad_n, n_rows)
    y = _moe_grouped(xs, tile_expert, n_used, w1, w3, w2, tm)
    return _moe_final(x1.reshape(x.shape), y, pos, gw, gt, final_g)


def kernel(x, c, ctx, c_ctx, w_ada, b_ada, norm_g, w_in, hy_short_w, hy_short_b, hy_f_w1, hy_f_b1, hy_f_w2, hy_f_b2, hy_f_w3, hy_f_freq, hy_skip, na_rpb, w_mix_out, ffn_w1, ffn_w3, ffn_w2, w_fourier, w_router, b_router, moe_w1, moe_w3, moe_w2, final_g):
    b, s, d = x.shape
    depth = w_ada.shape[0]
    assert depth == 2, "layer 0 mixes with Hyena/attention, layer 1 with Fourier/MoE"
    c_hy = hy_skip.shape[-1]
    c_na = d - c_hy

    cvec = jnp.concatenate([c, c_ctx[None, :], jnp.zeros((8 - b - 1, d), F32)], axis=0)
    mods = _ada(cvec, w_ada, b_ada)

    def mod(layer, idx, ctx_row=False):
        m = mods[layer, :, idx * d:(idx + 1) * d]
        return m[b:b + 1, None, :] if ctx_row else m[0:b, None, :]

    row = lambda a: a.reshape(1, -1)

    w_in0 = w_in[0].astype(BF16)
    w_hy, w_qkv = w_in0[:, 0:3 * c_hy], w_in0[:, 3 * c_hy:]
    v, x1, x2, q, k, va = _inproj(x, row(norm_g[0, 0]), mod(0, 0), mod(0, 1), w_hy, w_qkv,
                                  hy_short_w[0], row(hy_short_b[0]))
    kc, vc = _ctxkv(ctx, row(norm_g[0, 0]), mod(0, 0, True), mod(0, 1, True), w_qkv[:, c_na:])
    y_na = _natt(q, k, va, kc, vc, _na_bias_table(na_rpb[0]))
    y_hy = _hyena(v, x1, x2, hy_f_w1[0], hy_f_b1[0], hy_f_w2[0], hy_f_b2[0], hy_f_w3[0],
                  hy_f_freq[0], hy_skip[0])
    x = _mix_ffn(x, y_hy, y_na, w_mix_out[0].astype(BF16), mod(0, 2),
                 row(norm_g[0, 1]), mod(0, 3), mod(0, 4), mod(0, 5),
                 ffn_w1[0].astype(BF16), ffn_w3[0].astype(BF16), ffn_w2[0].astype(BF16))

    y_f = _fourier_mix(x, row(norm_g[1, 0]), mod(1, 0), mod(1, 1))
    return _moe_routed(x, y_f, w_fourier[0].astype(BF16), mod(1, 2),
                       row(norm_g[1, 1]), mod(1, 3), mod(1, 4), mod(1, 5), row(final_g),
                       w_router[0], b_router[0],
                       moe_w1[0], moe_w3[0], moe_w2[0])
```

```python
import functools
import math

import numpy as np
import jax
import jax.numpy as jnp
from jax import lax
from jax.experimental import pallas as pl
from jax.experimental.pallas import tpu as pltpu

F32 = jnp.float32
BF16 = jnp.bfloat16
HIGHEST = lax.Precision.HIGHEST

GRID_W = 64
NA_HEAD_DIM = 32
NA_WIN_R = 8
NA_WIN_C = 16
HYENA_EMB = 33
HYENA_BANDS = (HYENA_EMB - 1) // 2
HYENA_FAST_DECAY = 0.3
HYENA_SLOW_DECAY = 1.5
HYENA_TARGET = 1e-2
F_GROUPS = 4
N_MOD = 6
EPS = 1e-6
NEG_INF = -1e30

FFT_A = 64
FFT_R = 128
FFT_KA = FFT_A // 2 + 1
FFT_KA_PAD = 40
FM_A = 64

VMEM_LIMIT = 48 * 1024 * 1024


def _cparams(sem, vmem_limit=VMEM_LIMIT):
    return pltpu.CompilerParams(dimension_semantics=sem, vmem_limit_bytes=vmem_limit)


def _dot(a, b):
    return jnp.dot(a, b, preferred_element_type=F32)


def _mxu_const(m):
    return jnp.asarray(m, dtype=F32).astype(BF16)


def _norm_mod(x, g, shift, scale):
    ms = jnp.mean(x * x, axis=-1, keepdims=True)
    y = x * lax.rsqrt(ms + EPS) * g
    return y * (1.0 + scale) + shift


def _ada_body(c_ref, w_ref, b_ref, o_ref):
    cv = c_ref[...]
    s = cv * jax.nn.sigmoid(cv)
    o_ref[0] = jnp.dot(s, w_ref[0], precision=HIGHEST, preferred_element_type=F32) + b_ref[0]


def _ada(cvec, w_ada, b_ada):
    depth, d, n = w_ada.shape
    rows = cvec.shape[0]
    bn = n // 4
    return pl.pallas_call(
        _ada_body,
        out_shape=jax.ShapeDtypeStruct((depth, rows, n), F32),
        grid=(depth, n // bn),
        in_specs=[pl.BlockSpec((rows, d), lambda l, j: (0, 0)),
                  pl.BlockSpec((1, d, bn), lambda l, j: (l, 0, j)),
                  pl.BlockSpec((1, 1, bn), lambda l, j: (l, 0, j))],
        out_specs=pl.BlockSpec((1, rows, bn), lambda l, j: (l, 0, j)),
        compiler_params=_cparams(("parallel", "parallel")),
        name="ada",
    )(cvec, w_ada, b_ada.reshape(depth, 1, n))


def _inproj_body(x_ref, xp_ref, xn_ref, g_ref, sh_ref, sc_ref, why_ref, wqkv_ref, sw_ref, sb_ref,
                 v_ref, x1_ref, x2_ref, q_ref, k_ref, va_ref, *, n_tiles, q_scale, c_hy, c_na):
    i = pl.program_id(1)
    g, sh, sc = g_ref[...], sh_ref[0], sc_ref[0]
    hf = _norm_mod(x_ref[0], g, sh, sc)
    h = hf.astype(BF16)
    tm = hf.shape[0]
    hx = jnp.concatenate([_norm_mod(xp_ref[0], g, sh, sc), hf, _norm_mod(xn_ref[0], g, sh, sc)], axis=0)
    zx = _dot(hx.astype(BF16), why_ref[...])
    zh = zx[8:8 + tm]
    zp = jnp.where(i > 0, zx[7:8], 0.0)
    zn = jnp.where(i < n_tiles - 1, zx[8 + tm:9 + tm], 0.0)
    row = lax.broadcasted_iota(jnp.int32, zh.shape, 0)
    z_m1 = jnp.where(row == 0, zp, pltpu.roll(zh, 1, 0))
    z_p1 = jnp.where(row == tm - 1, zn, pltpu.roll(zh, tm - 1, 0))
    sw = sw_ref[...]
    zc = z_m1 * sw[0:1] + zh * sw[1:2] + z_p1 * sw[2:3] + sb_ref[...]
    v_ref[0] = zc[:, 0:c_hy].astype(BF16)
    x1_ref[0] = zc[:, c_hy:2 * c_hy].astype(BF16)
    x2_ref[0] = zc[:, 2 * c_hy:3 * c_hy].astype(BF16)
    zq = _dot(h, wqkv_ref[...])
    q_ref[0] = (zq[:, 0:c_na] * q_scale).astype(BF16)
    k_ref[0] = zq[:, c_na:2 * c_na].astype(BF16)
    va_ref[0] = zq[:, 2 * c_na:3 * c_na].astype(BF16)


def _inproj(x, g, shift, scale, w_hy, w_qkv, short_w, short_b, tm=512):
    b, s, d = x.shape
    c_hy = w_hy.shape[1] // 3
    c_na = w_qkv.shape[1] // 3
    n_tiles = s // tm
    r8 = tm // 8
    body = functools.partial(_inproj_body, n_tiles=n_tiles, q_scale=NA_HEAD_DIM ** -0.5,
                             c_hy=c_hy, c_na=c_na)
    tok = lambda c: pl.BlockSpec((1, tm, c), lambda bi, i: (bi, i, 0))
    full2 = lambda a: pl.BlockSpec(a.shape, lambda bi, i: (0, 0))
    per_b = pl.BlockSpec((1, 1, d), lambda bi, i: (bi, 0, 0))
    return pl.pallas_call(
        body,
        out_shape=[jax.ShapeDtypeStruct((b, s, c_hy), BF16)] * 3 + [jax.ShapeDtypeStruct((b, s, c_na), BF16)] * 3,
        grid=(b, n_tiles),
        in_specs=[tok(d),
                  pl.BlockSpec((1, 8, d), lambda bi, i: (bi, jnp.maximum(i * r8 - 1, 0), 0)),
                  pl.BlockSpec((1, 8, d), lambda bi, i: (bi, jnp.minimum((i + 1) * r8, s // 8 - 1), 0)),
                  full2(g), per_b, per_b, full2(w_hy), full2(w_qkv), full2(short_w), full2(short_b)],
        out_specs=[tok(c_hy)] * 3 + [tok(c_na)] * 3,
        compiler_params=_cparams(("parallel", "parallel")),
        name="inproj",
    )(x, x, x, g, shift, scale, w_hy, w_qkv, short_w, short_b)


def _ctxkv_body(x_ref, g_ref, sh_ref, sc_ref, w_ref, k_ref, v_ref, *, c_na):
    h = _norm_mod(x_ref[0], g_ref[...], sh_ref[0], sc_ref[0]).astype(BF16)
    z = _dot(h, w_ref[...])
    k_ref[0] = z[:, 0:c_na].astype(BF16)
    v_ref[0] = z[:, c_na:2 * c_na].astype(BF16)


def _ctxkv(ctx, g, shift, scale, w_kv):
    b, n, d = ctx.shape
    c_na = w_kv.shape[1] // 2
    one = pl.BlockSpec((1, 1, d), lambda bi: (0, 0, 0))
    return pl.pallas_call(
        functools.partial(_ctxkv_body, c_na=c_na),
        out_shape=[jax.ShapeDtypeStruct((b, n, c_na), BF16)] * 2,
        grid=(b,),
        in_specs=[pl.BlockSpec((1, n, d), lambda bi: (bi, 0, 0)),
                  pl.BlockSpec(g.shape, lambda bi: (0, 0)), one, one,
                  pl.BlockSpec(w_kv.shape, lambda bi: (0, 0))],
        out_specs=[pl.BlockSpec((1, n, c_na), lambda bi: (bi, 0, 0))] * 2,
        compiler_params=_cparams(("parallel",)),
        name="ctxkv",
    )(ctx, g, shift, scale, w_kv)


NA_HEADS_PER_BLK = 4
NA_ROWS_PER_STEP = 4


def _na_bias_body(r_ref, e_ref, ok_ref, o_ref):
    t = jnp.dot(r_ref[...], e_ref[...], precision=HIGHEST, preferred_element_type=F32)
    o_ref[...] = jnp.where(ok_ref[...] > 0.5, t, NEG_INF)


def _na_bias_table(rpb):
    w = GRID_W
    h, nr, nc = rpb.shape
    col = np.arange(w)[:, None]
    kc = np.arange(w)[None, :]
    c_start = np.clip(col - NA_WIN_C // 2, 0, w - NA_WIN_C)
    valid = ((kc >= c_start) & (kc < c_start + NA_WIN_C)).reshape(1, w * w)
    expand = (np.arange(32)[:, None, None] == (kc - col + NA_WIN_C - 1)[None]).reshape(32, w * w)
    rp = jnp.pad(rpb.astype(F32).reshape(h * nr, nc), ((0, 0), (0, 32 - nc)))
    full = lambda a: pl.BlockSpec(a.shape, lambda: (0,) * a.ndim)
    expand = jnp.asarray(expand, dtype=F32)
    ok = jnp.asarray(valid, dtype=F32)
    toep = pl.pallas_call(
        _na_bias_body,
        out_shape=jax.ShapeDtypeStruct((h * nr, w * w), F32),
        in_specs=[full(rp), full(expand), full(ok)],
        out_specs=pl.BlockSpec((h * nr, w * w), lambda: (0, 0)),
        name="na_bias",
    )(rp, expand, ok)
    t2 = toep.reshape(h, nr, w, w).transpose(0, 2, 1, 3).reshape(h, w, nr * w)
    slabs = jnp.stack([t2[:, :, (NA_WIN_R - 1 - off) * w:(2 * NA_WIN_R - 1 - off) * w]
                       for off in range(NA_WIN_R)], axis=1)
    hpb = NA_HEADS_PER_BLK
    slabs = slabs.reshape(h // hpb, hpb, NA_WIN_R, w, NA_WIN_R * w).transpose(0, 2, 1, 3, 4)
    return slabs.reshape(h // hpb, NA_WIN_R, hpb * w, NA_WIN_R * w)


def _natt_body(q_ref, k_ref, v_ref, kc_ref, vc_ref, bias_ref, o_ref, *, rows):
    w = GRID_W
    hpb = NA_HEADS_PER_BLK
    nloc = NA_WIN_R * w
    lane = lax.broadcasted_iota(jnp.int32, (1, hpb * NA_HEAD_DIM), 1)
    in_head = [(lane >= NA_HEAD_DIM * hh) & (lane < NA_HEAD_DIM * (hh + 1)) for hh in range(hpb)]
    kcx = kc_ref[0]
    vcx = vc_ref[0]
    nt = (((1,), (1,)), ((), ()))

    def one_row(r):
        r0 = jnp.clip(r - NA_WIN_R // 2, 0, rows - NA_WIN_R)
        off = r - r0
        qs = q_ref[0, pl.ds(pl.multiple_of(r * w, w), w), :]
        kw = k_ref[0, pl.ds(pl.multiple_of(r0 * w, w), nloc), :]
        vw = v_ref[0, pl.ds(pl.multiple_of(r0 * w, w), nloc), :]
        zero = jnp.zeros_like(qs)
        qst = jnp.concatenate([jnp.where(m, qs, zero) for m in in_head], axis=0)
        s_loc = lax.dot_general(qst, kw, nt, preferred_element_type=F32) + bias_ref[0, off]
        s_ctx = lax.dot_general(qst, kcx, nt, preferred_element_type=F32)
        m = jnp.maximum(jnp.max(s_loc, axis=-1, keepdims=True), jnp.max(s_ctx, axis=-1, keepdims=True))
        p_loc = jnp.exp(s_loc - m)
        p_ctx = jnp.exp(s_ctx - m)
        den = jnp.sum(p_loc, axis=-1, keepdims=True) + jnp.sum(p_ctx, axis=-1, keepdims=True)
        o = (_dot(p_loc.astype(BF16), vw) + _dot(p_ctx.astype(BF16), vcx)) * (1.0 / den)
        acc = jnp.where(in_head[0], o[0:w], 0.0)
        for hh in range(1, hpb):
            acc = acc + jnp.where(in_head[hh], o[hh * w:(hh + 1) * w], 0.0)
        o_ref[0, pl.ds(pl.multiple_of(r * w, w), w), :] = acc.astype(BF16)

    def row_group(i, carry):
        for r in range(NA_ROWS_PER_STEP):
            one_row(NA_ROWS_PER_STEP * i + r)
        return carry

    lax.fori_loop(0, rows // NA_ROWS_PER_STEP, row_group, 0)


def _natt(q, k, v, kc, vc, bias):
    b, s, c = q.shape
    nctx = kc.shape[1]
    lw = NA_HEADS_PER_BLK * NA_HEAD_DIM
    rows = s // GRID_W
    seq = pl.BlockSpec((1, s, lw), lambda bi, g: (bi, 0, g))
    cx = pl.BlockSpec((1, nctx, lw), lambda bi, g: (bi, 0, g))
    return pl.pallas_call(
        functools.partial(_natt_body, rows=rows),
        out_shape=jax.ShapeDtypeStruct((b, s, c), BF16),
        grid=(b, c // lw),
        in_specs=[seq, seq, seq, cx, cx,
                  pl.BlockSpec((1,) + bias.shape[1:], lambda bi, g: (g, 0, 0, 0))],
        out_specs=seq,
        compiler_params=_cparams(("parallel", "parallel")),
        name="natt",
    )(q, k, v, kc, vc, bias)


def _hyena_feats(seq_len):
    t = jnp.linspace(0.0, 1.0, seq_len, dtype=F32)[:, None]
    bands = jnp.linspace(1e-4, HYENA_BANDS - 1, HYENA_BANDS, dtype=F32)
    ang = (2.0 * math.pi / seq_len) * jnp.arange(seq_len, dtype=F32)[:, None] * bands[None, :]
    feats = jnp.concatenate([t, jnp.cos(ang), -jnp.sin(ang)], axis=-1)
    return jnp.pad(feats, ((0, 0), (0, 128 - HYENA_EMB)))


def _filt_body(feat_ref, w1_ref, b1_ref, w2_ref, b2_ref, w3_ref, fr_ref, dl_ref, o_ref, l1_ref, h_scr):
    j = pl.program_id(0)
    hp = functools.partial(jnp.dot, precision=HIGHEST, preferred_element_type=F32)
    feats = feat_ref[...]

    @pl.when(j == 0)
    def _():
        fr = fr_ref[...]
        h = jnp.sin(fr[0:1] * (hp(feats, w1_ref[...]) + b1_ref[...]))
        h_scr[...] = jnp.sin(fr[1:2] * (hp(h, w2_ref[...]) + b2_ref[...]))

    hc = hp(h_scr[...], w3_ref[...])
    t = feats[:, 0:1]
    hc = hc * jnp.exp(-t * dl_ref[...])
    row = lax.broadcasted_iota(jnp.int32, hc.shape, 0)
    hc = jnp.where((row == 0) & (j % 2 == 1), 0.0, hc)
    l1_ref[0] = jnp.sum(jnp.abs(hc), axis=0, keepdims=True)
    o_ref[0] = hc.astype(BF16)


def _hyena_filter_taps(seq_len, f_w1, f_b1, f_w2, f_b2, f_w3, f_freq, c_hy):
    feats = _hyena_feats(seq_len)
    hid = f_w1.shape[1]
    w1 = jnp.pad(f_w1.astype(F32), ((0, 128 - HYENA_EMB), (0, 0)))
    deltas = jnp.abs(jnp.linspace(math.log(HYENA_TARGET) / HYENA_SLOW_DECAY,
                                  math.log(HYENA_TARGET) / HYENA_FAST_DECAY, c_hy, dtype=F32))[None, :]
    nblk = f_w3.shape[1] // c_hy
    c0 = lambda a: pl.BlockSpec(a.shape, lambda j: (0, 0))
    b1, b2 = f_b1.reshape(1, hid), f_b2.reshape(1, hid)
    return pl.pallas_call(
        _filt_body,
        out_shape=[jax.ShapeDtypeStruct((nblk, seq_len, c_hy), BF16),
                   jax.ShapeDtypeStruct((nblk, 1, c_hy), F32)],
        grid=(nblk,),
        in_specs=[c0(feats), c0(w1), c0(b1), c0(f_w2), c0(b2),
                  pl.BlockSpec((hid, c_hy), lambda j: (0, j)), c0(f_freq), c0(deltas)],
        out_specs=[pl.BlockSpec((1, seq_len, c_hy), lambda j: (j, 0, 0)),
                   pl.BlockSpec((1, 1, c_hy), lambda j: (j, 0, 0))],
        scratch_shapes=[pltpu.VMEM((seq_len, hid), F32)],
        compiler_params=_cparams(("arbitrary",)),
        name="hyena_filter",
    )(feats, w1, b1, f_w2, b2, f_w3, f_freq, deltas)


def _conv_dft_constants():
    a_half = FFT_A // 2
    n = FFT_A * FFT_R
    ka = np.arange(FFT_KA)[:, None]
    a = np.arange(a_half)[None, :]
    ph = 2.0 * np.pi * ka * a / FFT_A
    m_fwd = np.zeros((2 * FFT_KA_PAD, a_half))
    m_fwd[:FFT_KA] = np.cos(ph)
    m_fwd[FFT_KA_PAD:FFT_KA_PAD + FFT_KA] = -np.sin(ph)
    wgt = np.where((ka == 0) | (ka == FFT_A // 2), 1.0, 2.0)
    m_inv = np.zeros((a_half, 2 * FFT_KA_PAD))
    m_inv[:, :FFT_KA] = (wgt * np.cos(ph)).T / n
    m_inv[:, FFT_KA_PAD:FFT_KA_PAD + FFT_KA] = (-wgt * np.sin(ph)).T / n
    kb = np.arange(FFT_R)[None, :, None]
    b = np.arange(FFT_R)[None, None, :]
    kaa = np.arange(FFT_KA)[:, None, None]
    th = 2.0 * np.pi * (b * kb / FFT_R + b * kaa / n)
    gr, gi = np.cos(th), -np.sin(th)
    g2 = np.zeros((FFT_KA_PAD, 2 * FFT_R, 2 * FFT_R))
    g2[:FFT_KA] = np.block([[gr, -gi], [gi, gr]])
    grt, git = gr.transpose(0, 2, 1), gi.transpose(0, 2, 1)
    g2h = np.zeros_like(g2)
    g2h[:FFT_KA] = np.block([[grt, git], [-git, grt]])
    return _mxu_const(m_fwd), _mxu_const(m_inv), _mxu_const(g2), _mxu_const(g2h)


def _fwd1_body(m_ref, u_ref, o_ref):
    res = _dot(m_ref[...], u_ref[0])
    o_ref[0, 0] = res[0:FFT_KA_PAD]
    o_ref[0, 1] = res[FFT_KA_PAD:2 * FFT_KA_PAD]


def _conv_fwd1(u, m_fwd, cb=8192):
    n, seq, c = u.shape
    a_half = FFT_A // 2
    cols = seq * c // a_half
    uv = u.reshape(n, a_half, cols)
    return pl.pallas_call(
        _fwd1_body,
        out_shape=jax.ShapeDtypeStruct((n, 2, FFT_KA_PAD, cols), F32),
        grid=(n, cols // cb),
        in_specs=[pl.BlockSpec(m_fwd.shape, lambda i, j: (0, 0)),
                  pl.BlockSpec((1, a_half, cb), lambda i, j: (i, 0, j))],
        out_specs=pl.BlockSpec((1, 2, FFT_KA_PAD, cb), lambda i, j: (i, 0, 0, j)),
        compiler_params=_cparams(("parallel", "parallel")),
        name="conv_fwd1",
    )(m_fwd, uv)


FFT_KB = 8


def _rows_to_slabs(src_ref, dst_scr, c):
    for part in range(2):
        for b in range(FFT_R):
            dst_scr[part, :, b, :] = src_ref[0, part, :, b * c:(b + 1) * c]


def _slabs_to_rows(src_scr, dst_ref, c):
    for part in range(2):
        for b in range(FFT_R):
            dst_ref[0, part, :, b * c:(b + 1) * c] = src_scr[part, :, b, :]


def _slab(scr, i):
    return jnp.concatenate([scr[0, i], scr[1, i]], axis=0).astype(BF16)


def _fwd2f_body(sf_ref, sb_ref, g_ref, l1_ref, kf_ref, f3, b3):
    o = pl.program_id(0)
    j = pl.program_id(1)
    r2 = 2 * FFT_R
    c = kf_ref.shape[-1]
    _rows_to_slabs(sf_ref, f3, c)
    _rows_to_slabs(sb_ref, b3, c)
    inv = 1.0 / (l1_ref[2 * o] + l1_ref[2 * o + 1] + EPS)
    for i in range(FFT_KB):
        @pl.when(j * FFT_KB + i < FFT_KA)
        def _():
            xf = _dot(g_ref[i], _slab(f3, i))
            xb = _dot(g_ref[i], _slab(b3, i))
            kf_ref[0, i, 0:FFT_R] = (xf[0:FFT_R] + xb[0:FFT_R]) * inv
            kf_ref[0, i, FFT_R:r2] = (xf[FFT_R:r2] - xb[FFT_R:r2]) * inv

        @pl.when(j * FFT_KB + i >= FFT_KA)
        def _():
            kf_ref[0, i] = jnp.zeros((r2, c), F32)


def _filter_spectrum(s_filt, l1, g2, c):
    n_ord = s_filt.shape[0] // 2
    cols = s_filt.shape[-1]
    r2 = 2 * FFT_R
    return pl.pallas_call(
        _fwd2f_body,
        out_shape=jax.ShapeDtypeStruct((n_ord, FFT_KA_PAD, r2, c), F32),
        grid=(n_ord, FFT_KA_PAD // FFT_KB),
        in_specs=[pl.BlockSpec((1, 2, FFT_KB, cols), lambda o, j: (2 * o, 0, j, 0)),
                  pl.BlockSpec((1, 2, FFT_KB, cols), lambda o, j: (2 * o + 1, 0, j, 0)),
                  pl.BlockSpec((FFT_KB, r2, r2), lambda o, j: (j, 0, 0)),
                  pl.BlockSpec(l1.shape, lambda o, j: (0, 0, 0))],
        out_specs=pl.BlockSpec((1, FFT_KB, r2, c), lambda o, j: (o, j, 0, 0)),
        scratch_shapes=[pltpu.VMEM((2, FFT_KB, FFT_R, c), F32)] * 2,
        compiler_params=_cparams(("parallel", "parallel")),
        name="filter_spectrum",
    )(s_filt, s_filt, g2, l1)


def _mid_body(s_ref, g_ref, gh_ref, kf_ref, t_ref, s3, t3):
    j = pl.program_id(1)
    r2 = 2 * FFT_R
    c = kf_ref.shape[-1]
    _rows_to_slabs(s_ref, s3, c)
    for i in range(FFT_KB):
        @pl.when(j * FFT_KB + i < FFT_KA)
        def _():
            x = _dot(g_ref[i], _slab(s3, i))
            xr, xi = x[0:FFT_R], x[FFT_R:r2]
            kr, ki = kf_ref[0, i, 0:FFT_R], kf_ref[0, i, FFT_R:r2]
            y = jnp.concatenate([xr * kr - xi * ki, xr * ki + xi * kr], axis=0).astype(BF16)
            t = _dot(gh_ref[i], y)
            t3[0, i] = t[0:FFT_R]
            t3[1, i] = t[FFT_R:r2]

        @pl.when(j * FFT_KB + i >= FFT_KA)
        def _():
            t3[0, i] = jnp.zeros((FFT_R, c), F32)
            t3[1, i] = jnp.zeros((FFT_R, c), F32)

    _slabs_to_rows(t3, t_ref, c)


def _conv_mid(s, kf, order, g2, g2h, c):
    n, _, _, cols = s.shape
    r2 = 2 * FFT_R
    blk = pl.BlockSpec((1, 2, FFT_KB, cols), lambda i, j: (i, 0, j, 0))
    gspec = pl.BlockSpec((FFT_KB, r2, r2), lambda i, j: (j, 0, 0))
    return pl.pallas_call(
        _mid_body,
        out_shape=jax.ShapeDtypeStruct(s.shape, F32),
        grid=(n, FFT_KA_PAD // FFT_KB),
        in_specs=[blk, gspec, gspec,
                  pl.BlockSpec((1, FFT_KB, r2, c), lambda i, j: (order, j, 0, 0))],
        out_specs=blk,
        scratch_shapes=[pltpu.VMEM((2, FFT_KB, FFT_R, c), F32)] * 2,
        compiler_params=_cparams(("parallel", "parallel")),
        name="conv_mid",
    )(s, g2, g2h, kf)


def _inv1_body(m_ref, t_ref, u_ref, xg_ref, sk_ref, o_ref):
    cb = t_ref.shape[-1]
    t2 = t_ref[0].reshape(2 * FFT_KA_PAD, cb).astype(BF16)
    y = _dot(m_ref[...], t2)
    u = u_ref[0].astype(F32)
    o_ref[0] = (xg_ref[0].astype(F32) * (y + u * sk_ref[...])).astype(BF16)


def _conv_inv1(t, u, xg, skip, m_inv, cb=8192):
    n, seq, c = u.shape
    a_half = FFT_A // 2
    cols = seq * c // a_half
    sk = jnp.tile(skip.astype(F32).reshape(1, c), (1, cols // c))
    uspec = pl.BlockSpec((1, a_half, cb), lambda i, j: (i, 0, j))
    out = pl.pallas_call(
        _inv1_body,
        out_shape=jax.ShapeDtypeStruct((n, a_half, cols), BF16),
        grid=(n, cols // cb),
        in_specs=[pl.BlockSpec(m_inv.shape, lambda i, j: (0, 0)),
                  pl.BlockSpec((1, 2, FFT_KA_PAD, cb), lambda i, j: (i, 0, 0, j)),
                  uspec, uspec,
                  pl.BlockSpec((1, cb), lambda i, j: (0, j))],
        out_specs=uspec,
        compiler_params=_cparams(("parallel", "parallel")),
        name="conv_inv1",
    )(m_inv, t, u.reshape(n, a_half, cols), xg.reshape(n, a_half, cols), sk)
    return out.reshape(n, seq, c)


def _hyena(v, x1, x2, f_w1, f_b1, f_w2, f_b2, f_w3, f_freq, skip):
    _, seq, c = v.shape
    assert 2 * seq == FFT_A * FFT_R
    m_fwd, m_inv, g2, g2h = _conv_dft_constants()
    taps, l1 = _hyena_filter_taps(seq, f_w1, f_b1, f_w2, f_b2, f_w3, f_freq, c)
    kf = _filter_spectrum(_conv_fwd1(taps, m_fwd), l1, g2, c)
    y = v
    for order, xg in enumerate((x1, x2)):
        t = _conv_mid(_conv_fwd1(y, m_fwd), kf, order, g2, g2h, c)
        y = _conv_inv1(t, y, xg, skip[order], m_inv)
    return y


def _mix_ffn_body(x_ref, a1_ref, a2_ref, wm_ref, gm_ref, g_ref, sh_ref, sc_ref, gt_ref,
                  w1_ref, w3_ref, w2_ref, o_ref, x_scr, h_scr, acc_scr):
    j = pl.program_id(2)

    @pl.when(j == 0)
    def _():
        c1 = a1_ref.shape[-1]
        mixed = _dot(a1_ref[0], wm_ref[0:c1]) + _dot(a2_ref[0], wm_ref[c1:])
        xm = x_ref[0] + gm_ref[0] * mixed
        x_scr[...] = xm
        h_scr[...] = _norm_mod(xm, g_ref[...], sh_ref[0], sc_ref[0]).astype(BF16)
        acc_scr[...] = jnp.zeros_like(acc_scr)

    h = h_scr[...]
    a = _dot(h, w1_ref[...])
    u = (a * jax.nn.sigmoid(a) * _dot(h, w3_ref[...])).astype(BF16)
    acc_scr[...] += _dot(u, w2_ref[...])

    @pl.when(j == pl.num_programs(2) - 1)
    def _():
        o_ref[0] = x_scr[...] + gt_ref[0] * acc_scr[...]


def _mix_ffn(x, a1, a2, w_mix, gate_mix, g, shift, scale, gate, w1, w3, w2, tm=512, fb=1408):
    b, s, d = x.shape
    f = w1.shape[1]
    tok = lambda c: pl.BlockSpec((1, tm, c), lambda bi, i, j: (bi, i, 0))
    per_b = pl.BlockSpec((1, 1, d), lambda bi, i, j: (bi, 0, 0))
    const = lambda a: pl.BlockSpec(a.shape, lambda bi, i, j: (0, 0))
    return pl.pallas_call(
        _mix_ffn_body,
        out_shape=jax.ShapeDtypeStruct(x.shape, F32),
        grid=(b, s // tm, f // fb),
        in_specs=[tok(d), tok(a1.shape[-1]), tok(a2.shape[-1]), const(w_mix), per_b,
                  const(g), per_b, per_b, per_b,
                  pl.BlockSpec((d, fb), lambda bi, i, j: (0, j)),
                  pl.BlockSpec((d, fb), lambda bi, i, j: (0, j)),
                  pl.BlockSpec((fb, d), lambda bi, i, j: (j, 0))],
        out_specs=tok(d),
        scratch_shapes=[pltpu.VMEM((tm, d), F32), pltpu.VMEM((tm, d), BF16), pltpu.VMEM((tm, d), F32)],
        compiler_params=_cparams(("parallel", "parallel", "arbitrary")),
        name="mix_ffn",
    )(x, a1, a2, w_mix, gate_mix, g, shift, scale, gate, w1, w3, w2)


def _fm_constants(cg):
    j = np.arange(cg)[:, None]
    m = np.arange(cg)[None, :]
    ph = 2.0 * np.pi * j * m / cg
    w_cs = np.concatenate([np.cos(ph), np.sin(ph)], axis=1)
    d = np.arange(FM_A)[:, None]
    a = np.arange(FM_A)[None, :]
    ph = 2.0 * np.pi * d * a / FM_A
    fr, fi = np.cos(ph), -np.sin(ph)
    m1 = np.block([[fr, fi], [fi, -fr]])
    n = FM_A * FM_A
    dd = np.arange(FM_A)[:, None, None]
    c = np.arange(FM_A)[None, :, None]
    b = np.arange(FM_A)[None, None, :]
    th = 2.0 * np.pi * (b * c / FM_A + b * dd / n)
    gcat = np.concatenate([np.cos(th), np.sin(th)], axis=2)
    return _mxu_const(w_cs), _mxu_const(m1), _mxu_const(gcat)


def _fm_chan_body(x_ref, g_ref, sh_ref, sc_ref, w_ref, o_ref, *, cg):
    h = _norm_mod(x_ref[0], g_ref[...], sh_ref[0], sc_ref[0]).astype(BF16)
    for grp in range(h.shape[-1] // cg):
        pq = _dot(h[:, grp * cg:(grp + 1) * cg], w_ref[...])
        o_ref[0, 0, :, grp * cg:(grp + 1) * cg] = pq[:, 0:cg].astype(BF16)
        o_ref[0, 1, :, grp * cg:(grp + 1) * cg] = pq[:, cg:2 * cg].astype(BF16)


def _fm_chan(x, g, shift, scale, w_cs, tm=512):
    b, s, d = x.shape
    cg = w_cs.shape[0]
    per_b = pl.BlockSpec((1, 1, d), lambda bi, i: (bi, 0, 0))
    return pl.pallas_call(
        functools.partial(_fm_chan_body, cg=cg),
        out_shape=jax.ShapeDtypeStruct((b, 2, s, d), BF16),
        grid=(b, s // tm),
        in_specs=[pl.BlockSpec((1, tm, d), lambda bi, i: (bi, i, 0)),
                  pl.BlockSpec(g.shape, lambda bi, i: (0, 0)), per_b, per_b,
                  pl.BlockSpec(w_cs.shape, lambda bi, i: (0, 0))],
        out_specs=pl.BlockSpec((1, 2, tm, d), lambda bi, i: (bi, 0, i, 0)),
        compiler_params=_cparams(("parallel", "parallel")),
        name="fm_chan",
    )(x, g, shift, scale, w_cs)


def _fm_s1_body(m_ref, pq_ref, o_ref):
    o_ref[0] = _dot(m_ref[...], pq_ref[0]).astype(BF16)


def _fm_stage1(pq, m1, cb=8192):
    b, _, s, d = pq.shape
    cols = s * d // FM_A
    blk = pl.BlockSpec((1, 2 * FM_A, cb), lambda bi, j: (bi, 0, j))
    return pl.pallas_call(
        _fm_s1_body,
        out_shape=jax.ShapeDtypeStruct((b, 2 * FM_A, cols), BF16),
        grid=(b, cols // cb),
        in_specs=[pl.BlockSpec(m1.shape, lambda bi, j: (0, 0)), blk],
        out_specs=blk,
        compiler_params=_cparams(("parallel", "parallel")),
        name="fm_stage1",
    )(m1, pq.reshape(b, 2 * FM_A, cols))


def _fm_s2_body(s_ref, g_ref, o_ref, *, dblk, scale):
    for i in range(dblk):
        s2 = jnp.concatenate([s_ref[0, 0, i], s_ref[0, 1, i]], axis=0)
        o_ref[:, i, :] = _dot(g_ref[i], s2) * scale


def _fm_stage2(s1, gcat, seq, d, dblk=8):
    b = s1.shape[0]
    sv = s1.reshape(b, 2, FM_A, FM_A, d)
    scale = 1.0 / math.sqrt(seq * (d // F_GROUPS))
    out = pl.pallas_call(
        functools.partial(_fm_s2_body, dblk=dblk, scale=scale),
        out_shape=jax.ShapeDtypeStruct((b * FM_A, dblk * (FM_A // dblk), d), F32),
        grid=(b, FM_A // dblk),
        in_specs=[pl.BlockSpec((1, 2, dblk, FM_A, d), lambda bi, j: (bi, 0, j, 0, 0)),
                  pl.BlockSpec((dblk, FM_A, 2 * FM_A), lambda bi, j: (j, 0, 0))],
        out_specs=pl.BlockSpec((FM_A, dblk, d), lambda bi, j: (bi, j, 0)),
        compiler_params=_cparams(("parallel", "parallel")),
        name="fm_stage2",
    )(sv, gcat)
    return out.reshape(b, seq, d)


def _fourier_mix(x, g, shift, scale):
    b, s, d = x.shape
    assert s == FM_A * FM_A
    w_cs, m1, gcat = _fm_constants(d // F_GROUPS)
    pq = _fm_chan(x, g, shift, scale, w_cs)
    return _fm_stage2(_fm_stage1(pq, m1), gcat, s, d)


LANES = 128
ROW_SL = 8
MOE_TM = 1024
DMA_WINDOW = 128


def _router_body(x_ref, yf_ref, wf_ref, gf_ref, g_ref, sh_ref, sc_ref, wr_ref, br_ref,
                 xo_ref, h_ref, meta_ref, gw_ref, cnt_ref, carry):
    i = pl.program_id(0)

    @pl.when(i == 0)
    def _():
        carry[...] = jnp.zeros_like(carry)

    xm = x_ref[...] + gf_ref[0] * _dot(yf_ref[...].astype(BF16), wf_ref[...])
    xo_ref[...] = xm
    h = _norm_mod(xm, g_ref[...], sh_ref[0], sc_ref[0])
    _rows_to_tiles(h_ref, h)
    h_hi = h.astype(BF16)
    h_lo = (h - h_hi.astype(F32)).astype(BF16)
    by_hi = _dot(h_hi, wr_ref[...])
    logits = by_hi[:, 0:LANES] + by_hi[:, LANES:] + _dot(h_lo, wr_ref[:, 0:LANES]) + br_ref[...]
    lane = lax.broadcasted_iota(jnp.int32, logits.shape, 1)
    nl = logits.shape[-1]
    m1 = jnp.max(logits, axis=-1, keepdims=True)
    i1 = jnp.min(jnp.where(logits == m1, lane, nl), axis=-1, keepdims=True)
    rest = jnp.where(lane == i1, -3.0e38, logits)
    m2 = jnp.max(rest, axis=-1, keepdims=True)
    i2 = jnp.min(jnp.where(rest == m2, lane, nl), axis=-1, keepdims=True)
    e = jnp.exp(m2 - m1)
    gw_ref[...] = jnp.where(lane == 0, 1.0 / (1.0 + e), jnp.where(lane == 1, e / (1.0 + e), 0.0))
    onehot = jnp.where((lane == i1) | (lane == i2), 1.0, 0.0)
    tm = onehot.shape[0]
    earlier = lax.broadcasted_iota(jnp.int32, (tm, tm), 0) > lax.broadcasted_iota(jnp.int32, (tm, tm), 1)
    excl = _dot(jnp.where(earlier, 1.0, 0.0).astype(BF16), onehot.astype(BF16)) + carry[...]
    r1 = jnp.sum(jnp.where(lane == i1, excl, 0.0), axis=-1, keepdims=True).astype(jnp.int32)
    r2 = jnp.sum(jnp.where(lane == i2, excl, 0.0), axis=-1, keepdims=True).astype(jnp.int32)
    meta_ref[...] = jnp.where(lane == 0, i1, jnp.where(lane == 1, i2, jnp.where(lane == 2, r1, jnp.where(lane == 3, r2, 0))))
    carry[...] = carry[...] + jnp.sum(onehot, axis=0, keepdims=True)
    cnt_ref[...] = carry[...]


def _router(x, y_f, w_f, gate_f, g, shift, scale, w_router, b_router, tm=512):
    b, s, d = x.shape
    t = b * s
    ne = w_router.shape[1]
    wr = jnp.pad(w_router.astype(F32), ((0, 0), (0, LANES - ne)))
    wr_hi = wr.astype(BF16)
    wr = jnp.concatenate([wr_hi, (wr - wr_hi.astype(F32)).astype(BF16)], axis=1)
    br = jnp.pad(b_router.astype(F32).reshape(1, ne), ((0, 0), (0, LANES - ne)), constant_values=NEG_INF)
    spt = s // tm
    per_b = pl.BlockSpec((1, 1, d), lambda i: (i // spt, 0, 0))
    const = lambda a: pl.BlockSpec(a.shape, lambda i: (0, 0))
    tok = pl.BlockSpec((tm, d), lambda i: (i, 0))
    return pl.pallas_call(
        _router_body,
        out_shape=[jax.ShapeDtypeStruct((t, d), F32),
                   jax.ShapeDtypeStruct((t * ROW_SL, LANES), F32),
                   jax.ShapeDtypeStruct((t, LANES), jnp.int32),
                   jax.ShapeDtypeStruct((t, LANES), F32),
                   jax.ShapeDtypeStruct((1, LANES), F32)],
        grid=(t // tm,),
        in_specs=[tok, tok, const(w_f), per_b, const(g), per_b, per_b, const(wr), const(br)],
        out_specs=[tok,
                   pl.BlockSpec((tm * ROW_SL, LANES), lambda i: (i, 0)),
                   pl.BlockSpec((tm, LANES), lambda i: (i, 0)),
                   pl.BlockSpec((tm, LANES), lambda i: (i, 0)),
                   pl.BlockSpec((1, LANES), lambda i: (0, 0))],
        scratch_shapes=[pltpu.VMEM((1, LANES), F32)],
        compiler_params=_cparams(("arbitrary",)),
        name="router",
    )(x.reshape(t, d), y_f.reshape(t, d), w_f, gate_f, g, shift, scale, wr, br)


def _moe_plan(meta, counts, ne, tm):
    i1, i2, r1, r2 = meta[:, 0], meta[:, 1], meta[:, 2], meta[:, 3]
    cnt = counts[0, :ne].astype(jnp.int32)
    padded = ((cnt + tm - 1) // tm) * tm
    ends = jnp.cumsum(padded)
    offs = ends - padded
    pick = lambda idx: sum(jnp.where(idx == e, offs[e], 0) for e in range(ne))
    pos = jnp.concatenate([pick(i1) + r1, pick(i2) + r2]).astype(jnp.int32)
    n_tiles = (2 * meta.shape[0]) // tm + ne
    n_used = (ends[ne - 1] // tm).astype(jnp.int32)
    tile_start = jnp.minimum(jnp.arange(n_tiles, dtype=jnp.int32), n_used - 1) * tm
    tile_expert = jnp.sum(tile_start[:, None] >= ends[None, :], axis=1).astype(jnp.int32)
    return pos, offs + cnt, padded - cnt, tile_expert, n_used.reshape(1)


def _windowed_copies(n, start_copy, wait_one, per_iter):
    def body(i, carry):
        @pl.when(i >= DMA_WINDOW)
        def _():
            for _ in range(per_iter):
                wait_one()
        start_copy(i)
        return carry

    lax.fori_loop(0, n, body, 0)

    def drain(i, carry):
        for _ in range(per_iter):
            wait_one()
        return carry

    lax.fori_loop(0, jnp.minimum(n, DMA_WINDOW), drain, 0)


def _tile_of(ref, row):
    return ref.at[pl.ds(pl.multiple_of(row * ROW_SL, ROW_SL), ROW_SL)]


def _tiles_to_rows(ref, n, first=0):
    return jnp.concatenate([ref[pl.ds(first * ROW_SL + sl, n, stride=ROW_SL), :] for sl in range(ROW_SL)], axis=1)


def _rows_to_tiles(ref, val):
    n = val.shape[0]
    for sl in range(ROW_SL):
        ref[pl.ds(sl, n, stride=ROW_SL), :] = val[:, sl * LANES:(sl + 1) * LANES]


def _dispatch_body(pos_ref, pad_start_ref, pad_n_ref, h_ref, xs_hbm, sem, *, n_tok, ne):
    i = pl.program_id(0)
    td = h_ref.shape[0] // ROW_SL
    base = i * td
    copy = lambda src, dst: pltpu.make_async_copy(_tile_of(h_ref, src), _tile_of(xs_hbm, dst), sem)
    wait_one = lambda: copy(0, 0).wait()

    def start_token(r, carry):
        copy(r, pos_ref[base + r]).start()
        copy(r, pos_ref[n_tok + base + r]).start()
        return carry

    lax.fori_loop(0, td, start_token, 0, unroll=8)
    whole_tile = pltpu.make_async_copy(h_ref, xs_hbm.at[pl.ds(0, td * ROW_SL)], sem)
    whole_tile.wait()
    whole_tile.wait()

    @pl.when(i == 0)
    def _():
        for e in range(ne):
            first = pad_start_ref[e]
            _windowed_copies(pad_n_ref[e], lambda r: copy(0, first + r).start(), wait_one, 1)


def _moe_dispatch(h3, pos, pad_start, pad_n, n_rows, td=1024):
    n_tok = h3.shape[0] // ROW_SL
    ne = pad_start.shape[0]
    return pl.pallas_call(
        functools.partial(_dispatch_body, n_tok=n_tok, ne=ne),
        out_shape=jax.ShapeDtypeStruct((n_rows * ROW_SL, LANES), h3.dtype),
        grid_spec=pltpu.PrefetchScalarGridSpec(
            num_scalar_prefetch=3, grid=(n_tok // td,),
            in_specs=[pl.BlockSpec((td * ROW_SL, LANES), lambda i, p, ps, pn: (i, 0))],
            out_specs=pl.BlockSpec(memory_space=pl.ANY),
            scratch_shapes=[pltpu.SemaphoreType.DMA(())]),
        compiler_params=_cparams(("arbitrary",)),
        name="moe_dispatch",
    )(pos, pad_start, pad_n, h3)


def _moe_grouped_body(te_ref, nu_ref, xs_ref, w1_ref, w3_ref, w2_ref, y_ref, xb_scr, acc_scr):
    i = pl.program_id(0)
    j = pl.program_id(1)
    tm = xb_scr.shape[0]

    @pl.when(i < nu_ref[0])
    def _():
        @pl.when(j == 0)
        def _():
            xb_scr[...] = _tiles_to_rows(xs_ref, tm).astype(BF16)

        h = xb_scr[...]
        a = _dot(h, w1_ref[0].astype(BF16))
        u = (a * jax.nn.sigmoid(a) * _dot(h, w3_ref[0].astype(BF16))).astype(BF16)
        part = _dot(u, w2_ref[0].astype(BF16))

        @pl.when(j == 0)
        def _():
            acc_scr[...] = part

        @pl.when(j > 0)
        def _():
            acc_scr[...] += part

        @pl.when(j == pl.num_programs(1) - 1)
        def _():
            _rows_to_tiles(y_ref, acc_scr[...])


def _moe_grouped(xs, tile_expert, n_used, w1, w3, w2, tm, fb=512):
    ne, d, f = w1.shape
    n_rows = xs.shape[0] // ROW_SL
    nj = f // fb
    row_tile = lambda i, j, te, nu: (jnp.maximum(jnp.minimum(i, nu[0] - 1), 0), 0)
    jj = lambda i, j, nu: jnp.where(i < nu[0], j, nj - 1)
    return pl.pallas_call(
        _moe_grouped_body,
        out_shape=jax.ShapeDtypeStruct(xs.shape, F32),
        grid_spec=pltpu.PrefetchScalarGridSpec(
            num_scalar_prefetch=2, grid=(n_rows // tm, nj),
            in_specs=[pl.BlockSpec((tm * ROW_SL, LANES), row_tile),
                      pl.BlockSpec((1, d, fb), lambda i, j, te, nu: (te[i], 0, jj(i, j, nu))),
                      pl.BlockSpec((1, d, fb), lambda i, j, te, nu: (te[i], 0, jj(i, j, nu))),
                      pl.BlockSpec((1, fb, d), lambda i, j, te, nu: (te[i], jj(i, j, nu), 0))],
            out_specs=pl.BlockSpec((tm * ROW_SL, LANES), row_tile),
            scratch_shapes=[pltpu.VMEM((tm, d), BF16), pltpu.VMEM((tm, d), F32)]),
        compiler_params=_cparams(("arbitrary", "arbitrary")),
        name="moe_grouped",
    )(tile_expert, n_used, xs, w1, w3, w2)


def _moe_final_body(pos_ref, x_ref, y_hbm, gw_ref, gt_ref, fg_ref, o_ref, yg_scr, sem, *, n_tok):
    i = pl.program_id(0)
    tc = x_ref.shape[0]
    slot = i % 2

    def gather_tile(step, into):
        base = step * tc

        def start_token(r, carry):
            dst = yg_scr.at[into]
            pltpu.make_async_copy(_tile_of(y_hbm, pos_ref[base + r]), _tile_of(dst, r), sem.at[into]).start()
            pltpu.make_async_copy(_tile_of(y_hbm, pos_ref[n_tok + base + r]), _tile_of(dst, tc + r),
                                  sem.at[into]).start()
            return carry

        lax.fori_loop(0, tc, start_token, 0, unroll=8)

    @pl.when(i == 0)
    def _():
        gather_tile(0, 0)

    @pl.when(i + 1 < pl.num_programs(0))
    def _():
        gather_tile(i + 1, 1 - slot)

    pltpu.make_async_copy(y_hbm.at[pl.ds(0, 2 * tc * ROW_SL)], yg_scr.at[slot], sem.at[slot]).wait()
    gw = gw_ref[...]
    rows = yg_scr.at[slot]
    y = gw[:, 0:1] * _tiles_to_rows(rows, tc) + gw[:, 1:2] * _tiles_to_rows(rows, tc, first=tc)
    xo = x_ref[...] + gt_ref[0] * y
    ms = jnp.mean(xo * xo, axis=-1, keepdims=True)
    o_ref[...] = xo * lax.rsqrt(ms + EPS) * fg_ref[...]


def _moe_final(x, y, pos, gw, gt, final_g, tc=256):
    b, s, d = x.shape
    t = b * s
    spt = s // tc
    nsl = d // LANES
    out = pl.pallas_call(
        functools.partial(_moe_final_body, n_tok=t),
        out_shape=jax.ShapeDtypeStruct((t, d), F32),
        grid_spec=pltpu.PrefetchScalarGridSpec(
            num_scalar_prefetch=1, grid=(t // tc,),
            in_specs=[pl.BlockSpec((tc, d), lambda i, p: (i, 0)),
                      pl.BlockSpec(memory_space=pl.ANY),
                      pl.BlockSpec((tc, LANES), lambda i, p: (i, 0)),
                      pl.BlockSpec((1, 1, d), lambda i, p: (i // spt, 0, 0)),
                      pl.BlockSpec(final_g.shape, lambda i, p: (0, 0))],
            out_specs=pl.BlockSpec((tc, d), lambda i, p: (i, 0)),
            scratch_shapes=[pltpu.VMEM((2, 2 * tc * ROW_SL, LANES), F32), pltpu.SemaphoreType.DMA((2,))]),
        compiler_params=_cparams(("arbitrary",)),
        name="moe_final",
    )(pos, x.reshape(t, d), y, gw, gt, final_g)
    return out.reshape(b, s, d)


def _moe_routed(x, y_f, w_f, gate_f, g, shift, scale, gt, final_g, w_router, b_router, w1, w3, w2):
    ne = w1.shape[0]
    tm = MOE_TM
    x1, h3, meta, gw, counts = _router(x, y_f, w_f, gate_f, g, shift, scale, w_router, b_router)
    pos, pad_start, pad_n, tile_expert, n_used = _moe_plan(meta, counts, ne, tm)
    assert x.shape[-1] == ROW_SL * LANES
    n_rows = (2 * (h3.shape[0] // ROW_SL) // tm + ne) * tm
    xs = _moe_dispatch(h3, pos, pad_start, pad_n, n_rows)
    y = _moe_grouped(xs, tile_expert, n_used, w1, w3, w2, tm)
    return _moe_final(x1.reshape(x.shape), y, pos, gw, gt, final_g)


def kernel(x, c, ctx, c_ctx, w_ada, b_ada, norm_g, w_in, hy_short_w, hy_short_b, hy_f_w1, hy_f_b1, hy_f_w2, hy_f_b2, hy_f_w3, hy_f_freq, hy_skip, na_rpb, w_mix_out, ffn_w1, ffn_w3, ffn_w2, w_fourier, w_router, b_router, moe_w1, moe_w3, moe_w2, final_g):
    b, s, d = x.shape
    depth = w_ada.shape[0]
    assert depth == 2, "layer 0 mixes with Hyena/attention, layer 1 with Fourier/MoE"
    c_hy = hy_skip.shape[-1]
    c_na = d - c_hy

    cvec = jnp.concatenate([c, c_ctx[None, :], jnp.zeros((8 - b - 1, d), F32)], axis=0)
    mods = _ada(cvec, w_ada, b_ada)

    def mod(layer, idx, ctx_row=False):
        m = mods[layer, :, idx * d:(idx + 1) * d]
        return m[b:b + 1, None, :] if ctx_row else m[0:b, None, :]

    row = lambda a: a.reshape(1, -1)

    w_in0 = w_in[0].astype(BF16)
    w_hy, w_qkv = w_in0[:, 0:3 * c_hy], w_in0[:, 3 * c_hy:]
    v, x1, x2, q, k, va = _inproj(x, row(norm_g[0, 0]), mod(0, 0), mod(0, 1), w_hy, w_qkv,
                                  hy_short_w[0], row(hy_short_b[0]))
    kc, vc = _ctxkv(ctx, row(norm_g[0, 0]), mod(0, 0, True), mod(0, 1, True), w_qkv[:, c_na:])
    y_na = _natt(q, k, va, kc, vc, _na_bias_table(na_rpb[0]))
    y_hy = _hyena(v, x1, x2, hy_f_w1[0], hy_f_b1[0], hy_f_w2[0], hy_f_b2[0], hy_f_w3[0],
                  hy_f_freq[0], hy_skip[0])
    x = _mix_ffn(x, y_hy, y_na, w_mix_out[0].astype(BF16), mod(0, 2),
                 row(norm_g[0, 1]), mod(0, 3), mod(0, 4), mod(0, 5),
                 ffn_w1[0].astype(BF16), ffn_w3[0].astype(BF16), ffn_w2[0].astype(BF16))

    y_f = _fourier_mix(x, row(norm_g[1, 0]), mod(1, 0), mod(1, 1))
    return _moe_routed(x, y_f, w_fourier[0].astype(BF16), mod(1, 2),
                       row(norm_g[1, 1]), mod(1, 3), mod(1, 4), mod(1, 5), row(final_g),
                       w_router[0], b_router[0],
                       moe_w1[0], moe_w3[0], moe_w2[0])
```

```python
import functools
import math

import numpy as np
import jax
import jax.numpy as jnp
from jax import lax
from jax.experimental import pallas as pl
from jax.experimental.pallas import tpu as pltpu

F32 = jnp.float32
BF16 = jnp.bfloat16
HIGHEST = lax.Precision.HIGHEST

GRID_W = 64
NA_HEAD_DIM = 32
NA_WIN_R = 8
NA_WIN_C = 16
HYENA_EMB = 33
HYENA_BANDS = (HYENA_EMB - 1) // 2
HYENA_FAST_DECAY = 0.3
HYENA_SLOW_DECAY = 1.5
HYENA_TARGET = 1e-2
F_GROUPS = 4
N_MOD = 6
EPS = 1e-6
NEG_INF = -1e30

FFT_A = 64
FFT_R = 128
FFT_KA = FFT_A // 2 + 1
FFT_KA_PAD = 40
FM_A = 64

VMEM_LIMIT = 48 * 1024 * 1024


def _cparams(sem, vmem_limit=VMEM_LIMIT):
    return pltpu.CompilerParams(dimension_semantics=sem, vmem_limit_bytes=vmem_limit)


def _dot(a, b):
    return jnp.dot(a, b, preferred_element_type=F32)


def _mxu_const(m):
    return jnp.asarray(m, dtype=F32).astype(BF16)


def _norm_mod(x, g, shift, scale):
    ms = jnp.mean(x * x, axis=-1, keepdims=True)
    y = x * lax.rsqrt(ms + EPS) * g
    return y * (1.0 + scale) + shift


def _ada_body(c_ref, w_ref, b_ref, o_ref):
    cv = c_ref[...]
    s = cv * jax.nn.sigmoid(cv)
    o_ref[0] = jnp.dot(s, w_ref[0], precision=HIGHEST, preferred_element_type=F32) + b_ref[0]


def _ada(cvec, w_ada, b_ada):
    depth, d, n = w_ada.shape
    rows = cvec.shape[0]
    bn = n // 4
    return pl.pallas_call(
        _ada_body,
        out_shape=jax.ShapeDtypeStruct((depth, rows, n), F32),
        grid=(depth, n // bn),
        in_specs=[pl.BlockSpec((rows, d), lambda l, j: (0, 0)),
                  pl.BlockSpec((1, d, bn), lambda l, j: (l, 0, j)),
                  pl.BlockSpec((1, 1, bn), lambda l, j: (l, 0, j))],
        out_specs=pl.BlockSpec((1, rows, bn), lambda l, j: (l, 0, j)),
        compiler_params=_cparams(("parallel", "parallel")),
        name="ada",
    )(cvec, w_ada, b_ada.reshape(depth, 1, n))


def _inproj_body(x_ref, xp_ref, xn_ref, g_ref, sh_ref, sc_ref, why_ref, wqkv_ref, sw_ref, sb_ref,
                 v_ref, x1_ref, x2_ref, q_ref, k_ref, va_ref, *, n_tiles, q_scale, c_hy, c_na):
    i = pl.program_id(1)
    g, sh, sc = g_ref[...], sh_ref[0], sc_ref[0]
    hf = _norm_mod(x_ref[0], g, sh, sc)
    h = hf.astype(BF16)
    tm = hf.shape[0]
    hx = jnp.concatenate([_norm_mod(xp_ref[0], g, sh, sc), hf, _norm_mod(xn_ref[0], g, sh, sc)], axis=0)
    zx = _dot(hx.astype(BF16), why_ref[...])
    zh = zx[8:8 + tm]
    zp = jnp.where(i > 0, zx[7:8], 0.0)
    zn = jnp.where(i < n_tiles - 1, zx[8 + tm:9 + tm], 0.0)
    row = lax.broadcasted_iota(jnp.int32, zh.shape, 0)
    z_m1 = jnp.where(row == 0, zp, pltpu.roll(zh, 1, 0))
    z_p1 = jnp.where(row == tm - 1, zn, pltpu.roll(zh, tm - 1, 0))
    sw = sw_ref[...]
    zc = z_m1 * sw[0:1] + zh * sw[1:2] + z_p1 * sw[2:3] + sb_ref[...]
    v_ref[0] = zc[:, 0:c_hy]
    x1_ref[0] = zc[:, c_hy:2 * c_hy]
    x2_ref[0] = zc[:, 2 * c_hy:3 * c_hy]
    zq = _dot(h, wqkv_ref[...])
    q_ref[0] = (zq[:, 0:c_na] * q_scale).astype(BF16)
    k_ref[0] = zq[:, c_na:2 * c_na].astype(BF16)
    va_ref[0] = zq[:, 2 * c_na:3 * c_na].astype(BF16)


def _inproj(x, g, shift, scale, w_hy, w_qkv, short_w, short_b, tm=512):
    b, s, d = x.shape
    c_hy = w_hy.shape[1] // 3
    c_na = w_qkv.shape[1] // 3
    n_tiles = s // tm
    r8 = tm // 8
    body = functools.partial(_inproj_body, n_tiles=n_tiles, q_scale=NA_HEAD_DIM ** -0.5,
                             c_hy=c_hy, c_na=c_na)
    tok = lambda c: pl.BlockSpec((1, tm, c), lambda bi, i: (bi, i, 0))
    full2 = lambda a: pl.BlockSpec(a.shape, lambda bi, i: (0, 0))
    per_b = pl.BlockSpec((1, 1, d), lambda bi, i: (bi, 0, 0))
    return pl.pallas_call(
        body,
        out_shape=[jax.ShapeDtypeStruct((b, s, c_hy), F32)] * 3 + [jax.ShapeDtypeStruct((b, s, c_na), BF16)] * 3,
        grid=(b, n_tiles),
        in_specs=[tok(d),
                  pl.BlockSpec((1, 8, d), lambda bi, i: (bi, jnp.maximum(i * r8 - 1, 0), 0)),
                  pl.BlockSpec((1, 8, d), lambda bi, i: (bi, jnp.minimum((i + 1) * r8, s // 8 - 1), 0)),
                  full2(g), per_b, per_b, full2(w_hy), full2(w_qkv), full2(short_w), full2(short_b)],
        out_specs=[tok(c_hy)] * 3 + [tok(c_na)] * 3,
        compiler_params=_cparams(("parallel", "parallel")),
        name="inproj",
    )(x, x, x, g, shift, scale, w_hy, w_qkv, short_w, short_b)


def _ctxkv_body(x_ref, g_ref, sh_ref, sc_ref, w_ref, k_ref, v_ref, *, c_na):
    h = _norm_mod(x_ref[0], g_ref[...], sh_ref[0], sc_ref[0]).astype(BF16)
    z = _dot(h, w_ref[...])
    k_ref[0] = z[:, 0:c_na].astype(BF16)
    v_ref[0] = z[:, c_na:2 * c_na].astype(BF16)


def _ctxkv(ctx, g, shift, scale, w_kv):
    b, n, d = ctx.shape
    c_na = w_kv.shape[1] // 2
    one = pl.BlockSpec((1, 1, d), lambda bi: (0, 0, 0))
    return pl.pallas_call(
        functools.partial(_ctxkv_body, c_na=c_na),
        out_shape=[jax.ShapeDtypeStruct((b, n, c_na), BF16)] * 2,
        grid=(b,),
        in_specs=[pl.BlockSpec((1, n, d), lambda bi: (bi, 0, 0)),
                  pl.BlockSpec(g.shape, lambda bi: (0, 0)), one, one,
                  pl.BlockSpec(w_kv.shape, lambda bi: (0, 0))],
        out_specs=[pl.BlockSpec((1, n, c_na), lambda bi: (bi, 0, 0))] * 2,
        compiler_params=_cparams(("parallel",)),
        name="ctxkv",
    )(ctx, g, shift, scale, w_kv)


NA_HEADS_PER_BLK = 4
NA_ROWS_PER_STEP = 4


def _na_bias_body(r_ref, e_ref, ok_ref, o_ref):
    t = jnp.dot(r_ref[...], e_ref[...], precision=HIGHEST, preferred_element_type=F32)
    o_ref[...] = jnp.where(ok_ref[...] > 0.5, t, NEG_INF)


def _na_bias_table(rpb):
    w = GRID_W
    h, nr, nc = rpb.shape
    col = np.arange(w)[:, None]
    kc = np.arange(w)[None, :]
    c_start = np.clip(col - NA_WIN_C // 2, 0, w - NA_WIN_C)
    valid = ((kc >= c_start) & (kc < c_start + NA_WIN_C)).reshape(1, w * w)
    expand = (np.arange(32)[:, None, None] == (kc - col + NA_WIN_C - 1)[None]).reshape(32, w * w)
    rp = jnp.pad(rpb.astype(F32).reshape(h * nr, nc), ((0, 0), (0, 32 - nc)))
    full = lambda a: pl.BlockSpec(a.shape, lambda: (0,) * a.ndim)
    expand = jnp.asarray(expand, dtype=F32)
    ok = jnp.asarray(valid, dtype=F32)
    toep = pl.pallas_call(
        _na_bias_body,
        out_shape=jax.ShapeDtypeStruct((h * nr, w * w), F32),
        in_specs=[full(rp), full(expand), full(ok)],
        out_specs=pl.BlockSpec((h * nr, w * w), lambda: (0, 0)),
        name="na_bias",
    )(rp, expand, ok)
    t2 = toep.reshape(h, nr, w, w).transpose(0, 2, 1, 3).reshape(h, w, nr * w)
    slabs = jnp.stack([t2[:, :, (NA_WIN_R - 1 - off) * w:(2 * NA_WIN_R - 1 - off) * w]
                       for off in range(NA_WIN_R)], axis=1)
    hpb = NA_HEADS_PER_BLK
    slabs = slabs.reshape(h // hpb, hpb, NA_WIN_R, w, NA_WIN_R * w).transpose(0, 2, 1, 3, 4)
    return slabs.reshape(h // hpb, NA_WIN_R, hpb * w, NA_WIN_R * w)


def _natt_body(q_ref, k_ref, v_ref, kc_ref, vc_ref, bias_ref, o_ref, *, rows):
    w = GRID_W
    hpb = NA_HEADS_PER_BLK
    nloc = NA_WIN_R * w
    lane = lax.broadcasted_iota(jnp.int32, (1, hpb * NA_HEAD_DIM), 1)
    in_head = [(lane >= NA_HEAD_DIM * hh) & (lane < NA_HEAD_DIM * (hh + 1)) for hh in range(hpb)]
    kcx = kc_ref[0]
    vcx = vc_ref[0]
    nt = (((1,), (1,)), ((), ()))

    def one_row(r):
        r0 = jnp.clip(r - NA_WIN_R // 2, 0, rows - NA_WIN_R)
        off = r - r0
        qs = q_ref[0, pl.ds(pl.multiple_of(r * w, w), w), :]
        kw = k_ref[0, pl.ds(pl.multiple_of(r0 * w, w), nloc), :]
        vw = v_ref[0, pl.ds(pl.multiple_of(r0 * w, w), nloc), :]
        zero = jnp.zeros_like(qs)
        qst = jnp.concatenate([jnp.where(m, qs, zero) for m in in_head], axis=0)
        s_loc = lax.dot_general(qst, kw, nt, preferred_element_type=F32) + bias_ref[0, off]
        s_ctx = lax.dot_general(qst, kcx, nt, preferred_element_type=F32)
        m = jnp.maximum(jnp.max(s_loc, axis=-1, keepdims=True), jnp.max(s_ctx, axis=-1, keepdims=True))
        p_loc = jnp.exp(s_loc - m)
        p_ctx = jnp.exp(s_ctx - m)
        den = jnp.sum(p_loc, axis=-1, keepdims=True) + jnp.sum(p_ctx, axis=-1, keepdims=True)
        o = (_dot(p_loc.astype(BF16), vw) + _dot(p_ctx.astype(BF16), vcx)) * (1.0 / den)
        acc = jnp.where(in_head[0], o[0:w], 0.0)
        for hh in range(1, hpb):
            acc = acc + jnp.where(in_head[hh], o[hh * w:(hh + 1) * w], 0.0)
        o_ref[0, pl.ds(pl.multiple_of(r * w, w), w), :] = acc.astype(BF16)

    def row_group(i, carry):
        for r in range(NA_ROWS_PER_STEP):
            one_row(NA_ROWS_PER_STEP * i + r)
        return carry

    lax.fori_loop(0, rows // NA_ROWS_PER_STEP, row_group, 0)


def _natt(q, k, v, kc, vc, bias):
    b, s, c = q.shape
    nctx = kc.shape[1]
    lw = NA_HEADS_PER_BLK * NA_HEAD_DIM
    rows = s // GRID_W
    seq = pl.BlockSpec((1, s, lw), lambda bi, g: (bi, 0, g))
    cx = pl.BlockSpec((1, nctx, lw), lambda bi, g: (bi, 0, g))
    return pl.pallas_call(
        functools.partial(_natt_body, rows=rows),
        out_shape=jax.ShapeDtypeStruct((b, s, c), BF16),
        grid=(b, c // lw),
        in_specs=[seq, seq, seq, cx, cx,
                  pl.BlockSpec((1,) + bias.shape[1:], lambda bi, g: (g, 0, 0, 0))],
        out_specs=seq,
        compiler_params=_cparams(("parallel", "parallel")),
        name="natt",
    )(q, k, v, kc, vc, bias)


def _hyena_feats(seq_len):
    t = jnp.linspace(0.0, 1.0, seq_len, dtype=F32)[:, None]
    bands = jnp.linspace(1e-4, HYENA_BANDS - 1, HYENA_BANDS, dtype=F32)
    ang = (2.0 * math.pi / seq_len) * jnp.arange(seq_len, dtype=F32)[:, None] * bands[None, :]
    feats = jnp.concatenate([t, jnp.cos(ang), -jnp.sin(ang)], axis=-1)
    return jnp.pad(feats, ((0, 0), (0, 128 - HYENA_EMB)))


def _filt_body(feat_ref, w1_ref, b1_ref, w2_ref, b2_ref, w3_ref, fr_ref, dl_ref, o_ref, l1_ref, h_scr,
               *, halves):
    j = pl.program_id(0)
    hp = functools.partial(jnp.dot, precision=HIGHEST, preferred_element_type=F32)
    feats = feat_ref[...]

    @pl.when(j == 0)
    def _():
        fr = fr_ref[...]
        h = jnp.sin(fr[0:1] * (hp(feats, w1_ref[...]) + b1_ref[...]))
        h_scr[...] = jnp.sin(fr[1:2] * (hp(h, w2_ref[...]) + b2_ref[...]))

    hc = hp(h_scr[...], w3_ref[...])
    t = feats[:, 0:1]
    hc = hc * jnp.exp(-t * dl_ref[...])
    row = lax.broadcasted_iota(jnp.int32, hc.shape, 0)
    hc = jnp.where((row == 0) & ((j // halves) % 2 == 1), 0.0, hc)
    l1_ref[0] = jnp.sum(jnp.abs(hc), axis=0, keepdims=True)
    o_ref[0] = hc


def _hyena_filter_taps(seq_len, f_w1, f_b1, f_w2, f_b2, f_w3, f_freq, c_hy):
    feats = _hyena_feats(seq_len)
    hid = f_w1.shape[1]
    w1 = jnp.pad(f_w1.astype(F32), ((0, 128 - HYENA_EMB), (0, 0)))
    deltas = jnp.abs(jnp.linspace(math.log(HYENA_TARGET) / HYENA_SLOW_DECAY,
                                  math.log(HYENA_TARGET) / HYENA_FAST_DECAY, c_hy, dtype=F32))[None, :]
    nblk = f_w3.shape[1] // c_hy
    halves = 2
    cb = c_hy // halves
    c0 = lambda a: pl.BlockSpec(a.shape, lambda j: (0, 0))
    b1, b2 = f_b1.reshape(1, hid), f_b2.reshape(1, hid)
    return pl.pallas_call(
        functools.partial(_filt_body, halves=halves),
        out_shape=[jax.ShapeDtypeStruct((nblk, seq_len, c_hy), F32),
                   jax.ShapeDtypeStruct((nblk, 1, c_hy), F32)],
        grid=(nblk * halves,),
        in_specs=[c0(feats), c0(w1), c0(b1), c0(f_w2), c0(b2),
                  pl.BlockSpec((hid, cb), lambda j: (0, j)), c0(f_freq),
                  pl.BlockSpec((1, cb), lambda j: (0, j % halves))],
        out_specs=[pl.BlockSpec((1, seq_len, cb), lambda j: (j // halves, 0, j % halves)),
                   pl.BlockSpec((1, 1, cb), lambda j: (j // halves, 0, j % halves))],
        scratch_shapes=[pltpu.VMEM((seq_len, hid), F32)],
        compiler_params=_cparams(("arbitrary",)),
        name="hyena_filter",
    )(feats, w1, b1, f_w2, b2, f_w3, f_freq, deltas)


def _conv_dft_constants():
    a_half = FFT_A // 2
    n = FFT_A * FFT_R
    ka = np.arange(FFT_KA)[:, None]
    a = np.arange(a_half)[None, :]
    ph = 2.0 * np.pi * ka * a / FFT_A
    m_fwd = np.zeros((2 * FFT_KA_PAD, a_half))
    m_fwd[:FFT_KA] = np.cos(ph)
    m_fwd[FFT_KA_PAD:FFT_KA_PAD + FFT_KA] = -np.sin(ph)
    wgt = np.where((ka == 0) | (ka == FFT_A // 2), 1.0, 2.0)
    m_inv = np.zeros((a_half, 2 * FFT_KA_PAD))
    m_inv[:, :FFT_KA] = (wgt * np.cos(ph)).T / n
    m_inv[:, FFT_KA_PAD:FFT_KA_PAD + FFT_KA] = (-wgt * np.sin(ph)).T / n
    kb = np.arange(FFT_R)[None, :, None]
    b = np.arange(FFT_R)[None, None, :]
    kaa = np.arange(FFT_KA)[:, None, None]
    th = 2.0 * np.pi * (b * kb / FFT_R + b * kaa / n)
    gr, gi = np.cos(th), -np.sin(th)
    g2 = np.zeros((FFT_KA_PAD, 2 * FFT_R, 2 * FFT_R))
    g2[:FFT_KA] = np.block([[gr, -gi], [gi, gr]])
    grt, git = gr.transpose(0, 2, 1), gi.transpose(0, 2, 1)
    g2h = np.zeros_like(g2)
    g2h[:FFT_KA] = np.block([[grt, git], [-git, grt]])
    return _mxu_const(m_fwd), _mxu_const(m_inv), _mxu_const(g2), _mxu_const(g2h)


FFT_NB = 16


def _fwd1_body(m_ref, u_ref, o_ref):
    c = u_ref.shape[-1]
    for bb in range(FFT_NB):
        res = _dot(m_ref[...], u_ref[0, :, bb, :].astype(BF16))
        o_ref[0, 0, :, bb * c:(bb + 1) * c] = res[0:FFT_KA_PAD]
        o_ref[0, 1, :, bb * c:(bb + 1) * c] = res[FFT_KA_PAD:2 * FFT_KA_PAD]


def _conv_fwd1(u, m_fwd):
    n, seq, c = u.shape
    a_half = FFT_A // 2
    return pl.pallas_call(
        _fwd1_body,
        out_shape=jax.ShapeDtypeStruct((n, 2, FFT_KA_PAD, FFT_R * c), F32),
        grid=(n, FFT_R // FFT_NB),
        in_specs=[pl.BlockSpec(m_fwd.shape, lambda i, j: (0, 0)),
                  pl.BlockSpec((1, a_half, FFT_NB, c), lambda i, j: (i, 0, j, 0))],
        out_specs=pl.BlockSpec((1, 2, FFT_KA_PAD, FFT_NB * c), lambda i, j: (i, 0, 0, j)),
        compiler_params=_cparams(("parallel", "parallel")),
        name="conv_fwd1",
    )(m_fwd, u.reshape(n, a_half, FFT_R, c))


FFT_KB = 8


def _rows_to_slabs(src_ref, dst_scr, c):
    for part in range(2):
        for b in range(FFT_R):
            dst_scr[part, :, b, :] = src_ref[0, part, :, b * c:(b + 1) * c]


def _slabs_to_rows(src_scr, dst_ref, c):
    for part in range(2):
        for b in range(FFT_R):
            dst_ref[0, part, :, b * c:(b + 1) * c] = src_scr[part, :, b, :]


def _slab(scr, i):
    return jnp.concatenate([scr[0, i], scr[1, i]], axis=0).astype(BF16)


def _fwd2f_body(sf_ref, sb_ref, g_ref, l1_ref, kf_ref, f3, b3):
    o = pl.program_id(0)
    j = pl.program_id(1)
    r2 = 2 * FFT_R
    c = kf_ref.shape[-1]
    _rows_to_slabs(sf_ref, f3, c)
    _rows_to_slabs(sb_ref, b3, c)
    inv = 1.0 / (l1_ref[2 * o] + l1_ref[2 * o + 1] + EPS)
    for i in range(FFT_KB):
        @pl.when(j * FFT_KB + i < FFT_KA)
        def _():
            xf = _dot(g_ref[i], _slab(f3, i))
            xb = _dot(g_ref[i], _slab(b3, i))
            kf_ref[0, i, 0:FFT_R] = (xf[0:FFT_R] + xb[0:FFT_R]) * inv
            kf_ref[0, i, FFT_R:r2] = (xf[FFT_R:r2] - xb[FFT_R:r2]) * inv

        @pl.when(j * FFT_KB + i >= FFT_KA)
        def _():
            kf_ref[0, i] = jnp.zeros((r2, c), F32)


def _filter_spectrum(s_filt, l1, g2, c):
    n_ord = s_filt.shape[0] // 2
    cols = s_filt.shape[-1]
    r2 = 2 * FFT_R
    return pl.pallas_call(
        _fwd2f_body,
        out_shape=jax.ShapeDtypeStruct((n_ord, FFT_KA_PAD, r2, c), F32),
        grid=(n_ord, FFT_KA_PAD // FFT_KB),
        in_specs=[pl.BlockSpec((1, 2, FFT_KB, cols), lambda o, j: (2 * o, 0, j, 0)),
                  pl.BlockSpec((1, 2, FFT_KB, cols), lambda o, j: (2 * o + 1, 0, j, 0)),
                  pl.BlockSpec((FFT_KB, r2, r2), lambda o, j: (j, 0, 0)),
                  pl.BlockSpec(l1.shape, lambda o, j: (0, 0, 0))],
        out_specs=pl.BlockSpec((1, FFT_KB, r2, c), lambda o, j: (o, j, 0, 0)),
        scratch_shapes=[pltpu.VMEM((2, FFT_KB, FFT_R, c), F32)] * 2,
        compiler_params=_cparams(("parallel", "parallel")),
        name="filter_spectrum",
    )(s_filt, s_filt, g2, l1)


def _mid_body(s_ref, g_ref, gh_ref, kf_ref, t_ref, s3, t3):
    j = pl.program_id(1)
    r2 = 2 * FFT_R
    c = kf_ref.shape[-1]
    _rows_to_slabs(s_ref, s3, c)
    for i in range(FFT_KB):
        @pl.when(j * FFT_KB + i < FFT_KA)
        def _():
            x = _dot(g_ref[i], _slab(s3, i))
            xr, xi = x[0:FFT_R], x[FFT_R:r2]
            kr, ki = kf_ref[0, i, 0:FFT_R], kf_ref[0, i, FFT_R:r2]
            y = jnp.concatenate([xr * kr - xi * ki, xr * ki + xi * kr], axis=0).astype(BF16)
            t = _dot(gh_ref[i], y)
            t3[0, i] = t[0:FFT_R]
            t3[1, i] = t[FFT_R:r2]

        @pl.when(j * FFT_KB + i >= FFT_KA)
        def _():
            t3[0, i] = jnp.zeros((FFT_R, c), F32)
            t3[1, i] = jnp.zeros((FFT_R, c), F32)

    _slabs_to_rows(t3, t_ref, c)


def _conv_mid(s, kf, order, g2, g2h, c):
    n, _, _, cols = s.shape
    r2 = 2 * FFT_R
    blk = pl.BlockSpec((1, 2, FFT_KB, cols), lambda i, j: (i, 0, j, 0))
    gspec = pl.BlockSpec((FFT_KB, r2, r2), lambda i, j: (j, 0, 0))
    return pl.pallas_call(
        _mid_body,
        out_shape=jax.ShapeDtypeStruct(s.shape, F32),
        grid=(n, FFT_KA_PAD // FFT_KB),
        in_specs=[blk, gspec, gspec,
                  pl.BlockSpec((1, FFT_KB, r2, c), lambda i, j: (order, j, 0, 0))],
        out_specs=blk,
        scratch_shapes=[pltpu.VMEM((2, FFT_KB, FFT_R, c), F32)] * 2,
        compiler_params=_cparams(("parallel", "parallel")),
        name="conv_mid",
    )(s, g2, g2h, kf)


def _inv1_body(m_ref, t_ref, u_ref, xg_ref, sk_ref, o_ref):
    c = u_ref.shape[-1]
    t2 = t_ref[0].reshape(2 * FFT_KA_PAD, FFT_NB * c).astype(BF16)
    y = _dot(m_ref[...], t2)
    for bb in range(FFT_NB):
        conv = y[:, bb * c:(bb + 1) * c] + u_ref[0, :, bb, :] * sk_ref[...]
        o_ref[0, :, bb, :] = xg_ref[0, :, bb, :] * conv


def _conv_inv1(t, u, xg, skip, m_inv):
    n, seq, c = u.shape
    a_half = FFT_A // 2
    sk = skip.astype(F32).reshape(1, c)
    uspec = pl.BlockSpec((1, a_half, FFT_NB, c), lambda i, j: (i, 0, j, 0))
    view = lambda a: a.reshape(n, a_half, FFT_R, c)
    out = pl.pallas_call(
        _inv1_body,
        out_shape=jax.ShapeDtypeStruct((n, a_half, FFT_R, c), F32),
        grid=(n, FFT_R // FFT_NB),
        in_specs=[pl.BlockSpec(m_inv.shape, lambda i, j: (0, 0)),
                  pl.BlockSpec((1, 2, FFT_KA_PAD, FFT_NB * c), lambda i, j: (i, 0, 0, j)),
                  uspec, uspec,
                  pl.BlockSpec((1, c), lambda i, j: (0, 0))],
        out_specs=uspec,
        compiler_params=_cparams(("parallel", "parallel")),
        name="conv_inv1",
    )(m_inv, t, view(u), view(xg), sk)
    return out.reshape(n, seq, c)


def _hyena(v, x1, x2, f_w1, f_b1, f_w2, f_b2, f_w3, f_freq, skip):
    _, seq, c = v.shape
    assert 2 * seq == FFT_A * FFT_R
    m_fwd, m_inv, g2, g2h = _conv_dft_constants()
    taps, l1 = _hyena_filter_taps(seq, f_w1, f_b1, f_w2, f_b2, f_w3, f_freq, c)
    kf = _filter_spectrum(_conv_fwd1(taps, m_fwd), l1, g2, c)
    y = v
    for order, xg in enumerate((x1, x2)):
        t = _conv_mid(_conv_fwd1(y, m_fwd), kf, order, g2, g2h, c)
        y = _conv_inv1(t, y, xg, skip[order], m_inv)
    return y


def _mix_ffn_body(x_ref, a1_ref, a2_ref, wm_ref, gm_ref, g_ref, sh_ref, sc_ref, gt_ref,
                  w1_ref, w3_ref, w2_ref, o_ref, x_scr, h_scr, acc_scr):
    j = pl.program_id(2)

    @pl.when(j == 0)
    def _():
        c1 = a1_ref.shape[-1]
        mixed = _dot(a1_ref[0].astype(BF16), wm_ref[0:c1]) + _dot(a2_ref[0].astype(BF16), wm_ref[c1:])
        xm = x_ref[0] + gm_ref[0] * mixed
        x_scr[...] = xm
        h_scr[...] = _norm_mod(xm, g_ref[...], sh_ref[0], sc_ref[0]).astype(BF16)
        acc_scr[...] = jnp.zeros_like(acc_scr)

    h = h_scr[...]
    a = _dot(h, w1_ref[...])
    u = (a * jax.nn.sigmoid(a) * _dot(h, w3_ref[...])).astype(BF16)
    acc_scr[...] += _dot(u, w2_ref[...])

    @pl.when(j == pl.num_programs(2) - 1)
    def _():
        o_ref[0] = x_scr[...] + gt_ref[0] * acc_scr[...]


def _mix_ffn(x, a1, a2, w_mix, gate_mix, g, shift, scale, gate, w1, w3, w2, tm=512, fb=1408):
    b, s, d = x.shape
    f = w1.shape[1]
    tok = lambda c: pl.BlockSpec((1, tm, c), lambda bi, i, j: (bi, i, 0))
    per_b = pl.BlockSpec((1, 1, d), lambda bi, i, j: (bi, 0, 0))
    const = lambda a: pl.BlockSpec(a.shape, lambda bi, i, j: (0, 0))
    return pl.pallas_call(
        _mix_ffn_body,
        out_shape=jax.ShapeDtypeStruct(x.shape, F32),
        grid=(b, s // tm, f // fb),
        in_specs=[tok(d), tok(a1.shape[-1]), tok(a2.shape[-1]), const(w_mix), per_b,
                  const(g), per_b, per_b, per_b,
                  pl.BlockSpec((d, fb), lambda bi, i, j: (0, j)),
                  pl.BlockSpec((d, fb), lambda bi, i, j: (0, j)),
                  pl.BlockSpec((fb, d), lambda bi, i, j: (j, 0))],
        out_specs=tok(d),
        scratch_shapes=[pltpu.VMEM((tm, d), F32), pltpu.VMEM((tm, d), BF16), pltpu.VMEM((tm, d), F32)],
        compiler_params=_cparams(("parallel", "parallel", "arbitrary")),
        name="mix_ffn",
    )(x, a1, a2, w_mix, gate_mix, g, shift, scale, gate, w1, w3, w2)


def _fm_constants(cg):
    j = np.arange(cg)[:, None]
    m = np.arange(cg)[None, :]
    ph = 2.0 * np.pi * j * m / cg
    w_cs = np.concatenate([np.cos(ph), np.sin(ph)], axis=1)
    d = np.arange(FM_A)[:, None]
    a = np.arange(FM_A)[None, :]
    ph = 2.0 * np.pi * d * a / FM_A
    fr, fi = np.cos(ph), -np.sin(ph)
    m1 = np.block([[fr, fi], [fi, -fr]])
    n = FM_A * FM_A
    dd = np.arange(FM_A)[:, None, None]
    c = np.arange(FM_A)[None, :, None]
    b = np.arange(FM_A)[None, None, :]
    th = 2.0 * np.pi * (b * c / FM_A + b * dd / n)
    gcat = np.concatenate([np.cos(th), np.sin(th)], axis=2)
    return _mxu_const(w_cs), _mxu_const(m1), _mxu_const(gcat)


def _fm_front_body(x_ref, g_ref, sh_ref, sc_ref, w_ref, m_ref, o_ref, *, cg, nb):
    d = x_ref.shape[-1]
    xs = jnp.concatenate([x_ref[0, :, bb, :] for bb in range(nb)], axis=0)
    h = _norm_mod(xs, g_ref[...], sh_ref[0], sc_ref[0]).astype(BF16)
    pq = [_dot(h[:, grp * cg:(grp + 1) * cg], w_ref[...]) for grp in range(d // cg)]
    p = jnp.concatenate([t[:, 0:cg] for t in pq], axis=1)
    q = jnp.concatenate([t[:, cg:2 * cg] for t in pq], axis=1)
    for bb in range(nb):
        rows = slice(bb * FM_A, (bb + 1) * FM_A)
        res = _dot(m_ref[...], jnp.concatenate([p[rows], q[rows]], axis=0).astype(BF16))
        o_ref[0, 0, :, bb, :] = res[0:FM_A]
        o_ref[0, 1, :, bb, :] = res[FM_A:2 * FM_A]


def _fm_front(x, g, shift, scale, w_cs, m1, nb=8):
    b, s, d = x.shape
    cg = w_cs.shape[0]
    per_b = pl.BlockSpec((1, 1, d), lambda bi, j: (bi, 0, 0))
    const = lambda a: pl.BlockSpec(a.shape, lambda bi, j: (0, 0))
    return pl.pallas_call(
        functools.partial(_fm_front_body, cg=cg, nb=nb),
        out_shape=jax.ShapeDtypeStruct((b, 2, FM_A, s // FM_A, d), F32),
        grid=(b, s // FM_A // nb),
        in_specs=[pl.BlockSpec((1, FM_A, nb, d), lambda bi, j: (bi, 0, j, 0)),
                  const(g), per_b, per_b, const(w_cs), const(m1)],
        out_specs=pl.BlockSpec((1, 2, FM_A, nb, d), lambda bi, j: (bi, 0, 0, j, 0)),
        compiler_params=_cparams(("parallel", "parallel")),
        name="fm_front",
    )(x.reshape(b, FM_A, s // FM_A, d), g, shift, scale, w_cs, m1)


def _fm_s2_body(s_ref, g_ref, o_ref, *, dblk, scale):
    for i in range(dblk):
        s2 = jnp.concatenate([s_ref[0, 0, i], s_ref[0, 1, i]], axis=0).astype(BF16)
        o_ref[:, i, :] = _dot(g_ref[i], s2) * scale


def _fm_stage2(sv, gcat, seq, d, dblk=8):
    b = sv.shape[0]
    scale = 1.0 / math.sqrt(seq * (d // F_GROUPS))
    out = pl.pallas_call(
        functools.partial(_fm_s2_body, dblk=dblk, scale=scale),
        out_shape=jax.ShapeDtypeStruct((b * FM_A, dblk * (FM_A // dblk), d), F32),
        grid=(b, FM_A // dblk),
        in_specs=[pl.BlockSpec((1, 2, dblk, FM_A, d), lambda bi, j: (bi, 0, j, 0, 0)),
                  pl.BlockSpec((dblk, FM_A, 2 * FM_A), lambda bi, j: (j, 0, 0))],
        out_specs=pl.BlockSpec((FM_A, dblk, d), lambda bi, j: (bi, j, 0)),
        compiler_params=_cparams(("parallel", "parallel")),
        name="fm_stage2",
    )(sv, gcat)
    return out.reshape(b, seq, d)


def _fourier_mix(x, g, shift, scale):
    b, s, d = x.shape
    assert s == FM_A * FM_A
    w_cs, m1, gcat = _fm_constants(d // F_GROUPS)
    return _fm_stage2(_fm_front(x, g, shift, scale, w_cs, m1), gcat, s, d)


LANES = 128
ROW_SL = 8
MOE_TM = 1024
DMA_WINDOW = 128


def _router_body(x_ref, yf_ref, wf_ref, gf_ref, g_ref, sh_ref, sc_ref, wr_ref, br_ref,
                 xo_ref, h_ref, meta_ref, gw_ref, cnt_ref, carry):
    i = pl.program_id(0)

    @pl.when(i == 0)
    def _():
        carry[...] = jnp.zeros_like(carry)

    xm = x_ref[...] + gf_ref[0] * _dot(yf_ref[...].astype(BF16), wf_ref[...])
    xo_ref[...] = xm
    h = _norm_mod(xm, g_ref[...], sh_ref[0], sc_ref[0])
    _rows_to_tiles(h_ref, h)
    h_hi = h.astype(BF16)
    h_lo = (h - h_hi.astype(F32)).astype(BF16)
    by_hi = _dot(h_hi, wr_ref[...])
    logits = by_hi[:, 0:LANES] + by_hi[:, LANES:] + _dot(h_lo, wr_ref[:, 0:LANES]) + br_ref[...]
    lane = lax.broadcasted_iota(jnp.int32, logits.shape, 1)
    nl = logits.shape[-1]
    m1 = jnp.max(logits, axis=-1, keepdims=True)
    i1 = jnp.min(jnp.where(logits == m1, lane, nl), axis=-1, keepdims=True)
    rest = jnp.where(lane == i1, -3.0e38, logits)
    m2 = jnp.max(rest, axis=-1, keepdims=True)
    i2 = jnp.min(jnp.where(rest == m2, lane, nl), axis=-1, keepdims=True)
    e = jnp.exp(m2 - m1)
    gw_ref[...] = jnp.where(lane == 0, 1.0 / (1.0 + e), jnp.where(lane == 1, e / (1.0 + e), 0.0))
    onehot = jnp.where((lane == i1) | (lane == i2), 1.0, 0.0)
    tm = onehot.shape[0]
    earlier = lax.broadcasted_iota(jnp.int32, (tm, tm), 0) > lax.broadcasted_iota(jnp.int32, (tm, tm), 1)
    excl = _dot(jnp.where(earlier, 1.0, 0.0).astype(BF16), onehot.astype(BF16)) + carry[...]
    r1 = jnp.sum(jnp.where(lane == i1, excl, 0.0), axis=-1, keepdims=True).astype(jnp.int32)
    r2 = jnp.sum(jnp.where(lane == i2, excl, 0.0), axis=-1, keepdims=True).astype(jnp.int32)
    meta_ref[...] = jnp.where(lane == 0, i1, jnp.where(lane == 1, i2, jnp.where(lane == 2, r1, jnp.where(lane == 3, r2, 0))))
    carry[...] = carry[...] + jnp.sum(onehot, axis=0, keepdims=True)
    cnt_ref[...] = carry[...]


def _router(x, y_f, w_f, gate_f, g, shift, scale, w_router, b_router, tm=512):
    b, s, d = x.shape
    t = b * s
    ne = w_router.shape[1]
    wr = jnp.pad(w_router.astype(F32), ((0, 0), (0, LANES - ne)))
    wr_hi = wr.astype(BF16)
    wr = jnp.concatenate([wr_hi, (wr - wr_hi.astype(F32)).astype(BF16)], axis=1)
    br = jnp.pad(b_router.astype(F32).reshape(1, ne), ((0, 0), (0, LANES - ne)), constant_values=NEG_INF)
    spt = s // tm
    per_b = pl.BlockSpec((1, 1, d), lambda i: (i // spt, 0, 0))
    const = lambda a: pl.BlockSpec(a.shape, lambda i: (0, 0))
    tok = pl.BlockSpec((tm, d), lambda i: (i, 0))
    return pl.pallas_call(
        _router_body,
        out_shape=[jax.ShapeDtypeStruct((t, d), F32),
                   jax.ShapeDtypeStruct((t * ROW_SL, LANES), F32),
                   jax.ShapeDtypeStruct((t, LANES), jnp.int32),
                   jax.ShapeDtypeStruct((t, LANES), F32),
                   jax.ShapeDtypeStruct((1, LANES), F32)],
        grid=(t // tm,),
        in_specs=[tok, tok, const(w_f), per_b, const(g), per_b, per_b, const(wr), const(br)],
        out_specs=[tok,
                   pl.BlockSpec((tm * ROW_SL, LANES), lambda i: (i, 0)),
                   pl.BlockSpec((tm, LANES), lambda i: (i, 0)),
                   pl.BlockSpec((tm, LANES), lambda i: (i, 0)),
                   pl.BlockSpec((1, LANES), lambda i: (0, 0))],
        scratch_shapes=[pltpu.VMEM((1, LANES), F32)],
        compiler_params=_cparams(("arbitrary",)),
        name="router",
    )(x.reshape(t, d), y_f.reshape(t, d), w_f, gate_f, g, shift, scale, wr, br)


def _moe_plan(meta, counts, ne, tm):
    i1, i2, r1, r2 = meta[:, 0], meta[:, 1], meta[:, 2], meta[:, 3]
    cnt = counts[0, :ne].astype(jnp.int32)
    padded = ((cnt + tm - 1) // tm) * tm
    ends = jnp.cumsum(padded)
    offs = ends - padded
    pick = lambda idx: sum(jnp.where(idx == e, offs[e], 0) for e in range(ne))
    pos = jnp.concatenate([pick(i1) + r1, pick(i2) + r2]).astype(jnp.int32)
    n_tiles = (2 * meta.shape[0]) // tm + ne
    n_used = (ends[ne - 1] // tm).astype(jnp.int32)
    tile_start = jnp.minimum(jnp.arange(n_tiles, dtype=jnp.int32), n_used - 1) * tm
    tile_expert = jnp.sum(tile_start[:, None] >= ends[None, :], axis=1).astype(jnp.int32)
    return pos, offs + cnt, padded - cnt, tile_expert, n_used.reshape(1)


def _windowed_copies(n, start_copy, wait_one, per_iter):
    def body(i, carry):
        @pl.when(i >= DMA_WINDOW)
        def _():
            for _ in range(per_iter):
                wait_one()
        start_copy(i)
        return carry

    lax.fori_loop(0, n, body, 0)

    def drain(i, carry):
        for _ in range(per_iter):
            wait_one()
        return carry

    lax.fori_loop(0, jnp.minimum(n, DMA_WINDOW), drain, 0)


def _tile_of(ref, row):
    return ref.at[pl.ds(pl.multiple_of(row * ROW_SL, ROW_SL), ROW_SL)]


def _tiles_to_rows(ref, n, first=0):
    return jnp.concatenate([ref[pl.ds(first * ROW_SL + sl, n, stride=ROW_SL), :] for sl in range(ROW_SL)], axis=1)


def _rows_to_tiles(ref, val):
    n = val.shape[0]
    for sl in range(ROW_SL):
        ref[pl.ds(sl, n, stride=ROW_SL), :] = val[:, sl * LANES:(sl + 1) * LANES]


def _dispatch_body(pos_ref, pad_start_ref, pad_n_ref, h_ref, xs_hbm, sem, *, n_tok, ne):
    i = pl.program_id(0)
    td = h_ref.shape[0] // ROW_SL
    base = i * td
    copy = lambda src, dst: pltpu.make_async_copy(_tile_of(h_ref, src), _tile_of(xs_hbm, dst), sem)
    wait_one = lambda: copy(0, 0).wait()

    def start_token(r, carry):
        copy(r, pos_ref[base + r]).start()
        copy(r, pos_ref[n_tok + base + r]).start()
        return carry

    lax.fori_loop(0, td, start_token, 0, unroll=8)
    whole_tile = pltpu.make_async_copy(h_ref, xs_hbm.at[pl.ds(0, td * ROW_SL)], sem)
    whole_tile.wait()
    whole_tile.wait()

    @pl.when(i == 0)
    def _():
        for e in range(ne):
            first = pad_start_ref[e]
            _windowed_copies(pad_n_ref[e], lambda r: copy(0, first + r).start(), wait_one, 1)


def _moe_dispatch(h3, pos, pad_start, pad_n, n_rows, td=1024):
    n_tok = h3.shape[0] // ROW_SL
    ne = pad_start.shape[0]
    return pl.pallas_call(
        functools.partial(_dispatch_body, n_tok=n_tok, ne=ne),
        out_shape=jax.ShapeDtypeStruct((n_rows * ROW_SL, LANES), h3.dtype),
        grid_spec=pltpu.PrefetchScalarGridSpec(
            num_scalar_prefetch=3, grid=(n_tok // td,),
            in_specs=[pl.BlockSpec((td * ROW_SL, LANES), lambda i, p, ps, pn: (i, 0))],
            out_specs=pl.BlockSpec(memory_space=pl.ANY),
            scratch_shapes=[pltpu.SemaphoreType.DMA(())]),
        compiler_params=_cparams(("arbitrary",)),
        name="moe_dispatch",
    )(pos, pad_start, pad_n, h3)


def _moe_grouped_body(te_ref, nu_ref, xs_ref, w1_ref, w3_ref, w2_ref, y_ref, xb_scr, acc_scr):
    i = pl.program_id(0)
    j = pl.program_id(1)
    tm = xb_scr.shape[0]

    @pl.when(i < nu_ref[0])
    def _():
        @pl.when(j == 0)
        def _():
            xb_scr[...] = _tiles_to_rows(xs_ref, tm).astype(BF16)

        h = xb_scr[...]
        a = _dot(h, w1_ref[0].astype(BF16))
        u = (a * jax.nn.sigmoid(a) * _dot(h, w3_ref[0].astype(BF16))).astype(BF16)
        part = _dot(u, w2_ref[0].astype(BF16))

        @pl.when(j == 0)
        def _():
            acc_scr[...] = part

        @pl.when(j > 0)
        def _():
            acc_scr[...] += part

        @pl.when(j == pl.num_programs(1) - 1)
        def _():
            _rows_to_tiles(y_ref, acc_scr[...])


def _moe_grouped(xs, tile_expert, n_used, w1, w3, w2, tm, fb=512):
    ne, d, f = w1.shape
    n_rows = xs.shape[0] // ROW_SL
    nj = f // fb
    row_tile = lambda i, j, te, nu: (jnp.maximum(jnp.minimum(i, nu[0] - 1), 0), 0)
    jj = lambda i, j, nu: jnp.where(i < nu[0], j, nj - 1)
    return pl.pallas_call(
        _moe_grouped_body,
        out_shape=jax.ShapeDtypeStruct(xs.shape, F32),
        grid_spec=pltpu.PrefetchScalarGridSpec(
            num_scalar_prefetch=2, grid=(n_rows // tm, nj),
            in_specs=[pl.BlockSpec((tm * ROW_SL, LANES), row_tile),
                      pl.BlockSpec((1, d, fb), lambda i, j, te, nu: (te[i], 0, jj(i, j, nu))),
                      pl.BlockSpec((1, d, fb), lambda i, j, te, nu: (te[i], 0, jj(i, j, nu))),
                      pl.BlockSpec((1, fb, d), lambda i, j, te, nu: (te[i], jj(i, j, nu), 0))],
            out_specs=pl.BlockSpec((tm * ROW_SL, LANES), row_tile),
            scratch_shapes=[pltpu.VMEM((tm, d), BF16), pltpu.VMEM((tm, d), F32)]),
        compiler_params=_cparams(("arbitrary", "arbitrary")),
        name="moe_grouped",
    )(tile_expert, n_used, xs, w1, w3, w2)


def _moe_final_body(pos_ref, x_ref, y_hbm, gw_ref, gt_ref, fg_ref, o_ref, yg_scr, sem, *, n_tok):
    i = pl.program_id(0)
    tc = x_ref.shape[0]
    slot = i % 2

    def gather_tile(step, into):
        base = step * tc

        def start_token(r, carry):
            dst = yg_scr.at[into]
            pltpu.make_async_copy(_tile_of(y_hbm, pos_ref[base + r]), _tile_of(dst, r), sem.at[into]).start()
            pltpu.make_async_copy(_tile_of(y_hbm, pos_ref[n_tok + base + r]), _tile_of(dst, tc + r),
                                  sem.at[into]).start()
            return carry

        lax.fori_loop(0, tc, start_token, 0, unroll=8)

    @pl.when(i == 0)
    def _():
        gather_tile(0, 0)

    @pl.when(i + 1 < pl.num_programs(0))
    def _():
        gather_tile(i + 1, 1 - slot)

    pltpu.make_async_copy(y_hbm.at[pl.ds(0, 2 * tc * ROW_SL)], yg_scr.at[slot], sem.at[slot]).wait()
    gw = gw_ref[...]
    rows = yg_scr.at[slot]
    y = gw[:, 0:1] * _tiles_to_rows(rows, tc) + gw[:, 1:2] * _tiles_to_rows(rows, tc, first=tc)
    xo = x_ref[...] + gt_ref[0] * y
    ms = jnp.mean(xo * xo, axis=-1, keepdims=True)
    o_ref[...] = xo * lax.rsqrt(ms + EPS) * fg_ref[...]


def _moe_final(x, y, pos, gw, gt, final_g, tc=256):
    b, s, d = x.shape
    t = b * s
    spt = s // tc
    nsl = d // LANES
    out = pl.pallas_call(
        functools.partial(_moe_final_body, n_tok=t),
        out_shape=jax.ShapeDtypeStruct((t, d), F32),
        grid_spec=pltpu.PrefetchScalarGridSpec(
            num_scalar_prefetch=1, grid=(t // tc,),
            in_specs=[pl.BlockSpec((tc, d), lambda i, p: (i, 0)),
                      pl.BlockSpec(memory_space=pl.ANY),
                      pl.BlockSpec((tc, LANES), lambda i, p: (i, 0)),
                      pl.BlockSpec((1, 1, d), lambda i, p: (i // spt, 0, 0)),
                      pl.BlockSpec(final_g.shape, lambda i, p: (0, 0))],
            out_specs=pl.BlockSpec((tc, d), lambda i, p: (i, 0)),
            scratch_shapes=[pltpu.VMEM((2, 2 * tc * ROW_SL, LANES), F32), pltpu.SemaphoreType.DMA((2,))]),
        compiler_params=_cparams(("arbitrary",)),
        name="moe_final",
    )(pos, x.reshape(t, d), y, gw, gt, final_g)
    return out.reshape(b, s, d)


def _moe_routed(x, y_f, w_f, gate_f, g, shift, scale, gt, final_g, w_router, b_router, w1, w3, w2):
    ne = w1.shape[0]
    tm = MOE_TM
    x1, h3, meta, gw, counts = _router(x, y_f, w_f, gate_f, g, shift, scale, w_router, b_router)
    pos, pad_start, pad_n, tile_expert, n_used = _moe_plan(meta, counts, ne, tm)
    assert x.shape[-1] == ROW_SL * LANES
    n_rows = (2 * (h3.shape[0] // ROW_SL) // tm + ne) * tm
    xs = _moe_dispatch(h3, pos, pad_start, pad_n, n_rows)
    y = _moe_grouped(xs, tile_expert, n_used, w1, w3, w2, tm)
    return _moe_final(x1.reshape(x.shape), y, pos, gw, gt, final_g)


def kernel(x, c, ctx, c_ctx, w_ada, b_ada, norm_g, w_in, hy_short_w, hy_short_b, hy_f_w1, hy_f_b1, hy_f_w2, hy_f_b2, hy_f_w3, hy_f_freq, hy_skip, na_rpb, w_mix_out, ffn_w1, ffn_w3, ffn_w2, w_fourier, w_router, b_router, moe_w1, moe_w3, moe_w2, final_g):
    b, s, d = x.shape
    depth = w_ada.shape[0]
    assert depth == 2, "layer 0 mixes with Hyena/attention, layer 1 with Fourier/MoE"
    c_hy = hy_skip.shape[-1]
    c_na = d - c_hy

    cvec = jnp.concatenate([c, c_ctx[None, :], jnp.zeros((8 - b - 1, d), F32)], axis=0)
    mods = _ada(cvec, w_ada, b_ada)

    def mod(layer, idx, ctx_row=False):
        m = mods[layer, :, idx * d:(idx + 1) * d]
        return m[b:b + 1, None, :] if ctx_row else m[0:b, None, :]

    row = lambda a: a.reshape(1, -1)

    w_in0 = w_in[0].astype(BF16)
    w_hy, w_qkv = w_in0[:, 0:3 * c_hy], w_in0[:, 3 * c_hy:]
    v, x1, x2, q, k, va = _inproj(x, row(norm_g[0, 0]), mod(0, 0), mod(0, 1), w_hy, w_qkv,
                                  hy_short_w[0], row(hy_short_b[0]))
    kc, vc = _ctxkv(ctx, row(norm_g[0, 0]), mod(0, 0, True), mod(0, 1, True), w_qkv[:, c_na:])
    y_na = _natt(q, k, va, kc, vc, _na_bias_table(na_rpb[0]))
    y_hy = _hyena(v, x1, x2, hy_f_w1[0], hy_f_b1[0], hy_f_w2[0], hy_f_b2[0], hy_f_w3[0],
                  hy_f_freq[0], hy_skip[0])
    x = _mix_ffn(x, y_hy, y_na, w_mix_out[0].astype(BF16), mod(0, 2),
                 row(norm_g[0, 1]), mod(0, 3), mod(0, 4), mod(0, 5),
                 ffn_w1[0].astype(BF16), ffn_w3[0].astype(BF16), ffn_w2[0].astype(BF16))

    y_f = _fourier_mix(x, row(norm_g[1, 0]), mod(1, 0), mod(1, 1))
    return _moe_routed(x, y_f, w_fourier[0].astype(BF16), mod(1, 2),
                       row(norm_g[1, 1]), mod(1, 3), mod(1, 4), mod(1, 5), row(final_g),
                       w_router[0], b_router[0],
                       moe_w1[0], moe_w3[0], moe_w2[0])
```

```python
import functools
import math

import numpy as np
import jax
import jax.numpy as jnp
from jax import lax
from jax.experimental import pallas as pl
from jax.experimental.pallas import tpu as pltpu

F32 = jnp.float32
BF16 = jnp.bfloat16
HIGHEST = lax.Precision.HIGHEST

GRID_W = 64
NA_HEAD_DIM = 32
NA_WIN_R = 8
NA_WIN_C = 16
HYENA_EMB = 33
HYENA_BANDS = (HYENA_EMB - 1) // 2
HYENA_FAST_DECAY = 0.3
HYENA_SLOW_DECAY = 1.5
HYENA_TARGET = 1e-2
F_GROUPS = 4
N_MOD = 6
EPS = 1e-6
NEG_INF = -1e30

FFT_A = 64
FFT_R = 128
FFT_KA = FFT_A // 2 + 1
FFT_KA_PAD = 40
FM_A = 64

VMEM_LIMIT = 48 * 1024 * 1024


def _cparams(sem, vmem_limit=VMEM_LIMIT):
    return pltpu.CompilerParams(dimension_semantics=sem, vmem_limit_bytes=vmem_limit)


def _dot(a, b):
    return jnp.dot(a, b, preferred_element_type=F32)


def _mxu_const(m):
    return jnp.asarray(m, dtype=F32).astype(BF16)


def _norm_mod(x, g, shift, scale):
    ms = jnp.mean(x * x, axis=-1, keepdims=True)
    y = x * lax.rsqrt(ms + EPS) * g
    return y * (1.0 + scale) + shift


def _ada_body(c_ref, w_ref, b_ref, o_ref):
    cv = c_ref[...]
    s = cv * jax.nn.sigmoid(cv)
    o_ref[0] = jnp.dot(s, w_ref[0], precision=HIGHEST, preferred_element_type=F32) + b_ref[0]


def _ada(cvec, w_ada, b_ada):
    depth, d, n = w_ada.shape
    rows = cvec.shape[0]
    bn = n // 4
    return pl.pallas_call(
        _ada_body,
        out_shape=jax.ShapeDtypeStruct((depth, rows, n), F32),
        grid=(depth, n // bn),
        in_specs=[pl.BlockSpec((rows, d), lambda l, j: (0, 0)),
                  pl.BlockSpec((1, d, bn), lambda l, j: (l, 0, j)),
                  pl.BlockSpec((1, 1, bn), lambda l, j: (l, 0, j))],
        out_specs=pl.BlockSpec((1, rows, bn), lambda l, j: (l, 0, j)),
        compiler_params=_cparams(("parallel", "parallel")),
        name="ada",
    )(cvec, w_ada, b_ada.reshape(depth, 1, n))


def _inproj_body(x_ref, xp_ref, xn_ref, g_ref, sh_ref, sc_ref, why_ref, wqkv_ref, sw_ref, sb_ref,
                 v_ref, x1_ref, x2_ref, q_ref, k_ref, va_ref, *, n_tiles, q_scale, c_hy, c_na):
    i = pl.program_id(1)
    g, sh, sc = g_ref[...], sh_ref[0], sc_ref[0]
    hf = _norm_mod(x_ref[0], g, sh, sc)
    h = hf.astype(BF16)
    tm = hf.shape[0]
    hx = jnp.concatenate([_norm_mod(xp_ref[0], g, sh, sc), hf, _norm_mod(xn_ref[0], g, sh, sc)], axis=0)
    zx = _dot(hx.astype(BF16), why_ref[...])
    zh = zx[8:8 + tm]
    zp = jnp.where(i > 0, zx[7:8], 0.0)
    zn = jnp.where(i < n_tiles - 1, zx[8 + tm:9 + tm], 0.0)
    row = lax.broadcasted_iota(jnp.int32, zh.shape, 0)
    z_m1 = jnp.where(row == 0, zp, pltpu.roll(zh, 1, 0))
    z_p1 = jnp.where(row == tm - 1, zn, pltpu.roll(zh, tm - 1, 0))
    sw = sw_ref[...]
    zc = z_m1 * sw[0:1] + zh * sw[1:2] + z_p1 * sw[2:3] + sb_ref[...]
    v_ref[0] = zc[:, 0:c_hy]
    x1_ref[0] = zc[:, c_hy:2 * c_hy]
    x2_ref[0] = zc[:, 2 * c_hy:3 * c_hy]
    zq = _dot(h, wqkv_ref[...])
    q_ref[0] = (zq[:, 0:c_na] * q_scale).astype(BF16)
    k_ref[0] = zq[:, c_na:2 * c_na].astype(BF16)
    va_ref[0] = zq[:, 2 * c_na:3 * c_na].astype(BF16)


def _inproj(x, g, shift, scale, w_hy, w_qkv, short_w, short_b, tm=512):
    b, s, d = x.shape
    c_hy = w_hy.shape[1] // 3
    c_na = w_qkv.shape[1] // 3
    n_tiles = s // tm
    r8 = tm // 8
    body = functools.partial(_inproj_body, n_tiles=n_tiles, q_scale=NA_HEAD_DIM ** -0.5,
                             c_hy=c_hy, c_na=c_na)
    tok = lambda c: pl.BlockSpec((1, tm, c), lambda bi, i: (bi, i, 0))
    full2 = lambda a: pl.BlockSpec(a.shape, lambda bi, i: (0, 0))
    per_b = pl.BlockSpec((1, 1, d), lambda bi, i: (bi, 0, 0))
    return pl.pallas_call(
        body,
        out_shape=[jax.ShapeDtypeStruct((b, s, c_hy), F32)] * 3 + [jax.ShapeDtypeStruct((b, s, c_na), BF16)] * 3,
        grid=(b, n_tiles),
        in_specs=[tok(d),
                  pl.BlockSpec((1, 8, d), lambda bi, i: (bi, jnp.maximum(i * r8 - 1, 0), 0)),
                  pl.BlockSpec((1, 8, d), lambda bi, i: (bi, jnp.minimum((i + 1) * r8, s // 8 - 1), 0)),
                  full2(g), per_b, per_b, full2(w_hy), full2(w_qkv), full2(short_w), full2(short_b)],
        out_specs=[tok(c_hy)] * 3 + [tok(c_na)] * 3,
        compiler_params=_cparams(("parallel", "parallel")),
        name="inproj",
    )(x, x, x, g, shift, scale, w_hy, w_qkv, short_w, short_b)


def _ctxkv_body(x_ref, g_ref, sh_ref, sc_ref, w_ref, k_ref, v_ref, *, c_na):
    h = _norm_mod(x_ref[0], g_ref[...], sh_ref[0], sc_ref[0]).astype(BF16)
    z = _dot(h, w_ref[...])
    k_ref[0] = z[:, 0:c_na].astype(BF16)
    v_ref[0] = z[:, c_na:2 * c_na].astype(BF16)


def _ctxkv(ctx, g, shift, scale, w_kv):
    b, n, d = ctx.shape
    c_na = w_kv.shape[1] // 2
    one = pl.BlockSpec((1, 1, d), lambda bi: (0, 0, 0))
    return pl.pallas_call(
        functools.partial(_ctxkv_body, c_na=c_na),
        out_shape=[jax.ShapeDtypeStruct((b, n, c_na), BF16)] * 2,
        grid=(b,),
        in_specs=[pl.BlockSpec((1, n, d), lambda bi: (bi, 0, 0)),
                  pl.BlockSpec(g.shape, lambda bi: (0, 0)), one, one,
                  pl.BlockSpec(w_kv.shape, lambda bi: (0, 0))],
        out_specs=[pl.BlockSpec((1, n, c_na), lambda bi: (bi, 0, 0))] * 2,
        compiler_params=_cparams(("parallel",)),
        name="ctxkv",
    )(ctx, g, shift, scale, w_kv)


NA_HEADS_PER_BLK = 4
NA_ROWS_PER_STEP = 4


def _na_bias_body(r_ref, e_ref, ok_ref, o_ref):
    t = jnp.dot(r_ref[...], e_ref[...], precision=HIGHEST, preferred_element_type=F32)
    o_ref[...] = jnp.where(ok_ref[...] > 0.5, t, NEG_INF)


def _na_bias_table(rpb):
    w = GRID_W
    h, nr, nc = rpb.shape
    col = np.arange(w)[:, None]
    kc = np.arange(w)[None, :]
    c_start = np.clip(col - NA_WIN_C // 2, 0, w - NA_WIN_C)
    valid = ((kc >= c_start) & (kc < c_start + NA_WIN_C)).reshape(1, w * w)
    expand = (np.arange(32)[:, None, None] == (kc - col + NA_WIN_C - 1)[None]).reshape(32, w * w)
    rp = jnp.pad(rpb.astype(F32).reshape(h * nr, nc), ((0, 0), (0, 32 - nc)))
    full = lambda a: pl.BlockSpec(a.shape, lambda: (0,) * a.ndim)
    expand = jnp.asarray(expand, dtype=F32)
    ok = jnp.asarray(valid, dtype=F32)
    toep = pl.pallas_call(
        _na_bias_body,
        out_shape=jax.ShapeDtypeStruct((h * nr, w * w), F32),
        in_specs=[full(rp), full(expand), full(ok)],
        out_specs=pl.BlockSpec((h * nr, w * w), lambda: (0, 0)),
        name="na_bias",
    )(rp, expand, ok)
    t2 = toep.reshape(h, nr, w, w).transpose(0, 2, 1, 3).reshape(h, w, nr * w)
    slabs = jnp.stack([t2[:, :, (NA_WIN_R - 1 - off) * w:(2 * NA_WIN_R - 1 - off) * w]
                       for off in range(NA_WIN_R)], axis=1)
    hpb = NA_HEADS_PER_BLK
    slabs = slabs.reshape(h // hpb, hpb, NA_WIN_R, w, NA_WIN_R * w).transpose(0, 2, 1, 3, 4)
    return slabs.reshape(h // hpb, NA_WIN_R, hpb * w, NA_WIN_R * w)


def _natt_body(q_ref, k_ref, v_ref, kc_ref, vc_ref, bias_ref, o_ref, *, rows):
    w = GRID_W
    hpb = NA_HEADS_PER_BLK
    nloc = NA_WIN_R * w
    lane = lax.broadcasted_iota(jnp.int32, (1, hpb * NA_HEAD_DIM), 1)
    in_head = [(lane >= NA_HEAD_DIM * hh) & (lane < NA_HEAD_DIM * (hh + 1)) for hh in range(hpb)]
    kcx = kc_ref[0]
    vcx = vc_ref[0]
    nt = (((1,), (1,)), ((), ()))

    def one_row(r):
        r0 = jnp.clip(r - NA_WIN_R // 2, 0, rows - NA_WIN_R)
        off = r - r0
        qs = q_ref[0, pl.ds(pl.multiple_of(r * w, w), w), :]
        kw = k_ref[0, pl.ds(pl.multiple_of(r0 * w, w), nloc), :]
        vw = v_ref[0, pl.ds(pl.multiple_of(r0 * w, w), nloc), :]
        zero = jnp.zeros_like(qs)
        qst = jnp.concatenate([jnp.where(m, qs, zero) for m in in_head], axis=0)
        s_loc = lax.dot_general(qst, kw, nt, preferred_element_type=F32) + bias_ref[0, off]
        s_ctx = lax.dot_general(qst, kcx, nt, preferred_element_type=F32)
        m = jnp.maximum(jnp.max(s_loc, axis=-1, keepdims=True), jnp.max(s_ctx, axis=-1, keepdims=True))
        p_loc = jnp.exp(s_loc - m)
        p_ctx = jnp.exp(s_ctx - m)
        den = jnp.sum(p_loc, axis=-1, keepdims=True) + jnp.sum(p_ctx, axis=-1, keepdims=True)
        o = (_dot(p_loc.astype(BF16), vw) + _dot(p_ctx.astype(BF16), vcx)) * (1.0 / den)
        acc = jnp.where(in_head[0], o[0:w], 0.0)
        for hh in range(1, hpb):
            acc = acc + jnp.where(in_head[hh], o[hh * w:(hh + 1) * w], 0.0)
        o_ref[0, pl.ds(pl.multiple_of(r * w, w), w), :] = acc.astype(BF16)

    def row_group(i, carry):
        for r in range(NA_ROWS_PER_STEP):
            one_row(NA_ROWS_PER_STEP * i + r)
        return carry

    lax.fori_loop(0, rows // NA_ROWS_PER_STEP, row_group, 0)


def _natt(q, k, v, kc, vc, bias):
    b, s, c = q.shape
    nctx = kc.shape[1]
    lw = NA_HEADS_PER_BLK * NA_HEAD_DIM
    rows = s // GRID_W
    seq = pl.BlockSpec((1, s, lw), lambda bi, g: (bi, 0, g))
    cx = pl.BlockSpec((1, nctx, lw), lambda bi, g: (bi, 0, g))
    return pl.pallas_call(
        functools.partial(_natt_body, rows=rows),
        out_shape=jax.ShapeDtypeStruct((b, s, c), BF16),
        grid=(b, c // lw),
        in_specs=[seq, seq, seq, cx, cx,
                  pl.BlockSpec((1,) + bias.shape[1:], lambda bi, g: (g, 0, 0, 0))],
        out_specs=seq,
        compiler_params=_cparams(("parallel", "parallel")),
        name="natt",
    )(q, k, v, kc, vc, bias)


def _hyena_feats(seq_len):
    t = jnp.linspace(0.0, 1.0, seq_len, dtype=F32)[:, None]
    bands = jnp.linspace(1e-4, HYENA_BANDS - 1, HYENA_BANDS, dtype=F32)
    ang = (2.0 * math.pi / seq_len) * jnp.arange(seq_len, dtype=F32)[:, None] * bands[None, :]
    feats = jnp.concatenate([t, jnp.cos(ang), -jnp.sin(ang)], axis=-1)
    return jnp.pad(feats, ((0, 0), (0, 128 - HYENA_EMB)))


def _filt_body(feat_ref, w1_ref, b1_ref, w2_ref, b2_ref, w3_ref, fr_ref, dl_ref, o_ref, l1_ref, h_scr,
               *, halves):
    j = pl.program_id(0)
    hp = functools.partial(jnp.dot, precision=HIGHEST, preferred_element_type=F32)
    feats = feat_ref[...]

    @pl.when(j == 0)
    def _():
        fr = fr_ref[...]
        h = jnp.sin(fr[0:1] * (hp(feats, w1_ref[...]) + b1_ref[...]))
        h_scr[...] = jnp.sin(fr[1:2] * (hp(h, w2_ref[...]) + b2_ref[...]))

    hc = hp(h_scr[...], w3_ref[...])
    t = feats[:, 0:1]
    hc = hc * jnp.exp(-t * dl_ref[...])
    row = lax.broadcasted_iota(jnp.int32, hc.shape, 0)
    hc = jnp.where((row == 0) & ((j // halves) % 2 == 1), 0.0, hc)
    l1_ref[0] = jnp.sum(jnp.abs(hc), axis=0, keepdims=True)
    o_ref[0] = hc


def _hyena_filter_taps(seq_len, f_w1, f_b1, f_w2, f_b2, f_w3, f_freq, c_hy):
    feats = _hyena_feats(seq_len)
    hid = f_w1.shape[1]
    w1 = jnp.pad(f_w1.astype(F32), ((0, 128 - HYENA_EMB), (0, 0)))
    deltas = jnp.abs(jnp.linspace(math.log(HYENA_TARGET) / HYENA_SLOW_DECAY,
                                  math.log(HYENA_TARGET) / HYENA_FAST_DECAY, c_hy, dtype=F32))[None, :]
    nblk = f_w3.shape[1] // c_hy
    halves = 2
    cb = c_hy // halves
    c0 = lambda a: pl.BlockSpec(a.shape, lambda j: (0, 0))
    b1, b2 = f_b1.reshape(1, hid), f_b2.reshape(1, hid)
    return pl.pallas_call(
        functools.partial(_filt_body, halves=halves),
        out_shape=[jax.ShapeDtypeStruct((nblk, seq_len, c_hy), F32),
                   jax.ShapeDtypeStruct((nblk, 1, c_hy), F32)],
        grid=(nblk * halves,),
        in_specs=[c0(feats), c0(w1), c0(b1), c0(f_w2), c0(b2),
                  pl.BlockSpec((hid, cb), lambda j: (0, j)), c0(f_freq),
                  pl.BlockSpec((1, cb), lambda j: (0, j % halves))],
        out_specs=[pl.BlockSpec((1, seq_len, cb), lambda j: (j // halves, 0, j % halves)),
                   pl.BlockSpec((1, 1, cb), lambda j: (j // halves, 0, j % halves))],
        scratch_shapes=[pltpu.VMEM((seq_len, hid), F32)],
        compiler_params=_cparams(("arbitrary",)),
        name="hyena_filter",
    )(feats, w1, b1, f_w2, b2, f_w3, f_freq, deltas)


def _conv_dft_constants():
    a_half = FFT_A // 2
    n = FFT_A * FFT_R
    ka = np.arange(FFT_KA)[:, None]
    a = np.arange(a_half)[None, :]
    ph = 2.0 * np.pi * ka * a / FFT_A
    m_fwd = np.zeros((2 * FFT_KA_PAD, a_half))
    m_fwd[:FFT_KA] = np.cos(ph)
    m_fwd[FFT_KA_PAD:FFT_KA_PAD + FFT_KA] = -np.sin(ph)
    wgt = np.where((ka == 0) | (ka == FFT_A // 2), 1.0, 2.0)
    m_inv = np.zeros((a_half, 2 * FFT_KA_PAD))
    m_inv[:, :FFT_KA] = (wgt * np.cos(ph)).T / n
    m_inv[:, FFT_KA_PAD:FFT_KA_PAD + FFT_KA] = (-wgt * np.sin(ph)).T / n
    kb = np.arange(FFT_R)[None, :, None]
    b = np.arange(FFT_R)[None, None, :]
    kaa = np.arange(FFT_KA)[:, None, None]
    th = 2.0 * np.pi * (b * kb / FFT_R + b * kaa / n)
    gr, gi = np.cos(th), -np.sin(th)
    g2 = np.zeros((FFT_KA_PAD, 2 * FFT_R, 2 * FFT_R))
    g2[:FFT_KA] = np.block([[gr, -gi], [gi, gr]])
    grt, git = gr.transpose(0, 2, 1), gi.transpose(0, 2, 1)
    g2h = np.zeros_like(g2)
    g2h[:FFT_KA] = np.block([[grt, git], [-git, grt]])
    return _mxu_const(m_fwd), _mxu_const(m_inv), _mxu_const(g2), _mxu_const(g2h)


FFT_NB = 16


def _fwd1_body(m_ref, u_ref, o_ref):
    u = jnp.concatenate([u_ref[0, :, bb, :] for bb in range(FFT_NB)], axis=1).astype(BF16)
    res = _dot(m_ref[...], u)
    o_ref[0, 0] = res[0:FFT_KA_PAD]
    o_ref[0, 1] = res[FFT_KA_PAD:2 * FFT_KA_PAD]


def _conv_fwd1(u, m_fwd):
    n, seq, c = u.shape
    a_half = FFT_A // 2
    return pl.pallas_call(
        _fwd1_body,
        out_shape=jax.ShapeDtypeStruct((n, 2, FFT_KA_PAD, FFT_R * c), F32),
        grid=(n, FFT_R // FFT_NB),
        in_specs=[pl.BlockSpec(m_fwd.shape, lambda i, j: (0, 0)),
                  pl.BlockSpec((1, a_half, FFT_NB, c), lambda i, j: (i, 0, j, 0))],
        out_specs=pl.BlockSpec((1, 2, FFT_KA_PAD, FFT_NB * c), lambda i, j: (i, 0, 0, j)),
        compiler_params=_cparams(("parallel", "parallel")),
        name="conv_fwd1",
    )(m_fwd, u.reshape(n, a_half, FFT_R, c))


FFT_KB = 8


def _rows_to_slabs(src_ref, dst_scr, c):
    for part in range(2):
        for b in range(FFT_R):
            dst_scr[part, :, b, :] = src_ref[0, part, :, b * c:(b + 1) * c]


def _slabs_to_rows(src_scr, dst_ref, c):
    for part in range(2):
        for b in range(FFT_R):
            dst_ref[0, part, :, b * c:(b + 1) * c] = src_scr[part, :, b, :]


def _slab(scr, i):
    return jnp.concatenate([scr[0, i], scr[1, i]], axis=0).astype(BF16)


def _fwd2f_body(sf_ref, sb_ref, g_ref, l1_ref, kf_ref, f3, b3):
    o = pl.program_id(0)
    j = pl.program_id(1)
    r2 = 2 * FFT_R
    c = kf_ref.shape[-1]
    _rows_to_slabs(sf_ref, f3, c)
    _rows_to_slabs(sb_ref, b3, c)
    inv = 1.0 / (l1_ref[2 * o] + l1_ref[2 * o + 1] + EPS)
    for i in range(FFT_KB):
        @pl.when(j * FFT_KB + i < FFT_KA)
        def _():
            xf = _dot(g_ref[i], _slab(f3, i))
            xb = _dot(g_ref[i], _slab(b3, i))
            kf_ref[0, i, 0:FFT_R] = (xf[0:FFT_R] + xb[0:FFT_R]) * inv
            kf_ref[0, i, FFT_R:r2] = (xf[FFT_R:r2] - xb[FFT_R:r2]) * inv

        @pl.when(j * FFT_KB + i >= FFT_KA)
        def _():
            kf_ref[0, i] = jnp.zeros((r2, c), F32)


def _filter_spectrum(s_filt, l1, g2, c):
    n_ord = s_filt.shape[0] // 2
    cols = s_filt.shape[-1]
    r2 = 2 * FFT_R
    return pl.pallas_call(
        _fwd2f_body,
        out_shape=jax.ShapeDtypeStruct((n_ord, FFT_KA_PAD, r2, c), F32),
        grid=(n_ord, FFT_KA_PAD // FFT_KB),
        in_specs=[pl.BlockSpec((1, 2, FFT_KB, cols), lambda o, j: (2 * o, 0, j, 0)),
                  pl.BlockSpec((1, 2, FFT_KB, cols), lambda o, j: (2 * o + 1, 0, j, 0)),
                  pl.BlockSpec((FFT_KB, r2, r2), lambda o, j: (j, 0, 0)),
                  pl.BlockSpec(l1.shape, lambda o, j: (0, 0, 0))],
        out_specs=pl.BlockSpec((1, FFT_KB, r2, c), lambda o, j: (o, j, 0, 0)),
        scratch_shapes=[pltpu.VMEM((2, FFT_KB, FFT_R, c), F32)] * 2,
        compiler_params=_cparams(("parallel", "parallel")),
        name="filter_spectrum",
    )(s_filt, s_filt, g2, l1)


def _mid_body(s_ref, g_ref, gh_ref, kf_ref, t_ref, s3, t3):
    j = pl.program_id(1)
    r2 = 2 * FFT_R
    c = kf_ref.shape[-1]
    _rows_to_slabs(s_ref, s3, c)
    for i in range(FFT_KB):
        @pl.when(j * FFT_KB + i < FFT_KA)
        def _():
            x = _dot(g_ref[i], _slab(s3, i))
            xr, xi = x[0:FFT_R], x[FFT_R:r2]
            kr, ki = kf_ref[0, i, 0:FFT_R], kf_ref[0, i, FFT_R:r2]
            y = jnp.concatenate([xr * kr - xi * ki, xr * ki + xi * kr], axis=0).astype(BF16)
            t = _dot(gh_ref[i], y)
            t3[0, i] = t[0:FFT_R]
            t3[1, i] = t[FFT_R:r2]

        @pl.when(j * FFT_KB + i >= FFT_KA)
        def _():
            t3[0, i] = jnp.zeros((FFT_R, c), F32)
            t3[1, i] = jnp.zeros((FFT_R, c), F32)

    _slabs_to_rows(t3, t_ref, c)


def _conv_mid(s, kf, order, g2, g2h, c):
    n, _, _, cols = s.shape
    r2 = 2 * FFT_R
    blk = pl.BlockSpec((1, 2, FFT_KB, cols), lambda i, j: (i, 0, j, 0))
    gspec = pl.BlockSpec((FFT_KB, r2, r2), lambda i, j: (j, 0, 0))
    return pl.pallas_call(
        _mid_body,
        out_shape=jax.ShapeDtypeStruct(s.shape, F32),
        grid=(n, FFT_KA_PAD // FFT_KB),
        in_specs=[blk, gspec, gspec,
                  pl.BlockSpec((1, FFT_KB, r2, c), lambda i, j: (order, j, 0, 0))],
        out_specs=blk,
        scratch_shapes=[pltpu.VMEM((2, FFT_KB, FFT_R, c), F32)] * 2,
        compiler_params=_cparams(("parallel", "parallel")),
        name="conv_mid",
    )(s, g2, g2h, kf)


def _inv1_body(m_ref, t_ref, u_ref, xg_ref, sk_ref, o_ref):
    c = u_ref.shape[-1]
    t2 = t_ref[0].reshape(2 * FFT_KA_PAD, FFT_NB * c).astype(BF16)
    y = _dot(m_ref[...], t2)
    for bb in range(FFT_NB):
        conv = y[:, bb * c:(bb + 1) * c] + u_ref[0, :, bb, :] * sk_ref[...]
        o_ref[0, :, bb, :] = xg_ref[0, :, bb, :] * conv


def _conv_inv1(t, u, xg, skip, m_inv):
    n, seq, c = u.shape
    a_half = FFT_A // 2
    sk = skip.astype(F32).reshape(1, c)
    uspec = pl.BlockSpec((1, a_half, FFT_NB, c), lambda i, j: (i, 0, j, 0))
    view = lambda a: a.reshape(n, a_half, FFT_R, c)
    out = pl.pallas_call(
        _inv1_body,
        out_shape=jax.ShapeDtypeStruct((n, a_half, FFT_R, c), F32),
        grid=(n, FFT_R // FFT_NB),
        in_specs=[pl.BlockSpec(m_inv.shape, lambda i, j: (0, 0)),
                  pl.BlockSpec((1, 2, FFT_KA_PAD, FFT_NB * c), lambda i, j: (i, 0, 0, j)),
                  uspec, uspec,
                  pl.BlockSpec((1, c), lambda i, j: (0, 0))],
        out_specs=uspec,
        compiler_params=_cparams(("parallel", "parallel")),
        name="conv_inv1",
    )(m_inv, t, view(u), view(xg), sk)
    return out.reshape(n, seq, c)


def _hyena(v, x1, x2, f_w1, f_b1, f_w2, f_b2, f_w3, f_freq, skip):
    _, seq, c = v.shape
    assert 2 * seq == FFT_A * FFT_R
    m_fwd, m_inv, g2, g2h = _conv_dft_constants()
    taps, l1 = _hyena_filter_taps(seq, f_w1, f_b1, f_w2, f_b2, f_w3, f_freq, c)
    kf = _filter_spectrum(_conv_fwd1(taps, m_fwd), l1, g2, c)
    y = v
    for order, xg in enumerate((x1, x2)):
        t = _conv_mid(_conv_fwd1(y, m_fwd), kf, order, g2, g2h, c)
        y = _conv_inv1(t, y, xg, skip[order], m_inv)
    return y


def _mix_ffn_body(x_ref, a1_ref, a2_ref, wm_ref, gm_ref, g_ref, sh_ref, sc_ref, gt_ref,
                  w1_ref, w3_ref, w2_ref, o_ref, x_scr, h_scr, acc_scr):
    j = pl.program_id(2)

    @pl.when(j == 0)
    def _():
        c1 = a1_ref.shape[-1]
        mixed = _dot(a1_ref[0].astype(BF16), wm_ref[0:c1]) + _dot(a2_ref[0].astype(BF16), wm_ref[c1:])
        xm = x_ref[0] + gm_ref[0] * mixed
        x_scr[...] = xm
        h_scr[...] = _norm_mod(xm, g_ref[...], sh_ref[0], sc_ref[0]).astype(BF16)
        acc_scr[...] = jnp.zeros_like(acc_scr)

    h = h_scr[...]
    a = _dot(h, w1_ref[...])
    u = (a * jax.nn.sigmoid(a) * _dot(h, w3_ref[...])).astype(BF16)
    acc_scr[...] += _dot(u, w2_ref[...])

    @pl.when(j == pl.num_programs(2) - 1)
    def _():
        o_ref[0] = x_scr[...] + gt_ref[0] * acc_scr[...]


def _mix_ffn(x, a1, a2, w_mix, gate_mix, g, shift, scale, gate, w1, w3, w2, tm=512, fb=1408):
    b, s, d = x.shape
    f = w1.shape[1]
    tok = lambda c: pl.BlockSpec((1, tm, c), lambda bi, i, j: (bi, i, 0))
    per_b = pl.BlockSpec((1, 1, d), lambda bi, i, j: (bi, 0, 0))
    const = lambda a: pl.BlockSpec(a.shape, lambda bi, i, j: (0, 0))
    return pl.pallas_call(
        _mix_ffn_body,
        out_shape=jax.ShapeDtypeStruct(x.shape, F32),
        grid=(b, s // tm, f // fb),
        in_specs=[tok(d), tok(a1.shape[-1]), tok(a2.shape[-1]), const(w_mix), per_b,
                  const(g), per_b, per_b, per_b,
                  pl.BlockSpec((d, fb), lambda bi, i, j: (0, j)),
                  pl.BlockSpec((d, fb), lambda bi, i, j: (0, j)),
                  pl.BlockSpec((fb, d), lambda bi, i, j: (j, 0))],
        out_specs=tok(d),
        scratch_shapes=[pltpu.VMEM((tm, d), F32), pltpu.VMEM((tm, d), BF16), pltpu.VMEM((tm, d), F32)],
        compiler_params=_cparams(("parallel", "parallel", "arbitrary")),
        name="mix_ffn",
    )(x, a1, a2, w_mix, gate_mix, g, shift, scale, gate, w1, w3, w2)


def _fm_constants(cg):
    j = np.arange(cg)[:, None]
    m = np.arange(cg)[None, :]
    ph = 2.0 * np.pi * j * m / cg
    w_cs = np.concatenate([np.cos(ph), np.sin(ph)], axis=1)
    d = np.arange(FM_A)[:, None]
    a = np.arange(FM_A)[None, :]
    ph = 2.0 * np.pi * d * a / FM_A
    fr, fi = np.cos(ph), -np.sin(ph)
    m1 = np.block([[fr, fi], [fi, -fr]])
    n = FM_A * FM_A
    dd = np.arange(FM_A)[:, None, None]
    c = np.arange(FM_A)[None, :, None]
    b = np.arange(FM_A)[None, None, :]
    th = 2.0 * np.pi * (b * c / FM_A + b * dd / n)
    gcat = np.concatenate([np.cos(th), np.sin(th)], axis=2)
    return _mxu_const(w_cs), _mxu_const(m1), _mxu_const(gcat)


def _fm_front_body(x_ref, g_ref, sh_ref, sc_ref, w_ref, m_ref, o_ref, *, cg, nb):
    d = x_ref.shape[-1]
    xs = jnp.concatenate([x_ref[0, :, bb, :] for bb in range(nb)], axis=0)
    h = _norm_mod(xs, g_ref[...], sh_ref[0], sc_ref[0]).astype(BF16)
    pq = [_dot(h[:, grp * cg:(grp + 1) * cg], w_ref[...]) for grp in range(d // cg)]
    p = jnp.concatenate([t[:, 0:cg] for t in pq], axis=1)
    q = jnp.concatenate([t[:, cg:2 * cg] for t in pq], axis=1)
    for bb in range(nb):
        rows = slice(bb * FM_A, (bb + 1) * FM_A)
        res = _dot(m_ref[...], jnp.concatenate([p[rows], q[rows]], axis=0).astype(BF16))
        o_ref[0, 0, :, bb, :] = res[0:FM_A]
        o_ref[0, 1, :, bb, :] = res[FM_A:2 * FM_A]


def _fm_front(x, g, shift, scale, w_cs, m1, nb=8):
    b, s, d = x.shape
    cg = w_cs.shape[0]
    per_b = pl.BlockSpec((1, 1, d), lambda bi, j: (bi, 0, 0))
    const = lambda a: pl.BlockSpec(a.shape, lambda bi, j: (0, 0))
    return pl.pallas_call(
        functools.partial(_fm_front_body, cg=cg, nb=nb),
        out_shape=jax.ShapeDtypeStruct((b, 2, FM_A, s // FM_A, d), F32),
        grid=(b, s // FM_A // nb),
        in_specs=[pl.BlockSpec((1, FM_A, nb, d), lambda bi, j: (bi, 0, j, 0)),
                  const(g), per_b, per_b, const(w_cs), const(m1)],
        out_specs=pl.BlockSpec((1, 2, FM_A, nb, d), lambda bi, j: (bi, 0, 0, j, 0)),
        compiler_params=_cparams(("parallel", "parallel")),
        name="fm_front",
    )(x.reshape(b, FM_A, s // FM_A, d), g, shift, scale, w_cs, m1)


def _fm_s2_body(s_ref, g_ref, o_ref, *, dblk, scale):
    for i in range(dblk):
        s2 = jnp.concatenate([s_ref[0, 0, i], s_ref[0, 1, i]], axis=0).astype(BF16)
        o_ref[:, i, :] = _dot(g_ref[i], s2) * scale


def _fm_stage2(sv, gcat, seq, d, dblk=8):
    b = sv.shape[0]
    scale = 1.0 / math.sqrt(seq * (d // F_GROUPS))
    out = pl.pallas_call(
        functools.partial(_fm_s2_body, dblk=dblk, scale=scale),
        out_shape=jax.ShapeDtypeStruct((b * FM_A, dblk * (FM_A // dblk), d), F32),
        grid=(b, FM_A // dblk),
        in_specs=[pl.BlockSpec((1, 2, dblk, FM_A, d), lambda bi, j: (bi, 0, j, 0, 0)),
                  pl.BlockSpec((dblk, FM_A, 2 * FM_A), lambda bi, j: (j, 0, 0))],
        out_specs=pl.BlockSpec((FM_A, dblk, d), lambda bi, j: (bi, j, 0)),
        compiler_params=_cparams(("parallel", "parallel")),
        name="fm_stage2",
    )(sv, gcat)
    return out.reshape(b, seq, d)


def _fourier_mix(x, g, shift, scale):
    b, s, d = x.shape
    assert s == FM_A * FM_A
    w_cs, m1, gcat = _fm_constants(d // F_GROUPS)
    return _fm_stage2(_fm_front(x, g, shift, scale, w_cs, m1), gcat, s, d)


LANES = 128
ROW_SL = 8
MOE_TM = 1024
DMA_WINDOW = 128


def _router_body(x_ref, yf_ref, wf_ref, gf_ref, g_ref, sh_ref, sc_ref, wr_ref, br_ref,
                 xo_ref, h_ref, meta_ref, gw_ref, cnt_ref, carry):
    i = pl.program_id(0)

    @pl.when(i == 0)
    def _():
        carry[...] = jnp.zeros_like(carry)

    xm = x_ref[...] + gf_ref[0] * _dot(yf_ref[...].astype(BF16), wf_ref[...])
    xo_ref[...] = xm
    h = _norm_mod(xm, g_ref[...], sh_ref[0], sc_ref[0])
    _rows_to_tiles(h_ref, h)
    h_hi = h.astype(BF16)
    h_lo = (h - h_hi.astype(F32)).astype(BF16)
    by_hi = _dot(h_hi, wr_ref[...])
    logits = by_hi[:, 0:LANES] + by_hi[:, LANES:] + _dot(h_lo, wr_ref[:, 0:LANES]) + br_ref[...]
    lane = lax.broadcasted_iota(jnp.int32, logits.shape, 1)
    nl = logits.shape[-1]
    m1 = jnp.max(logits, axis=-1, keepdims=True)
    i1 = jnp.min(jnp.where(logits == m1, lane, nl), axis=-1, keepdims=True)
    rest = jnp.where(lane == i1, -3.0e38, logits)
    m2 = jnp.max(rest, axis=-1, keepdims=True)
    i2 = jnp.min(jnp.where(rest == m2, lane, nl), axis=-1, keepdims=True)
    e = jnp.exp(m2 - m1)
    gw_ref[...] = jnp.where(lane == 0, 1.0 / (1.0 + e), jnp.where(lane == 1, e / (1.0 + e), 0.0))
    onehot = jnp.where((lane == i1) | (lane == i2), 1.0, 0.0)
    tm = onehot.shape[0]
    earlier = lax.broadcasted_iota(jnp.int32, (tm, tm), 0) > lax.broadcasted_iota(jnp.int32, (tm, tm), 1)
    excl = _dot(jnp.where(earlier, 1.0, 0.0).astype(BF16), onehot.astype(BF16)) + carry[...]
    r1 = jnp.sum(jnp.where(lane == i1, excl, 0.0), axis=-1, keepdims=True).astype(jnp.int32)
    r2 = jnp.sum(jnp.where(lane == i2, excl, 0.0), axis=-1, keepdims=True).astype(jnp.int32)
    meta_ref[...] = jnp.where(lane == 0, i1, jnp.where(lane == 1, i2, jnp.where(lane == 2, r1, jnp.where(lane == 3, r2, 0))))
    carry[...] = carry[...] + jnp.sum(onehot, axis=0, keepdims=True)
    cnt_ref[...] = carry[...]


def _router(x, y_f, w_f, gate_f, g, shift, scale, w_router, b_router, tm=512):
    b, s, d = x.shape
    t = b * s
    ne = w_router.shape[1]
    wr = jnp.pad(w_router.astype(F32), ((0, 0), (0, LANES - ne)))
    wr_hi = wr.astype(BF16)
    wr = jnp.concatenate([wr_hi, (wr - wr_hi.astype(F32)).astype(BF16)], axis=1)
    br = jnp.pad(b_router.astype(F32).reshape(1, ne), ((0, 0), (0, LANES - ne)), constant_values=NEG_INF)
    spt = s // tm
    per_b = pl.BlockSpec((1, 1, d), lambda i: (i // spt, 0, 0))
    const = lambda a: pl.BlockSpec(a.shape, lambda i: (0, 0))
    tok = pl.BlockSpec((tm, d), lambda i: (i, 0))
    return pl.pallas_call(
        _router_body,
        out_shape=[jax.ShapeDtypeStruct((t, d), F32),
                   jax.ShapeDtypeStruct((t * ROW_SL, LANES), F32),
                   jax.ShapeDtypeStruct((t, LANES), jnp.int32),
                   jax.ShapeDtypeStruct((t, LANES), F32),
                   jax.ShapeDtypeStruct((1, LANES), F32)],
        grid=(t // tm,),
        in_specs=[tok, tok, const(w_f), per_b, const(g), per_b, per_b, const(wr), const(br)],
        out_specs=[tok,
                   pl.BlockSpec((tm * ROW_SL, LANES), lambda i: (i, 0)),
                   pl.BlockSpec((tm, LANES), lambda i: (i, 0)),
                   pl.BlockSpec((tm, LANES), lambda i: (i, 0)),
                   pl.BlockSpec((1, LANES), lambda i: (0, 0))],
        scratch_shapes=[pltpu.VMEM((1, LANES), F32)],
        compiler_params=_cparams(("arbitrary",)),
        name="router",
    )(x.reshape(t, d), y_f.reshape(t, d), w_f, gate_f, g, shift, scale, wr, br)


def _moe_plan(meta, counts, ne, tm):
    i1, i2, r1, r2 = meta[:, 0], meta[:, 1], meta[:, 2], meta[:, 3]
    cnt = counts[0, :ne].astype(jnp.int32)
    padded = ((cnt + tm - 1) // tm) * tm
    ends = jnp.cumsum(padded)
    offs = ends - padded
    pick = lambda idx: sum(jnp.where(idx == e, offs[e], 0) for e in range(ne))
    pos = jnp.concatenate([pick(i1) + r1, pick(i2) + r2]).astype(jnp.int32)
    n_tiles = (2 * meta.shape[0]) // tm + ne
    n_used = (ends[ne - 1] // tm).astype(jnp.int32)
    tile_start = jnp.minimum(jnp.arange(n_tiles, dtype=jnp.int32), n_used - 1) * tm
    tile_expert = jnp.sum(tile_start[:, None] >= ends[None, :], axis=1).astype(jnp.int32)
    group_end = sum(jnp.where(tile_expert == e, offs[e] + cnt[e], 0) for e in range(ne))
    tile_rows = jnp.clip(group_end - tile_start, 0, tm).astype(jnp.int32)
    return pos, offs + cnt, padded - cnt, tile_expert, n_used.reshape(1), tile_rows


def _windowed_copies(n, start_copy, wait_one, per_iter):
    def body(i, carry):
        @pl.when(i >= DMA_WINDOW)
        def _():
            for _ in range(per_iter):
                wait_one()
        start_copy(i)
        return carry

    lax.fori_loop(0, n, body, 0)

    def drain(i, carry):
        for _ in range(per_iter):
            wait_one()
        return carry

    lax.fori_loop(0, jnp.minimum(n, DMA_WINDOW), drain, 0)


def _tile_of(ref, row):
    return ref.at[pl.ds(pl.multiple_of(row * ROW_SL, ROW_SL), ROW_SL)]


def _tiles_to_rows(ref, n, first=0):
    return jnp.concatenate([ref[pl.ds(first * ROW_SL + sl, n, stride=ROW_SL), :] for sl in range(ROW_SL)], axis=1)


def _rows_to_tiles(ref, val):
    n = val.shape[0]
    for sl in range(ROW_SL):
        ref[pl.ds(sl, n, stride=ROW_SL), :] = val[:, sl * LANES:(sl + 1) * LANES]


def _dispatch_body(pos_ref, pad_start_ref, pad_n_ref, h_ref, xs_hbm, sem, *, n_tok, ne):
    i = pl.program_id(0)
    td = h_ref.shape[0] // ROW_SL
    base = i * td
    copy = lambda src, dst: pltpu.make_async_copy(_tile_of(h_ref, src), _tile_of(xs_hbm, dst), sem)
    wait_one = lambda: copy(0, 0).wait()

    def start_token(r, carry):
        copy(r, pos_ref[base + r]).start()
        copy(r, pos_ref[n_tok + base + r]).start()
        return carry

    lax.fori_loop(0, td, start_token, 0, unroll=8)
    whole_tile = pltpu.make_async_copy(h_ref, xs_hbm.at[pl.ds(0, td * ROW_SL)], sem)
    whole_tile.wait()
    whole_tile.wait()

    @pl.when(i == 0)
    def _():
        for e in range(ne):
            first = pad_start_ref[e]
            _windowed_copies(pad_n_ref[e], lambda r: copy(0, first + r).start(), wait_one, 1)


def _moe_dispatch(h3, pos, pad_start, pad_n, n_rows, td=1024):
    n_tok = h3.shape[0] // ROW_SL
    ne = pad_start.shape[0]
    return pl.pallas_call(
        functools.partial(_dispatch_body, n_tok=n_tok, ne=ne),
        out_shape=jax.ShapeDtypeStruct((n_rows * ROW_SL, LANES), h3.dtype),
        grid_spec=pltpu.PrefetchScalarGridSpec(
            num_scalar_prefetch=3, grid=(n_tok // td,),
            in_specs=[pl.BlockSpec((td * ROW_SL, LANES), lambda i, p, ps, pn: (i, 0))],
            out_specs=pl.BlockSpec(memory_space=pl.ANY),
            scratch_shapes=[pltpu.SemaphoreType.DMA(())]),
        compiler_params=_cparams(("arbitrary",)),
        name="moe_dispatch",
    )(pos, pad_start, pad_n, h3)


def _moe_grouped_body(te_ref, nu_ref, tr_ref, xs_ref, w1_ref, w3_ref, w2_ref, y_ref, xb_scr, acc_scr):
    i = pl.program_id(0)
    j = pl.program_id(1)
    tm = xb_scr.shape[0]
    hm = tm // 2

    def expert_rows(nrows):
        h = xb_scr[0:nrows]
        a = _dot(h, w1_ref[0].astype(BF16))
        u = (a * jax.nn.sigmoid(a) * _dot(h, w3_ref[0].astype(BF16))).astype(BF16)
        part = _dot(u, w2_ref[0].astype(BF16))

        @pl.when(j == 0)
        def _():
            acc_scr[0:nrows] = part

        @pl.when(j > 0)
        def _():
            acc_scr[0:nrows] += part

    @pl.when(i < nu_ref[0])
    def _():
        @pl.when(j == 0)
        def _():
            xb_scr[...] = _tiles_to_rows(xs_ref, tm).astype(BF16)

        @pl.when(tr_ref[i] > hm)
        def _():
            expert_rows(tm)

        @pl.when(tr_ref[i] <= hm)
        def _():
            expert_rows(hm)

            @pl.when(j == 0)
            def _():
                acc_scr[hm:tm] = jnp.zeros((tm - hm, acc_scr.shape[1]), F32)

        @pl.when(j == pl.num_programs(1) - 1)
        def _():
            _rows_to_tiles(y_ref, acc_scr[...])


def _moe_grouped(xs, tile_expert, n_used, tile_rows, w1, w3, w2, tm, fb=512):
    ne, d, f = w1.shape
    n_rows = xs.shape[0] // ROW_SL
    nj = f // fb
    row_tile = lambda i, j, te, nu, tr: (jnp.maximum(jnp.minimum(i, nu[0] - 1), 0), 0)
    jj = lambda i, j, nu: jnp.where(i < nu[0], j, nj - 1)
    return pl.pallas_call(
        _moe_grouped_body,
        out_shape=jax.ShapeDtypeStruct(xs.shape, F32),
        grid_spec=pltpu.PrefetchScalarGridSpec(
            num_scalar_prefetch=3, grid=(n_rows // tm, nj),
            in_specs=[pl.BlockSpec((tm * ROW_SL, LANES), row_tile),
                      pl.BlockSpec((1, d, fb), lambda i, j, te, nu, tr: (te[i], 0, jj(i, j, nu))),
                      pl.BlockSpec((1, d, fb), lambda i, j, te, nu, tr: (te[i], 0, jj(i, j, nu))),
                      pl.BlockSpec((1, fb, d), lambda i, j, te, nu, tr: (te[i], jj(i, j, nu), 0))],
            out_specs=pl.BlockSpec((tm * ROW_SL, LANES), row_tile),
            scratch_shapes=[pltpu.VMEM((tm, d), BF16), pltpu.VMEM((tm, d), F32)]),
        compiler_params=_cparams(("arbitrary", "arbitrary")),
        name="moe_grouped",
    )(tile_expert, n_used, tile_rows, xs, w1, w3, w2)


def _moe_final_body(pos_ref, x_ref, y_hbm, gw_ref, gt_ref, fg_ref, o_ref, yg_scr, sem, *, n_tok):
    i = pl.program_id(0)
    tc = x_ref.shape[0]
    slot = i % 2

    def gather_tile(step, into):
        base = step * tc

        def start_token(r, carry):
            dst = yg_scr.at[into]
            pltpu.make_async_copy(_tile_of(y_hbm, pos_ref[base + r]), _tile_of(dst, r), sem.at[into]).start()
            pltpu.make_async_copy(_tile_of(y_hbm, pos_ref[n_tok + base + r]), _tile_of(dst, tc + r),
                                  sem.at[into]).start()
            return carry

        lax.fori_loop(0, tc, start_token, 0, unroll=8)

    @pl.when(i == 0)
    def _():
        gather_tile(0, 0)

    @pl.when(i + 1 < pl.num_programs(0))
    def _():
        gather_tile(i + 1, 1 - slot)

    pltpu.make_async_copy(y_hbm.at[pl.ds(0, 2 * tc * ROW_SL)], yg_scr.at[slot], sem.at[slot]).wait()
    gw = gw_ref[...]
    rows = yg_scr.at[slot]
    y = gw[:, 0:1] * _tiles_to_rows(rows, tc) + gw[:, 1:2] * _tiles_to_rows(rows, tc, first=tc)
    xo = x_ref[...] + gt_ref[0] * y
    ms = jnp.mean(xo * xo, axis=-1, keepdims=True)
    o_ref[...] = xo * lax.rsqrt(ms + EPS) * fg_ref[...]


def _moe_final(x, y, pos, gw, gt, final_g, tc=256):
    b, s, d = x.shape
    t = b * s
    spt = s // tc
    nsl = d // LANES
    out = pl.pallas_call(
        functools.partial(_moe_final_body, n_tok=t),
        out_shape=jax.ShapeDtypeStruct((t, d), F32),
        grid_spec=pltpu.PrefetchScalarGridSpec(
            num_scalar_prefetch=1, grid=(t // tc,),
            in_specs=[pl.BlockSpec((tc, d), lambda i, p: (i, 0)),
                      pl.BlockSpec(memory_space=pl.ANY),
                      pl.BlockSpec((tc, LANES), lambda i, p: (i, 0)),
                      pl.BlockSpec((1, 1, d), lambda i, p: (i // spt, 0, 0)),
                      pl.BlockSpec(final_g.shape, lambda i, p: (0, 0))],
            out_specs=pl.BlockSpec((tc, d), lambda i, p: (i, 0)),
            scratch_shapes=[pltpu.VMEM((2, 2 * tc * ROW_SL, LANES), F32), pltpu.SemaphoreType.DMA((2,))]),
        compiler_params=_cparams(("arbitrary",)),
        name="moe_final",
    )(pos, x.reshape(t, d), y, gw, gt, final_g)
    return out.reshape(b, s, d)


def _moe_routed(x, y_f, w_f, gate_f, g, shift, scale, gt, final_g, w_router, b_router, w1, w3, w2):
    ne = w1.shape[0]
    tm = MOE_TM
    x1, h3, meta, gw, counts = _router(x, y_f, w_f, gate_f, g, shift, scale, w_router, b_router)
    pos, pad_start, pad_n, tile_expert, n_used, tile_rows = _moe_plan(meta, counts, ne, tm)
    assert x.shape[-1] == ROW_SL * LANES
    n_rows = (2 * (h3.shape[0] // ROW_SL) // tm + ne) * tm
    xs = _moe_dispatch(h3, pos, pad_start, pad_n, n_rows)
    y = _moe_grouped(xs, tile_expert, n_used, tile_rows, w1, w3, w2, tm)
    return _moe_final(x1.reshape(x.shape), y, pos, gw, gt, final_g)


def kernel(x, c, ctx, c_ctx, w_ada, b_ada, norm_g, w_in, hy_short_w, hy_short_b, hy_f_w1, hy_f_b1, hy_f_w2, hy_f_b2, hy_f_w3, hy_f_freq, hy_skip, na_rpb, w_mix_out, ffn_w1, ffn_w3, ffn_w2, w_fourier, w_router, b_router, moe_w1, moe_w3, moe_w2, final_g):
    b, s, d = x.shape
    depth = w_ada.shape[0]
    assert depth == 2, "layer 0 mixes with Hyena/attention, layer 1 with Fourier/MoE"
    c_hy = hy_skip.shape[-1]
    c_na = d - c_hy

    cvec = jnp.concatenate([c, c_ctx[None, :], jnp.zeros((8 - b - 1, d), F32)], axis=0)
    mods = _ada(cvec, w_ada, b_ada)

    def mod(layer, idx, ctx_row=False):
        m = mods[layer, :, idx * d:(idx + 1) * d]
        return m[b:b + 1, None, :] if ctx_row else m[0:b, None, :]

    row = lambda a: a.reshape(1, -1)

    w_in0 = w_in[0].astype(BF16)
    w_hy, w_qkv = w_in0[:, 0:3 * c_hy], w_in0[:, 3 * c_hy:]
    v, x1, x2, q, k, va = _inproj(x, row(norm_g[0, 0]), mod(0, 0), mod(0, 1), w_hy, w_qkv,
                                  hy_short_w[0], row(hy_short_b[0]))
    kc, vc = _ctxkv(ctx, row(norm_g[0, 0]), mod(0, 0, True), mod(0, 1, True), w_qkv[:, c_na:])
    y_na = _natt(q, k, va, kc, vc, _na_bias_table(na_rpb[0]))
    y_hy = _hyena(v, x1, x2, hy_f_w1[0], hy_f_b1[0], hy_f_w2[0], hy_f_b2[0], hy_f_w3[0],
                  hy_f_freq[0], hy_skip[0])
    x = _mix_ffn(x, y_hy, y_na, w_mix_out[0].astype(BF16), mod(0, 2),
                 row(norm_g[0, 1]), mod(0, 3), mod(0, 4), mod(0, 5),
                 ffn_w1[0].astype(BF16), ffn_w3[0].astype(BF16), ffn_w2[0].astype(BF16))

    y_f = _fourier_mix(x, row(norm_g[1, 0]), mod(1, 0), mod(1, 1))
    return _moe_routed(x, y_f, w_fourier[0].astype(BF16), mod(1, 2),
                       row(norm_g[1, 1]), mod(1, 3), mod(1, 4), mod(1, 5), row(final_g),
                       w_router[0], b_router[0],
                       moe_w1[0], moe_w3[0], moe_w2[0])
```

```python
import functools
import math

import numpy as np
import jax
import jax.numpy as jnp
from jax import lax
from jax.experimental import pallas as pl
from jax.experimental.pallas import tpu as pltpu

F32 = jnp.float32
BF16 = jnp.bfloat16
HIGHEST = lax.Precision.HIGHEST

GRID_W = 64
NA_HEAD_DIM = 32
NA_WIN_R = 8
NA_WIN_C = 16
HYENA_EMB = 33
HYENA_BANDS = (HYENA_EMB - 1) // 2
HYENA_FAST_DECAY = 0.3
HYENA_SLOW_DECAY = 1.5
HYENA_TARGET = 1e-2
F_GROUPS = 4
N_MOD = 6
EPS = 1e-6
NEG_INF = -1e30

FFT_A = 64
FFT_R = 128
FFT_KA = FFT_A // 2 + 1
FFT_KA_PAD = 40
FM_A = 64

VMEM_LIMIT = 48 * 1024 * 1024


def _cparams(sem, vmem_limit=VMEM_LIMIT):
    return pltpu.CompilerParams(dimension_semantics=sem, vmem_limit_bytes=vmem_limit)


def _dot(a, b):
    return jnp.dot(a, b, preferred_element_type=F32)


def _mxu_const(m):
    return jnp.asarray(m, dtype=F32).astype(BF16)


def _norm_mod(x, g, shift, scale):
    ms = jnp.mean(x * x, axis=-1, keepdims=True)
    y = x * lax.rsqrt(ms + EPS) * g
    return y * (1.0 + scale) + shift


def _ada_body(c_ref, w_ref, b_ref, o_ref):
    cv = c_ref[...]
    s = cv * jax.nn.sigmoid(cv)
    o_ref[0] = jnp.dot(s, w_ref[0], precision=HIGHEST, preferred_element_type=F32) + b_ref[0]


def _ada(cvec, w_ada, b_ada):
    depth, d, n = w_ada.shape
    rows = cvec.shape[0]
    bn = n // 4
    return pl.pallas_call(
        _ada_body,
        out_shape=jax.ShapeDtypeStruct((depth, rows, n), F32),
        grid=(depth, n // bn),
        in_specs=[pl.BlockSpec((rows, d), lambda l, j: (0, 0)),
                  pl.BlockSpec((1, d, bn), lambda l, j: (l, 0, j)),
                  pl.BlockSpec((1, 1, bn), lambda l, j: (l, 0, j))],
        out_specs=pl.BlockSpec((1, rows, bn), lambda l, j: (l, 0, j)),
        compiler_params=_cparams(("parallel", "parallel")),
        name="ada",
    )(cvec, w_ada, b_ada.reshape(depth, 1, n))


def _inproj_body(x_ref, xp_ref, xn_ref, g_ref, sh_ref, sc_ref, why_ref, wqkv_ref, sw_ref, sb_ref,
                 v_ref, x1_ref, x2_ref, q_ref, k_ref, va_ref, *, n_tiles, q_scale, c_hy, c_na):
    i = pl.program_id(1)
    g, sh, sc = g_ref[...], sh_ref[0], sc_ref[0]
    hf = _norm_mod(x_ref[0], g, sh, sc)
    h = hf.astype(BF16)
    tm = hf.shape[0]
    hx = jnp.concatenate([_norm_mod(xp_ref[0], g, sh, sc), hf, _norm_mod(xn_ref[0], g, sh, sc)], axis=0)
    zx = _dot(hx.astype(BF16), why_ref[...])
    zh = zx[8:8 + tm]
    zp = jnp.where(i > 0, zx[7:8], 0.0)
    zn = jnp.where(i < n_tiles - 1, zx[8 + tm:9 + tm], 0.0)
    row = lax.broadcasted_iota(jnp.int32, zh.shape, 0)
    z_m1 = jnp.where(row == 0, zp, pltpu.roll(zh, 1, 0))
    z_p1 = jnp.where(row == tm - 1, zn, pltpu.roll(zh, tm - 1, 0))
    sw = sw_ref[...]
    zc = z_m1 * sw[0:1] + zh * sw[1:2] + z_p1 * sw[2:3] + sb_ref[...]
    v_ref[0] = zc[:, 0:c_hy]
    x1_ref[0] = zc[:, c_hy:2 * c_hy]
    x2_ref[0] = zc[:, 2 * c_hy:3 * c_hy]
    zq = _dot(h, wqkv_ref[...])
    q_ref[0] = (zq[:, 0:c_na] * q_scale).astype(BF16)
    k_ref[0] = zq[:, c_na:2 * c_na].astype(BF16)
    va_ref[0] = zq[:, 2 * c_na:3 * c_na].astype(BF16)


def _inproj(x, g, shift, scale, w_hy, w_qkv, short_w, short_b, tm=512):
    b, s, d = x.shape
    c_hy = w_hy.shape[1] // 3
    c_na = w_qkv.shape[1] // 3
    n_tiles = s // tm
    r8 = tm // 8
    body = functools.partial(_inproj_body, n_tiles=n_tiles, q_scale=NA_HEAD_DIM ** -0.5,
                             c_hy=c_hy, c_na=c_na)
    tok = lambda c: pl.BlockSpec((1, tm, c), lambda bi, i: (bi, i, 0))
    full2 = lambda a: pl.BlockSpec(a.shape, lambda bi, i: (0, 0))
    per_b = pl.BlockSpec((1, 1, d), lambda bi, i: (bi, 0, 0))
    return pl.pallas_call(
        body,
        out_shape=[jax.ShapeDtypeStruct((b, s, c_hy), F32)] * 3 + [jax.ShapeDtypeStruct((b, s, c_na), BF16)] * 3,
        grid=(b, n_tiles),
        in_specs=[tok(d),
                  pl.BlockSpec((1, 8, d), lambda bi, i: (bi, jnp.maximum(i * r8 - 1, 0), 0)),
                  pl.BlockSpec((1, 8, d), lambda bi, i: (bi, jnp.minimum((i + 1) * r8, s // 8 - 1), 0)),
                  full2(g), per_b, per_b, full2(w_hy), full2(w_qkv), full2(short_w), full2(short_b)],
        out_specs=[tok(c_hy)] * 3 + [tok(c_na)] * 3,
        compiler_params=_cparams(("parallel", "parallel")),
        name="inproj",
    )(x, x, x, g, shift, scale, w_hy, w_qkv, short_w, short_b)


def _ctxkv_body(x_ref, g_ref, sh_ref, sc_ref, w_ref, k_ref, v_ref, *, c_na):
    h = _norm_mod(x_ref[0], g_ref[...], sh_ref[0], sc_ref[0]).astype(BF16)
    z = _dot(h, w_ref[...])
    k_ref[0] = z[:, 0:c_na].astype(BF16)
    v_ref[0] = z[:, c_na:2 * c_na].astype(BF16)


def _ctxkv(ctx, g, shift, scale, w_kv):
    b, n, d = ctx.shape
    c_na = w_kv.shape[1] // 2
    one = pl.BlockSpec((1, 1, d), lambda bi: (0, 0, 0))
    return pl.pallas_call(
        functools.partial(_ctxkv_body, c_na=c_na),
        out_shape=[jax.ShapeDtypeStruct((b, n, c_na), BF16)] * 2,
        grid=(b,),
        in_specs=[pl.BlockSpec((1, n, d), lambda bi: (bi, 0, 0)),
                  pl.BlockSpec(g.shape, lambda bi: (0, 0)), one, one,
                  pl.BlockSpec(w_kv.shape, lambda bi: (0, 0))],
        out_specs=[pl.BlockSpec((1, n, c_na), lambda bi: (bi, 0, 0))] * 2,
        compiler_params=_cparams(("parallel",)),
        name="ctxkv",
    )(ctx, g, shift, scale, w_kv)


NA_HEADS_PER_BLK = 4
NA_ROWS_PER_STEP = 4


def _na_bias_body(r_ref, e_ref, ok_ref, o_ref):
    t = jnp.dot(r_ref[...], e_ref[...], precision=HIGHEST, preferred_element_type=F32)
    o_ref[...] = jnp.where(ok_ref[...] > 0.5, t, NEG_INF)


def _na_bias_table(rpb):
    w = GRID_W
    h, nr, nc = rpb.shape
    col = np.arange(w)[:, None]
    kc = np.arange(w)[None, :]
    c_start = np.clip(col - NA_WIN_C // 2, 0, w - NA_WIN_C)
    valid = ((kc >= c_start) & (kc < c_start + NA_WIN_C)).reshape(1, w * w)
    expand = (np.arange(32)[:, None, None] == (kc - col + NA_WIN_C - 1)[None]).reshape(32, w * w)
    rp = jnp.pad(rpb.astype(F32).reshape(h * nr, nc), ((0, 0), (0, 32 - nc)))
    full = lambda a: pl.BlockSpec(a.shape, lambda: (0,) * a.ndim)
    expand = jnp.asarray(expand, dtype=F32)
    ok = jnp.asarray(valid, dtype=F32)
    toep = pl.pallas_call(
        _na_bias_body,
        out_shape=jax.ShapeDtypeStruct((h * nr, w * w), F32),
        in_specs=[full(rp), full(expand), full(ok)],
        out_specs=pl.BlockSpec((h * nr, w * w), lambda: (0, 0)),
        name="na_bias",
    )(rp, expand, ok)
    t2 = toep.reshape(h, nr, w, w).transpose(0, 2, 1, 3).reshape(h, w, nr * w)
    slabs = jnp.stack([t2[:, :, (NA_WIN_R - 1 - off) * w:(2 * NA_WIN_R - 1 - off) * w]
                       for off in range(NA_WIN_R)], axis=1)
    hpb = NA_HEADS_PER_BLK
    slabs = slabs.reshape(h // hpb, hpb, NA_WIN_R, w, NA_WIN_R * w).transpose(0, 2, 1, 3, 4)
    return slabs.reshape(h // hpb, NA_WIN_R, hpb * w, NA_WIN_R * w)


def _natt_body(q_ref, k_ref, v_ref, kc_ref, vc_ref, bias_ref, o_ref, *, rows):
    w = GRID_W
    hpb = NA_HEADS_PER_BLK
    nloc = NA_WIN_R * w
    lane = lax.broadcasted_iota(jnp.int32, (1, hpb * NA_HEAD_DIM), 1)
    in_head = [(lane >= NA_HEAD_DIM * hh) & (lane < NA_HEAD_DIM * (hh + 1)) for hh in range(hpb)]
    kcx = kc_ref[0]
    vcx = vc_ref[0]
    nt = (((1,), (1,)), ((), ()))

    def one_row(r):
        r0 = jnp.clip(r - NA_WIN_R // 2, 0, rows - NA_WIN_R)
        off = r - r0
        qs = q_ref[0, pl.ds(pl.multiple_of(r * w, w), w), :]
        kw = k_ref[0, pl.ds(pl.multiple_of(r0 * w, w), nloc), :]
        vw = v_ref[0, pl.ds(pl.multiple_of(r0 * w, w), nloc), :]
        zero = jnp.zeros_like(qs)
        qst = jnp.concatenate([jnp.where(m, qs, zero) for m in in_head], axis=0)
        s_loc = lax.dot_general(qst, kw, nt, preferred_element_type=F32) + bias_ref[0, off]
        s_ctx = lax.dot_general(qst, kcx, nt, preferred_element_type=F32)
        m = jnp.maximum(jnp.max(s_loc, axis=-1, keepdims=True), jnp.max(s_ctx, axis=-1, keepdims=True))
        p_loc = jnp.exp(s_loc - m)
        p_ctx = jnp.exp(s_ctx - m)
        den = jnp.sum(p_loc, axis=-1, keepdims=True) + jnp.sum(p_ctx, axis=-1, keepdims=True)
        o = (_dot(p_loc.astype(BF16), vw) + _dot(p_ctx.astype(BF16), vcx)) * (1.0 / den)
        acc = jnp.where(in_head[0], o[0:w], 0.0)
        for hh in range(1, hpb):
            acc = acc + jnp.where(in_head[hh], o[hh * w:(hh + 1) * w], 0.0)
        o_ref[0, pl.ds(pl.multiple_of(r * w, w), w), :] = acc.astype(BF16)

    def row_group(i, carry):
        for r in range(NA_ROWS_PER_STEP):
            one_row(NA_ROWS_PER_STEP * i + r)
        return carry

    lax.fori_loop(0, rows // NA_ROWS_PER_STEP, row_group, 0)


def _natt(q, k, v, kc, vc, bias):
    b, s, c = q.shape
    nctx = kc.shape[1]
    lw = NA_HEADS_PER_BLK * NA_HEAD_DIM
    rows = s // GRID_W
    seq = pl.BlockSpec((1, s, lw), lambda bi, g: (bi, 0, g))
    cx = pl.BlockSpec((1, nctx, lw), lambda bi, g: (bi, 0, g))
    return pl.pallas_call(
        functools.partial(_natt_body, rows=rows),
        out_shape=jax.ShapeDtypeStruct((b, s, c), BF16),
        grid=(b, c // lw),
        in_specs=[seq, seq, seq, cx, cx,
                  pl.BlockSpec((1,) + bias.shape[1:], lambda bi, g: (g, 0, 0, 0))],
        out_specs=seq,
        compiler_params=_cparams(("parallel", "parallel")),
        name="natt",
    )(q, k, v, kc, vc, bias)


def _hyena_feats(seq_len):
    t = jnp.linspace(0.0, 1.0, seq_len, dtype=F32)[:, None]
    bands = jnp.linspace(1e-4, HYENA_BANDS - 1, HYENA_BANDS, dtype=F32)
    ang = (2.0 * math.pi / seq_len) * jnp.arange(seq_len, dtype=F32)[:, None] * bands[None, :]
    feats = jnp.concatenate([t, jnp.cos(ang), -jnp.sin(ang)], axis=-1)
    return jnp.pad(feats, ((0, 0), (0, 128 - HYENA_EMB)))


def _filt_body(feat_ref, w1_ref, b1_ref, w2_ref, b2_ref, w3_ref, fr_ref, dl_ref, o_ref, l1_ref, h_scr,
               *, halves):
    j = pl.program_id(0)
    hp = functools.partial(jnp.dot, precision=HIGHEST, preferred_element_type=F32)
    feats = feat_ref[...]

    @pl.when(j == 0)
    def _():
        fr = fr_ref[...]
        h = jnp.sin(fr[0:1] * (hp(feats, w1_ref[...]) + b1_ref[...]))
        h_scr[...] = jnp.sin(fr[1:2] * (hp(h, w2_ref[...]) + b2_ref[...]))

    hc = hp(h_scr[...], w3_ref[...])
    t = feats[:, 0:1]
    hc = hc * jnp.exp(-t * dl_ref[...])
    row = lax.broadcasted_iota(jnp.int32, hc.shape, 0)
    hc = jnp.where((row == 0) & ((j // halves) % 2 == 1), 0.0, hc)
    l1_ref[0] = jnp.sum(jnp.abs(hc), axis=0, keepdims=True)
    o_ref[0] = hc


def _hyena_filter_taps(seq_len, f_w1, f_b1, f_w2, f_b2, f_w3, f_freq, c_hy):
    feats = _hyena_feats(seq_len)
    hid = f_w1.shape[1]
    w1 = jnp.pad(f_w1.astype(F32), ((0, 128 - HYENA_EMB), (0, 0)))
    deltas = jnp.abs(jnp.linspace(math.log(HYENA_TARGET) / HYENA_SLOW_DECAY,
                                  math.log(HYENA_TARGET) / HYENA_FAST_DECAY, c_hy, dtype=F32))[None, :]
    nblk = f_w3.shape[1] // c_hy
    halves = 2
    cb = c_hy // halves
    c0 = lambda a: pl.BlockSpec(a.shape, lambda j: (0, 0))
    b1, b2 = f_b1.reshape(1, hid), f_b2.reshape(1, hid)
    return pl.pallas_call(
        functools.partial(_filt_body, halves=halves),
        out_shape=[jax.ShapeDtypeStruct((nblk, seq_len, c_hy), F32),
                   jax.ShapeDtypeStruct((nblk, 1, c_hy), F32)],
        grid=(nblk * halves,),
        in_specs=[c0(feats), c0(w1), c0(b1), c0(f_w2), c0(b2),
                  pl.BlockSpec((hid, cb), lambda j: (0, j)), c0(f_freq),
                  pl.BlockSpec((1, cb), lambda j: (0, j % halves))],
        out_specs=[pl.BlockSpec((1, seq_len, cb), lambda j: (j // halves, 0, j % halves)),
                   pl.BlockSpec((1, 1, cb), lambda j: (j // halves, 0, j % halves))],
        scratch_shapes=[pltpu.VMEM((seq_len, hid), F32)],
        compiler_params=_cparams(("arbitrary",)),
        name="hyena_filter",
    )(feats, w1, b1, f_w2, b2, f_w3, f_freq, deltas)


def _conv_dft_constants():
    a_half = FFT_A // 2
    n = FFT_A * FFT_R
    ka = np.arange(FFT_KA)[:, None]
    a = np.arange(a_half)[None, :]
    ph = 2.0 * np.pi * ka * a / FFT_A
    m_fwd = np.zeros((2 * FFT_KA_PAD, a_half))
    m_fwd[:FFT_KA] = np.cos(ph)
    m_fwd[FFT_KA_PAD:FFT_KA_PAD + FFT_KA] = -np.sin(ph)
    wgt = np.where((ka == 0) | (ka == FFT_A // 2), 1.0, 2.0)
    m_inv = np.zeros((a_half, 2 * FFT_KA_PAD))
    m_inv[:, :FFT_KA] = (wgt * np.cos(ph)).T / n
    m_inv[:, FFT_KA_PAD:FFT_KA_PAD + FFT_KA] = (-wgt * np.sin(ph)).T / n
    kb = np.arange(FFT_R)[None, :, None]
    b = np.arange(FFT_R)[None, None, :]
    kaa = np.arange(FFT_KA)[:, None, None]
    th = 2.0 * np.pi * (b * kb / FFT_R + b * kaa / n)
    gr, gi = np.cos(th), -np.sin(th)
    g2 = np.zeros((FFT_KA_PAD, 2 * FFT_R, 2 * FFT_R))
    g2[:FFT_KA] = np.block([[gr, -gi], [gi, gr]])
    grt, git = gr.transpose(0, 2, 1), gi.transpose(0, 2, 1)
    g2h = np.zeros_like(g2)
    g2h[:FFT_KA] = np.block([[grt, git], [-git, grt]])
    return _mxu_const(m_fwd), _mxu_const(m_inv), _mxu_const(g2), _mxu_const(g2h)


FFT_NB = 16


def _fwd1_body(m_ref, u_ref, o_ref):
    u = jnp.concatenate([u_ref[0, :, bb, :] for bb in range(FFT_NB)], axis=1).astype(BF16)
    res = _dot(m_ref[...], u)
    o_ref[0, 0] = res[0:FFT_KA_PAD]
    o_ref[0, 1] = res[FFT_KA_PAD:2 * FFT_KA_PAD]


def _conv_fwd1(u, m_fwd):
    n, seq, c = u.shape
    a_half = FFT_A // 2
    return pl.pallas_call(
        _fwd1_body,
        out_shape=jax.ShapeDtypeStruct((n, 2, FFT_KA_PAD, FFT_R * c), F32),
        grid=(n, FFT_R // FFT_NB),
        in_specs=[pl.BlockSpec(m_fwd.shape, lambda i, j: (0, 0)),
                  pl.BlockSpec((1, a_half, FFT_NB, c), lambda i, j: (i, 0, j, 0))],
        out_specs=pl.BlockSpec((1, 2, FFT_KA_PAD, FFT_NB * c), lambda i, j: (i, 0, 0, j)),
        compiler_params=_cparams(("parallel", "parallel")),
        name="conv_fwd1",
    )(m_fwd, u.reshape(n, a_half, FFT_R, c))


FFT_KB = 8


def _rows_to_slabs(src_ref, dst_scr, c):
    for part in range(2):
        for b in range(FFT_R):
            dst_scr[part, :, b, :] = src_ref[0, part, :, b * c:(b + 1) * c]


def _slabs_to_rows(src_scr, dst_ref, c):
    for part in range(2):
        for b in range(FFT_R):
            dst_ref[0, part, :, b * c:(b + 1) * c] = src_scr[part, :, b, :]


def _slab(scr, i):
    return jnp.concatenate([scr[0, i], scr[1, i]], axis=0).astype(BF16)


def _fwd2f_body(sf_ref, sb_ref, g_ref, l1_ref, kf_ref, f3, b3):
    o = pl.program_id(0)
    j = pl.program_id(1)
    r2 = 2 * FFT_R
    c = kf_ref.shape[-1]
    _rows_to_slabs(sf_ref, f3, c)
    _rows_to_slabs(sb_ref, b3, c)
    inv = 1.0 / (l1_ref[2 * o] + l1_ref[2 * o + 1] + EPS)
    for i in range(FFT_KB):
        @pl.when(j * FFT_KB + i < FFT_KA)
        def _():
            xf = _dot(g_ref[i], _slab(f3, i))
            xb = _dot(g_ref[i], _slab(b3, i))
            kf_ref[0, i, 0:FFT_R] = (xf[0:FFT_R] + xb[0:FFT_R]) * inv
            kf_ref[0, i, FFT_R:r2] = (xf[FFT_R:r2] - xb[FFT_R:r2]) * inv

        @pl.when(j * FFT_KB + i >= FFT_KA)
        def _():
            kf_ref[0, i] = jnp.zeros((r2, c), F32)


def _filter_spectrum(s_filt, l1, g2, c):
    n_ord = s_filt.shape[0] // 2
    cols = s_filt.shape[-1]
    r2 = 2 * FFT_R
    return pl.pallas_call(
        _fwd2f_body,
        out_shape=jax.ShapeDtypeStruct((n_ord, FFT_KA_PAD, r2, c), F32),
        grid=(n_ord, FFT_KA_PAD // FFT_KB),
        in_specs=[pl.BlockSpec((1, 2, FFT_KB, cols), lambda o, j: (2 * o, 0, j, 0)),
                  pl.BlockSpec((1, 2, FFT_KB, cols), lambda o, j: (2 * o + 1, 0, j, 0)),
                  pl.BlockSpec((FFT_KB, r2, r2), lambda o, j: (j, 0, 0)),
                  pl.BlockSpec(l1.shape, lambda o, j: (0, 0, 0))],
        out_specs=pl.BlockSpec((1, FFT_KB, r2, c), lambda o, j: (o, j, 0, 0)),
        scratch_shapes=[pltpu.VMEM((2, FFT_KB, FFT_R, c), F32)] * 2,
        compiler_params=_cparams(("parallel", "parallel")),
        name="filter_spectrum",
    )(s_filt, s_filt, g2, l1)


def _mid_body(s_ref, g_ref, gh_ref, kf_ref, t_ref, s3, t3):
    j = pl.program_id(1)
    r2 = 2 * FFT_R
    c = kf_ref.shape[-1]
    _rows_to_slabs(s_ref, s3, c)
    for i in range(FFT_KB):
        @pl.when(j * FFT_KB + i < FFT_KA)
        def _():
            x = _dot(g_ref[i], _slab(s3, i))
            xr, xi = x[0:FFT_R], x[FFT_R:r2]
            kr, ki = kf_ref[0, i, 0:FFT_R], kf_ref[0, i, FFT_R:r2]
            y = jnp.concatenate([xr * kr - xi * ki, xr * ki + xi * kr], axis=0).astype(BF16)
            t = _dot(gh_ref[i], y)
            t3[0, i] = t[0:FFT_R]
            t3[1, i] = t[FFT_R:r2]

        @pl.when(j * FFT_KB + i >= FFT_KA)
        def _():
            t3[0, i] = jnp.zeros((FFT_R, c), F32)
            t3[1, i] = jnp.zeros((FFT_R, c), F32)

    _slabs_to_rows(t3, t_ref, c)


def _conv_mid(s, kf, order, g2, g2h, c):
    n, _, _, cols = s.shape
    r2 = 2 * FFT_R
    blk = pl.BlockSpec((1, 2, FFT_KB, cols), lambda i, j: (i, 0, j, 0))
    gspec = pl.BlockSpec((FFT_KB, r2, r2), lambda i, j: (j, 0, 0))
    return pl.pallas_call(
        _mid_body,
        out_shape=jax.ShapeDtypeStruct(s.shape, F32),
        grid=(n, FFT_KA_PAD // FFT_KB),
        in_specs=[blk, gspec, gspec,
                  pl.BlockSpec((1, FFT_KB, r2, c), lambda i, j: (order, j, 0, 0))],
        out_specs=blk,
        scratch_shapes=[pltpu.VMEM((2, FFT_KB, FFT_R, c), F32)] * 2,
        compiler_params=_cparams(("parallel", "parallel")),
        name="conv_mid",
    )(s, g2, g2h, kf)


def _inv1_body(m_ref, t_ref, u_ref, xg_ref, sk_ref, o_ref):
    c = u_ref.shape[-1]
    t2 = t_ref[0].reshape(2 * FFT_KA_PAD, FFT_NB * c).astype(BF16)
    y = _dot(m_ref[...], t2)
    for bb in range(FFT_NB):
        conv = y[:, bb * c:(bb + 1) * c] + u_ref[0, :, bb, :] * sk_ref[...]
        o_ref[0, :, bb, :] = xg_ref[0, :, bb, :] * conv


def _conv_inv1(t, u, xg, skip, m_inv):
    n, seq, c = u.shape
    a_half = FFT_A // 2
    sk = skip.astype(F32).reshape(1, c)
    uspec = pl.BlockSpec((1, a_half, FFT_NB, c), lambda i, j: (i, 0, j, 0))
    view = lambda a: a.reshape(n, a_half, FFT_R, c)
    out = pl.pallas_call(
        _inv1_body,
        out_shape=jax.ShapeDtypeStruct((n, a_half, FFT_R, c), F32),
        grid=(n, FFT_R // FFT_NB),
        in_specs=[pl.BlockSpec(m_inv.shape, lambda i, j: (0, 0)),
                  pl.BlockSpec((1, 2, FFT_KA_PAD, FFT_NB * c), lambda i, j: (i, 0, 0, j)),
                  uspec, uspec,
                  pl.BlockSpec((1, c), lambda i, j: (0, 0))],
        out_specs=uspec,
        compiler_params=_cparams(("parallel", "parallel")),
        name="conv_inv1",
    )(m_inv, t, view(u), view(xg), sk)
    return out.reshape(n, seq, c)


def _hyena(v, x1, x2, f_w1, f_b1, f_w2, f_b2, f_w3, f_freq, skip):
    _, seq, c = v.shape
    assert 2 * seq == FFT_A * FFT_R
    m_fwd, m_inv, g2, g2h = _conv_dft_constants()
    taps, l1 = _hyena_filter_taps(seq, f_w1, f_b1, f_w2, f_b2, f_w3, f_freq, c)
    kf = _filter_spectrum(_conv_fwd1(taps, m_fwd), l1, g2, c)
    y = v
    for order, xg in enumerate((x1, x2)):
        t = _conv_mid(_conv_fwd1(y, m_fwd), kf, order, g2, g2h, c)
        y = _conv_inv1(t, y, xg, skip[order], m_inv)
    return y


def _mix_ffn_body(x_ref, a1_ref, a2_ref, wm_ref, gm_ref, g_ref, sh_ref, sc_ref, gt_ref,
                  w1_ref, w3_ref, w2_ref, o_ref, x_scr, h_scr, acc_scr):
    j = pl.program_id(2)

    @pl.when(j == 0)
    def _():
        c1 = a1_ref.shape[-1]
        mixed = _dot(a1_ref[0].astype(BF16), wm_ref[0:c1]) + _dot(a2_ref[0].astype(BF16), wm_ref[c1:])
        xm = x_ref[0] + gm_ref[0] * mixed
        x_scr[...] = xm
        h_scr[...] = _norm_mod(xm, g_ref[...], sh_ref[0], sc_ref[0]).astype(BF16)
        acc_scr[...] = jnp.zeros_like(acc_scr)

    h = h_scr[...]
    a = _dot(h, w1_ref[...])
    u = (a * jax.nn.sigmoid(a) * _dot(h, w3_ref[...])).astype(BF16)
    acc_scr[...] += _dot(u, w2_ref[...])

    @pl.when(j == pl.num_programs(2) - 1)
    def _():
        o_ref[0] = x_scr[...] + gt_ref[0] * acc_scr[...]


def _mix_ffn(x, a1, a2, w_mix, gate_mix, g, shift, scale, gate, w1, w3, w2, tm=512, fb=1408):
    b, s, d = x.shape
    f = w1.shape[1]
    tok = lambda c: pl.BlockSpec((1, tm, c), lambda bi, i, j: (bi, i, 0))
    per_b = pl.BlockSpec((1, 1, d), lambda bi, i, j: (bi, 0, 0))
    const = lambda a: pl.BlockSpec(a.shape, lambda bi, i, j: (0, 0))
    return pl.pallas_call(
        _mix_ffn_body,
        out_shape=jax.ShapeDtypeStruct(x.shape, F32),
        grid=(b, s // tm, f // fb),
        in_specs=[tok(d), tok(a1.shape[-1]), tok(a2.shape[-1]), const(w_mix), per_b,
                  const(g), per_b, per_b, per_b,
                  pl.BlockSpec((d, fb), lambda bi, i, j: (0, j)),
                  pl.BlockSpec((d, fb), lambda bi, i, j: (0, j)),
                  pl.BlockSpec((fb, d), lambda bi, i, j: (j, 0))],
        out_specs=tok(d),
        scratch_shapes=[pltpu.VMEM((tm, d), F32), pltpu.VMEM((tm, d), BF16), pltpu.VMEM((tm, d), F32)],
        compiler_params=_cparams(("parallel", "parallel", "arbitrary")),
        name="mix_ffn",
    )(x, a1, a2, w_mix, gate_mix, g, shift, scale, gate, w1, w3, w2)


def _fm_constants(cg):
    j = np.arange(cg)[:, None]
    m = np.arange(cg)[None, :]
    ph = 2.0 * np.pi * j * m / cg
    w_cs = np.concatenate([np.cos(ph), np.sin(ph)], axis=1)
    d = np.arange(FM_A)[:, None]
    a = np.arange(FM_A)[None, :]
    ph = 2.0 * np.pi * d * a / FM_A
    fr, fi = np.cos(ph), -np.sin(ph)
    m1 = np.block([[fr, fi], [fi, -fr]])
    n = FM_A * FM_A
    dd = np.arange(FM_A)[:, None, None]
    c = np.arange(FM_A)[None, :, None]
    b = np.arange(FM_A)[None, None, :]
    th = 2.0 * np.pi * (b * c / FM_A + b * dd / n)
    gcat = np.concatenate([np.cos(th), np.sin(th)], axis=2)
    return _mxu_const(w_cs), _mxu_const(m1), _mxu_const(gcat)


def _fm_front_body(x_ref, g_ref, sh_ref, sc_ref, w_ref, m_ref, o_ref, *, cg, nb):
    d = x_ref.shape[-1]
    xs = jnp.concatenate([x_ref[0, :, bb, :] for bb in range(nb)], axis=0)
    h = _norm_mod(xs, g_ref[...], sh_ref[0], sc_ref[0]).astype(BF16)
    pq = [_dot(h[:, grp * cg:(grp + 1) * cg], w_ref[...]) for grp in range(d // cg)]
    p = jnp.concatenate([t[:, 0:cg] for t in pq], axis=1)
    q = jnp.concatenate([t[:, cg:2 * cg] for t in pq], axis=1)
    for bb in range(nb):
        rows = slice(bb * FM_A, (bb + 1) * FM_A)
        res = _dot(m_ref[...], jnp.concatenate([p[rows], q[rows]], axis=0).astype(BF16))
        o_ref[0, 0, :, bb, :] = res[0:FM_A]
        o_ref[0, 1, :, bb, :] = res[FM_A:2 * FM_A]


def _fm_front(x, g, shift, scale, w_cs, m1, nb=8):
    b, s, d = x.shape
    cg = w_cs.shape[0]
    per_b = pl.BlockSpec((1, 1, d), lambda bi, j: (bi, 0, 0))
    const = lambda a: pl.BlockSpec(a.shape, lambda bi, j: (0, 0))
    return pl.pallas_call(
        functools.partial(_fm_front_body, cg=cg, nb=nb),
        out_shape=jax.ShapeDtypeStruct((b, 2, FM_A, s // FM_A, d), F32),
        grid=(b, s // FM_A // nb),
        in_specs=[pl.BlockSpec((1, FM_A, nb, d), lambda bi, j: (bi, 0, j, 0)),
                  const(g), per_b, per_b, const(w_cs), const(m1)],
        out_specs=pl.BlockSpec((1, 2, FM_A, nb, d), lambda bi, j: (bi, 0, 0, j, 0)),
        compiler_params=_cparams(("parallel", "parallel")),
        name="fm_front",
    )(x.reshape(b, FM_A, s // FM_A, d), g, shift, scale, w_cs, m1)


def _fm_s2_body(s_ref, g_ref, o_ref, *, dblk, scale):
    for i in range(dblk):
        s2 = jnp.concatenate([s_ref[0, 0, i], s_ref[0, 1, i]], axis=0).astype(BF16)
        o_ref[:, i, :] = _dot(g_ref[i], s2) * scale


def _fm_stage2(sv, gcat, seq, d, dblk=8):
    b = sv.shape[0]
    scale = 1.0 / math.sqrt(seq * (d // F_GROUPS))
    out = pl.pallas_call(
        functools.partial(_fm_s2_body, dblk=dblk, scale=scale),
        out_shape=jax.ShapeDtypeStruct((b * FM_A, dblk * (FM_A // dblk), d), F32),
        grid=(b, FM_A // dblk),
        in_specs=[pl.BlockSpec((1, 2, dblk, FM_A, d), lambda bi, j: (bi, 0, j, 0, 0)),
                  pl.BlockSpec((dblk, FM_A, 2 * FM_A), lambda bi, j: (j, 0, 0))],
        out_specs=pl.BlockSpec((FM_A, dblk, d), lambda bi, j: (bi, j, 0)),
        compiler_params=_cparams(("parallel", "parallel")),
        name="fm_stage2",
    )(sv, gcat)
    return out.reshape(b, seq, d)


def _fourier_mix(x, g, shift, scale):
    b, s, d = x.shape
    assert s == FM_A * FM_A
    w_cs, m1, gcat = _fm_constants(d // F_GROUPS)
    return _fm_stage2(_fm_front(x, g, shift, scale, w_cs, m1), gcat, s, d)


LANES = 128
ROW_SL = 8
MOE_TM = 1024
DMA_WINDOW = 128


def _router_body(x_ref, yf_ref, wf_ref, gf_ref, g_ref, sh_ref, sc_ref, wr_ref, br_ref,
                 xo_ref, h_ref, meta_ref, gw_ref, cnt_ref, carry):
    i = pl.program_id(0)

    @pl.when(i == 0)
    def _():
        carry[...] = jnp.zeros_like(carry)

    xm = x_ref[...] + gf_ref[0] * _dot(yf_ref[...].astype(BF16), wf_ref[...])
    xo_ref[...] = xm
    h = _norm_mod(xm, g_ref[...], sh_ref[0], sc_ref[0])
    _rows_to_tiles(h_ref, h)
    h_hi = h.astype(BF16)
    h_lo = (h - h_hi.astype(F32)).astype(BF16)
    by_hi = _dot(h_hi, wr_ref[...])
    logits = by_hi[:, 0:LANES] + by_hi[:, LANES:] + _dot(h_lo, wr_ref[:, 0:LANES]) + br_ref[...]
    lane = lax.broadcasted_iota(jnp.int32, logits.shape, 1)
    nl = logits.shape[-1]
    m1 = jnp.max(logits, axis=-1, keepdims=True)
    i1 = jnp.min(jnp.where(logits == m1, lane, nl), axis=-1, keepdims=True)
    rest = jnp.where(lane == i1, -3.0e38, logits)
    m2 = jnp.max(rest, axis=-1, keepdims=True)
    i2 = jnp.min(jnp.where(rest == m2, lane, nl), axis=-1, keepdims=True)
    e = jnp.exp(m2 - m1)
    gw_ref[...] = jnp.where(lane == 0, 1.0 / (1.0 + e), jnp.where(lane == 1, e / (1.0 + e), 0.0))
    onehot = jnp.where((lane == i1) | (lane == i2), 1.0, 0.0)
    tm = onehot.shape[0]
    earlier = lax.broadcasted_iota(jnp.int32, (tm, tm), 0) > lax.broadcasted_iota(jnp.int32, (tm, tm), 1)
    excl = _dot(jnp.where(earlier, 1.0, 0.0).astype(BF16), onehot.astype(BF16)) + carry[...]
    r1 = jnp.sum(jnp.where(lane == i1, excl, 0.0), axis=-1, keepdims=True).astype(jnp.int32)
    r2 = jnp.sum(jnp.where(lane == i2, excl, 0.0), axis=-1, keepdims=True).astype(jnp.int32)
    meta_ref[...] = jnp.where(lane == 0, i1, jnp.where(lane == 1, i2, jnp.where(lane == 2, r1, jnp.where(lane == 3, r2, 0))))
    carry[...] = carry[...] + jnp.sum(onehot, axis=0, keepdims=True)
    cnt_ref[...] = carry[...]


def _router(x, y_f, w_f, gate_f, g, shift, scale, w_router, b_router, tm=512):
    b, s, d = x.shape
    t = b * s
    ne = w_router.shape[1]
    wr = jnp.pad(w_router.astype(F32), ((0, 0), (0, LANES - ne)))
    wr_hi = wr.astype(BF16)
    wr = jnp.concatenate([wr_hi, (wr - wr_hi.astype(F32)).astype(BF16)], axis=1)
    br = jnp.pad(b_router.astype(F32).reshape(1, ne), ((0, 0), (0, LANES - ne)), constant_values=NEG_INF)
    spt = s // tm
    per_b = pl.BlockSpec((1, 1, d), lambda i: (i // spt, 0, 0))
    const = lambda a: pl.BlockSpec(a.shape, lambda i: (0, 0))
    tok = pl.BlockSpec((tm, d), lambda i: (i, 0))
    return pl.pallas_call(
        _router_body,
        out_shape=[jax.ShapeDtypeStruct((t, d), F32),
                   jax.ShapeDtypeStruct((t * ROW_SL, LANES), F32),
                   jax.ShapeDtypeStruct((t, LANES), jnp.int32),
                   jax.ShapeDtypeStruct((t, LANES), F32),
                   jax.ShapeDtypeStruct((1, LANES), F32)],
        grid=(t // tm,),
        in_specs=[tok, tok, const(w_f), per_b, const(g), per_b, per_b, const(wr), const(br)],
        out_specs=[tok,
                   pl.BlockSpec((tm * ROW_SL, LANES), lambda i: (i, 0)),
                   pl.BlockSpec((tm, LANES), lambda i: (i, 0)),
                   pl.BlockSpec((tm, LANES), lambda i: (i, 0)),
                   pl.BlockSpec((1, LANES), lambda i: (0, 0))],
        scratch_shapes=[pltpu.VMEM((1, LANES), F32)],
        compiler_params=_cparams(("arbitrary",)),
        name="router",
    )(x.reshape(t, d), y_f.reshape(t, d), w_f, gate_f, g, shift, scale, wr, br)


def _moe_plan(meta, counts, ne, tm):
    i1, i2, r1, r2 = meta[:, 0], meta[:, 1], meta[:, 2], meta[:, 3]
    cnt = counts[0, :ne].astype(jnp.int32)
    padded = ((cnt + tm - 1) // tm) * tm
    ends = jnp.cumsum(padded)
    offs = ends - padded
    pick = lambda idx: sum(jnp.where(idx == e, offs[e], 0) for e in range(ne))
    pos = jnp.concatenate([pick(i1) + r1, pick(i2) + r2]).astype(jnp.int32)
    n_tiles = (2 * meta.shape[0]) // tm + ne
    n_used = (ends[ne - 1] // tm).astype(jnp.int32)
    tile_start = jnp.minimum(jnp.arange(n_tiles, dtype=jnp.int32), n_used - 1) * tm
    tile_expert = jnp.sum(tile_start[:, None] >= ends[None, :], axis=1).astype(jnp.int32)
    group_end = sum(jnp.where(tile_expert == e, offs[e] + cnt[e], 0) for e in range(ne))
    tile_rows = jnp.clip(group_end - tile_start, 0, tm).astype(jnp.int32)
    return pos, offs + cnt, padded - cnt, tile_expert, n_used.reshape(1), tile_rows


def _windowed_copies(n, start_copy, wait_one, per_iter):
    def body(i, carry):
        @pl.when(i >= DMA_WINDOW)
        def _():
            for _ in range(per_iter):
                wait_one()
        start_copy(i)
        return carry

    lax.fori_loop(0, n, body, 0)

    def drain(i, carry):
        for _ in range(per_iter):
            wait_one()
        return carry

    lax.fori_loop(0, jnp.minimum(n, DMA_WINDOW), drain, 0)


def _tile_of(ref, row):
    return ref.at[pl.ds(pl.multiple_of(row * ROW_SL, ROW_SL), ROW_SL)]


def _tiles_to_rows(ref, n, first=0):
    return jnp.concatenate([ref[pl.ds(first * ROW_SL + sl, n, stride=ROW_SL), :] for sl in range(ROW_SL)], axis=1)


def _rows_to_tiles(ref, val):
    n = val.shape[0]
    for sl in range(ROW_SL):
        ref[pl.ds(sl, n, stride=ROW_SL), :] = val[:, sl * LANES:(sl + 1) * LANES]


def _dispatch_body(pos_ref, pad_start_ref, pad_n_ref, h_ref, xs_hbm, sem, *, n_tok, ne):
    i = pl.program_id(0)
    td = h_ref.shape[0] // ROW_SL
    base = i * td
    copy = lambda src, dst: pltpu.make_async_copy(_tile_of(h_ref, src), _tile_of(xs_hbm, dst), sem)
    wait_one = lambda: copy(0, 0).wait()

    def start_token(r, carry):
        copy(r, pos_ref[base + r]).start(priority=0)
        copy(r, pos_ref[n_tok + base + r]).start(priority=1)
        return carry

    lax.fori_loop(0, td, start_token, 0, unroll=8)
    whole_tile = pltpu.make_async_copy(h_ref, xs_hbm.at[pl.ds(0, td * ROW_SL)], sem)
    whole_tile.wait()
    whole_tile.wait()

    @pl.when(i == 0)
    def _():
        for e in range(ne):
            first = pad_start_ref[e]
            _windowed_copies(pad_n_ref[e], lambda r: copy(0, first + r).start(), wait_one, 1)


def _moe_dispatch(h3, pos, pad_start, pad_n, n_rows, td=1024):
    n_tok = h3.shape[0] // ROW_SL
    ne = pad_start.shape[0]
    return pl.pallas_call(
        functools.partial(_dispatch_body, n_tok=n_tok, ne=ne),
        out_shape=jax.ShapeDtypeStruct((n_rows * ROW_SL, LANES), h3.dtype),
        grid_spec=pltpu.PrefetchScalarGridSpec(
            num_scalar_prefetch=3, grid=(n_tok // td,),
            in_specs=[pl.BlockSpec((td * ROW_SL, LANES), lambda i, p, ps, pn: (i, 0))],
            out_specs=pl.BlockSpec(memory_space=pl.ANY),
            scratch_shapes=[pltpu.SemaphoreType.DMA(())]),
        compiler_params=_cparams(("arbitrary",)),
        name="moe_dispatch",
    )(pos, pad_start, pad_n, h3)


def _moe_grouped_body(te_ref, nu_ref, tr_ref, xs_ref, w1_ref, w3_ref, w2_ref, y_ref, xb_scr, acc_scr):
    i = pl.program_id(0)
    j = pl.program_id(1)
    tm = xb_scr.shape[0]
    hm = tm // 2

    def expert_rows(nrows):
        h = xb_scr[0:nrows]
        a = _dot(h, w1_ref[0].astype(BF16))
        u = (a * jax.nn.sigmoid(a) * _dot(h, w3_ref[0].astype(BF16))).astype(BF16)
        part = _dot(u, w2_ref[0].astype(BF16))

        @pl.when(j == 0)
        def _():
            acc_scr[0:nrows] = part

        @pl.when(j > 0)
        def _():
            acc_scr[0:nrows] += part

    @pl.when(i < nu_ref[0])
    def _():
        @pl.when(j == 0)
        def _():
            xb_scr[...] = _tiles_to_rows(xs_ref, tm).astype(BF16)

        @pl.when(tr_ref[i] > hm)
        def _():
            expert_rows(tm)

        @pl.when(tr_ref[i] <= hm)
        def _():
            expert_rows(hm)

            @pl.when(j == 0)
            def _():
                acc_scr[hm:tm] = jnp.zeros((tm - hm, acc_scr.shape[1]), F32)

        @pl.when(j == pl.num_programs(1) - 1)
        def _():
            _rows_to_tiles(y_ref, acc_scr[...])


def _moe_grouped(xs, tile_expert, n_used, tile_rows, w1, w3, w2, tm, fb=512):
    ne, d, f = w1.shape
    n_rows = xs.shape[0] // ROW_SL
    nj = f // fb
    row_tile = lambda i, j, te, nu, tr: (jnp.maximum(jnp.minimum(i, nu[0] - 1), 0), 0)
    jj = lambda i, j, nu: jnp.where(i < nu[0], j, nj - 1)
    return pl.pallas_call(
        _moe_grouped_body,
        out_shape=jax.ShapeDtypeStruct(xs.shape, F32),
        grid_spec=pltpu.PrefetchScalarGridSpec(
            num_scalar_prefetch=3, grid=(n_rows // tm, nj),
            in_specs=[pl.BlockSpec((tm * ROW_SL, LANES), row_tile),
                      pl.BlockSpec((1, d, fb), lambda i, j, te, nu, tr: (te[i], 0, jj(i, j, nu))),
                      pl.BlockSpec((1, d, fb), lambda i, j, te, nu, tr: (te[i], 0, jj(i, j, nu))),
                      pl.BlockSpec((1, fb, d), lambda i, j, te, nu, tr: (te[i], jj(i, j, nu), 0))],
            out_specs=pl.BlockSpec((tm * ROW_SL, LANES), row_tile),
            scratch_shapes=[pltpu.VMEM((tm, d), BF16), pltpu.VMEM((tm, d), F32)]),
        compiler_params=_cparams(("arbitrary", "arbitrary")),
        name="moe_grouped",
    )(tile_expert, n_used, tile_rows, xs, w1, w3, w2)


def _moe_final_body(pos_ref, x_ref, y_hbm, gw_ref, gt_ref, fg_ref, o_ref, yg_scr, sem, *, n_tok):
    i = pl.program_id(0)
    tc = x_ref.shape[0]
    slot = i % 2

    def gather_tile(step, into):
        base = step * tc

        def start_token(r, carry):
            dst = yg_scr.at[into]
            pltpu.make_async_copy(_tile_of(y_hbm, pos_ref[base + r]), _tile_of(dst, r),
                                  sem.at[into]).start(priority=0)
            pltpu.make_async_copy(_tile_of(y_hbm, pos_ref[n_tok + base + r]), _tile_of(dst, tc + r),
                                  sem.at[into]).start(priority=1)
            return carry

        lax.fori_loop(0, tc, start_token, 0, unroll=8)

    @pl.when(i == 0)
    def _():
        gather_tile(0, 0)

    @pl.when(i + 1 < pl.num_programs(0))
    def _():
        gather_tile(i + 1, 1 - slot)

    pltpu.make_async_copy(y_hbm.at[pl.ds(0, 2 * tc * ROW_SL)], yg_scr.at[slot], sem.at[slot]).wait()
    gw = gw_ref[...]
    rows = yg_scr.at[slot]
    y = gw[:, 0:1] * _tiles_to_rows(rows, tc) + gw[:, 1:2] * _tiles_to_rows(rows, tc, first=tc)
    xo = x_ref[...] + gt_ref[0] * y
    ms = jnp.mean(xo * xo, axis=-1, keepdims=True)
    o_ref[...] = xo * lax.rsqrt(ms + EPS) * fg_ref[...]


def _moe_final(x, y, pos, gw, gt, final_g, tc=512):
    b, s, d = x.shape
    t = b * s
    spt = s // tc
    nsl = d // LANES
    out = pl.pallas_call(
        functools.partial(_moe_final_body, n_tok=t),
        out_shape=jax.ShapeDtypeStruct((t, d), F32),
        grid_spec=pltpu.PrefetchScalarGridSpec(
            num_scalar_prefetch=1, grid=(t // tc,),
            in_specs=[pl.BlockSpec((tc, d), lambda i, p: (i, 0)),
                      pl.BlockSpec(memory_space=pl.ANY),
                      pl.BlockSpec((tc, LANES), lambda i, p: (i, 0)),
                      pl.BlockSpec((1, 1, d), lambda i, p: (i // spt, 0, 0)),
                      pl.BlockSpec(final_g.shape, lambda i, p: (0, 0))],
            out_specs=pl.BlockSpec((tc, d), lambda i, p: (i, 0)),
            scratch_shapes=[pltpu.VMEM((2, 2 * tc * ROW_SL, LANES), F32), pltpu.SemaphoreType.DMA((2,))]),
        compiler_params=_cparams(("arbitrary",)),
        name="moe_final",
    )(pos, x.reshape(t, d), y, gw, gt, final_g)
    return out.reshape(b, s, d)


def _moe_routed(x, y_f, w_f, gate_f, g, shift, scale, gt, final_g, w_router, b_router, w1, w3, w2):
    ne = w1.shape[0]
    tm = MOE_TM
    x1, h3, meta, gw, counts = _router(x, y_f, w_f, gate_f, g, shift, scale, w_router, b_router)
    pos, pad_start, pad_n, tile_expert, n_used, tile_rows = _moe_plan(meta, counts, ne, tm)
    assert x.shape[-1] == ROW_SL * LANES
    n_rows = (2 * (h3.shape[0] // ROW_SL) // tm + ne) * tm
    xs = _moe_dispatch(h3, pos, pad_start, pad_n, n_rows)
    y = _moe_grouped(xs, tile_expert, n_used, tile_rows, w1, w3, w2, tm)
    return _moe_final(x1.reshape(x.shape), y, pos, gw, gt, final_g)


def kernel(x, c, ctx, c_ctx, w_ada, b_ada, norm_g, w_in, hy_short_w, hy_short_b, hy_f_w1, hy_f_b1, hy_f_w2, hy_f_b2, hy_f_w3, hy_f_freq, hy_skip, na_rpb, w_mix_out, ffn_w1, ffn_w3, ffn_w2, w_fourier, w_router, b_router, moe_w1, moe_w3, moe_w2, final_g):
    b, s, d = x.shape
    depth = w_ada.shape[0]
    assert depth == 2, "layer 0 mixes with Hyena/attention, layer 1 with Fourier/MoE"
    c_hy = hy_skip.shape[-1]
    c_na = d - c_hy

    cvec = jnp.concatenate([c, c_ctx[None, :], jnp.zeros((8 - b - 1, d), F32)], axis=0)
    mods = _ada(cvec, w_ada, b_ada)

    def mod(layer, idx, ctx_row=False):
        m = mods[layer, :, idx * d:(idx + 1) * d]
        return m[b:b + 1, None, :] if ctx_row else m[0:b, None, :]

    row = lambda a: a.reshape(1, -1)

    w_in0 = w_in[0].astype(BF16)
    w_hy, w_qkv = w_in0[:, 0:3 * c_hy], w_in0[:, 3 * c_hy:]
    v, x1, x2, q, k, va = _inproj(x, row(norm_g[0, 0]), mod(0, 0), mod(0, 1), w_hy, w_qkv,
                                  hy_short_w[0], row(hy_short_b[0]))
    kc, vc = _ctxkv(ctx, row(norm_g[0, 0]), mod(0, 0, True), mod(0, 1, True), w_qkv[:, c_na:])
    y_na = _natt(q, k, va, kc, vc, _na_bias_table(na_rpb[0]))
    y_hy = _hyena(v, x1, x2, hy_f_w1[0], hy_f_b1[0], hy_f_w2[0], hy_f_b2[0], hy_f_w3[0],
                  hy_f_freq[0], hy_skip[0])
    x = _mix_ffn(x, y_hy, y_na, w_mix_out[0].astype(BF16), mod(0, 2),
                 row(norm_g[0, 1]), mod(0, 3), mod(0, 4), mod(0, 5),
                 ffn_w1[0].astype(BF16), ffn_w3[0].astype(BF16), ffn_w2[0].astype(BF16))

    y_f = _fourier_mix(x, row(norm_g[1, 0]), mod(1, 0), mod(1, 1))
    return _moe_routed(x, y_f, w_fourier[0].astype(BF16), mod(1, 2),
                       row(norm_g[1, 1]), mod(1, 3), mod(1, 4), mod(1, 5), row(final_g),
                       w_router[0], b_router[0],
                       moe_w1[0], moe_w3[0], moe_w2[0])
```

```python
import functools
import math

import numpy as np
import jax
import jax.numpy as jnp
from jax import lax
from jax.experimental import pallas as pl
from jax.experimental.pallas import tpu as pltpu

F32 = jnp.float32
BF16 = jnp.bfloat16
HIGHEST = lax.Precision.HIGHEST

GRID_W = 64
NA_HEAD_DIM = 32
NA_WIN_R = 8
NA_WIN_C = 16
HYENA_EMB = 33
HYENA_BANDS = (HYENA_EMB - 1) // 2
HYENA_FAST_DECAY = 0.3
HYENA_SLOW_DECAY = 1.5
HYENA_TARGET = 1e-2
F_GROUPS = 4
N_MOD = 6
EPS = 1e-6
NEG_INF = -1e30

FFT_A = 64
FFT_R = 128
FFT_KA = FFT_A // 2 + 1
FFT_KA_PAD = 40
FM_A = 64

VMEM_LIMIT = 48 * 1024 * 1024


def _cparams(sem, vmem_limit=VMEM_LIMIT):
    return pltpu.CompilerParams(dimension_semantics=sem, vmem_limit_bytes=vmem_limit)


def _dot(a, b):
    return jnp.dot(a, b, preferred_element_type=F32)


def _mxu_const(m):
    return jnp.asarray(m, dtype=F32).astype(BF16)


def _norm_mod(x, g, shift, scale):
    ms = jnp.mean(x * x, axis=-1, keepdims=True)
    y = x * lax.rsqrt(ms + EPS) * g
    return y * (1.0 + scale) + shift


def _ada_body(c_ref, w_ref, b_ref, o_ref):
    cv = c_ref[...]
    s = cv * jax.nn.sigmoid(cv)
    o_ref[0] = jnp.dot(s, w_ref[0], precision=HIGHEST, preferred_element_type=F32) + b_ref[0]


def _ada(cvec, w_ada, b_ada):
    depth, d, n = w_ada.shape
    rows = cvec.shape[0]
    bn = n // 4
    return pl.pallas_call(
        _ada_body,
        out_shape=jax.ShapeDtypeStruct((depth, rows, n), F32),
        grid=(depth, n // bn),
        in_specs=[pl.BlockSpec((rows, d), lambda l, j: (0, 0)),
                  pl.BlockSpec((1, d, bn), lambda l, j: (l, 0, j)),
                  pl.BlockSpec((1, 1, bn), lambda l, j: (l, 0, j))],
        out_specs=pl.BlockSpec((1, rows, bn), lambda l, j: (l, 0, j)),
        compiler_params=_cparams(("parallel", "parallel")),
        name="ada",
    )(cvec, w_ada, b_ada.reshape(depth, 1, n))


def _inproj_body(x_ref, xp_ref, xn_ref, g_ref, sh_ref, sc_ref, why_ref, wqkv_ref, sw_ref, sb_ref,
                 v_ref, x1_ref, x2_ref, q_ref, k_ref, va_ref, *, n_tiles, q_scale, c_hy, c_na):
    i = pl.program_id(1)
    g, sh, sc = g_ref[...], sh_ref[0], sc_ref[0]
    hf = _norm_mod(x_ref[0], g, sh, sc)
    h = hf.astype(BF16)
    tm = hf.shape[0]
    hx = jnp.concatenate([_norm_mod(xp_ref[0], g, sh, sc), hf, _norm_mod(xn_ref[0], g, sh, sc)], axis=0)
    zx = _dot(hx.astype(BF16), why_ref[...])
    zh = zx[8:8 + tm]
    zp = jnp.where(i > 0, zx[7:8], 0.0)
    zn = jnp.where(i < n_tiles - 1, zx[8 + tm:9 + tm], 0.0)
    row = lax.broadcasted_iota(jnp.int32, zh.shape, 0)
    z_m1 = jnp.where(row == 0, zp, pltpu.roll(zh, 1, 0))
    z_p1 = jnp.where(row == tm - 1, zn, pltpu.roll(zh, tm - 1, 0))
    sw = sw_ref[...]
    zc = z_m1 * sw[0:1] + zh * sw[1:2] + z_p1 * sw[2:3] + sb_ref[...]
    v_ref[0] = zc[:, 0:c_hy]
    x1_ref[0] = zc[:, c_hy:2 * c_hy]
    x2_ref[0] = zc[:, 2 * c_hy:3 * c_hy]
    zq = _dot(h, wqkv_ref[...])
    q_ref[0] = (zq[:, 0:c_na] * q_scale).astype(BF16)
    k_ref[0] = zq[:, c_na:2 * c_na].astype(BF16)
    va_ref[0] = zq[:, 2 * c_na:3 * c_na].astype(BF16)


def _inproj(x, g, shift, scale, w_hy, w_qkv, short_w, short_b, tm=512):
    b, s, d = x.shape
    c_hy = w_hy.shape[1] // 3
    c_na = w_qkv.shape[1] // 3
    n_tiles = s // tm
    r8 = tm // 8
    body = functools.partial(_inproj_body, n_tiles=n_tiles, q_scale=NA_HEAD_DIM ** -0.5,
                             c_hy=c_hy, c_na=c_na)
    tok = lambda c: pl.BlockSpec((1, tm, c), lambda bi, i: (bi, i, 0))
    full2 = lambda a: pl.BlockSpec(a.shape, lambda bi, i: (0, 0))
    per_b = pl.BlockSpec((1, 1, d), lambda bi, i: (bi, 0, 0))
    return pl.pallas_call(
        body,
        out_shape=[jax.ShapeDtypeStruct((b, s, c_hy), F32)] * 3 + [jax.ShapeDtypeStruct((b, s, c_na), BF16)] * 3,
        grid=(b, n_tiles),
        in_specs=[tok(d),
                  pl.BlockSpec((1, 8, d), lambda bi, i: (bi, jnp.maximum(i * r8 - 1, 0), 0)),
                  pl.BlockSpec((1, 8, d), lambda bi, i: (bi, jnp.minimum((i + 1) * r8, s // 8 - 1), 0)),
                  full2(g), per_b, per_b, full2(w_hy), full2(w_qkv), full2(short_w), full2(short_b)],
        out_specs=[tok(c_hy)] * 3 + [tok(c_na)] * 3,
        compiler_params=_cparams(("parallel", "parallel")),
        name="inproj",
    )(x, x, x, g, shift, scale, w_hy, w_qkv, short_w, short_b)


def _ctxkv_body(x_ref, g_ref, sh_ref, sc_ref, w_ref, k_ref, v_ref, *, c_na):
    h = _norm_mod(x_ref[0], g_ref[...], sh_ref[0], sc_ref[0]).astype(BF16)
    z = _dot(h, w_ref[...])
    k_ref[0] = z[:, 0:c_na].astype(BF16)
    v_ref[0] = z[:, c_na:2 * c_na].astype(BF16)


def _ctxkv(ctx, g, shift, scale, w_kv):
    b, n, d = ctx.shape
    c_na = w_kv.shape[1] // 2
    one = pl.BlockSpec((1, 1, d), lambda bi: (0, 0, 0))
    return pl.pallas_call(
        functools.partial(_ctxkv_body, c_na=c_na),
        out_shape=[jax.ShapeDtypeStruct((b, n, c_na), BF16)] * 2,
        grid=(b,),
        in_specs=[pl.BlockSpec((1, n, d), lambda bi: (bi, 0, 0)),
                  pl.BlockSpec(g.shape, lambda bi: (0, 0)), one, one,
                  pl.BlockSpec(w_kv.shape, lambda bi: (0, 0))],
        out_specs=[pl.BlockSpec((1, n, c_na), lambda bi: (bi, 0, 0))] * 2,
        compiler_params=_cparams(("parallel",)),
        name="ctxkv",
    )(ctx, g, shift, scale, w_kv)


NA_HEADS_PER_BLK = 8
NA_ROWS_PER_STEP = 2


def _na_bias_body(r_ref, e_ref, ok_ref, o_ref):
    t = jnp.dot(r_ref[...], e_ref[...], precision=HIGHEST, preferred_element_type=F32)
    o_ref[...] = jnp.where(ok_ref[...] > 0.5, t, NEG_INF)


def _na_bias_table(rpb):
    w = GRID_W
    h, nr, nc = rpb.shape
    col = np.arange(w)[:, None]
    kc = np.arange(w)[None, :]
    c_start = np.clip(col - NA_WIN_C // 2, 0, w - NA_WIN_C)
    valid = ((kc >= c_start) & (kc < c_start + NA_WIN_C)).reshape(1, w * w)
    expand = (np.arange(32)[:, None, None] == (kc - col + NA_WIN_C - 1)[None]).reshape(32, w * w)
    rp = jnp.pad(rpb.astype(F32).reshape(h * nr, nc), ((0, 0), (0, 32 - nc)))
    full = lambda a: pl.BlockSpec(a.shape, lambda: (0,) * a.ndim)
    expand = jnp.asarray(expand, dtype=F32)
    ok = jnp.asarray(valid, dtype=F32)
    toep = pl.pallas_call(
        _na_bias_body,
        out_shape=jax.ShapeDtypeStruct((h * nr, w * w), F32),
        in_specs=[full(rp), full(expand), full(ok)],
        out_specs=pl.BlockSpec((h * nr, w * w), lambda: (0, 0)),
        name="na_bias",
    )(rp, expand, ok)
    t2 = toep.reshape(h, nr, w, w).transpose(0, 2, 1, 3).reshape(h, w, nr * w)
    slabs = jnp.stack([t2[:, :, (NA_WIN_R - 1 - off) * w:(2 * NA_WIN_R - 1 - off) * w]
                       for off in range(NA_WIN_R)], axis=1)
    hpb = NA_HEADS_PER_BLK
    slabs = slabs.reshape(h // hpb, hpb, NA_WIN_R, w, NA_WIN_R * w).transpose(0, 2, 1, 3, 4)
    return slabs.reshape(h // hpb, NA_WIN_R, hpb * w, NA_WIN_R * w)


def _natt_body(q_ref, k_ref, v_ref, kc_ref, vc_ref, bias_ref, o_ref, *, rows):
    w = GRID_W
    hpb = NA_HEADS_PER_BLK
    nloc = NA_WIN_R * w
    lane = lax.broadcasted_iota(jnp.int32, (1, hpb * NA_HEAD_DIM), 1)
    in_head = [(lane >= NA_HEAD_DIM * hh) & (lane < NA_HEAD_DIM * (hh + 1)) for hh in range(hpb)]
    kcx = kc_ref[0]
    vcx = vc_ref[0]
    nt = (((1,), (1,)), ((), ()))

    def one_row(r):
        r0 = jnp.clip(r - NA_WIN_R // 2, 0, rows - NA_WIN_R)
        off = r - r0
        qs = q_ref[0, pl.ds(pl.multiple_of(r * w, w), w), :]
        kw = k_ref[0, pl.ds(pl.multiple_of(r0 * w, w), nloc), :]
        vw = v_ref[0, pl.ds(pl.multiple_of(r0 * w, w), nloc), :]
        zero = jnp.zeros_like(qs)
        qst = jnp.concatenate([jnp.where(m, qs, zero) for m in in_head], axis=0)
        s_loc = lax.dot_general(qst, kw, nt, preferred_element_type=F32) + bias_ref[0, off]
        s_ctx = lax.dot_general(qst, kcx, nt, preferred_element_type=F32)
        m = jnp.maximum(jnp.max(s_loc, axis=-1, keepdims=True), jnp.max(s_ctx, axis=-1, keepdims=True))
        p_loc = jnp.exp(s_loc - m)
        p_ctx = jnp.exp(s_ctx - m)
        den = jnp.sum(p_loc, axis=-1, keepdims=True) + jnp.sum(p_ctx, axis=-1, keepdims=True)
        o = (_dot(p_loc.astype(BF16), vw) + _dot(p_ctx.astype(BF16), vcx)) * (1.0 / den)
        acc = jnp.where(in_head[0], o[0:w], 0.0)
        for hh in range(1, hpb):
            acc = acc + jnp.where(in_head[hh], o[hh * w:(hh + 1) * w], 0.0)
        o_ref[0, pl.ds(pl.multiple_of(r * w, w), w), :] = acc.astype(BF16)

    def row_group(i, carry):
        for r in range(NA_ROWS_PER_STEP):
            one_row(NA_ROWS_PER_STEP * i + r)
        return carry

    lax.fori_loop(0, rows // NA_ROWS_PER_STEP, row_group, 0)


def _natt(q, k, v, kc, vc, bias):
    b, s, c = q.shape
    nctx = kc.shape[1]
    lw = NA_HEADS_PER_BLK * NA_HEAD_DIM
    rows = s // GRID_W
    seq = pl.BlockSpec((1, s, lw), lambda bi, g: (bi, 0, g))
    cx = pl.BlockSpec((1, nctx, lw), lambda bi, g: (bi, 0, g))
    return pl.pallas_call(
        functools.partial(_natt_body, rows=rows),
        out_shape=jax.ShapeDtypeStruct((b, s, c), BF16),
        grid=(b, c // lw),
        in_specs=[seq, seq, seq, cx, cx,
                  pl.BlockSpec((1,) + bias.shape[1:], lambda bi, g: (g, 0, 0, 0))],
        out_specs=seq,
        compiler_params=_cparams(("parallel", "parallel")),
        name="natt",
    )(q, k, v, kc, vc, bias)


def _hyena_feats(seq_len):
    t = jnp.linspace(0.0, 1.0, seq_len, dtype=F32)[:, None]
    bands = jnp.linspace(1e-4, HYENA_BANDS - 1, HYENA_BANDS, dtype=F32)
    ang = (2.0 * math.pi / seq_len) * jnp.arange(seq_len, dtype=F32)[:, None] * bands[None, :]
    feats = jnp.concatenate([t, jnp.cos(ang), -jnp.sin(ang)], axis=-1)
    return jnp.pad(feats, ((0, 0), (0, 128 - HYENA_EMB)))


def _filt_body(feat_ref, w1_ref, b1_ref, w2_ref, b2_ref, w3_ref, fr_ref, dl_ref, o_ref, l1_ref, h_scr,
               *, halves):
    j = pl.program_id(0)
    hp = functools.partial(jnp.dot, precision=HIGHEST, preferred_element_type=F32)
    feats = feat_ref[...]

    @pl.when(j == 0)
    def _():
        fr = fr_ref[...]
        h = jnp.sin(fr[0:1] * (hp(feats, w1_ref[...]) + b1_ref[...]))
        h_scr[...] = jnp.sin(fr[1:2] * (hp(h, w2_ref[...]) + b2_ref[...]))

    hc = hp(h_scr[...], w3_ref[...])
    t = feats[:, 0:1]
    hc = hc * jnp.exp(-t * dl_ref[...])
    row = lax.broadcasted_iota(jnp.int32, hc.shape, 0)
    hc = jnp.where((row == 0) & ((j // halves) % 2 == 1), 0.0, hc)
    l1_ref[0] = jnp.sum(jnp.abs(hc), axis=0, keepdims=True)
    o_ref[0] = hc


def _hyena_filter_taps(seq_len, f_w1, f_b1, f_w2, f_b2, f_w3, f_freq, c_hy):
    feats = _hyena_feats(seq_len)
    hid = f_w1.shape[1]
    w1 = jnp.pad(f_w1.astype(F32), ((0, 128 - HYENA_EMB), (0, 0)))
    deltas = jnp.abs(jnp.linspace(math.log(HYENA_TARGET) / HYENA_SLOW_DECAY,
                                  math.log(HYENA_TARGET) / HYENA_FAST_DECAY, c_hy, dtype=F32))[None, :]
    nblk = f_w3.shape[1] // c_hy
    halves = 2
    cb = c_hy // halves
    c0 = lambda a: pl.BlockSpec(a.shape, lambda j: (0, 0))
    b1, b2 = f_b1.reshape(1, hid), f_b2.reshape(1, hid)
    return pl.pallas_call(
        functools.partial(_filt_body, halves=halves),
        out_shape=[jax.ShapeDtypeStruct((nblk, seq_len, c_hy), F32),
                   jax.ShapeDtypeStruct((nblk, 1, c_hy), F32)],
        grid=(nblk * halves,),
        in_specs=[c0(feats), c0(w1), c0(b1), c0(f_w2), c0(b2),
                  pl.BlockSpec((hid, cb), lambda j: (0, j)), c0(f_freq),
                  pl.BlockSpec((1, cb), lambda j: (0, j % halves))],
        out_specs=[pl.BlockSpec((1, seq_len, cb), lambda j: (j // halves, 0, j % halves)),
                   pl.BlockSpec((1, 1, cb), lambda j: (j // halves, 0, j % halves))],
        scratch_shapes=[pltpu.VMEM((seq_len, hid), F32)],
        compiler_params=_cparams(("arbitrary",)),
        name="hyena_filter",
    )(feats, w1, b1, f_w2, b2, f_w3, f_freq, deltas)


def _conv_dft_constants():
    a_half = FFT_A // 2
    n = FFT_A * FFT_R
    ka = np.arange(FFT_KA)[:, None]
    a = np.arange(a_half)[None, :]
    ph = 2.0 * np.pi * ka * a / FFT_A
    m_fwd = np.zeros((2 * FFT_KA_PAD, a_half))
    m_fwd[:FFT_KA] = np.cos(ph)
    m_fwd[FFT_KA_PAD:FFT_KA_PAD + FFT_KA] = -np.sin(ph)
    wgt = np.where((ka == 0) | (ka == FFT_A // 2), 1.0, 2.0)
    m_inv = np.zeros((a_half, 2 * FFT_KA_PAD))
    m_inv[:, :FFT_KA] = (wgt * np.cos(ph)).T / n
    m_inv[:, FFT_KA_PAD:FFT_KA_PAD + FFT_KA] = (-wgt * np.sin(ph)).T / n
    kb = np.arange(FFT_R)[None, :, None]
    b = np.arange(FFT_R)[None, None, :]
    kaa = np.arange(FFT_KA)[:, None, None]
    th = 2.0 * np.pi * (b * kb / FFT_R + b * kaa / n)
    gr, gi = np.cos(th), -np.sin(th)
    g2 = np.zeros((FFT_KA_PAD, 2 * FFT_R, 2 * FFT_R))
    g2[:FFT_KA] = np.block([[gr, -gi], [gi, gr]])
    grt, git = gr.transpose(0, 2, 1), gi.transpose(0, 2, 1)
    g2h = np.zeros_like(g2)
    g2h[:FFT_KA] = np.block([[grt, git], [-git, grt]])
    return _mxu_const(m_fwd), _mxu_const(m_inv), _mxu_const(g2), _mxu_const(g2h)


FFT_NB = 16


def _fwd1_body(m_ref, u_ref, o_ref):
    u = jnp.concatenate([u_ref[0, :, bb, :] for bb in range(FFT_NB)], axis=1).astype(BF16)
    res = _dot(m_ref[...], u)
    o_ref[0, 0] = res[0:FFT_KA_PAD]
    o_ref[0, 1] = res[FFT_KA_PAD:2 * FFT_KA_PAD]


def _conv_fwd1(u, m_fwd):
    n, seq, c = u.shape
    a_half = FFT_A // 2
    return pl.pallas_call(
        _fwd1_body,
        out_shape=jax.ShapeDtypeStruct((n, 2, FFT_KA_PAD, FFT_R * c), F32),
        grid=(n, FFT_R // FFT_NB),
        in_specs=[pl.BlockSpec(m_fwd.shape, lambda i, j: (0, 0)),
                  pl.BlockSpec((1, a_half, FFT_NB, c), lambda i, j: (i, 0, j, 0))],
        out_specs=pl.BlockSpec((1, 2, FFT_KA_PAD, FFT_NB * c), lambda i, j: (i, 0, 0, j)),
        compiler_params=_cparams(("parallel", "parallel")),
        name="conv_fwd1",
    )(m_fwd, u.reshape(n, a_half, FFT_R, c))


FFT_KB = 8


def _rows_to_slabs(src_ref, dst_scr, c):
    for part in range(2):
        for b in range(FFT_R):
            dst_scr[part, :, b, :] = src_ref[0, part, :, b * c:(b + 1) * c]


def _slabs_to_rows(src_scr, dst_ref, c):
    for part in range(2):
        for b in range(FFT_R):
            dst_ref[0, part, :, b * c:(b + 1) * c] = src_scr[part, :, b, :]


def _slab(scr, i):
    return jnp.concatenate([scr[0, i], scr[1, i]], axis=0).astype(BF16)


def _fwd2f_body(sf_ref, sb_ref, g_ref, l1_ref, kf_ref, f3, b3):
    o = pl.program_id(0)
    j = pl.program_id(1)
    r2 = 2 * FFT_R
    c = kf_ref.shape[-1]
    _rows_to_slabs(sf_ref, f3, c)
    _rows_to_slabs(sb_ref, b3, c)
    inv = 1.0 / (l1_ref[2 * o] + l1_ref[2 * o + 1] + EPS)
    for i in range(FFT_KB):
        @pl.when(j * FFT_KB + i < FFT_KA)
        def _():
            xf = _dot(g_ref[i], _slab(f3, i))
            xb = _dot(g_ref[i], _slab(b3, i))
            kf_ref[0, i, 0:FFT_R] = (xf[0:FFT_R] + xb[0:FFT_R]) * inv
            kf_ref[0, i, FFT_R:r2] = (xf[FFT_R:r2] - xb[FFT_R:r2]) * inv

        @pl.when(j * FFT_KB + i >= FFT_KA)
        def _():
            kf_ref[0, i] = jnp.zeros((r2, c), F32)


def _filter_spectrum(s_filt, l1, g2, c):
    n_ord = s_filt.shape[0] // 2
    cols = s_filt.shape[-1]
    r2 = 2 * FFT_R
    return pl.pallas_call(
        _fwd2f_body,
        out_shape=jax.ShapeDtypeStruct((n_ord, FFT_KA_PAD, r2, c), F32),
        grid=(n_ord, FFT_KA_PAD // FFT_KB),
        in_specs=[pl.BlockSpec((1, 2, FFT_KB, cols), lambda o, j: (2 * o, 0, j, 0)),
                  pl.BlockSpec((1, 2, FFT_KB, cols), lambda o, j: (2 * o + 1, 0, j, 0)),
                  pl.BlockSpec((FFT_KB, r2, r2), lambda o, j: (j, 0, 0)),
                  pl.BlockSpec(l1.shape, lambda o, j: (0, 0, 0))],
        out_specs=pl.BlockSpec((1, FFT_KB, r2, c), lambda o, j: (o, j, 0, 0)),
        scratch_shapes=[pltpu.VMEM((2, FFT_KB, FFT_R, c), F32)] * 2,
        compiler_params=_cparams(("parallel", "parallel")),
        name="filter_spectrum",
    )(s_filt, s_filt, g2, l1)


def _mid_body(s_ref, g_ref, gh_ref, kf_ref, t_ref, s3, t3):
    j = pl.program_id(1)
    r2 = 2 * FFT_R
    c = kf_ref.shape[-1]
    _rows_to_slabs(s_ref, s3, c)
    for i in range(FFT_KB):
        @pl.when(j * FFT_KB + i < FFT_KA)
        def _():
            x = _dot(g_ref[i], _slab(s3, i))
            xr, xi = x[0:FFT_R], x[FFT_R:r2]
            kr, ki = kf_ref[0, i, 0:FFT_R], kf_ref[0, i, FFT_R:r2]
            y = jnp.concatenate([xr * kr - xi * ki, xr * ki + xi * kr], axis=0).astype(BF16)
            t = _dot(gh_ref[i], y)
            t3[0, i] = t[0:FFT_R]
            t3[1, i] = t[FFT_R:r2]

        @pl.when(j * FFT_KB + i >= FFT_KA)
        def _():
            t3[0, i] = jnp.zeros((FFT_R, c), F32)
            t3[1, i] = jnp.zeros((FFT_R, c), F32)

    _slabs_to_rows(t3, t_ref, c)


def _conv_mid(s, kf, order, g2, g2h, c):
    n, _, _, cols = s.shape
    r2 = 2 * FFT_R
    blk = pl.BlockSpec((1, 2, FFT_KB, cols), lambda i, j: (i, 0, j, 0))
    gspec = pl.BlockSpec((FFT_KB, r2, r2), lambda i, j: (j, 0, 0))
    return pl.pallas_call(
        _mid_body,
        out_shape=jax.ShapeDtypeStruct(s.shape, F32),
        grid=(n, FFT_KA_PAD // FFT_KB),
        in_specs=[blk, gspec, gspec,
                  pl.BlockSpec((1, FFT_KB, r2, c), lambda i, j: (order, j, 0, 0))],
        out_specs=blk,
        scratch_shapes=[pltpu.VMEM((2, FFT_KB, FFT_R, c), F32)] * 2,
        compiler_params=_cparams(("parallel", "parallel")),
        name="conv_mid",
    )(s, g2, g2h, kf)


def _inv1_body(m_ref, t_ref, u_ref, xg_ref, sk_ref, o_ref):
    c = u_ref.shape[-1]
    t2 = t_ref[0].reshape(2 * FFT_KA_PAD, FFT_NB * c).astype(BF16)
    y = _dot(m_ref[...], t2)
    for bb in range(FFT_NB):
        conv = y[:, bb * c:(bb + 1) * c] + u_ref[0, :, bb, :] * sk_ref[...]
        o_ref[0, :, bb, :] = xg_ref[0, :, bb, :] * conv


def _conv_inv1(t, u, xg, skip, m_inv):
    n, seq, c = u.shape
    a_half = FFT_A // 2
    sk = skip.astype(F32).reshape(1, c)
    uspec = pl.BlockSpec((1, a_half, FFT_NB, c), lambda i, j: (i, 0, j, 0))
    view = lambda a: a.reshape(n, a_half, FFT_R, c)
    out = pl.pallas_call(
        _inv1_body,
        out_shape=jax.ShapeDtypeStruct((n, a_half, FFT_R, c), F32),
        grid=(n, FFT_R // FFT_NB),
        in_specs=[pl.BlockSpec(m_inv.shape, lambda i, j: (0, 0)),
                  pl.BlockSpec((1, 2, FFT_KA_PAD, FFT_NB * c), lambda i, j: (i, 0, 0, j)),
                  uspec, uspec,
                  pl.BlockSpec((1, c), lambda i, j: (0, 0))],
        out_specs=uspec,
        compiler_params=_cparams(("parallel", "parallel")),
        name="conv_inv1",
    )(m_inv, t, view(u), view(xg), sk)
    return out.reshape(n, seq, c)


def _hyena(v, x1, x2, f_w1, f_b1, f_w2, f_b2, f_w3, f_freq, skip):
    _, seq, c = v.shape
    assert 2 * seq == FFT_A * FFT_R
    m_fwd, m_inv, g2, g2h = _conv_dft_constants()
    taps, l1 = _hyena_filter_taps(seq, f_w1, f_b1, f_w2, f_b2, f_w3, f_freq, c)
    kf = _filter_spectrum(_conv_fwd1(taps, m_fwd), l1, g2, c)
    y = v
    for order, xg in enumerate((x1, x2)):
        t = _conv_mid(_conv_fwd1(y, m_fwd), kf, order, g2, g2h, c)
        y = _conv_inv1(t, y, xg, skip[order], m_inv)
    return y


def _mix_ffn_body(x_ref, a1_ref, a2_ref, wm_ref, gm_ref, g_ref, sh_ref, sc_ref, gt_ref,
                  w1_ref, w3_ref, w2_ref, o_ref, x_scr, h_scr, acc_scr):
    j = pl.program_id(2)

    @pl.when(j == 0)
    def _():
        c1 = a1_ref.shape[-1]
        mixed = _dot(a1_ref[0].astype(BF16), wm_ref[0:c1]) + _dot(a2_ref[0].astype(BF16), wm_ref[c1:])
        xm = x_ref[0] + gm_ref[0] * mixed
        x_scr[...] = xm
        h_scr[...] = _norm_mod(xm, g_ref[...], sh_ref[0], sc_ref[0]).astype(BF16)
        acc_scr[...] = jnp.zeros_like(acc_scr)

    h = h_scr[...]
    a = _dot(h, w1_ref[...])
    u = (a * jax.nn.sigmoid(a) * _dot(h, w3_ref[...])).astype(BF16)
    acc_scr[...] += _dot(u, w2_ref[...])

    @pl.when(j == pl.num_programs(2) - 1)
    def _():
        o_ref[0] = x_scr[...] + gt_ref[0] * acc_scr[...]


def _mix_ffn(x, a1, a2, w_mix, gate_mix, g, shift, scale, gate, w1, w3, w2, tm=512, fb=1408):
    b, s, d = x.shape
    f = w1.shape[1]
    tok = lambda c: pl.BlockSpec((1, tm, c), lambda bi, i, j: (bi, i, 0))
    per_b = pl.BlockSpec((1, 1, d), lambda bi, i, j: (bi, 0, 0))
    const = lambda a: pl.BlockSpec(a.shape, lambda bi, i, j: (0, 0))
    return pl.pallas_call(
        _mix_ffn_body,
        out_shape=jax.ShapeDtypeStruct(x.shape, F32),
        grid=(b, s // tm, f // fb),
        in_specs=[tok(d), tok(a1.shape[-1]), tok(a2.shape[-1]), const(w_mix), per_b,
                  const(g), per_b, per_b, per_b,
                  pl.BlockSpec((d, fb), lambda bi, i, j: (0, j)),
                  pl.BlockSpec((d, fb), lambda bi, i, j: (0, j)),
                  pl.BlockSpec((fb, d), lambda bi, i, j: (j, 0))],
        out_specs=tok(d),
        scratch_shapes=[pltpu.VMEM((tm, d), F32), pltpu.VMEM((tm, d), BF16), pltpu.VMEM((tm, d), F32)],
        compiler_params=_cparams(("parallel", "parallel", "arbitrary")),
        name="mix_ffn",
    )(x, a1, a2, w_mix, gate_mix, g, shift, scale, gate, w1, w3, w2)


def _fm_constants(cg):
    j = np.arange(cg)[:, None]
    m = np.arange(cg)[None, :]
    ph = 2.0 * np.pi * j * m / cg
    w_cs = np.concatenate([np.cos(ph), np.sin(ph)], axis=1)
    d = np.arange(FM_A)[:, None]
    a = np.arange(FM_A)[None, :]
    ph = 2.0 * np.pi * d * a / FM_A
    fr, fi = np.cos(ph), -np.sin(ph)
    m1 = np.block([[fr, fi], [fi, -fr]])
    n = FM_A * FM_A
    dd = np.arange(FM_A)[:, None, None]
    c = np.arange(FM_A)[None, :, None]
    b = np.arange(FM_A)[None, None, :]
    th = 2.0 * np.pi * (b * c / FM_A + b * dd / n)
    gcat = np.concatenate([np.cos(th), np.sin(th)], axis=2)
    return _mxu_const(w_cs), _mxu_const(m1), _mxu_const(gcat)


def _fm_front_body(x_ref, g_ref, sh_ref, sc_ref, w_ref, m_ref, o_ref, *, cg, nb):
    d = x_ref.shape[-1]
    xs = jnp.concatenate([x_ref[0, :, bb, :] for bb in range(nb)], axis=0)
    h = _norm_mod(xs, g_ref[...], sh_ref[0], sc_ref[0]).astype(BF16)
    pq = [_dot(h[:, grp * cg:(grp + 1) * cg], w_ref[...]) for grp in range(d // cg)]
    p = jnp.concatenate([t[:, 0:cg] for t in pq], axis=1)
    q = jnp.concatenate([t[:, cg:2 * cg] for t in pq], axis=1)
    for bb in range(nb):
        rows = slice(bb * FM_A, (bb + 1) * FM_A)
        res = _dot(m_ref[...], jnp.concatenate([p[rows], q[rows]], axis=0).astype(BF16))
        o_ref[0, 0, :, bb, :] = res[0:FM_A]
        o_ref[0, 1, :, bb, :] = res[FM_A:2 * FM_A]


def _fm_front(x, g, shift, scale, w_cs, m1, nb=8):
    b, s, d = x.shape
    cg = w_cs.shape[0]
    per_b = pl.BlockSpec((1, 1, d), lambda bi, j: (bi, 0, 0))
    const = lambda a: pl.BlockSpec(a.shape, lambda bi, j: (0, 0))
    return pl.pallas_call(
        functools.partial(_fm_front_body, cg=cg, nb=nb),
        out_shape=jax.ShapeDtypeStruct((b, 2, FM_A, s // FM_A, d), F32),
        grid=(b, s // FM_A // nb),
        in_specs=[pl.BlockSpec((1, FM_A, nb, d), lambda bi, j: (bi, 0, j, 0)),
                  const(g), per_b, per_b, const(w_cs), const(m1)],
        out_specs=pl.BlockSpec((1, 2, FM_A, nb, d), lambda bi, j: (bi, 0, 0, j, 0)),
        compiler_params=_cparams(("parallel", "parallel")),
        name="fm_front",
    )(x.reshape(b, FM_A, s // FM_A, d), g, shift, scale, w_cs, m1)


def _fm_s2_body(s_ref, g_ref, o_ref, *, dblk, scale):
    for i in range(dblk):
        s2 = jnp.concatenate([s_ref[0, 0, i], s_ref[0, 1, i]], axis=0).astype(BF16)
        o_ref[:, i, :] = _dot(g_ref[i], s2) * scale


def _fm_stage2(sv, gcat, seq, d, dblk=8):
    b = sv.shape[0]
    scale = 1.0 / math.sqrt(seq * (d // F_GROUPS))
    out = pl.pallas_call(
        functools.partial(_fm_s2_body, dblk=dblk, scale=scale),
        out_shape=jax.ShapeDtypeStruct((b * FM_A, dblk * (FM_A // dblk), d), F32),
        grid=(b, FM_A // dblk),
        in_specs=[pl.BlockSpec((1, 2, dblk, FM_A, d), lambda bi, j: (bi, 0, j, 0, 0)),
                  pl.BlockSpec((dblk, FM_A, 2 * FM_A), lambda bi, j: (j, 0, 0))],
        out_specs=pl.BlockSpec((FM_A, dblk, d), lambda bi, j: (bi, j, 0)),
        compiler_params=_cparams(("parallel", "parallel")),
        name="fm_stage2",
    )(sv, gcat)
    return out.reshape(b, seq, d)


def _fourier_mix(x, g, shift, scale):
    b, s, d = x.shape
    assert s == FM_A * FM_A
    w_cs, m1, gcat = _fm_constants(d // F_GROUPS)
    return _fm_stage2(_fm_front(x, g, shift, scale, w_cs, m1), gcat, s, d)


LANES = 128
ROW_SL = 8
MOE_TM = 1024
DMA_WINDOW = 128


def _router_body(x_ref, yf_ref, wf_ref, gf_ref, g_ref, sh_ref, sc_ref, wr_ref, br_ref,
                 xo_ref, h_ref, meta_ref, gw_ref, cnt_ref, carry):
    i = pl.program_id(0)

    @pl.when(i == 0)
    def _():
        carry[...] = jnp.zeros_like(carry)

    xm = x_ref[...] + gf_ref[0] * _dot(yf_ref[...].astype(BF16), wf_ref[...])
    xo_ref[...] = xm
    h = _norm_mod(xm, g_ref[...], sh_ref[0], sc_ref[0])
    _rows_to_tiles(h_ref, h)
    h_hi = h.astype(BF16)
    h_lo = (h - h_hi.astype(F32)).astype(BF16)
    by_hi = _dot(h_hi, wr_ref[...])
    logits = by_hi[:, 0:LANES] + by_hi[:, LANES:] + _dot(h_lo, wr_ref[:, 0:LANES]) + br_ref[...]
    lane = lax.broadcasted_iota(jnp.int32, logits.shape, 1)
    nl = logits.shape[-1]
    m1 = jnp.max(logits, axis=-1, keepdims=True)
    i1 = jnp.min(jnp.where(logits == m1, lane, nl), axis=-1, keepdims=True)
    rest = jnp.where(lane == i1, -3.0e38, logits)
    m2 = jnp.max(rest, axis=-1, keepdims=True)
    i2 = jnp.min(jnp.where(rest == m2, lane, nl), axis=-1, keepdims=True)
    e = jnp.exp(m2 - m1)
    gw_ref[...] = jnp.where(lane == 0, 1.0 / (1.0 + e), jnp.where(lane == 1, e / (1.0 + e), 0.0))
    onehot = jnp.where((lane == i1) | (lane == i2), 1.0, 0.0)
    tm = onehot.shape[0]
    earlier = lax.broadcasted_iota(jnp.int32, (tm, tm), 0) > lax.broadcasted_iota(jnp.int32, (tm, tm), 1)
    excl = _dot(jnp.where(earlier, 1.0, 0.0).astype(BF16), onehot.astype(BF16)) + carry[...]
    r1 = jnp.sum(jnp.where(lane == i1, excl, 0.0), axis=-1, keepdims=True).astype(jnp.int32)
    r2 = jnp.sum(jnp.where(lane == i2, excl, 0.0), axis=-1, keepdims=True).astype(jnp.int32)
    meta_ref[...] = jnp.where(lane == 0, i1, jnp.where(lane == 1, i2, jnp.where(lane == 2, r1, jnp.where(lane == 3, r2, 0))))
    carry[...] = carry[...] + jnp.sum(onehot, axis=0, keepdims=True)
    cnt_ref[...] = carry[...]


def _router(x, y_f, w_f, gate_f, g, shift, scale, w_router, b_router, tm=512):
    b, s, d = x.shape
    t = b * s
    ne = w_router.shape[1]
    wr = jnp.pad(w_router.astype(F32), ((0, 0), (0, LANES - ne)))
    wr_hi = wr.astype(BF16)
    wr = jnp.concatenate([wr_hi, (wr - wr_hi.astype(F32)).astype(BF16)], axis=1)
    br = jnp.pad(b_router.astype(F32).reshape(1, ne), ((0, 0), (0, LANES - ne)), constant_values=NEG_INF)
    spt = s // tm
    per_b = pl.BlockSpec((1, 1, d), lambda i: (i // spt, 0, 0))
    const = lambda a: pl.BlockSpec(a.shape, lambda i: (0, 0))
    tok = pl.BlockSpec((tm, d), lambda i: (i, 0))
    return pl.pallas_call(
        _router_body,
        out_shape=[jax.ShapeDtypeStruct((t, d), F32),
                   jax.ShapeDtypeStruct((t * ROW_SL, LANES), F32),
                   jax.ShapeDtypeStruct((t, LANES), jnp.int32),
                   jax.ShapeDtypeStruct((t, LANES), F32),
                   jax.ShapeDtypeStruct((1, LANES), F32)],
        grid=(t // tm,),
        in_specs=[tok, tok, const(w_f), per_b, const(g), per_b, per_b, const(wr), const(br)],
        out_specs=[tok,
                   pl.BlockSpec((tm * ROW_SL, LANES), lambda i: (i, 0)),
                   pl.BlockSpec((tm, LANES), lambda i: (i, 0)),
                   pl.BlockSpec((tm, LANES), lambda i: (i, 0)),
                   pl.BlockSpec((1, LANES), lambda i: (0, 0))],
        scratch_shapes=[pltpu.VMEM((1, LANES), F32)],
        compiler_params=_cparams(("arbitrary",)),
        name="router",
    )(x.reshape(t, d), y_f.reshape(t, d), w_f, gate_f, g, shift, scale, wr, br)


def _moe_plan(meta, counts, ne, tm):
    i1, i2, r1, r2 = meta[:, 0], meta[:, 1], meta[:, 2], meta[:, 3]
    cnt = counts[0, :ne].astype(jnp.int32)
    padded = ((cnt + tm - 1) // tm) * tm
    ends = jnp.cumsum(padded)
    offs = ends - padded
    pick = lambda idx: sum(jnp.where(idx == e, offs[e], 0) for e in range(ne))
    pos = jnp.concatenate([pick(i1) + r1, pick(i2) + r2]).astype(jnp.int32)
    n_tiles = (2 * meta.shape[0]) // tm + ne
    n_used = (ends[ne - 1] // tm).astype(jnp.int32)
    tile_start = jnp.minimum(jnp.arange(n_tiles, dtype=jnp.int32), n_used - 1) * tm
    tile_expert = jnp.sum(tile_start[:, None] >= ends[None, :], axis=1).astype(jnp.int32)
    group_end = sum(jnp.where(tile_expert == e, offs[e] + cnt[e], 0) for e in range(ne))
    tile_rows = jnp.clip(group_end - tile_start, 0, tm).astype(jnp.int32)
    return pos, offs + cnt, padded - cnt, tile_expert, n_used.reshape(1), tile_rows


def _windowed_copies(n, start_copy, wait_one, per_iter):
    def body(i, carry):
        @pl.when(i >= DMA_WINDOW)
        def _():
            for _ in range(per_iter):
                wait_one()
        start_copy(i)
        return carry

    lax.fori_loop(0, n, body, 0)

    def drain(i, carry):
        for _ in range(per_iter):
            wait_one()
        return carry

    lax.fori_loop(0, jnp.minimum(n, DMA_WINDOW), drain, 0)


def _tile_of(ref, row):
    return ref.at[pl.ds(pl.multiple_of(row * ROW_SL, ROW_SL), ROW_SL)]


def _tiles_to_rows(ref, n, first=0):
    return jnp.concatenate([ref[pl.ds(first * ROW_SL + sl, n, stride=ROW_SL), :] for sl in range(ROW_SL)], axis=1)


def _rows_to_tiles(ref, val):
    n = val.shape[0]
    for sl in range(ROW_SL):
        ref[pl.ds(sl, n, stride=ROW_SL), :] = val[:, sl * LANES:(sl + 1) * LANES]


def _dispatch_body(pos_ref, pad_start_ref, pad_n_ref, h_ref, xs_hbm, sem, *, n_tok, ne):
    i = pl.program_id(0)
    td = h_ref.shape[0] // ROW_SL
    base = i * td
    copy = lambda src, dst: pltpu.make_async_copy(_tile_of(h_ref, src), _tile_of(xs_hbm, dst), sem)
    wait_one = lambda: copy(0, 0).wait()

    def start_token(r, carry):
        copy(r, pos_ref[base + r]).start(priority=0)
        copy(r, pos_ref[n_tok + base + r]).start(priority=1)
        return carry

    lax.fori_loop(0, td, start_token, 0, unroll=8)
    whole_tile = pltpu.make_async_copy(h_ref, xs_hbm.at[pl.ds(0, td * ROW_SL)], sem)
    whole_tile.wait()
    whole_tile.wait()

    @pl.when(i == 0)
    def _():
        for e in range(ne):
            first = pad_start_ref[e]
            _windowed_copies(pad_n_ref[e], lambda r: copy(0, first + r).start(), wait_one, 1)


def _moe_dispatch(h3, pos, pad_start, pad_n, n_rows, td=1024):
    n_tok = h3.shape[0] // ROW_SL
    ne = pad_start.shape[0]
    return pl.pallas_call(
        functools.partial(_dispatch_body, n_tok=n_tok, ne=ne),
        out_shape=jax.ShapeDtypeStruct((n_rows * ROW_SL, LANES), h3.dtype),
        grid_spec=pltpu.PrefetchScalarGridSpec(
            num_scalar_prefetch=3, grid=(n_tok // td,),
            in_specs=[pl.BlockSpec((td * ROW_SL, LANES), lambda i, p, ps, pn: (i, 0))],
            out_specs=pl.BlockSpec(memory_space=pl.ANY),
            scratch_shapes=[pltpu.SemaphoreType.DMA(())]),
        compiler_params=_cparams(("arbitrary",)),
        name="moe_dispatch",
    )(pos, pad_start, pad_n, h3)


def _moe_grouped_body(te_ref, nu_ref, tr_ref, xs_ref, w1_ref, w3_ref, w2_ref, y_ref, xb_scr, acc_scr):
    i = pl.program_id(0)
    j = pl.program_id(1)
    tm = xb_scr.shape[0]
    hm = tm // 2

    def expert_rows(nrows):
        h = xb_scr[0:nrows]
        a = _dot(h, w1_ref[0].astype(BF16))
        u = (a * jax.nn.sigmoid(a) * _dot(h, w3_ref[0].astype(BF16))).astype(BF16)
        part = _dot(u, w2_ref[0].astype(BF16))

        @pl.when(j == 0)
        def _():
            acc_scr[0:nrows] = part

        @pl.when(j > 0)
        def _():
            acc_scr[0:nrows] += part

    @pl.when(i < nu_ref[0])
    def _():
        @pl.when(j == 0)
        def _():
            xb_scr[...] = _tiles_to_rows(xs_ref, tm).astype(BF16)

        @pl.when(tr_ref[i] > hm)
        def _():
            expert_rows(tm)

        @pl.when(tr_ref[i] <= hm)
        def _():
            expert_rows(hm)

            @pl.when(j == 0)
            def _():
                acc_scr[hm:tm] = jnp.zeros((tm - hm, acc_scr.shape[1]), F32)

        @pl.when(j == pl.num_programs(1) - 1)
        def _():
            _rows_to_tiles(y_ref, acc_scr[...])


def _moe_grouped(xs, tile_expert, n_used, tile_rows, w1, w3, w2, tm, fb=512):
    ne, d, f = w1.shape
    n_rows = xs.shape[0] // ROW_SL
    nj = f // fb
    row_tile = lambda i, j, te, nu, tr: (jnp.maximum(jnp.minimum(i, nu[0] - 1), 0), 0)
    jj = lambda i, j, nu: jnp.where(i < nu[0], j, nj - 1)
    return pl.pallas_call(
        _moe_grouped_body,
        out_shape=jax.ShapeDtypeStruct(xs.shape, F32),
        grid_spec=pltpu.PrefetchScalarGridSpec(
            num_scalar_prefetch=3, grid=(n_rows // tm, nj),
            in_specs=[pl.BlockSpec((tm * ROW_SL, LANES), row_tile),
                      pl.BlockSpec((1, d, fb), lambda i, j, te, nu, tr: (te[i], 0, jj(i, j, nu))),
                      pl.BlockSpec((1, d, fb), lambda i, j, te, nu, tr: (te[i], 0, jj(i, j, nu))),
                      pl.BlockSpec((1, fb, d), lambda i, j, te, nu, tr: (te[i], jj(i, j, nu), 0))],
            out_specs=pl.BlockSpec((tm * ROW_SL, LANES), row_tile),
            scratch_shapes=[pltpu.VMEM((tm, d), BF16), pltpu.VMEM((tm, d), F32)]),
        compiler_params=_cparams(("arbitrary", "arbitrary")),
        name="moe_grouped",
    )(tile_expert, n_used, tile_rows, xs, w1, w3, w2)


def _moe_final_body(pos_ref, x_ref, y_hbm, gw_ref, gt_ref, fg_ref, o_ref, yg_scr, sem, *, n_tok):
    i = pl.program_id(0)
    tc = x_ref.shape[0]
    slot = i % 2

    def gather_tile(step, into):
        base = step * tc

        def start_token(r, carry):
            dst = yg_scr.at[into]
            pltpu.make_async_copy(_tile_of(y_hbm, pos_ref[base + r]), _tile_of(dst, r),
                                  sem.at[into]).start(priority=0)
            pltpu.make_async_copy(_tile_of(y_hbm, pos_ref[n_tok + base + r]), _tile_of(dst, tc + r),
                                  sem.at[into]).start(priority=1)
            return carry

        lax.fori_loop(0, tc, start_token, 0, unroll=8)

    @pl.when(i == 0)
    def _():
        gather_tile(0, 0)

    @pl.when(i + 1 < pl.num_programs(0))
    def _():
        gather_tile(i + 1, 1 - slot)

    pltpu.make_async_copy(y_hbm.at[pl.ds(0, 2 * tc * ROW_SL)], yg_scr.at[slot], sem.at[slot]).wait()
    gw = gw_ref[...]
    rows = yg_scr.at[slot]
    y = gw[:, 0:1] * _tiles_to_rows(rows, tc) + gw[:, 1:2] * _tiles_to_rows(rows, tc, first=tc)
    xo = x_ref[...] + gt_ref[0] * y
    ms = jnp.mean(xo * xo, axis=-1, keepdims=True)
    o_ref[...] = xo * lax.rsqrt(ms + EPS) * fg_ref[...]


def _moe_final(x, y, pos, gw, gt, final_g, tc=512):
    b, s, d = x.shape
    t = b * s
    spt = s // tc
    nsl = d // LANES
    out = pl.pallas_call(
        functools.partial(_moe_final_body, n_tok=t),
        out_shape=jax.ShapeDtypeStruct((t, d), F32),
        grid_spec=pltpu.PrefetchScalarGridSpec(
            num_scalar_prefetch=1, grid=(t // tc,),
            in_specs=[pl.BlockSpec((tc, d), lambda i, p: (i, 0)),
                      pl.BlockSpec(memory_space=pl.ANY),
                      pl.BlockSpec((tc, LANES), lambda i, p: (i, 0)),
                      pl.BlockSpec((1, 1, d), lambda i, p: (i // spt, 0, 0)),
                      pl.BlockSpec(final_g.shape, lambda i, p: (0, 0))],
            out_specs=pl.BlockSpec((tc, d), lambda i, p: (i, 0)),
            scratch_shapes=[pltpu.VMEM((2, 2 * tc * ROW_SL, LANES), F32), pltpu.SemaphoreType.DMA((2,))]),
        compiler_params=_cparams(("arbitrary",)),
        name="moe_final",
    )(pos, x.reshape(t, d), y, gw, gt, final_g)
    return out.reshape(b, s, d)


def _moe_routed(x, y_f, w_f, gate_f, g, shift, scale, gt, final_g, w_router, b_router, w1, w3, w2):
    ne = w1.shape[0]
    tm = MOE_TM
    x1, h3, meta, gw, counts = _router(x, y_f, w_f, gate_f, g, shift, scale, w_router, b_router)
    pos, pad_start, pad_n, tile_expert, n_used, tile_rows = _moe_plan(meta, counts, ne, tm)
    assert x.shape[-1] == ROW_SL * LANES
    n_rows = (2 * (h3.shape[0] // ROW_SL) // tm + ne) * tm
    xs = _moe_dispatch(h3, pos, pad_start, pad_n, n_rows)
    y = _moe_grouped(xs, tile_expert, n_used, tile_rows, w1, w3, w2, tm)
    return _moe_final(x1.reshape(x.shape), y, pos, gw, gt, final_g)


def kernel(x, c, ctx, c_ctx, w_ada, b_ada, norm_g, w_in, hy_short_w, hy_short_b, hy_f_w1, hy_f_b1, hy_f_w2, hy_f_b2, hy_f_w3, hy_f_freq, hy_skip, na_rpb, w_mix_out, ffn_w1, ffn_w3, ffn_w2, w_fourier, w_router, b_router, moe_w1, moe_w3, moe_w2, final_g):
    b, s, d = x.shape
    depth = w_ada.shape[0]
    assert depth == 2, "layer 0 mixes with Hyena/attention, layer 1 with Fourier/MoE"
    c_hy = hy_skip.shape[-1]
    c_na = d - c_hy

    cvec = jnp.concatenate([c, c_ctx[None, :], jnp.zeros((8 - b - 1, d), F32)], axis=0)
    mods = _ada(cvec, w_ada, b_ada)

    def mod(layer, idx, ctx_row=False):
        m = mods[layer, :, idx * d:(idx + 1) * d]
        return m[b:b + 1, None, :] if ctx_row else m[0:b, None, :]

    row = lambda a: a.reshape(1, -1)

    w_in0 = w_in[0].astype(BF16)
    w_hy, w_qkv = w_in0[:, 0:3 * c_hy], w_in0[:, 3 * c_hy:]
    v, x1, x2, q, k, va = _inproj(x, row(norm_g[0, 0]), mod(0, 0), mod(0, 1), w_hy, w_qkv,
                                  hy_short_w[0], row(hy_short_b[0]))
    kc, vc = _ctxkv(ctx, row(norm_g[0, 0]), mod(0, 0, True), mod(0, 1, True), w_qkv[:, c_na:])
    y_na = _natt(q, k, va, kc, vc, _na_bias_table(na_rpb[0]))
    y_hy = _hyena(v, x1, x2, hy_f_w1[0], hy_f_b1[0], hy_f_w2[0], hy_f_b2[0], hy_f_w3[0],
                  hy_f_freq[0], hy_skip[0])
    x = _mix_ffn(x, y_hy, y_na, w_mix_out[0].astype(BF16), mod(0, 2),
                 row(norm_g[0, 1]), mod(0, 3), mod(0, 4), mod(0, 5),
                 ffn_w1[0].astype(BF16), ffn_w3[0].astype(BF16), ffn_w2[0].astype(BF16))

    y_f = _fourier_mix(x, row(norm_g[1, 0]), mod(1, 0), mod(1, 1))
    return _moe_routed(x, y_f, w_fourier[0].astype(BF16), mod(1, 2),
                       row(norm_g[1, 1]), mod(1, 3), mod(1, 4), mod(1, 5), row(final_g),
                       w_router[0], b_router[0],
                       moe_w1[0], moe_w3[0], moe_w2[0])
```

```python
import functools
import math

import numpy as np
import jax
import jax.numpy as jnp
from jax import lax
from jax.experimental import pallas as pl
from jax.experimental.pallas import tpu as pltpu

F32 = jnp.float32
BF16 = jnp.bfloat16
HIGHEST = lax.Precision.HIGHEST

GRID_W = 64
NA_HEAD_DIM = 32
NA_WIN_R = 8
NA_WIN_C = 16
HYENA_EMB = 33
HYENA_BANDS = (HYENA_EMB - 1) // 2
HYENA_FAST_DECAY = 0.3
HYENA_SLOW_DECAY = 1.5
HYENA_TARGET = 1e-2
F_GROUPS = 4
N_MOD = 6
EPS = 1e-6
NEG_INF = -1e30

FFT_A = 64
FFT_R = 128
FFT_KA = FFT_A // 2 + 1
FFT_KA_PAD = 40
FM_A = 64

VMEM_LIMIT = 48 * 1024 * 1024


def _cparams(sem, vmem_limit=VMEM_LIMIT):
    return pltpu.CompilerParams(dimension_semantics=sem, vmem_limit_bytes=vmem_limit)


def _dot(a, b):
    return jnp.dot(a, b, preferred_element_type=F32)


def _mxu_const(m):
    return jnp.asarray(m, dtype=F32).astype(BF16)


def _norm_mod(x, g, shift, scale):
    ms = jnp.mean(x * x, axis=-1, keepdims=True)
    y = x * lax.rsqrt(ms + EPS) * g
    return y * (1.0 + scale) + shift


def _ada_body(c_ref, w_ref, b_ref, o_ref):
    cv = c_ref[...]
    s = cv * jax.nn.sigmoid(cv)
    o_ref[0] = jnp.dot(s, w_ref[0], precision=HIGHEST, preferred_element_type=F32) + b_ref[0]


def _ada(cvec, w_ada, b_ada):
    depth, d, n = w_ada.shape
    rows = cvec.shape[0]
    bn = n // 4
    return pl.pallas_call(
        _ada_body,
        out_shape=jax.ShapeDtypeStruct((depth, rows, n), F32),
        grid=(depth, n // bn),
        in_specs=[pl.BlockSpec((rows, d), lambda l, j: (0, 0)),
                  pl.BlockSpec((1, d, bn), lambda l, j: (l, 0, j)),
                  pl.BlockSpec((1, 1, bn), lambda l, j: (l, 0, j))],
        out_specs=pl.BlockSpec((1, rows, bn), lambda l, j: (l, 0, j)),
        compiler_params=_cparams(("parallel", "parallel")),
        name="ada",
    )(cvec, w_ada, b_ada.reshape(depth, 1, n))


def _inproj_body(x_ref, xp_ref, xn_ref, g_ref, sh_ref, sc_ref, why_ref, wqkv_ref, sw_ref, sb_ref,
                 v_ref, x1_ref, x2_ref, q_ref, k_ref, va_ref, *, n_tiles, q_scale, c_hy, c_na):
    i = pl.program_id(1)
    g, sh, sc = g_ref[...], sh_ref[0], sc_ref[0]
    hf = _norm_mod(x_ref[0], g, sh, sc)
    h = hf.astype(BF16)
    tm = hf.shape[0]
    hx = jnp.concatenate([_norm_mod(xp_ref[0], g, sh, sc), hf, _norm_mod(xn_ref[0], g, sh, sc)], axis=0)
    zx = _dot(hx.astype(BF16), why_ref[...])
    zh = zx[8:8 + tm]
    zp = jnp.where(i > 0, zx[7:8], 0.0)
    zn = jnp.where(i < n_tiles - 1, zx[8 + tm:9 + tm], 0.0)
    row = lax.broadcasted_iota(jnp.int32, zh.shape, 0)
    z_m1 = jnp.where(row == 0, zp, pltpu.roll(zh, 1, 0))
    z_p1 = jnp.where(row == tm - 1, zn, pltpu.roll(zh, tm - 1, 0))
    sw = sw_ref[...]
    zc = z_m1 * sw[0:1] + zh * sw[1:2] + z_p1 * sw[2:3] + sb_ref[...]
    v_ref[0] = zc[:, 0:c_hy]
    x1_ref[0] = zc[:, c_hy:2 * c_hy]
    x2_ref[0] = zc[:, 2 * c_hy:3 * c_hy]
    zq = _dot(h, wqkv_ref[...])
    q_ref[0] = (zq[:, 0:c_na] * q_scale).astype(BF16)
    k_ref[0] = zq[:, c_na:2 * c_na].astype(BF16)
    va_ref[0] = zq[:, 2 * c_na:3 * c_na].astype(BF16)


def _inproj(x, g, shift, scale, w_hy, w_qkv, short_w, short_b, tm=512):
    b, s, d = x.shape
    c_hy = w_hy.shape[1] // 3
    c_na = w_qkv.shape[1] // 3
    n_tiles = s // tm
    r8 = tm // 8
    body = functools.partial(_inproj_body, n_tiles=n_tiles, q_scale=NA_HEAD_DIM ** -0.5,
                             c_hy=c_hy, c_na=c_na)
    tok = lambda c: pl.BlockSpec((1, tm, c), lambda bi, i: (bi, i, 0))
    full2 = lambda a: pl.BlockSpec(a.shape, lambda bi, i: (0, 0))
    per_b = pl.BlockSpec((1, 1, d), lambda bi, i: (bi, 0, 0))
    return pl.pallas_call(
        body,
        out_shape=[jax.ShapeDtypeStruct((b, s, c_hy), F32)] * 3 + [jax.ShapeDtypeStruct((b, s, c_na), BF16)] * 3,
        grid=(b, n_tiles),
        in_specs=[tok(d),
                  pl.BlockSpec((1, 8, d), lambda bi, i: (bi, jnp.maximum(i * r8 - 1, 0), 0)),
                  pl.BlockSpec((1, 8, d), lambda bi, i: (bi, jnp.minimum((i + 1) * r8, s // 8 - 1), 0)),
                  full2(g), per_b, per_b, full2(w_hy), full2(w_qkv), full2(short_w), full2(short_b)],
        out_specs=[tok(c_hy)] * 3 + [tok(c_na)] * 3,
        compiler_params=_cparams(("parallel", "parallel")),
        name="inproj",
    )(x, x, x, g, shift, scale, w_hy, w_qkv, short_w, short_b)


def _ctxkv_body(x_ref, g_ref, sh_ref, sc_ref, w_ref, k_ref, v_ref, *, c_na):
    h = _norm_mod(x_ref[0], g_ref[...], sh_ref[0], sc_ref[0]).astype(BF16)
    z = _dot(h, w_ref[...])
    k_ref[0] = z[:, 0:c_na].astype(BF16)
    v_ref[0] = z[:, c_na:2 * c_na].astype(BF16)


def _ctxkv(ctx, g, shift, scale, w_kv):
    b, n, d = ctx.shape
    c_na = w_kv.shape[1] // 2
    one = pl.BlockSpec((1, 1, d), lambda bi: (0, 0, 0))
    return pl.pallas_call(
        functools.partial(_ctxkv_body, c_na=c_na),
        out_shape=[jax.ShapeDtypeStruct((b, n, c_na), BF16)] * 2,
        grid=(b,),
        in_specs=[pl.BlockSpec((1, n, d), lambda bi: (bi, 0, 0)),
                  pl.BlockSpec(g.shape, lambda bi: (0, 0)), one, one,
                  pl.BlockSpec(w_kv.shape, lambda bi: (0, 0))],
        out_specs=[pl.BlockSpec((1, n, c_na), lambda bi: (bi, 0, 0))] * 2,
        compiler_params=_cparams(("parallel",)),
        name="ctxkv",
    )(ctx, g, shift, scale, w_kv)


NA_HEADS_PER_BLK = 8
NA_ROWS_PER_STEP = 4


def _na_bias_body(r_ref, e_ref, ok_ref, o_ref):
    t = jnp.dot(r_ref[...], e_ref[...], precision=HIGHEST, preferred_element_type=F32)
    o_ref[...] = jnp.where(ok_ref[...] > 0.5, t, NEG_INF)


def _na_bias_table(rpb):
    w = GRID_W
    h, nr, nc = rpb.shape
    col = np.arange(w)[:, None]
    kc = np.arange(w)[None, :]
    c_start = np.clip(col - NA_WIN_C // 2, 0, w - NA_WIN_C)
    valid = ((kc >= c_start) & (kc < c_start + NA_WIN_C)).reshape(1, w * w)
    expand = (np.arange(32)[:, None, None] == (kc - col + NA_WIN_C - 1)[None]).reshape(32, w * w)
    rp = jnp.pad(rpb.astype(F32).reshape(h * nr, nc), ((0, 0), (0, 32 - nc)))
    full = lambda a: pl.BlockSpec(a.shape, lambda: (0,) * a.ndim)
    expand = jnp.asarray(expand, dtype=F32)
    ok = jnp.asarray(valid, dtype=F32)
    toep = pl.pallas_call(
        _na_bias_body,
        out_shape=jax.ShapeDtypeStruct((h * nr, w * w), F32),
        in_specs=[full(rp), full(expand), full(ok)],
        out_specs=pl.BlockSpec((h * nr, w * w), lambda: (0, 0)),
        name="na_bias",
    )(rp, expand, ok)
    t2 = toep.reshape(h, nr, w, w).transpose(0, 2, 1, 3).reshape(h, w, nr * w)
    slabs = jnp.stack([t2[:, :, (NA_WIN_R - 1 - off) * w:(2 * NA_WIN_R - 1 - off) * w]
                       for off in range(NA_WIN_R)], axis=1)
    hpb = NA_HEADS_PER_BLK
    slabs = slabs.reshape(h // hpb, hpb, NA_WIN_R, w, NA_WIN_R * w).transpose(0, 2, 1, 3, 4)
    return slabs.reshape(h // hpb, NA_WIN_R, hpb * w, NA_WIN_R * w)


def _natt_body(q_ref, k_ref, v_ref, kc_ref, vc_ref, bias_ref, o_ref, *, rows):
    w = GRID_W
    hpb = NA_HEADS_PER_BLK
    nloc = NA_WIN_R * w
    lane = lax.broadcasted_iota(jnp.int32, (1, hpb * NA_HEAD_DIM), 1)
    in_head = [(lane >= NA_HEAD_DIM * hh) & (lane < NA_HEAD_DIM * (hh + 1)) for hh in range(hpb)]
    kcx = kc_ref[0]
    vcx = vc_ref[0]
    nt = (((1,), (1,)), ((), ()))

    def one_row(r):
        r0 = jnp.clip(r - NA_WIN_R // 2, 0, rows - NA_WIN_R)
        off = r - r0
        qs = q_ref[0, pl.ds(pl.multiple_of(r * w, w), w), :]
        kw = k_ref[0, pl.ds(pl.multiple_of(r0 * w, w), nloc), :]
        vw = v_ref[0, pl.ds(pl.multiple_of(r0 * w, w), nloc), :]
        zero = jnp.zeros_like(qs)
        qst = jnp.concatenate([jnp.where(m, qs, zero) for m in in_head], axis=0)
        s_loc = lax.dot_general(qst, kw, nt, preferred_element_type=F32) + bias_ref[0, off]
        s_ctx = lax.dot_general(qst, kcx, nt, preferred_element_type=F32)
        m = jnp.maximum(jnp.max(s_loc, axis=-1, keepdims=True), jnp.max(s_ctx, axis=-1, keepdims=True))
        p_loc = jnp.exp(s_loc - m)
        p_ctx = jnp.exp(s_ctx - m)
        den = jnp.sum(p_loc, axis=-1, keepdims=True) + jnp.sum(p_ctx, axis=-1, keepdims=True)
        o = (_dot(p_loc.astype(BF16), vw) + _dot(p_ctx.astype(BF16), vcx)) * (1.0 / den)
        acc = jnp.where(in_head[0], o[0:w], 0.0)
        for hh in range(1, hpb):
            acc = acc + jnp.where(in_head[hh], o[hh * w:(hh + 1) * w], 0.0)
        o_ref[0, pl.ds(pl.multiple_of(r * w, w), w), :] = acc.astype(BF16)

    def row_group(i, carry):
        for r in range(NA_ROWS_PER_STEP):
            one_row(NA_ROWS_PER_STEP * i + r)
        return carry

    lax.fori_loop(0, rows // NA_ROWS_PER_STEP, row_group, 0)


def _natt(q, k, v, kc, vc, bias):
    b, s, c = q.shape
    nctx = kc.shape[1]
    lw = NA_HEADS_PER_BLK * NA_HEAD_DIM
    rows = s // GRID_W
    seq = pl.BlockSpec((1, s, lw), lambda bi, g: (bi, 0, g))
    cx = pl.BlockSpec((1, nctx, lw), lambda bi, g: (bi, 0, g))
    return pl.pallas_call(
        functools.partial(_natt_body, rows=rows),
        out_shape=jax.ShapeDtypeStruct((b, s, c), BF16),
        grid=(b, c // lw),
        in_specs=[seq, seq, seq, cx, cx,
                  pl.BlockSpec((1,) + bias.shape[1:], lambda bi, g: (g, 0, 0, 0))],
        out_specs=seq,
        compiler_params=_cparams(("parallel", "parallel")),
        name="natt",
    )(q, k, v, kc, vc, bias)


def _hyena_feats(seq_len):
    t = jnp.linspace(0.0, 1.0, seq_len, dtype=F32)[:, None]
    bands = jnp.linspace(1e-4, HYENA_BANDS - 1, HYENA_BANDS, dtype=F32)
    ang = (2.0 * math.pi / seq_len) * jnp.arange(seq_len, dtype=F32)[:, None] * bands[None, :]
    feats = jnp.concatenate([t, jnp.cos(ang), -jnp.sin(ang)], axis=-1)
    return jnp.pad(feats, ((0, 0), (0, 128 - HYENA_EMB)))


def _filt_body(feat_ref, w1_ref, b1_ref, w2_ref, b2_ref, w3_ref, fr_ref, dl_ref, o_ref, l1_ref, h_scr,
               *, halves):
    j = pl.program_id(0)
    hp = functools.partial(jnp.dot, precision=HIGHEST, preferred_element_type=F32)
    feats = feat_ref[...]

    @pl.when(j == 0)
    def _():
        fr = fr_ref[...]
        h = jnp.sin(fr[0:1] * (hp(feats, w1_ref[...]) + b1_ref[...]))
        h_scr[...] = jnp.sin(fr[1:2] * (hp(h, w2_ref[...]) + b2_ref[...]))

    hc = hp(h_scr[...], w3_ref[...])
    t = feats[:, 0:1]
    hc = hc * jnp.exp(-t * dl_ref[...])
    row = lax.broadcasted_iota(jnp.int32, hc.shape, 0)
    hc = jnp.where((row == 0) & ((j // halves) % 2 == 1), 0.0, hc)
    l1_ref[0] = jnp.sum(jnp.abs(hc), axis=0, keepdims=True)
    o_ref[0] = hc


def _hyena_filter_taps(seq_len, f_w1, f_b1, f_w2, f_b2, f_w3, f_freq, c_hy):
    feats = _hyena_feats(seq_len)
    hid = f_w1.shape[1]
    w1 = jnp.pad(f_w1.astype(F32), ((0, 128 - HYENA_EMB), (0, 0)))
    deltas = jnp.abs(jnp.linspace(math.log(HYENA_TARGET) / HYENA_SLOW_DECAY,
                                  math.log(HYENA_TARGET) / HYENA_FAST_DECAY, c_hy, dtype=F32))[None, :]
    nblk = f_w3.shape[1] // c_hy
    halves = 2
    cb = c_hy // halves
    c0 = lambda a: pl.BlockSpec(a.shape, lambda j: (0, 0))
    b1, b2 = f_b1.reshape(1, hid), f_b2.reshape(1, hid)
    return pl.pallas_call(
        functools.partial(_filt_body, halves=halves),
        out_shape=[jax.ShapeDtypeStruct((nblk, seq_len, c_hy), F32),
                   jax.ShapeDtypeStruct((nblk, 1, c_hy), F32)],
        grid=(nblk * halves,),
        in_specs=[c0(feats), c0(w1), c0(b1), c0(f_w2), c0(b2),
                  pl.BlockSpec((hid, cb), lambda j: (0, j)), c0(f_freq),
                  pl.BlockSpec((1, cb), lambda j: (0, j % halves))],
        out_specs=[pl.BlockSpec((1, seq_len, cb), lambda j: (j // halves, 0, j % halves)),
                   pl.BlockSpec((1, 1, cb), lambda j: (j // halves, 0, j % halves))],
        scratch_shapes=[pltpu.VMEM((seq_len, hid), F32)],
        compiler_params=_cparams(("arbitrary",)),
        name="hyena_filter",
    )(feats, w1, b1, f_w2, b2, f_w3, f_freq, deltas)


def _conv_dft_constants():
    a_half = FFT_A // 2
    n = FFT_A * FFT_R
    ka = np.arange(FFT_KA)[:, None]
    a = np.arange(a_half)[None, :]
    ph = 2.0 * np.pi * ka * a / FFT_A
    m_fwd = np.zeros((2 * FFT_KA_PAD, a_half))
    m_fwd[:FFT_KA] = np.cos(ph)
    m_fwd[FFT_KA_PAD:FFT_KA_PAD + FFT_KA] = -np.sin(ph)
    wgt = np.where((ka == 0) | (ka == FFT_A // 2), 1.0, 2.0)
    m_inv = np.zeros((a_half, 2 * FFT_KA_PAD))
    m_inv[:, :FFT_KA] = (wgt * np.cos(ph)).T / n
    m_inv[:, FFT_KA_PAD:FFT_KA_PAD + FFT_KA] = (-wgt * np.sin(ph)).T / n
    kb = np.arange(FFT_R)[None, :, None]
    b = np.arange(FFT_R)[None, None, :]
    kaa = np.arange(FFT_KA)[:, None, None]
    th = 2.0 * np.pi * (b * kb / FFT_R + b * kaa / n)
    gr, gi = np.cos(th), -np.sin(th)
    g2 = np.zeros((FFT_KA_PAD, 2 * FFT_R, 2 * FFT_R))
    g2[:FFT_KA] = np.block([[gr, -gi], [gi, gr]])
    grt, git = gr.transpose(0, 2, 1), gi.transpose(0, 2, 1)
    g2h = np.zeros_like(g2)
    g2h[:FFT_KA] = np.block([[grt, git], [-git, grt]])
    return _mxu_const(m_fwd), _mxu_const(m_inv), _mxu_const(g2), _mxu_const(g2h)


FFT_NB = 16


def _fwd1_body(m_ref, u_ref, o_ref):
    u = jnp.concatenate([u_ref[0, :, bb, :] for bb in range(FFT_NB)], axis=1).astype(BF16)
    res = _dot(m_ref[...], u)
    o_ref[0, 0] = res[0:FFT_KA_PAD]
    o_ref[0, 1] = res[FFT_KA_PAD:2 * FFT_KA_PAD]


def _conv_fwd1(u, m_fwd):
    n, seq, c = u.shape
    a_half = FFT_A // 2
    return pl.pallas_call(
        _fwd1_body,
        out_shape=jax.ShapeDtypeStruct((n, 2, FFT_KA_PAD, FFT_R * c), F32),
        grid=(n, FFT_R // FFT_NB),
        in_specs=[pl.BlockSpec(m_fwd.shape, lambda i, j: (0, 0)),
                  pl.BlockSpec((1, a_half, FFT_NB, c), lambda i, j: (i, 0, j, 0))],
        out_specs=pl.BlockSpec((1, 2, FFT_KA_PAD, FFT_NB * c), lambda i, j: (i, 0, 0, j)),
        compiler_params=_cparams(("parallel", "parallel")),
        name="conv_fwd1",
    )(m_fwd, u.reshape(n, a_half, FFT_R, c))


FFT_KB = 8


def _rows_to_slabs(src_ref, dst_scr, c):
    for part in range(2):
        for b in range(FFT_R):
            dst_scr[part, :, b, :] = src_ref[0, part, :, b * c:(b + 1) * c]


def _slabs_to_rows(src_scr, dst_ref, c):
    for part in range(2):
        for b in range(FFT_R):
            dst_ref[0, part, :, b * c:(b + 1) * c] = src_scr[part, :, b, :]


def _slab(scr, i):
    return jnp.concatenate([scr[0, i], scr[1, i]], axis=0).astype(BF16)


def _fwd2f_body(sf_ref, sb_ref, g_ref, l1_ref, kf_ref, f3, b3):
    o = pl.program_id(0)
    j = pl.program_id(1)
    r2 = 2 * FFT_R
    c = kf_ref.shape[-1]
    _rows_to_slabs(sf_ref, f3, c)
    _rows_to_slabs(sb_ref, b3, c)
    inv = 1.0 / (l1_ref[2 * o] + l1_ref[2 * o + 1] + EPS)
    for i in range(FFT_KB):
        @pl.when(j * FFT_KB + i < FFT_KA)
        def _():
            xf = _dot(g_ref[i], _slab(f3, i))
            xb = _dot(g_ref[i], _slab(b3, i))
            kf_ref[0, i, 0:FFT_R] = (xf[0:FFT_R] + xb[0:FFT_R]) * inv
            kf_ref[0, i, FFT_R:r2] = (xf[FFT_R:r2] - xb[FFT_R:r2]) * inv

        @pl.when(j * FFT_KB + i >= FFT_KA)
        def _():
            kf_ref[0, i] = jnp.zeros((r2, c), F32)


def _filter_spectrum(s_filt, l1, g2, c):
    n_ord = s_filt.shape[0] // 2
    cols = s_filt.shape[-1]
    r2 = 2 * FFT_R
    return pl.pallas_call(
        _fwd2f_body,
        out_shape=jax.ShapeDtypeStruct((n_ord, FFT_KA_PAD, r2, c), F32),
        grid=(n_ord, FFT_KA_PAD // FFT_KB),
        in_specs=[pl.BlockSpec((1, 2, FFT_KB, cols), lambda o, j: (2 * o, 0, j, 0)),
                  pl.BlockSpec((1, 2, FFT_KB, cols), lambda o, j: (2 * o + 1, 0, j, 0)),
                  pl.BlockSpec((FFT_KB, r2, r2), lambda o, j: (j, 0, 0)),
                  pl.BlockSpec(l1.shape, lambda o, j: (0, 0, 0))],
        out_specs=pl.BlockSpec((1, FFT_KB, r2, c), lambda o, j: (o, j, 0, 0)),
        scratch_shapes=[pltpu.VMEM((2, FFT_KB, FFT_R, c), F32)] * 2,
        compiler_params=_cparams(("parallel", "parallel")),
        name="filter_spectrum",
    )(s_filt, s_filt, g2, l1)


def _mid_body(s_ref, g_ref, gh_ref, kf_ref, t_ref, s3, t3):
    j = pl.program_id(1)
    r2 = 2 * FFT_R
    c = kf_ref.shape[-1]
    _rows_to_slabs(s_ref, s3, c)
    for i in range(FFT_KB):
        @pl.when(j * FFT_KB + i < FFT_KA)
        def _():
            x = _dot(g_ref[i], _slab(s3, i))
            xr, xi = x[0:FFT_R], x[FFT_R:r2]
            kr, ki = kf_ref[0, i, 0:FFT_R], kf_ref[0, i, FFT_R:r2]
            y = jnp.concatenate([xr * kr - xi * ki, xr * ki + xi * kr], axis=0).astype(BF16)
            t = _dot(gh_ref[i], y)
            t3[0, i] = t[0:FFT_R]
            t3[1, i] = t[FFT_R:r2]

        @pl.when(j * FFT_KB + i >= FFT_KA)
        def _():
            t3[0, i] = jnp.zeros((FFT_R, c), F32)
            t3[1, i] = jnp.zeros((FFT_R, c), F32)

    _slabs_to_rows(t3, t_ref, c)


def _conv_mid(s, kf, order, g2, g2h, c):
    n, _, _, cols = s.shape
    r2 = 2 * FFT_R
    blk = pl.BlockSpec((1, 2, FFT_KB, cols), lambda i, j: (i, 0, j, 0))
    gspec = pl.BlockSpec((FFT_KB, r2, r2), lambda i, j: (j, 0, 0))
    return pl.pallas_call(
        _mid_body,
        out_shape=jax.ShapeDtypeStruct(s.shape, F32),
        grid=(n, FFT_KA_PAD // FFT_KB),
        in_specs=[blk, gspec, gspec,
                  pl.BlockSpec((1, FFT_KB, r2, c), lambda i, j: (order, j, 0, 0))],
        out_specs=blk,
        scratch_shapes=[pltpu.VMEM((2, FFT_KB, FFT_R, c), F32)] * 2,
        compiler_params=_cparams(("parallel", "parallel")),
        name="conv_mid",
    )(s, g2, g2h, kf)


def _inv1_body(m_ref, t_ref, u_ref, xg_ref, sk_ref, o_ref):
    c = u_ref.shape[-1]
    t2 = t_ref[0].reshape(2 * FFT_KA_PAD, FFT_NB * c).astype(BF16)
    y = _dot(m_ref[...], t2)
    for bb in range(FFT_NB):
        conv = y[:, bb * c:(bb + 1) * c] + u_ref[0, :, bb, :] * sk_ref[...]
        o_ref[0, :, bb, :] = xg_ref[0, :, bb, :] * conv


def _conv_inv1(t, u, xg, skip, m_inv):
    n, seq, c = u.shape
    a_half = FFT_A // 2
    sk = skip.astype(F32).reshape(1, c)
    uspec = pl.BlockSpec((1, a_half, FFT_NB, c), lambda i, j: (i, 0, j, 0))
    view = lambda a: a.reshape(n, a_half, FFT_R, c)
    out = pl.pallas_call(
        _inv1_body,
        out_shape=jax.ShapeDtypeStruct((n, a_half, FFT_R, c), F32),
        grid=(n, FFT_R // FFT_NB),
        in_specs=[pl.BlockSpec(m_inv.shape, lambda i, j: (0, 0)),
                  pl.BlockSpec((1, 2, FFT_KA_PAD, FFT_NB * c), lambda i, j: (i, 0, 0, j)),
                  uspec, uspec,
                  pl.BlockSpec((1, c), lambda i, j: (0, 0))],
        out_specs=uspec,
        compiler_params=_cparams(("parallel", "parallel")),
        name="conv_inv1",
    )(m_inv, t, view(u), view(xg), sk)
    return out.reshape(n, seq, c)


def _hyena(v, x1, x2, f_w1, f_b1, f_w2, f_b2, f_w3, f_freq, skip):
    _, seq, c = v.shape
    assert 2 * seq == FFT_A * FFT_R
    m_fwd, m_inv, g2, g2h = _conv_dft_constants()
    taps, l1 = _hyena_filter_taps(seq, f_w1, f_b1, f_w2, f_b2, f_w3, f_freq, c)
    kf = _filter_spectrum(_conv_fwd1(taps, m_fwd), l1, g2, c)
    y = v
    for order, xg in enumerate((x1, x2)):
        t = _conv_mid(_conv_fwd1(y, m_fwd), kf, order, g2, g2h, c)
        y = _conv_inv1(t, y, xg, skip[order], m_inv)
    return y


def _mix_ffn_body(x_ref, a1_ref, a2_ref, wm_ref, gm_ref, g_ref, sh_ref, sc_ref, gt_ref,
                  w1_ref, w3_ref, w2_ref, o_ref, x_scr, h_scr, acc_scr):
    j = pl.program_id(2)

    @pl.when(j == 0)
    def _():
        c1 = a1_ref.shape[-1]
        mixed = _dot(a1_ref[0].astype(BF16), wm_ref[0:c1]) + _dot(a2_ref[0].astype(BF16), wm_ref[c1:])
        xm = x_ref[0] + gm_ref[0] * mixed
        x_scr[...] = xm
        h_scr[...] = _norm_mod(xm, g_ref[...], sh_ref[0], sc_ref[0]).astype(BF16)
        acc_scr[...] = jnp.zeros_like(acc_scr)

    h = h_scr[...]
    a = _dot(h, w1_ref[...])
    u = (a * jax.nn.sigmoid(a) * _dot(h, w3_ref[...])).astype(BF16)
    acc_scr[...] += _dot(u, w2_ref[...])

    @pl.when(j == pl.num_programs(2) - 1)
    def _():
        o_ref[0] = x_scr[...] + gt_ref[0] * acc_scr[...]


def _mix_ffn(x, a1, a2, w_mix, gate_mix, g, shift, scale, gate, w1, w3, w2, tm=512, fb=1408):
    b, s, d = x.shape
    f = w1.shape[1]
    tok = lambda c: pl.BlockSpec((1, tm, c), lambda bi, i, j: (bi, i, 0))
    per_b = pl.BlockSpec((1, 1, d), lambda bi, i, j: (bi, 0, 0))
    const = lambda a: pl.BlockSpec(a.shape, lambda bi, i, j: (0, 0))
    return pl.pallas_call(
        _mix_ffn_body,
        out_shape=jax.ShapeDtypeStruct(x.shape, F32),
        grid=(b, s // tm, f // fb),
        in_specs=[tok(d), tok(a1.shape[-1]), tok(a2.shape[-1]), const(w_mix), per_b,
                  const(g), per_b, per_b, per_b,
                  pl.BlockSpec((d, fb), lambda bi, i, j: (0, j)),
                  pl.BlockSpec((d, fb), lambda bi, i, j: (0, j)),
                  pl.BlockSpec((fb, d), lambda bi, i, j: (j, 0))],
        out_specs=tok(d),
        scratch_shapes=[pltpu.VMEM((tm, d), F32), pltpu.VMEM((tm, d), BF16), pltpu.VMEM((tm, d), F32)],
        compiler_params=_cparams(("parallel", "parallel", "arbitrary")),
        name="mix_ffn",
    )(x, a1, a2, w_mix, gate_mix, g, shift, scale, gate, w1, w3, w2)


def _fm_constants(cg):
    j = np.arange(cg)[:, None]
    m = np.arange(cg)[None, :]
    ph = 2.0 * np.pi * j * m / cg
    w_cs = np.concatenate([np.cos(ph), np.sin(ph)], axis=1)
    d = np.arange(FM_A)[:, None]
    a = np.arange(FM_A)[None, :]
    ph = 2.0 * np.pi * d * a / FM_A
    fr, fi = np.cos(ph), -np.sin(ph)
    m1 = np.block([[fr, fi], [fi, -fr]])
    n = FM_A * FM_A
    dd = np.arange(FM_A)[:, None, None]
    c = np.arange(FM_A)[None, :, None]
    b = np.arange(FM_A)[None, None, :]
    th = 2.0 * np.pi * (b * c / FM_A + b * dd / n)
    gcat = np.concatenate([np.cos(th), np.sin(th)], axis=2)
    return _mxu_const(w_cs), _mxu_const(m1), _mxu_const(gcat)


def _fm_front_body(x_ref, g_ref, sh_ref, sc_ref, w_ref, m_ref, o_ref, *, cg, nb):
    d = x_ref.shape[-1]
    xs = jnp.concatenate([x_ref[0, :, bb, :] for bb in range(nb)], axis=0)
    h = _norm_mod(xs, g_ref[...], sh_ref[0], sc_ref[0]).astype(BF16)
    pq = [_dot(h[:, grp * cg:(grp + 1) * cg], w_ref[...]) for grp in range(d // cg)]
    p = jnp.concatenate([t[:, 0:cg] for t in pq], axis=1)
    q = jnp.concatenate([t[:, cg:2 * cg] for t in pq], axis=1)
    for bb in range(nb):
        rows = slice(bb * FM_A, (bb + 1) * FM_A)
        res = _dot(m_ref[...], jnp.concatenate([p[rows], q[rows]], axis=0).astype(BF16))
        o_ref[0, 0, :, bb, :] = res[0:FM_A]
        o_ref[0, 1, :, bb, :] = res[FM_A:2 * FM_A]


def _fm_front(x, g, shift, scale, w_cs, m1, nb=8):
    b, s, d = x.shape
    cg = w_cs.shape[0]
    per_b = pl.BlockSpec((1, 1, d), lambda bi, j: (bi, 0, 0))
    const = lambda a: pl.BlockSpec(a.shape, lambda bi, j: (0, 0))
    return pl.pallas_call(
        functools.partial(_fm_front_body, cg=cg, nb=nb),
        out_shape=jax.ShapeDtypeStruct((b, 2, FM_A, s // FM_A, d), F32),
        grid=(b, s // FM_A // nb),
        in_specs=[pl.BlockSpec((1, FM_A, nb, d), lambda bi, j: (bi, 0, j, 0)),
                  const(g), per_b, per_b, const(w_cs), const(m1)],
        out_specs=pl.BlockSpec((1, 2, FM_A, nb, d), lambda bi, j: (bi, 0, 0, j, 0)),
        compiler_params=_cparams(("parallel", "parallel")),
        name="fm_front",
    )(x.reshape(b, FM_A, s // FM_A, d), g, shift, scale, w_cs, m1)


def _fm_s2_body(s_ref, g_ref, o_ref, *, dblk, scale):
    for i in range(dblk):
        s2 = jnp.concatenate([s_ref[0, 0, i], s_ref[0, 1, i]], axis=0).astype(BF16)
        o_ref[:, i, :] = _dot(g_ref[i], s2) * scale


def _fm_stage2(sv, gcat, seq, d, dblk=8):
    b = sv.shape[0]
    scale = 1.0 / math.sqrt(seq * (d // F_GROUPS))
    out = pl.pallas_call(
        functools.partial(_fm_s2_body, dblk=dblk, scale=scale),
        out_shape=jax.ShapeDtypeStruct((b * FM_A, dblk * (FM_A // dblk), d), F32),
        grid=(b, FM_A // dblk),
        in_specs=[pl.BlockSpec((1, 2, dblk, FM_A, d), lambda bi, j: (bi, 0, j, 0, 0)),
                  pl.BlockSpec((dblk, FM_A, 2 * FM_A), lambda bi, j: (j, 0, 0))],
        out_specs=pl.BlockSpec((FM_A, dblk, d), lambda bi, j: (bi, j, 0)),
        compiler_params=_cparams(("parallel", "parallel")),
        name="fm_stage2",
    )(sv, gcat)
    return out.reshape(b, seq, d)


def _fourier_mix(x, g, shift, scale):
    b, s, d = x.shape
    assert s == FM_A * FM_A
    w_cs, m1, gcat = _fm_constants(d // F_GROUPS)
    return _fm_stage2(_fm_front(x, g, shift, scale, w_cs, m1), gcat, s, d)


LANES = 128
ROW_SL = 8
MOE_TM = 1024
DMA_WINDOW = 128


def _router_body(x_ref, yf_ref, wf_ref, gf_ref, g_ref, sh_ref, sc_ref, wr_ref, br_ref,
                 xo_ref, h_ref, meta_ref, gw_ref, cnt_ref, carry):
    i = pl.program_id(0)

    @pl.when(i == 0)
    def _():
        carry[...] = jnp.zeros_like(carry)

    xm = x_ref[...] + gf_ref[0] * _dot(yf_ref[...].astype(BF16), wf_ref[...])
    xo_ref[...] = xm
    h = _norm_mod(xm, g_ref[...], sh_ref[0], sc_ref[0])
    _rows_to_tiles(h_ref, h)
    h_hi = h.astype(BF16)
    h_lo = (h - h_hi.astype(F32)).astype(BF16)
    by_hi = _dot(h_hi, wr_ref[...])
    logits = by_hi[:, 0:LANES] + by_hi[:, LANES:] + _dot(h_lo, wr_ref[:, 0:LANES]) + br_ref[...]
    lane = lax.broadcasted_iota(jnp.int32, logits.shape, 1)
    nl = logits.shape[-1]
    m1 = jnp.max(logits, axis=-1, keepdims=True)
    i1 = jnp.min(jnp.where(logits == m1, lane, nl), axis=-1, keepdims=True)
    rest = jnp.where(lane == i1, -3.0e38, logits)
    m2 = jnp.max(rest, axis=-1, keepdims=True)
    i2 = jnp.min(jnp.where(rest == m2, lane, nl), axis=-1, keepdims=True)
    e = jnp.exp(m2 - m1)
    gw_ref[...] = jnp.where(lane == 0, 1.0 / (1.0 + e), jnp.where(lane == 1, e / (1.0 + e), 0.0))
    onehot = jnp.where((lane == i1) | (lane == i2), 1.0, 0.0)
    tm = onehot.shape[0]
    earlier = lax.broadcasted_iota(jnp.int32, (tm, tm), 0) > lax.broadcasted_iota(jnp.int32, (tm, tm), 1)
    excl = _dot(jnp.where(earlier, 1.0, 0.0).astype(BF16), onehot.astype(BF16)) + carry[...]
    r1 = jnp.sum(jnp.where(lane == i1, excl, 0.0), axis=-1, keepdims=True).astype(jnp.int32)
    r2 = jnp.sum(jnp.where(lane == i2, excl, 0.0), axis=-1, keepdims=True).astype(jnp.int32)
    meta_ref[...] = jnp.where(lane == 0, i1, jnp.where(lane == 1, i2, jnp.where(lane == 2, r1, jnp.where(lane == 3, r2, 0))))
    carry[...] = carry[...] + jnp.sum(onehot, axis=0, keepdims=True)
    cnt_ref[...] = carry[...]


def _router(x, y_f, w_f, gate_f, g, shift, scale, w_router, b_router, tm=512):
    b, s, d = x.shape
    t = b * s
    ne = w_router.shape[1]
    wr = jnp.pad(w_router.astype(F32), ((0, 0), (0, LANES - ne)))
    wr_hi = wr.astype(BF16)
    wr = jnp.concatenate([wr_hi, (wr - wr_hi.astype(F32)).astype(BF16)], axis=1)
    br = jnp.pad(b_router.astype(F32).reshape(1, ne), ((0, 0), (0, LANES - ne)), constant_values=NEG_INF)
    spt = s // tm
    per_b = pl.BlockSpec((1, 1, d), lambda i: (i // spt, 0, 0))
    const = lambda a: pl.BlockSpec(a.shape, lambda i: (0, 0))
    tok = pl.BlockSpec((tm, d), lambda i: (i, 0))
    return pl.pallas_call(
        _router_body,
        out_shape=[jax.ShapeDtypeStruct((t, d), F32),
                   jax.ShapeDtypeStruct((t * ROW_SL, LANES), F32),
                   jax.ShapeDtypeStruct((t, LANES), jnp.int32),
                   jax.ShapeDtypeStruct((t, LANES), F32),
                   jax.ShapeDtypeStruct((1, LANES), F32)],
        grid=(t // tm,),
        in_specs=[tok, tok, const(w_f), per_b, const(g), per_b, per_b, const(wr), const(br)],
        out_specs=[tok,
                   pl.BlockSpec((tm * ROW_SL, LANES), lambda i: (i, 0)),
                   pl.BlockSpec((tm, LANES), lambda i: (i, 0)),
                   pl.BlockSpec((tm, LANES), lambda i: (i, 0)),
                   pl.BlockSpec((1, LANES), lambda i: (0, 0))],
        scratch_shapes=[pltpu.VMEM((1, LANES), F32)],
        compiler_params=_cparams(("arbitrary",)),
        name="router",
    )(x.reshape(t, d), y_f.reshape(t, d), w_f, gate_f, g, shift, scale, wr, br)


def _moe_plan(meta, counts, ne, tm):
    i1, i2, r1, r2 = meta[:, 0], meta[:, 1], meta[:, 2], meta[:, 3]
    cnt = counts[0, :ne].astype(jnp.int32)
    padded = ((cnt + tm - 1) // tm) * tm
    ends = jnp.cumsum(padded)
    offs = ends - padded
    pick = lambda idx: sum(jnp.where(idx == e, offs[e], 0) for e in range(ne))
    pos = jnp.concatenate([pick(i1) + r1, pick(i2) + r2]).astype(jnp.int32)
    n_tiles = (2 * meta.shape[0]) // tm + ne
    n_used = (ends[ne - 1] // tm).astype(jnp.int32)
    tile_start = jnp.minimum(jnp.arange(n_tiles, dtype=jnp.int32), n_used - 1) * tm
    tile_expert = jnp.sum(tile_start[:, None] >= ends[None, :], axis=1).astype(jnp.int32)
    group_end = sum(jnp.where(tile_expert == e, offs[e] + cnt[e], 0) for e in range(ne))
    tile_rows = jnp.clip(group_end - tile_start, 0, tm).astype(jnp.int32)
    return pos, offs + cnt, padded - cnt, tile_expert, n_used.reshape(1), tile_rows


def _windowed_copies(n, start_copy, wait_one, per_iter):
    def body(i, carry):
        @pl.when(i >= DMA_WINDOW)
        def _():
            for _ in range(per_iter):
                wait_one()
        start_copy(i)
        return carry

    lax.fori_loop(0, n, body, 0)

    def drain(i, carry):
        for _ in range(per_iter):
            wait_one()
        return carry

    lax.fori_loop(0, jnp.minimum(n, DMA_WINDOW), drain, 0)


def _tile_of(ref, row):
    return ref.at[pl.ds(pl.multiple_of(row * ROW_SL, ROW_SL), ROW_SL)]


def _tiles_to_rows(ref, n, first=0):
    return jnp.concatenate([ref[pl.ds(first * ROW_SL + sl, n, stride=ROW_SL), :] for sl in range(ROW_SL)], axis=1)


def _rows_to_tiles(ref, val):
    n = val.shape[0]
    for sl in range(ROW_SL):
        ref[pl.ds(sl, n, stride=ROW_SL), :] = val[:, sl * LANES:(sl + 1) * LANES]


def _dispatch_body(pos_ref, pad_start_ref, pad_n_ref, h_ref, xs_hbm, sem, *, n_tok, ne):
    i = pl.program_id(0)
    td = h_ref.shape[0] // ROW_SL
    base = i * td
    copy = lambda src, dst: pltpu.make_async_copy(_tile_of(h_ref, src), _tile_of(xs_hbm, dst), sem)
    wait_one = lambda: copy(0, 0).wait()

    def start_token(r, carry):
        copy(r, pos_ref[base + r]).start(priority=0)
        copy(r, pos_ref[n_tok + base + r]).start(priority=1)
        return carry

    lax.fori_loop(0, td, start_token, 0, unroll=8)
    whole_tile = pltpu.make_async_copy(h_ref, xs_hbm.at[pl.ds(0, td * ROW_SL)], sem)
    whole_tile.wait()
    whole_tile.wait()

    @pl.when(i == 0)
    def _():
        for e in range(ne):
            first = pad_start_ref[e]
            _windowed_copies(pad_n_ref[e], lambda r: copy(0, first + r).start(), wait_one, 1)


def _moe_dispatch(h3, pos, pad_start, pad_n, n_rows, td=1024):
    n_tok = h3.shape[0] // ROW_SL
    ne = pad_start.shape[0]
    return pl.pallas_call(
        functools.partial(_dispatch_body, n_tok=n_tok, ne=ne),
        out_shape=jax.ShapeDtypeStruct((n_rows * ROW_SL, LANES), h3.dtype),
        grid_spec=pltpu.PrefetchScalarGridSpec(
            num_scalar_prefetch=3, grid=(n_tok // td,),
            in_specs=[pl.BlockSpec((td * ROW_SL, LANES), lambda i, p, ps, pn: (i, 0))],
            out_specs=pl.BlockSpec(memory_space=pl.ANY),
            scratch_shapes=[pltpu.SemaphoreType.DMA(())]),
        compiler_params=_cparams(("arbitrary",)),
        name="moe_dispatch",
    )(pos, pad_start, pad_n, h3)


def _moe_grouped_body(te_ref, nu_ref, tr_ref, xs_ref, w1_ref, w3_ref, w2_ref, y_ref, xb_scr, acc_scr):
    i = pl.program_id(0)
    j = pl.program_id(1)
    tm = xb_scr.shape[0]
    hm = tm // 2

    def expert_rows(nrows):
        h = xb_scr[0:nrows]
        a = _dot(h, w1_ref[0].astype(BF16))
        u = (a * jax.nn.sigmoid(a) * _dot(h, w3_ref[0].astype(BF16))).astype(BF16)
        part = _dot(u, w2_ref[0].astype(BF16))

        @pl.when(j == 0)
        def _():
            acc_scr[0:nrows] = part

        @pl.when(j > 0)
        def _():
            acc_scr[0:nrows] += part

    @pl.when(i < nu_ref[0])
    def _():
        @pl.when(j == 0)
        def _():
            xb_scr[...] = _tiles_to_rows(xs_ref, tm).astype(BF16)

        @pl.when(tr_ref[i] > hm)
        def _():
            expert_rows(tm)

        @pl.when(tr_ref[i] <= hm)
        def _():
            expert_rows(hm)

            @pl.when(j == 0)
            def _():
                acc_scr[hm:tm] = jnp.zeros((tm - hm, acc_scr.shape[1]), F32)

        @pl.when(j == pl.num_programs(1) - 1)
        def _():
            _rows_to_tiles(y_ref, acc_scr[...])


def _moe_grouped(xs, tile_expert, n_used, tile_rows, w1, w3, w2, tm, fb=512):
    ne, d, f = w1.shape
    n_rows = xs.shape[0] // ROW_SL
    nj = f // fb
    row_tile = lambda i, j, te, nu, tr: (jnp.maximum(jnp.minimum(i, nu[0] - 1), 0), 0)
    jj = lambda i, j, nu: jnp.where(i < nu[0], j, nj - 1)
    return pl.pallas_call(
        _moe_grouped_body,
        out_shape=jax.ShapeDtypeStruct(xs.shape, F32),
        grid_spec=pltpu.PrefetchScalarGridSpec(
            num_scalar_prefetch=3, grid=(n_rows // tm, nj),
            in_specs=[pl.BlockSpec((tm * ROW_SL, LANES), row_tile),
                      pl.BlockSpec((1, d, fb), lambda i, j, te, nu, tr: (te[i], 0, jj(i, j, nu))),
                      pl.BlockSpec((1, d, fb), lambda i, j, te, nu, tr: (te[i], 0, jj(i, j, nu))),
                      pl.BlockSpec((1, fb, d), lambda i, j, te, nu, tr: (te[i], jj(i, j, nu), 0))],
            out_specs=pl.BlockSpec((tm * ROW_SL, LANES), row_tile),
            scratch_shapes=[pltpu.VMEM((tm, d), BF16), pltpu.VMEM((tm, d), F32)]),
        compiler_params=_cparams(("arbitrary", "arbitrary")),
        name="moe_grouped",
    )(tile_expert, n_used, tile_rows, xs, w1, w3, w2)


def _moe_final_body(pos_ref, x_ref, y_hbm, gw_ref, gt_ref, fg_ref, o_ref, yg_scr, sem, *, n_tok):
    i = pl.program_id(0)
    tc = x_ref.shape[0]
    slot = i % 2

    def gather_tile(step, into):
        base = step * tc

        def start_token(r, carry):
            dst = yg_scr.at[into]
            pltpu.make_async_copy(_tile_of(y_hbm, pos_ref[base + r]), _tile_of(dst, r),
                                  sem.at[into]).start(priority=0)
            pltpu.make_async_copy(_tile_of(y_hbm, pos_ref[n_tok + base + r]), _tile_of(dst, tc + r),
                                  sem.at[into]).start(priority=1)
            return carry

        lax.fori_loop(0, tc, start_token, 0, unroll=8)

    @pl.when(i == 0)
    def _():
        gather_tile(0, 0)

    @pl.when(i + 1 < pl.num_programs(0))
    def _():
        gather_tile(i + 1, 1 - slot)

    pltpu.make_async_copy(y_hbm.at[pl.ds(0, 2 * tc * ROW_SL)], yg_scr.at[slot], sem.at[slot]).wait()
    gw = gw_ref[...]
    rows = yg_scr.at[slot]
    y = gw[:, 0:1] * _tiles_to_rows(rows, tc) + gw[:, 1:2] * _tiles_to_rows(rows, tc, first=tc)
    xo = x_ref[...] + gt_ref[0] * y
    ms = jnp.mean(xo * xo, axis=-1, keepdims=True)
    o_ref[...] = xo * lax.rsqrt(ms + EPS) * fg_ref[...]


def _moe_final(x, y, pos, gw, gt, final_g, tc=512):
    b, s, d = x.shape
    t = b * s
    spt = s // tc
    nsl = d // LANES
    out = pl.pallas_call(
        functools.partial(_moe_final_body, n_tok=t),
        out_shape=jax.ShapeDtypeStruct((t, d), F32),
        grid_spec=pltpu.PrefetchScalarGridSpec(
            num_scalar_prefetch=1, grid=(t // tc,),
            in_specs=[pl.BlockSpec((tc, d), lambda i, p: (i, 0)),
                      pl.BlockSpec(memory_space=pl.ANY),
                      pl.BlockSpec((tc, LANES), lambda i, p: (i, 0)),
                      pl.BlockSpec((1, 1, d), lambda i, p: (i // spt, 0, 0)),
                      pl.BlockSpec(final_g.shape, lambda i, p: (0, 0))],
            out_specs=pl.BlockSpec((tc, d), lambda i, p: (i, 0)),
            scratch_shapes=[pltpu.VMEM((2, 2 * tc * ROW_SL, LANES), F32), pltpu.SemaphoreType.DMA((2,))]),
        compiler_params=_cparams(("arbitrary",)),
        name="moe_final",
    )(pos, x.reshape(t, d), y, gw, gt, final_g)
    return out.reshape(b, s, d)


def _moe_routed(x, y_f, w_f, gate_f, g, shift, scale, gt, final_g, w_router, b_router, w1, w3, w2):
    ne = w1.shape[0]
    tm = MOE_TM
    x1, h3, meta, gw, counts = _router(x, y_f, w_f, gate_f, g, shift, scale, w_router, b_router)
    pos, pad_start, pad_n, tile_expert, n_used, tile_rows = _moe_plan(meta, counts, ne, tm)
    assert x.shape[-1] == ROW_SL * LANES
    n_rows = (2 * (h3.shape[0] // ROW_SL) // tm + ne) * tm
    xs = _moe_dispatch(h3, pos, pad_start, pad_n, n_rows)
    y = _moe_grouped(xs, tile_expert, n_used, tile_rows, w1, w3, w2, tm)
    return _moe_final(x1.reshape(x.shape), y, pos, gw, gt, final_g)


def kernel(x, c, ctx, c_ctx, w_ada, b_ada, norm_g, w_in, hy_short_w, hy_short_b, hy_f_w1, hy_f_b1, hy_f_w2, hy_f_b2, hy_f_w3, hy_f_freq, hy_skip, na_rpb, w_mix_out, ffn_w1, ffn_w3, ffn_w2, w_fourier, w_router, b_router, moe_w1, moe_w3, moe_w2, final_g):
    b, s, d = x.shape
    depth = w_ada.shape[0]
    assert depth == 2, "layer 0 mixes with Hyena/attention, layer 1 with Fourier/MoE"
    c_hy = hy_skip.shape[-1]
    c_na = d - c_hy

    cvec = jnp.concatenate([c, c_ctx[None, :], jnp.zeros((8 - b - 1, d), F32)], axis=0)
    mods = _ada(cvec, w_ada, b_ada)

    def mod(layer, idx, ctx_row=False):
        m = mods[layer, :, idx * d:(idx + 1) * d]
        return m[b:b + 1, None, :] if ctx_row else m[0:b, None, :]

    row = lambda a: a.reshape(1, -1)

    w_in0 = w_in[0].astype(BF16)
    w_hy, w_qkv = w_in0[:, 0:3 * c_hy], w_in0[:, 3 * c_hy:]
    v, x1, x2, q, k, va = _inproj(x, row(norm_g[0, 0]), mod(0, 0), mod(0, 1), w_hy, w_qkv,
                                  hy_short_w[0], row(hy_short_b[0]))
    kc, vc = _ctxkv(ctx, row(norm_g[0, 0]), mod(0, 0, True), mod(0, 1, True), w_qkv[:, c_na:])
    y_na = _natt(q, k, va, kc, vc, _na_bias_table(na_rpb[0]))
    y_hy = _hyena(v, x1, x2, hy_f_w1[0], hy_f_b1[0], hy_f_w2[0], hy_f_b2[0], hy_f_w3[0],
                  hy_f_freq[0], hy_skip[0])
    x = _mix_ffn(x, y_hy, y_na, w_mix_out[0].astype(BF16), mod(0, 2),
                 row(norm_g[0, 1]), mod(0, 3), mod(0, 4), mod(0, 5),
                 ffn_w1[0].astype(BF16), ffn_w3[0].astype(BF16), ffn_w2[0].astype(BF16))

    y_f = _fourier_mix(x, row(norm_g[1, 0]), mod(1, 0), mod(1, 1))
    return _moe_routed(x, y_f, w_fourier[0].astype(BF16), mod(1, 2),
                       row(norm_g[1, 1]), mod(1, 3), mod(1, 4), mod(1, 5), row(final_g),
                       w_router[0], b_router[0],
                       moe_w1[0], moe_w3[0], moe_w2[0])
```

```python
import functools
import math

import numpy as np
import jax
import jax.numpy as jnp
from jax import lax
from jax.experimental import pallas as pl
from jax.experimental.pallas import tpu as pltpu

F32 = jnp.float32
BF16 = jnp.bfloat16
HIGHEST = lax.Precision.HIGHEST

GRID_W = 64
NA_HEAD_DIM = 32
NA_WIN_R = 8
NA_WIN_C = 16
HYENA_EMB = 33
HYENA_BANDS = (HYENA_EMB - 1) // 2
HYENA_FAST_DECAY = 0.3
HYENA_SLOW_DECAY = 1.5
HYENA_TARGET = 1e-2
F_GROUPS = 4
N_MOD = 6
EPS = 1e-6
NEG_INF = -1e30

FFT_A = 64
FFT_R = 128
FFT_KA = FFT_A // 2 + 1
FFT_KA_PAD = 40
FM_A = 64

VMEM_LIMIT = 48 * 1024 * 1024


def _cparams(sem, vmem_limit=VMEM_LIMIT):
    return pltpu.CompilerParams(dimension_semantics=sem, vmem_limit_bytes=vmem_limit)


def _dot(a, b):
    return jnp.dot(a, b, preferred_element_type=F32)


def _mxu_const(m):
    return jnp.asarray(m, dtype=F32).astype(BF16)


def _norm_mod(x, g, shift, scale):
    ms = jnp.mean(x * x, axis=-1, keepdims=True)
    y = x * lax.rsqrt(ms + EPS) * g
    return y * (1.0 + scale) + shift


def _ada_body(c_ref, w_ref, b_ref, o_ref):
    cv = c_ref[...]
    s = cv * jax.nn.sigmoid(cv)
    o_ref[0] = jnp.dot(s, w_ref[0], precision=HIGHEST, preferred_element_type=F32) + b_ref[0]


def _ada(cvec, w_ada, b_ada):
    depth, d, n = w_ada.shape
    rows = cvec.shape[0]
    bn = n // 4
    return pl.pallas_call(
        _ada_body,
        out_shape=jax.ShapeDtypeStruct((depth, rows, n), F32),
        grid=(depth, n // bn),
        in_specs=[pl.BlockSpec((rows, d), lambda l, j: (0, 0)),
                  pl.BlockSpec((1, d, bn), lambda l, j: (l, 0, j)),
                  pl.BlockSpec((1, 1, bn), lambda l, j: (l, 0, j))],
        out_specs=pl.BlockSpec((1, rows, bn), lambda l, j: (l, 0, j)),
        compiler_params=_cparams(("parallel", "parallel")),
        name="ada",
    )(cvec, w_ada, b_ada.reshape(depth, 1, n))


def _inproj_body(x_ref, xp_ref, xn_ref, g_ref, sh_ref, sc_ref, why_ref, wqkv_ref, sw_ref, sb_ref,
                 v_ref, x1_ref, x2_ref, q_ref, k_ref, va_ref, *, n_tiles, q_scale, c_hy, c_na):
    i = pl.program_id(1)
    g, sh, sc = g_ref[...], sh_ref[0], sc_ref[0]
    hf = _norm_mod(x_ref[0], g, sh, sc)
    h = hf.astype(BF16)
    tm = hf.shape[0]
    hx = jnp.concatenate([_norm_mod(xp_ref[0], g, sh, sc), hf, _norm_mod(xn_ref[0], g, sh, sc)], axis=0)
    zx = _dot(hx.astype(BF16), why_ref[...])
    zh = zx[8:8 + tm]
    zp = jnp.where(i > 0, zx[7:8], 0.0)
    zn = jnp.where(i < n_tiles - 1, zx[8 + tm:9 + tm], 0.0)
    row = lax.broadcasted_iota(jnp.int32, zh.shape, 0)
    z_m1 = jnp.where(row == 0, zp, pltpu.roll(zh, 1, 0))
    z_p1 = jnp.where(row == tm - 1, zn, pltpu.roll(zh, tm - 1, 0))
    sw = sw_ref[...]
    zc = z_m1 * sw[0:1] + zh * sw[1:2] + z_p1 * sw[2:3] + sb_ref[...]
    v_ref[0] = zc[:, 0:c_hy]
    x1_ref[0] = zc[:, c_hy:2 * c_hy]
    x2_ref[0] = zc[:, 2 * c_hy:3 * c_hy]
    zq = _dot(h, wqkv_ref[...])
    q_ref[0] = (zq[:, 0:c_na] * q_scale).astype(BF16)
    k_ref[0] = zq[:, c_na:2 * c_na].astype(BF16)
    va_ref[0] = zq[:, 2 * c_na:3 * c_na].astype(BF16)


def _inproj(x, g, shift, scale, w_hy, w_qkv, short_w, short_b, tm=512):
    b, s, d = x.shape
    c_hy = w_hy.shape[1] // 3
    c_na = w_qkv.shape[1] // 3
    n_tiles = s // tm
    r8 = tm // 8
    body = functools.partial(_inproj_body, n_tiles=n_tiles, q_scale=NA_HEAD_DIM ** -0.5,
                             c_hy=c_hy, c_na=c_na)
    tok = lambda c: pl.BlockSpec((1, tm, c), lambda bi, i: (bi, i, 0))
    full2 = lambda a: pl.BlockSpec(a.shape, lambda bi, i: (0, 0))
    per_b = pl.BlockSpec((1, 1, d), lambda bi, i: (bi, 0, 0))
    return pl.pallas_call(
        body,
        out_shape=[jax.ShapeDtypeStruct((b, s, c_hy), F32)] * 3 + [jax.ShapeDtypeStruct((b, s, c_na), BF16)] * 3,
        grid=(b, n_tiles),
        in_specs=[tok(d),
                  pl.BlockSpec((1, 8, d), lambda bi, i: (bi, jnp.maximum(i * r8 - 1, 0), 0)),
                  pl.BlockSpec((1, 8, d), lambda bi, i: (bi, jnp.minimum((i + 1) * r8, s // 8 - 1), 0)),
                  full2(g), per_b, per_b, full2(w_hy), full2(w_qkv), full2(short_w), full2(short_b)],
        out_specs=[tok(c_hy)] * 3 + [tok(c_na)] * 3,
        compiler_params=_cparams(("parallel", "parallel")),
        name="inproj",
    )(x, x, x, g, shift, scale, w_hy, w_qkv, short_w, short_b)


def _ctxkv_body(x_ref, g_ref, sh_ref, sc_ref, w_ref, k_ref, v_ref, *, c_na):
    h = _norm_mod(x_ref[0], g_ref[...], sh_ref[0], sc_ref[0]).astype(BF16)
    z = _dot(h, w_ref[...])
    k_ref[0] = z[:, 0:c_na].astype(BF16)
    v_ref[0] = z[:, c_na:2 * c_na].astype(BF16)


def _ctxkv(ctx, g, shift, scale, w_kv):
    b, n, d = ctx.shape
    c_na = w_kv.shape[1] // 2
    one = pl.BlockSpec((1, 1, d), lambda bi: (0, 0, 0))
    return pl.pallas_call(
        functools.partial(_ctxkv_body, c_na=c_na),
        out_shape=[jax.ShapeDtypeStruct((b, n, c_na), BF16)] * 2,
        grid=(b,),
        in_specs=[pl.BlockSpec((1, n, d), lambda bi: (bi, 0, 0)),
                  pl.BlockSpec(g.shape, lambda bi: (0, 0)), one, one,
                  pl.BlockSpec(w_kv.shape, lambda bi: (0, 0))],
        out_specs=[pl.BlockSpec((1, n, c_na), lambda bi: (bi, 0, 0))] * 2,
        compiler_params=_cparams(("parallel",)),
        name="ctxkv",
    )(ctx, g, shift, scale, w_kv)


NA_HEADS_PER_BLK = 8
NA_ROWS_PER_STEP = 8


def _na_bias_body(r_ref, e_ref, ok_ref, o_ref):
    t = jnp.dot(r_ref[...], e_ref[...], precision=HIGHEST, preferred_element_type=F32)
    o_ref[...] = jnp.where(ok_ref[...] > 0.5, t, NEG_INF)


def _na_bias_table(rpb):
    w = GRID_W
    h, nr, nc = rpb.shape
    col = np.arange(w)[:, None]
    kc = np.arange(w)[None, :]
    c_start = np.clip(col - NA_WIN_C // 2, 0, w - NA_WIN_C)
    valid = ((kc >= c_start) & (kc < c_start + NA_WIN_C)).reshape(1, w * w)
    expand = (np.arange(32)[:, None, None] == (kc - col + NA_WIN_C - 1)[None]).reshape(32, w * w)
    rp = jnp.pad(rpb.astype(F32).reshape(h * nr, nc), ((0, 0), (0, 32 - nc)))
    full = lambda a: pl.BlockSpec(a.shape, lambda: (0,) * a.ndim)
    expand = jnp.asarray(expand, dtype=F32)
    ok = jnp.asarray(valid, dtype=F32)
    toep = pl.pallas_call(
        _na_bias_body,
        out_shape=jax.ShapeDtypeStruct((h * nr, w * w), F32),
        in_specs=[full(rp), full(expand), full(ok)],
        out_specs=pl.BlockSpec((h * nr, w * w), lambda: (0, 0)),
        name="na_bias",
    )(rp, expand, ok)
    t2 = toep.reshape(h, nr, w, w).transpose(0, 2, 1, 3).reshape(h, w, nr * w)
    slabs = jnp.stack([t2[:, :, (NA_WIN_R - 1 - off) * w:(2 * NA_WIN_R - 1 - off) * w]
                       for off in range(NA_WIN_R)], axis=1)
    hpb = NA_HEADS_PER_BLK
    slabs = slabs.reshape(h // hpb, hpb, NA_WIN_R, w, NA_WIN_R * w).transpose(0, 2, 1, 3, 4)
    return slabs.reshape(h // hpb, NA_WIN_R, hpb * w, NA_WIN_R * w)


def _natt_body(q_ref, k_ref, v_ref, kc_ref, vc_ref, bias_ref, o_ref, *, rows):
    w = GRID_W
    hpb = NA_HEADS_PER_BLK
    nloc = NA_WIN_R * w
    lane = lax.broadcasted_iota(jnp.int32, (1, hpb * NA_HEAD_DIM), 1)
    in_head = [(lane >= NA_HEAD_DIM * hh) & (lane < NA_HEAD_DIM * (hh + 1)) for hh in range(hpb)]
    kcx = kc_ref[0]
    vcx = vc_ref[0]
    nt = (((1,), (1,)), ((), ()))

    def one_row(r):
        r0 = jnp.clip(r - NA_WIN_R // 2, 0, rows - NA_WIN_R)
        off = r - r0
        qs = q_ref[0, pl.ds(pl.multiple_of(r * w, w), w), :]
        kw = k_ref[0, pl.ds(pl.multiple_of(r0 * w, w), nloc), :]
        vw = v_ref[0, pl.ds(pl.multiple_of(r0 * w, w), nloc), :]
        zero = jnp.zeros_like(qs)
        qst = jnp.concatenate([jnp.where(m, qs, zero) for m in in_head], axis=0)
        s_loc = lax.dot_general(qst, kw, nt, preferred_element_type=F32) + bias_ref[0, off]
        s_ctx = lax.dot_general(qst, kcx, nt, preferred_element_type=F32)
        m = jnp.maximum(jnp.max(s_loc, axis=-1, keepdims=True), jnp.max(s_ctx, axis=-1, keepdims=True))
        p_loc = jnp.exp(s_loc - m)
        p_ctx = jnp.exp(s_ctx - m)
        den = jnp.sum(p_loc, axis=-1, keepdims=True) + jnp.sum(p_ctx, axis=-1, keepdims=True)
        o = (_dot(p_loc.astype(BF16), vw) + _dot(p_ctx.astype(BF16), vcx)) * (1.0 / den)
        acc = jnp.where(in_head[0], o[0:w], 0.0)
        for hh in range(1, hpb):
            acc = acc + jnp.where(in_head[hh], o[hh * w:(hh + 1) * w], 0.0)
        o_ref[0, pl.ds(pl.multiple_of(r * w, w), w), :] = acc.astype(BF16)

    def row_group(i, carry):
        for r in range(NA_ROWS_PER_STEP):
            one_row(NA_ROWS_PER_STEP * i + r)
        return carry

    lax.fori_loop(0, rows // NA_ROWS_PER_STEP, row_group, 0)


def _natt(q, k, v, kc, vc, bias):
    b, s, c = q.shape
    nctx = kc.shape[1]
    lw = NA_HEADS_PER_BLK * NA_HEAD_DIM
    rows = s // GRID_W
    seq = pl.BlockSpec((1, s, lw), lambda bi, g: (bi, 0, g))
    cx = pl.BlockSpec((1, nctx, lw), lambda bi, g: (bi, 0, g))
    return pl.pallas_call(
        functools.partial(_natt_body, rows=rows),
        out_shape=jax.ShapeDtypeStruct((b, s, c), BF16),
        grid=(b, c // lw),
        in_specs=[seq, seq, seq, cx, cx,
                  pl.BlockSpec((1,) + bias.shape[1:], lambda bi, g: (g, 0, 0, 0))],
        out_specs=seq,
        compiler_params=_cparams(("parallel", "parallel")),
        name="natt",
    )(q, k, v, kc, vc, bias)


def _hyena_feats(seq_len):
    t = jnp.linspace(0.0, 1.0, seq_len, dtype=F32)[:, None]
    bands = jnp.linspace(1e-4, HYENA_BANDS - 1, HYENA_BANDS, dtype=F32)
    ang = (2.0 * math.pi / seq_len) * jnp.arange(seq_len, dtype=F32)[:, None] * bands[None, :]
    feats = jnp.concatenate([t, jnp.cos(ang), -jnp.sin(ang)], axis=-1)
    return jnp.pad(feats, ((0, 0), (0, 128 - HYENA_EMB)))


def _filt_body(feat_ref, w1_ref, b1_ref, w2_ref, b2_ref, w3_ref, fr_ref, dl_ref, o_ref, l1_ref, h_scr,
               *, halves):
    j = pl.program_id(0)
    hp = functools.partial(jnp.dot, precision=HIGHEST, preferred_element_type=F32)
    feats = feat_ref[...]

    @pl.when(j == 0)
    def _():
        fr = fr_ref[...]
        h = jnp.sin(fr[0:1] * (hp(feats, w1_ref[...]) + b1_ref[...]))
        h_scr[...] = jnp.sin(fr[1:2] * (hp(h, w2_ref[...]) + b2_ref[...]))

    hc = hp(h_scr[...], w3_ref[...])
    t = feats[:, 0:1]
    hc = hc * jnp.exp(-t * dl_ref[...])
    row = lax.broadcasted_iota(jnp.int32, hc.shape, 0)
    hc = jnp.where((row == 0) & ((j // halves) % 2 == 1), 0.0, hc)
    l1_ref[0] = jnp.sum(jnp.abs(hc), axis=0, keepdims=True)
    o_ref[0] = hc


def _hyena_filter_taps(seq_len, f_w1, f_b1, f_w2, f_b2, f_w3, f_freq, c_hy):
    feats = _hyena_feats(seq_len)
    hid = f_w1.shape[1]
    w1 = jnp.pad(f_w1.astype(F32), ((0, 128 - HYENA_EMB), (0, 0)))
    deltas = jnp.abs(jnp.linspace(math.log(HYENA_TARGET) / HYENA_SLOW_DECAY,
                                  math.log(HYENA_TARGET) / HYENA_FAST_DECAY, c_hy, dtype=F32))[None, :]
    nblk = f_w3.shape[1] // c_hy
    halves = 2
    cb = c_hy // halves
    c0 = lambda a: pl.BlockSpec(a.shape, lambda j: (0, 0))
    b1, b2 = f_b1.reshape(1, hid), f_b2.reshape(1, hid)
    return pl.pallas_call(
        functools.partial(_filt_body, halves=halves),
        out_shape=[jax.ShapeDtypeStruct((nblk, seq_len, c_hy), F32),
                   jax.ShapeDtypeStruct((nblk, 1, c_hy), F32)],
        grid=(nblk * halves,),
        in_specs=[c0(feats), c0(w1), c0(b1), c0(f_w2), c0(b2),
                  pl.BlockSpec((hid, cb), lambda j: (0, j)), c0(f_freq),
                  pl.BlockSpec((1, cb), lambda j: (0, j % halves))],
        out_specs=[pl.BlockSpec((1, seq_len, cb), lambda j: (j // halves, 0, j % halves)),
                   pl.BlockSpec((1, 1, cb), lambda j: (j // halves, 0, j % halves))],
        scratch_shapes=[pltpu.VMEM((seq_len, hid), F32)],
        compiler_params=_cparams(("arbitrary",)),
        name="hyena_filter",
    )(feats, w1, b1, f_w2, b2, f_w3, f_freq, deltas)


def _conv_dft_constants():
    a_half = FFT_A // 2
    n = FFT_A * FFT_R
    ka = np.arange(FFT_KA)[:, None]
    a = np.arange(a_half)[None, :]
    ph = 2.0 * np.pi * ka * a / FFT_A
    m_fwd = np.zeros((2 * FFT_KA_PAD, a_half))
    m_fwd[:FFT_KA] = np.cos(ph)
    m_fwd[FFT_KA_PAD:FFT_KA_PAD + FFT_KA] = -np.sin(ph)
    wgt = np.where((ka == 0) | (ka == FFT_A // 2), 1.0, 2.0)
    m_inv = np.zeros((a_half, 2 * FFT_KA_PAD))
    m_inv[:, :FFT_KA] = (wgt * np.cos(ph)).T / n
    m_inv[:, FFT_KA_PAD:FFT_KA_PAD + FFT_KA] = (-wgt * np.sin(ph)).T / n
    kb = np.arange(FFT_R)[None, :, None]
    b = np.arange(FFT_R)[None, None, :]
    kaa = np.arange(FFT_KA)[:, None, None]
    th = 2.0 * np.pi * (b * kb / FFT_R + b * kaa / n)
    gr, gi = np.cos(th), -np.sin(th)
    g2 = np.zeros((FFT_KA_PAD, 2 * FFT_R, 2 * FFT_R))
    g2[:FFT_KA] = np.block([[gr, -gi], [gi, gr]])
    grt, git = gr.transpose(0, 2, 1), gi.transpose(0, 2, 1)
    g2h = np.zeros_like(g2)
    g2h[:FFT_KA] = np.block([[grt, git], [-git, grt]])
    return _mxu_const(m_fwd), _mxu_const(m_inv), _mxu_const(g2), _mxu_const(g2h)


FFT_NB = 16


def _fwd1_body(m_ref, u_ref, o_ref):
    u = jnp.concatenate([u_ref[0, :, bb, :] for bb in range(FFT_NB)], axis=1).astype(BF16)
    res = _dot(m_ref[...], u)
    o_ref[0, 0] = res[0:FFT_KA_PAD]
    o_ref[0, 1] = res[FFT_KA_PAD:2 * FFT_KA_PAD]


def _conv_fwd1(u, m_fwd):
    n, seq, c = u.shape
    a_half = FFT_A // 2
    return pl.pallas_call(
        _fwd1_body,
        out_shape=jax.ShapeDtypeStruct((n, 2, FFT_KA_PAD, FFT_R * c), F32),
        grid=(n, FFT_R // FFT_NB),
        in_specs=[pl.BlockSpec(m_fwd.shape, lambda i, j: (0, 0)),
                  pl.BlockSpec((1, a_half, FFT_NB, c), lambda i, j: (i, 0, j, 0))],
        out_specs=pl.BlockSpec((1, 2, FFT_KA_PAD, FFT_NB * c), lambda i, j: (i, 0, 0, j)),
        compiler_params=_cparams(("parallel", "parallel")),
        name="conv_fwd1",
    )(m_fwd, u.reshape(n, a_half, FFT_R, c))


FFT_KB = 8


def _rows_to_slabs(src_ref, dst_scr, c):
    for part in range(2):
        for b in range(FFT_R):
            dst_scr[part, :, b, :] = src_ref[0, part, :, b * c:(b + 1) * c]


def _slabs_to_rows(src_scr, dst_ref, c):
    for part in range(2):
        for b in range(FFT_R):
            dst_ref[0, part, :, b * c:(b + 1) * c] = src_scr[part, :, b, :]


def _slab(scr, i):
    return jnp.concatenate([scr[0, i], scr[1, i]], axis=0).astype(BF16)


def _fwd2f_body(sf_ref, sb_ref, g_ref, l1_ref, kf_ref, f3, b3):
    o = pl.program_id(0)
    j = pl.program_id(1)
    r2 = 2 * FFT_R
    c = kf_ref.shape[-1]
    _rows_to_slabs(sf_ref, f3, c)
    _rows_to_slabs(sb_ref, b3, c)
    inv = 1.0 / (l1_ref[2 * o] + l1_ref[2 * o + 1] + EPS)
    for i in range(FFT_KB):
        @pl.when(j * FFT_KB + i < FFT_KA)
        def _():
            xf = _dot(g_ref[i], _slab(f3, i))
            xb = _dot(g_ref[i], _slab(b3, i))
            kf_ref[0, i, 0:FFT_R] = (xf[0:FFT_R] + xb[0:FFT_R]) * inv
            kf_ref[0, i, FFT_R:r2] = (xf[FFT_R:r2] - xb[FFT_R:r2]) * inv

        @pl.when(j * FFT_KB + i >= FFT_KA)
        def _():
            kf_ref[0, i] = jnp.zeros((r2, c), F32)


def _filter_spectrum(s_filt, l1, g2, c):
    n_ord = s_filt.shape[0] // 2
    cols = s_filt.shape[-1]
    r2 = 2 * FFT_R
    return pl.pallas_call(
        _fwd2f_body,
        out_shape=jax.ShapeDtypeStruct((n_ord, FFT_KA_PAD, r2, c), F32),
        grid=(n_ord, FFT_KA_PAD // FFT_KB),
        in_specs=[pl.BlockSpec((1, 2, FFT_KB, cols), lambda o, j: (2 * o, 0, j, 0)),
                  pl.BlockSpec((1, 2, FFT_KB, cols), lambda o, j: (2 * o + 1, 0, j, 0)),
                  pl.BlockSpec((FFT_KB, r2, r2), lambda o, j: (j, 0, 0)),
                  pl.BlockSpec(l1.shape, lambda o, j: (0, 0, 0))],
        out_specs=pl.BlockSpec((1, FFT_KB, r2, c), lambda o, j: (o, j, 0, 0)),
        scratch_shapes=[pltpu.VMEM((2, FFT_KB, FFT_R, c), F32)] * 2,
        compiler_params=_cparams(("parallel", "parallel")),
        name="filter_spectrum",
    )(s_filt, s_filt, g2, l1)


def _mid_body(s_ref, g_ref, gh_ref, kf_ref, t_ref, s3, t3):
    j = pl.program_id(1)
    r2 = 2 * FFT_R
    c = kf_ref.shape[-1]
    _rows_to_slabs(s_ref, s3, c)
    for i in range(FFT_KB):
        @pl.when(j * FFT_KB + i < FFT_KA)
        def _():
            x = _dot(g_ref[i], _slab(s3, i))
            xr, xi = x[0:FFT_R], x[FFT_R:r2]
            kr, ki = kf_ref[0, i, 0:FFT_R], kf_ref[0, i, FFT_R:r2]
            y = jnp.concatenate([xr * kr - xi * ki, xr * ki + xi * kr], axis=0).astype(BF16)
            t = _dot(gh_ref[i], y)
            t3[0, i] = t[0:FFT_R]
            t3[1, i] = t[FFT_R:r2]

        @pl.when(j * FFT_KB + i >= FFT_KA)
        def _():
            t3[0, i] = jnp.zeros((FFT_R, c), F32)
            t3[1, i] = jnp.zeros((FFT_R, c), F32)

    _slabs_to_rows(t3, t_ref, c)


def _conv_mid(s, kf, order, g2, g2h, c):
    n, _, _, cols = s.shape
    r2 = 2 * FFT_R
    blk = pl.BlockSpec((1, 2, FFT_KB, cols), lambda i, j: (i, 0, j, 0))
    gspec = pl.BlockSpec((FFT_KB, r2, r2), lambda i, j: (j, 0, 0))
    return pl.pallas_call(
        _mid_body,
        out_shape=jax.ShapeDtypeStruct(s.shape, F32),
        grid=(n, FFT_KA_PAD // FFT_KB),
        in_specs=[blk, gspec, gspec,
                  pl.BlockSpec((1, FFT_KB, r2, c), lambda i, j: (order, j, 0, 0))],
        out_specs=blk,
        scratch_shapes=[pltpu.VMEM((2, FFT_KB, FFT_R, c), F32)] * 2,
        compiler_params=_cparams(("parallel", "parallel")),
        name="conv_mid",
    )(s, g2, g2h, kf)


def _inv1_body(m_ref, t_ref, u_ref, xg_ref, sk_ref, o_ref):
    c = u_ref.shape[-1]
    t2 = t_ref[0].reshape(2 * FFT_KA_PAD, FFT_NB * c).astype(BF16)
    y = _dot(m_ref[...], t2)
    for bb in range(FFT_NB):
        conv = y[:, bb * c:(bb + 1) * c] + u_ref[0, :, bb, :] * sk_ref[...]
        o_ref[0, :, bb, :] = xg_ref[0, :, bb, :] * conv


def _conv_inv1(t, u, xg, skip, m_inv):
    n, seq, c = u.shape
    a_half = FFT_A // 2
    sk = skip.astype(F32).reshape(1, c)
    uspec = pl.BlockSpec((1, a_half, FFT_NB, c), lambda i, j: (i, 0, j, 0))
    view = lambda a: a.reshape(n, a_half, FFT_R, c)
    out = pl.pallas_call(
        _inv1_body,
        out_shape=jax.ShapeDtypeStruct((n, a_half, FFT_R, c), F32),
        grid=(n, FFT_R // FFT_NB),
        in_specs=[pl.BlockSpec(m_inv.shape, lambda i, j: (0, 0)),
                  pl.BlockSpec((1, 2, FFT_KA_PAD, FFT_NB * c), lambda i, j: (i, 0, 0, j)),
                  uspec, uspec,
                  pl.BlockSpec((1, c), lambda i, j: (0, 0))],
        out_specs=uspec,
        compiler_params=_cparams(("parallel", "parallel")),
        name="conv_inv1",
    )(m_inv, t, view(u), view(xg), sk)
    return out.reshape(n, seq, c)


def _hyena(v, x1, x2, f_w1, f_b1, f_w2, f_b2, f_w3, f_freq, skip):
    _, seq, c = v.shape
    assert 2 * seq == FFT_A * FFT_R
    m_fwd, m_inv, g2, g2h = _conv_dft_constants()
    taps, l1 = _hyena_filter_taps(seq, f_w1, f_b1, f_w2, f_b2, f_w3, f_freq, c)
    kf = _filter_spectrum(_conv_fwd1(taps, m_fwd), l1, g2, c)
    y = v
    for order, xg in enumerate((x1, x2)):
        t = _conv_mid(_conv_fwd1(y, m_fwd), kf, order, g2, g2h, c)
        y = _conv_inv1(t, y, xg, skip[order], m_inv)
    return y


def _mix_ffn_body(x_ref, a1_ref, a2_ref, wm_ref, gm_ref, g_ref, sh_ref, sc_ref, gt_ref,
                  w1_ref, w3_ref, w2_ref, o_ref, x_scr, h_scr, acc_scr):
    j = pl.program_id(2)

    @pl.when(j == 0)
    def _():
        c1 = a1_ref.shape[-1]
        mixed = _dot(a1_ref[0].astype(BF16), wm_ref[0:c1]) + _dot(a2_ref[0].astype(BF16), wm_ref[c1:])
        xm = x_ref[0] + gm_ref[0] * mixed
        x_scr[...] = xm
        h_scr[...] = _norm_mod(xm, g_ref[...], sh_ref[0], sc_ref[0]).astype(BF16)
        acc_scr[...] = jnp.zeros_like(acc_scr)

    h = h_scr[...]
    a = _dot(h, w1_ref[...])
    u = (a * jax.nn.sigmoid(a) * _dot(h, w3_ref[...])).astype(BF16)
    acc_scr[...] += _dot(u, w2_ref[...])

    @pl.when(j == pl.num_programs(2) - 1)
    def _():
        o_ref[0] = x_scr[...] + gt_ref[0] * acc_scr[...]


def _mix_ffn(x, a1, a2, w_mix, gate_mix, g, shift, scale, gate, w1, w3, w2, tm=512, fb=1408):
    b, s, d = x.shape
    f = w1.shape[1]
    tok = lambda c: pl.BlockSpec((1, tm, c), lambda bi, i, j: (bi, i, 0))
    per_b = pl.BlockSpec((1, 1, d), lambda bi, i, j: (bi, 0, 0))
    const = lambda a: pl.BlockSpec(a.shape, lambda bi, i, j: (0, 0))
    return pl.pallas_call(
        _mix_ffn_body,
        out_shape=jax.ShapeDtypeStruct(x.shape, F32),
        grid=(b, s // tm, f // fb),
        in_specs=[tok(d), tok(a1.shape[-1]), tok(a2.shape[-1]), const(w_mix), per_b,
                  const(g), per_b, per_b, per_b,
                  pl.BlockSpec((d, fb), lambda bi, i, j: (0, j)),
                  pl.BlockSpec((d, fb), lambda bi, i, j: (0, j)),
                  pl.BlockSpec((fb, d), lambda bi, i, j: (j, 0))],
        out_specs=tok(d),
        scratch_shapes=[pltpu.VMEM((tm, d), F32), pltpu.VMEM((tm, d), BF16), pltpu.VMEM((tm, d), F32)],
        compiler_params=_cparams(("parallel", "parallel", "arbitrary")),
        name="mix_ffn",
    )(x, a1, a2, w_mix, gate_mix, g, shift, scale, gate, w1, w3, w2)


def _fm_constants(cg):
    j = np.arange(cg)[:, None]
    m = np.arange(cg)[None, :]
    ph = 2.0 * np.pi * j * m / cg
    w_cs = np.concatenate([np.cos(ph), np.sin(ph)], axis=1)
    d = np.arange(FM_A)[:, None]
    a = np.arange(FM_A)[None, :]
    ph = 2.0 * np.pi * d * a / FM_A
    fr, fi = np.cos(ph), -np.sin(ph)
    m1 = np.block([[fr, fi], [fi, -fr]])
    n = FM_A * FM_A
    dd = np.arange(FM_A)[:, None, None]
    c = np.arange(FM_A)[None, :, None]
    b = np.arange(FM_A)[None, None, :]
    th = 2.0 * np.pi * (b * c / FM_A + b * dd / n)
    gcat = np.concatenate([np.cos(th), np.sin(th)], axis=2)
    return _mxu_const(w_cs), _mxu_const(m1), _mxu_const(gcat)


def _fm_front_body(x_ref, g_ref, sh_ref, sc_ref, w_ref, m_ref, o_ref, *, cg, nb):
    d = x_ref.shape[-1]
    xs = jnp.concatenate([x_ref[0, :, bb, :] for bb in range(nb)], axis=0)
    h = _norm_mod(xs, g_ref[...], sh_ref[0], sc_ref[0]).astype(BF16)
    pq = [_dot(h[:, grp * cg:(grp + 1) * cg], w_ref[...]) for grp in range(d // cg)]
    p = jnp.concatenate([t[:, 0:cg] for t in pq], axis=1)
    q = jnp.concatenate([t[:, cg:2 * cg] for t in pq], axis=1)
    for bb in range(nb):
        rows = slice(bb * FM_A, (bb + 1) * FM_A)
        res = _dot(m_ref[...], jnp.concatenate([p[rows], q[rows]], axis=0).astype(BF16))
        o_ref[0, 0, :, bb, :] = res[0:FM_A]
        o_ref[0, 1, :, bb, :] = res[FM_A:2 * FM_A]


def _fm_front(x, g, shift, scale, w_cs, m1, nb=8):
    b, s, d = x.shape
    cg = w_cs.shape[0]
    per_b = pl.BlockSpec((1, 1, d), lambda bi, j: (bi, 0, 0))
    const = lambda a: pl.BlockSpec(a.shape, lambda bi, j: (0, 0))
    return pl.pallas_call(
        functools.partial(_fm_front_body, cg=cg, nb=nb),
        out_shape=jax.ShapeDtypeStruct((b, 2, FM_A, s // FM_A, d), F32),
        grid=(b, s // FM_A // nb),
        in_specs=[pl.BlockSpec((1, FM_A, nb, d), lambda bi, j: (bi, 0, j, 0)),
                  const(g), per_b, per_b, const(w_cs), const(m1)],
        out_specs=pl.BlockSpec((1, 2, FM_A, nb, d), lambda bi, j: (bi, 0, 0, j, 0)),
        compiler_params=_cparams(("parallel", "parallel")),
        name="fm_front",
    )(x.reshape(b, FM_A, s // FM_A, d), g, shift, scale, w_cs, m1)


def _fm_s2_body(s_ref, g_ref, o_ref, *, dblk, scale):
    for i in range(dblk):
        s2 = jnp.concatenate([s_ref[0, 0, i], s_ref[0, 1, i]], axis=0).astype(BF16)
        o_ref[:, i, :] = _dot(g_ref[i], s2) * scale


def _fm_stage2(sv, gcat, seq, d, dblk=8):
    b = sv.shape[0]
    scale = 1.0 / math.sqrt(seq * (d // F_GROUPS))
    out = pl.pallas_call(
        functools.partial(_fm_s2_body, dblk=dblk, scale=scale),
        out_shape=jax.ShapeDtypeStruct((b * FM_A, dblk * (FM_A // dblk), d), F32),
        grid=(b, FM_A // dblk),
        in_specs=[pl.BlockSpec((1, 2, dblk, FM_A, d), lambda bi, j: (bi, 0, j, 0, 0)),
                  pl.BlockSpec((dblk, FM_A, 2 * FM_A), lambda bi, j: (j, 0, 0))],
        out_specs=pl.BlockSpec((FM_A, dblk, d), lambda bi, j: (bi, j, 0)),
        compiler_params=_cparams(("parallel", "parallel")),
        name="fm_stage2",
    )(sv, gcat)
    return out.reshape(b, seq, d)


def _fourier_mix(x, g, shift, scale):
    b, s, d = x.shape
    assert s == FM_A * FM_A
    w_cs, m1, gcat = _fm_constants(d // F_GROUPS)
    return _fm_stage2(_fm_front(x, g, shift, scale, w_cs, m1), gcat, s, d)


LANES = 128
ROW_SL = 8
MOE_TM = 1024
DMA_WINDOW = 128


def _router_body(x_ref, yf_ref, wf_ref, gf_ref, g_ref, sh_ref, sc_ref, wr_ref, br_ref,
                 xo_ref, h_ref, meta_ref, gw_ref, cnt_ref, carry):
    i = pl.program_id(0)

    @pl.when(i == 0)
    def _():
        carry[...] = jnp.zeros_like(carry)

    xm = x_ref[...] + gf_ref[0] * _dot(yf_ref[...].astype(BF16), wf_ref[...])
    xo_ref[...] = xm
    h = _norm_mod(xm, g_ref[...], sh_ref[0], sc_ref[0])
    _rows_to_tiles(h_ref, h)
    h_hi = h.astype(BF16)
    h_lo = (h - h_hi.astype(F32)).astype(BF16)
    by_hi = _dot(h_hi, wr_ref[...])
    logits = by_hi[:, 0:LANES] + by_hi[:, LANES:] + _dot(h_lo, wr_ref[:, 0:LANES]) + br_ref[...]
    lane = lax.broadcasted_iota(jnp.int32, logits.shape, 1)
    nl = logits.shape[-1]
    m1 = jnp.max(logits, axis=-1, keepdims=True)
    i1 = jnp.min(jnp.where(logits == m1, lane, nl), axis=-1, keepdims=True)
    rest = jnp.where(lane == i1, -3.0e38, logits)
    m2 = jnp.max(rest, axis=-1, keepdims=True)
    i2 = jnp.min(jnp.where(rest == m2, lane, nl), axis=-1, keepdims=True)
    e = jnp.exp(m2 - m1)
    gw_ref[...] = jnp.where(lane == 0, 1.0 / (1.0 + e), jnp.where(lane == 1, e / (1.0 + e), 0.0))
    onehot = jnp.where((lane == i1) | (lane == i2), 1.0, 0.0)
    tm = onehot.shape[0]
    earlier = lax.broadcasted_iota(jnp.int32, (tm, tm), 0) > lax.broadcasted_iota(jnp.int32, (tm, tm), 1)
    excl = _dot(jnp.where(earlier, 1.0, 0.0).astype(BF16), onehot.astype(BF16)) + carry[...]
    r1 = jnp.sum(jnp.where(lane == i1, excl, 0.0), axis=-1, keepdims=True).astype(jnp.int32)
    r2 = jnp.sum(jnp.where(lane == i2, excl, 0.0), axis=-1, keepdims=True).astype(jnp.int32)
    meta_ref[...] = jnp.where(lane == 0, i1, jnp.where(lane == 1, i2, jnp.where(lane == 2, r1, jnp.where(lane == 3, r2, 0))))
    carry[...] = carry[...] + jnp.sum(onehot, axis=0, keepdims=True)
    cnt_ref[...] = carry[...]


def _router(x, y_f, w_f, gate_f, g, shift, scale, w_router, b_router, tm=512):
    b, s, d = x.shape
    t = b * s
    ne = w_router.shape[1]
    wr = jnp.pad(w_router.astype(F32), ((0, 0), (0, LANES - ne)))
    wr_hi = wr.astype(BF16)
    wr = jnp.concatenate([wr_hi, (wr - wr_hi.astype(F32)).astype(BF16)], axis=1)
    br = jnp.pad(b_router.astype(F32).reshape(1, ne), ((0, 0), (0, LANES - ne)), constant_values=NEG_INF)
    spt = s // tm
    per_b = pl.BlockSpec((1, 1, d), lambda i: (i // spt, 0, 0))
    const = lambda a: pl.BlockSpec(a.shape, lambda i: (0, 0))
    tok = pl.BlockSpec((tm, d), lambda i: (i, 0))
    return pl.pallas_call(
        _router_body,
        out_shape=[jax.ShapeDtypeStruct((t, d), F32),
                   jax.ShapeDtypeStruct((t * ROW_SL, LANES), F32),
                   jax.ShapeDtypeStruct((t, LANES), jnp.int32),
                   jax.ShapeDtypeStruct((t, LANES), F32),
                   jax.ShapeDtypeStruct((1, LANES), F32)],
        grid=(t // tm,),
        in_specs=[tok, tok, const(w_f), per_b, const(g), per_b, per_b, const(wr), const(br)],
        out_specs=[tok,
                   pl.BlockSpec((tm * ROW_SL, LANES), lambda i: (i, 0)),
                   pl.BlockSpec((tm, LANES), lambda i: (i, 0)),
                   pl.BlockSpec((tm, LANES), lambda i: (i, 0)),
                   pl.BlockSpec((1, LANES), lambda i: (0, 0))],
        scratch_shapes=[pltpu.VMEM((1, LANES), F32)],
        compiler_params=_cparams(("arbitrary",)),
        name="router",
    )(x.reshape(t, d), y_f.reshape(t, d), w_f, gate_f, g, shift, scale, wr, br)


def _moe_plan(meta, counts, ne, tm):
    i1, i2, r1, r2 = meta[:, 0], meta[:, 1], meta[:, 2], meta[:, 3]
    cnt = counts[0, :ne].astype(jnp.int32)
    padded = ((cnt + tm - 1) // tm) * tm
    ends = jnp.cumsum(padded)
    offs = ends - padded
    pick = lambda idx: sum(jnp.where(idx == e, offs[e], 0) for e in range(ne))
    pos = jnp.concatenate([pick(i1) + r1, pick(i2) + r2]).astype(jnp.int32)
    n_tiles = (2 * meta.shape[0]) // tm + ne
    n_used = (ends[ne - 1] // tm).astype(jnp.int32)
    tile_start = jnp.minimum(jnp.arange(n_tiles, dtype=jnp.int32), n_used - 1) * tm
    tile_expert = jnp.sum(tile_start[:, None] >= ends[None, :], axis=1).astype(jnp.int32)
    group_end = sum(jnp.where(tile_expert == e, offs[e] + cnt[e], 0) for e in range(ne))
    tile_rows = jnp.clip(group_end - tile_start, 0, tm).astype(jnp.int32)
    return pos, offs + cnt, padded - cnt, tile_expert, n_used.reshape(1), tile_rows


def _windowed_copies(n, start_copy, wait_one, per_iter):
    def body(i, carry):
        @pl.when(i >= DMA_WINDOW)
        def _():
            for _ in range(per_iter):
                wait_one()
        start_copy(i)
        return carry

    lax.fori_loop(0, n, body, 0)

    def drain(i, carry):
        for _ in range(per_iter):
            wait_one()
        return carry

    lax.fori_loop(0, jnp.minimum(n, DMA_WINDOW), drain, 0)


def _tile_of(ref, row):
    return ref.at[pl.ds(pl.multiple_of(row * ROW_SL, ROW_SL), ROW_SL)]


def _tiles_to_rows(ref, n, first=0):
    return jnp.concatenate([ref[pl.ds(first * ROW_SL + sl, n, stride=ROW_SL), :] for sl in range(ROW_SL)], axis=1)


def _rows_to_tiles(ref, val):
    n = val.shape[0]
    for sl in range(ROW_SL):
        ref[pl.ds(sl, n, stride=ROW_SL), :] = val[:, sl * LANES:(sl + 1) * LANES]


def _dispatch_body(pos_ref, pad_start_ref, pad_n_ref, h_ref, xs_hbm, sem, *, n_tok, ne):
    i = pl.program_id(0)
    td = h_ref.shape[0] // ROW_SL
    base = i * td
    copy = lambda src, dst: pltpu.make_async_copy(_tile_of(h_ref, src), _tile_of(xs_hbm, dst), sem)
    wait_one = lambda: copy(0, 0).wait()

    def start_token(r, carry):
        copy(r, pos_ref[base + r]).start(priority=0)
        copy(r, pos_ref[n_tok + base + r]).start(priority=1)
        return carry

    lax.fori_loop(0, td, start_token, 0, unroll=8)
    whole_tile = pltpu.make_async_copy(h_ref, xs_hbm.at[pl.ds(0, td * ROW_SL)], sem)
    whole_tile.wait()
    whole_tile.wait()

    @pl.when(i == 0)
    def _():
        for e in range(ne):
            first = pad_start_ref[e]
            _windowed_copies(pad_n_ref[e], lambda r: copy(0, first + r).start(), wait_one, 1)


def _moe_dispatch(h3, pos, pad_start, pad_n, n_rows, td=1024):
    n_tok = h3.shape[0] // ROW_SL
    ne = pad_start.shape[0]
    return pl.pallas_call(
        functools.partial(_dispatch_body, n_tok=n_tok, ne=ne),
        out_shape=jax.ShapeDtypeStruct((n_rows * ROW_SL, LANES), h3.dtype),
        grid_spec=pltpu.PrefetchScalarGridSpec(
            num_scalar_prefetch=3, grid=(n_tok // td,),
            in_specs=[pl.BlockSpec((td * ROW_SL, LANES), lambda i, p, ps, pn: (i, 0))],
            out_specs=pl.BlockSpec(memory_space=pl.ANY),
            scratch_shapes=[pltpu.SemaphoreType.DMA(())]),
        compiler_params=_cparams(("arbitrary",)),
        name="moe_dispatch",
    )(pos, pad_start, pad_n, h3)


def _moe_grouped_body(te_ref, nu_ref, tr_ref, xs_ref, w1_ref, w3_ref, w2_ref, y_ref, xb_scr, acc_scr):
    i = pl.program_id(0)
    j = pl.program_id(1)
    tm = xb_scr.shape[0]
    hm = tm // 2

    def expert_rows(nrows):
        h = xb_scr[0:nrows]
        a = _dot(h, w1_ref[0].astype(BF16))
        u = (a * jax.nn.sigmoid(a) * _dot(h, w3_ref[0].astype(BF16))).astype(BF16)
        part = _dot(u, w2_ref[0].astype(BF16))

        @pl.when(j == 0)
        def _():
            acc_scr[0:nrows] = part

        @pl.when(j > 0)
        def _():
            acc_scr[0:nrows] += part

    @pl.when(i < nu_ref[0])
    def _():
        @pl.when(j == 0)
        def _():
            xb_scr[...] = _tiles_to_rows(xs_ref, tm).astype(BF16)

        @pl.when(tr_ref[i] > hm)
        def _():
            expert_rows(tm)

        @pl.when(tr_ref[i] <= hm)
        def _():
            expert_rows(hm)

            @pl.when(j == 0)
            def _():
                acc_scr[hm:tm] = jnp.zeros((tm - hm, acc_scr.shape[1]), F32)

        @pl.when(j == pl.num_programs(1) - 1)
        def _():
            _rows_to_tiles(y_ref, acc_scr[...])


def _moe_grouped(xs, tile_expert, n_used, tile_rows, w1, w3, w2, tm, fb=512):
    ne, d, f = w1.shape
    n_rows = xs.shape[0] // ROW_SL
    nj = f // fb
    row_tile = lambda i, j, te, nu, tr: (jnp.maximum(jnp.minimum(i, nu[0] - 1), 0), 0)
    jj = lambda i, j, nu: jnp.where(i < nu[0], j, nj - 1)
    return pl.pallas_call(
        _moe_grouped_body,
        out_shape=jax.ShapeDtypeStruct(xs.shape, F32),
        grid_spec=pltpu.PrefetchScalarGridSpec(
            num_scalar_prefetch=3, grid=(n_rows // tm, nj),
            in_specs=[pl.BlockSpec((tm * ROW_SL, LANES), row_tile),
                      pl.BlockSpec((1, d, fb), lambda i, j, te, nu, tr: (te[i], 0, jj(i, j, nu))),
                      pl.BlockSpec((1, d, fb), lambda i, j, te, nu, tr: (te[i], 0, jj(i, j, nu))),
                      pl.BlockSpec((1, fb, d), lambda i, j, te, nu, tr: (te[i], jj(i, j, nu), 0))],
            out_specs=pl.BlockSpec((tm * ROW_SL, LANES), row_tile),
            scratch_shapes=[pltpu.VMEM((tm, d), BF16), pltpu.VMEM((tm, d), F32)]),
        compiler_params=_cparams(("arbitrary", "arbitrary")),
        name="moe_grouped",
    )(tile_expert, n_used, tile_rows, xs, w1, w3, w2)


def _moe_final_body(pos_ref, x_ref, y_hbm, gw_ref, gt_ref, fg_ref, o_ref, yg_scr, sem, *, n_tok):
    i = pl.program_id(0)
    tc = x_ref.shape[0]
    slot = i % 2

    def gather_tile(step, into):
        base = step * tc

        def start_token(r, carry):
            dst = yg_scr.at[into]
            pltpu.make_async_copy(_tile_of(y_hbm, pos_ref[base + r]), _tile_of(dst, r),
                                  sem.at[into]).start(priority=0)
            pltpu.make_async_copy(_tile_of(y_hbm, pos_ref[n_tok + base + r]), _tile_of(dst, tc + r),
                                  sem.at[into]).start(priority=1)
            return carry

        lax.fori_loop(0, tc, start_token, 0, unroll=8)

    @pl.when(i == 0)
    def _():
        gather_tile(0, 0)

    @pl.when(i + 1 < pl.num_programs(0))
    def _():
        gather_tile(i + 1, 1 - slot)

    pltpu.make_async_copy(y_hbm.at[pl.ds(0, 2 * tc * ROW_SL)], yg_scr.at[slot], sem.at[slot]).wait()
    gw = gw_ref[...]
    rows = yg_scr.at[slot]
    y = gw[:, 0:1] * _tiles_to_rows(rows, tc) + gw[:, 1:2] * _tiles_to_rows(rows, tc, first=tc)
    xo = x_ref[...] + gt_ref[0] * y
    ms = jnp.mean(xo * xo, axis=-1, keepdims=True)
    o_ref[...] = xo * lax.rsqrt(ms + EPS) * fg_ref[...]


def _moe_final(x, y, pos, gw, gt, final_g, tc=512):
    b, s, d = x.shape
    t = b * s
    spt = s // tc
    nsl = d // LANES
    out = pl.pallas_call(
        functools.partial(_moe_final_body, n_tok=t),
        out_shape=jax.ShapeDtypeStruct((t, d), F32),
        grid_spec=pltpu.PrefetchScalarGridSpec(
            num_scalar_prefetch=1, grid=(t // tc,),
            in_specs=[pl.BlockSpec((tc, d), lambda i, p: (i, 0)),
                      pl.BlockSpec(memory_space=pl.ANY),
                      pl.BlockSpec((tc, LANES), lambda i, p: (i, 0)),
                      pl.BlockSpec((1, 1, d), lambda i, p: (i // spt, 0, 0)),
                      pl.BlockSpec(final_g.shape, lambda i, p: (0, 0))],
            out_specs=pl.BlockSpec((tc, d), lambda i, p: (i, 0)),
            scratch_shapes=[pltpu.VMEM((2, 2 * tc * ROW_SL, LANES), F32), pltpu.SemaphoreType.DMA((2,))]),
        compiler_params=_cparams(("arbitrary",)),
        name="moe_final",
    )(pos, x.reshape(t, d), y, gw, gt, final_g)
    return out.reshape(b, s, d)


def _moe_routed(x, y_f, w_f, gate_f, g, shift, scale, gt, final_g, w_router, b_router, w1, w3, w2):
    ne = w1.shape[0]
    tm = MOE_TM
    x1, h3, meta, gw, counts = _router(x, y_f, w_f, gate_f, g, shift, scale, w_router, b_router)
    pos, pad_start, pad_n, tile_expert, n_used, tile_rows = _moe_plan(meta, counts, ne, tm)
    assert x.shape[-1] == ROW_SL * LANES
    n_rows = (2 * (h3.shape[0] // ROW_SL) // tm + ne) * tm
    xs = _moe_dispatch(h3, pos, pad_start, pad_n, n_rows)
    y = _moe_grouped(xs, tile_expert, n_used, tile_rows, w1, w3, w2, tm)
    return _moe_final(x1.reshape(x.shape), y, pos, gw, gt, final_g)


def kernel(x, c, ctx, c_ctx, w_ada, b_ada, norm_g, w_in, hy_short_w, hy_short_b, hy_f_w1, hy_f_b1, hy_f_w2, hy_f_b2, hy_f_w3, hy_f_freq, hy_skip, na_rpb, w_mix_out, ffn_w1, ffn_w3, ffn_w2, w_fourier, w_router, b_router, moe_w1, moe_w3, moe_w2, final_g):
    b, s, d = x.shape
    depth = w_ada.shape[0]
    assert depth == 2, "layer 0 mixes with Hyena/attention, layer 1 with Fourier/MoE"
    c_hy = hy_skip.shape[-1]
    c_na = d - c_hy

    cvec = jnp.concatenate([c, c_ctx[None, :], jnp.zeros((8 - b - 1, d), F32)], axis=0)
    mods = _ada(cvec, w_ada, b_ada)

    def mod(layer, idx, ctx_row=False):
        m = mods[layer, :, idx * d:(idx + 1) * d]
        return m[b:b + 1, None, :] if ctx_row else m[0:b, None, :]

    row = lambda a: a.reshape(1, -1)

    w_in0 = w_in[0].astype(BF16)
    w_hy, w_qkv = w_in0[:, 0:3 * c_hy], w_in0[:, 3 * c_hy:]
    v, x1, x2, q, k, va = _inproj(x, row(norm_g[0, 0]), mod(0, 0), mod(0, 1), w_hy, w_qkv,
                                  hy_short_w[0], row(hy_short_b[0]))
    kc, vc = _ctxkv(ctx, row(norm_g[0, 0]), mod(0, 0, True), mod(0, 1, True), w_qkv[:, c_na:])
    y_na = _natt(q, k, va, kc, vc, _na_bias_table(na_rpb[0]))
    y_hy = _hyena(v, x1, x2, hy_f_w1[0], hy_f_b1[0], hy_f_w2[0], hy_f_b2[0], hy_f_w3[0],
                  hy_f_freq[0], hy_skip[0])
    x = _mix_ffn(x, y_hy, y_na, w_mix_out[0].astype(BF16), mod(0, 2),
                 row(norm_g[0, 1]), mod(0, 3), mod(0, 4), mod(0, 5),
                 ffn_w1[0].astype(BF16), ffn_w3[0].astype(BF16), ffn_w2[0].astype(BF16))

    y_f = _fourier_mix(x, row(norm_g[1, 0]), mod(1, 0), mod(1, 1))
    return _moe_routed(x, y_f, w_fourier[0].astype(BF16), mod(1, 2),
                       row(norm_g[1, 1]), mod(1, 3), mod(1, 4), mod(1, 5), row(final_g),
                       w_router[0], b_router[0],
                       moe_w1[0], moe_w3[0], moe_w2[0])
```

```python
import functools
import math

import numpy as np
import jax
import jax.numpy as jnp
from jax import lax
from jax.experimental import pallas as pl
from jax.experimental.pallas import tpu as pltpu

F32 = jnp.float32
BF16 = jnp.bfloat16
HIGHEST = lax.Precision.HIGHEST

GRID_W = 64
NA_HEAD_DIM = 32
NA_WIN_R = 8
NA_WIN_C = 16
HYENA_EMB = 33
HYENA_BANDS = (HYENA_EMB - 1) // 2
HYENA_FAST_DECAY = 0.3
HYENA_SLOW_DECAY = 1.5
HYENA_TARGET = 1e-2
F_GROUPS = 4
N_MOD = 6
EPS = 1e-6
NEG_INF = -1e30

FFT_A = 64
FFT_R = 128
FFT_KA = FFT_A // 2 + 1
FFT_KA_PAD = 40
FM_A = 64

VMEM_LIMIT = 48 * 1024 * 1024
FFN_VMEM_LIMIT = 56 * 1024 * 1024


def _cparams(sem, vmem_limit=VMEM_LIMIT):
    return pltpu.CompilerParams(dimension_semantics=sem, vmem_limit_bytes=vmem_limit)


def _dot(a, b):
    return jnp.dot(a, b, preferred_element_type=F32)


def _mxu_const(m):
    return jnp.asarray(m, dtype=F32).astype(BF16)


def _norm_mod(x, g, shift, scale):
    ms = jnp.mean(x * x, axis=-1, keepdims=True)
    y = x * lax.rsqrt(ms + EPS) * g
    return y * (1.0 + scale) + shift


def _ada_body(c_ref, w_ref, b_ref, o_ref):
    cv = c_ref[...]
    s = cv * jax.nn.sigmoid(cv)
    o_ref[0] = jnp.dot(s, w_ref[0], precision=HIGHEST, preferred_element_type=F32) + b_ref[0]


def _ada(cvec, w_ada, b_ada):
    depth, d, n = w_ada.shape
    rows = cvec.shape[0]
    bn = n // 4
    return pl.pallas_call(
        _ada_body,
        out_shape=jax.ShapeDtypeStruct((depth, rows, n), F32),
        grid=(depth, n // bn),
        in_specs=[pl.BlockSpec((rows, d), lambda l, j: (0, 0)),
                  pl.BlockSpec((1, d, bn), lambda l, j: (l, 0, j)),
                  pl.BlockSpec((1, 1, bn), lambda l, j: (l, 0, j))],
        out_specs=pl.BlockSpec((1, rows, bn), lambda l, j: (l, 0, j)),
        compiler_params=_cparams(("parallel", "parallel")),
        name="ada",
    )(cvec, w_ada, b_ada.reshape(depth, 1, n))


def _inproj_body(x_ref, xp_ref, xn_ref, g_ref, sh_ref, sc_ref, why_ref, wqkv_ref, sw_ref, sb_ref,
                 v_ref, x1_ref, x2_ref, q_ref, k_ref, va_ref, *, n_tiles, q_scale, c_hy, c_na):
    i = pl.program_id(1)
    g, sh, sc = g_ref[...], sh_ref[0], sc_ref[0]
    hf = _norm_mod(x_ref[0], g, sh, sc)
    h = hf.astype(BF16)
    tm = hf.shape[0]
    hx = jnp.concatenate([_norm_mod(xp_ref[0], g, sh, sc), hf, _norm_mod(xn_ref[0], g, sh, sc)], axis=0)
    zx = _dot(hx.astype(BF16), why_ref[...])
    zh = zx[8:8 + tm]
    zp = jnp.where(i > 0, zx[7:8], 0.0)
    zn = jnp.where(i < n_tiles - 1, zx[8 + tm:9 + tm], 0.0)
    row = lax.broadcasted_iota(jnp.int32, zh.shape, 0)
    z_m1 = jnp.where(row == 0, zp, pltpu.roll(zh, 1, 0))
    z_p1 = jnp.where(row == tm - 1, zn, pltpu.roll(zh, tm - 1, 0))
    sw = sw_ref[...]
    zc = z_m1 * sw[0:1] + zh * sw[1:2] + z_p1 * sw[2:3] + sb_ref[...]
    v_ref[0] = zc[:, 0:c_hy]
    x1_ref[0] = zc[:, c_hy:2 * c_hy]
    x2_ref[0] = zc[:, 2 * c_hy:3 * c_hy]
    zq = _dot(h, wqkv_ref[...])
    q_ref[0] = (zq[:, 0:c_na] * q_scale).astype(BF16)
    k_ref[0] = zq[:, c_na:2 * c_na].astype(BF16)
    va_ref[0] = zq[:, 2 * c_na:3 * c_na].astype(BF16)


def _inproj(x, g, shift, scale, w_hy, w_qkv, short_w, short_b, tm=512):
    b, s, d = x.shape
    c_hy = w_hy.shape[1] // 3
    c_na = w_qkv.shape[1] // 3
    n_tiles = s // tm
    r8 = tm // 8
    body = functools.partial(_inproj_body, n_tiles=n_tiles, q_scale=NA_HEAD_DIM ** -0.5,
                             c_hy=c_hy, c_na=c_na)
    tok = lambda c: pl.BlockSpec((1, tm, c), lambda bi, i: (bi, i, 0))
    full2 = lambda a: pl.BlockSpec(a.shape, lambda bi, i: (0, 0))
    per_b = pl.BlockSpec((1, 1, d), lambda bi, i: (bi, 0, 0))
    return pl.pallas_call(
        body,
        out_shape=[jax.ShapeDtypeStruct((b, s, c_hy), F32)] * 3 + [jax.ShapeDtypeStruct((b, s, c_na), BF16)] * 3,
        grid=(b, n_tiles),
        in_specs=[tok(d),
                  pl.BlockSpec((1, 8, d), lambda bi, i: (bi, jnp.maximum(i * r8 - 1, 0), 0)),
                  pl.BlockSpec((1, 8, d), lambda bi, i: (bi, jnp.minimum((i + 1) * r8, s // 8 - 1), 0)),
                  full2(g), per_b, per_b, full2(w_hy), full2(w_qkv), full2(short_w), full2(short_b)],
        out_specs=[tok(c_hy)] * 3 + [tok(c_na)] * 3,
        compiler_params=_cparams(("parallel", "parallel")),
        name="inproj",
    )(x, x, x, g, shift, scale, w_hy, w_qkv, short_w, short_b)


def _ctxkv_body(x_ref, g_ref, sh_ref, sc_ref, w_ref, k_ref, v_ref, *, c_na):
    h = _norm_mod(x_ref[0], g_ref[...], sh_ref[0], sc_ref[0]).astype(BF16)
    z = _dot(h, w_ref[...])
    k_ref[0] = z[:, 0:c_na].astype(BF16)
    v_ref[0] = z[:, c_na:2 * c_na].astype(BF16)


def _ctxkv(ctx, g, shift, scale, w_kv):
    b, n, d = ctx.shape
    c_na = w_kv.shape[1] // 2
    one = pl.BlockSpec((1, 1, d), lambda bi: (0, 0, 0))
    return pl.pallas_call(
        functools.partial(_ctxkv_body, c_na=c_na),
        out_shape=[jax.ShapeDtypeStruct((b, n, c_na), BF16)] * 2,
        grid=(b,),
        in_specs=[pl.BlockSpec((1, n, d), lambda bi: (bi, 0, 0)),
                  pl.BlockSpec(g.shape, lambda bi: (0, 0)), one, one,
                  pl.BlockSpec(w_kv.shape, lambda bi: (0, 0))],
        out_specs=[pl.BlockSpec((1, n, c_na), lambda bi: (bi, 0, 0))] * 2,
        compiler_params=_cparams(("parallel",)),
        name="ctxkv",
    )(ctx, g, shift, scale, w_kv)


NA_HEADS_PER_BLK = 8
NA_ROWS_PER_STEP = 8


def _na_bias_body(r_ref, e_ref, ok_ref, o_ref):
    t = jnp.dot(r_ref[...], e_ref[...], precision=HIGHEST, preferred_element_type=F32)
    o_ref[...] = jnp.where(ok_ref[...] > 0.5, t, NEG_INF)


def _na_bias_table(rpb):
    w = GRID_W
    h, nr, nc = rpb.shape
    col = np.arange(w)[:, None]
    kc = np.arange(w)[None, :]
    c_start = np.clip(col - NA_WIN_C // 2, 0, w - NA_WIN_C)
    valid = ((kc >= c_start) & (kc < c_start + NA_WIN_C)).reshape(1, w * w)
    expand = (np.arange(32)[:, None, None] == (kc - col + NA_WIN_C - 1)[None]).reshape(32, w * w)
    rp = jnp.pad(rpb.astype(F32).reshape(h * nr, nc), ((0, 0), (0, 32 - nc)))
    full = lambda a: pl.BlockSpec(a.shape, lambda: (0,) * a.ndim)
    expand = jnp.asarray(expand, dtype=F32)
    ok = jnp.asarray(valid, dtype=F32)
    toep = pl.pallas_call(
        _na_bias_body,
        out_shape=jax.ShapeDtypeStruct((h * nr, w * w), F32),
        in_specs=[full(rp), full(expand), full(ok)],
        out_specs=pl.BlockSpec((h * nr, w * w), lambda: (0, 0)),
        name="na_bias",
    )(rp, expand, ok)
    t2 = toep.reshape(h, nr, w, w).transpose(0, 2, 1, 3).reshape(h, w, nr * w)
    slabs = jnp.stack([t2[:, :, (NA_WIN_R - 1 - off) * w:(2 * NA_WIN_R - 1 - off) * w]
                       for off in range(NA_WIN_R)], axis=1)
    hpb = NA_HEADS_PER_BLK
    slabs = slabs.reshape(h // hpb, hpb, NA_WIN_R, w, NA_WIN_R * w).transpose(0, 2, 1, 3, 4)
    return slabs.reshape(h // hpb, NA_WIN_R, hpb * w, NA_WIN_R * w)


def _natt_body(q_ref, k_ref, v_ref, kc_ref, vc_ref, bias_ref, o_ref, *, rows):
    w = GRID_W
    hpb = NA_HEADS_PER_BLK
    nloc = NA_WIN_R * w
    lane = lax.broadcasted_iota(jnp.int32, (1, hpb * NA_HEAD_DIM), 1)
    in_head = [(lane >= NA_HEAD_DIM * hh) & (lane < NA_HEAD_DIM * (hh + 1)) for hh in range(hpb)]
    kcx = kc_ref[0]
    vcx = vc_ref[0]
    nt = (((1,), (1,)), ((), ()))

    def one_row(r):
        r0 = jnp.clip(r - NA_WIN_R // 2, 0, rows - NA_WIN_R)
        off = r - r0
        qs = q_ref[0, pl.ds(pl.multiple_of(r * w, w), w), :]
        kw = k_ref[0, pl.ds(pl.multiple_of(r0 * w, w), nloc), :]
        vw = v_ref[0, pl.ds(pl.multiple_of(r0 * w, w), nloc), :]
        zero = jnp.zeros_like(qs)
        qst = jnp.concatenate([jnp.where(m, qs, zero) for m in in_head], axis=0)
        s_loc = lax.dot_general(qst, kw, nt, preferred_element_type=F32) + bias_ref[0, off]
        s_ctx = lax.dot_general(qst, kcx, nt, preferred_element_type=F32)
        m = jnp.maximum(jnp.max(s_loc, axis=-1, keepdims=True), jnp.max(s_ctx, axis=-1, keepdims=True))
        p_loc = jnp.exp(s_loc - m)
        p_ctx = jnp.exp(s_ctx - m)
        den = jnp.sum(p_loc, axis=-1, keepdims=True) + jnp.sum(p_ctx, axis=-1, keepdims=True)
        o = (_dot(p_loc.astype(BF16), vw) + _dot(p_ctx.astype(BF16), vcx)) * (1.0 / den)
        acc = jnp.where(in_head[0], o[0:w], 0.0)
        for hh in range(1, hpb):
            acc = acc + jnp.where(in_head[hh], o[hh * w:(hh + 1) * w], 0.0)
        o_ref[0, pl.ds(pl.multiple_of(r * w, w), w), :] = acc.astype(BF16)

    def row_group(i, carry):
        for r in range(NA_ROWS_PER_STEP):
            one_row(NA_ROWS_PER_STEP * i + r)
        return carry

    lax.fori_loop(0, rows // NA_ROWS_PER_STEP, row_group, 0)


def _natt(q, k, v, kc, vc, bias):
    b, s, c = q.shape
    nctx = kc.shape[1]
    lw = NA_HEADS_PER_BLK * NA_HEAD_DIM
    rows = s // GRID_W
    seq = pl.BlockSpec((1, s, lw), lambda bi, g: (bi, 0, g))
    cx = pl.BlockSpec((1, nctx, lw), lambda bi, g: (bi, 0, g))
    return pl.pallas_call(
        functools.partial(_natt_body, rows=rows),
        out_shape=jax.ShapeDtypeStruct((b, s, c), BF16),
        grid=(b, c // lw),
        in_specs=[seq, seq, seq, cx, cx,
                  pl.BlockSpec((1,) + bias.shape[1:], lambda bi, g: (g, 0, 0, 0))],
        out_specs=seq,
        compiler_params=_cparams(("parallel", "parallel")),
        name="natt",
    )(q, k, v, kc, vc, bias)


def _hyena_feats(seq_len):
    t = jnp.linspace(0.0, 1.0, seq_len, dtype=F32)[:, None]
    bands = jnp.linspace(1e-4, HYENA_BANDS - 1, HYENA_BANDS, dtype=F32)
    ang = (2.0 * math.pi / seq_len) * jnp.arange(seq_len, dtype=F32)[:, None] * bands[None, :]
    feats = jnp.concatenate([t, jnp.cos(ang), -jnp.sin(ang)], axis=-1)
    return jnp.pad(feats, ((0, 0), (0, 128 - HYENA_EMB)))


def _filt_body(feat_ref, w1_ref, b1_ref, w2_ref, b2_ref, w3_ref, fr_ref, dl_ref, o_ref, l1_ref, h_scr,
               *, halves):
    j = pl.program_id(0)
    hp = functools.partial(jnp.dot, precision=HIGHEST, preferred_element_type=F32)
    feats = feat_ref[...]

    @pl.when(j == 0)
    def _():
        fr = fr_ref[...]
        h = jnp.sin(fr[0:1] * (hp(feats, w1_ref[...]) + b1_ref[...]))
        h_scr[...] = jnp.sin(fr[1:2] * (hp(h, w2_ref[...]) + b2_ref[...]))

    hc = hp(h_scr[...], w3_ref[...])
    t = feats[:, 0:1]
    hc = hc * jnp.exp(-t * dl_ref[...])
    row = lax.broadcasted_iota(jnp.int32, hc.shape, 0)
    hc = jnp.where((row == 0) & ((j // halves) % 2 == 1), 0.0, hc)
    l1_ref[0] = jnp.sum(jnp.abs(hc), axis=0, keepdims=True)
    o_ref[0] = hc


def _hyena_filter_taps(seq_len, f_w1, f_b1, f_w2, f_b2, f_w3, f_freq, c_hy):
    feats = _hyena_feats(seq_len)
    hid = f_w1.shape[1]
    w1 = jnp.pad(f_w1.astype(F32), ((0, 128 - HYENA_EMB), (0, 0)))
    deltas = jnp.abs(jnp.linspace(math.log(HYENA_TARGET) / HYENA_SLOW_DECAY,
                                  math.log(HYENA_TARGET) / HYENA_FAST_DECAY, c_hy, dtype=F32))[None, :]
    nblk = f_w3.shape[1] // c_hy
    halves = 2
    cb = c_hy // halves
    c0 = lambda a: pl.BlockSpec(a.shape, lambda j: (0, 0))
    b1, b2 = f_b1.reshape(1, hid), f_b2.reshape(1, hid)
    return pl.pallas_call(
        functools.partial(_filt_body, halves=halves),
        out_shape=[jax.ShapeDtypeStruct((nblk, seq_len, c_hy), F32),
                   jax.ShapeDtypeStruct((nblk, 1, c_hy), F32)],
        grid=(nblk * halves,),
        in_specs=[c0(feats), c0(w1), c0(b1), c0(f_w2), c0(b2),
                  pl.BlockSpec((hid, cb), lambda j: (0, j)), c0(f_freq),
                  pl.BlockSpec((1, cb), lambda j: (0, j % halves))],
        out_specs=[pl.BlockSpec((1, seq_len, cb), lambda j: (j // halves, 0, j % halves)),
                   pl.BlockSpec((1, 1, cb), lambda j: (j // halves, 0, j % halves))],
        scratch_shapes=[pltpu.VMEM((seq_len, hid), F32)],
        compiler_params=_cparams(("arbitrary",)),
        name="hyena_filter",
    )(feats, w1, b1, f_w2, b2, f_w3, f_freq, deltas)


def _conv_dft_constants():
    a_half = FFT_A // 2
    n = FFT_A * FFT_R
    ka = np.arange(FFT_KA)[:, None]
    a = np.arange(a_half)[None, :]
    ph = 2.0 * np.pi * ka * a / FFT_A
    m_fwd = np.zeros((2 * FFT_KA_PAD, a_half))
    m_fwd[:FFT_KA] = np.cos(ph)
    m_fwd[FFT_KA_PAD:FFT_KA_PAD + FFT_KA] = -np.sin(ph)
    wgt = np.where((ka == 0) | (ka == FFT_A // 2), 1.0, 2.0)
    m_inv = np.zeros((a_half, 2 * FFT_KA_PAD))
    m_inv[:, :FFT_KA] = (wgt * np.cos(ph)).T / n
    m_inv[:, FFT_KA_PAD:FFT_KA_PAD + FFT_KA] = (-wgt * np.sin(ph)).T / n
    kb = np.arange(FFT_R)[None, :, None]
    b = np.arange(FFT_R)[None, None, :]
    kaa = np.arange(FFT_KA)[:, None, None]
    th = 2.0 * np.pi * (b * kb / FFT_R + b * kaa / n)
    gr, gi = np.cos(th), -np.sin(th)
    g2 = np.zeros((FFT_KA_PAD, 2 * FFT_R, 2 * FFT_R))
    g2[:FFT_KA] = np.block([[gr, -gi], [gi, gr]])
    grt, git = gr.transpose(0, 2, 1), gi.transpose(0, 2, 1)
    g2h = np.zeros_like(g2)
    g2h[:FFT_KA] = np.block([[grt, git], [-git, grt]])
    return _mxu_const(m_fwd), _mxu_const(m_inv), _mxu_const(g2), _mxu_const(g2h)


FFT_NB = 16


def _fwd1_body(m_ref, u_ref, o_ref):
    u = jnp.concatenate([u_ref[0, :, bb, :] for bb in range(FFT_NB)], axis=1).astype(BF16)
    res = _dot(m_ref[...], u)
    o_ref[0, 0] = res[0:FFT_KA_PAD]
    o_ref[0, 1] = res[FFT_KA_PAD:2 * FFT_KA_PAD]


def _conv_fwd1(u, m_fwd):
    n, seq, c = u.shape
    a_half = FFT_A // 2
    return pl.pallas_call(
        _fwd1_body,
        out_shape=jax.ShapeDtypeStruct((n, 2, FFT_KA_PAD, FFT_R * c), F32),
        grid=(n, FFT_R // FFT_NB),
        in_specs=[pl.BlockSpec(m_fwd.shape, lambda i, j: (0, 0)),
                  pl.BlockSpec((1, a_half, FFT_NB, c), lambda i, j: (i, 0, j, 0))],
        out_specs=pl.BlockSpec((1, 2, FFT_KA_PAD, FFT_NB * c), lambda i, j: (i, 0, 0, j)),
        compiler_params=_cparams(("parallel", "parallel")),
        name="conv_fwd1",
    )(m_fwd, u.reshape(n, a_half, FFT_R, c))


FFT_KB = 8


def _rows_to_slabs(src_ref, dst_scr, c):
    for part in range(2):
        for b in range(FFT_R):
            dst_scr[part, :, b, :] = src_ref[0, part, :, b * c:(b + 1) * c]


def _slabs_to_rows(src_scr, dst_ref, c):
    for part in range(2):
        for b in range(FFT_R):
            dst_ref[0, part, :, b * c:(b + 1) * c] = src_scr[part, :, b, :]


def _slab(scr, i):
    return jnp.concatenate([scr[0, i], scr[1, i]], axis=0).astype(BF16)


def _fwd2f_body(sf_ref, sb_ref, g_ref, l1_ref, kf_ref, f3, b3):
    o = pl.program_id(0)
    j = pl.program_id(1)
    r2 = 2 * FFT_R
    c = kf_ref.shape[-1]
    _rows_to_slabs(sf_ref, f3, c)
    _rows_to_slabs(sb_ref, b3, c)
    inv = 1.0 / (l1_ref[2 * o] + l1_ref[2 * o + 1] + EPS)
    for i in range(FFT_KB):
        @pl.when(j * FFT_KB + i < FFT_KA)
        def _():
            xf = _dot(g_ref[i], _slab(f3, i))
            xb = _dot(g_ref[i], _slab(b3, i))
            kf_ref[0, i, 0:FFT_R] = (xf[0:FFT_R] + xb[0:FFT_R]) * inv
            kf_ref[0, i, FFT_R:r2] = (xf[FFT_R:r2] - xb[FFT_R:r2]) * inv

        @pl.when(j * FFT_KB + i >= FFT_KA)
        def _():
            kf_ref[0, i] = jnp.zeros((r2, c), F32)


def _filter_spectrum(s_filt, l1, g2, c):
    n_ord = s_filt.shape[0] // 2
    cols = s_filt.shape[-1]
    r2 = 2 * FFT_R
    return pl.pallas_call(
        _fwd2f_body,
        out_shape=jax.ShapeDtypeStruct((n_ord, FFT_KA_PAD, r2, c), F32),
        grid=(n_ord, FFT_KA_PAD // FFT_KB),
        in_specs=[pl.BlockSpec((1, 2, FFT_KB, cols), lambda o, j: (2 * o, 0, j, 0)),
                  pl.BlockSpec((1, 2, FFT_KB, cols), lambda o, j: (2 * o + 1, 0, j, 0)),
                  pl.BlockSpec((FFT_KB, r2, r2), lambda o, j: (j, 0, 0)),
                  pl.BlockSpec(l1.shape, lambda o, j: (0, 0, 0))],
        out_specs=pl.BlockSpec((1, FFT_KB, r2, c), lambda o, j: (o, j, 0, 0)),
        scratch_shapes=[pltpu.VMEM((2, FFT_KB, FFT_R, c), F32)] * 2,
        compiler_params=_cparams(("parallel", "parallel")),
        name="filter_spectrum",
    )(s_filt, s_filt, g2, l1)


def _mid_body(s_ref, g_ref, gh_ref, kf_ref, t_ref, s3, t3):
    j = pl.program_id(1)
    r2 = 2 * FFT_R
    c = kf_ref.shape[-1]
    _rows_to_slabs(s_ref, s3, c)
    for i in range(FFT_KB):
        @pl.when(j * FFT_KB + i < FFT_KA)
        def _():
            x = _dot(g_ref[i], _slab(s3, i))
            xr, xi = x[0:FFT_R], x[FFT_R:r2]
            kr, ki = kf_ref[0, i, 0:FFT_R], kf_ref[0, i, FFT_R:r2]
            y = jnp.concatenate([xr * kr - xi * ki, xr * ki + xi * kr], axis=0).astype(BF16)
            t = _dot(gh_ref[i], y)
            t3[0, i] = t[0:FFT_R]
            t3[1, i] = t[FFT_R:r2]

        @pl.when(j * FFT_KB + i >= FFT_KA)
        def _():
            t3[0, i] = jnp.zeros((FFT_R, c), F32)
            t3[1, i] = jnp.zeros((FFT_R, c), F32)

    _slabs_to_rows(t3, t_ref, c)


def _conv_mid(s, kf, order, g2, g2h, c):
    n, _, _, cols = s.shape
    r2 = 2 * FFT_R
    blk = pl.BlockSpec((1, 2, FFT_KB, cols), lambda i, j: (i, 0, j, 0))
    gspec = pl.BlockSpec((FFT_KB, r2, r2), lambda i, j: (j, 0, 0))
    return pl.pallas_call(
        _mid_body,
        out_shape=jax.ShapeDtypeStruct(s.shape, F32),
        grid=(n, FFT_KA_PAD // FFT_KB),
        in_specs=[blk, gspec, gspec,
                  pl.BlockSpec((1, FFT_KB, r2, c), lambda i, j: (order, j, 0, 0))],
        out_specs=blk,
        scratch_shapes=[pltpu.VMEM((2, FFT_KB, FFT_R, c), F32)] * 2,
        compiler_params=_cparams(("parallel", "parallel")),
        name="conv_mid",
    )(s, g2, g2h, kf)


def _inv1_body(m_ref, t_ref, u_ref, xg_ref, sk_ref, o_ref):
    c = u_ref.shape[-1]
    t2 = t_ref[0].reshape(2 * FFT_KA_PAD, FFT_NB * c).astype(BF16)
    y = _dot(m_ref[...], t2)
    for bb in range(FFT_NB):
        conv = y[:, bb * c:(bb + 1) * c] + u_ref[0, :, bb, :] * sk_ref[...]
        o_ref[0, :, bb, :] = xg_ref[0, :, bb, :] * conv


def _conv_inv1(t, u, xg, skip, m_inv):
    n, seq, c = u.shape
    a_half = FFT_A // 2
    sk = skip.astype(F32).reshape(1, c)
    uspec = pl.BlockSpec((1, a_half, FFT_NB, c), lambda i, j: (i, 0, j, 0))
    view = lambda a: a.reshape(n, a_half, FFT_R, c)
    out = pl.pallas_call(
        _inv1_body,
        out_shape=jax.ShapeDtypeStruct((n, a_half, FFT_R, c), F32),
        grid=(n, FFT_R // FFT_NB),
        in_specs=[pl.BlockSpec(m_inv.shape, lambda i, j: (0, 0)),
                  pl.BlockSpec((1, 2, FFT_KA_PAD, FFT_NB * c), lambda i, j: (i, 0, 0, j)),
                  uspec, uspec,
                  pl.BlockSpec((1, c), lambda i, j: (0, 0))],
        out_specs=uspec,
        compiler_params=_cparams(("parallel", "parallel")),
        name="conv_inv1",
    )(m_inv, t, view(u), view(xg), sk)
    return out.reshape(n, seq, c)


def _hyena(v, x1, x2, f_w1, f_b1, f_w2, f_b2, f_w3, f_freq, skip):
    _, seq, c = v.shape
    assert 2 * seq == FFT_A * FFT_R
    m_fwd, m_inv, g2, g2h = _conv_dft_constants()
    taps, l1 = _hyena_filter_taps(seq, f_w1, f_b1, f_w2, f_b2, f_w3, f_freq, c)
    kf = _filter_spectrum(_conv_fwd1(taps, m_fwd), l1, g2, c)
    y = v
    for order, xg in enumerate((x1, x2)):
        t = _conv_mid(_conv_fwd1(y, m_fwd), kf, order, g2, g2h, c)
        y = _conv_inv1(t, y, xg, skip[order], m_inv)
    return y


def _mix_ffn_body(x_ref, a1_ref, a2_ref, wm_ref, gm_ref, g_ref, sh_ref, sc_ref, gt_ref,
                  w1_ref, w3_ref, w2_ref, o_ref, *, fb):
    c1 = a1_ref.shape[-1]
    mixed = _dot(a1_ref[0].astype(BF16), wm_ref[0:c1]) + _dot(a2_ref[0].astype(BF16), wm_ref[c1:])
    xm = x_ref[0] + gm_ref[0] * mixed
    h = _norm_mod(xm, g_ref[...], sh_ref[0], sc_ref[0]).astype(BF16)
    acc = None
    for lo in range(0, w1_ref.shape[1], fb):
        a = _dot(h, w1_ref[:, lo:lo + fb])
        u = (a * jax.nn.sigmoid(a) * _dot(h, w3_ref[:, lo:lo + fb])).astype(BF16)
        part = _dot(u, w2_ref[lo:lo + fb, :])
        acc = part if acc is None else acc + part
    o_ref[0] = xm + gt_ref[0] * acc


def _mix_ffn(x, a1, a2, w_mix, gate_mix, g, shift, scale, gate, w1, w3, w2, tm=512, fb=1408):
    b, s, d = x.shape
    tok = lambda c: pl.BlockSpec((1, tm, c), lambda bi, i: (bi, i, 0))
    per_b = pl.BlockSpec((1, 1, d), lambda bi, i: (bi, 0, 0))
    const = lambda a: pl.BlockSpec(a.shape, lambda bi, i: (0, 0))
    resident = lambda a: pl.BlockSpec(a.shape, lambda bi, i: (0, 0), pipeline_mode=pl.Buffered(1))
    return pl.pallas_call(
        functools.partial(_mix_ffn_body, fb=fb),
        out_shape=jax.ShapeDtypeStruct(x.shape, F32),
        grid=(b, s // tm),
        in_specs=[tok(d), tok(a1.shape[-1]), tok(a2.shape[-1]), resident(w_mix), per_b,
                  const(g), per_b, per_b, per_b, resident(w1), resident(w3), resident(w2)],
        out_specs=tok(d),
        compiler_params=_cparams(("parallel", "parallel"), vmem_limit=FFN_VMEM_LIMIT),
        name="mix_ffn",
    )(x, a1, a2, w_mix, gate_mix, g, shift, scale, gate, w1, w3, w2)


def _fm_constants(cg):
    j = np.arange(cg)[:, None]
    m = np.arange(cg)[None, :]
    ph = 2.0 * np.pi * j * m / cg
    w_cs = np.concatenate([np.cos(ph), np.sin(ph)], axis=1)
    d = np.arange(FM_A)[:, None]
    a = np.arange(FM_A)[None, :]
    ph = 2.0 * np.pi * d * a / FM_A
    fr, fi = np.cos(ph), -np.sin(ph)
    m1 = np.block([[fr, fi], [fi, -fr]])
    n = FM_A * FM_A
    dd = np.arange(FM_A)[:, None, None]
    c = np.arange(FM_A)[None, :, None]
    b = np.arange(FM_A)[None, None, :]
    th = 2.0 * np.pi * (b * c / FM_A + b * dd / n)
    gcat = np.concatenate([np.cos(th), np.sin(th)], axis=2)
    return _mxu_const(w_cs), _mxu_const(m1), _mxu_const(gcat)


def _fm_front_body(x_ref, g_ref, sh_ref, sc_ref, w_ref, m_ref, o_ref, *, cg, nb):
    d = x_ref.shape[-1]
    xs = jnp.concatenate([x_ref[0, :, bb, :] for bb in range(nb)], axis=0)
    h = _norm_mod(xs, g_ref[...], sh_ref[0], sc_ref[0]).astype(BF16)
    pq = [_dot(h[:, grp * cg:(grp + 1) * cg], w_ref[...]) for grp in range(d // cg)]
    p = jnp.concatenate([t[:, 0:cg] for t in pq], axis=1)
    q = jnp.concatenate([t[:, cg:2 * cg] for t in pq], axis=1)
    for bb in range(nb):
        rows = slice(bb * FM_A, (bb + 1) * FM_A)
        res = _dot(m_ref[...], jnp.concatenate([p[rows], q[rows]], axis=0).astype(BF16))
        o_ref[0, 0, :, bb, :] = res[0:FM_A]
        o_ref[0, 1, :, bb, :] = res[FM_A:2 * FM_A]


def _fm_front(x, g, shift, scale, w_cs, m1, nb=8):
    b, s, d = x.shape
    cg = w_cs.shape[0]
    per_b = pl.BlockSpec((1, 1, d), lambda bi, j: (bi, 0, 0))
    const = lambda a: pl.BlockSpec(a.shape, lambda bi, j: (0, 0))
    return pl.pallas_call(
        functools.partial(_fm_front_body, cg=cg, nb=nb),
        out_shape=jax.ShapeDtypeStruct((b, 2, FM_A, s // FM_A, d), F32),
        grid=(b, s // FM_A // nb),
        in_specs=[pl.BlockSpec((1, FM_A, nb, d), lambda bi, j: (bi, 0, j, 0)),
                  const(g), per_b, per_b, const(w_cs), const(m1)],
        out_specs=pl.BlockSpec((1, 2, FM_A, nb, d), lambda bi, j: (bi, 0, 0, j, 0)),
        compiler_params=_cparams(("parallel", "parallel")),
        name="fm_front",
    )(x.reshape(b, FM_A, s // FM_A, d), g, shift, scale, w_cs, m1)


def _fm_s2_body(s_ref, g_ref, o_ref, *, dblk, scale):
    for i in range(dblk):
        s2 = jnp.concatenate([s_ref[0, 0, i], s_ref[0, 1, i]], axis=0).astype(BF16)
        o_ref[:, i, :] = _dot(g_ref[i], s2) * scale


def _fm_stage2(sv, gcat, seq, d, dblk=8):
    b = sv.shape[0]
    scale = 1.0 / math.sqrt(seq * (d // F_GROUPS))
    out = pl.pallas_call(
        functools.partial(_fm_s2_body, dblk=dblk, scale=scale),
        out_shape=jax.ShapeDtypeStruct((b * FM_A, dblk * (FM_A // dblk), d), F32),
        grid=(b, FM_A // dblk),
        in_specs=[pl.BlockSpec((1, 2, dblk, FM_A, d), lambda bi, j: (bi, 0, j, 0, 0)),
                  pl.BlockSpec((dblk, FM_A, 2 * FM_A), lambda bi, j: (j, 0, 0))],
        out_specs=pl.BlockSpec((FM_A, dblk, d), lambda bi, j: (bi, j, 0)),
        compiler_params=_cparams(("parallel", "parallel")),
        name="fm_stage2",
    )(sv, gcat)
    return out.reshape(b, seq, d)


def _fourier_mix(x, g, shift, scale):
    b, s, d = x.shape
    assert s == FM_A * FM_A
    w_cs, m1, gcat = _fm_constants(d // F_GROUPS)
    return _fm_stage2(_fm_front(x, g, shift, scale, w_cs, m1), gcat, s, d)


LANES = 128
ROW_SL = 8
MOE_TM = 1024
DMA_WINDOW = 128


def _router_body(x_ref, yf_ref, wf_ref, gf_ref, g_ref, sh_ref, sc_ref, wr_ref, br_ref,
                 xo_ref, h_ref, meta_ref, gw_ref, cnt_ref, carry):
    i = pl.program_id(0)

    @pl.when(i == 0)
    def _():
        carry[...] = jnp.zeros_like(carry)

    xm = x_ref[...] + gf_ref[0] * _dot(yf_ref[...].astype(BF16), wf_ref[...])
    xo_ref[...] = xm
    h = _norm_mod(xm, g_ref[...], sh_ref[0], sc_ref[0])
    _rows_to_tiles(h_ref, h)
    h_hi = h.astype(BF16)
    h_lo = (h - h_hi.astype(F32)).astype(BF16)
    by_hi = _dot(h_hi, wr_ref[...])
    logits = by_hi[:, 0:LANES] + by_hi[:, LANES:] + _dot(h_lo, wr_ref[:, 0:LANES]) + br_ref[...]
    lane = lax.broadcasted_iota(jnp.int32, logits.shape, 1)
    nl = logits.shape[-1]
    m1 = jnp.max(logits, axis=-1, keepdims=True)
    i1 = jnp.min(jnp.where(logits == m1, lane, nl), axis=-1, keepdims=True)
    rest = jnp.where(lane == i1, -3.0e38, logits)
    m2 = jnp.max(rest, axis=-1, keepdims=True)
    i2 = jnp.min(jnp.where(rest == m2, lane, nl), axis=-1, keepdims=True)
    e = jnp.exp(m2 - m1)
    gw_ref[...] = jnp.where(lane == 0, 1.0 / (1.0 + e), jnp.where(lane == 1, e / (1.0 + e), 0.0))
    onehot = jnp.where((lane == i1) | (lane == i2), 1.0, 0.0)
    tm = onehot.shape[0]
    earlier = lax.broadcasted_iota(jnp.int32, (tm, tm), 0) > lax.broadcasted_iota(jnp.int32, (tm, tm), 1)
    excl = _dot(jnp.where(earlier, 1.0, 0.0).astype(BF16), onehot.astype(BF16)) + carry[...]
    r1 = jnp.sum(jnp.where(lane == i1, excl, 0.0), axis=-1, keepdims=True).astype(jnp.int32)
    r2 = jnp.sum(jnp.where(lane == i2, excl, 0.0), axis=-1, keepdims=True).astype(jnp.int32)
    meta_ref[...] = jnp.where(lane == 0, i1, jnp.where(lane == 1, i2, jnp.where(lane == 2, r1, jnp.where(lane == 3, r2, 0))))
    carry[...] = carry[...] + jnp.sum(onehot, axis=0, keepdims=True)
    cnt_ref[...] = carry[...]


def _router(x, y_f, w_f, gate_f, g, shift, scale, w_router, b_router, tm=512):
    b, s, d = x.shape
    t = b * s
    ne = w_router.shape[1]
    wr = jnp.pad(w_router.astype(F32), ((0, 0), (0, LANES - ne)))
    wr_hi = wr.astype(BF16)
    wr = jnp.concatenate([wr_hi, (wr - wr_hi.astype(F32)).astype(BF16)], axis=1)
    br = jnp.pad(b_router.astype(F32).reshape(1, ne), ((0, 0), (0, LANES - ne)), constant_values=NEG_INF)
    spt = s // tm
    per_b = pl.BlockSpec((1, 1, d), lambda i: (i // spt, 0, 0))
    const = lambda a: pl.BlockSpec(a.shape, lambda i: (0, 0))
    tok = pl.BlockSpec((tm, d), lambda i: (i, 0))
    return pl.pallas_call(
        _router_body,
        out_shape=[jax.ShapeDtypeStruct((t, d), F32),
                   jax.ShapeDtypeStruct((t * ROW_SL, LANES), F32),
                   jax.ShapeDtypeStruct((t, LANES), jnp.int32),
                   jax.ShapeDtypeStruct((t, LANES), F32),
                   jax.ShapeDtypeStruct((1, LANES), F32)],
        grid=(t // tm,),
        in_specs=[tok, tok, const(w_f), per_b, const(g), per_b, per_b, const(wr), const(br)],
        out_specs=[tok,
                   pl.BlockSpec((tm * ROW_SL, LANES), lambda i: (i, 0)),
                   pl.BlockSpec((tm, LANES), lambda i: (i, 0)),
                   pl.BlockSpec((tm, LANES), lambda i: (i, 0)),
                   pl.BlockSpec((1, LANES), lambda i: (0, 0))],
        scratch_shapes=[pltpu.VMEM((1, LANES), F32)],
        compiler_params=_cparams(("arbitrary",)),
        name="router",
    )(x.reshape(t, d), y_f.reshape(t, d), w_f, gate_f, g, shift, scale, wr, br)


def _moe_plan(meta, counts, ne, tm):
    i1, i2, r1, r2 = meta[:, 0], meta[:, 1], meta[:, 2], meta[:, 3]
    cnt = counts[0, :ne].astype(jnp.int32)
    padded = ((cnt + tm - 1) // tm) * tm
    ends = jnp.cumsum(padded)
    offs = ends - padded
    pick = lambda idx: sum(jnp.where(idx == e, offs[e], 0) for e in range(ne))
    pos = jnp.concatenate([pick(i1) + r1, pick(i2) + r2]).astype(jnp.int32)
    n_tiles = (2 * meta.shape[0]) // tm + ne
    n_used = (ends[ne - 1] // tm).astype(jnp.int32)
    tile_start = jnp.minimum(jnp.arange(n_tiles, dtype=jnp.int32), n_used - 1) * tm
    tile_expert = jnp.sum(tile_start[:, None] >= ends[None, :], axis=1).astype(jnp.int32)
    group_end = sum(jnp.where(tile_expert == e, offs[e] + cnt[e], 0) for e in range(ne))
    tile_rows = jnp.clip(group_end - tile_start, 0, tm).astype(jnp.int32)
    return pos, offs + cnt, padded - cnt, tile_expert, n_used.reshape(1), tile_rows


def _windowed_copies(n, start_copy, wait_one, per_iter):
    def body(i, carry):
        @pl.when(i >= DMA_WINDOW)
        def _():
            for _ in range(per_iter):
                wait_one()
        start_copy(i)
        return carry

    lax.fori_loop(0, n, body, 0)

    def drain(i, carry):
        for _ in range(per_iter):
            wait_one()
        return carry

    lax.fori_loop(0, jnp.minimum(n, DMA_WINDOW), drain, 0)


def _tile_of(ref, row):
    return ref.at[pl.ds(pl.multiple_of(row * ROW_SL, ROW_SL), ROW_SL)]


def _tiles_to_rows(ref, n, first=0):
    return jnp.concatenate([ref[pl.ds(first * ROW_SL + sl, n, stride=ROW_SL), :] for sl in range(ROW_SL)], axis=1)


def _rows_to_tiles(ref, val):
    n = val.shape[0]
    for sl in range(ROW_SL):
        ref[pl.ds(sl, n, stride=ROW_SL), :] = val[:, sl * LANES:(sl + 1) * LANES]


def _dispatch_body(pos_ref, pad_start_ref, pad_n_ref, h_ref, xs_hbm, sem, *, n_tok, ne):
    i = pl.program_id(0)
    td = h_ref.shape[0] // ROW_SL
    base = i * td
    copy = lambda src, dst: pltpu.make_async_copy(_tile_of(h_ref, src), _tile_of(xs_hbm, dst), sem)
    wait_one = lambda: copy(0, 0).wait()

    def start_token(r, carry):
        copy(r, pos_ref[base + r]).start(priority=0)
        copy(r, pos_ref[n_tok + base + r]).start(priority=1)
        return carry

    lax.fori_loop(0, td, start_token, 0, unroll=8)
    whole_tile = pltpu.make_async_copy(h_ref, xs_hbm.at[pl.ds(0, td * ROW_SL)], sem)
    whole_tile.wait()
    whole_tile.wait()

    @pl.when(i == 0)
    def _():
        for e in range(ne):
            first = pad_start_ref[e]
            _windowed_copies(pad_n_ref[e], lambda r: copy(0, first + r).start(), wait_one, 1)


def _moe_dispatch(h3, pos, pad_start, pad_n, n_rows, td=1024):
    n_tok = h3.shape[0] // ROW_SL
    ne = pad_start.shape[0]
    return pl.pallas_call(
        functools.partial(_dispatch_body, n_tok=n_tok, ne=ne),
        out_shape=jax.ShapeDtypeStruct((n_rows * ROW_SL, LANES), h3.dtype),
        grid_spec=pltpu.PrefetchScalarGridSpec(
            num_scalar_prefetch=3, grid=(n_tok // td,),
            in_specs=[pl.BlockSpec((td * ROW_SL, LANES), lambda i, p, ps, pn: (i, 0))],
            out_specs=pl.BlockSpec(memory_space=pl.ANY),
            scratch_shapes=[pltpu.SemaphoreType.DMA(())]),
        compiler_params=_cparams(("arbitrary",)),
        name="moe_dispatch",
    )(pos, pad_start, pad_n, h3)


def _moe_grouped_body(te_ref, nu_ref, tr_ref, xs_ref, w1_ref, w3_ref, w2_ref, y_ref, xb_scr, acc_scr):
    i = pl.program_id(0)
    j = pl.program_id(1)
    tm = xb_scr.shape[0]
    hm = tm // 2

    def expert_rows(nrows):
        h = xb_scr[0:nrows]
        a = _dot(h, w1_ref[0].astype(BF16))
        u = (a * jax.nn.sigmoid(a) * _dot(h, w3_ref[0].astype(BF16))).astype(BF16)
        part = _dot(u, w2_ref[0].astype(BF16))

        @pl.when(j == 0)
        def _():
            acc_scr[0:nrows] = part

        @pl.when(j > 0)
        def _():
            acc_scr[0:nrows] += part

    @pl.when(i < nu_ref[0])
    def _():
        @pl.when(j == 0)
        def _():
            xb_scr[...] = _tiles_to_rows(xs_ref, tm).astype(BF16)

        @pl.when(tr_ref[i] > hm)
        def _():
            expert_rows(tm)

        @pl.when(tr_ref[i] <= hm)
        def _():
            expert_rows(hm)

            @pl.when(j == 0)
            def _():
                acc_scr[hm:tm] = jnp.zeros((tm - hm, acc_scr.shape[1]), F32)

        @pl.when(j == pl.num_programs(1) - 1)
        def _():
            _rows_to_tiles(y_ref, acc_scr[...])


def _moe_grouped(xs, tile_expert, n_used, tile_rows, w1, w3, w2, tm, fb=512):
    ne, d, f = w1.shape
    n_rows = xs.shape[0] // ROW_SL
    nj = f // fb
    row_tile = lambda i, j, te, nu, tr: (jnp.maximum(jnp.minimum(i, nu[0] - 1), 0), 0)
    jj = lambda i, j, nu: jnp.where(i < nu[0], j, nj - 1)
    return pl.pallas_call(
        _moe_grouped_body,
        out_shape=jax.ShapeDtypeStruct(xs.shape, F32),
        grid_spec=pltpu.PrefetchScalarGridSpec(
            num_scalar_prefetch=3, grid=(n_rows // tm, nj),
            in_specs=[pl.BlockSpec((tm * ROW_SL, LANES), row_tile),
                      pl.BlockSpec((1, d, fb), lambda i, j, te, nu, tr: (te[i], 0, jj(i, j, nu))),
                      pl.BlockSpec((1, d, fb), lambda i, j, te, nu, tr: (te[i], 0, jj(i, j, nu))),
                      pl.BlockSpec((1, fb, d), lambda i, j, te, nu, tr: (te[i], jj(i, j, nu), 0))],
            out_specs=pl.BlockSpec((tm * ROW_SL, LANES), row_tile),
            scratch_shapes=[pltpu.VMEM((tm, d), BF16), pltpu.VMEM((tm, d), F32)]),
        compiler_params=_cparams(("arbitrary", "arbitrary")),
        name="moe_grouped",
    )(tile_expert, n_used, tile_rows, xs, w1, w3, w2)


def _moe_final_body(pos_ref, x_ref, y_hbm, gw_ref, gt_ref, fg_ref, o_ref, yg_scr, sem, *, n_tok):
    i = pl.program_id(0)
    tc = x_ref.shape[0]
    slot = i % 2

    def gather_tile(step, into):
        base = step * tc

        def start_token(r, carry):
            dst = yg_scr.at[into]
            pltpu.make_async_copy(_tile_of(y_hbm, pos_ref[base + r]), _tile_of(dst, r),
                                  sem.at[into]).start(priority=0)
            pltpu.make_async_copy(_tile_of(y_hbm, pos_ref[n_tok + base + r]), _tile_of(dst, tc + r),
                                  sem.at[into]).start(priority=1)
            return carry

        lax.fori_loop(0, tc, start_token, 0, unroll=8)

    @pl.when(i == 0)
    def _():
        gather_tile(0, 0)

    @pl.when(i + 1 < pl.num_programs(0))
    def _():
        gather_tile(i + 1, 1 - slot)

    pltpu.make_async_copy(y_hbm.at[pl.ds(0, 2 * tc * ROW_SL)], yg_scr.at[slot], sem.at[slot]).wait()
    gw = gw_ref[...]
    rows = yg_scr.at[slot]
    y = gw[:, 0:1] * _tiles_to_rows(rows, tc) + gw[:, 1:2] * _tiles_to_rows(rows, tc, first=tc)
    xo = x_ref[...] + gt_ref[0] * y
    ms = jnp.mean(xo * xo, axis=-1, keepdims=True)
    o_ref[...] = xo * lax.rsqrt(ms + EPS) * fg_ref[...]


def _moe_final(x, y, pos, gw, gt, final_g, tc=512):
    b, s, d = x.shape
    t = b * s
    spt = s // tc
    nsl = d // LANES
    out = pl.pallas_call(
        functools.partial(_moe_final_body, n_tok=t),
        out_shape=jax.ShapeDtypeStruct((t, d), F32),
        grid_spec=pltpu.PrefetchScalarGridSpec(
            num_scalar_prefetch=1, grid=(t // tc,),
            in_specs=[pl.BlockSpec((tc, d), lambda i, p: (i, 0)),
                      pl.BlockSpec(memory_space=pl.ANY),
                      pl.BlockSpec((tc, LANES), lambda i, p: (i, 0)),
                      pl.BlockSpec((1, 1, d), lambda i, p: (i // spt, 0, 0)),
                      pl.BlockSpec(final_g.shape, lambda i, p: (0, 0))],
            out_specs=pl.BlockSpec((tc, d), lambda i, p: (i, 0)),
            scratch_shapes=[pltpu.VMEM((2, 2 * tc * ROW_SL, LANES), F32), pltpu.SemaphoreType.DMA((2,))]),
        compiler_params=_cparams(("arbitrary",)),
        name="moe_final",
    )(pos, x.reshape(t, d), y, gw, gt, final_g)
    return out.reshape(b, s, d)


def _moe_routed(x, y_f, w_f, gate_f, g, shift, scale, gt, final_g, w_router, b_router, w1, w3, w2):
    ne = w1.shape[0]
    tm = MOE_TM
    x1, h3, meta, gw, counts = _router(x, y_f, w_f, gate_f, g, shift, scale, w_router, b_router)
    pos, pad_start, pad_n, tile_expert, n_used, tile_rows = _moe_plan(meta, counts, ne, tm)
    assert x.shape[-1] == ROW_SL * LANES
    n_rows = (2 * (h3.shape[0] // ROW_SL) // tm + ne) * tm
    xs = _moe_dispatch(h3, pos, pad_start, pad_n, n_rows)
    y = _moe_grouped(xs, tile_expert, n_used, tile_rows, w1, w3, w2, tm)
    return _moe_final(x1.reshape(x.shape), y, pos, gw, gt, final_g)


def kernel(x, c, ctx, c_ctx, w_ada, b_ada, norm_g, w_in, hy_short_w, hy_short_b, hy_f_w1, hy_f_b1, hy_f_w2, hy_f_b2, hy_f_w3, hy_f_freq, hy_skip, na_rpb, w_mix_out, ffn_w1, ffn_w3, ffn_w2, w_fourier, w_router, b_router, moe_w1, moe_w3, moe_w2, final_g):
    b, s, d = x.shape
    depth = w_ada.shape[0]
    assert depth == 2, "layer 0 mixes with Hyena/attention, layer 1 with Fourier/MoE"
    c_hy = hy_skip.shape[-1]
    c_na = d - c_hy

    cvec = jnp.concatenate([c, c_ctx[None, :], jnp.zeros((8 - b - 1, d), F32)], axis=0)
    mods = _ada(cvec, w_ada, b_ada)

    def mod(layer, idx, ctx_row=False):
        m = mods[layer, :, idx * d:(idx + 1) * d]
        return m[b:b + 1, None, :] if ctx_row else m[0:b, None, :]

    row = lambda a: a.reshape(1, -1)

    w_in0 = w_in[0].astype(BF16)
    w_hy, w_qkv = w_in0[:, 0:3 * c_hy], w_in0[:, 3 * c_hy:]
    v, x1, x2, q, k, va = _inproj(x, row(norm_g[0, 0]), mod(0, 0), mod(0, 1), w_hy, w_qkv,
                                  hy_short_w[0], row(hy_short_b[0]))
    kc, vc = _ctxkv(ctx, row(norm_g[0, 0]), mod(0, 0, True), mod(0, 1, True), w_qkv[:, c_na:])
    y_na = _natt(q, k, va, kc, vc, _na_bias_table(na_rpb[0]))
    y_hy = _hyena(v, x1, x2, hy_f_w1[0], hy_f_b1[0], hy_f_w2[0], hy_f_b2[0], hy_f_w3[0],
                  hy_f_freq[0], hy_skip[0])
    x = _mix_ffn(x, y_hy, y_na, w_mix_out[0].astype(BF16), mod(0, 2),
                 row(norm_g[0, 1]), mod(0, 3), mod(0, 4), mod(0, 5),
                 ffn_w1[0].astype(BF16), ffn_w3[0].astype(BF16), ffn_w2[0].astype(BF16))

    y_f = _fourier_mix(x, row(norm_g[1, 0]), mod(1, 0), mod(1, 1))
    return _moe_routed(x, y_f, w_fourier[0].astype(BF16), mod(1, 2),
                       row(norm_g[1, 1]), mod(1, 3), mod(1, 4), mod(1, 5), row(final_g),
                       w_router[0], b_router[0],
                       moe_w1[0], moe_w3[0], moe_w2[0])
```

```python
import functools
import math

import numpy as np
import jax
import jax.numpy as jnp
from jax import lax
from jax.experimental import pallas as pl
from jax.experimental.pallas import tpu as pltpu

F32 = jnp.float32
BF16 = jnp.bfloat16
HIGHEST = lax.Precision.HIGHEST

GRID_W = 64
NA_HEAD_DIM = 32
NA_WIN_R = 8
NA_WIN_C = 16
HYENA_EMB = 33
HYENA_BANDS = (HYENA_EMB - 1) // 2
HYENA_FAST_DECAY = 0.3
HYENA_SLOW_DECAY = 1.5
HYENA_TARGET = 1e-2
F_GROUPS = 4
N_MOD = 6
EPS = 1e-6
NEG_INF = -1e30

FFT_A = 64
FFT_R = 128
FFT_KA = FFT_A // 2 + 1
FFT_KA_PAD = 40
FM_A = 64

VMEM_LIMIT = 48 * 1024 * 1024
FFN_VMEM_LIMIT = 56 * 1024 * 1024


def _cparams(sem, vmem_limit=VMEM_LIMIT):
    return pltpu.CompilerParams(dimension_semantics=sem, vmem_limit_bytes=vmem_limit)


def _dot(a, b):
    return jnp.dot(a, b, preferred_element_type=F32)


def _mxu_const(m):
    return jnp.asarray(m, dtype=F32).astype(BF16)


def _norm_mod(x, g, shift, scale):
    ms = jnp.mean(x * x, axis=-1, keepdims=True)
    y = x * lax.rsqrt(ms + EPS) * g
    return y * (1.0 + scale) + shift


def _ada_body(c_ref, w_ref, b_ref, o_ref):
    cv = c_ref[...]
    s = cv * jax.nn.sigmoid(cv)
    o_ref[0] = jnp.dot(s, w_ref[0], precision=HIGHEST, preferred_element_type=F32) + b_ref[0]


def _ada(cvec, w_ada, b_ada):
    depth, d, n = w_ada.shape
    rows = cvec.shape[0]
    bn = n // 4
    return pl.pallas_call(
        _ada_body,
        out_shape=jax.ShapeDtypeStruct((depth, rows, n), F32),
        grid=(depth, n // bn),
        in_specs=[pl.BlockSpec((rows, d), lambda l, j: (0, 0)),
                  pl.BlockSpec((1, d, bn), lambda l, j: (l, 0, j)),
                  pl.BlockSpec((1, 1, bn), lambda l, j: (l, 0, j))],
        out_specs=pl.BlockSpec((1, rows, bn), lambda l, j: (l, 0, j)),
        compiler_params=_cparams(("parallel", "parallel")),
        name="ada",
    )(cvec, w_ada, b_ada.reshape(depth, 1, n))


def _inproj_body(x_ref, xp_ref, xn_ref, g_ref, sh_ref, sc_ref, why_ref, wqkv_ref, sw_ref, sb_ref,
                 v_ref, x1_ref, x2_ref, q_ref, k_ref, va_ref, *, n_tiles, q_scale, c_hy, c_na):
    i = pl.program_id(1)
    g, sh, sc = g_ref[...], sh_ref[0], sc_ref[0]
    hf = _norm_mod(x_ref[0], g, sh, sc)
    h = hf.astype(BF16)
    tm = hf.shape[0]
    hx = jnp.concatenate([_norm_mod(xp_ref[0], g, sh, sc), hf, _norm_mod(xn_ref[0], g, sh, sc)],
                         axis=0).astype(BF16)
    row = lax.broadcasted_iota(jnp.int32, (tm, c_hy), 0)
    sw = sw_ref[...]
    sb = sb_ref[...]
    for ci, out_ref in enumerate((v_ref, x1_ref, x2_ref)):
        cols = slice(ci * c_hy, (ci + 1) * c_hy)
        zx = _dot(hx, why_ref[:, cols])
        zh = zx[8:8 + tm]
        zp = jnp.where(i > 0, zx[7:8], 0.0)
        zn = jnp.where(i < n_tiles - 1, zx[8 + tm:9 + tm], 0.0)
        z_m1 = jnp.where(row == 0, zp, pltpu.roll(zh, 1, 0))
        z_p1 = jnp.where(row == tm - 1, zn, pltpu.roll(zh, tm - 1, 0))
        out_ref[0] = z_m1 * sw[0:1, cols] + zh * sw[1:2, cols] + z_p1 * sw[2:3, cols] + sb[:, cols]
    for ci, (out_ref, mult) in enumerate(((q_ref, q_scale), (k_ref, None), (va_ref, None))):
        z = _dot(h, wqkv_ref[:, ci * c_na:(ci + 1) * c_na])
        out_ref[0] = (z if mult is None else z * mult).astype(BF16)


def _inproj(x, g, shift, scale, w_hy, w_qkv, short_w, short_b, tm=1024):
    b, s, d = x.shape
    c_hy = w_hy.shape[1] // 3
    c_na = w_qkv.shape[1] // 3
    n_tiles = s // tm
    r8 = tm // 8
    body = functools.partial(_inproj_body, n_tiles=n_tiles, q_scale=NA_HEAD_DIM ** -0.5,
                             c_hy=c_hy, c_na=c_na)
    tok = lambda c: pl.BlockSpec((1, tm, c), lambda bi, i: (bi, i, 0))
    full2 = lambda a: pl.BlockSpec(a.shape, lambda bi, i: (0, 0))
    resident = lambda a: pl.BlockSpec(a.shape, lambda bi, i: (0, 0), pipeline_mode=pl.Buffered(1))
    per_b = pl.BlockSpec((1, 1, d), lambda bi, i: (bi, 0, 0))
    return pl.pallas_call(
        body,
        out_shape=[jax.ShapeDtypeStruct((b, s, c_hy), F32)] * 3 + [jax.ShapeDtypeStruct((b, s, c_na), BF16)] * 3,
        grid=(b, n_tiles),
        in_specs=[tok(d),
                  pl.BlockSpec((1, 8, d), lambda bi, i: (bi, jnp.maximum(i * r8 - 1, 0), 0)),
                  pl.BlockSpec((1, 8, d), lambda bi, i: (bi, jnp.minimum((i + 1) * r8, s // 8 - 1), 0)),
                  full2(g), per_b, per_b, resident(w_hy), resident(w_qkv), full2(short_w), full2(short_b)],
        out_specs=[tok(c_hy)] * 3 + [tok(c_na)] * 3,
        compiler_params=_cparams(("parallel", "parallel"), vmem_limit=FFN_VMEM_LIMIT),
        name="inproj",
    )(x, x, x, g, shift, scale, w_hy, w_qkv, short_w, short_b)


def _ctxkv_body(x_ref, g_ref, sh_ref, sc_ref, w_ref, k_ref, v_ref, *, c_na):
    h = _norm_mod(x_ref[0], g_ref[...], sh_ref[0], sc_ref[0]).astype(BF16)
    z = _dot(h, w_ref[...])
    k_ref[0] = z[:, 0:c_na].astype(BF16)
    v_ref[0] = z[:, c_na:2 * c_na].astype(BF16)


def _ctxkv(ctx, g, shift, scale, w_kv):
    b, n, d = ctx.shape
    c_na = w_kv.shape[1] // 2
    one = pl.BlockSpec((1, 1, d), lambda bi: (0, 0, 0))
    return pl.pallas_call(
        functools.partial(_ctxkv_body, c_na=c_na),
        out_shape=[jax.ShapeDtypeStruct((b, n, c_na), BF16)] * 2,
        grid=(b,),
        in_specs=[pl.BlockSpec((1, n, d), lambda bi: (bi, 0, 0)),
                  pl.BlockSpec(g.shape, lambda bi: (0, 0)), one, one,
                  pl.BlockSpec(w_kv.shape, lambda bi: (0, 0))],
        out_specs=[pl.BlockSpec((1, n, c_na), lambda bi: (bi, 0, 0))] * 2,
        compiler_params=_cparams(("parallel",)),
        name="ctxkv",
    )(ctx, g, shift, scale, w_kv)


NA_HEADS_PER_BLK = 8
NA_ROWS_PER_STEP = 8


def _na_bias_body(r_ref, e_ref, ok_ref, o_ref):
    t = jnp.dot(r_ref[...], e_ref[...], precision=HIGHEST, preferred_element_type=F32)
    o_ref[...] = jnp.where(ok_ref[...] > 0.5, t, NEG_INF)


def _na_bias_table(rpb):
    w = GRID_W
    h, nr, nc = rpb.shape
    col = np.arange(w)[:, None]
    kc = np.arange(w)[None, :]
    c_start = np.clip(col - NA_WIN_C // 2, 0, w - NA_WIN_C)
    valid = ((kc >= c_start) & (kc < c_start + NA_WIN_C)).reshape(1, w * w)
    expand = (np.arange(32)[:, None, None] == (kc - col + NA_WIN_C - 1)[None]).reshape(32, w * w)
    rp = jnp.pad(rpb.astype(F32).reshape(h * nr, nc), ((0, 0), (0, 32 - nc)))
    full = lambda a: pl.BlockSpec(a.shape, lambda: (0,) * a.ndim)
    expand = jnp.asarray(expand, dtype=F32)
    ok = jnp.asarray(valid, dtype=F32)
    toep = pl.pallas_call(
        _na_bias_body,
        out_shape=jax.ShapeDtypeStruct((h * nr, w * w), F32),
        in_specs=[full(rp), full(expand), full(ok)],
        out_specs=pl.BlockSpec((h * nr, w * w), lambda: (0, 0)),
        name="na_bias",
    )(rp, expand, ok)
    t2 = toep.reshape(h, nr, w, w).transpose(0, 2, 1, 3).reshape(h, w, nr * w)
    slabs = jnp.stack([t2[:, :, (NA_WIN_R - 1 - off) * w:(2 * NA_WIN_R - 1 - off) * w]
                       for off in range(NA_WIN_R)], axis=1)
    hpb = NA_HEADS_PER_BLK
    slabs = slabs.reshape(h // hpb, hpb, NA_WIN_R, w, NA_WIN_R * w).transpose(0, 2, 1, 3, 4)
    return slabs.reshape(h // hpb, NA_WIN_R, hpb * w, NA_WIN_R * w)


def _natt_body(q_ref, k_ref, v_ref, kc_ref, vc_ref, bias_ref, o_ref, *, rows):
    w = GRID_W
    hpb = NA_HEADS_PER_BLK
    nloc = NA_WIN_R * w
    lane = lax.broadcasted_iota(jnp.int32, (1, hpb * NA_HEAD_DIM), 1)
    in_head = [(lane >= NA_HEAD_DIM * hh) & (lane < NA_HEAD_DIM * (hh + 1)) for hh in range(hpb)]
    kcx = kc_ref[0]
    vcx = vc_ref[0]
    nt = (((1,), (1,)), ((), ()))

    def one_row(r):
        r0 = jnp.clip(r - NA_WIN_R // 2, 0, rows - NA_WIN_R)
        off = r - r0
        qs = q_ref[0, pl.ds(pl.multiple_of(r * w, w), w), :]
        kw = k_ref[0, pl.ds(pl.multiple_of(r0 * w, w), nloc), :]
        vw = v_ref[0, pl.ds(pl.multiple_of(r0 * w, w), nloc), :]
        zero = jnp.zeros_like(qs)
        qst = jnp.concatenate([jnp.where(m, qs, zero) for m in in_head], axis=0)
        s_loc = lax.dot_general(qst, kw, nt, preferred_element_type=F32) + bias_ref[0, off]
        s_ctx = lax.dot_general(qst, kcx, nt, preferred_element_type=F32)
        m = jnp.maximum(jnp.max(s_loc, axis=-1, keepdims=True), jnp.max(s_ctx, axis=-1, keepdims=True))
        p_loc = jnp.exp(s_loc - m)
        p_ctx = jnp.exp(s_ctx - m)
        den = jnp.sum(p_loc, axis=-1, keepdims=True) + jnp.sum(p_ctx, axis=-1, keepdims=True)
        o = (_dot(p_loc.astype(BF16), vw) + _dot(p_ctx.astype(BF16), vcx)) * (1.0 / den)
        acc = jnp.where(in_head[0], o[0:w], 0.0)
        for hh in range(1, hpb):
            acc = acc + jnp.where(in_head[hh], o[hh * w:(hh + 1) * w], 0.0)
        o_ref[0, pl.ds(pl.multiple_of(r * w, w), w), :] = acc.astype(BF16)

    def row_group(i, carry):
        for r in range(NA_ROWS_PER_STEP):
            one_row(NA_ROWS_PER_STEP * i + r)
        return carry

    lax.fori_loop(0, rows // NA_ROWS_PER_STEP, row_group, 0)


def _natt(q, k, v, kc, vc, bias):
    b, s, c = q.shape
    nctx = kc.shape[1]
    lw = NA_HEADS_PER_BLK * NA_HEAD_DIM
    rows = s // GRID_W
    seq = pl.BlockSpec((1, s, lw), lambda bi, g: (bi, 0, g))
    cx = pl.BlockSpec((1, nctx, lw), lambda bi, g: (bi, 0, g))
    return pl.pallas_call(
        functools.partial(_natt_body, rows=rows),
        out_shape=jax.ShapeDtypeStruct((b, s, c), BF16),
        grid=(b, c // lw),
        in_specs=[seq, seq, seq, cx, cx,
                  pl.BlockSpec((1,) + bias.shape[1:], lambda bi, g: (g, 0, 0, 0))],
        out_specs=seq,
        compiler_params=_cparams(("parallel", "parallel")),
        name="natt",
    )(q, k, v, kc, vc, bias)


def _hyena_feats(seq_len):
    t = jnp.linspace(0.0, 1.0, seq_len, dtype=F32)[:, None]
    bands = jnp.linspace(1e-4, HYENA_BANDS - 1, HYENA_BANDS, dtype=F32)
    ang = (2.0 * math.pi / seq_len) * jnp.arange(seq_len, dtype=F32)[:, None] * bands[None, :]
    feats = jnp.concatenate([t, jnp.cos(ang), -jnp.sin(ang)], axis=-1)
    return jnp.pad(feats, ((0, 0), (0, 128 - HYENA_EMB)))


def _filt_body(feat_ref, w1_ref, b1_ref, w2_ref, b2_ref, w3_ref, fr_ref, dl_ref, o_ref, l1_ref, h_scr,
               *, halves):
    j = pl.program_id(0)
    hp = functools.partial(jnp.dot, precision=HIGHEST, preferred_element_type=F32)
    feats = feat_ref[...]

    @pl.when(j == 0)
    def _():
        fr = fr_ref[...]
        h = jnp.sin(fr[0:1] * (hp(feats, w1_ref[...]) + b1_ref[...]))
        h_scr[...] = jnp.sin(fr[1:2] * (hp(h, w2_ref[...]) + b2_ref[...]))

    h2 = h_scr[...]
    h_hi = h2.astype(BF16)
    h_lo = (h2 - h_hi.astype(F32)).astype(BF16)
    w3 = w3_ref[...]
    w_hi = w3.astype(BF16)
    w_lo = (w3 - w_hi.astype(F32)).astype(BF16)
    hc = (_dot(jnp.concatenate([h_hi, h_lo], axis=1), jnp.concatenate([w_hi, w_hi], axis=0))
          + _dot(h_hi, w_lo))
    t = feats[:, 0:1]
    hc = hc * jnp.exp(-t * dl_ref[...])
    row = lax.broadcasted_iota(jnp.int32, hc.shape, 0)
    hc = jnp.where((row == 0) & ((j // halves) % 2 == 1), 0.0, hc)
    l1_ref[0] = jnp.sum(jnp.abs(hc), axis=0, keepdims=True)
    o_ref[0] = hc


def _hyena_filter_taps(seq_len, f_w1, f_b1, f_w2, f_b2, f_w3, f_freq, c_hy):
    feats = _hyena_feats(seq_len)
    hid = f_w1.shape[1]
    w1 = jnp.pad(f_w1.astype(F32), ((0, 128 - HYENA_EMB), (0, 0)))
    deltas = jnp.abs(jnp.linspace(math.log(HYENA_TARGET) / HYENA_SLOW_DECAY,
                                  math.log(HYENA_TARGET) / HYENA_FAST_DECAY, c_hy, dtype=F32))[None, :]
    nblk = f_w3.shape[1] // c_hy
    halves = 2
    cb = c_hy // halves
    c0 = lambda a: pl.BlockSpec(a.shape, lambda j: (0, 0))
    b1, b2 = f_b1.reshape(1, hid), f_b2.reshape(1, hid)
    return pl.pallas_call(
        functools.partial(_filt_body, halves=halves),
        out_shape=[jax.ShapeDtypeStruct((nblk, seq_len, c_hy), F32),
                   jax.ShapeDtypeStruct((nblk, 1, c_hy), F32)],
        grid=(nblk * halves,),
        in_specs=[c0(feats), c0(w1), c0(b1), c0(f_w2), c0(b2),
                  pl.BlockSpec((hid, cb), lambda j: (0, j)), c0(f_freq),
                  pl.BlockSpec((1, cb), lambda j: (0, j % halves))],
        out_specs=[pl.BlockSpec((1, seq_len, cb), lambda j: (j // halves, 0, j % halves)),
                   pl.BlockSpec((1, 1, cb), lambda j: (j // halves, 0, j % halves))],
        scratch_shapes=[pltpu.VMEM((seq_len, hid), F32)],
        compiler_params=_cparams(("arbitrary",)),
        name="hyena_filter",
    )(feats, w1, b1, f_w2, b2, f_w3, f_freq, deltas)


def _conv_dft_constants():
    a_half = FFT_A // 2
    n = FFT_A * FFT_R
    ka = np.arange(FFT_KA)[:, None]
    a = np.arange(a_half)[None, :]
    ph = 2.0 * np.pi * ka * a / FFT_A
    m_fwd = np.zeros((2 * FFT_KA_PAD, a_half))
    m_fwd[:FFT_KA] = np.cos(ph)
    m_fwd[FFT_KA_PAD:FFT_KA_PAD + FFT_KA] = -np.sin(ph)
    wgt = np.where((ka == 0) | (ka == FFT_A // 2), 1.0, 2.0)
    m_inv = np.zeros((a_half, 2 * FFT_KA_PAD))
    m_inv[:, :FFT_KA] = (wgt * np.cos(ph)).T / n
    m_inv[:, FFT_KA_PAD:FFT_KA_PAD + FFT_KA] = (-wgt * np.sin(ph)).T / n
    kb = np.arange(FFT_R)[None, :, None]
    b = np.arange(FFT_R)[None, None, :]
    kaa = np.arange(FFT_KA)[:, None, None]
    th = 2.0 * np.pi * (b * kb / FFT_R + b * kaa / n)
    gr, gi = np.cos(th), -np.sin(th)
    g2 = np.zeros((FFT_KA_PAD, 2 * FFT_R, 2 * FFT_R))
    g2[:FFT_KA] = np.block([[gr, -gi], [gi, gr]])
    grt, git = gr.transpose(0, 2, 1), gi.transpose(0, 2, 1)
    g2h = np.zeros_like(g2)
    g2h[:FFT_KA] = np.block([[grt, git], [-git, grt]])
    return _mxu_const(m_fwd), _mxu_const(m_inv), _mxu_const(g2), _mxu_const(g2h)


FFT_NB = 16


def _fwd1_body(m_ref, u_ref, o_ref):
    u = jnp.concatenate([u_ref[0, :, bb, :] for bb in range(FFT_NB)], axis=1).astype(BF16)
    res = _dot(m_ref[...], u)
    o_ref[0, 0] = res[0:FFT_KA_PAD]
    o_ref[0, 1] = res[FFT_KA_PAD:2 * FFT_KA_PAD]


def _conv_fwd1(u, m_fwd):
    n, seq, c = u.shape
    a_half = FFT_A // 2
    return pl.pallas_call(
        _fwd1_body,
        out_shape=jax.ShapeDtypeStruct((n, 2, FFT_KA_PAD, FFT_R * c), F32),
        grid=(n, FFT_R // FFT_NB),
        in_specs=[pl.BlockSpec(m_fwd.shape, lambda i, j: (0, 0)),
                  pl.BlockSpec((1, a_half, FFT_NB, c), lambda i, j: (i, 0, j, 0))],
        out_specs=pl.BlockSpec((1, 2, FFT_KA_PAD, FFT_NB * c), lambda i, j: (i, 0, 0, j)),
        compiler_params=_cparams(("parallel", "parallel")),
        name="conv_fwd1",
    )(m_fwd, u.reshape(n, a_half, FFT_R, c))


FFT_KB = 8


def _rows_to_slabs(src_ref, dst_scr, c):
    for part in range(2):
        for b in range(FFT_R):
            dst_scr[part, :, b, :] = src_ref[0, part, :, b * c:(b + 1) * c]


def _slabs_to_rows(src_scr, dst_ref, c):
    for part in range(2):
        for b in range(FFT_R):
            dst_ref[0, part, :, b * c:(b + 1) * c] = src_scr[part, :, b, :]


def _slab(scr, i):
    return jnp.concatenate([scr[0, i], scr[1, i]], axis=0).astype(BF16)


def _fwd2f_body(sf_ref, sb_ref, g_ref, l1_ref, kf_ref, f3, b3):
    o = pl.program_id(0)
    j = pl.program_id(1)
    r2 = 2 * FFT_R
    c = kf_ref.shape[-1]
    _rows_to_slabs(sf_ref, f3, c)
    _rows_to_slabs(sb_ref, b3, c)
    inv = 1.0 / (l1_ref[2 * o] + l1_ref[2 * o + 1] + EPS)
    for i in range(FFT_KB):
        @pl.when(j * FFT_KB + i < FFT_KA)
        def _():
            xf = _dot(g_ref[i], _slab(f3, i))
            xb = _dot(g_ref[i], _slab(b3, i))
            kf_ref[0, i, 0:FFT_R] = (xf[0:FFT_R] + xb[0:FFT_R]) * inv
            kf_ref[0, i, FFT_R:r2] = (xf[FFT_R:r2] - xb[FFT_R:r2]) * inv

        @pl.when(j * FFT_KB + i >= FFT_KA)
        def _():
            kf_ref[0, i] = jnp.zeros((r2, c), F32)


def _filter_spectrum(s_filt, l1, g2, c):
    n_ord = s_filt.shape[0] // 2
    cols = s_filt.shape[-1]
    r2 = 2 * FFT_R
    return pl.pallas_call(
        _fwd2f_body,
        out_shape=jax.ShapeDtypeStruct((n_ord, FFT_KA_PAD, r2, c), F32),
        grid=(n_ord, FFT_KA_PAD // FFT_KB),
        in_specs=[pl.BlockSpec((1, 2, FFT_KB, cols), lambda o, j: (2 * o, 0, j, 0)),
                  pl.BlockSpec((1, 2, FFT_KB, cols), lambda o, j: (2 * o + 1, 0, j, 0)),
                  pl.BlockSpec((FFT_KB, r2, r2), lambda o, j: (j, 0, 0)),
                  pl.BlockSpec(l1.shape, lambda o, j: (0, 0, 0))],
        out_specs=pl.BlockSpec((1, FFT_KB, r2, c), lambda o, j: (o, j, 0, 0)),
        scratch_shapes=[pltpu.VMEM((2, FFT_KB, FFT_R, c), F32)] * 2,
        compiler_params=_cparams(("parallel", "parallel")),
        name="filter_spectrum",
    )(s_filt, s_filt, g2, l1)


def _mid_body(s_ref, g_ref, gh_ref, kf_ref, t_ref, s3, t3):
    j = pl.program_id(1)
    r2 = 2 * FFT_R
    c = kf_ref.shape[-1]
    _rows_to_slabs(s_ref, s3, c)
    for i in range(FFT_KB):
        @pl.when(j * FFT_KB + i < FFT_KA)
        def _():
            x = _dot(g_ref[i], _slab(s3, i))
            xr, xi = x[0:FFT_R], x[FFT_R:r2]
            kr, ki = kf_ref[0, i, 0:FFT_R], kf_ref[0, i, FFT_R:r2]
            y = jnp.concatenate([xr * kr - xi * ki, xr * ki + xi * kr], axis=0).astype(BF16)
            t = _dot(gh_ref[i], y)
            t3[0, i] = t[0:FFT_R]
            t3[1, i] = t[FFT_R:r2]

        @pl.when(j * FFT_KB + i >= FFT_KA)
        def _():
            t3[0, i] = jnp.zeros((FFT_R, c), F32)
            t3[1, i] = jnp.zeros((FFT_R, c), F32)

    _slabs_to_rows(t3, t_ref, c)


def _conv_mid(s, kf, order, g2, g2h, c):
    n, _, _, cols = s.shape
    r2 = 2 * FFT_R
    blk = pl.BlockSpec((1, 2, FFT_KB, cols), lambda i, j: (i, 0, j, 0))
    gspec = pl.BlockSpec((FFT_KB, r2, r2), lambda i, j: (j, 0, 0))
    return pl.pallas_call(
        _mid_body,
        out_shape=jax.ShapeDtypeStruct(s.shape, F32),
        grid=(n, FFT_KA_PAD // FFT_KB),
        in_specs=[blk, gspec, gspec,
                  pl.BlockSpec((1, FFT_KB, r2, c), lambda i, j: (order, j, 0, 0))],
        out_specs=blk,
        scratch_shapes=[pltpu.VMEM((2, FFT_KB, FFT_R, c), F32)] * 2,
        compiler_params=_cparams(("parallel", "parallel")),
        name="conv_mid",
    )(s, g2, g2h, kf)


def _inv1_body(m_ref, t_ref, u_ref, xg_ref, sk_ref, o_ref):
    c = u_ref.shape[-1]
    t2 = t_ref[0].reshape(2 * FFT_KA_PAD, FFT_NB * c).astype(BF16)
    y = _dot(m_ref[...], t2)
    for bb in range(FFT_NB):
        conv = y[:, bb * c:(bb + 1) * c] + u_ref[0, :, bb, :] * sk_ref[...]
        o_ref[0, :, bb, :] = xg_ref[0, :, bb, :] * conv


def _conv_inv1(t, u, xg, skip, m_inv):
    n, seq, c = u.shape
    a_half = FFT_A // 2
    sk = skip.astype(F32).reshape(1, c)
    uspec = pl.BlockSpec((1, a_half, FFT_NB, c), lambda i, j: (i, 0, j, 0))
    view = lambda a: a.reshape(n, a_half, FFT_R, c)
    out = pl.pallas_call(
        _inv1_body,
        out_shape=jax.ShapeDtypeStruct((n, a_half, FFT_R, c), F32),
        grid=(n, FFT_R // FFT_NB),
        in_specs=[pl.BlockSpec(m_inv.shape, lambda i, j: (0, 0)),
                  pl.BlockSpec((1, 2, FFT_KA_PAD, FFT_NB * c), lambda i, j: (i, 0, 0, j)),
                  uspec, uspec,
                  pl.BlockSpec((1, c), lambda i, j: (0, 0))],
        out_specs=uspec,
        compiler_params=_cparams(("parallel", "parallel")),
        name="conv_inv1",
    )(m_inv, t, view(u), view(xg), sk)
    return out.reshape(n, seq, c)


def _hyena(v, x1, x2, f_w1, f_b1, f_w2, f_b2, f_w3, f_freq, skip):
    _, seq, c = v.shape
    assert 2 * seq == FFT_A * FFT_R
    m_fwd, m_inv, g2, g2h = _conv_dft_constants()
    taps, l1 = _hyena_filter_taps(seq, f_w1, f_b1, f_w2, f_b2, f_w3, f_freq, c)
    kf = _filter_spectrum(_conv_fwd1(taps, m_fwd), l1, g2, c)
    y = v
    for order, xg in enumerate((x1, x2)):
        t = _conv_mid(_conv_fwd1(y, m_fwd), kf, order, g2, g2h, c)
        y = _conv_inv1(t, y, xg, skip[order], m_inv)
    return y


def _mix_ffn_body(x_ref, a1_ref, a2_ref, wm_ref, gm_ref, g_ref, sh_ref, sc_ref, gt_ref,
                  w1_ref, w3_ref, w2_ref, o_ref, *, fb):
    c1 = a1_ref.shape[-1]
    mixed = _dot(a1_ref[0].astype(BF16), wm_ref[0:c1]) + _dot(a2_ref[0].astype(BF16), wm_ref[c1:])
    xm = x_ref[0] + gm_ref[0] * mixed
    h = _norm_mod(xm, g_ref[...], sh_ref[0], sc_ref[0]).astype(BF16)
    acc = None
    for lo in range(0, w1_ref.shape[1], fb):
        a = _dot(h, w1_ref[:, lo:lo + fb])
        u = (a * jax.nn.sigmoid(a) * _dot(h, w3_ref[:, lo:lo + fb])).astype(BF16)
        part = _dot(u, w2_ref[lo:lo + fb, :])
        acc = part if acc is None else acc + part
    o_ref[0] = xm + gt_ref[0] * acc


def _mix_ffn(x, a1, a2, w_mix, gate_mix, g, shift, scale, gate, w1, w3, w2, tm=512, fb=1408):
    b, s, d = x.shape
    tok = lambda c: pl.BlockSpec((1, tm, c), lambda bi, i: (bi, i, 0))
    per_b = pl.BlockSpec((1, 1, d), lambda bi, i: (bi, 0, 0))
    const = lambda a: pl.BlockSpec(a.shape, lambda bi, i: (0, 0))
    resident = lambda a: pl.BlockSpec(a.shape, lambda bi, i: (0, 0), pipeline_mode=pl.Buffered(1))
    return pl.pallas_call(
        functools.partial(_mix_ffn_body, fb=fb),
        out_shape=jax.ShapeDtypeStruct(x.shape, F32),
        grid=(b, s // tm),
        in_specs=[tok(d), tok(a1.shape[-1]), tok(a2.shape[-1]), resident(w_mix), per_b,
                  const(g), per_b, per_b, per_b, resident(w1), resident(w3), resident(w2)],
        out_specs=tok(d),
        compiler_params=_cparams(("parallel", "parallel"), vmem_limit=FFN_VMEM_LIMIT),
        name="mix_ffn",
    )(x, a1, a2, w_mix, gate_mix, g, shift, scale, gate, w1, w3, w2)


def _fm_constants(cg):
    j = np.arange(cg)[:, None]
    m = np.arange(cg)[None, :]
    ph = 2.0 * np.pi * j * m / cg
    w_cs = np.concatenate([np.cos(ph), np.sin(ph)], axis=1)
    d = np.arange(FM_A)[:, None]
    a = np.arange(FM_A)[None, :]
    ph = 2.0 * np.pi * d * a / FM_A
    fr, fi = np.cos(ph), -np.sin(ph)
    m1 = np.block([[fr, fi], [fi, -fr]])
    n = FM_A * FM_A
    dd = np.arange(FM_A)[:, None, None]
    c = np.arange(FM_A)[None, :, None]
    b = np.arange(FM_A)[None, None, :]
    th = 2.0 * np.pi * (b * c / FM_A + b * dd / n)
    gcat = np.concatenate([np.cos(th), np.sin(th)], axis=2)
    return _mxu_const(w_cs), _mxu_const(m1), _mxu_const(gcat)


def _fm_front_body(x_ref, g_ref, sh_ref, sc_ref, w_ref, m_ref, o_ref, *, cg, nb):
    d = x_ref.shape[-1]
    xs = jnp.concatenate([x_ref[0, :, bb, :] for bb in range(nb)], axis=0)
    h = _norm_mod(xs, g_ref[...], sh_ref[0], sc_ref[0]).astype(BF16)
    pq = [_dot(h[:, grp * cg:(grp + 1) * cg], w_ref[...]) for grp in range(d // cg)]
    p = jnp.concatenate([t[:, 0:cg] for t in pq], axis=1)
    q = jnp.concatenate([t[:, cg:2 * cg] for t in pq], axis=1)
    for bb in range(nb):
        rows = slice(bb * FM_A, (bb + 1) * FM_A)
        res = _dot(m_ref[...], jnp.concatenate([p[rows], q[rows]], axis=0).astype(BF16))
        o_ref[0, 0, :, bb, :] = res[0:FM_A]
        o_ref[0, 1, :, bb, :] = res[FM_A:2 * FM_A]


def _fm_front(x, g, shift, scale, w_cs, m1, nb=8):
    b, s, d = x.shape
    cg = w_cs.shape[0]
    per_b = pl.BlockSpec((1, 1, d), lambda bi, j: (bi, 0, 0))
    const = lambda a: pl.BlockSpec(a.shape, lambda bi, j: (0, 0))
    return pl.pallas_call(
        functools.partial(_fm_front_body, cg=cg, nb=nb),
        out_shape=jax.ShapeDtypeStruct((b, 2, FM_A, s // FM_A, d), F32),
        grid=(b, s // FM_A // nb),
        in_specs=[pl.BlockSpec((1, FM_A, nb, d), lambda bi, j: (bi, 0, j, 0)),
                  const(g), per_b, per_b, const(w_cs), const(m1)],
        out_specs=pl.BlockSpec((1, 2, FM_A, nb, d), lambda bi, j: (bi, 0, 0, j, 0)),
        compiler_params=_cparams(("parallel", "parallel")),
        name="fm_front",
    )(x.reshape(b, FM_A, s // FM_A, d), g, shift, scale, w_cs, m1)


def _fm_s2_body(s_ref, g_ref, o_ref, *, dblk, scale):
    for i in range(dblk):
        s2 = jnp.concatenate([s_ref[0, 0, i], s_ref[0, 1, i]], axis=0).astype(BF16)
        o_ref[:, i, :] = _dot(g_ref[i], s2) * scale


def _fm_stage2(sv, gcat, seq, d, dblk=8):
    b = sv.shape[0]
    scale = 1.0 / math.sqrt(seq * (d // F_GROUPS))
    out = pl.pallas_call(
        functools.partial(_fm_s2_body, dblk=dblk, scale=scale),
        out_shape=jax.ShapeDtypeStruct((b * FM_A, dblk * (FM_A // dblk), d), F32),
        grid=(b, FM_A // dblk),
        in_specs=[pl.BlockSpec((1, 2, dblk, FM_A, d), lambda bi, j: (bi, 0, j, 0, 0)),
                  pl.BlockSpec((dblk, FM_A, 2 * FM_A), lambda bi, j: (j, 0, 0))],
        out_specs=pl.BlockSpec((FM_A, dblk, d), lambda bi, j: (bi, j, 0)),
        compiler_params=_cparams(("parallel", "parallel")),
        name="fm_stage2",
    )(sv, gcat)
    return out.reshape(b, seq, d)


def _fourier_mix(x, g, shift, scale):
    b, s, d = x.shape
    assert s == FM_A * FM_A
    w_cs, m1, gcat = _fm_constants(d // F_GROUPS)
    return _fm_stage2(_fm_front(x, g, shift, scale, w_cs, m1), gcat, s, d)


LANES = 128
ROW_SL = 8
MOE_TM = 1024
DMA_WINDOW = 128


def _router_body(x_ref, yf_ref, wf_ref, gf_ref, g_ref, sh_ref, sc_ref, wr_ref, br_ref,
                 xo_ref, h_ref, meta_ref, gw_ref, cnt_ref, carry):
    i = pl.program_id(0)

    @pl.when(i == 0)
    def _():
        carry[...] = jnp.zeros_like(carry)

    xm = x_ref[...] + gf_ref[0] * _dot(yf_ref[...].astype(BF16), wf_ref[...])
    xo_ref[...] = xm
    h = _norm_mod(xm, g_ref[...], sh_ref[0], sc_ref[0])
    _rows_to_tiles(h_ref, h)
    h_hi = h.astype(BF16)
    h_lo = (h - h_hi.astype(F32)).astype(BF16)
    by_hi = _dot(h_hi, wr_ref[...])
    logits = by_hi[:, 0:LANES] + by_hi[:, LANES:] + _dot(h_lo, wr_ref[:, 0:LANES]) + br_ref[...]
    lane = lax.broadcasted_iota(jnp.int32, logits.shape, 1)
    nl = logits.shape[-1]
    m1 = jnp.max(logits, axis=-1, keepdims=True)
    i1 = jnp.min(jnp.where(logits == m1, lane, nl), axis=-1, keepdims=True)
    rest = jnp.where(lane == i1, -3.0e38, logits)
    m2 = jnp.max(rest, axis=-1, keepdims=True)
    i2 = jnp.min(jnp.where(rest == m2, lane, nl), axis=-1, keepdims=True)
    e = jnp.exp(m2 - m1)
    gw_ref[...] = jnp.where(lane == 0, 1.0 / (1.0 + e), jnp.where(lane == 1, e / (1.0 + e), 0.0))
    onehot = jnp.where((lane == i1) | (lane == i2), 1.0, 0.0)
    tm = onehot.shape[0]
    earlier = lax.broadcasted_iota(jnp.int32, (tm, tm), 0) > lax.broadcasted_iota(jnp.int32, (tm, tm), 1)
    excl = _dot(jnp.where(earlier, 1.0, 0.0).astype(BF16), onehot.astype(BF16)) + carry[...]
    r1 = jnp.sum(jnp.where(lane == i1, excl, 0.0), axis=-1, keepdims=True).astype(jnp.int32)
    r2 = jnp.sum(jnp.where(lane == i2, excl, 0.0), axis=-1, keepdims=True).astype(jnp.int32)
    meta_ref[...] = jnp.where(lane == 0, i1, jnp.where(lane == 1, i2, jnp.where(lane == 2, r1, jnp.where(lane == 3, r2, 0))))
    carry[...] = carry[...] + jnp.sum(onehot, axis=0, keepdims=True)
    cnt_ref[...] = carry[...]


def _router(x, y_f, w_f, gate_f, g, shift, scale, w_router, b_router, tm=512):
    b, s, d = x.shape
    t = b * s
    ne = w_router.shape[1]
    wr = jnp.pad(w_router.astype(F32), ((0, 0), (0, LANES - ne)))
    wr_hi = wr.astype(BF16)
    wr = jnp.concatenate([wr_hi, (wr - wr_hi.astype(F32)).astype(BF16)], axis=1)
    br = jnp.pad(b_router.astype(F32).reshape(1, ne), ((0, 0), (0, LANES - ne)), constant_values=NEG_INF)
    spt = s // tm
    per_b = pl.BlockSpec((1, 1, d), lambda i: (i // spt, 0, 0))
    const = lambda a: pl.BlockSpec(a.shape, lambda i: (0, 0))
    tok = pl.BlockSpec((tm, d), lambda i: (i, 0))
    return pl.pallas_call(
        _router_body,
        out_shape=[jax.ShapeDtypeStruct((t, d), F32),
                   jax.ShapeDtypeStruct((t * ROW_SL, LANES), F32),
                   jax.ShapeDtypeStruct((t, LANES), jnp.int32),
                   jax.ShapeDtypeStruct((t, LANES), F32),
                   jax.ShapeDtypeStruct((1, LANES), F32)],
        grid=(t // tm,),
        in_specs=[tok, tok, const(w_f), per_b, const(g), per_b, per_b, const(wr), const(br)],
        out_specs=[tok,
                   pl.BlockSpec((tm * ROW_SL, LANES), lambda i: (i, 0)),
                   pl.BlockSpec((tm, LANES), lambda i: (i, 0)),
                   pl.BlockSpec((tm, LANES), lambda i: (i, 0)),
                   pl.BlockSpec((1, LANES), lambda i: (0, 0))],
        scratch_shapes=[pltpu.VMEM((1, LANES), F32)],
        compiler_params=_cparams(("arbitrary",)),
        name="router",
    )(x.reshape(t, d), y_f.reshape(t, d), w_f, gate_f, g, shift, scale, wr, br)


def _moe_plan(meta, counts, ne, tm):
    i1, i2, r1, r2 = meta[:, 0], meta[:, 1], meta[:, 2], meta[:, 3]
    cnt = counts[0, :ne].astype(jnp.int32)
    padded = ((cnt + tm - 1) // tm) * tm
    ends = jnp.cumsum(padded)
    offs = ends - padded
    pick = lambda idx: sum(jnp.where(idx == e, offs[e], 0) for e in range(ne))
    pos = jnp.concatenate([pick(i1) + r1, pick(i2) + r2]).astype(jnp.int32)
    n_tiles = (2 * meta.shape[0]) // tm + ne
    n_used = (ends[ne - 1] // tm).astype(jnp.int32)
    tile_start = jnp.minimum(jnp.arange(n_tiles, dtype=jnp.int32), n_used - 1) * tm
    tile_expert = jnp.sum(tile_start[:, None] >= ends[None, :], axis=1).astype(jnp.int32)
    group_end = sum(jnp.where(tile_expert == e, offs[e] + cnt[e], 0) for e in range(ne))
    tile_rows = jnp.clip(group_end - tile_start, 0, tm).astype(jnp.int32)
    return pos, offs + cnt, padded - cnt, tile_expert, n_used.reshape(1), tile_rows


def _windowed_copies(n, start_copy, wait_one, per_iter):
    def body(i, carry):
        @pl.when(i >= DMA_WINDOW)
        def _():
            for _ in range(per_iter):
                wait_one()
        start_copy(i)
        return carry

    lax.fori_loop(0, n, body, 0)

    def drain(i, carry):
        for _ in range(per_iter):
            wait_one()
        return carry

    lax.fori_loop(0, jnp.minimum(n, DMA_WINDOW), drain, 0)


def _tile_of(ref, row):
    return ref.at[pl.ds(pl.multiple_of(row * ROW_SL, ROW_SL), ROW_SL)]


def _tiles_to_rows(ref, n, first=0):
    return jnp.concatenate([ref[pl.ds(first * ROW_SL + sl, n, stride=ROW_SL), :] for sl in range(ROW_SL)], axis=1)


def _rows_to_tiles(ref, val):
    n = val.shape[0]
    for sl in range(ROW_SL):
        ref[pl.ds(sl, n, stride=ROW_SL), :] = val[:, sl * LANES:(sl + 1) * LANES]


def _dispatch_body(pos_ref, pad_start_ref, pad_n_ref, h_ref, xs_hbm, sem, *, n_tok, ne):
    i = pl.program_id(0)
    td = h_ref.shape[0] // ROW_SL
    base = i * td
    copy = lambda src, dst: pltpu.make_async_copy(_tile_of(h_ref, src), _tile_of(xs_hbm, dst), sem)
    wait_one = lambda: copy(0, 0).wait()

    def start_token(r, carry):
        copy(r, pos_ref[base + r]).start(priority=0)
        copy(r, pos_ref[n_tok + base + r]).start(priority=1)
        return carry

    lax.fori_loop(0, td, start_token, 0, unroll=8)
    whole_tile = pltpu.make_async_copy(h_ref, xs_hbm.at[pl.ds(0, td * ROW_SL)], sem)
    whole_tile.wait()
    whole_tile.wait()

    @pl.when(i == 0)
    def _():
        for e in range(ne):
            first = pad_start_ref[e]
            _windowed_copies(pad_n_ref[e], lambda r: copy(0, first + r).start(), wait_one, 1)


def _moe_dispatch(h3, pos, pad_start, pad_n, n_rows, td=1024):
    n_tok = h3.shape[0] // ROW_SL
    ne = pad_start.shape[0]
    return pl.pallas_call(
        functools.partial(_dispatch_body, n_tok=n_tok, ne=ne),
        out_shape=jax.ShapeDtypeStruct((n_rows * ROW_SL, LANES), h3.dtype),
        grid_spec=pltpu.PrefetchScalarGridSpec(
            num_scalar_prefetch=3, grid=(n_tok // td,),
            in_specs=[pl.BlockSpec((td * ROW_SL, LANES), lambda i, p, ps, pn: (i, 0))],
            out_specs=pl.BlockSpec(memory_space=pl.ANY),
            scratch_shapes=[pltpu.SemaphoreType.DMA(())]),
        compiler_params=_cparams(("arbitrary",)),
        name="moe_dispatch",
    )(pos, pad_start, pad_n, h3)


def _moe_grouped_body(te_ref, nu_ref, tr_ref, xs_ref, w1_ref, w3_ref, w2_ref, y_ref, xb_scr, acc_scr):
    i = pl.program_id(0)
    j = pl.program_id(1)
    tm = xb_scr.shape[0]
    hm = tm // 2

    def expert_rows(nrows):
        h = xb_scr[0:nrows]
        a = _dot(h, w1_ref[0].astype(BF16))
        u = (a * jax.nn.sigmoid(a) * _dot(h, w3_ref[0].astype(BF16))).astype(BF16)
        part = _dot(u, w2_ref[0].astype(BF16))

        @pl.when(j == 0)
        def _():
            acc_scr[0:nrows] = part

        @pl.when(j > 0)
        def _():
            acc_scr[0:nrows] += part

    @pl.when(i < nu_ref[0])
    def _():
        @pl.when(j == 0)
        def _():
            xb_scr[...] = _tiles_to_rows(xs_ref, tm).astype(BF16)

        @pl.when(tr_ref[i] > hm)
        def _():
            expert_rows(tm)

        @pl.when(tr_ref[i] <= hm)
        def _():
            expert_rows(hm)

            @pl.when(j == 0)
            def _():
                acc_scr[hm:tm] = jnp.zeros((tm - hm, acc_scr.shape[1]), F32)

        @pl.when(j == pl.num_programs(1) - 1)
        def _():
            _rows_to_tiles(y_ref, acc_scr[...])


def _moe_grouped(xs, tile_expert, n_used, tile_rows, w1, w3, w2, tm, fb=512):
    ne, d, f = w1.shape
    n_rows = xs.shape[0] // ROW_SL
    nj = f // fb
    row_tile = lambda i, j, te, nu, tr: (jnp.maximum(jnp.minimum(i, nu[0] - 1), 0), 0)
    jj = lambda i, j, nu: jnp.where(i < nu[0], j, nj - 1)
    return pl.pallas_call(
        _moe_grouped_body,
        out_shape=jax.ShapeDtypeStruct(xs.shape, F32),
        grid_spec=pltpu.PrefetchScalarGridSpec(
            num_scalar_prefetch=3, grid=(n_rows // tm, nj),
            in_specs=[pl.BlockSpec((tm * ROW_SL, LANES), row_tile),
                      pl.BlockSpec((1, d, fb), lambda i, j, te, nu, tr: (te[i], 0, jj(i, j, nu))),
                      pl.BlockSpec((1, d, fb), lambda i, j, te, nu, tr: (te[i], 0, jj(i, j, nu))),
                      pl.BlockSpec((1, fb, d), lambda i, j, te, nu, tr: (te[i], jj(i, j, nu), 0))],
            out_specs=pl.BlockSpec((tm * ROW_SL, LANES), row_tile),
            scratch_shapes=[pltpu.VMEM((tm, d), BF16), pltpu.VMEM((tm, d), F32)]),
        compiler_params=_cparams(("arbitrary", "arbitrary")),
        name="moe_grouped",
    )(tile_expert, n_used, tile_rows, xs, w1, w3, w2)


def _moe_final_body(pos_ref, x_ref, y_hbm, gw_ref, gt_ref, fg_ref, o_ref, yg_scr, sem, *, n_tok):
    i = pl.program_id(0)
    tc = x_ref.shape[0]
    slot = i % 2

    def gather_tile(step, into):
        base = step * tc

        def start_token(r, carry):
            dst = yg_scr.at[into]
            pltpu.make_async_copy(_tile_of(y_hbm, pos_ref[base + r]), _tile_of(dst, r),
                                  sem.at[into]).start(priority=0)
            pltpu.make_async_copy(_tile_of(y_hbm, pos_ref[n_tok + base + r]), _tile_of(dst, tc + r),
                                  sem.at[into]).start(priority=1)
            return carry

        lax.fori_loop(0, tc, start_token, 0, unroll=8)

    @pl.when(i == 0)
    def _():
        gather_tile(0, 0)

    @pl.when(i + 1 < pl.num_programs(0))
    def _():
        gather_tile(i + 1, 1 - slot)

    pltpu.make_async_copy(y_hbm.at[pl.ds(0, 2 * tc * ROW_SL)], yg_scr.at[slot], sem.at[slot]).wait()
    gw = gw_ref[...]
    rows = yg_scr.at[slot]
    y = gw[:, 0:1] * _tiles_to_rows(rows, tc) + gw[:, 1:2] * _tiles_to_rows(rows, tc, first=tc)
    xo = x_ref[...] + gt_ref[0] * y
    ms = jnp.mean(xo * xo, axis=-1, keepdims=True)
    o_ref[...] = xo * lax.rsqrt(ms + EPS) * fg_ref[...]


def _moe_final(x, y, pos, gw, gt, final_g, tc=512):
    b, s, d = x.shape
    t = b * s
    spt = s // tc
    nsl = d // LANES
    out = pl.pallas_call(
        functools.partial(_moe_final_body, n_tok=t),
        out_shape=jax.ShapeDtypeStruct((t, d), F32),
        grid_spec=pltpu.PrefetchScalarGridSpec(
            num_scalar_prefetch=1, grid=(t // tc,),
            in_specs=[pl.BlockSpec((tc, d), lambda i, p: (i, 0)),
                      pl.BlockSpec(memory_space=pl.ANY),
                      pl.BlockSpec((tc, LANES), lambda i, p: (i, 0)),
                      pl.BlockSpec((1, 1, d), lambda i, p: (i // spt, 0, 0)),
                      pl.BlockSpec(final_g.shape, lambda i, p: (0, 0))],
            out_specs=pl.BlockSpec((tc, d), lambda i, p: (i, 0)),
            scratch_shapes=[pltpu.VMEM((2, 2 * tc * ROW_SL, LANES), F32), pltpu.SemaphoreType.DMA((2,))]),
        compiler_params=_cparams(("arbitrary",)),
        name="moe_final",
    )(pos, x.reshape(t, d), y, gw, gt, final_g)
    return out.reshape(b, s, d)


def _moe_routed(x, y_f, w_f, gate_f, g, shift, scale, gt, final_g, w_router, b_router, w1, w3, w2):
    ne = w1.shape[0]
    tm = MOE_TM
    x1, h3, meta, gw, counts = _router(x, y_f, w_f, gate_f, g, shift, scale, w_router, b_router)
    pos, pad_start, pad_n, tile_expert, n_used, tile_rows = _moe_plan(meta, counts, ne, tm)
    assert x.shape[-1] == ROW_SL * LANES
    n_rows = (2 * (h3.shape[0] // ROW_SL) // tm + ne) * tm
    xs = _moe_dispatch(h3, pos, pad_start, pad_n, n_rows)
    y = _moe_grouped(xs, tile_expert, n_used, tile_rows, w1, w3, w2, tm)
    return _moe_final(x1.reshape(x.shape), y, pos, gw, gt, final_g)


def kernel(x, c, ctx, c_ctx, w_ada, b_ada, norm_g, w_in, hy_short_w, hy_short_b, hy_f_w1, hy_f_b1, hy_f_w2, hy_f_b2, hy_f_w3, hy_f_freq, hy_skip, na_rpb, w_mix_out, ffn_w1, ffn_w3, ffn_w2, w_fourier, w_router, b_router, moe_w1, moe_w3, moe_w2, final_g):
    b, s, d = x.shape
    depth = w_ada.shape[0]
    assert depth == 2, "layer 0 mixes with Hyena/attention, layer 1 with Fourier/MoE"
    c_hy = hy_skip.shape[-1]
    c_na = d - c_hy

    cvec = jnp.concatenate([c, c_ctx[None, :], jnp.zeros((8 - b - 1, d), F32)], axis=0)
    mods = _ada(cvec, w_ada, b_ada)

    def mod(layer, idx, ctx_row=False):
        m = mods[layer, :, idx * d:(idx + 1) * d]
        return m[b:b + 1, None, :] if ctx_row else m[0:b, None, :]

    row = lambda a: a.reshape(1, -1)

    w_in0 = w_in[0].astype(BF16)
    w_hy, w_qkv = w_in0[:, 0:3 * c_hy], w_in0[:, 3 * c_hy:]
    v, x1, x2, q, k, va = _inproj(x, row(norm_g[0, 0]), mod(0, 0), mod(0, 1), w_hy, w_qkv,
                                  hy_short_w[0], row(hy_short_b[0]))
    kc, vc = _ctxkv(ctx, row(norm_g[0, 0]), mod(0, 0, True), mod(0, 1, True), w_qkv[:, c_na:])
    y_na = _natt(q, k, va, kc, vc, _na_bias_table(na_rpb[0]))
    y_hy = _hyena(v, x1, x2, hy_f_w1[0], hy_f_b1[0], hy_f_w2[0], hy_f_b2[0], hy_f_w3[0],
                  hy_f_freq[0], hy_skip[0])
    x = _mix_ffn(x, y_hy, y_na, w_mix_out[0].astype(BF16), mod(0, 2),
                 row(norm_g[0, 1]), mod(0, 3), mod(0, 4), mod(0, 5),
                 ffn_w1[0].astype(BF16), ffn_w3[0].astype(BF16), ffn_w2[0].astype(BF16))

    y_f = _fourier_mix(x, row(norm_g[1, 0]), mod(1, 0), mod(1, 1))
    return _moe_routed(x, y_f, w_fourier[0].astype(BF16), mod(1, 2),
                       row(norm_g[1, 1]), mod(1, 3), mod(1, 4), mod(1, 5), row(final_g),
                       w_router[0], b_router[0],
                       moe_w1[0], moe_w3[0], moe_w2[0])
```

```python
import functools
import math

import numpy as np
import jax
import jax.numpy as jnp
from jax import lax
from jax.experimental import pallas as pl
from jax.experimental.pallas import tpu as pltpu

F32 = jnp.float32
BF16 = jnp.bfloat16
HIGHEST = lax.Precision.HIGHEST

GRID_W = 64
NA_HEAD_DIM = 32
NA_WIN_R = 8
NA_WIN_C = 16
HYENA_EMB = 33
HYENA_BANDS = (HYENA_EMB - 1) // 2
HYENA_FAST_DECAY = 0.3
HYENA_SLOW_DECAY = 1.5
HYENA_TARGET = 1e-2
F_GROUPS = 4
N_MOD = 6
EPS = 1e-6
NEG_INF = -1e30

FFT_A = 64
FFT_R = 128
FFT_KA = FFT_A // 2 + 1
FFT_KA_PAD = 40
FM_A = 64

VMEM_LIMIT = 48 * 1024 * 1024
FFN_VMEM_LIMIT = 56 * 1024 * 1024


def _cparams(sem, vmem_limit=VMEM_LIMIT):
    return pltpu.CompilerParams(dimension_semantics=sem, vmem_limit_bytes=vmem_limit)


def _dot(a, b):
    return jnp.dot(a, b, preferred_element_type=F32)


def _mxu_const(m):
    return jnp.asarray(m, dtype=F32).astype(BF16)


def _norm_mod(x, g, shift, scale):
    ms = jnp.mean(x * x, axis=-1, keepdims=True)
    y = x * lax.rsqrt(ms + EPS) * g
    return y * (1.0 + scale) + shift


def _ada_body(c_ref, w_ref, b_ref, o_ref):
    cv = c_ref[...]
    s = cv * jax.nn.sigmoid(cv)
    o_ref[0] = jnp.dot(s, w_ref[0], precision=HIGHEST, preferred_element_type=F32) + b_ref[0]


def _ada(cvec, w_ada, b_ada):
    depth, d, n = w_ada.shape
    rows = cvec.shape[0]
    bn = n // 4
    return pl.pallas_call(
        _ada_body,
        out_shape=jax.ShapeDtypeStruct((depth, rows, n), F32),
        grid=(depth, n // bn),
        in_specs=[pl.BlockSpec((rows, d), lambda l, j: (0, 0)),
                  pl.BlockSpec((1, d, bn), lambda l, j: (l, 0, j)),
                  pl.BlockSpec((1, 1, bn), lambda l, j: (l, 0, j))],
        out_specs=pl.BlockSpec((1, rows, bn), lambda l, j: (l, 0, j)),
        compiler_params=_cparams(("parallel", "parallel")),
        name="ada",
    )(cvec, w_ada, b_ada.reshape(depth, 1, n))


def _inproj_body(x_ref, xp_ref, xn_ref, g_ref, sh_ref, sc_ref, why_ref, wqkv_ref, sw_ref, sb_ref,
                 v_ref, x1_ref, x2_ref, q_ref, k_ref, va_ref, *, n_tiles, q_scale, c_hy, c_na):
    i = pl.program_id(1)
    g, sh, sc = g_ref[...], sh_ref[0], sc_ref[0]
    hf = _norm_mod(x_ref[0], g, sh, sc)
    h = hf.astype(BF16)
    tm = hf.shape[0]
    hx = jnp.concatenate([_norm_mod(xp_ref[0], g, sh, sc), hf, _norm_mod(xn_ref[0], g, sh, sc)],
                         axis=0).astype(BF16)
    row = lax.broadcasted_iota(jnp.int32, (tm, c_hy), 0)
    sw = sw_ref[...]
    sb = sb_ref[...]
    for ci, out_ref in enumerate((v_ref, x1_ref, x2_ref)):
        cols = slice(ci * c_hy, (ci + 1) * c_hy)
        zx = _dot(hx, why_ref[:, cols])
        zh = zx[8:8 + tm]
        zp = jnp.where(i > 0, zx[7:8], 0.0)
        zn = jnp.where(i < n_tiles - 1, zx[8 + tm:9 + tm], 0.0)
        z_m1 = jnp.where(row == 0, zp, pltpu.roll(zh, 1, 0))
        z_p1 = jnp.where(row == tm - 1, zn, pltpu.roll(zh, tm - 1, 0))
        out_ref[0] = z_m1 * sw[0:1, cols] + zh * sw[1:2, cols] + z_p1 * sw[2:3, cols] + sb[:, cols]
    for ci, (out_ref, mult) in enumerate(((q_ref, q_scale), (k_ref, None), (va_ref, None))):
        z = _dot(h, wqkv_ref[:, ci * c_na:(ci + 1) * c_na])
        out_ref[0] = (z if mult is None else z * mult).astype(BF16)


def _inproj(x, g, shift, scale, w_hy, w_qkv, short_w, short_b, tm=1024):
    b, s, d = x.shape
    c_hy = w_hy.shape[1] // 3
    c_na = w_qkv.shape[1] // 3
    n_tiles = s // tm
    r8 = tm // 8
    body = functools.partial(_inproj_body, n_tiles=n_tiles, q_scale=NA_HEAD_DIM ** -0.5,
                             c_hy=c_hy, c_na=c_na)
    tok = lambda c: pl.BlockSpec((1, tm, c), lambda bi, i: (bi, i, 0))
    full2 = lambda a: pl.BlockSpec(a.shape, lambda bi, i: (0, 0))
    resident = lambda a: pl.BlockSpec(a.shape, lambda bi, i: (0, 0), pipeline_mode=pl.Buffered(1))
    per_b = pl.BlockSpec((1, 1, d), lambda bi, i: (bi, 0, 0))
    return pl.pallas_call(
        body,
        out_shape=[jax.ShapeDtypeStruct((b, s, c_hy), F32)] * 3 + [jax.ShapeDtypeStruct((b, s, c_na), BF16)] * 3,
        grid=(b, n_tiles),
        in_specs=[tok(d),
                  pl.BlockSpec((1, 8, d), lambda bi, i: (bi, jnp.maximum(i * r8 - 1, 0), 0)),
                  pl.BlockSpec((1, 8, d), lambda bi, i: (bi, jnp.minimum((i + 1) * r8, s // 8 - 1), 0)),
                  full2(g), per_b, per_b, resident(w_hy), resident(w_qkv), full2(short_w), full2(short_b)],
        out_specs=[tok(c_hy)] * 3 + [tok(c_na)] * 3,
        compiler_params=_cparams(("parallel", "parallel"), vmem_limit=FFN_VMEM_LIMIT),
        name="inproj",
    )(x, x, x, g, shift, scale, w_hy, w_qkv, short_w, short_b)


def _ctxkv_body(x_ref, g_ref, sh_ref, sc_ref, w_ref, k_ref, v_ref, *, c_na):
    h = _norm_mod(x_ref[0], g_ref[...], sh_ref[0], sc_ref[0]).astype(BF16)
    z = _dot(h, w_ref[...])
    k_ref[0] = z[:, 0:c_na].astype(BF16)
    v_ref[0] = z[:, c_na:2 * c_na].astype(BF16)


def _ctxkv(ctx, g, shift, scale, w_kv):
    b, n, d = ctx.shape
    c_na = w_kv.shape[1] // 2
    one = pl.BlockSpec((1, 1, d), lambda bi: (0, 0, 0))
    return pl.pallas_call(
        functools.partial(_ctxkv_body, c_na=c_na),
        out_shape=[jax.ShapeDtypeStruct((b, n, c_na), BF16)] * 2,
        grid=(b,),
        in_specs=[pl.BlockSpec((1, n, d), lambda bi: (bi, 0, 0)),
                  pl.BlockSpec(g.shape, lambda bi: (0, 0)), one, one,
                  pl.BlockSpec(w_kv.shape, lambda bi: (0, 0))],
        out_specs=[pl.BlockSpec((1, n, c_na), lambda bi: (bi, 0, 0))] * 2,
        compiler_params=_cparams(("parallel",)),
        name="ctxkv",
    )(ctx, g, shift, scale, w_kv)


NA_HEADS_PER_BLK = 8
NA_ROWS_PER_STEP = 8


def _na_bias_body(r_ref, e_ref, ok_ref, o_ref):
    t = jnp.dot(r_ref[...], e_ref[...], precision=HIGHEST, preferred_element_type=F32)
    o_ref[...] = jnp.where(ok_ref[...] > 0.5, t, NEG_INF)


def _na_bias_table(rpb):
    w = GRID_W
    h, nr, nc = rpb.shape
    col = np.arange(w)[:, None]
    kc = np.arange(w)[None, :]
    c_start = np.clip(col - NA_WIN_C // 2, 0, w - NA_WIN_C)
    valid = ((kc >= c_start) & (kc < c_start + NA_WIN_C)).reshape(1, w * w)
    expand = (np.arange(32)[:, None, None] == (kc - col + NA_WIN_C - 1)[None]).reshape(32, w * w)
    rp = jnp.pad(rpb.astype(F32).reshape(h * nr, nc), ((0, 0), (0, 32 - nc)))
    full = lambda a: pl.BlockSpec(a.shape, lambda: (0,) * a.ndim)
    expand = jnp.asarray(expand, dtype=F32)
    ok = jnp.asarray(valid, dtype=F32)
    toep = pl.pallas_call(
        _na_bias_body,
        out_shape=jax.ShapeDtypeStruct((h * nr, w * w), F32),
        in_specs=[full(rp), full(expand), full(ok)],
        out_specs=pl.BlockSpec((h * nr, w * w), lambda: (0, 0)),
        name="na_bias",
    )(rp, expand, ok)
    t2 = toep.reshape(h, nr, w, w).transpose(0, 2, 1, 3).reshape(h, w, nr * w)
    slabs = jnp.stack([t2[:, :, (NA_WIN_R - 1 - off) * w:(2 * NA_WIN_R - 1 - off) * w]
                       for off in range(NA_WIN_R)], axis=1)
    hpb = NA_HEADS_PER_BLK
    slabs = slabs.reshape(h // hpb, hpb, NA_WIN_R, w, NA_WIN_R * w).transpose(0, 2, 1, 3, 4)
    return slabs.reshape(h // hpb, NA_WIN_R, hpb * w, NA_WIN_R * w)


def _natt_body(q_ref, k_ref, v_ref, kc_ref, vc_ref, bias_ref, o_ref, *, rows):
    w = GRID_W
    hpb = NA_HEADS_PER_BLK
    nloc = NA_WIN_R * w
    lane = lax.broadcasted_iota(jnp.int32, (1, hpb * NA_HEAD_DIM), 1)
    in_head = [(lane >= NA_HEAD_DIM * hh) & (lane < NA_HEAD_DIM * (hh + 1)) for hh in range(hpb)]
    kcx = kc_ref[0]
    vcx = vc_ref[0]
    nt = (((1,), (1,)), ((), ()))

    def one_row(r):
        r0 = jnp.clip(r - NA_WIN_R // 2, 0, rows - NA_WIN_R)
        off = r - r0
        qs = q_ref[0, pl.ds(pl.multiple_of(r * w, w), w), :]
        kw = k_ref[0, pl.ds(pl.multiple_of(r0 * w, w), nloc), :]
        vw = v_ref[0, pl.ds(pl.multiple_of(r0 * w, w), nloc), :]
        zero = jnp.zeros_like(qs)
        qst = jnp.concatenate([jnp.where(m, qs, zero) for m in in_head], axis=0)
        s_loc = lax.dot_general(qst, kw, nt, preferred_element_type=F32) + bias_ref[0, off]
        s_ctx = lax.dot_general(qst, kcx, nt, preferred_element_type=F32)
        m = jnp.maximum(jnp.max(s_loc, axis=-1, keepdims=True), jnp.max(s_ctx, axis=-1, keepdims=True))
        p_loc = jnp.exp(s_loc - m)
        p_ctx = jnp.exp(s_ctx - m)
        den = jnp.sum(p_loc, axis=-1, keepdims=True) + jnp.sum(p_ctx, axis=-1, keepdims=True)
        o = (_dot(p_loc.astype(BF16), vw) + _dot(p_ctx.astype(BF16), vcx)) * (1.0 / den)
        acc = jnp.where(in_head[0], o[0:w], 0.0)
        for hh in range(1, hpb):
            acc = acc + jnp.where(in_head[hh], o[hh * w:(hh + 1) * w], 0.0)
        o_ref[0, pl.ds(pl.multiple_of(r * w, w), w), :] = acc.astype(BF16)

    def row_group(i, carry):
        for r in range(NA_ROWS_PER_STEP):
            one_row(NA_ROWS_PER_STEP * i + r)
        return carry

    lax.fori_loop(0, rows // NA_ROWS_PER_STEP, row_group, 0)


def _natt(q, k, v, kc, vc, bias):
    b, s, c = q.shape
    nctx = kc.shape[1]
    lw = NA_HEADS_PER_BLK * NA_HEAD_DIM
    rows = s // GRID_W
    seq = pl.BlockSpec((1, s, lw), lambda bi, g: (bi, 0, g))
    cx = pl.BlockSpec((1, nctx, lw), lambda bi, g: (bi, 0, g))
    return pl.pallas_call(
        functools.partial(_natt_body, rows=rows),
        out_shape=jax.ShapeDtypeStruct((b, s, c), BF16),
        grid=(b, c // lw),
        in_specs=[seq, seq, seq, cx, cx,
                  pl.BlockSpec((1,) + bias.shape[1:], lambda bi, g: (g, 0, 0, 0))],
        out_specs=seq,
        compiler_params=_cparams(("parallel", "parallel")),
        name="natt",
    )(q, k, v, kc, vc, bias)


def _hyena_feats(seq_len):
    t = jnp.linspace(0.0, 1.0, seq_len, dtype=F32)[:, None]
    bands = jnp.linspace(1e-4, HYENA_BANDS - 1, HYENA_BANDS, dtype=F32)
    ang = (2.0 * math.pi / seq_len) * jnp.arange(seq_len, dtype=F32)[:, None] * bands[None, :]
    feats = jnp.concatenate([t, jnp.cos(ang), -jnp.sin(ang)], axis=-1)
    return jnp.pad(feats, ((0, 0), (0, 128 - HYENA_EMB)))


def _filt_body(feat_ref, w1_ref, b1_ref, w2_ref, b2_ref, w3_ref, fr_ref, dl_ref, o_ref, l1_ref, h_scr,
               *, halves):
    j = pl.program_id(0)
    hp = functools.partial(jnp.dot, precision=HIGHEST, preferred_element_type=F32)
    feats = feat_ref[...]

    @pl.when(j == 0)
    def _():
        fr = fr_ref[...]
        h = jnp.sin(fr[0:1] * (hp(feats, w1_ref[...]) + b1_ref[...]))
        h_scr[...] = jnp.sin(fr[1:2] * (hp(h, w2_ref[...]) + b2_ref[...]))

    h2 = h_scr[...]
    h_hi = h2.astype(BF16)
    h_lo = (h2 - h_hi.astype(F32)).astype(BF16)
    w3 = w3_ref[...]
    w_hi = w3.astype(BF16)
    w_lo = (w3 - w_hi.astype(F32)).astype(BF16)
    hc = (_dot(jnp.concatenate([h_hi, h_lo], axis=1), jnp.concatenate([w_hi, w_hi], axis=0))
          + _dot(h_hi, w_lo))
    t = feats[:, 0:1]
    hc = hc * jnp.exp(-t * dl_ref[...])
    row = lax.broadcasted_iota(jnp.int32, hc.shape, 0)
    hc = jnp.where((row == 0) & ((j // halves) % 2 == 1), 0.0, hc)
    l1_ref[0] = jnp.sum(jnp.abs(hc), axis=0, keepdims=True)
    o_ref[0] = hc


def _hyena_filter_taps(seq_len, f_w1, f_b1, f_w2, f_b2, f_w3, f_freq, c_hy):
    feats = _hyena_feats(seq_len)
    hid = f_w1.shape[1]
    w1 = jnp.pad(f_w1.astype(F32), ((0, 128 - HYENA_EMB), (0, 0)))
    deltas = jnp.abs(jnp.linspace(math.log(HYENA_TARGET) / HYENA_SLOW_DECAY,
                                  math.log(HYENA_TARGET) / HYENA_FAST_DECAY, c_hy, dtype=F32))[None, :]
    nblk = f_w3.shape[1] // c_hy
    halves = 2
    cb = c_hy // halves
    c0 = lambda a: pl.BlockSpec(a.shape, lambda j: (0, 0))
    b1, b2 = f_b1.reshape(1, hid), f_b2.reshape(1, hid)
    return pl.pallas_call(
        functools.partial(_filt_body, halves=halves),
        out_shape=[jax.ShapeDtypeStruct((nblk, seq_len, c_hy), F32),
                   jax.ShapeDtypeStruct((nblk, 1, c_hy), F32)],
        grid=(nblk * halves,),
        in_specs=[c0(feats), c0(w1), c0(b1), c0(f_w2), c0(b2),
                  pl.BlockSpec((hid, cb), lambda j: (0, j)), c0(f_freq),
                  pl.BlockSpec((1, cb), lambda j: (0, j % halves))],
        out_specs=[pl.BlockSpec((1, seq_len, cb), lambda j: (j // halves, 0, j % halves)),
                   pl.BlockSpec((1, 1, cb), lambda j: (j // halves, 0, j % halves))],
        scratch_shapes=[pltpu.VMEM((seq_len, hid), F32)],
        compiler_params=_cparams(("arbitrary",)),
        name="hyena_filter",
    )(feats, w1, b1, f_w2, b2, f_w3, f_freq, deltas)


def _conv_dft_constants():
    a_half = FFT_A // 2
    n = FFT_A * FFT_R
    ka = np.arange(FFT_KA)[:, None]
    a = np.arange(a_half)[None, :]
    ph = 2.0 * np.pi * ka * a / FFT_A
    m_fwd = np.zeros((2 * FFT_KA_PAD, a_half))
    m_fwd[:FFT_KA] = np.cos(ph)
    m_fwd[FFT_KA_PAD:FFT_KA_PAD + FFT_KA] = -np.sin(ph)
    wgt = np.where((ka == 0) | (ka == FFT_A // 2), 1.0, 2.0)
    m_inv = np.zeros((a_half, 2 * FFT_KA_PAD))
    m_inv[:, :FFT_KA] = (wgt * np.cos(ph)).T / n
    m_inv[:, FFT_KA_PAD:FFT_KA_PAD + FFT_KA] = (-wgt * np.sin(ph)).T / n
    kb = np.arange(FFT_R)[None, :, None]
    b = np.arange(FFT_R)[None, None, :]
    kaa = np.arange(FFT_KA)[:, None, None]
    th = 2.0 * np.pi * (b * kb / FFT_R + b * kaa / n)
    gr, gi = np.cos(th), -np.sin(th)
    g2 = np.zeros((FFT_KA_PAD, 2 * FFT_R, 2 * FFT_R))
    g2[:FFT_KA] = np.block([[gr, -gi], [gi, gr]])
    grt, git = gr.transpose(0, 2, 1), gi.transpose(0, 2, 1)
    g2h = np.zeros_like(g2)
    g2h[:FFT_KA] = np.block([[grt, git], [-git, grt]])
    return _mxu_const(m_fwd), _mxu_const(m_inv), _mxu_const(g2), _mxu_const(g2h)


FFT_NB = 16


def _fwd1_body(m_ref, u_ref, o_ref):
    u = jnp.concatenate([u_ref[0, :, bb, :] for bb in range(FFT_NB)], axis=1).astype(BF16)
    res = _dot(m_ref[...], u)
    o_ref[0, 0] = res[0:FFT_KA_PAD]
    o_ref[0, 1] = res[FFT_KA_PAD:2 * FFT_KA_PAD]


def _conv_fwd1(u, m_fwd):
    n, seq, c = u.shape
    a_half = FFT_A // 2
    return pl.pallas_call(
        _fwd1_body,
        out_shape=jax.ShapeDtypeStruct((n, 2, FFT_KA_PAD, FFT_R * c), F32),
        grid=(n, FFT_R // FFT_NB),
        in_specs=[pl.BlockSpec(m_fwd.shape, lambda i, j: (0, 0)),
                  pl.BlockSpec((1, a_half, FFT_NB, c), lambda i, j: (i, 0, j, 0))],
        out_specs=pl.BlockSpec((1, 2, FFT_KA_PAD, FFT_NB * c), lambda i, j: (i, 0, 0, j)),
        compiler_params=_cparams(("parallel", "parallel")),
        name="conv_fwd1",
    )(m_fwd, u.reshape(n, a_half, FFT_R, c))


FFT_KB = 8


def _rows_to_slabs(src_ref, dst_scr, c):
    for part in range(2):
        for b in range(FFT_R):
            dst_scr[part, :, b, :] = src_ref[0, part, :, b * c:(b + 1) * c]


def _slabs_to_rows(src_scr, dst_ref, c):
    for part in range(2):
        for b in range(FFT_R):
            dst_ref[0, part, :, b * c:(b + 1) * c] = src_scr[part, :, b, :]


def _slab(scr, i):
    return jnp.concatenate([scr[0, i], scr[1, i]], axis=0).astype(BF16)


def _per_ka_block(j, work, clear):
    full_blocks = FFT_KA // FFT_KB
    tail = FFT_KA - full_blocks * FFT_KB

    @pl.when(j < full_blocks)
    def _():
        for i in range(FFT_KB):
            work(i)

    @pl.when(j >= full_blocks)
    def _():
        for i in range(FFT_KB):
            (work if i < tail else clear)(i)


def _fwd2f_body(sf_ref, sb_ref, g_ref, l1_ref, kf_ref, f3, b3):
    o = pl.program_id(0)
    j = pl.program_id(1)
    r2 = 2 * FFT_R
    c = kf_ref.shape[-1]
    _rows_to_slabs(sf_ref, f3, c)
    _rows_to_slabs(sb_ref, b3, c)
    inv = 1.0 / (l1_ref[2 * o] + l1_ref[2 * o + 1] + EPS)

    def spectrum(i):
        xf = _dot(g_ref[i], _slab(f3, i))
        xb = _dot(g_ref[i], _slab(b3, i))
        kf_ref[0, i, 0:FFT_R] = (xf[0:FFT_R] + xb[0:FFT_R]) * inv
        kf_ref[0, i, FFT_R:r2] = (xf[FFT_R:r2] - xb[FFT_R:r2]) * inv

    def clear(i):
        kf_ref[0, i] = jnp.zeros((r2, c), F32)

    _per_ka_block(j, spectrum, clear)


def _filter_spectrum(s_filt, l1, g2, c):
    n_ord = s_filt.shape[0] // 2
    cols = s_filt.shape[-1]
    r2 = 2 * FFT_R
    return pl.pallas_call(
        _fwd2f_body,
        out_shape=jax.ShapeDtypeStruct((n_ord, FFT_KA_PAD, r2, c), F32),
        grid=(n_ord, FFT_KA_PAD // FFT_KB),
        in_specs=[pl.BlockSpec((1, 2, FFT_KB, cols), lambda o, j: (2 * o, 0, j, 0)),
                  pl.BlockSpec((1, 2, FFT_KB, cols), lambda o, j: (2 * o + 1, 0, j, 0)),
                  pl.BlockSpec((FFT_KB, r2, r2), lambda o, j: (j, 0, 0)),
                  pl.BlockSpec(l1.shape, lambda o, j: (0, 0, 0))],
        out_specs=pl.BlockSpec((1, FFT_KB, r2, c), lambda o, j: (o, j, 0, 0)),
        scratch_shapes=[pltpu.VMEM((2, FFT_KB, FFT_R, c), F32)] * 2,
        compiler_params=_cparams(("parallel", "parallel")),
        name="filter_spectrum",
    )(s_filt, s_filt, g2, l1)


def _mid_body(s_ref, g_ref, gh_ref, kf_ref, t_ref, s3, t3):
    j = pl.program_id(1)
    r2 = 2 * FFT_R
    c = kf_ref.shape[-1]
    _rows_to_slabs(s_ref, s3, c)

    def convolve(i):
        x = _dot(g_ref[i], _slab(s3, i))
        xr, xi = x[0:FFT_R], x[FFT_R:r2]
        kr, ki = kf_ref[0, i, 0:FFT_R], kf_ref[0, i, FFT_R:r2]
        y = jnp.concatenate([xr * kr - xi * ki, xr * ki + xi * kr], axis=0).astype(BF16)
        t = _dot(gh_ref[i], y)
        t3[0, i] = t[0:FFT_R]
        t3[1, i] = t[FFT_R:r2]

    def clear(i):
        t3[0, i] = jnp.zeros((FFT_R, c), F32)
        t3[1, i] = jnp.zeros((FFT_R, c), F32)

    _per_ka_block(j, convolve, clear)
    _slabs_to_rows(t3, t_ref, c)


def _conv_mid(s, kf, order, g2, g2h, c):
    n, _, _, cols = s.shape
    r2 = 2 * FFT_R
    blk = pl.BlockSpec((1, 2, FFT_KB, cols), lambda i, j: (i, 0, j, 0))
    gspec = pl.BlockSpec((FFT_KB, r2, r2), lambda i, j: (j, 0, 0))
    return pl.pallas_call(
        _mid_body,
        out_shape=jax.ShapeDtypeStruct(s.shape, F32),
        grid=(n, FFT_KA_PAD // FFT_KB),
        in_specs=[blk, gspec, gspec,
                  pl.BlockSpec((1, FFT_KB, r2, c), lambda i, j: (order, j, 0, 0))],
        out_specs=blk,
        scratch_shapes=[pltpu.VMEM((2, FFT_KB, FFT_R, c), F32)] * 2,
        compiler_params=_cparams(("parallel", "parallel")),
        name="conv_mid",
    )(s, g2, g2h, kf)


def _inv1_body(m_ref, t_ref, u_ref, xg_ref, sk_ref, o_ref):
    c = u_ref.shape[-1]
    t2 = t_ref[0].reshape(2 * FFT_KA_PAD, FFT_NB * c).astype(BF16)
    y = _dot(m_ref[...], t2)
    for bb in range(FFT_NB):
        conv = y[:, bb * c:(bb + 1) * c] + u_ref[0, :, bb, :] * sk_ref[...]
        o_ref[0, :, bb, :] = xg_ref[0, :, bb, :] * conv


def _conv_inv1(t, u, xg, skip, m_inv):
    n, seq, c = u.shape
    a_half = FFT_A // 2
    sk = skip.astype(F32).reshape(1, c)
    uspec = pl.BlockSpec((1, a_half, FFT_NB, c), lambda i, j: (i, 0, j, 0))
    view = lambda a: a.reshape(n, a_half, FFT_R, c)
    out = pl.pallas_call(
        _inv1_body,
        out_shape=jax.ShapeDtypeStruct((n, a_half, FFT_R, c), F32),
        grid=(n, FFT_R // FFT_NB),
        in_specs=[pl.BlockSpec(m_inv.shape, lambda i, j: (0, 0)),
                  pl.BlockSpec((1, 2, FFT_KA_PAD, FFT_NB * c), lambda i, j: (i, 0, 0, j)),
                  uspec, uspec,
                  pl.BlockSpec((1, c), lambda i, j: (0, 0))],
        out_specs=uspec,
        compiler_params=_cparams(("parallel", "parallel")),
        name="conv_inv1",
    )(m_inv, t, view(u), view(xg), sk)
    return out.reshape(n, seq, c)


def _hyena(v, x1, x2, f_w1, f_b1, f_w2, f_b2, f_w3, f_freq, skip):
    _, seq, c = v.shape
    assert 2 * seq == FFT_A * FFT_R
    m_fwd, m_inv, g2, g2h = _conv_dft_constants()
    taps, l1 = _hyena_filter_taps(seq, f_w1, f_b1, f_w2, f_b2, f_w3, f_freq, c)
    kf = _filter_spectrum(_conv_fwd1(taps, m_fwd), l1, g2, c)
    y = v
    for order, xg in enumerate((x1, x2)):
        t = _conv_mid(_conv_fwd1(y, m_fwd), kf, order, g2, g2h, c)
        y = _conv_inv1(t, y, xg, skip[order], m_inv)
    return y


def _mix_ffn_body(x_ref, a1_ref, a2_ref, wm_ref, gm_ref, g_ref, sh_ref, sc_ref, gt_ref,
                  w1_ref, w3_ref, w2_ref, o_ref, *, fb):
    c1 = a1_ref.shape[-1]
    mixed = _dot(a1_ref[0].astype(BF16), wm_ref[0:c1]) + _dot(a2_ref[0].astype(BF16), wm_ref[c1:])
    xm = x_ref[0] + gm_ref[0] * mixed
    h = _norm_mod(xm, g_ref[...], sh_ref[0], sc_ref[0]).astype(BF16)
    acc = None
    for lo in range(0, w1_ref.shape[1], fb):
        a = _dot(h, w1_ref[:, lo:lo + fb])
        u = (a * jax.nn.sigmoid(a) * _dot(h, w3_ref[:, lo:lo + fb])).astype(BF16)
        part = _dot(u, w2_ref[lo:lo + fb, :])
        acc = part if acc is None else acc + part
    o_ref[0] = xm + gt_ref[0] * acc


def _mix_ffn(x, a1, a2, w_mix, gate_mix, g, shift, scale, gate, w1, w3, w2, tm=512, fb=1408):
    b, s, d = x.shape
    tok = lambda c: pl.BlockSpec((1, tm, c), lambda bi, i: (bi, i, 0))
    per_b = pl.BlockSpec((1, 1, d), lambda bi, i: (bi, 0, 0))
    const = lambda a: pl.BlockSpec(a.shape, lambda bi, i: (0, 0))
    resident = lambda a: pl.BlockSpec(a.shape, lambda bi, i: (0, 0), pipeline_mode=pl.Buffered(1))
    return pl.pallas_call(
        functools.partial(_mix_ffn_body, fb=fb),
        out_shape=jax.ShapeDtypeStruct(x.shape, F32),
        grid=(b, s // tm),
        in_specs=[tok(d), tok(a1.shape[-1]), tok(a2.shape[-1]), resident(w_mix), per_b,
                  const(g), per_b, per_b, per_b, resident(w1), resident(w3), resident(w2)],
        out_specs=tok(d),
        compiler_params=_cparams(("parallel", "parallel"), vmem_limit=FFN_VMEM_LIMIT),
        name="mix_ffn",
    )(x, a1, a2, w_mix, gate_mix, g, shift, scale, gate, w1, w3, w2)


def _fm_constants(cg):
    j = np.arange(cg)[:, None]
    m = np.arange(cg)[None, :]
    ph = 2.0 * np.pi * j * m / cg
    w_cs = np.concatenate([np.cos(ph), np.sin(ph)], axis=1)
    d = np.arange(FM_A)[:, None]
    a = np.arange(FM_A)[None, :]
    ph = 2.0 * np.pi * d * a / FM_A
    fr, fi = np.cos(ph), -np.sin(ph)
    m1 = np.block([[fr, fi], [fi, -fr]])
    n = FM_A * FM_A
    dd = np.arange(FM_A)[:, None, None]
    c = np.arange(FM_A)[None, :, None]
    b = np.arange(FM_A)[None, None, :]
    th = 2.0 * np.pi * (b * c / FM_A + b * dd / n)
    gcat = np.concatenate([np.cos(th), np.sin(th)], axis=2)
    return _mxu_const(w_cs), _mxu_const(m1), _mxu_const(gcat)


def _fm_front_body(x_ref, g_ref, sh_ref, sc_ref, w_ref, m_ref, o_ref, *, cg, nb):
    d = x_ref.shape[-1]
    xs = jnp.concatenate([x_ref[0, :, bb, :] for bb in range(nb)], axis=0)
    h = _norm_mod(xs, g_ref[...], sh_ref[0], sc_ref[0]).astype(BF16)
    pq = [_dot(h[:, grp * cg:(grp + 1) * cg], w_ref[...]) for grp in range(d // cg)]
    p = jnp.concatenate([t[:, 0:cg] for t in pq], axis=1)
    q = jnp.concatenate([t[:, cg:2 * cg] for t in pq], axis=1)
    for bb in range(nb):
        rows = slice(bb * FM_A, (bb + 1) * FM_A)
        res = _dot(m_ref[...], jnp.concatenate([p[rows], q[rows]], axis=0).astype(BF16))
        o_ref[0, 0, :, bb, :] = res[0:FM_A]
        o_ref[0, 1, :, bb, :] = res[FM_A:2 * FM_A]


def _fm_front(x, g, shift, scale, w_cs, m1, nb=8):
    b, s, d = x.shape
    cg = w_cs.shape[0]
    per_b = pl.BlockSpec((1, 1, d), lambda bi, j: (bi, 0, 0))
    const = lambda a: pl.BlockSpec(a.shape, lambda bi, j: (0, 0))
    return pl.pallas_call(
        functools.partial(_fm_front_body, cg=cg, nb=nb),
        out_shape=jax.ShapeDtypeStruct((b, 2, FM_A, s // FM_A, d), F32),
        grid=(b, s // FM_A // nb),
        in_specs=[pl.BlockSpec((1, FM_A, nb, d), lambda bi, j: (bi, 0, j, 0)),
                  const(g), per_b, per_b, const(w_cs), const(m1)],
        out_specs=pl.BlockSpec((1, 2, FM_A, nb, d), lambda bi, j: (bi, 0, 0, j, 0)),
        compiler_params=_cparams(("parallel", "parallel")),
        name="fm_front",
    )(x.reshape(b, FM_A, s // FM_A, d), g, shift, scale, w_cs, m1)


def _fm_s2_body(s_ref, g_ref, o_ref, *, dblk, scale):
    for i in range(dblk):
        s2 = jnp.concatenate([s_ref[0, 0, i], s_ref[0, 1, i]], axis=0).astype(BF16)
        o_ref[:, i, :] = _dot(g_ref[i], s2) * scale


def _fm_stage2(sv, gcat, seq, d, dblk=8):
    b = sv.shape[0]
    scale = 1.0 / math.sqrt(seq * (d // F_GROUPS))
    out = pl.pallas_call(
        functools.partial(_fm_s2_body, dblk=dblk, scale=scale),
        out_shape=jax.ShapeDtypeStruct((b * FM_A, dblk * (FM_A // dblk), d), F32),
        grid=(b, FM_A // dblk),
        in_specs=[pl.BlockSpec((1, 2, dblk, FM_A, d), lambda bi, j: (bi, 0, j, 0, 0)),
                  pl.BlockSpec((dblk, FM_A, 2 * FM_A), lambda bi, j: (j, 0, 0))],
        out_specs=pl.BlockSpec((FM_A, dblk, d), lambda bi, j: (bi, j, 0)),
        compiler_params=_cparams(("parallel", "parallel")),
        name="fm_stage2",
    )(sv, gcat)
    return out.reshape(b, seq, d)


def _fourier_mix(x, g, shift, scale):
    b, s, d = x.shape
    assert s == FM_A * FM_A
    w_cs, m1, gcat = _fm_constants(d // F_GROUPS)
    return _fm_stage2(_fm_front(x, g, shift, scale, w_cs, m1), gcat, s, d)


LANES = 128
ROW_SL = 8
MOE_TM = 1024
DMA_WINDOW = 128


def _router_body(x_ref, yf_ref, wf_ref, gf_ref, g_ref, sh_ref, sc_ref, wr_ref, br_ref,
                 xo_ref, h_ref, meta_ref, gw_ref, cnt_ref, carry):
    i = pl.program_id(0)

    @pl.when(i == 0)
    def _():
        carry[...] = jnp.zeros_like(carry)

    xm = x_ref[...] + gf_ref[0] * _dot(yf_ref[...].astype(BF16), wf_ref[...])
    xo_ref[...] = xm
    h = _norm_mod(xm, g_ref[...], sh_ref[0], sc_ref[0])
    _rows_to_tiles(h_ref, h)
    h_hi = h.astype(BF16)
    h_lo = (h - h_hi.astype(F32)).astype(BF16)
    by_hi = _dot(h_hi, wr_ref[...])
    logits = by_hi[:, 0:LANES] + by_hi[:, LANES:] + _dot(h_lo, wr_ref[:, 0:LANES]) + br_ref[...]
    lane = lax.broadcasted_iota(jnp.int32, logits.shape, 1)
    nl = logits.shape[-1]
    m1 = jnp.max(logits, axis=-1, keepdims=True)
    i1 = jnp.min(jnp.where(logits == m1, lane, nl), axis=-1, keepdims=True)
    rest = jnp.where(lane == i1, -3.0e38, logits)
    m2 = jnp.max(rest, axis=-1, keepdims=True)
    i2 = jnp.min(jnp.where(rest == m2, lane, nl), axis=-1, keepdims=True)
    e = jnp.exp(m2 - m1)
    gw_ref[...] = jnp.where(lane == 0, 1.0 / (1.0 + e), jnp.where(lane == 1, e / (1.0 + e), 0.0))
    onehot = jnp.where((lane == i1) | (lane == i2), 1.0, 0.0)
    tm = onehot.shape[0]
    earlier = lax.broadcasted_iota(jnp.int32, (tm, tm), 0) > lax.broadcasted_iota(jnp.int32, (tm, tm), 1)
    excl = _dot(jnp.where(earlier, 1.0, 0.0).astype(BF16), onehot.astype(BF16)) + carry[...]
    r1 = jnp.sum(jnp.where(lane == i1, excl, 0.0), axis=-1, keepdims=True).astype(jnp.int32)
    r2 = jnp.sum(jnp.where(lane == i2, excl, 0.0), axis=-1, keepdims=True).astype(jnp.int32)
    meta_ref[...] = jnp.where(lane == 0, i1, jnp.where(lane == 1, i2, jnp.where(lane == 2, r1, jnp.where(lane == 3, r2, 0))))
    carry[...] = carry[...] + jnp.sum(onehot, axis=0, keepdims=True)
    cnt_ref[...] = carry[...]


def _router(x, y_f, w_f, gate_f, g, shift, scale, w_router, b_router, tm=512):
    b, s, d = x.shape
    t = b * s
    ne = w_router.shape[1]
    wr = jnp.pad(w_router.astype(F32), ((0, 0), (0, LANES - ne)))
    wr_hi = wr.astype(BF16)
    wr = jnp.concatenate([wr_hi, (wr - wr_hi.astype(F32)).astype(BF16)], axis=1)
    br = jnp.pad(b_router.astype(F32).reshape(1, ne), ((0, 0), (0, LANES - ne)), constant_values=NEG_INF)
    spt = s // tm
    per_b = pl.BlockSpec((1, 1, d), lambda i: (i // spt, 0, 0))
    const = lambda a: pl.BlockSpec(a.shape, lambda i: (0, 0))
    tok = pl.BlockSpec((tm, d), lambda i: (i, 0))
    return pl.pallas_call(
        _router_body,
        out_shape=[jax.ShapeDtypeStruct((t, d), F32),
                   jax.ShapeDtypeStruct((t * ROW_SL, LANES), F32),
                   jax.ShapeDtypeStruct((t, LANES), jnp.int32),
                   jax.ShapeDtypeStruct((t, LANES), F32),
                   jax.ShapeDtypeStruct((1, LANES), F32)],
        grid=(t // tm,),
        in_specs=[tok, tok, const(w_f), per_b, const(g), per_b, per_b, const(wr), const(br)],
        out_specs=[tok,
                   pl.BlockSpec((tm * ROW_SL, LANES), lambda i: (i, 0)),
                   pl.BlockSpec((tm, LANES), lambda i: (i, 0)),
                   pl.BlockSpec((tm, LANES), lambda i: (i, 0)),
                   pl.BlockSpec((1, LANES), lambda i: (0, 0))],
        scratch_shapes=[pltpu.VMEM((1, LANES), F32)],
        compiler_params=_cparams(("arbitrary",)),
        name="router",
    )(x.reshape(t, d), y_f.reshape(t, d), w_f, gate_f, g, shift, scale, wr, br)


def _moe_plan(meta, counts, ne, tm):
    i1, i2, r1, r2 = meta[:, 0], meta[:, 1], meta[:, 2], meta[:, 3]
    cnt = counts[0, :ne].astype(jnp.int32)
    padded = ((cnt + tm - 1) // tm) * tm
    ends = jnp.cumsum(padded)
    offs = ends - padded
    pick = lambda idx: sum(jnp.where(idx == e, offs[e], 0) for e in range(ne))
    pos = jnp.concatenate([pick(i1) + r1, pick(i2) + r2]).astype(jnp.int32)
    n_tiles = (2 * meta.shape[0]) // tm + ne
    n_used = (ends[ne - 1] // tm).astype(jnp.int32)
    tile_start = jnp.minimum(jnp.arange(n_tiles, dtype=jnp.int32), n_used - 1) * tm
    tile_expert = jnp.sum(tile_start[:, None] >= ends[None, :], axis=1).astype(jnp.int32)
    group_end = sum(jnp.where(tile_expert == e, offs[e] + cnt[e], 0) for e in range(ne))
    tile_rows = jnp.clip(group_end - tile_start, 0, tm).astype(jnp.int32)
    return pos, offs + cnt, padded - cnt, tile_expert, n_used.reshape(1), tile_rows


def _windowed_copies(n, start_copy, wait_one, per_iter):
    def body(i, carry):
        @pl.when(i >= DMA_WINDOW)
        def _():
            for _ in range(per_iter):
                wait_one()
        start_copy(i)
        return carry

    lax.fori_loop(0, n, body, 0)

    def drain(i, carry):
        for _ in range(per_iter):
            wait_one()
        return carry

    lax.fori_loop(0, jnp.minimum(n, DMA_WINDOW), drain, 0)


def _tile_of(ref, row):
    return ref.at[pl.ds(pl.multiple_of(row * ROW_SL, ROW_SL), ROW_SL)]


def _tiles_to_rows(ref, n, first=0):
    return jnp.concatenate([ref[pl.ds(first * ROW_SL + sl, n, stride=ROW_SL), :] for sl in range(ROW_SL)], axis=1)


def _rows_to_tiles(ref, val):
    n = val.shape[0]
    for sl in range(ROW_SL):
        ref[pl.ds(sl, n, stride=ROW_SL), :] = val[:, sl * LANES:(sl + 1) * LANES]


def _dispatch_body(pos_ref, pad_start_ref, pad_n_ref, h_ref, xs_hbm, sem, *, n_tok, ne):
    i = pl.program_id(0)
    td = h_ref.shape[0] // ROW_SL
    base = i * td
    copy = lambda src, dst: pltpu.make_async_copy(_tile_of(h_ref, src), _tile_of(xs_hbm, dst), sem)
    wait_one = lambda: copy(0, 0).wait()

    def start_token(r, carry):
        copy(r, pos_ref[base + r]).start(priority=0)
        copy(r, pos_ref[n_tok + base + r]).start(priority=1)
        return carry

    lax.fori_loop(0, td, start_token, 0, unroll=8)
    whole_tile = pltpu.make_async_copy(h_ref, xs_hbm.at[pl.ds(0, td * ROW_SL)], sem)
    whole_tile.wait()
    whole_tile.wait()

    @pl.when(i == 0)
    def _():
        for e in range(ne):
            first = pad_start_ref[e]
            _windowed_copies(pad_n_ref[e], lambda r: copy(0, first + r).start(), wait_one, 1)


def _moe_dispatch(h3, pos, pad_start, pad_n, n_rows, td=1024):
    n_tok = h3.shape[0] // ROW_SL
    ne = pad_start.shape[0]
    return pl.pallas_call(
        functools.partial(_dispatch_body, n_tok=n_tok, ne=ne),
        out_shape=jax.ShapeDtypeStruct((n_rows * ROW_SL, LANES), h3.dtype),
        grid_spec=pltpu.PrefetchScalarGridSpec(
            num_scalar_prefetch=3, grid=(n_tok // td,),
            in_specs=[pl.BlockSpec((td * ROW_SL, LANES), lambda i, p, ps, pn: (i, 0))],
            out_specs=pl.BlockSpec(memory_space=pl.ANY),
            scratch_shapes=[pltpu.SemaphoreType.DMA(())]),
        compiler_params=_cparams(("arbitrary",)),
        name="moe_dispatch",
    )(pos, pad_start, pad_n, h3)


def _moe_grouped_body(te_ref, nu_ref, tr_ref, xs_ref, w1_ref, w3_ref, w2_ref, y_ref, xb_scr, acc_scr):
    i = pl.program_id(0)
    j = pl.program_id(1)
    tm = xb_scr.shape[0]
    hm = tm // 2

    def expert_rows(nrows):
        h = xb_scr[0:nrows]
        a = _dot(h, w1_ref[0].astype(BF16))
        u = (a * jax.nn.sigmoid(a) * _dot(h, w3_ref[0].astype(BF16))).astype(BF16)
        part = _dot(u, w2_ref[0].astype(BF16))

        @pl.when(j == 0)
        def _():
            acc_scr[0:nrows] = part

        @pl.when(j > 0)
        def _():
            acc_scr[0:nrows] += part

    @pl.when(i < nu_ref[0])
    def _():
        @pl.when(j == 0)
        def _():
            xb_scr[...] = _tiles_to_rows(xs_ref, tm).astype(BF16)

        @pl.when(tr_ref[i] > hm)
        def _():
            expert_rows(tm)

        @pl.when(tr_ref[i] <= hm)
        def _():
            expert_rows(hm)

            @pl.when(j == 0)
            def _():
                acc_scr[hm:tm] = jnp.zeros((tm - hm, acc_scr.shape[1]), F32)

        @pl.when(j == pl.num_programs(1) - 1)
        def _():
            _rows_to_tiles(y_ref, acc_scr[...])


def _moe_grouped(xs, tile_expert, n_used, tile_rows, w1, w3, w2, tm, fb=512):
    ne, d, f = w1.shape
    n_rows = xs.shape[0] // ROW_SL
    nj = f // fb
    row_tile = lambda i, j, te, nu, tr: (jnp.maximum(jnp.minimum(i, nu[0] - 1), 0), 0)
    jj = lambda i, j, nu: jnp.where(i < nu[0], j, nj - 1)
    return pl.pallas_call(
        _moe_grouped_body,
        out_shape=jax.ShapeDtypeStruct(xs.shape, F32),
        grid_spec=pltpu.PrefetchScalarGridSpec(
            num_scalar_prefetch=3, grid=(n_rows // tm, nj),
            in_specs=[pl.BlockSpec((tm * ROW_SL, LANES), row_tile),
                      pl.BlockSpec((1, d, fb), lambda i, j, te, nu, tr: (te[i], 0, jj(i, j, nu))),
                      pl.BlockSpec((1, d, fb), lambda i, j, te, nu, tr: (te[i], 0, jj(i, j, nu))),
                      pl.BlockSpec((1, fb, d), lambda i, j, te, nu, tr: (te[i], jj(i, j, nu), 0))],
            out_specs=pl.BlockSpec((tm * ROW_SL, LANES), row_tile),
            scratch_shapes=[pltpu.VMEM((tm, d), BF16), pltpu.VMEM((tm, d), F32)]),
        compiler_params=_cparams(("arbitrary", "arbitrary")),
        name="moe_grouped",
    )(tile_expert, n_used, tile_rows, xs, w1, w3, w2)


def _moe_final_body(pos_ref, x_ref, y_hbm, gw_ref, gt_ref, fg_ref, o_ref, yg_scr, sem, *, n_tok):
    i = pl.program_id(0)
    tc = x_ref.shape[0]
    slot = i % 2

    def gather_tile(step, into):
        base = step * tc

        def start_token(r, carry):
            dst = yg_scr.at[into]
            pltpu.make_async_copy(_tile_of(y_hbm, pos_ref[base + r]), _tile_of(dst, r),
                                  sem.at[into]).start(priority=0)
            pltpu.make_async_copy(_tile_of(y_hbm, pos_ref[n_tok + base + r]), _tile_of(dst, tc + r),
                                  sem.at[into]).start(priority=1)
            return carry

        lax.fori_loop(0, tc, start_token, 0, unroll=8)

    @pl.when(i == 0)
    def _():
        gather_tile(0, 0)

    @pl.when(i + 1 < pl.num_programs(0))
    def _():
        gather_tile(i + 1, 1 - slot)

    pltpu.make_async_copy(y_hbm.at[pl.ds(0, 2 * tc * ROW_SL)], yg_scr.at[slot], sem.at[slot]).wait()
    gw = gw_ref[...]
    rows = yg_scr.at[slot]
    y = gw[:, 0:1] * _tiles_to_rows(rows, tc) + gw[:, 1:2] * _tiles_to_rows(rows, tc, first=tc)
    xo = x_ref[...] + gt_ref[0] * y
    ms = jnp.mean(xo * xo, axis=-1, keepdims=True)
    o_ref[...] = xo * lax.rsqrt(ms + EPS) * fg_ref[...]


def _moe_final(x, y, pos, gw, gt, final_g, tc=512):
    b, s, d = x.shape
    t = b * s
    spt = s // tc
    nsl = d // LANES
    out = pl.pallas_call(
        functools.partial(_moe_final_body, n_tok=t),
        out_shape=jax.ShapeDtypeStruct((t, d), F32),
        grid_spec=pltpu.PrefetchScalarGridSpec(
            num_scalar_prefetch=1, grid=(t // tc,),
            in_specs=[pl.BlockSpec((tc, d), lambda i, p: (i, 0)),
                      pl.BlockSpec(memory_space=pl.ANY),
                      pl.BlockSpec((tc, LANES), lambda i, p: (i, 0)),
                      pl.BlockSpec((1, 1, d), lambda i, p: (i // spt, 0, 0)),
                      pl.BlockSpec(final_g.shape, lambda i, p: (0, 0))],
            out_specs=pl.BlockSpec((tc, d), lambda i, p: (i, 0)),
            scratch_shapes=[pltpu.VMEM((2, 2 * tc * ROW_SL, LANES), F32), pltpu.SemaphoreType.DMA((2,))]),
        compiler_params=_cparams(("arbitrary",)),
        name="moe_final",
    )(pos, x.reshape(t, d), y, gw, gt, final_g)
    return out.reshape(b, s, d)


def _moe_routed(x, y_f, w_f, gate_f, g, shift, scale, gt, final_g, w_router, b_router, w1, w3, w2):
    ne = w1.shape[0]
    tm = MOE_TM
    x1, h3, meta, gw, counts = _router(x, y_f, w_f, gate_f, g, shift, scale, w_router, b_router)
    pos, pad_start, pad_n, tile_expert, n_used, tile_rows = _moe_plan(meta, counts, ne, tm)
    assert x.shape[-1] == ROW_SL * LANES
    n_rows = (2 * (h3.shape[0] // ROW_SL) // tm + ne) * tm
    xs = _moe_dispatch(h3, pos, pad_start, pad_n, n_rows)
    y = _moe_grouped(xs, tile_expert, n_used, tile_rows, w1, w3, w2, tm)
    return _moe_final(x1.reshape(x.shape), y, pos, gw, gt, final_g)


def kernel(x, c, ctx, c_ctx, w_ada, b_ada, norm_g, w_in, hy_short_w, hy_short_b, hy_f_w1, hy_f_b1, hy_f_w2, hy_f_b2, hy_f_w3, hy_f_freq, hy_skip, na_rpb, w_mix_out, ffn_w1, ffn_w3, ffn_w2, w_fourier, w_router, b_router, moe_w1, moe_w3, moe_w2, final_g):
    b, s, d = x.shape
    depth = w_ada.shape[0]
    assert depth == 2, "layer 0 mixes with Hyena/attention, layer 1 with Fourier/MoE"
    c_hy = hy_skip.shape[-1]
    c_na = d - c_hy

    cvec = jnp.concatenate([c, c_ctx[None, :], jnp.zeros((8 - b - 1, d), F32)], axis=0)
    mods = _ada(cvec, w_ada, b_ada)

    def mod(layer, idx, ctx_row=False):
        m = mods[layer, :, idx * d:(idx + 1) * d]
        return m[b:b + 1, None, :] if ctx_row else m[0:b, None, :]

    row = lambda a: a.reshape(1, -1)

    w_in0 = w_in[0].astype(BF16)
    w_hy, w_qkv = w_in0[:, 0:3 * c_hy], w_in0[:, 3 * c_hy:]
    v, x1, x2, q, k, va = _inproj(x, row(norm_g[0, 0]), mod(0, 0), mod(0, 1), w_hy, w_qkv,
                                  hy_short_w[0], row(hy_short_b[0]))
    kc, vc = _ctxkv(ctx, row(norm_g[0, 0]), mod(0, 0, True), mod(0, 1, True), w_qkv[:, c_na:])
    y_na = _natt(q, k, va, kc, vc, _na_bias_table(na_rpb[0]))
    y_hy = _hyena(v, x1, x2, hy_f_w1[0], hy_f_b1[0], hy_f_w2[0], hy_f_b2[0], hy_f_w3[0],
                  hy_f_freq[0], hy_skip[0])
    x = _mix_ffn(x, y_hy, y_na, w_mix_out[0].astype(BF16), mod(0, 2),
                 row(norm_g[0, 1]), mod(0, 3), mod(0, 4), mod(0, 5),
                 ffn_w1[0].astype(BF16), ffn_w3[0].astype(BF16), ffn_w2[0].astype(BF16))

    y_f = _fourier_mix(x, row(norm_g[1, 0]), mod(1, 0), mod(1, 1))
    return _moe_routed(x, y_f, w_fourier[0].astype(BF16), mod(1, 2),
                       row(norm_g[1, 1]), mod(1, 3), mod(1, 4), mod(1, 5), row(final_g),
                       w_router[0], b_router[0],
                       moe_w1[0], moe_w3[0], moe_w2[0])
```

```python
import functools
import math

import numpy as np
import jax
import jax.numpy as jnp
from jax import lax
from jax.experimental import pallas as pl
from jax.experimental.pallas import tpu as pltpu

F32 = jnp.float32
BF16 = jnp.bfloat16
HIGHEST = lax.Precision.HIGHEST

GRID_W = 64
NA_HEAD_DIM = 32
NA_WIN_R = 8
NA_WIN_C = 16
HYENA_EMB = 33
HYENA_BANDS = (HYENA_EMB - 1) // 2
HYENA_FAST_DECAY = 0.3
HYENA_SLOW_DECAY = 1.5
HYENA_TARGET = 1e-2
F_GROUPS = 4
N_MOD = 6
EPS = 1e-6
NEG_INF = -1e30

FFT_A = 64
FFT_R = 128
FFT_KA = FFT_A // 2 + 1
FFT_KA_PAD = 40
FM_A = 64

LANES = 128
ROW_SL = 8
VMEM_LIMIT = 48 * 1024 * 1024
LARGE_VMEM_LIMIT = 56 * 1024 * 1024


def _cparams(sem, vmem_limit=VMEM_LIMIT):
    return pltpu.CompilerParams(dimension_semantics=sem, vmem_limit_bytes=vmem_limit)


def _dot(a, b):
    return jnp.dot(a, b, preferred_element_type=F32)


def _mxu_const(m):
    return jnp.asarray(m, dtype=F32).astype(BF16)


def _norm_mod(x, g, shift, scale):
    ms = jnp.mean(x * x, axis=-1, keepdims=True)
    y = x * lax.rsqrt(ms + EPS) * g
    return y * (1.0 + scale) + shift


def _ada_body(c_ref, w_ref, b_ref, o_ref):
    cv = c_ref[...]
    s = cv * jax.nn.sigmoid(cv)
    o_ref[0] = jnp.dot(s, w_ref[0], precision=HIGHEST, preferred_element_type=F32) + b_ref[0]


def _ada(cvec, w_ada, b_ada):
    depth, d, n = w_ada.shape
    rows = cvec.shape[0]
    bn = n // 4
    return pl.pallas_call(
        _ada_body,
        out_shape=jax.ShapeDtypeStruct((depth, rows, n), F32),
        grid=(depth, n // bn),
        in_specs=[pl.BlockSpec((rows, d), lambda l, j: (0, 0)),
                  pl.BlockSpec((1, d, bn), lambda l, j: (l, 0, j)),
                  pl.BlockSpec((1, 1, bn), lambda l, j: (l, 0, j))],
        out_specs=pl.BlockSpec((1, rows, bn), lambda l, j: (l, 0, j)),
        compiler_params=_cparams(("parallel", "parallel")),
        name="ada",
    )(cvec, w_ada, b_ada.reshape(depth, 1, n))


def _inproj_body(x_ref, xp_ref, xn_ref, g_ref, sh_ref, sc_ref, why_ref, wqkv_ref, sw_ref, sb_ref,
                 v_ref, x1_ref, x2_ref, q_ref, k_ref, va_ref, *, n_tiles, q_scale, c_hy, c_na):
    i = pl.program_id(1)
    g, sh, sc = g_ref[...], sh_ref[0], sc_ref[0]
    hf = _norm_mod(x_ref[0], g, sh, sc)
    h = hf.astype(BF16)
    tm = hf.shape[0]
    hx = jnp.concatenate([_norm_mod(xp_ref[0], g, sh, sc), hf, _norm_mod(xn_ref[0], g, sh, sc)],
                         axis=0).astype(BF16)
    row = lax.broadcasted_iota(jnp.int32, (tm, c_hy), 0)
    sw = sw_ref[...]
    sb = sb_ref[...]
    lo, hi = ROW_SL, ROW_SL + tm
    for ci, out_ref in enumerate((v_ref, x1_ref, x2_ref)):
        cols = slice(ci * c_hy, (ci + 1) * c_hy)
        zx = _dot(hx, why_ref[:, cols])
        zh = zx[lo:hi]
        zp = jnp.where(i > 0, zx[lo - 1:lo], 0.0)
        zn = jnp.where(i < n_tiles - 1, zx[hi:hi + 1], 0.0)
        z_m1 = jnp.where(row == 0, zp, pltpu.roll(zh, 1, 0))
        z_p1 = jnp.where(row == tm - 1, zn, pltpu.roll(zh, tm - 1, 0))
        out_ref[0] = z_m1 * sw[0:1, cols] + zh * sw[1:2, cols] + z_p1 * sw[2:3, cols] + sb[:, cols]
    for ci, (out_ref, mult) in enumerate(((q_ref, q_scale), (k_ref, None), (va_ref, None))):
        z = _dot(h, wqkv_ref[:, ci * c_na:(ci + 1) * c_na])
        out_ref[0] = (z if mult is None else z * mult).astype(BF16)


def _inproj(x, g, shift, scale, w_hy, w_qkv, short_w, short_b, tm=1024):
    b, s, d = x.shape
    c_hy = w_hy.shape[1] // 3
    c_na = w_qkv.shape[1] // 3
    n_tiles = s // tm
    halo_per_tile = tm // ROW_SL
    last_halo = s // ROW_SL - 1
    body = functools.partial(_inproj_body, n_tiles=n_tiles, q_scale=NA_HEAD_DIM ** -0.5,
                             c_hy=c_hy, c_na=c_na)
    tok = lambda c: pl.BlockSpec((1, tm, c), lambda bi, i: (bi, i, 0))
    full2 = lambda a: pl.BlockSpec(a.shape, lambda bi, i: (0, 0))
    resident = lambda a: pl.BlockSpec(a.shape, lambda bi, i: (0, 0), pipeline_mode=pl.Buffered(1))
    per_b = pl.BlockSpec((1, 1, d), lambda bi, i: (bi, 0, 0))
    return pl.pallas_call(
        body,
        out_shape=[jax.ShapeDtypeStruct((b, s, c_hy), F32)] * 3 + [jax.ShapeDtypeStruct((b, s, c_na), BF16)] * 3,
        grid=(b, n_tiles),
        in_specs=[tok(d),
                  pl.BlockSpec((1, ROW_SL, d), lambda bi, i: (bi, jnp.maximum(i * halo_per_tile - 1, 0), 0)),
                  pl.BlockSpec((1, ROW_SL, d),
                               lambda bi, i: (bi, jnp.minimum((i + 1) * halo_per_tile, last_halo), 0)),
                  full2(g), per_b, per_b, resident(w_hy), resident(w_qkv), full2(short_w), full2(short_b)],
        out_specs=[tok(c_hy)] * 3 + [tok(c_na)] * 3,
        compiler_params=_cparams(("parallel", "parallel"), vmem_limit=LARGE_VMEM_LIMIT),
        name="inproj",
    )(x, x, x, g, shift, scale, w_hy, w_qkv, short_w, short_b)


def _ctxkv_body(x_ref, g_ref, sh_ref, sc_ref, w_ref, k_ref, v_ref, *, c_na):
    h = _norm_mod(x_ref[0], g_ref[...], sh_ref[0], sc_ref[0]).astype(BF16)
    z = _dot(h, w_ref[...])
    k_ref[0] = z[:, 0:c_na].astype(BF16)
    v_ref[0] = z[:, c_na:2 * c_na].astype(BF16)


def _ctxkv(ctx, g, shift, scale, w_kv):
    b, n, d = ctx.shape
    c_na = w_kv.shape[1] // 2
    one = pl.BlockSpec((1, 1, d), lambda bi: (0, 0, 0))
    return pl.pallas_call(
        functools.partial(_ctxkv_body, c_na=c_na),
        out_shape=[jax.ShapeDtypeStruct((b, n, c_na), BF16)] * 2,
        grid=(b,),
        in_specs=[pl.BlockSpec((1, n, d), lambda bi: (bi, 0, 0)),
                  pl.BlockSpec(g.shape, lambda bi: (0, 0)), one, one,
                  pl.BlockSpec(w_kv.shape, lambda bi: (0, 0))],
        out_specs=[pl.BlockSpec((1, n, c_na), lambda bi: (bi, 0, 0))] * 2,
        compiler_params=_cparams(("parallel",)),
        name="ctxkv",
    )(ctx, g, shift, scale, w_kv)


NA_HEADS_PER_BLK = 8
NA_ROWS_PER_STEP = 8


def _na_bias_body(r_ref, e_ref, ok_ref, o_ref):
    t = jnp.dot(r_ref[...], e_ref[...], precision=HIGHEST, preferred_element_type=F32)
    o_ref[...] = jnp.where(ok_ref[...] > 0.5, t, NEG_INF)


def _na_bias_table(rpb):
    w = GRID_W
    h, nr, nc = rpb.shape
    col = np.arange(w)[:, None]
    kc = np.arange(w)[None, :]
    c_start = np.clip(col - NA_WIN_C // 2, 0, w - NA_WIN_C)
    valid = ((kc >= c_start) & (kc < c_start + NA_WIN_C)).reshape(1, w * w)
    nc_pad = -(-nc // ROW_SL) * ROW_SL
    expand = (np.arange(nc_pad)[:, None, None] == (kc - col + NA_WIN_C - 1)[None]).reshape(nc_pad, w * w)
    rp = jnp.pad(rpb.astype(F32).reshape(h * nr, nc), ((0, 0), (0, nc_pad - nc)))
    full = lambda a: pl.BlockSpec(a.shape, lambda: (0,) * a.ndim)
    expand = jnp.asarray(expand, dtype=F32)
    ok = jnp.asarray(valid, dtype=F32)
    toep = pl.pallas_call(
        _na_bias_body,
        out_shape=jax.ShapeDtypeStruct((h * nr, w * w), F32),
        in_specs=[full(rp), full(expand), full(ok)],
        out_specs=pl.BlockSpec((h * nr, w * w), lambda: (0, 0)),
        name="na_bias",
    )(rp, expand, ok)
    t2 = toep.reshape(h, nr, w, w).transpose(0, 2, 1, 3).reshape(h, w, nr * w)
    slabs = jnp.stack([t2[:, :, (NA_WIN_R - 1 - off) * w:(2 * NA_WIN_R - 1 - off) * w]
                       for off in range(NA_WIN_R)], axis=1)
    hpb = NA_HEADS_PER_BLK
    slabs = slabs.reshape(h // hpb, hpb, NA_WIN_R, w, NA_WIN_R * w).transpose(0, 2, 1, 3, 4)
    return slabs.reshape(h // hpb, NA_WIN_R, hpb * w, NA_WIN_R * w)


def _natt_body(q_ref, k_ref, v_ref, kc_ref, vc_ref, bias_ref, o_ref, *, rows):
    w = GRID_W
    hpb = NA_HEADS_PER_BLK
    nloc = NA_WIN_R * w
    lane = lax.broadcasted_iota(jnp.int32, (1, hpb * NA_HEAD_DIM), 1)
    in_head = [(lane >= NA_HEAD_DIM * hh) & (lane < NA_HEAD_DIM * (hh + 1)) for hh in range(hpb)]
    kcx = kc_ref[0]
    vcx = vc_ref[0]
    nt = (((1,), (1,)), ((), ()))

    def one_row(r):
        r0 = jnp.clip(r - NA_WIN_R // 2, 0, rows - NA_WIN_R)
        off = r - r0
        qs = q_ref[0, pl.ds(pl.multiple_of(r * w, w), w), :]
        kw = k_ref[0, pl.ds(pl.multiple_of(r0 * w, w), nloc), :]
        vw = v_ref[0, pl.ds(pl.multiple_of(r0 * w, w), nloc), :]
        zero = jnp.zeros_like(qs)
        qst = jnp.concatenate([jnp.where(m, qs, zero) for m in in_head], axis=0)
        s_loc = lax.dot_general(qst, kw, nt, preferred_element_type=F32) + bias_ref[0, off]
        s_ctx = lax.dot_general(qst, kcx, nt, preferred_element_type=F32)
        m = jnp.maximum(jnp.max(s_loc, axis=-1, keepdims=True), jnp.max(s_ctx, axis=-1, keepdims=True))
        p_loc = jnp.exp(s_loc - m)
        p_ctx = jnp.exp(s_ctx - m)
        den = jnp.sum(p_loc, axis=-1, keepdims=True) + jnp.sum(p_ctx, axis=-1, keepdims=True)
        o = (_dot(p_loc.astype(BF16), vw) + _dot(p_ctx.astype(BF16), vcx)) * (1.0 / den)
        acc = jnp.where(in_head[0], o[0:w], 0.0)
        for hh in range(1, hpb):
            acc = acc + jnp.where(in_head[hh], o[hh * w:(hh + 1) * w], 0.0)
        o_ref[0, pl.ds(pl.multiple_of(r * w, w), w), :] = acc.astype(BF16)

    def row_group(i, carry):
        for r in range(NA_ROWS_PER_STEP):
            one_row(NA_ROWS_PER_STEP * i + r)
        return carry

    lax.fori_loop(0, rows // NA_ROWS_PER_STEP, row_group, 0)


def _natt(q, k, v, kc, vc, bias):
    b, s, c = q.shape
    nctx = kc.shape[1]
    lw = NA_HEADS_PER_BLK * NA_HEAD_DIM
    rows = s // GRID_W
    seq = pl.BlockSpec((1, s, lw), lambda bi, g: (bi, 0, g))
    cx = pl.BlockSpec((1, nctx, lw), lambda bi, g: (bi, 0, g))
    return pl.pallas_call(
        functools.partial(_natt_body, rows=rows),
        out_shape=jax.ShapeDtypeStruct((b, s, c), BF16),
        grid=(b, c // lw),
        in_specs=[seq, seq, seq, cx, cx,
                  pl.BlockSpec((1,) + bias.shape[1:], lambda bi, g: (g, 0, 0, 0))],
        out_specs=seq,
        compiler_params=_cparams(("parallel", "parallel")),
        name="natt",
    )(q, k, v, kc, vc, bias)


def _hyena_feats(seq_len):
    t = jnp.linspace(0.0, 1.0, seq_len, dtype=F32)[:, None]
    bands = jnp.linspace(1e-4, HYENA_BANDS - 1, HYENA_BANDS, dtype=F32)
    ang = (2.0 * math.pi / seq_len) * jnp.arange(seq_len, dtype=F32)[:, None] * bands[None, :]
    feats = jnp.concatenate([t, jnp.cos(ang), -jnp.sin(ang)], axis=-1)
    return jnp.pad(feats, ((0, 0), (0, LANES - HYENA_EMB)))


def _filt_body(feat_ref, w1_ref, b1_ref, w2_ref, b2_ref, w3_ref, fr_ref, dl_ref, o_ref, l1_ref, h_scr,
               *, halves):
    j = pl.program_id(0)
    hp = functools.partial(jnp.dot, precision=HIGHEST, preferred_element_type=F32)
    feats = feat_ref[...]

    @pl.when(j == 0)
    def _():
        fr = fr_ref[...]
        h = jnp.sin(fr[0:1] * (hp(feats, w1_ref[...]) + b1_ref[...]))
        h_scr[...] = jnp.sin(fr[1:2] * (hp(h, w2_ref[...]) + b2_ref[...]))

    h2 = h_scr[...]
    h_hi = h2.astype(BF16)
    h_lo = (h2 - h_hi.astype(F32)).astype(BF16)
    w3 = w3_ref[...]
    w_hi = w3.astype(BF16)
    w_lo = (w3 - w_hi.astype(F32)).astype(BF16)
    hc = (_dot(jnp.concatenate([h_hi, h_lo], axis=1), jnp.concatenate([w_hi, w_hi], axis=0))
          + _dot(h_hi, w_lo))
    t = feats[:, 0:1]
    hc = hc * jnp.exp(-t * dl_ref[...])
    row = lax.broadcasted_iota(jnp.int32, hc.shape, 0)
    hc = jnp.where((row == 0) & ((j // halves) % 2 == 1), 0.0, hc)
    l1_ref[0] = jnp.sum(jnp.abs(hc), axis=0, keepdims=True)
    o_ref[0] = hc


def _hyena_filter_taps(seq_len, f_w1, f_b1, f_w2, f_b2, f_w3, f_freq, c_hy):
    feats = _hyena_feats(seq_len)
    hid = f_w1.shape[1]
    w1 = jnp.pad(f_w1.astype(F32), ((0, LANES - HYENA_EMB), (0, 0)))
    deltas = jnp.abs(jnp.linspace(math.log(HYENA_TARGET) / HYENA_SLOW_DECAY,
                                  math.log(HYENA_TARGET) / HYENA_FAST_DECAY, c_hy, dtype=F32))[None, :]
    nblk = f_w3.shape[1] // c_hy
    halves = 2
    cb = c_hy // halves
    c0 = lambda a: pl.BlockSpec(a.shape, lambda j: (0, 0))
    b1, b2 = f_b1.reshape(1, hid), f_b2.reshape(1, hid)
    return pl.pallas_call(
        functools.partial(_filt_body, halves=halves),
        out_shape=[jax.ShapeDtypeStruct((nblk, seq_len, c_hy), F32),
                   jax.ShapeDtypeStruct((nblk, 1, c_hy), F32)],
        grid=(nblk * halves,),
        in_specs=[c0(feats), c0(w1), c0(b1), c0(f_w2), c0(b2),
                  pl.BlockSpec((hid, cb), lambda j: (0, j)), c0(f_freq),
                  pl.BlockSpec((1, cb), lambda j: (0, j % halves))],
        out_specs=[pl.BlockSpec((1, seq_len, cb), lambda j: (j // halves, 0, j % halves)),
                   pl.BlockSpec((1, 1, cb), lambda j: (j // halves, 0, j % halves))],
        scratch_shapes=[pltpu.VMEM((seq_len, hid), F32)],
        compiler_params=_cparams(("arbitrary",)),
        name="hyena_filter",
    )(feats, w1, b1, f_w2, b2, f_w3, f_freq, deltas)


def _conv_dft_constants():
    a_half = FFT_A // 2
    n = FFT_A * FFT_R
    ka = np.arange(FFT_KA)[:, None]
    a = np.arange(a_half)[None, :]
    ph = 2.0 * np.pi * ka * a / FFT_A
    m_fwd = np.zeros((2 * FFT_KA_PAD, a_half))
    m_fwd[:FFT_KA] = np.cos(ph)
    m_fwd[FFT_KA_PAD:FFT_KA_PAD + FFT_KA] = -np.sin(ph)
    wgt = np.where((ka == 0) | (ka == FFT_A // 2), 1.0, 2.0)
    m_inv = np.zeros((a_half, 2 * FFT_KA_PAD))
    m_inv[:, :FFT_KA] = (wgt * np.cos(ph)).T / n
    m_inv[:, FFT_KA_PAD:FFT_KA_PAD + FFT_KA] = (-wgt * np.sin(ph)).T / n
    kb = np.arange(FFT_R)[None, :, None]
    b = np.arange(FFT_R)[None, None, :]
    kaa = np.arange(FFT_KA)[:, None, None]
    th = 2.0 * np.pi * (b * kb / FFT_R + b * kaa / n)
    gr, gi = np.cos(th), -np.sin(th)
    g2 = np.zeros((FFT_KA_PAD, 2 * FFT_R, 2 * FFT_R))
    g2[:FFT_KA] = np.block([[gr, -gi], [gi, gr]])
    grt, git = gr.transpose(0, 2, 1), gi.transpose(0, 2, 1)
    g2h = np.zeros_like(g2)
    g2h[:FFT_KA] = np.block([[grt, git], [-git, grt]])
    return _mxu_const(m_fwd), _mxu_const(m_inv), _mxu_const(g2), _mxu_const(g2h)


FFT_NB = 16


def _fwd1_body(m_ref, u_ref, o_ref):
    u = jnp.concatenate([u_ref[0, :, bb, :] for bb in range(FFT_NB)], axis=1).astype(BF16)
    res = _dot(m_ref[...], u)
    o_ref[0, 0] = res[0:FFT_KA_PAD]
    o_ref[0, 1] = res[FFT_KA_PAD:2 * FFT_KA_PAD]


def _conv_fwd1(u, m_fwd):
    n, seq, c = u.shape
    a_half = FFT_A // 2
    return pl.pallas_call(
        _fwd1_body,
        out_shape=jax.ShapeDtypeStruct((n, 2, FFT_KA_PAD, FFT_R * c), F32),
        grid=(n, FFT_R // FFT_NB),
        in_specs=[pl.BlockSpec(m_fwd.shape, lambda i, j: (0, 0)),
                  pl.BlockSpec((1, a_half, FFT_NB, c), lambda i, j: (i, 0, j, 0))],
        out_specs=pl.BlockSpec((1, 2, FFT_KA_PAD, FFT_NB * c), lambda i, j: (i, 0, 0, j)),
        compiler_params=_cparams(("parallel", "parallel")),
        name="conv_fwd1",
    )(m_fwd, u.reshape(n, a_half, FFT_R, c))


FFT_KB = 8


def _rows_to_slabs(src_ref, dst_scr, c):
    for part in range(2):
        for b in range(FFT_R):
            dst_scr[part, :, b, :] = src_ref[0, part, :, b * c:(b + 1) * c]


def _slabs_to_rows(src_scr, dst_ref, c):
    for part in range(2):
        for b in range(FFT_R):
            dst_ref[0, part, :, b * c:(b + 1) * c] = src_scr[part, :, b, :]


def _slab(scr, i):
    return jnp.concatenate([scr[0, i], scr[1, i]], axis=0).astype(BF16)


def _per_ka_block(j, work, clear):
    full_blocks = FFT_KA // FFT_KB
    tail = FFT_KA - full_blocks * FFT_KB

    @pl.when(j < full_blocks)
    def _():
        for i in range(FFT_KB):
            work(i)

    @pl.when(j >= full_blocks)
    def _():
        for i in range(FFT_KB):
            (work if i < tail else clear)(i)


def _fwd2f_body(sf_ref, sb_ref, g_ref, l1_ref, kf_ref, f3, b3):
    o = pl.program_id(0)
    j = pl.program_id(1)
    r2 = 2 * FFT_R
    c = kf_ref.shape[-1]
    _rows_to_slabs(sf_ref, f3, c)
    _rows_to_slabs(sb_ref, b3, c)
    inv = 1.0 / (l1_ref[2 * o] + l1_ref[2 * o + 1] + EPS)

    def spectrum(i):
        xf = _dot(g_ref[i], _slab(f3, i))
        xb = _dot(g_ref[i], _slab(b3, i))
        kf_ref[0, i, 0:FFT_R] = (xf[0:FFT_R] + xb[0:FFT_R]) * inv
        kf_ref[0, i, FFT_R:r2] = (xf[FFT_R:r2] - xb[FFT_R:r2]) * inv

    def clear(i):
        kf_ref[0, i] = jnp.zeros((r2, c), F32)

    _per_ka_block(j, spectrum, clear)


def _filter_spectrum(s_filt, l1, g2, c):
    n_ord = s_filt.shape[0] // 2
    cols = s_filt.shape[-1]
    r2 = 2 * FFT_R
    return pl.pallas_call(
        _fwd2f_body,
        out_shape=jax.ShapeDtypeStruct((n_ord, FFT_KA_PAD, r2, c), F32),
        grid=(n_ord, FFT_KA_PAD // FFT_KB),
        in_specs=[pl.BlockSpec((1, 2, FFT_KB, cols), lambda o, j: (2 * o, 0, j, 0)),
                  pl.BlockSpec((1, 2, FFT_KB, cols), lambda o, j: (2 * o + 1, 0, j, 0)),
                  pl.BlockSpec((FFT_KB, r2, r2), lambda o, j: (j, 0, 0)),
                  pl.BlockSpec(l1.shape, lambda o, j: (0, 0, 0))],
        out_specs=pl.BlockSpec((1, FFT_KB, r2, c), lambda o, j: (o, j, 0, 0)),
        scratch_shapes=[pltpu.VMEM((2, FFT_KB, FFT_R, c), F32)] * 2,
        compiler_params=_cparams(("parallel", "parallel")),
        name="filter_spectrum",
    )(s_filt, s_filt, g2, l1)


def _mid_body(s_ref, g_ref, gh_ref, kf_ref, t_ref, s3, t3):
    j = pl.program_id(1)
    r2 = 2 * FFT_R
    c = kf_ref.shape[-1]
    _rows_to_slabs(s_ref, s3, c)

    def convolve(i):
        x = _dot(g_ref[i], _slab(s3, i))
        xr, xi = x[0:FFT_R], x[FFT_R:r2]
        kr, ki = kf_ref[0, i, 0:FFT_R], kf_ref[0, i, FFT_R:r2]
        y = jnp.concatenate([xr * kr - xi * ki, xr * ki + xi * kr], axis=0).astype(BF16)
        t = _dot(gh_ref[i], y)
        t3[0, i] = t[0:FFT_R]
        t3[1, i] = t[FFT_R:r2]

    def clear(i):
        t3[0, i] = jnp.zeros((FFT_R, c), F32)
        t3[1, i] = jnp.zeros((FFT_R, c), F32)

    _per_ka_block(j, convolve, clear)
    _slabs_to_rows(t3, t_ref, c)


def _conv_mid(s, kf, order, g2, g2h, c):
    n, _, _, cols = s.shape
    r2 = 2 * FFT_R
    blk = pl.BlockSpec((1, 2, FFT_KB, cols), lambda i, j: (i, 0, j, 0))
    gspec = pl.BlockSpec((FFT_KB, r2, r2), lambda i, j: (j, 0, 0))
    return pl.pallas_call(
        _mid_body,
        out_shape=jax.ShapeDtypeStruct(s.shape, F32),
        grid=(n, FFT_KA_PAD // FFT_KB),
        in_specs=[blk, gspec, gspec,
                  pl.BlockSpec((1, FFT_KB, r2, c), lambda i, j: (order, j, 0, 0))],
        out_specs=blk,
        scratch_shapes=[pltpu.VMEM((2, FFT_KB, FFT_R, c), F32)] * 2,
        compiler_params=_cparams(("parallel", "parallel")),
        name="conv_mid",
    )(s, g2, g2h, kf)


def _inv1_body(m_ref, t_ref, u_ref, xg_ref, sk_ref, o_ref):
    c = u_ref.shape[-1]
    t2 = t_ref[0].reshape(2 * FFT_KA_PAD, FFT_NB * c).astype(BF16)
    y = _dot(m_ref[...], t2)
    for bb in range(FFT_NB):
        conv = y[:, bb * c:(bb + 1) * c] + u_ref[0, :, bb, :] * sk_ref[...]
        o_ref[0, :, bb, :] = xg_ref[0, :, bb, :] * conv


def _conv_inv1(t, u, xg, skip, m_inv):
    n, seq, c = u.shape
    a_half = FFT_A // 2
    sk = skip.astype(F32).reshape(1, c)
    uspec = pl.BlockSpec((1, a_half, FFT_NB, c), lambda i, j: (i, 0, j, 0))
    view = lambda a: a.reshape(n, a_half, FFT_R, c)
    out = pl.pallas_call(
        _inv1_body,
        out_shape=jax.ShapeDtypeStruct((n, a_half, FFT_R, c), F32),
        grid=(n, FFT_R // FFT_NB),
        in_specs=[pl.BlockSpec(m_inv.shape, lambda i, j: (0, 0)),
                  pl.BlockSpec((1, 2, FFT_KA_PAD, FFT_NB * c), lambda i, j: (i, 0, 0, j)),
                  uspec, uspec,
                  pl.BlockSpec((1, c), lambda i, j: (0, 0))],
        out_specs=uspec,
        compiler_params=_cparams(("parallel", "parallel")),
        name="conv_inv1",
    )(m_inv, t, view(u), view(xg), sk)
    return out.reshape(n, seq, c)


def _hyena(v, x1, x2, f_w1, f_b1, f_w2, f_b2, f_w3, f_freq, skip):
    _, seq, c = v.shape
    assert 2 * seq == FFT_A * FFT_R
    m_fwd, m_inv, g2, g2h = _conv_dft_constants()
    taps, l1 = _hyena_filter_taps(seq, f_w1, f_b1, f_w2, f_b2, f_w3, f_freq, c)
    kf = _filter_spectrum(_conv_fwd1(taps, m_fwd), l1, g2, c)
    y = v
    for order, xg in enumerate((x1, x2)):
        t = _conv_mid(_conv_fwd1(y, m_fwd), kf, order, g2, g2h, c)
        y = _conv_inv1(t, y, xg, skip[order], m_inv)
    return y


def _mix_ffn_body(x_ref, a1_ref, a2_ref, wm_ref, gm_ref, g_ref, sh_ref, sc_ref, gt_ref,
                  w1_ref, w3_ref, w2_ref, o_ref, *, fb):
    c1 = a1_ref.shape[-1]
    mixed = _dot(a1_ref[0].astype(BF16), wm_ref[0:c1]) + _dot(a2_ref[0].astype(BF16), wm_ref[c1:])
    xm = x_ref[0] + gm_ref[0] * mixed
    h = _norm_mod(xm, g_ref[...], sh_ref[0], sc_ref[0]).astype(BF16)
    acc = None
    for lo in range(0, w1_ref.shape[1], fb):
        a = _dot(h, w1_ref[:, lo:lo + fb])
        u = (a * jax.nn.sigmoid(a) * _dot(h, w3_ref[:, lo:lo + fb])).astype(BF16)
        part = _dot(u, w2_ref[lo:lo + fb, :])
        acc = part if acc is None else acc + part
    o_ref[0] = xm + gt_ref[0] * acc


def _mix_ffn(x, a1, a2, w_mix, gate_mix, g, shift, scale, gate, w1, w3, w2, tm=512, fb=1408):
    b, s, d = x.shape
    tok = lambda c: pl.BlockSpec((1, tm, c), lambda bi, i: (bi, i, 0))
    per_b = pl.BlockSpec((1, 1, d), lambda bi, i: (bi, 0, 0))
    const = lambda a: pl.BlockSpec(a.shape, lambda bi, i: (0, 0))
    resident = lambda a: pl.BlockSpec(a.shape, lambda bi, i: (0, 0), pipeline_mode=pl.Buffered(1))
    return pl.pallas_call(
        functools.partial(_mix_ffn_body, fb=fb),
        out_shape=jax.ShapeDtypeStruct(x.shape, F32),
        grid=(b, s // tm),
        in_specs=[tok(d), tok(a1.shape[-1]), tok(a2.shape[-1]), resident(w_mix), per_b,
                  const(g), per_b, per_b, per_b, resident(w1), resident(w3), resident(w2)],
        out_specs=tok(d),
        compiler_params=_cparams(("parallel", "parallel"), vmem_limit=LARGE_VMEM_LIMIT),
        name="mix_ffn",
    )(x, a1, a2, w_mix, gate_mix, g, shift, scale, gate, w1, w3, w2)


def _fm_constants(cg):
    j = np.arange(cg)[:, None]
    m = np.arange(cg)[None, :]
    ph = 2.0 * np.pi * j * m / cg
    w_cs = np.concatenate([np.cos(ph), np.sin(ph)], axis=1)
    d = np.arange(FM_A)[:, None]
    a = np.arange(FM_A)[None, :]
    ph = 2.0 * np.pi * d * a / FM_A
    fr, fi = np.cos(ph), -np.sin(ph)
    m1 = np.block([[fr, fi], [fi, -fr]])
    n = FM_A * FM_A
    dd = np.arange(FM_A)[:, None, None]
    c = np.arange(FM_A)[None, :, None]
    b = np.arange(FM_A)[None, None, :]
    th = 2.0 * np.pi * (b * c / FM_A + b * dd / n)
    gcat = np.concatenate([np.cos(th), np.sin(th)], axis=2)
    return _mxu_const(w_cs), _mxu_const(m1), _mxu_const(gcat)


def _fm_front_body(x_ref, g_ref, sh_ref, sc_ref, w_ref, m_ref, o_ref, *, cg, nb):
    d = x_ref.shape[-1]
    xs = jnp.concatenate([x_ref[0, :, bb, :] for bb in range(nb)], axis=0)
    h = _norm_mod(xs, g_ref[...], sh_ref[0], sc_ref[0]).astype(BF16)
    pq = [_dot(h[:, grp * cg:(grp + 1) * cg], w_ref[...]) for grp in range(d // cg)]
    p = jnp.concatenate([t[:, 0:cg] for t in pq], axis=1)
    q = jnp.concatenate([t[:, cg:2 * cg] for t in pq], axis=1)
    for bb in range(nb):
        rows = slice(bb * FM_A, (bb + 1) * FM_A)
        res = _dot(m_ref[...], jnp.concatenate([p[rows], q[rows]], axis=0).astype(BF16))
        o_ref[0, 0, :, bb, :] = res[0:FM_A]
        o_ref[0, 1, :, bb, :] = res[FM_A:2 * FM_A]


def _fm_front(x, g, shift, scale, w_cs, m1, nb=8):
    b, s, d = x.shape
    cg = w_cs.shape[0]
    per_b = pl.BlockSpec((1, 1, d), lambda bi, j: (bi, 0, 0))
    const = lambda a: pl.BlockSpec(a.shape, lambda bi, j: (0, 0))
    return pl.pallas_call(
        functools.partial(_fm_front_body, cg=cg, nb=nb),
        out_shape=jax.ShapeDtypeStruct((b, 2, FM_A, s // FM_A, d), F32),
        grid=(b, s // FM_A // nb),
        in_specs=[pl.BlockSpec((1, FM_A, nb, d), lambda bi, j: (bi, 0, j, 0)),
                  const(g), per_b, per_b, const(w_cs), const(m1)],
        out_specs=pl.BlockSpec((1, 2, FM_A, nb, d), lambda bi, j: (bi, 0, 0, j, 0)),
        compiler_params=_cparams(("parallel", "parallel")),
        name="fm_front",
    )(x.reshape(b, FM_A, s // FM_A, d), g, shift, scale, w_cs, m1)


def _fm_s2_body(s_ref, g_ref, o_ref, *, dblk, scale):
    for i in range(dblk):
        s2 = jnp.concatenate([s_ref[0, 0, i], s_ref[0, 1, i]], axis=0).astype(BF16)
        o_ref[:, i, :] = _dot(g_ref[i], s2) * scale


def _fm_stage2(sv, gcat, seq, d, dblk=8):
    b = sv.shape[0]
    scale = 1.0 / math.sqrt(seq * (d // F_GROUPS))
    out = pl.pallas_call(
        functools.partial(_fm_s2_body, dblk=dblk, scale=scale),
        out_shape=jax.ShapeDtypeStruct((b * FM_A, dblk * (FM_A // dblk), d), F32),
        grid=(b, FM_A // dblk),
        in_specs=[pl.BlockSpec((1, 2, dblk, FM_A, d), lambda bi, j: (bi, 0, j, 0, 0)),
                  pl.BlockSpec((dblk, FM_A, 2 * FM_A), lambda bi, j: (j, 0, 0))],
        out_specs=pl.BlockSpec((FM_A, dblk, d), lambda bi, j: (bi, j, 0)),
        compiler_params=_cparams(("parallel", "parallel")),
        name="fm_stage2",
    )(sv, gcat)
    return out.reshape(b, seq, d)


def _fourier_mix(x, g, shift, scale):
    b, s, d = x.shape
    assert s == FM_A * FM_A
    w_cs, m1, gcat = _fm_constants(d // F_GROUPS)
    return _fm_stage2(_fm_front(x, g, shift, scale, w_cs, m1), gcat, s, d)


MOE_TM = 1024
DMA_WINDOW = 128


def _router_body(x_ref, yf_ref, wf_ref, gf_ref, g_ref, sh_ref, sc_ref, wr_ref, br_ref,
                 xo_ref, h_ref, meta_ref, gw_ref, cnt_ref, carry):
    i = pl.program_id(0)

    @pl.when(i == 0)
    def _():
        carry[...] = jnp.zeros_like(carry)

    xm = x_ref[...] + gf_ref[0] * _dot(yf_ref[...].astype(BF16), wf_ref[...])
    xo_ref[...] = xm
    h = _norm_mod(xm, g_ref[...], sh_ref[0], sc_ref[0])
    _rows_to_tiles(h_ref, h)
    h_hi = h.astype(BF16)
    h_lo = (h - h_hi.astype(F32)).astype(BF16)
    by_hi = _dot(h_hi, wr_ref[...])
    logits = by_hi[:, 0:LANES] + by_hi[:, LANES:] + _dot(h_lo, wr_ref[:, 0:LANES]) + br_ref[...]
    lane = lax.broadcasted_iota(jnp.int32, logits.shape, 1)
    nl = logits.shape[-1]
    m1 = jnp.max(logits, axis=-1, keepdims=True)
    i1 = jnp.min(jnp.where(logits == m1, lane, nl), axis=-1, keepdims=True)
    rest = jnp.where(lane == i1, 3.0 * NEG_INF, logits)
    m2 = jnp.max(rest, axis=-1, keepdims=True)
    i2 = jnp.min(jnp.where(rest == m2, lane, nl), axis=-1, keepdims=True)
    e = jnp.exp(m2 - m1)
    gw_ref[...] = jnp.where(lane == 0, 1.0 / (1.0 + e), jnp.where(lane == 1, e / (1.0 + e), 0.0))
    onehot = jnp.where((lane == i1) | (lane == i2), 1.0, 0.0)
    tm = onehot.shape[0]
    earlier = lax.broadcasted_iota(jnp.int32, (tm, tm), 0) > lax.broadcasted_iota(jnp.int32, (tm, tm), 1)
    excl = _dot(jnp.where(earlier, 1.0, 0.0).astype(BF16), onehot.astype(BF16)) + carry[...]
    r1 = jnp.sum(jnp.where(lane == i1, excl, 0.0), axis=-1, keepdims=True).astype(jnp.int32)
    r2 = jnp.sum(jnp.where(lane == i2, excl, 0.0), axis=-1, keepdims=True).astype(jnp.int32)
    meta_ref[...] = jnp.where(lane == 0, i1, jnp.where(lane == 1, i2, jnp.where(lane == 2, r1, jnp.where(lane == 3, r2, 0))))
    carry[...] = carry[...] + jnp.sum(onehot, axis=0, keepdims=True)
    cnt_ref[...] = carry[...]


def _router(x, y_f, w_f, gate_f, g, shift, scale, w_router, b_router, tm=512):
    b, s, d = x.shape
    t = b * s
    ne = w_router.shape[1]
    wr = jnp.pad(w_router.astype(F32), ((0, 0), (0, LANES - ne)))
    wr_hi = wr.astype(BF16)
    wr = jnp.concatenate([wr_hi, (wr - wr_hi.astype(F32)).astype(BF16)], axis=1)
    br = jnp.pad(b_router.astype(F32).reshape(1, ne), ((0, 0), (0, LANES - ne)), constant_values=NEG_INF)
    spt = s // tm
    per_b = pl.BlockSpec((1, 1, d), lambda i: (i // spt, 0, 0))
    const = lambda a: pl.BlockSpec(a.shape, lambda i: (0, 0))
    tok = pl.BlockSpec((tm, d), lambda i: (i, 0))
    return pl.pallas_call(
        _router_body,
        out_shape=[jax.ShapeDtypeStruct((t, d), F32),
                   jax.ShapeDtypeStruct((t * ROW_SL, LANES), F32),
                   jax.ShapeDtypeStruct((t, LANES), jnp.int32),
                   jax.ShapeDtypeStruct((t, LANES), F32),
                   jax.ShapeDtypeStruct((1, LANES), F32)],
        grid=(t // tm,),
        in_specs=[tok, tok, const(w_f), per_b, const(g), per_b, per_b, const(wr), const(br)],
        out_specs=[tok,
                   pl.BlockSpec((tm * ROW_SL, LANES), lambda i: (i, 0)),
                   pl.BlockSpec((tm, LANES), lambda i: (i, 0)),
                   pl.BlockSpec((tm, LANES), lambda i: (i, 0)),
                   pl.BlockSpec((1, LANES), lambda i: (0, 0))],
        scratch_shapes=[pltpu.VMEM((1, LANES), F32)],
        compiler_params=_cparams(("arbitrary",)),
        name="router",
    )(x.reshape(t, d), y_f.reshape(t, d), w_f, gate_f, g, shift, scale, wr, br)


def _moe_plan(meta, counts, ne, tm):
    i1, i2, r1, r2 = meta[:, 0], meta[:, 1], meta[:, 2], meta[:, 3]
    cnt = counts[0, :ne].astype(jnp.int32)
    padded = ((cnt + tm - 1) // tm) * tm
    ends = jnp.cumsum(padded)
    offs = ends - padded
    pick = lambda idx: sum(jnp.where(idx == e, offs[e], 0) for e in range(ne))
    pos = jnp.concatenate([pick(i1) + r1, pick(i2) + r2]).astype(jnp.int32)
    n_tiles = (2 * meta.shape[0]) // tm + ne
    n_used = (ends[ne - 1] // tm).astype(jnp.int32)
    tile_start = jnp.minimum(jnp.arange(n_tiles, dtype=jnp.int32), n_used - 1) * tm
    tile_expert = jnp.sum(tile_start[:, None] >= ends[None, :], axis=1).astype(jnp.int32)
    group_end = sum(jnp.where(tile_expert == e, offs[e] + cnt[e], 0) for e in range(ne))
    tile_rows = jnp.clip(group_end - tile_start, 0, tm).astype(jnp.int32)
    return pos, offs + cnt, padded - cnt, tile_expert, n_used.reshape(1), tile_rows


def _windowed_copies(n, start_copy, wait_one):
    def body(i, carry):
        @pl.when(i >= DMA_WINDOW)
        def _():
            wait_one()
        start_copy(i)
        return carry

    lax.fori_loop(0, n, body, 0)

    def drain(i, carry):
        wait_one()
        return carry

    lax.fori_loop(0, jnp.minimum(n, DMA_WINDOW), drain, 0)


def _tile_of(ref, row):
    return ref.at[pl.ds(pl.multiple_of(row * ROW_SL, ROW_SL), ROW_SL)]


def _tiles_to_rows(ref, n, first=0):
    return jnp.concatenate([ref[pl.ds(first * ROW_SL + sl, n, stride=ROW_SL), :] for sl in range(ROW_SL)], axis=1)


def _rows_to_tiles(ref, val):
    n = val.shape[0]
    for sl in range(ROW_SL):
        ref[pl.ds(sl, n, stride=ROW_SL), :] = val[:, sl * LANES:(sl + 1) * LANES]


def _dispatch_body(pos_ref, pad_start_ref, pad_n_ref, h_ref, xs_hbm, sem, *, n_tok, ne):
    i = pl.program_id(0)
    td = h_ref.shape[0] // ROW_SL
    base = i * td
    copy = lambda src, dst: pltpu.make_async_copy(_tile_of(h_ref, src), _tile_of(xs_hbm, dst), sem)
    wait_one = lambda: copy(0, 0).wait()

    def start_token(r, carry):
        copy(r, pos_ref[base + r]).start(priority=0)
        copy(r, pos_ref[n_tok + base + r]).start(priority=1)
        return carry

    lax.fori_loop(0, td, start_token, 0, unroll=8)
    whole_tile = pltpu.make_async_copy(h_ref, xs_hbm.at[pl.ds(0, td * ROW_SL)], sem)
    whole_tile.wait()
    whole_tile.wait()

    @pl.when(i == 0)
    def _():
        for e in range(ne):
            first = pad_start_ref[e]
            _windowed_copies(pad_n_ref[e], lambda r: copy(0, first + r).start(), wait_one)


def _moe_dispatch(h3, pos, pad_start, pad_n, n_rows, td=1024):
    n_tok = h3.shape[0] // ROW_SL
    ne = pad_start.shape[0]
    return pl.pallas_call(
        functools.partial(_dispatch_body, n_tok=n_tok, ne=ne),
        out_shape=jax.ShapeDtypeStruct((n_rows * ROW_SL, LANES), h3.dtype),
        grid_spec=pltpu.PrefetchScalarGridSpec(
            num_scalar_prefetch=3, grid=(n_tok // td,),
            in_specs=[pl.BlockSpec((td * ROW_SL, LANES), lambda i, p, ps, pn: (i, 0))],
            out_specs=pl.BlockSpec(memory_space=pl.ANY),
            scratch_shapes=[pltpu.SemaphoreType.DMA(())]),
        compiler_params=_cparams(("arbitrary",)),
        name="moe_dispatch",
    )(pos, pad_start, pad_n, h3)


def _moe_grouped_body(te_ref, nu_ref, tr_ref, xs_ref, w1_ref, w3_ref, w2_ref, y_ref, xb_scr, acc_scr):
    i = pl.program_id(0)
    j = pl.program_id(1)
    tm = xb_scr.shape[0]
    hm = tm // 2

    def expert_rows(nrows):
        h = xb_scr[0:nrows]
        a = _dot(h, w1_ref[0].astype(BF16))
        u = (a * jax.nn.sigmoid(a) * _dot(h, w3_ref[0].astype(BF16))).astype(BF16)
        part = _dot(u, w2_ref[0].astype(BF16))

        @pl.when(j == 0)
        def _():
            acc_scr[0:nrows] = part

        @pl.when(j > 0)
        def _():
            acc_scr[0:nrows] += part

    @pl.when(i < nu_ref[0])
    def _():
        @pl.when(j == 0)
        def _():
            xb_scr[...] = _tiles_to_rows(xs_ref, tm).astype(BF16)

        @pl.when(tr_ref[i] > hm)
        def _():
            expert_rows(tm)

        @pl.when(tr_ref[i] <= hm)
        def _():
            expert_rows(hm)

            @pl.when(j == 0)
            def _():
                acc_scr[hm:tm] = jnp.zeros((tm - hm, acc_scr.shape[1]), F32)

        @pl.when(j == pl.num_programs(1) - 1)
        def _():
            _rows_to_tiles(y_ref, acc_scr[...])


def _moe_grouped(xs, tile_expert, n_used, tile_rows, w1, w3, w2, tm, fb=512):
    ne, d, f = w1.shape
    n_rows = xs.shape[0] // ROW_SL
    nj = f // fb
    row_tile = lambda i, j, te, nu, tr: (jnp.maximum(jnp.minimum(i, nu[0] - 1), 0), 0)
    jj = lambda i, j, nu: jnp.where(i < nu[0], j, nj - 1)
    return pl.pallas_call(
        _moe_grouped_body,
        out_shape=jax.ShapeDtypeStruct(xs.shape, F32),
        grid_spec=pltpu.PrefetchScalarGridSpec(
            num_scalar_prefetch=3, grid=(n_rows // tm, nj),
            in_specs=[pl.BlockSpec((tm * ROW_SL, LANES), row_tile),
                      pl.BlockSpec((1, d, fb), lambda i, j, te, nu, tr: (te[i], 0, jj(i, j, nu))),
                      pl.BlockSpec((1, d, fb), lambda i, j, te, nu, tr: (te[i], 0, jj(i, j, nu))),
                      pl.BlockSpec((1, fb, d), lambda i, j, te, nu, tr: (te[i], jj(i, j, nu), 0))],
            out_specs=pl.BlockSpec((tm * ROW_SL, LANES), row_tile),
            scratch_shapes=[pltpu.VMEM((tm, d), BF16), pltpu.VMEM((tm, d), F32)]),
        compiler_params=_cparams(("arbitrary", "arbitrary")),
        name="moe_grouped",
    )(tile_expert, n_used, tile_rows, xs, w1, w3, w2)


def _moe_final_body(pos_ref, x_ref, y_hbm, gw_ref, gt_ref, fg_ref, o_ref, yg_scr, sem, *, n_tok):
    i = pl.program_id(0)
    tc = x_ref.shape[0]
    slot = i % 2

    def gather_tile(step, into):
        base = step * tc

        def start_token(r, carry):
            dst = yg_scr.at[into]
            pltpu.make_async_copy(_tile_of(y_hbm, pos_ref[base + r]), _tile_of(dst, r),
                                  sem.at[into]).start(priority=0)
            pltpu.make_async_copy(_tile_of(y_hbm, pos_ref[n_tok + base + r]), _tile_of(dst, tc + r),
                                  sem.at[into]).start(priority=1)
            return carry

        lax.fori_loop(0, tc, start_token, 0, unroll=8)

    @pl.when(i == 0)
    def _():
        gather_tile(0, 0)

    @pl.when(i + 1 < pl.num_programs(0))
    def _():
        gather_tile(i + 1, 1 - slot)

    pltpu.make_async_copy(y_hbm.at[pl.ds(0, 2 * tc * ROW_SL)], yg_scr.at[slot], sem.at[slot]).wait()
    gw = gw_ref[...]
    rows = yg_scr.at[slot]
    y = gw[:, 0:1] * _tiles_to_rows(rows, tc) + gw[:, 1:2] * _tiles_to_rows(rows, tc, first=tc)
    xo = x_ref[...] + gt_ref[0] * y
    ms = jnp.mean(xo * xo, axis=-1, keepdims=True)
    o_ref[...] = xo * lax.rsqrt(ms + EPS) * fg_ref[...]


def _moe_final(x, y, pos, gw, gt, final_g, tc=512):
    b, s, d = x.shape
    t = b * s
    spt = s // tc
    out = pl.pallas_call(
        functools.partial(_moe_final_body, n_tok=t),
        out_shape=jax.ShapeDtypeStruct((t, d), F32),
        grid_spec=pltpu.PrefetchScalarGridSpec(
            num_scalar_prefetch=1, grid=(t // tc,),
            in_specs=[pl.BlockSpec((tc, d), lambda i, p: (i, 0)),
                      pl.BlockSpec(memory_space=pl.ANY),
                      pl.BlockSpec((tc, LANES), lambda i, p: (i, 0)),
                      pl.BlockSpec((1, 1, d), lambda i, p: (i // spt, 0, 0)),
                      pl.BlockSpec(final_g.shape, lambda i, p: (0, 0))],
            out_specs=pl.BlockSpec((tc, d), lambda i, p: (i, 0)),
            scratch_shapes=[pltpu.VMEM((2, 2 * tc * ROW_SL, LANES), F32), pltpu.SemaphoreType.DMA((2,))]),
        compiler_params=_cparams(("arbitrary",)),
        name="moe_final",
    )(pos, x.reshape(t, d), y, gw, gt, final_g)
    return out.reshape(b, s, d)


def _moe_routed(x, y_f, w_f, gate_f, g, shift, scale, gt, final_g, w_router, b_router, w1, w3, w2):
    ne = w1.shape[0]
    tm = MOE_TM
    x1, h3, meta, gw, counts = _router(x, y_f, w_f, gate_f, g, shift, scale, w_router, b_router)
    pos, pad_start, pad_n, tile_expert, n_used, tile_rows = _moe_plan(meta, counts, ne, tm)
    assert x.shape[-1] == ROW_SL * LANES
    n_rows = (2 * (h3.shape[0] // ROW_SL) // tm + ne) * tm
    xs = _moe_dispatch(h3, pos, pad_start, pad_n, n_rows)
    y = _moe_grouped(xs, tile_expert, n_used, tile_rows, w1, w3, w2, tm)
    return _moe_final(x1.reshape(x.shape), y, pos, gw, gt, final_g)


def kernel(x, c, ctx, c_ctx, w_ada, b_ada, norm_g, w_in, hy_short_w, hy_short_b, hy_f_w1, hy_f_b1, hy_f_w2, hy_f_b2, hy_f_w3, hy_f_freq, hy_skip, na_rpb, w_mix_out, ffn_w1, ffn_w3, ffn_w2, w_fourier, w_router, b_router, moe_w1, moe_w3, moe_w2, final_g):
    b, s, d = x.shape
    depth = w_ada.shape[0]
    assert depth == 2, "layer 0 mixes with Hyena/attention, layer 1 with Fourier/MoE"
    c_hy = hy_skip.shape[-1]
    c_na = d - c_hy

    cvec = jnp.concatenate([c, c_ctx[None, :], jnp.zeros((8 - b - 1, d), F32)], axis=0)
    mods = _ada(cvec, w_ada, b_ada)

    def mod(layer, idx, ctx_row=False):
        m = mods[layer, :, idx * d:(idx + 1) * d]
        return m[b:b + 1, None, :] if ctx_row else m[0:b, None, :]

    row = lambda a: a.reshape(1, -1)

    w_in0 = w_in[0].astype(BF16)
    w_hy, w_qkv = w_in0[:, 0:3 * c_hy], w_in0[:, 3 * c_hy:]
    v, x1, x2, q, k, va = _inproj(x, row(norm_g[0, 0]), mod(0, 0), mod(0, 1), w_hy, w_qkv,
                                  hy_short_w[0], row(hy_short_b[0]))
    kc, vc = _ctxkv(ctx, row(norm_g[0, 0]), mod(0, 0, True), mod(0, 1, True), w_qkv[:, c_na:])
    y_na = _natt(q, k, va, kc, vc, _na_bias_table(na_rpb[0]))
    y_hy = _hyena(v, x1, x2, hy_f_w1[0], hy_f_b1[0], hy_f_w2[0], hy_f_b2[0], hy_f_w3[0],
                  hy_f_freq[0], hy_skip[0])
    x = _mix_ffn(x, y_hy, y_na, w_mix_out[0].astype(BF16), mod(0, 2),
                 row(norm_g[0, 1]), mod(0, 3), mod(0, 4), mod(0, 5),
                 ffn_w1[0].astype(BF16), ffn_w3[0].astype(BF16), ffn_w2[0].astype(BF16))

    y_f = _fourier_mix(x, row(norm_g[1, 0]), mod(1, 0), mod(1, 1))
    return _moe_routed(x, y_f, w_fourier[0].astype(BF16), mod(1, 2),
                       row(norm_g[1, 1]), mod(1, 3), mod(1, 4), mod(1, 5), row(final_g),
                       w_router[0], b_router[0],
                       moe_w1[0], moe_w3[0], moe_w2[0])
```

```python
import functools
import math

import numpy as np
import jax
import jax.numpy as jnp
from jax import lax
from jax.experimental import pallas as pl
from jax.experimental.pallas import tpu as pltpu

F32 = jnp.float32
BF16 = jnp.bfloat16
HIGHEST = lax.Precision.HIGHEST

GRID_W = 64
NA_HEAD_DIM = 32
NA_WIN_R = 8
NA_WIN_C = 16
HYENA_EMB = 33
HYENA_BANDS = (HYENA_EMB - 1) // 2
HYENA_FAST_DECAY = 0.3
HYENA_SLOW_DECAY = 1.5
HYENA_TARGET = 1e-2
F_GROUPS = 4
N_MOD = 6
EPS = 1e-6
NEG_INF = -1e30

FFT_A = 64
FFT_R = 128
FFT_KA = FFT_A // 2 + 1
FFT_KA_PAD = 40
FM_A = 64

LANES = 128
ROW_SL = 8
VMEM_LIMIT = 48 * 1024 * 1024
LARGE_VMEM_LIMIT = 56 * 1024 * 1024


def _cparams(sem, vmem_limit=VMEM_LIMIT):
    return pltpu.CompilerParams(dimension_semantics=sem, vmem_limit_bytes=vmem_limit)


def _dot(a, b):
    return jnp.dot(a, b, preferred_element_type=F32)


def _mxu_const(m):
    return jnp.asarray(m, dtype=F32).astype(BF16)


def _norm_mod(x, g, shift, scale):
    ms = jnp.mean(x * x, axis=-1, keepdims=True)
    y = x * lax.rsqrt(ms + EPS) * g
    return y * (1.0 + scale) + shift


def _ada_body(c_ref, w_ref, b_ref, o_ref):
    cv = c_ref[...]
    s = cv * jax.nn.sigmoid(cv)
    o_ref[0] = jnp.dot(s, w_ref[0], precision=HIGHEST, preferred_element_type=F32) + b_ref[0]


def _ada(cvec, w_ada, b_ada):
    depth, d, n = w_ada.shape
    rows = cvec.shape[0]
    bn = n // 4
    return pl.pallas_call(
        _ada_body,
        out_shape=jax.ShapeDtypeStruct((depth, rows, n), F32),
        grid=(depth, n // bn),
        in_specs=[pl.BlockSpec((rows, d), lambda l, j: (0, 0)),
                  pl.BlockSpec((1, d, bn), lambda l, j: (l, 0, j)),
                  pl.BlockSpec((1, 1, bn), lambda l, j: (l, 0, j))],
        out_specs=pl.BlockSpec((1, rows, bn), lambda l, j: (l, 0, j)),
        compiler_params=_cparams(("parallel", "parallel")),
        name="ada",
    )(cvec, w_ada, b_ada.reshape(depth, 1, n))


def _inproj_body(x_ref, xp_ref, xn_ref, g_ref, sh_ref, sc_ref, why_ref, wqkv_ref, sw_ref, sb_ref,
                 v_ref, x1_ref, x2_ref, q_ref, k_ref, va_ref, *, n_tiles, q_scale, c_hy, c_na):
    i = pl.program_id(1)
    g, sh, sc = g_ref[...], sh_ref[0], sc_ref[0]
    hf = _norm_mod(x_ref[0], g, sh, sc)
    h = hf.astype(BF16)
    tm = hf.shape[0]
    hx = jnp.concatenate([_norm_mod(xp_ref[0], g, sh, sc), hf, _norm_mod(xn_ref[0], g, sh, sc)],
                         axis=0).astype(BF16)
    row = lax.broadcasted_iota(jnp.int32, (tm, c_hy), 0)
    sw = sw_ref[...]
    sb = sb_ref[...]
    lo, hi = ROW_SL, ROW_SL + tm
    for ci, out_ref in enumerate((v_ref, x1_ref, x2_ref)):
        cols = slice(ci * c_hy, (ci + 1) * c_hy)
        zx = _dot(hx, why_ref[:, cols])
        zh = zx[lo:hi]
        zp = jnp.where(i > 0, zx[lo - 1:lo], 0.0)
        zn = jnp.where(i < n_tiles - 1, zx[hi:hi + 1], 0.0)
        z_m1 = jnp.where(row == 0, zp, pltpu.roll(zh, 1, 0))
        z_p1 = jnp.where(row == tm - 1, zn, pltpu.roll(zh, tm - 1, 0))
        out_ref[0] = z_m1 * sw[0:1, cols] + zh * sw[1:2, cols] + z_p1 * sw[2:3, cols] + sb[:, cols]
    for ci, (out_ref, mult) in enumerate(((q_ref, q_scale), (k_ref, None), (va_ref, None))):
        z = _dot(h, wqkv_ref[:, ci * c_na:(ci + 1) * c_na])
        out_ref[0] = (z if mult is None else z * mult).astype(BF16)


def _inproj(x, g, shift, scale, w_hy, w_qkv, short_w, short_b, tm=1024):
    b, s, d = x.shape
    c_hy = w_hy.shape[1] // 3
    c_na = w_qkv.shape[1] // 3
    n_tiles = s // tm
    halo_per_tile = tm // ROW_SL
    last_halo = s // ROW_SL - 1
    body = functools.partial(_inproj_body, n_tiles=n_tiles, q_scale=NA_HEAD_DIM ** -0.5,
                             c_hy=c_hy, c_na=c_na)
    tok = lambda c: pl.BlockSpec((1, tm, c), lambda bi, i: (bi, i, 0))
    full2 = lambda a: pl.BlockSpec(a.shape, lambda bi, i: (0, 0))
    resident = lambda a: pl.BlockSpec(a.shape, lambda bi, i: (0, 0), pipeline_mode=pl.Buffered(1))
    per_b = pl.BlockSpec((1, 1, d), lambda bi, i: (bi, 0, 0))
    return pl.pallas_call(
        body,
        out_shape=[jax.ShapeDtypeStruct((b, s, c_hy), F32)] * 3 + [jax.ShapeDtypeStruct((b, s, c_na), BF16)] * 3,
        grid=(b, n_tiles),
        in_specs=[tok(d),
                  pl.BlockSpec((1, ROW_SL, d), lambda bi, i: (bi, jnp.maximum(i * halo_per_tile - 1, 0), 0)),
                  pl.BlockSpec((1, ROW_SL, d),
                               lambda bi, i: (bi, jnp.minimum((i + 1) * halo_per_tile, last_halo), 0)),
                  full2(g), per_b, per_b, resident(w_hy), resident(w_qkv), full2(short_w), full2(short_b)],
        out_specs=[tok(c_hy)] * 3 + [tok(c_na)] * 3,
        compiler_params=_cparams(("parallel", "parallel"), vmem_limit=LARGE_VMEM_LIMIT),
        name="inproj",
    )(x, x, x, g, shift, scale, w_hy, w_qkv, short_w, short_b)


def _ctxkv_body(x_ref, g_ref, sh_ref, sc_ref, w_ref, k_ref, v_ref, *, c_na):
    h = _norm_mod(x_ref[0], g_ref[...], sh_ref[0], sc_ref[0]).astype(BF16)
    z = _dot(h, w_ref[...])
    k_ref[0] = z[:, 0:c_na].astype(BF16)
    v_ref[0] = z[:, c_na:2 * c_na].astype(BF16)


def _ctxkv(ctx, g, shift, scale, w_kv):
    b, n, d = ctx.shape
    c_na = w_kv.shape[1] // 2
    one = pl.BlockSpec((1, 1, d), lambda bi: (0, 0, 0))
    return pl.pallas_call(
        functools.partial(_ctxkv_body, c_na=c_na),
        out_shape=[jax.ShapeDtypeStruct((b, n, c_na), BF16)] * 2,
        grid=(b,),
        in_specs=[pl.BlockSpec((1, n, d), lambda bi: (bi, 0, 0)),
                  pl.BlockSpec(g.shape, lambda bi: (0, 0)), one, one,
                  pl.BlockSpec(w_kv.shape, lambda bi: (0, 0))],
        out_specs=[pl.BlockSpec((1, n, c_na), lambda bi: (bi, 0, 0))] * 2,
        compiler_params=_cparams(("parallel",)),
        name="ctxkv",
    )(ctx, g, shift, scale, w_kv)


NA_HEADS_PER_BLK = 8
NA_ROWS_PER_STEP = 8


def _na_bias_body(r_ref, e_ref, ok_ref, o_ref):
    t = jnp.dot(r_ref[...], e_ref[...], precision=HIGHEST, preferred_element_type=F32)
    o_ref[...] = jnp.where(ok_ref[...] > 0.5, t, NEG_INF)


def _na_bias_table(rpb):
    w = GRID_W
    h, nr, nc = rpb.shape
    col = np.arange(w)[:, None]
    kc = np.arange(w)[None, :]
    c_start = np.clip(col - NA_WIN_C // 2, 0, w - NA_WIN_C)
    valid = ((kc >= c_start) & (kc < c_start + NA_WIN_C)).reshape(1, w * w)
    nc_pad = -(-nc // ROW_SL) * ROW_SL
    expand = (np.arange(nc_pad)[:, None, None] == (kc - col + NA_WIN_C - 1)[None]).reshape(nc_pad, w * w)
    rp = jnp.pad(rpb.astype(F32).reshape(h * nr, nc), ((0, 0), (0, nc_pad - nc)))
    full = lambda a: pl.BlockSpec(a.shape, lambda: (0,) * a.ndim)
    expand = jnp.asarray(expand, dtype=F32)
    ok = jnp.asarray(valid, dtype=F32)
    toep = pl.pallas_call(
        _na_bias_body,
        out_shape=jax.ShapeDtypeStruct((h * nr, w * w), F32),
        in_specs=[full(rp), full(expand), full(ok)],
        out_specs=pl.BlockSpec((h * nr, w * w), lambda: (0, 0)),
        name="na_bias",
    )(rp, expand, ok)
    t2 = toep.reshape(h, nr, w, w).transpose(0, 2, 1, 3).reshape(h, w, nr * w)
    slabs = jnp.stack([t2[:, :, (NA_WIN_R - 1 - off) * w:(2 * NA_WIN_R - 1 - off) * w]
                       for off in range(NA_WIN_R)], axis=1)
    hpb = NA_HEADS_PER_BLK
    slabs = slabs.reshape(h // hpb, hpb, NA_WIN_R, w, NA_WIN_R * w).transpose(0, 2, 1, 3, 4)
    return slabs.reshape(h // hpb, NA_WIN_R, hpb * w, NA_WIN_R * w)


def _natt_body(q_ref, k_ref, v_ref, kc_ref, vc_ref, bias_ref, o_ref, *, rows):
    w = GRID_W
    hpb = NA_HEADS_PER_BLK
    nloc = NA_WIN_R * w
    lane = lax.broadcasted_iota(jnp.int32, (1, hpb * NA_HEAD_DIM), 1)
    in_head = [(lane >= NA_HEAD_DIM * hh) & (lane < NA_HEAD_DIM * (hh + 1)) for hh in range(hpb)]
    kcx = kc_ref[0]
    vcx = vc_ref[0]
    nt = (((1,), (1,)), ((), ()))

    def one_row(r):
        r0 = jnp.clip(r - NA_WIN_R // 2, 0, rows - NA_WIN_R)
        off = r - r0
        qs = q_ref[0, pl.ds(pl.multiple_of(r * w, w), w), :]
        kw = k_ref[0, pl.ds(pl.multiple_of(r0 * w, w), nloc), :]
        vw = v_ref[0, pl.ds(pl.multiple_of(r0 * w, w), nloc), :]
        zero = jnp.zeros_like(qs)
        qst = jnp.concatenate([jnp.where(m, qs, zero) for m in in_head], axis=0)
        s_loc = lax.dot_general(qst, kw, nt, preferred_element_type=F32) + bias_ref[0, off]
        s_ctx = lax.dot_general(qst, kcx, nt, preferred_element_type=F32)
        m = jnp.maximum(jnp.max(s_loc, axis=-1, keepdims=True), jnp.max(s_ctx, axis=-1, keepdims=True))
        p_loc = jnp.exp(s_loc - m)
        p_ctx = jnp.exp(s_ctx - m)
        den = jnp.sum(p_loc, axis=-1, keepdims=True) + jnp.sum(p_ctx, axis=-1, keepdims=True)
        o = (_dot(p_loc.astype(BF16), vw) + _dot(p_ctx.astype(BF16), vcx)) * (1.0 / den)
        acc = jnp.where(in_head[0], o[0:w], 0.0)
        for hh in range(1, hpb):
            acc = acc + jnp.where(in_head[hh], o[hh * w:(hh + 1) * w], 0.0)
        o_ref[0, pl.ds(pl.multiple_of(r * w, w), w), :] = acc.astype(BF16)

    def row_group(i, carry):
        for r in range(NA_ROWS_PER_STEP):
            one_row(NA_ROWS_PER_STEP * i + r)
        return carry

    lax.fori_loop(0, rows // NA_ROWS_PER_STEP, row_group, 0)


def _natt(q, k, v, kc, vc, bias):
    b, s, c = q.shape
    nctx = kc.shape[1]
    lw = NA_HEADS_PER_BLK * NA_HEAD_DIM
    rows = s // GRID_W
    seq = pl.BlockSpec((1, s, lw), lambda bi, g: (bi, 0, g))
    cx = pl.BlockSpec((1, nctx, lw), lambda bi, g: (bi, 0, g))
    return pl.pallas_call(
        functools.partial(_natt_body, rows=rows),
        out_shape=jax.ShapeDtypeStruct((b, s, c), BF16),
        grid=(b, c // lw),
        in_specs=[seq, seq, seq, cx, cx,
                  pl.BlockSpec((1,) + bias.shape[1:], lambda bi, g: (g, 0, 0, 0))],
        out_specs=seq,
        compiler_params=_cparams(("parallel", "parallel")),
        name="natt",
    )(q, k, v, kc, vc, bias)


def _hyena_feats(seq_len):
    t = jnp.linspace(0.0, 1.0, seq_len, dtype=F32)[:, None]
    bands = jnp.linspace(1e-4, HYENA_BANDS - 1, HYENA_BANDS, dtype=F32)
    ang = (2.0 * math.pi / seq_len) * jnp.arange(seq_len, dtype=F32)[:, None] * bands[None, :]
    feats = jnp.concatenate([t, jnp.cos(ang), -jnp.sin(ang)], axis=-1)
    return jnp.pad(feats, ((0, 0), (0, LANES - HYENA_EMB)))


def _filt_body(feat_ref, w1_ref, b1_ref, w2_ref, b2_ref, w3_ref, fr_ref, dl_ref, o_ref, l1_ref, h_scr,
               *, halves):
    j = pl.program_id(0)
    hp = functools.partial(jnp.dot, precision=HIGHEST, preferred_element_type=F32)
    feats = feat_ref[...]

    @pl.when(j == 0)
    def _():
        fr = fr_ref[...]
        h = jnp.sin(fr[0:1] * (hp(feats, w1_ref[...]) + b1_ref[...]))
        h_scr[...] = jnp.sin(fr[1:2] * (hp(h, w2_ref[...]) + b2_ref[...]))

    h2 = h_scr[...]
    h_hi = h2.astype(BF16)
    h_lo = (h2 - h_hi.astype(F32)).astype(BF16)
    w3 = w3_ref[...]
    w_hi = w3.astype(BF16)
    w_lo = (w3 - w_hi.astype(F32)).astype(BF16)
    hc = (_dot(jnp.concatenate([h_hi, h_lo], axis=1), jnp.concatenate([w_hi, w_hi], axis=0))
          + _dot(h_hi, w_lo))
    t = feats[:, 0:1]
    hc = hc * jnp.exp(-t * dl_ref[...])
    row = lax.broadcasted_iota(jnp.int32, hc.shape, 0)
    hc = jnp.where((row == 0) & ((j // halves) % 2 == 1), 0.0, hc)
    l1_ref[0] = jnp.sum(jnp.abs(hc), axis=0, keepdims=True)
    o_ref[0] = hc


def _hyena_filter_taps(seq_len, f_w1, f_b1, f_w2, f_b2, f_w3, f_freq, c_hy):
    feats = _hyena_feats(seq_len)
    hid = f_w1.shape[1]
    w1 = jnp.pad(f_w1.astype(F32), ((0, LANES - HYENA_EMB), (0, 0)))
    deltas = jnp.abs(jnp.linspace(math.log(HYENA_TARGET) / HYENA_SLOW_DECAY,
                                  math.log(HYENA_TARGET) / HYENA_FAST_DECAY, c_hy, dtype=F32))[None, :]
    nblk = f_w3.shape[1] // c_hy
    halves = 2
    cb = c_hy // halves
    c0 = lambda a: pl.BlockSpec(a.shape, lambda j: (0, 0))
    b1, b2 = f_b1.reshape(1, hid), f_b2.reshape(1, hid)
    return pl.pallas_call(
        functools.partial(_filt_body, halves=halves),
        out_shape=[jax.ShapeDtypeStruct((nblk, seq_len, c_hy), F32),
                   jax.ShapeDtypeStruct((nblk, 1, c_hy), F32)],
        grid=(nblk * halves,),
        in_specs=[c0(feats), c0(w1), c0(b1), c0(f_w2), c0(b2),
                  pl.BlockSpec((hid, cb), lambda j: (0, j)), c0(f_freq),
                  pl.BlockSpec((1, cb), lambda j: (0, j % halves))],
        out_specs=[pl.BlockSpec((1, seq_len, cb), lambda j: (j // halves, 0, j % halves)),
                   pl.BlockSpec((1, 1, cb), lambda j: (j // halves, 0, j % halves))],
        scratch_shapes=[pltpu.VMEM((seq_len, hid), F32)],
        compiler_params=_cparams(("arbitrary",)),
        name="hyena_filter",
    )(feats, w1, b1, f_w2, b2, f_w3, f_freq, deltas)


def _conv_dft_constants():
    a_half = FFT_A // 2
    n = FFT_A * FFT_R
    ka = np.arange(FFT_KA)[:, None]
    a = np.arange(a_half)[None, :]
    ph = 2.0 * np.pi * ka * a / FFT_A
    m_fwd = np.zeros((2 * FFT_KA_PAD, a_half))
    m_fwd[:FFT_KA] = np.cos(ph)
    m_fwd[FFT_KA_PAD:FFT_KA_PAD + FFT_KA] = -np.sin(ph)
    wgt = np.where((ka == 0) | (ka == FFT_A // 2), 1.0, 2.0)
    m_inv = np.zeros((a_half, 2 * FFT_KA_PAD))
    m_inv[:, :FFT_KA] = (wgt * np.cos(ph)).T / n
    m_inv[:, FFT_KA_PAD:FFT_KA_PAD + FFT_KA] = (-wgt * np.sin(ph)).T / n
    kb = np.arange(FFT_R)[None, :, None]
    b = np.arange(FFT_R)[None, None, :]
    kaa = np.arange(FFT_KA)[:, None, None]
    th = 2.0 * np.pi * (b * kb / FFT_R + b * kaa / n)
    gr, gi = np.cos(th), -np.sin(th)
    g2 = np.zeros((FFT_KA_PAD, 2 * FFT_R, 2 * FFT_R))
    g2[:FFT_KA] = np.block([[gr, -gi], [gi, gr]])
    grt, git = gr.transpose(0, 2, 1), gi.transpose(0, 2, 1)
    g2h = np.zeros_like(g2)
    g2h[:FFT_KA] = np.block([[grt, git], [-git, grt]])
    return _mxu_const(m_fwd), _mxu_const(m_inv), _mxu_const(g2), _mxu_const(g2h)


FFT_NB = 16


def _fwd1_body(m_ref, u_ref, o_ref):
    u = jnp.concatenate([u_ref[0, :, bb, :] for bb in range(FFT_NB)], axis=1).astype(BF16)
    res = _dot(m_ref[...], u)
    o_ref[0, 0] = res[0:FFT_KA_PAD]
    o_ref[0, 1] = res[FFT_KA_PAD:2 * FFT_KA_PAD]


def _conv_fwd1(u, m_fwd):
    n, seq, c = u.shape
    a_half = FFT_A // 2
    return pl.pallas_call(
        _fwd1_body,
        out_shape=jax.ShapeDtypeStruct((n, 2, FFT_KA_PAD, FFT_R * c), F32),
        grid=(n, FFT_R // FFT_NB),
        in_specs=[pl.BlockSpec(m_fwd.shape, lambda i, j: (0, 0)),
                  pl.BlockSpec((1, a_half, FFT_NB, c), lambda i, j: (i, 0, j, 0))],
        out_specs=pl.BlockSpec((1, 2, FFT_KA_PAD, FFT_NB * c), lambda i, j: (i, 0, 0, j)),
        compiler_params=_cparams(("parallel", "parallel")),
        name="conv_fwd1",
    )(m_fwd, u.reshape(n, a_half, FFT_R, c))


FFT_KB = 8


def _rows_to_slabs(src_ref, dst_scr, c):
    for part in range(2):
        for b in range(FFT_R):
            dst_scr[part, :, b, :] = src_ref[0, part, :, b * c:(b + 1) * c]


def _slabs_to_rows(src_scr, dst_ref, c):
    for part in range(2):
        for b in range(FFT_R):
            dst_ref[0, part, :, b * c:(b + 1) * c] = src_scr[part, :, b, :]


def _slab(scr, i):
    return jnp.concatenate([scr[0, i], scr[1, i]], axis=0).astype(BF16)


def _per_ka_block(j, work, clear):
    full_blocks = FFT_KA // FFT_KB
    tail = FFT_KA - full_blocks * FFT_KB

    @pl.when(j < full_blocks)
    def _():
        for i in range(FFT_KB):
            work(i)

    @pl.when(j >= full_blocks)
    def _():
        for i in range(FFT_KB):
            (work if i < tail else clear)(i)


def _fwd2f_body(sf_ref, sb_ref, g_ref, l1_ref, kf_ref, f3, b3):
    o = pl.program_id(0)
    j = pl.program_id(1)
    r2 = 2 * FFT_R
    c = kf_ref.shape[-1]
    _rows_to_slabs(sf_ref, f3, c)
    _rows_to_slabs(sb_ref, b3, c)
    inv = 1.0 / (l1_ref[2 * o] + l1_ref[2 * o + 1] + EPS)

    def spectrum(i):
        xf = _dot(g_ref[i], _slab(f3, i))
        xb = _dot(g_ref[i], _slab(b3, i))
        kf_ref[0, i, 0:FFT_R] = (xf[0:FFT_R] + xb[0:FFT_R]) * inv
        kf_ref[0, i, FFT_R:r2] = (xf[FFT_R:r2] - xb[FFT_R:r2]) * inv

    def clear(i):
        kf_ref[0, i] = jnp.zeros((r2, c), F32)

    _per_ka_block(j, spectrum, clear)


def _filter_spectrum(s_filt, l1, g2, c):
    n_ord = s_filt.shape[0] // 2
    cols = s_filt.shape[-1]
    r2 = 2 * FFT_R
    return pl.pallas_call(
        _fwd2f_body,
        out_shape=jax.ShapeDtypeStruct((n_ord, FFT_KA_PAD, r2, c), F32),
        grid=(n_ord, FFT_KA_PAD // FFT_KB),
        in_specs=[pl.BlockSpec((1, 2, FFT_KB, cols), lambda o, j: (2 * o, 0, j, 0)),
                  pl.BlockSpec((1, 2, FFT_KB, cols), lambda o, j: (2 * o + 1, 0, j, 0)),
                  pl.BlockSpec((FFT_KB, r2, r2), lambda o, j: (j, 0, 0)),
                  pl.BlockSpec(l1.shape, lambda o, j: (0, 0, 0))],
        out_specs=pl.BlockSpec((1, FFT_KB, r2, c), lambda o, j: (o, j, 0, 0)),
        scratch_shapes=[pltpu.VMEM((2, FFT_KB, FFT_R, c), F32)] * 2,
        compiler_params=_cparams(("parallel", "parallel")),
        name="filter_spectrum",
    )(s_filt, s_filt, g2, l1)


def _mid_body(s_ref, g_ref, gh_ref, kf_ref, t_ref, s3, t3):
    j = pl.program_id(1)
    r2 = 2 * FFT_R
    c = kf_ref.shape[-1]
    _rows_to_slabs(s_ref, s3, c)

    def convolve(i):
        x = _dot(g_ref[i], _slab(s3, i))
        xr, xi = x[0:FFT_R], x[FFT_R:r2]
        kr, ki = kf_ref[0, i, 0:FFT_R], kf_ref[0, i, FFT_R:r2]
        y = jnp.concatenate([xr * kr - xi * ki, xr * ki + xi * kr], axis=0).astype(BF16)
        t = _dot(gh_ref[i], y)
        t3[0, i] = t[0:FFT_R]
        t3[1, i] = t[FFT_R:r2]

    def clear(i):
        t3[0, i] = jnp.zeros((FFT_R, c), F32)
        t3[1, i] = jnp.zeros((FFT_R, c), F32)

    _per_ka_block(j, convolve, clear)
    _slabs_to_rows(t3, t_ref, c)


def _conv_mid(s, kf, order, g2, g2h, c):
    n, _, _, cols = s.shape
    r2 = 2 * FFT_R
    blk = pl.BlockSpec((1, 2, FFT_KB, cols), lambda i, j: (i, 0, j, 0))
    gspec = pl.BlockSpec((FFT_KB, r2, r2), lambda i, j: (j, 0, 0))
    return pl.pallas_call(
        _mid_body,
        out_shape=jax.ShapeDtypeStruct(s.shape, F32),
        grid=(n, FFT_KA_PAD // FFT_KB),
        in_specs=[blk, gspec, gspec,
                  pl.BlockSpec((1, FFT_KB, r2, c), lambda i, j: (order, j, 0, 0))],
        out_specs=blk,
        scratch_shapes=[pltpu.VMEM((2, FFT_KB, FFT_R, c), F32)] * 2,
        compiler_params=_cparams(("parallel", "parallel")),
        name="conv_mid",
    )(s, g2, g2h, kf)


def _inv1_body(m_ref, t_ref, u_ref, xg_ref, sk_ref, o_ref):
    c = u_ref.shape[-1]
    t2 = t_ref[0].reshape(2 * FFT_KA_PAD, FFT_NB * c).astype(BF16)
    y = _dot(m_ref[...], t2)
    for bb in range(FFT_NB):
        conv = y[:, bb * c:(bb + 1) * c] + u_ref[0, :, bb, :] * sk_ref[...]
        o_ref[0, :, bb, :] = xg_ref[0, :, bb, :] * conv


def _conv_inv1(t, u, xg, skip, m_inv):
    n, seq, c = u.shape
    a_half = FFT_A // 2
    sk = skip.astype(F32).reshape(1, c)
    uspec = pl.BlockSpec((1, a_half, FFT_NB, c), lambda i, j: (i, 0, j, 0))
    view = lambda a: a.reshape(n, a_half, FFT_R, c)
    out = pl.pallas_call(
        _inv1_body,
        out_shape=jax.ShapeDtypeStruct((n, a_half, FFT_R, c), F32),
        grid=(n, FFT_R // FFT_NB),
        in_specs=[pl.BlockSpec(m_inv.shape, lambda i, j: (0, 0)),
                  pl.BlockSpec((1, 2, FFT_KA_PAD, FFT_NB * c), lambda i, j: (i, 0, 0, j)),
                  uspec, uspec,
                  pl.BlockSpec((1, c), lambda i, j: (0, 0))],
        out_specs=uspec,
        compiler_params=_cparams(("parallel", "parallel")),
        name="conv_inv1",
    )(m_inv, t, view(u), view(xg), sk)
    return out.reshape(n, seq, c)


def _hyena(v, x1, x2, f_w1, f_b1, f_w2, f_b2, f_w3, f_freq, skip):
    _, seq, c = v.shape
    assert 2 * seq == FFT_A * FFT_R
    m_fwd, m_inv, g2, g2h = _conv_dft_constants()
    taps, l1 = _hyena_filter_taps(seq, f_w1, f_b1, f_w2, f_b2, f_w3, f_freq, c)
    kf = _filter_spectrum(_conv_fwd1(taps, m_fwd), l1, g2, c)
    y = v
    for order, xg in enumerate((x1, x2)):
        t = _conv_mid(_conv_fwd1(y, m_fwd), kf, order, g2, g2h, c)
        y = _conv_inv1(t, y, xg, skip[order], m_inv)
    return y


def _mix_ffn_body(x_ref, a1_ref, a2_ref, wm_ref, gm_ref, g_ref, sh_ref, sc_ref, gt_ref,
                  w1_ref, w3_ref, w2_ref, o_ref, *, fb):
    c1 = a1_ref.shape[-1]
    mixed = _dot(a1_ref[0].astype(BF16), wm_ref[0:c1]) + _dot(a2_ref[0].astype(BF16), wm_ref[c1:])
    xm = x_ref[0] + gm_ref[0] * mixed
    h = _norm_mod(xm, g_ref[...], sh_ref[0], sc_ref[0]).astype(BF16)
    acc = None
    for lo in range(0, w1_ref.shape[1], fb):
        a = _dot(h, w1_ref[:, lo:lo + fb])
        u = (a * jax.nn.sigmoid(a) * _dot(h, w3_ref[:, lo:lo + fb])).astype(BF16)
        part = _dot(u, w2_ref[lo:lo + fb, :])
        acc = part if acc is None else acc + part
    o_ref[0] = xm + gt_ref[0] * acc


def _mix_ffn(x, a1, a2, w_mix, gate_mix, g, shift, scale, gate, w1, w3, w2, tm=512, fb=1408):
    b, s, d = x.shape
    tok = lambda c: pl.BlockSpec((1, tm, c), lambda bi, i: (bi, i, 0))
    per_b = pl.BlockSpec((1, 1, d), lambda bi, i: (bi, 0, 0))
    const = lambda a: pl.BlockSpec(a.shape, lambda bi, i: (0, 0))
    resident = lambda a: pl.BlockSpec(a.shape, lambda bi, i: (0, 0), pipeline_mode=pl.Buffered(1))
    return pl.pallas_call(
        functools.partial(_mix_ffn_body, fb=fb),
        out_shape=jax.ShapeDtypeStruct(x.shape, F32),
        grid=(b, s // tm),
        in_specs=[tok(d), tok(a1.shape[-1]), tok(a2.shape[-1]), resident(w_mix), per_b,
                  const(g), per_b, per_b, per_b, resident(w1), resident(w3), resident(w2)],
        out_specs=tok(d),
        compiler_params=_cparams(("parallel", "parallel"), vmem_limit=LARGE_VMEM_LIMIT),
        name="mix_ffn",
    )(x, a1, a2, w_mix, gate_mix, g, shift, scale, gate, w1, w3, w2)


def _fm_constants(cg):
    j = np.arange(cg)[:, None]
    m = np.arange(cg)[None, :]
    ph = 2.0 * np.pi * j * m / cg
    w_cs = np.concatenate([np.cos(ph), np.sin(ph)], axis=1)
    d = np.arange(FM_A)[:, None]
    a = np.arange(FM_A)[None, :]
    ph = 2.0 * np.pi * d * a / FM_A
    fr, fi = np.cos(ph), -np.sin(ph)
    m1 = np.block([[fr, fi], [fi, -fr]])
    n = FM_A * FM_A
    dd = np.arange(FM_A)[:, None, None]
    c = np.arange(FM_A)[None, :, None]
    b = np.arange(FM_A)[None, None, :]
    th = 2.0 * np.pi * (b * c / FM_A + b * dd / n)
    gcat = np.concatenate([np.cos(th), np.sin(th)], axis=2)
    return _mxu_const(w_cs), _mxu_const(m1), _mxu_const(gcat)


def _fm_front_body(x_ref, g_ref, sh_ref, sc_ref, w_ref, m_ref, o_ref, *, cg, nb):
    d = x_ref.shape[-1]
    xs = jnp.concatenate([x_ref[0, :, bb, :] for bb in range(nb)], axis=0)
    h = _norm_mod(xs, g_ref[...], sh_ref[0], sc_ref[0]).astype(BF16)
    pq = [_dot(h[:, grp * cg:(grp + 1) * cg], w_ref[...]) for grp in range(d // cg)]
    p = jnp.concatenate([t[:, 0:cg] for t in pq], axis=1)
    q = jnp.concatenate([t[:, cg:2 * cg] for t in pq], axis=1)
    for bb in range(nb):
        rows = slice(bb * FM_A, (bb + 1) * FM_A)
        res = _dot(m_ref[...], jnp.concatenate([p[rows], q[rows]], axis=0).astype(BF16))
        o_ref[0, 0, :, bb, :] = res[0:FM_A]
        o_ref[0, 1, :, bb, :] = res[FM_A:2 * FM_A]


def _fm_front(x, g, shift, scale, w_cs, m1, nb=8):
    b, s, d = x.shape
    cg = w_cs.shape[0]
    per_b = pl.BlockSpec((1, 1, d), lambda bi, j: (bi, 0, 0))
    const = lambda a: pl.BlockSpec(a.shape, lambda bi, j: (0, 0))
    return pl.pallas_call(
        functools.partial(_fm_front_body, cg=cg, nb=nb),
        out_shape=jax.ShapeDtypeStruct((b, 2, FM_A, s // FM_A, d), F32),
        grid=(b, s // FM_A // nb),
        in_specs=[pl.BlockSpec((1, FM_A, nb, d), lambda bi, j: (bi, 0, j, 0)),
                  const(g), per_b, per_b, const(w_cs), const(m1)],
        out_specs=pl.BlockSpec((1, 2, FM_A, nb, d), lambda bi, j: (bi, 0, 0, j, 0)),
        compiler_params=_cparams(("parallel", "parallel")),
        name="fm_front",
    )(x.reshape(b, FM_A, s // FM_A, d), g, shift, scale, w_cs, m1)


def _fm_s2_body(s_ref, g_ref, o_ref, *, dblk, scale):
    for i in range(dblk):
        s2 = jnp.concatenate([s_ref[0, 0, i], s_ref[0, 1, i]], axis=0).astype(BF16)
        o_ref[:, i, :] = _dot(g_ref[i], s2) * scale


def _fm_stage2(sv, gcat, seq, d, dblk=8):
    b = sv.shape[0]
    scale = 1.0 / math.sqrt(seq * (d // F_GROUPS))
    out = pl.pallas_call(
        functools.partial(_fm_s2_body, dblk=dblk, scale=scale),
        out_shape=jax.ShapeDtypeStruct((b * FM_A, dblk * (FM_A // dblk), d), F32),
        grid=(b, FM_A // dblk),
        in_specs=[pl.BlockSpec((1, 2, dblk, FM_A, d), lambda bi, j: (bi, 0, j, 0, 0)),
                  pl.BlockSpec((dblk, FM_A, 2 * FM_A), lambda bi, j: (j, 0, 0))],
        out_specs=pl.BlockSpec((FM_A, dblk, d), lambda bi, j: (bi, j, 0)),
        compiler_params=_cparams(("parallel", "parallel")),
        name="fm_stage2",
    )(sv, gcat)
    return out.reshape(b, seq, d)


def _fourier_mix(x, g, shift, scale):
    b, s, d = x.shape
    assert s == FM_A * FM_A
    w_cs, m1, gcat = _fm_constants(d // F_GROUPS)
    return _fm_stage2(_fm_front(x, g, shift, scale, w_cs, m1), gcat, s, d)


MOE_TM = 1024
DMA_WINDOW = 128


def _router_body(x_ref, yf_ref, wf_ref, gf_ref, g_ref, sh_ref, sc_ref, wr_ref, br_ref,
                 xo_ref, h_ref, meta_ref, gw_ref, cnt_ref, carry):
    i = pl.program_id(0)

    @pl.when(i == 0)
    def _():
        carry[...] = jnp.zeros_like(carry)

    xm = x_ref[...] + gf_ref[0] * _dot(yf_ref[...].astype(BF16), wf_ref[...])
    xo_ref[...] = xm
    h = _norm_mod(xm, g_ref[...], sh_ref[0], sc_ref[0])
    _rows_to_tiles(h_ref, h)
    h_hi = h.astype(BF16)
    h_lo = (h - h_hi.astype(F32)).astype(BF16)
    by_hi = _dot(h_hi, wr_ref[...])
    logits = by_hi[:, 0:LANES] + by_hi[:, LANES:] + _dot(h_lo, wr_ref[:, 0:LANES]) + br_ref[...]
    lane = lax.broadcasted_iota(jnp.int32, logits.shape, 1)
    nl = logits.shape[-1]
    m1 = jnp.max(logits, axis=-1, keepdims=True)
    i1 = jnp.min(jnp.where(logits == m1, lane, nl), axis=-1, keepdims=True)
    rest = jnp.where(lane == i1, 3.0 * NEG_INF, logits)
    m2 = jnp.max(rest, axis=-1, keepdims=True)
    i2 = jnp.min(jnp.where(rest == m2, lane, nl), axis=-1, keepdims=True)
    e = jnp.exp(m2 - m1)
    gw_ref[...] = jnp.where(lane == 0, 1.0 / (1.0 + e), jnp.where(lane == 1, e / (1.0 + e), 0.0))
    onehot = jnp.where((lane == i1) | (lane == i2), 1.0, 0.0)
    tm = onehot.shape[0]
    earlier = lax.broadcasted_iota(jnp.int32, (tm, tm), 0) > lax.broadcasted_iota(jnp.int32, (tm, tm), 1)
    excl = _dot(jnp.where(earlier, 1.0, 0.0).astype(BF16), onehot.astype(BF16)) + carry[...]
    r1 = jnp.sum(jnp.where(lane == i1, excl, 0.0), axis=-1, keepdims=True).astype(jnp.int32)
    r2 = jnp.sum(jnp.where(lane == i2, excl, 0.0), axis=-1, keepdims=True).astype(jnp.int32)
    meta = jnp.where(lane == 0, i1, jnp.where(lane == 1, i2, jnp.where(lane == 2, r1, jnp.where(lane == 3, r2, 0))))
    meta_ref[...] = meta.T[0:ROW_SL]
    carry[...] = carry[...] + jnp.sum(onehot, axis=0, keepdims=True)
    cnt_ref[...] = carry[...]


def _router(x, y_f, w_f, gate_f, g, shift, scale, w_router, b_router, tm=512):
    b, s, d = x.shape
    t = b * s
    ne = w_router.shape[1]
    wr = jnp.pad(w_router.astype(F32), ((0, 0), (0, LANES - ne)))
    wr_hi = wr.astype(BF16)
    wr = jnp.concatenate([wr_hi, (wr - wr_hi.astype(F32)).astype(BF16)], axis=1)
    br = jnp.pad(b_router.astype(F32).reshape(1, ne), ((0, 0), (0, LANES - ne)), constant_values=NEG_INF)
    spt = s // tm
    per_b = pl.BlockSpec((1, 1, d), lambda i: (i // spt, 0, 0))
    const = lambda a: pl.BlockSpec(a.shape, lambda i: (0, 0))
    tok = pl.BlockSpec((tm, d), lambda i: (i, 0))
    return pl.pallas_call(
        _router_body,
        out_shape=[jax.ShapeDtypeStruct((t, d), F32),
                   jax.ShapeDtypeStruct((t * ROW_SL, LANES), F32),
                   jax.ShapeDtypeStruct((ROW_SL, t), jnp.int32),
                   jax.ShapeDtypeStruct((t, LANES), F32),
                   jax.ShapeDtypeStruct((1, LANES), F32)],
        grid=(t // tm,),
        in_specs=[tok, tok, const(w_f), per_b, const(g), per_b, per_b, const(wr), const(br)],
        out_specs=[tok,
                   pl.BlockSpec((tm * ROW_SL, LANES), lambda i: (i, 0)),
                   pl.BlockSpec((ROW_SL, tm), lambda i: (0, i)),
                   pl.BlockSpec((tm, LANES), lambda i: (i, 0)),
                   pl.BlockSpec((1, LANES), lambda i: (0, 0))],
        scratch_shapes=[pltpu.VMEM((1, LANES), F32)],
        compiler_params=_cparams(("arbitrary",)),
        name="router",
    )(x.reshape(t, d), y_f.reshape(t, d), w_f, gate_f, g, shift, scale, wr, br)


def _moe_plan(meta, counts, ne, tm):
    i1, i2, r1, r2 = meta[0], meta[1], meta[2], meta[3]
    cnt = counts[0, :ne].astype(jnp.int32)
    padded = ((cnt + tm - 1) // tm) * tm
    ends = jnp.cumsum(padded)
    offs = ends - padded
    pick = lambda idx: sum(jnp.where(idx == e, offs[e], 0) for e in range(ne))
    pos = jnp.concatenate([pick(i1) + r1, pick(i2) + r2]).astype(jnp.int32)
    n_tiles = (2 * i1.shape[0]) // tm + ne
    n_used = (ends[ne - 1] // tm).astype(jnp.int32)
    tile_start = jnp.minimum(jnp.arange(n_tiles, dtype=jnp.int32), n_used - 1) * tm
    tile_expert = jnp.sum(tile_start[:, None] >= ends[None, :], axis=1).astype(jnp.int32)
    group_end = sum(jnp.where(tile_expert == e, offs[e] + cnt[e], 0) for e in range(ne))
    tile_rows = jnp.clip(group_end - tile_start, 0, tm).astype(jnp.int32)
    return pos, offs + cnt, padded - cnt, tile_expert, n_used.reshape(1), tile_rows


def _windowed_copies(n, start_copy, wait_one):
    def body(i, carry):
        @pl.when(i >= DMA_WINDOW)
        def _():
            wait_one()
        start_copy(i)
        return carry

    lax.fori_loop(0, n, body, 0)

    def drain(i, carry):
        wait_one()
        return carry

    lax.fori_loop(0, jnp.minimum(n, DMA_WINDOW), drain, 0)


def _tile_of(ref, row):
    return ref.at[pl.ds(pl.multiple_of(row * ROW_SL, ROW_SL), ROW_SL)]


def _tiles_to_rows(ref, n, first=0):
    return jnp.concatenate([ref[pl.ds(first * ROW_SL + sl, n, stride=ROW_SL), :] for sl in range(ROW_SL)], axis=1)


def _rows_to_tiles(ref, val):
    n = val.shape[0]
    for sl in range(ROW_SL):
        ref[pl.ds(sl, n, stride=ROW_SL), :] = val[:, sl * LANES:(sl + 1) * LANES]


def _dispatch_body(pos_ref, pad_start_ref, pad_n_ref, h_ref, xs_hbm, sem, *, n_tok, ne):
    i = pl.program_id(0)
    td = h_ref.shape[0] // ROW_SL
    base = i * td
    copy = lambda src, dst: pltpu.make_async_copy(_tile_of(h_ref, src), _tile_of(xs_hbm, dst), sem)
    wait_one = lambda: copy(0, 0).wait()

    def start_token(r, carry):
        copy(r, pos_ref[base + r]).start(priority=0)
        copy(r, pos_ref[n_tok + base + r]).start(priority=1)
        return carry

    lax.fori_loop(0, td, start_token, 0, unroll=8)
    whole_tile = pltpu.make_async_copy(h_ref, xs_hbm.at[pl.ds(0, td * ROW_SL)], sem)
    whole_tile.wait()
    whole_tile.wait()

    @pl.when(i == 0)
    def _():
        for e in range(ne):
            first = pad_start_ref[e]
            _windowed_copies(pad_n_ref[e], lambda r: copy(0, first + r).start(), wait_one)


def _moe_dispatch(h3, pos, pad_start, pad_n, n_rows, td=1024):
    n_tok = h3.shape[0] // ROW_SL
    ne = pad_start.shape[0]
    return pl.pallas_call(
        functools.partial(_dispatch_body, n_tok=n_tok, ne=ne),
        out_shape=jax.ShapeDtypeStruct((n_rows * ROW_SL, LANES), h3.dtype),
        grid_spec=pltpu.PrefetchScalarGridSpec(
            num_scalar_prefetch=3, grid=(n_tok // td,),
            in_specs=[pl.BlockSpec((td * ROW_SL, LANES), lambda i, p, ps, pn: (i, 0))],
            out_specs=pl.BlockSpec(memory_space=pl.ANY),
            scratch_shapes=[pltpu.SemaphoreType.DMA(())]),
        compiler_params=_cparams(("arbitrary",)),
        name="moe_dispatch",
    )(pos, pad_start, pad_n, h3)


def _moe_grouped_body(te_ref, nu_ref, tr_ref, xs_ref, w1_ref, w3_ref, w2_ref, y_ref, xb_scr, acc_scr):
    i = pl.program_id(0)
    j = pl.program_id(1)
    tm = xb_scr.shape[0]
    hm = tm // 2

    def expert_rows(nrows):
        h = xb_scr[0:nrows]
        a = _dot(h, w1_ref[0].astype(BF16))
        u = (a * jax.nn.sigmoid(a) * _dot(h, w3_ref[0].astype(BF16))).astype(BF16)
        part = _dot(u, w2_ref[0].astype(BF16))

        @pl.when(j == 0)
        def _():
            acc_scr[0:nrows] = part

        @pl.when(j > 0)
        def _():
            acc_scr[0:nrows] += part

    @pl.when(i < nu_ref[0])
    def _():
        @pl.when(j == 0)
        def _():
            xb_scr[...] = _tiles_to_rows(xs_ref, tm).astype(BF16)

        @pl.when(tr_ref[i] > hm)
        def _():
            expert_rows(tm)

        @pl.when(tr_ref[i] <= hm)
        def _():
            expert_rows(hm)

            @pl.when(j == 0)
            def _():
                acc_scr[hm:tm] = jnp.zeros((tm - hm, acc_scr.shape[1]), F32)

        @pl.when(j == pl.num_programs(1) - 1)
        def _():
            _rows_to_tiles(y_ref, acc_scr[...])


def _moe_grouped(xs, tile_expert, n_used, tile_rows, w1, w3, w2, tm, fb=512):
    ne, d, f = w1.shape
    n_rows = xs.shape[0] // ROW_SL
    nj = f // fb
    row_tile = lambda i, j, te, nu, tr: (jnp.maximum(jnp.minimum(i, nu[0] - 1), 0), 0)
    jj = lambda i, j, nu: jnp.where(i < nu[0], j, nj - 1)
    return pl.pallas_call(
        _moe_grouped_body,
        out_shape=jax.ShapeDtypeStruct(xs.shape, F32),
        grid_spec=pltpu.PrefetchScalarGridSpec(
            num_scalar_prefetch=3, grid=(n_rows // tm, nj),
            in_specs=[pl.BlockSpec((tm * ROW_SL, LANES), row_tile),
                      pl.BlockSpec((1, d, fb), lambda i, j, te, nu, tr: (te[i], 0, jj(i, j, nu))),
                      pl.BlockSpec((1, d, fb), lambda i, j, te, nu, tr: (te[i], 0, jj(i, j, nu))),
                      pl.BlockSpec((1, fb, d), lambda i, j, te, nu, tr: (te[i], jj(i, j, nu), 0))],
            out_specs=pl.BlockSpec((tm * ROW_SL, LANES), row_tile),
            scratch_shapes=[pltpu.VMEM((tm, d), BF16), pltpu.VMEM((tm, d), F32)]),
        compiler_params=_cparams(("arbitrary", "arbitrary")),
        name="moe_grouped",
    )(tile_expert, n_used, tile_rows, xs, w1, w3, w2)


def _moe_final_body(pos_ref, x_ref, y_hbm, gw_ref, gt_ref, fg_ref, o_ref, yg_scr, sem, *, n_tok):
    i = pl.program_id(0)
    tc = x_ref.shape[0]
    slot = i % 2

    def gather_tile(step, into):
        base = step * tc

        def start_token(r, carry):
            dst = yg_scr.at[into]
            pltpu.make_async_copy(_tile_of(y_hbm, pos_ref[base + r]), _tile_of(dst, r),
                                  sem.at[into]).start(priority=0)
            pltpu.make_async_copy(_tile_of(y_hbm, pos_ref[n_tok + base + r]), _tile_of(dst, tc + r),
                                  sem.at[into]).start(priority=1)
            return carry

        lax.fori_loop(0, tc, start_token, 0, unroll=8)

    @pl.when(i == 0)
    def _():
        gather_tile(0, 0)

    @pl.when(i + 1 < pl.num_programs(0))
    def _():
        gather_tile(i + 1, 1 - slot)

    pltpu.make_async_copy(y_hbm.at[pl.ds(0, 2 * tc * ROW_SL)], yg_scr.at[slot], sem.at[slot]).wait()
    gw = gw_ref[...]
    rows = yg_scr.at[slot]
    y = gw[:, 0:1] * _tiles_to_rows(rows, tc) + gw[:, 1:2] * _tiles_to_rows(rows, tc, first=tc)
    xo = x_ref[...] + gt_ref[0] * y
    ms = jnp.mean(xo * xo, axis=-1, keepdims=True)
    o_ref[...] = xo * lax.rsqrt(ms + EPS) * fg_ref[...]


def _moe_final(x, y, pos, gw, gt, final_g, tc=512):
    b, s, d = x.shape
    t = b * s
    spt = s // tc
    out = pl.pallas_call(
        functools.partial(_moe_final_body, n_tok=t),
        out_shape=jax.ShapeDtypeStruct((t, d), F32),
        grid_spec=pltpu.PrefetchScalarGridSpec(
            num_scalar_prefetch=1, grid=(t // tc,),
            in_specs=[pl.BlockSpec((tc, d), lambda i, p: (i, 0)),
                      pl.BlockSpec(memory_space=pl.ANY),
                      pl.BlockSpec((tc, LANES), lambda i, p: (i, 0)),
                      pl.BlockSpec((1, 1, d), lambda i, p: (i // spt, 0, 0)),
                      pl.BlockSpec(final_g.shape, lambda i, p: (0, 0))],
            out_specs=pl.BlockSpec((tc, d), lambda i, p: (i, 0)),
            scratch_shapes=[pltpu.VMEM((2, 2 * tc * ROW_SL, LANES), F32), pltpu.SemaphoreType.DMA((2,))]),
        compiler_params=_cparams(("arbitrary",)),
        name="moe_final",
    )(pos, x.reshape(t, d), y, gw, gt, final_g)
    return out.reshape(b, s, d)


def _moe_routed(x, y_f, w_f, gate_f, g, shift, scale, gt, final_g, w_router, b_router, w1, w3, w2):
    ne = w1.shape[0]
    tm = MOE_TM
    x1, h3, meta, gw, counts = _router(x, y_f, w_f, gate_f, g, shift, scale, w_router, b_router)
    pos, pad_start, pad_n, tile_expert, n_used, tile_rows = _moe_plan(meta, counts, ne, tm)
    assert x.shape[-1] == ROW_SL * LANES
    n_rows = (2 * (h3.shape[0] // ROW_SL) // tm + ne) * tm
    xs = _moe_dispatch(h3, pos, pad_start, pad_n, n_rows)
    y = _moe_grouped(xs, tile_expert, n_used, tile_rows, w1, w3, w2, tm)
    return _moe_final(x1.reshape(x.shape), y, pos, gw, gt, final_g)


def kernel(x, c, ctx, c_ctx, w_ada, b_ada, norm_g, w_in, hy_short_w, hy_short_b, hy_f_w1, hy_f_b1, hy_f_w2, hy_f_b2, hy_f_w3, hy_f_freq, hy_skip, na_rpb, w_mix_out, ffn_w1, ffn_w3, ffn_w2, w_fourier, w_router, b_router, moe_w1, moe_w3, moe_w2, final_g):
    b, s, d = x.shape
    depth = w_ada.shape[0]
    assert depth == 2, "layer 0 mixes with Hyena/attention, layer 1 with Fourier/MoE"
    c_hy = hy_skip.shape[-1]
    c_na = d - c_hy

    cvec = jnp.concatenate([c, c_ctx[None, :], jnp.zeros((8 - b - 1, d), F32)], axis=0)
    mods = _ada(cvec, w_ada, b_ada)

    def mod(layer, idx, ctx_row=False):
        m = mods[layer, :, idx * d:(idx + 1) * d]
        return m[b:b + 1, None, :] if ctx_row else m[0:b, None, :]

    row = lambda a: a.reshape(1, -1)

    w_in0 = w_in[0].astype(BF16)
    w_hy, w_qkv = w_in0[:, 0:3 * c_hy], w_in0[:, 3 * c_hy:]
    v, x1, x2, q, k, va = _inproj(x, row(norm_g[0, 0]), mod(0, 0), mod(0, 1), w_hy, w_qkv,
                                  hy_short_w[0], row(hy_short_b[0]))
    kc, vc = _ctxkv(ctx, row(norm_g[0, 0]), mod(0, 0, True), mod(0, 1, True), w_qkv[:, c_na:])
    y_na = _natt(q, k, va, kc, vc, _na_bias_table(na_rpb[0]))
    y_hy = _hyena(v, x1, x2, hy_f_w1[0], hy_f_b1[0], hy_f_w2[0], hy_f_b2[0], hy_f_w3[0],
                  hy_f_freq[0], hy_skip[0])
    x = _mix_ffn(x, y_hy, y_na, w_mix_out[0].astype(BF16), mod(0, 2),
                 row(norm_g[0, 1]), mod(0, 3), mod(0, 4), mod(0, 5),
                 ffn_w1[0].astype(BF16), ffn_w3[0].astype(BF16), ffn_w2[0].astype(BF16))

    y_f = _fourier_mix(x, row(norm_g[1, 0]), mod(1, 0), mod(1, 1))
    return _moe_routed(x, y_f, w_fourier[0].astype(BF16), mod(1, 2),
                       row(norm_g[1, 1]), mod(1, 3), mod(1, 4), mod(1, 5), row(final_g),
                       w_router[0], b_router[0],
                       moe_w1[0], moe_w3[0], moe_w2[0])
```

```python
import functools
import math

import numpy as np
import jax
import jax.numpy as jnp
from jax import lax
from jax.experimental import pallas as pl
from jax.experimental.pallas import tpu as pltpu

F32 = jnp.float32
BF16 = jnp.bfloat16
HIGHEST = lax.Precision.HIGHEST

GRID_W = 64
NA_HEAD_DIM = 32
NA_WIN_R = 8
NA_WIN_C = 16
HYENA_EMB = 33
HYENA_BANDS = (HYENA_EMB - 1) // 2
HYENA_FAST_DECAY = 0.3
HYENA_SLOW_DECAY = 1.5
HYENA_TARGET = 1e-2
F_GROUPS = 4
N_MOD = 6
EPS = 1e-6
NEG_INF = -1e30

FFT_A = 64
FFT_R = 128
FFT_KA = FFT_A // 2 + 1
FFT_KA_PAD = 40
FM_A = 64

LANES = 128
ROW_SL = 8
VMEM_LIMIT = 48 * 1024 * 1024
LARGE_VMEM_LIMIT = 56 * 1024 * 1024


def _cparams(sem, vmem_limit=VMEM_LIMIT):
    return pltpu.CompilerParams(dimension_semantics=sem, vmem_limit_bytes=vmem_limit)


def _dot(a, b):
    return jnp.dot(a, b, preferred_element_type=F32)


def _mxu_const(m):
    return jnp.asarray(m, dtype=F32).astype(BF16)


def _norm_mod(x, g, shift, scale):
    ms = jnp.mean(x * x, axis=-1, keepdims=True)
    y = x * lax.rsqrt(ms + EPS) * g
    return y * (1.0 + scale) + shift


def _ada_body(c_ref, w_ref, b_ref, o_ref):
    cv = c_ref[...]
    s = cv * jax.nn.sigmoid(cv)
    o_ref[0] = jnp.dot(s, w_ref[0], precision=HIGHEST, preferred_element_type=F32) + b_ref[0]


def _ada(cvec, w_ada, b_ada):
    depth, d, n = w_ada.shape
    rows = cvec.shape[0]
    bn = n // 4
    return pl.pallas_call(
        _ada_body,
        out_shape=jax.ShapeDtypeStruct((depth, rows, n), F32),
        grid=(depth, n // bn),
        in_specs=[pl.BlockSpec((rows, d), lambda l, j: (0, 0)),
                  pl.BlockSpec((1, d, bn), lambda l, j: (l, 0, j)),
                  pl.BlockSpec((1, 1, bn), lambda l, j: (l, 0, j))],
        out_specs=pl.BlockSpec((1, rows, bn), lambda l, j: (l, 0, j)),
        compiler_params=_cparams(("parallel", "parallel")),
        name="ada",
    )(cvec, w_ada, b_ada.reshape(depth, 1, n))


def _inproj_body(x_ref, xp_ref, xn_ref, g_ref, sh_ref, sc_ref, why_ref, wqkv_ref, sw_ref, sb_ref,
                 v_ref, x1_ref, x2_ref, q_ref, k_ref, va_ref, *, n_tiles, q_scale, c_hy, c_na):
    i = pl.program_id(1)
    g, sh, sc = g_ref[...], sh_ref[0], sc_ref[0]
    hf = _norm_mod(x_ref[0], g, sh, sc)
    h = hf.astype(BF16)
    tm = hf.shape[0]
    hx = jnp.concatenate([_norm_mod(xp_ref[0], g, sh, sc), hf, _norm_mod(xn_ref[0], g, sh, sc)],
                         axis=0).astype(BF16)
    row = lax.broadcasted_iota(jnp.int32, (tm, c_hy), 0)
    sw = sw_ref[...]
    sb = sb_ref[...]
    lo, hi = ROW_SL, ROW_SL + tm
    for ci, out_ref in enumerate((v_ref, x1_ref, x2_ref)):
        cols = slice(ci * c_hy, (ci + 1) * c_hy)
        zx = _dot(hx, why_ref[:, cols])
        zh = zx[lo:hi]
        zp = jnp.where(i > 0, zx[lo - 1:lo], 0.0)
        zn = jnp.where(i < n_tiles - 1, zx[hi:hi + 1], 0.0)
        z_m1 = jnp.where(row == 0, zp, pltpu.roll(zh, 1, 0))
        z_p1 = jnp.where(row == tm - 1, zn, pltpu.roll(zh, tm - 1, 0))
        out_ref[0] = z_m1 * sw[0:1, cols] + zh * sw[1:2, cols] + z_p1 * sw[2:3, cols] + sb[:, cols]
    for ci, (out_ref, mult) in enumerate(((q_ref, q_scale), (k_ref, None), (va_ref, None))):
        z = _dot(h, wqkv_ref[:, ci * c_na:(ci + 1) * c_na])
        out_ref[0] = (z if mult is None else z * mult).astype(BF16)


def _inproj(x, g, shift, scale, w_hy, w_qkv, short_w, short_b, tm=1024):
    b, s, d = x.shape
    c_hy = w_hy.shape[1] // 3
    c_na = w_qkv.shape[1] // 3
    n_tiles = s // tm
    halo_per_tile = tm // ROW_SL
    last_halo = s // ROW_SL - 1
    body = functools.partial(_inproj_body, n_tiles=n_tiles, q_scale=NA_HEAD_DIM ** -0.5,
                             c_hy=c_hy, c_na=c_na)
    tok = lambda c: pl.BlockSpec((1, tm, c), lambda bi, i: (bi, i, 0))
    full2 = lambda a: pl.BlockSpec(a.shape, lambda bi, i: (0, 0))
    resident = lambda a: pl.BlockSpec(a.shape, lambda bi, i: (0, 0), pipeline_mode=pl.Buffered(1))
    per_b = pl.BlockSpec((1, 1, d), lambda bi, i: (bi, 0, 0))
    return pl.pallas_call(
        body,
        out_shape=[jax.ShapeDtypeStruct((b, s, c_hy), F32)] * 3 + [jax.ShapeDtypeStruct((b, s, c_na), BF16)] * 3,
        grid=(b, n_tiles),
        in_specs=[tok(d),
                  pl.BlockSpec((1, ROW_SL, d), lambda bi, i: (bi, jnp.maximum(i * halo_per_tile - 1, 0), 0)),
                  pl.BlockSpec((1, ROW_SL, d),
                               lambda bi, i: (bi, jnp.minimum((i + 1) * halo_per_tile, last_halo), 0)),
                  full2(g), per_b, per_b, resident(w_hy), resident(w_qkv), full2(short_w), full2(short_b)],
        out_specs=[tok(c_hy)] * 3 + [tok(c_na)] * 3,
        compiler_params=_cparams(("parallel", "parallel"), vmem_limit=LARGE_VMEM_LIMIT),
        name="inproj",
    )(x, x, x, g, shift, scale, w_hy, w_qkv, short_w, short_b)


def _ctxkv_body(x_ref, g_ref, sh_ref, sc_ref, w_ref, k_ref, v_ref, *, c_na):
    h = _norm_mod(x_ref[0], g_ref[...], sh_ref[0], sc_ref[0]).astype(BF16)
    z = _dot(h, w_ref[...])
    k_ref[0] = z[:, 0:c_na].astype(BF16)
    v_ref[0] = z[:, c_na:2 * c_na].astype(BF16)


def _ctxkv(ctx, g, shift, scale, w_kv):
    b, n, d = ctx.shape
    c_na = w_kv.shape[1] // 2
    one = pl.BlockSpec((1, 1, d), lambda bi: (0, 0, 0))
    return pl.pallas_call(
        functools.partial(_ctxkv_body, c_na=c_na),
        out_shape=[jax.ShapeDtypeStruct((b, n, c_na), BF16)] * 2,
        grid=(b,),
        in_specs=[pl.BlockSpec((1, n, d), lambda bi: (bi, 0, 0)),
                  pl.BlockSpec(g.shape, lambda bi: (0, 0)), one, one,
                  pl.BlockSpec(w_kv.shape, lambda bi: (0, 0))],
        out_specs=[pl.BlockSpec((1, n, c_na), lambda bi: (bi, 0, 0))] * 2,
        compiler_params=_cparams(("parallel",)),
        name="ctxkv",
    )(ctx, g, shift, scale, w_kv)


NA_HEADS_PER_BLK = 8
NA_ROWS_PER_STEP = 8


def _na_bias_body(r_ref, e_ref, ok_ref, o_ref):
    t = jnp.dot(r_ref[...], e_ref[...], precision=HIGHEST, preferred_element_type=F32)
    o_ref[...] = jnp.where(ok_ref[...] > 0.5, t, NEG_INF)


def _na_bias_table(rpb):
    w = GRID_W
    h, nr, nc = rpb.shape
    col = np.arange(w)[:, None]
    kc = np.arange(w)[None, :]
    c_start = np.clip(col - NA_WIN_C // 2, 0, w - NA_WIN_C)
    valid = ((kc >= c_start) & (kc < c_start + NA_WIN_C)).reshape(1, w * w)
    nc_pad = -(-nc // ROW_SL) * ROW_SL
    expand = (np.arange(nc_pad)[:, None, None] == (kc - col + NA_WIN_C - 1)[None]).reshape(nc_pad, w * w)
    rp = jnp.pad(rpb.astype(F32).reshape(h * nr, nc), ((0, 0), (0, nc_pad - nc)))
    full = lambda a: pl.BlockSpec(a.shape, lambda: (0,) * a.ndim)
    expand = jnp.asarray(expand, dtype=F32)
    ok = jnp.asarray(valid, dtype=F32)
    toep = pl.pallas_call(
        _na_bias_body,
        out_shape=jax.ShapeDtypeStruct((h * nr, w * w), F32),
        in_specs=[full(rp), full(expand), full(ok)],
        out_specs=pl.BlockSpec((h * nr, w * w), lambda: (0, 0)),
        name="na_bias",
    )(rp, expand, ok)
    t2 = toep.reshape(h, nr, w, w).transpose(0, 2, 1, 3).reshape(h, w, nr * w)
    slabs = jnp.stack([t2[:, :, (NA_WIN_R - 1 - off) * w:(2 * NA_WIN_R - 1 - off) * w]
                       for off in range(NA_WIN_R)], axis=1)
    hpb = NA_HEADS_PER_BLK
    slabs = slabs.reshape(h // hpb, hpb, NA_WIN_R, w, NA_WIN_R * w).transpose(0, 2, 1, 3, 4)
    return slabs.reshape(h // hpb, NA_WIN_R, hpb * w, NA_WIN_R * w)


def _natt_body(q_ref, k_ref, v_ref, kc_ref, vc_ref, bias_ref, o_ref, *, rows):
    w = GRID_W
    hpb = NA_HEADS_PER_BLK
    nloc = NA_WIN_R * w
    lane = lax.broadcasted_iota(jnp.int32, (1, hpb * NA_HEAD_DIM), 1)
    in_head = [(lane >= NA_HEAD_DIM * hh) & (lane < NA_HEAD_DIM * (hh + 1)) for hh in range(hpb)]
    kcx = kc_ref[0]
    vcx = vc_ref[0]
    nt = (((1,), (1,)), ((), ()))

    def one_row(r):
        r0 = jnp.clip(r - NA_WIN_R // 2, 0, rows - NA_WIN_R)
        off = r - r0
        qs = q_ref[0, pl.ds(pl.multiple_of(r * w, w), w), :]
        kw = k_ref[0, pl.ds(pl.multiple_of(r0 * w, w), nloc), :]
        vw = v_ref[0, pl.ds(pl.multiple_of(r0 * w, w), nloc), :]
        zero = jnp.zeros_like(qs)
        qst = jnp.concatenate([jnp.where(m, qs, zero) for m in in_head], axis=0)
        s_loc = lax.dot_general(qst, kw, nt, preferred_element_type=F32) + bias_ref[0, off]
        s_ctx = lax.dot_general(qst, kcx, nt, preferred_element_type=F32)
        m = jnp.maximum(jnp.max(s_loc, axis=-1, keepdims=True), jnp.max(s_ctx, axis=-1, keepdims=True))
        p_loc = jnp.exp(s_loc - m)
        p_ctx = jnp.exp(s_ctx - m)
        den = jnp.sum(p_loc, axis=-1, keepdims=True) + jnp.sum(p_ctx, axis=-1, keepdims=True)
        o = (_dot(p_loc.astype(BF16), vw) + _dot(p_ctx.astype(BF16), vcx)) * (1.0 / den)
        acc = jnp.where(in_head[0], o[0:w], 0.0)
        for hh in range(1, hpb):
            acc = acc + jnp.where(in_head[hh], o[hh * w:(hh + 1) * w], 0.0)
        o_ref[0, pl.ds(pl.multiple_of(r * w, w), w), :] = acc.astype(BF16)

    def row_group(i, carry):
        for r in range(NA_ROWS_PER_STEP):
            one_row(NA_ROWS_PER_STEP * i + r)
        return carry

    lax.fori_loop(0, rows // NA_ROWS_PER_STEP, row_group, 0)


def _natt(q, k, v, kc, vc, bias):
    b, s, c = q.shape
    nctx = kc.shape[1]
    lw = NA_HEADS_PER_BLK * NA_HEAD_DIM
    rows = s // GRID_W
    seq = pl.BlockSpec((1, s, lw), lambda bi, g: (bi, 0, g))
    cx = pl.BlockSpec((1, nctx, lw), lambda bi, g: (bi, 0, g))
    return pl.pallas_call(
        functools.partial(_natt_body, rows=rows),
        out_shape=jax.ShapeDtypeStruct((b, s, c), BF16),
        grid=(b, c // lw),
        in_specs=[seq, seq, seq, cx, cx,
                  pl.BlockSpec((1,) + bias.shape[1:], lambda bi, g: (g, 0, 0, 0))],
        out_specs=seq,
        compiler_params=_cparams(("parallel", "parallel")),
        name="natt",
    )(q, k, v, kc, vc, bias)


def _hyena_feats(seq_len):
    t = jnp.linspace(0.0, 1.0, seq_len, dtype=F32)[:, None]
    bands = jnp.linspace(1e-4, HYENA_BANDS - 1, HYENA_BANDS, dtype=F32)
    ang = (2.0 * math.pi / seq_len) * jnp.arange(seq_len, dtype=F32)[:, None] * bands[None, :]
    feats = jnp.concatenate([t, jnp.cos(ang), -jnp.sin(ang)], axis=-1)
    return jnp.pad(feats, ((0, 0), (0, LANES - HYENA_EMB)))


def _filt_body(feat_ref, w1_ref, b1_ref, w2_ref, b2_ref, w3_ref, fr_ref, dl_ref, o_ref, l1_ref, h_scr,
               *, halves):
    j = pl.program_id(0)
    hp = functools.partial(jnp.dot, precision=HIGHEST, preferred_element_type=F32)
    feats = feat_ref[...]

    @pl.when(j == 0)
    def _():
        fr = fr_ref[...]
        h = jnp.sin(fr[0:1] * (hp(feats, w1_ref[...]) + b1_ref[...]))
        h_scr[...] = jnp.sin(fr[1:2] * (hp(h, w2_ref[...]) + b2_ref[...]))

    h2 = h_scr[...]
    h_hi = h2.astype(BF16)
    h_lo = (h2 - h_hi.astype(F32)).astype(BF16)
    w3 = w3_ref[...]
    w_hi = w3.astype(BF16)
    w_lo = (w3 - w_hi.astype(F32)).astype(BF16)
    hc = (_dot(jnp.concatenate([h_hi, h_lo], axis=1), jnp.concatenate([w_hi, w_hi], axis=0))
          + _dot(h_hi, w_lo))
    t = feats[:, 0:1]
    hc = hc * jnp.exp(-t * dl_ref[...])
    row = lax.broadcasted_iota(jnp.int32, hc.shape, 0)
    hc = jnp.where((row == 0) & ((j // halves) % 2 == 1), 0.0, hc)
    l1_ref[0] = jnp.sum(jnp.abs(hc), axis=0, keepdims=True)
    o_ref[0] = hc


def _hyena_filter_taps(seq_len, f_w1, f_b1, f_w2, f_b2, f_w3, f_freq, c_hy):
    feats = _hyena_feats(seq_len)
    hid = f_w1.shape[1]
    w1 = jnp.pad(f_w1.astype(F32), ((0, LANES - HYENA_EMB), (0, 0)))
    deltas = jnp.abs(jnp.linspace(math.log(HYENA_TARGET) / HYENA_SLOW_DECAY,
                                  math.log(HYENA_TARGET) / HYENA_FAST_DECAY, c_hy, dtype=F32))[None, :]
    nblk = f_w3.shape[1] // c_hy
    halves = 2
    cb = c_hy // halves
    c0 = lambda a: pl.BlockSpec(a.shape, lambda j: (0, 0))
    b1, b2 = f_b1.reshape(1, hid), f_b2.reshape(1, hid)
    return pl.pallas_call(
        functools.partial(_filt_body, halves=halves),
        out_shape=[jax.ShapeDtypeStruct((nblk, seq_len, c_hy), F32),
                   jax.ShapeDtypeStruct((nblk, 1, c_hy), F32)],
        grid=(nblk * halves,),
        in_specs=[c0(feats), c0(w1), c0(b1), c0(f_w2), c0(b2),
                  pl.BlockSpec((hid, cb), lambda j: (0, j)), c0(f_freq),
                  pl.BlockSpec((1, cb), lambda j: (0, j % halves))],
        out_specs=[pl.BlockSpec((1, seq_len, cb), lambda j: (j // halves, 0, j % halves)),
                   pl.BlockSpec((1, 1, cb), lambda j: (j // halves, 0, j % halves))],
        scratch_shapes=[pltpu.VMEM((seq_len, hid), F32)],
        compiler_params=_cparams(("arbitrary",)),
        name="hyena_filter",
    )(feats, w1, b1, f_w2, b2, f_w3, f_freq, deltas)


def _conv_dft_constants():
    a_half = FFT_A // 2
    n = FFT_A * FFT_R
    ka = np.arange(FFT_KA)[:, None]
    a = np.arange(a_half)[None, :]
    ph = 2.0 * np.pi * ka * a / FFT_A
    m_fwd = np.zeros((2 * FFT_KA_PAD, a_half))
    m_fwd[:FFT_KA] = np.cos(ph)
    m_fwd[FFT_KA_PAD:FFT_KA_PAD + FFT_KA] = -np.sin(ph)
    wgt = np.where((ka == 0) | (ka == FFT_A // 2), 1.0, 2.0)
    m_inv = np.zeros((a_half, 2 * FFT_KA_PAD))
    m_inv[:, :FFT_KA] = (wgt * np.cos(ph)).T / n
    m_inv[:, FFT_KA_PAD:FFT_KA_PAD + FFT_KA] = (-wgt * np.sin(ph)).T / n
    kb = np.arange(FFT_R)[None, :, None]
    b = np.arange(FFT_R)[None, None, :]
    kaa = np.arange(FFT_KA)[:, None, None]
    th = 2.0 * np.pi * (b * kb / FFT_R + b * kaa / n)
    gr, gi = np.cos(th), -np.sin(th)
    g2 = np.zeros((FFT_KA_PAD, 2 * FFT_R, 2 * FFT_R))
    g2[:FFT_KA] = np.block([[gr, -gi], [gi, gr]])
    grt, git = gr.transpose(0, 2, 1), gi.transpose(0, 2, 1)
    g2h = np.zeros_like(g2)
    g2h[:FFT_KA] = np.block([[grt, git], [-git, grt]])
    return _mxu_const(m_fwd), _mxu_const(m_inv), _mxu_const(g2), _mxu_const(g2h)


FFT_NB = 32


def _fwd1_body(m_ref, u_ref, o_ref):
    u = jnp.concatenate([u_ref[0, :, bb, :] for bb in range(FFT_NB)], axis=1).astype(BF16)
    res = _dot(m_ref[...], u)
    o_ref[0, 0] = res[0:FFT_KA_PAD]
    o_ref[0, 1] = res[FFT_KA_PAD:2 * FFT_KA_PAD]


def _conv_fwd1(u, m_fwd):
    n, seq, c = u.shape
    a_half = FFT_A // 2
    return pl.pallas_call(
        _fwd1_body,
        out_shape=jax.ShapeDtypeStruct((n, 2, FFT_KA_PAD, FFT_R * c), F32),
        grid=(n, FFT_R // FFT_NB),
        in_specs=[pl.BlockSpec(m_fwd.shape, lambda i, j: (0, 0)),
                  pl.BlockSpec((1, a_half, FFT_NB, c), lambda i, j: (i, 0, j, 0))],
        out_specs=pl.BlockSpec((1, 2, FFT_KA_PAD, FFT_NB * c), lambda i, j: (i, 0, 0, j)),
        compiler_params=_cparams(("parallel", "parallel")),
        name="conv_fwd1",
    )(m_fwd, u.reshape(n, a_half, FFT_R, c))


FFT_KB = 8


def _rows_to_slabs(src_ref, dst_scr, c):
    for part in range(2):
        for b in range(FFT_R):
            dst_scr[part, :, b, :] = src_ref[0, part, :, b * c:(b + 1) * c]


def _slabs_to_rows(src_scr, dst_ref, c):
    for part in range(2):
        for b in range(FFT_R):
            dst_ref[0, part, :, b * c:(b + 1) * c] = src_scr[part, :, b, :]


def _slab(scr, i):
    return jnp.concatenate([scr[0, i], scr[1, i]], axis=0).astype(BF16)


def _per_ka_block(j, work, clear):
    full_blocks = FFT_KA // FFT_KB
    tail = FFT_KA - full_blocks * FFT_KB

    @pl.when(j < full_blocks)
    def _():
        for i in range(FFT_KB):
            work(i)

    @pl.when(j >= full_blocks)
    def _():
        for i in range(FFT_KB):
            (work if i < tail else clear)(i)


def _fwd2f_body(sf_ref, sb_ref, g_ref, l1_ref, kf_ref, f3, b3):
    o = pl.program_id(0)
    j = pl.program_id(1)
    r2 = 2 * FFT_R
    c = kf_ref.shape[-1]
    _rows_to_slabs(sf_ref, f3, c)
    _rows_to_slabs(sb_ref, b3, c)
    inv = 1.0 / (l1_ref[2 * o] + l1_ref[2 * o + 1] + EPS)

    def spectrum(i):
        xf = _dot(g_ref[i], _slab(f3, i))
        xb = _dot(g_ref[i], _slab(b3, i))
        kf_ref[0, i, 0:FFT_R] = (xf[0:FFT_R] + xb[0:FFT_R]) * inv
        kf_ref[0, i, FFT_R:r2] = (xf[FFT_R:r2] - xb[FFT_R:r2]) * inv

    def clear(i):
        kf_ref[0, i] = jnp.zeros((r2, c), F32)

    _per_ka_block(j, spectrum, clear)


def _filter_spectrum(s_filt, l1, g2, c):
    n_ord = s_filt.shape[0] // 2
    cols = s_filt.shape[-1]
    r2 = 2 * FFT_R
    return pl.pallas_call(
        _fwd2f_body,
        out_shape=jax.ShapeDtypeStruct((n_ord, FFT_KA_PAD, r2, c), F32),
        grid=(n_ord, FFT_KA_PAD // FFT_KB),
        in_specs=[pl.BlockSpec((1, 2, FFT_KB, cols), lambda o, j: (2 * o, 0, j, 0)),
                  pl.BlockSpec((1, 2, FFT_KB, cols), lambda o, j: (2 * o + 1, 0, j, 0)),
                  pl.BlockSpec((FFT_KB, r2, r2), lambda o, j: (j, 0, 0)),
                  pl.BlockSpec(l1.shape, lambda o, j: (0, 0, 0))],
        out_specs=pl.BlockSpec((1, FFT_KB, r2, c), lambda o, j: (o, j, 0, 0)),
        scratch_shapes=[pltpu.VMEM((2, FFT_KB, FFT_R, c), F32)] * 2,
        compiler_params=_cparams(("parallel", "parallel")),
        name="filter_spectrum",
    )(s_filt, s_filt, g2, l1)


def _mid_body(s_ref, g_ref, gh_ref, kf_ref, t_ref, s3, t3):
    j = pl.program_id(1)
    r2 = 2 * FFT_R
    c = kf_ref.shape[-1]
    _rows_to_slabs(s_ref, s3, c)

    def convolve(i):
        x = _dot(g_ref[i], _slab(s3, i))
        xr, xi = x[0:FFT_R], x[FFT_R:r2]
        kr, ki = kf_ref[0, i, 0:FFT_R], kf_ref[0, i, FFT_R:r2]
        y = jnp.concatenate([xr * kr - xi * ki, xr * ki + xi * kr], axis=0).astype(BF16)
        t = _dot(gh_ref[i], y)
        t3[0, i] = t[0:FFT_R]
        t3[1, i] = t[FFT_R:r2]

    def clear(i):
        t3[0, i] = jnp.zeros((FFT_R, c), F32)
        t3[1, i] = jnp.zeros((FFT_R, c), F32)

    _per_ka_block(j, convolve, clear)
    _slabs_to_rows(t3, t_ref, c)


def _conv_mid(s, kf, order, g2, g2h, c):
    n, _, _, cols = s.shape
    r2 = 2 * FFT_R
    blk = pl.BlockSpec((1, 2, FFT_KB, cols), lambda i, j: (i, 0, j, 0))
    gspec = pl.BlockSpec((FFT_KB, r2, r2), lambda i, j: (j, 0, 0))
    return pl.pallas_call(
        _mid_body,
        out_shape=jax.ShapeDtypeStruct(s.shape, F32),
        grid=(n, FFT_KA_PAD // FFT_KB),
        in_specs=[blk, gspec, gspec,
                  pl.BlockSpec((1, FFT_KB, r2, c), lambda i, j: (order, j, 0, 0))],
        out_specs=blk,
        scratch_shapes=[pltpu.VMEM((2, FFT_KB, FFT_R, c), F32)] * 2,
        compiler_params=_cparams(("parallel", "parallel")),
        name="conv_mid",
    )(s, g2, g2h, kf)


def _inv1_body(m_ref, t_ref, u_ref, xg_ref, sk_ref, o_ref):
    c = u_ref.shape[-1]
    t2 = t_ref[0].reshape(2 * FFT_KA_PAD, FFT_NB * c).astype(BF16)
    y = _dot(m_ref[...], t2)
    for bb in range(FFT_NB):
        conv = y[:, bb * c:(bb + 1) * c] + u_ref[0, :, bb, :] * sk_ref[...]
        o_ref[0, :, bb, :] = xg_ref[0, :, bb, :] * conv


def _conv_inv1(t, u, xg, skip, m_inv):
    n, seq, c = u.shape
    a_half = FFT_A // 2
    sk = skip.astype(F32).reshape(1, c)
    uspec = pl.BlockSpec((1, a_half, FFT_NB, c), lambda i, j: (i, 0, j, 0))
    view = lambda a: a.reshape(n, a_half, FFT_R, c)
    out = pl.pallas_call(
        _inv1_body,
        out_shape=jax.ShapeDtypeStruct((n, a_half, FFT_R, c), F32),
        grid=(n, FFT_R // FFT_NB),
        in_specs=[pl.BlockSpec(m_inv.shape, lambda i, j: (0, 0)),
                  pl.BlockSpec((1, 2, FFT_KA_PAD, FFT_NB * c), lambda i, j: (i, 0, 0, j)),
                  uspec, uspec,
                  pl.BlockSpec((1, c), lambda i, j: (0, 0))],
        out_specs=uspec,
        compiler_params=_cparams(("parallel", "parallel")),
        name="conv_inv1",
    )(m_inv, t, view(u), view(xg), sk)
    return out.reshape(n, seq, c)


def _hyena(v, x1, x2, f_w1, f_b1, f_w2, f_b2, f_w3, f_freq, skip):
    _, seq, c = v.shape
    assert 2 * seq == FFT_A * FFT_R
    m_fwd, m_inv, g2, g2h = _conv_dft_constants()
    taps, l1 = _hyena_filter_taps(seq, f_w1, f_b1, f_w2, f_b2, f_w3, f_freq, c)
    kf = _filter_spectrum(_conv_fwd1(taps, m_fwd), l1, g2, c)
    y = v
    for order, xg in enumerate((x1, x2)):
        t = _conv_mid(_conv_fwd1(y, m_fwd), kf, order, g2, g2h, c)
        y = _conv_inv1(t, y, xg, skip[order], m_inv)
    return y


def _mix_ffn_body(x_ref, a1_ref, a2_ref, wm_ref, gm_ref, g_ref, sh_ref, sc_ref, gt_ref,
                  w1_ref, w3_ref, w2_ref, o_ref, *, fb):
    c1 = a1_ref.shape[-1]
    mixed = _dot(a1_ref[0].astype(BF16), wm_ref[0:c1]) + _dot(a2_ref[0].astype(BF16), wm_ref[c1:])
    xm = x_ref[0] + gm_ref[0] * mixed
    h = _norm_mod(xm, g_ref[...], sh_ref[0], sc_ref[0]).astype(BF16)
    acc = None
    for lo in range(0, w1_ref.shape[1], fb):
        a = _dot(h, w1_ref[:, lo:lo + fb])
        u = (a * jax.nn.sigmoid(a) * _dot(h, w3_ref[:, lo:lo + fb])).astype(BF16)
        part = _dot(u, w2_ref[lo:lo + fb, :])
        acc = part if acc is None else acc + part
    o_ref[0] = xm + gt_ref[0] * acc


def _mix_ffn(x, a1, a2, w_mix, gate_mix, g, shift, scale, gate, w1, w3, w2, tm=512, fb=1408):
    b, s, d = x.shape
    tok = lambda c: pl.BlockSpec((1, tm, c), lambda bi, i: (bi, i, 0))
    per_b = pl.BlockSpec((1, 1, d), lambda bi, i: (bi, 0, 0))
    const = lambda a: pl.BlockSpec(a.shape, lambda bi, i: (0, 0))
    resident = lambda a: pl.BlockSpec(a.shape, lambda bi, i: (0, 0), pipeline_mode=pl.Buffered(1))
    return pl.pallas_call(
        functools.partial(_mix_ffn_body, fb=fb),
        out_shape=jax.ShapeDtypeStruct(x.shape, F32),
        grid=(b, s // tm),
        in_specs=[tok(d), tok(a1.shape[-1]), tok(a2.shape[-1]), resident(w_mix), per_b,
                  const(g), per_b, per_b, per_b, resident(w1), resident(w3), resident(w2)],
        out_specs=tok(d),
        compiler_params=_cparams(("parallel", "parallel"), vmem_limit=LARGE_VMEM_LIMIT),
        name="mix_ffn",
    )(x, a1, a2, w_mix, gate_mix, g, shift, scale, gate, w1, w3, w2)


def _fm_constants(cg):
    j = np.arange(cg)[:, None]
    m = np.arange(cg)[None, :]
    ph = 2.0 * np.pi * j * m / cg
    w_cs = np.concatenate([np.cos(ph), np.sin(ph)], axis=1)
    d = np.arange(FM_A)[:, None]
    a = np.arange(FM_A)[None, :]
    ph = 2.0 * np.pi * d * a / FM_A
    fr, fi = np.cos(ph), -np.sin(ph)
    m1 = np.block([[fr, fi], [fi, -fr]])
    n = FM_A * FM_A
    dd = np.arange(FM_A)[:, None, None]
    c = np.arange(FM_A)[None, :, None]
    b = np.arange(FM_A)[None, None, :]
    th = 2.0 * np.pi * (b * c / FM_A + b * dd / n)
    gcat = np.concatenate([np.cos(th), np.sin(th)], axis=2)
    return _mxu_const(w_cs), _mxu_const(m1), _mxu_const(gcat)


def _fm_front_body(x_ref, g_ref, sh_ref, sc_ref, w_ref, m_ref, o_ref, *, cg, nb):
    d = x_ref.shape[-1]
    xs = jnp.concatenate([x_ref[0, :, bb, :] for bb in range(nb)], axis=0)
    h = _norm_mod(xs, g_ref[...], sh_ref[0], sc_ref[0]).astype(BF16)
    pq = [_dot(h[:, grp * cg:(grp + 1) * cg], w_ref[...]) for grp in range(d // cg)]
    p = jnp.concatenate([t[:, 0:cg] for t in pq], axis=1)
    q = jnp.concatenate([t[:, cg:2 * cg] for t in pq], axis=1)
    for bb in range(nb):
        rows = slice(bb * FM_A, (bb + 1) * FM_A)
        res = _dot(m_ref[...], jnp.concatenate([p[rows], q[rows]], axis=0).astype(BF16))
        o_ref[0, 0, :, bb, :] = res[0:FM_A]
        o_ref[0, 1, :, bb, :] = res[FM_A:2 * FM_A]


def _fm_front(x, g, shift, scale, w_cs, m1, nb=8):
    b, s, d = x.shape
    cg = w_cs.shape[0]
    per_b = pl.BlockSpec((1, 1, d), lambda bi, j: (bi, 0, 0))
    const = lambda a: pl.BlockSpec(a.shape, lambda bi, j: (0, 0))
    return pl.pallas_call(
        functools.partial(_fm_front_body, cg=cg, nb=nb),
        out_shape=jax.ShapeDtypeStruct((b, 2, FM_A, s // FM_A, d), F32),
        grid=(b, s // FM_A // nb),
        in_specs=[pl.BlockSpec((1, FM_A, nb, d), lambda bi, j: (bi, 0, j, 0)),
                  const(g), per_b, per_b, const(w_cs), const(m1)],
        out_specs=pl.BlockSpec((1, 2, FM_A, nb, d), lambda bi, j: (bi, 0, 0, j, 0)),
        compiler_params=_cparams(("parallel", "parallel")),
        name="fm_front",
    )(x.reshape(b, FM_A, s // FM_A, d), g, shift, scale, w_cs, m1)


def _fm_s2_body(s_ref, g_ref, o_ref, *, dblk, scale):
    for i in range(dblk):
        s2 = jnp.concatenate([s_ref[0, 0, i], s_ref[0, 1, i]], axis=0).astype(BF16)
        o_ref[:, i, :] = _dot(g_ref[i], s2) * scale


def _fm_stage2(sv, gcat, seq, d, dblk=8):
    b = sv.shape[0]
    scale = 1.0 / math.sqrt(seq * (d // F_GROUPS))
    out = pl.pallas_call(
        functools.partial(_fm_s2_body, dblk=dblk, scale=scale),
        out_shape=jax.ShapeDtypeStruct((b * FM_A, dblk * (FM_A // dblk), d), F32),
        grid=(b, FM_A // dblk),
        in_specs=[pl.BlockSpec((1, 2, dblk, FM_A, d), lambda bi, j: (bi, 0, j, 0, 0)),
                  pl.BlockSpec((dblk, FM_A, 2 * FM_A), lambda bi, j: (j, 0, 0))],
        out_specs=pl.BlockSpec((FM_A, dblk, d), lambda bi, j: (bi, j, 0)),
        compiler_params=_cparams(("parallel", "parallel")),
        name="fm_stage2",
    )(sv, gcat)
    return out.reshape(b, seq, d)


def _fourier_mix(x, g, shift, scale):
    b, s, d = x.shape
    assert s == FM_A * FM_A
    w_cs, m1, gcat = _fm_constants(d // F_GROUPS)
    return _fm_stage2(_fm_front(x, g, shift, scale, w_cs, m1), gcat, s, d)


MOE_TM = 1024
DMA_WINDOW = 128


def _router_body(x_ref, yf_ref, wf_ref, gf_ref, g_ref, sh_ref, sc_ref, wr_ref, br_ref,
                 xo_ref, h_ref, meta_ref, gw_ref, cnt_ref, carry):
    i = pl.program_id(0)

    @pl.when(i == 0)
    def _():
        carry[...] = jnp.zeros_like(carry)

    xm = x_ref[...] + gf_ref[0] * _dot(yf_ref[...].astype(BF16), wf_ref[...])
    xo_ref[...] = xm
    h = _norm_mod(xm, g_ref[...], sh_ref[0], sc_ref[0])
    _rows_to_tiles(h_ref, h)
    h_hi = h.astype(BF16)
    h_lo = (h - h_hi.astype(F32)).astype(BF16)
    by_hi = _dot(h_hi, wr_ref[...])
    logits = by_hi[:, 0:LANES] + by_hi[:, LANES:] + _dot(h_lo, wr_ref[:, 0:LANES]) + br_ref[...]
    lane = lax.broadcasted_iota(jnp.int32, logits.shape, 1)
    nl = logits.shape[-1]
    m1 = jnp.max(logits, axis=-1, keepdims=True)
    i1 = jnp.min(jnp.where(logits == m1, lane, nl), axis=-1, keepdims=True)
    rest = jnp.where(lane == i1, 3.0 * NEG_INF, logits)
    m2 = jnp.max(rest, axis=-1, keepdims=True)
    i2 = jnp.min(jnp.where(rest == m2, lane, nl), axis=-1, keepdims=True)
    e = jnp.exp(m2 - m1)
    gw_ref[...] = jnp.where(lane == 0, 1.0 / (1.0 + e), jnp.where(lane == 1, e / (1.0 + e), 0.0))
    onehot = jnp.where((lane == i1) | (lane == i2), 1.0, 0.0)
    tm = onehot.shape[0]
    earlier = lax.broadcasted_iota(jnp.int32, (tm, tm), 0) > lax.broadcasted_iota(jnp.int32, (tm, tm), 1)
    excl = _dot(jnp.where(earlier, 1.0, 0.0).astype(BF16), onehot.astype(BF16)) + carry[...]
    r1 = jnp.sum(jnp.where(lane == i1, excl, 0.0), axis=-1, keepdims=True).astype(jnp.int32)
    r2 = jnp.sum(jnp.where(lane == i2, excl, 0.0), axis=-1, keepdims=True).astype(jnp.int32)
    meta = jnp.where(lane == 0, i1, jnp.where(lane == 1, i2, jnp.where(lane == 2, r1, jnp.where(lane == 3, r2, 0))))
    meta_ref[...] = meta.T[0:ROW_SL]
    carry[...] = carry[...] + jnp.sum(onehot, axis=0, keepdims=True)
    cnt_ref[...] = carry[...]


def _router(x, y_f, w_f, gate_f, g, shift, scale, w_router, b_router, tm=512):
    b, s, d = x.shape
    t = b * s
    ne = w_router.shape[1]
    wr = jnp.pad(w_router.astype(F32), ((0, 0), (0, LANES - ne)))
    wr_hi = wr.astype(BF16)
    wr = jnp.concatenate([wr_hi, (wr - wr_hi.astype(F32)).astype(BF16)], axis=1)
    br = jnp.pad(b_router.astype(F32).reshape(1, ne), ((0, 0), (0, LANES - ne)), constant_values=NEG_INF)
    spt = s // tm
    per_b = pl.BlockSpec((1, 1, d), lambda i: (i // spt, 0, 0))
    const = lambda a: pl.BlockSpec(a.shape, lambda i: (0, 0))
    tok = pl.BlockSpec((tm, d), lambda i: (i, 0))
    return pl.pallas_call(
        _router_body,
        out_shape=[jax.ShapeDtypeStruct((t, d), F32),
                   jax.ShapeDtypeStruct((t * ROW_SL, LANES), F32),
                   jax.ShapeDtypeStruct((ROW_SL, t), jnp.int32),
                   jax.ShapeDtypeStruct((t, LANES), F32),
                   jax.ShapeDtypeStruct((1, LANES), F32)],
        grid=(t // tm,),
        in_specs=[tok, tok, const(w_f), per_b, const(g), per_b, per_b, const(wr), const(br)],
        out_specs=[tok,
                   pl.BlockSpec((tm * ROW_SL, LANES), lambda i: (i, 0)),
                   pl.BlockSpec((ROW_SL, tm), lambda i: (0, i)),
                   pl.BlockSpec((tm, LANES), lambda i: (i, 0)),
                   pl.BlockSpec((1, LANES), lambda i: (0, 0))],
        scratch_shapes=[pltpu.VMEM((1, LANES), F32)],
        compiler_params=_cparams(("arbitrary",)),
        name="router",
    )(x.reshape(t, d), y_f.reshape(t, d), w_f, gate_f, g, shift, scale, wr, br)


def _moe_plan(meta, counts, ne, tm):
    i1, i2, r1, r2 = meta[0], meta[1], meta[2], meta[3]
    cnt = counts[0, :ne].astype(jnp.int32)
    padded = ((cnt + tm - 1) // tm) * tm
    ends = jnp.cumsum(padded)
    offs = ends - padded
    pick = lambda idx: sum(jnp.where(idx == e, offs[e], 0) for e in range(ne))
    pos = jnp.concatenate([pick(i1) + r1, pick(i2) + r2]).astype(jnp.int32)
    n_tiles = (2 * i1.shape[0]) // tm + ne
    n_used = (ends[ne - 1] // tm).astype(jnp.int32)
    tile_start = jnp.minimum(jnp.arange(n_tiles, dtype=jnp.int32), n_used - 1) * tm
    tile_expert = jnp.sum(tile_start[:, None] >= ends[None, :], axis=1).astype(jnp.int32)
    group_end = sum(jnp.where(tile_expert == e, offs[e] + cnt[e], 0) for e in range(ne))
    tile_rows = jnp.clip(group_end - tile_start, 0, tm).astype(jnp.int32)
    return pos, offs + cnt, padded - cnt, tile_expert, n_used.reshape(1), tile_rows


def _windowed_copies(n, start_copy, wait_one):
    def body(i, carry):
        @pl.when(i >= DMA_WINDOW)
        def _():
            wait_one()
        start_copy(i)
        return carry

    lax.fori_loop(0, n, body, 0)

    def drain(i, carry):
        wait_one()
        return carry

    lax.fori_loop(0, jnp.minimum(n, DMA_WINDOW), drain, 0)


def _tile_of(ref, row):
    return ref.at[pl.ds(pl.multiple_of(row * ROW_SL, ROW_SL), ROW_SL)]


def _tiles_to_rows(ref, n, first=0):
    return jnp.concatenate([ref[pl.ds(first * ROW_SL + sl, n, stride=ROW_SL), :] for sl in range(ROW_SL)], axis=1)


def _rows_to_tiles(ref, val):
    n = val.shape[0]
    for sl in range(ROW_SL):
        ref[pl.ds(sl, n, stride=ROW_SL), :] = val[:, sl * LANES:(sl + 1) * LANES]


def _dispatch_body(pos_ref, pad_start_ref, pad_n_ref, h_ref, xs_hbm, sem, *, n_tok, ne):
    i = pl.program_id(0)
    td = h_ref.shape[0] // ROW_SL
    base = i * td
    copy = lambda src, dst: pltpu.make_async_copy(_tile_of(h_ref, src), _tile_of(xs_hbm, dst), sem)
    wait_one = lambda: copy(0, 0).wait()

    def start_token(r, carry):
        copy(r, pos_ref[base + r]).start(priority=0)
        copy(r, pos_ref[n_tok + base + r]).start(priority=1)
        return carry

    lax.fori_loop(0, td, start_token, 0, unroll=8)
    whole_tile = pltpu.make_async_copy(h_ref, xs_hbm.at[pl.ds(0, td * ROW_SL)], sem)
    whole_tile.wait()
    whole_tile.wait()

    @pl.when(i == 0)
    def _():
        for e in range(ne):
            first = pad_start_ref[e]
            _windowed_copies(pad_n_ref[e], lambda r: copy(0, first + r).start(), wait_one)


def _moe_dispatch(h3, pos, pad_start, pad_n, n_rows, td=1024):
    n_tok = h3.shape[0] // ROW_SL
    ne = pad_start.shape[0]
    return pl.pallas_call(
        functools.partial(_dispatch_body, n_tok=n_tok, ne=ne),
        out_shape=jax.ShapeDtypeStruct((n_rows * ROW_SL, LANES), h3.dtype),
        grid_spec=pltpu.PrefetchScalarGridSpec(
            num_scalar_prefetch=3, grid=(n_tok // td,),
            in_specs=[pl.BlockSpec((td * ROW_SL, LANES), lambda i, p, ps, pn: (i, 0))],
            out_specs=pl.BlockSpec(memory_space=pl.ANY),
            scratch_shapes=[pltpu.SemaphoreType.DMA(())]),
        compiler_params=_cparams(("arbitrary",)),
        name="moe_dispatch",
    )(pos, pad_start, pad_n, h3)


def _moe_grouped_body(te_ref, nu_ref, tr_ref, xs_ref, w1_ref, w3_ref, w2_ref, y_ref, xb_scr, acc_scr):
    i = pl.program_id(0)
    j = pl.program_id(1)
    tm = xb_scr.shape[0]
    hm = tm // 2

    def expert_rows(nrows):
        h = xb_scr[0:nrows]
        a = _dot(h, w1_ref[0].astype(BF16))
        u = (a * jax.nn.sigmoid(a) * _dot(h, w3_ref[0].astype(BF16))).astype(BF16)
        part = _dot(u, w2_ref[0].astype(BF16))

        @pl.when(j == 0)
        def _():
            acc_scr[0:nrows] = part

        @pl.when(j > 0)
        def _():
            acc_scr[0:nrows] += part

    @pl.when(i < nu_ref[0])
    def _():
        @pl.when(j == 0)
        def _():
            xb_scr[...] = _tiles_to_rows(xs_ref, tm).astype(BF16)

        @pl.when(tr_ref[i] > hm)
        def _():
            expert_rows(tm)

        @pl.when(tr_ref[i] <= hm)
        def _():
            expert_rows(hm)

            @pl.when(j == 0)
            def _():
                acc_scr[hm:tm] = jnp.zeros((tm - hm, acc_scr.shape[1]), F32)

        @pl.when(j == pl.num_programs(1) - 1)
        def _():
            _rows_to_tiles(y_ref, acc_scr[...])


def _moe_grouped(xs, tile_expert, n_used, tile_rows, w1, w3, w2, tm, fb=512):
    ne, d, f = w1.shape
    n_rows = xs.shape[0] // ROW_SL
    nj = f // fb
    row_tile = lambda i, j, te, nu, tr: (jnp.maximum(jnp.minimum(i, nu[0] - 1), 0), 0)
    jj = lambda i, j, nu: jnp.where(i < nu[0], j, nj - 1)
    return pl.pallas_call(
        _moe_grouped_body,
        out_shape=jax.ShapeDtypeStruct(xs.shape, F32),
        grid_spec=pltpu.PrefetchScalarGridSpec(
            num_scalar_prefetch=3, grid=(n_rows // tm, nj),
            in_specs=[pl.BlockSpec((tm * ROW_SL, LANES), row_tile),
                      pl.BlockSpec((1, d, fb), lambda i, j, te, nu, tr: (te[i], 0, jj(i, j, nu))),
                      pl.BlockSpec((1, d, fb), lambda i, j, te, nu, tr: (te[i], 0, jj(i, j, nu))),
                      pl.BlockSpec((1, fb, d), lambda i, j, te, nu, tr: (te[i], jj(i, j, nu), 0))],
            out_specs=pl.BlockSpec((tm * ROW_SL, LANES), row_tile),
            scratch_shapes=[pltpu.VMEM((tm, d), BF16), pltpu.VMEM((tm, d), F32)]),
        compiler_params=_cparams(("arbitrary", "arbitrary")),
        name="moe_grouped",
    )(tile_expert, n_used, tile_rows, xs, w1, w3, w2)


def _moe_final_body(pos_ref, x_ref, y_hbm, gw_ref, gt_ref, fg_ref, o_ref, yg_scr, sem, *, n_tok):
    i = pl.program_id(0)
    tc = x_ref.shape[0]
    slot = i % 2

    def gather_tile(step, into):
        base = step * tc

        def start_token(r, carry):
            dst = yg_scr.at[into]
            pltpu.make_async_copy(_tile_of(y_hbm, pos_ref[base + r]), _tile_of(dst, r),
                                  sem.at[into]).start(priority=0)
            pltpu.make_async_copy(_tile_of(y_hbm, pos_ref[n_tok + base + r]), _tile_of(dst, tc + r),
                                  sem.at[into]).start(priority=1)
            return carry

        lax.fori_loop(0, tc, start_token, 0, unroll=8)

    @pl.when(i == 0)
    def _():
        gather_tile(0, 0)

    @pl.when(i + 1 < pl.num_programs(0))
    def _():
        gather_tile(i + 1, 1 - slot)

    pltpu.make_async_copy(y_hbm.at[pl.ds(0, 2 * tc * ROW_SL)], yg_scr.at[slot], sem.at[slot]).wait()
    gw = gw_ref[...]
    rows = yg_scr.at[slot]
    y = gw[:, 0:1] * _tiles_to_rows(rows, tc) + gw[:, 1:2] * _tiles_to_rows(rows, tc, first=tc)
    xo = x_ref[...] + gt_ref[0] * y
    ms = jnp.mean(xo * xo, axis=-1, keepdims=True)
    o_ref[...] = xo * lax.rsqrt(ms + EPS) * fg_ref[...]


def _moe_final(x, y, pos, gw, gt, final_g, tc=512):
    b, s, d = x.shape
    t = b * s
    spt = s // tc
    out = pl.pallas_call(
        functools.partial(_moe_final_body, n_tok=t),
        out_shape=jax.ShapeDtypeStruct((t, d), F32),
        grid_spec=pltpu.PrefetchScalarGridSpec(
            num_scalar_prefetch=1, grid=(t // tc,),
            in_specs=[pl.BlockSpec((tc, d), lambda i, p: (i, 0)),
                      pl.BlockSpec(memory_space=pl.ANY),
                      pl.BlockSpec((tc, LANES), lambda i, p: (i, 0)),
                      pl.BlockSpec((1, 1, d), lambda i, p: (i // spt, 0, 0)),
                      pl.BlockSpec(final_g.shape, lambda i, p: (0, 0))],
            out_specs=pl.BlockSpec((tc, d), lambda i, p: (i, 0)),
            scratch_shapes=[pltpu.VMEM((2, 2 * tc * ROW_SL, LANES), F32), pltpu.SemaphoreType.DMA((2,))]),
        compiler_params=_cparams(("arbitrary",)),
        name="moe_final",
    )(pos, x.reshape(t, d), y, gw, gt, final_g)
    return out.reshape(b, s, d)


def _moe_routed(x, y_f, w_f, gate_f, g, shift, scale, gt, final_g, w_router, b_router, w1, w3, w2):
    ne = w1.shape[0]
    tm = MOE_TM
    x1, h3, meta, gw, counts = _router(x, y_f, w_f, gate_f, g, shift, scale, w_router, b_router)
    pos, pad_start, pad_n, tile_expert, n_used, tile_rows = _moe_plan(meta, counts, ne, tm)
    assert x.shape[-1] == ROW_SL * LANES
    n_rows = (2 * (h3.shape[0] // ROW_SL) // tm + ne) * tm
    xs = _moe_dispatch(h3, pos, pad_start, pad_n, n_rows)
    y = _moe_grouped(xs, tile_expert, n_used, tile_rows, w1, w3, w2, tm)
    return _moe_final(x1.reshape(x.shape), y, pos, gw, gt, final_g)


def kernel(x, c, ctx, c_ctx, w_ada, b_ada, norm_g, w_in, hy_short_w, hy_short_b, hy_f_w1, hy_f_b1, hy_f_w2, hy_f_b2, hy_f_w3, hy_f_freq, hy_skip, na_rpb, w_mix_out, ffn_w1, ffn_w3, ffn_w2, w_fourier, w_router, b_router, moe_w1, moe_w3, moe_w2, final_g):
    b, s, d = x.shape
    depth = w_ada.shape[0]
    assert depth == 2, "layer 0 mixes with Hyena/attention, layer 1 with Fourier/MoE"
    c_hy = hy_skip.shape[-1]
    c_na = d - c_hy

    cvec = jnp.concatenate([c, c_ctx[None, :], jnp.zeros((8 - b - 1, d), F32)], axis=0)
    mods = _ada(cvec, w_ada, b_ada)

    def mod(layer, idx, ctx_row=False):
        m = mods[layer, :, idx * d:(idx + 1) * d]
        return m[b:b + 1, None, :] if ctx_row else m[0:b, None, :]

    row = lambda a: a.reshape(1, -1)

    w_in0 = w_in[0].astype(BF16)
    w_hy, w_qkv = w_in0[:, 0:3 * c_hy], w_in0[:, 3 * c_hy:]
    v, x1, x2, q, k, va = _inproj(x, row(norm_g[0, 0]), mod(0, 0), mod(0, 1), w_hy, w_qkv,
                                  hy_short_w[0], row(hy_short_b[0]))
    kc, vc = _ctxkv(ctx, row(norm_g[0, 0]), mod(0, 0, True), mod(0, 1, True), w_qkv[:, c_na:])
    y_na = _natt(q, k, va, kc, vc, _na_bias_table(na_rpb[0]))
    y_hy = _hyena(v, x1, x2, hy_f_w1[0], hy_f_b1[0], hy_f_w2[0], hy_f_b2[0], hy_f_w3[0],
                  hy_f_freq[0], hy_skip[0])
    x = _mix_ffn(x, y_hy, y_na, w_mix_out[0].astype(BF16), mod(0, 2),
                 row(norm_g[0, 1]), mod(0, 3), mod(0, 4), mod(0, 5),
                 ffn_w1[0].astype(BF16), ffn_w3[0].astype(BF16), ffn_w2[0].astype(BF16))

    y_f = _fourier_mix(x, row(norm_g[1, 0]), mod(1, 0), mod(1, 1))
    return _moe_routed(x, y_f, w_fourier[0].astype(BF16), mod(1, 2),
                       row(norm_g[1, 1]), mod(1, 3), mod(1, 4), mod(1, 5), row(final_g),
                       w_router[0], b_router[0],
                       moe_w1[0], moe_w3[0], moe_w2[0])
```

```python
import functools
import math

import numpy as np
import jax
import jax.numpy as jnp
from jax import lax
from jax.experimental import pallas as pl
from jax.experimental.pallas import tpu as pltpu

F32 = jnp.float32
BF16 = jnp.bfloat16
HIGHEST = lax.Precision.HIGHEST

GRID_W = 64
NA_HEAD_DIM = 32
NA_WIN_R = 8
NA_WIN_C = 16
HYENA_EMB = 33
HYENA_BANDS = (HYENA_EMB - 1) // 2
HYENA_FAST_DECAY = 0.3
HYENA_SLOW_DECAY = 1.5
HYENA_TARGET = 1e-2
F_GROUPS = 4
N_MOD = 6
EPS = 1e-6
NEG_INF = -1e30

FFT_A = 64
FFT_R = 128
FFT_KA = FFT_A // 2 + 1
FFT_KA_PAD = 40
FM_A = 64

LANES = 128
ROW_SL = 8
VMEM_LIMIT = 48 * 1024 * 1024
LARGE_VMEM_LIMIT = 56 * 1024 * 1024


def _cparams(sem, vmem_limit=VMEM_LIMIT):
    return pltpu.CompilerParams(dimension_semantics=sem, vmem_limit_bytes=vmem_limit)


def _dot(a, b):
    return jnp.dot(a, b, preferred_element_type=F32)


def _mxu_const(m):
    return jnp.asarray(m, dtype=F32).astype(BF16)


def _norm_mod(x, g, shift, scale):
    ms = jnp.mean(x * x, axis=-1, keepdims=True)
    y = x * lax.rsqrt(ms + EPS) * g
    return y * (1.0 + scale) + shift


def _ada_body(c_ref, w_ref, b_ref, o_ref):
    cv = c_ref[...]
    s = cv * jax.nn.sigmoid(cv)
    o_ref[0] = jnp.dot(s, w_ref[0], precision=HIGHEST, preferred_element_type=F32) + b_ref[0]


def _ada(cvec, w_ada, b_ada):
    depth, d, n = w_ada.shape
    rows = cvec.shape[0]
    bn = n // 4
    return pl.pallas_call(
        _ada_body,
        out_shape=jax.ShapeDtypeStruct((depth, rows, n), F32),
        grid=(depth, n // bn),
        in_specs=[pl.BlockSpec((rows, d), lambda l, j: (0, 0)),
                  pl.BlockSpec((1, d, bn), lambda l, j: (l, 0, j)),
                  pl.BlockSpec((1, 1, bn), lambda l, j: (l, 0, j))],
        out_specs=pl.BlockSpec((1, rows, bn), lambda l, j: (l, 0, j)),
        compiler_params=_cparams(("parallel", "parallel")),
        name="ada",
    )(cvec, w_ada, b_ada.reshape(depth, 1, n))


def _inproj_body(x_ref, xp_ref, xn_ref, g_ref, sh_ref, sc_ref, why_ref, wqkv_ref, sw_ref, sb_ref,
                 v_ref, x1_ref, x2_ref, q_ref, k_ref, va_ref, *, n_tiles, q_scale, c_hy, c_na):
    i = pl.program_id(1)
    g, sh, sc = g_ref[...], sh_ref[0], sc_ref[0]
    hf = _norm_mod(x_ref[0], g, sh, sc)
    h = hf.astype(BF16)
    tm = hf.shape[0]
    hx = jnp.concatenate([_norm_mod(xp_ref[0], g, sh, sc), hf, _norm_mod(xn_ref[0], g, sh, sc)],
                         axis=0).astype(BF16)
    row = lax.broadcasted_iota(jnp.int32, (tm, c_hy), 0)
    sw = sw_ref[...]
    sb = sb_ref[...]
    lo, hi = ROW_SL, ROW_SL + tm
    for ci, out_ref in enumerate((v_ref, x1_ref, x2_ref)):
        cols = slice(ci * c_hy, (ci + 1) * c_hy)
        zx = _dot(hx, why_ref[:, cols])
        zh = zx[lo:hi]
        zp = jnp.where(i > 0, zx[lo - 1:lo], 0.0)
        zn = jnp.where(i < n_tiles - 1, zx[hi:hi + 1], 0.0)
        z_m1 = jnp.where(row == 0, zp, pltpu.roll(zh, 1, 0))
        z_p1 = jnp.where(row == tm - 1, zn, pltpu.roll(zh, tm - 1, 0))
        out_ref[0] = z_m1 * sw[0:1, cols] + zh * sw[1:2, cols] + z_p1 * sw[2:3, cols] + sb[:, cols]
    for ci, (out_ref, mult) in enumerate(((q_ref, q_scale), (k_ref, None), (va_ref, None))):
        z = _dot(h, wqkv_ref[:, ci * c_na:(ci + 1) * c_na])
        out_ref[0] = (z if mult is None else z * mult).astype(BF16)


def _inproj(x, g, shift, scale, w_hy, w_qkv, short_w, short_b, tm=1024):
    b, s, d = x.shape
    c_hy = w_hy.shape[1] // 3
    c_na = w_qkv.shape[1] // 3
    n_tiles = s // tm
    halo_per_tile = tm // ROW_SL
    last_halo = s // ROW_SL - 1
    body = functools.partial(_inproj_body, n_tiles=n_tiles, q_scale=NA_HEAD_DIM ** -0.5,
                             c_hy=c_hy, c_na=c_na)
    tok = lambda c: pl.BlockSpec((1, tm, c), lambda bi, i: (bi, i, 0))
    full2 = lambda a: pl.BlockSpec(a.shape, lambda bi, i: (0, 0))
    resident = lambda a: pl.BlockSpec(a.shape, lambda bi, i: (0, 0), pipeline_mode=pl.Buffered(1))
    per_b = pl.BlockSpec((1, 1, d), lambda bi, i: (bi, 0, 0))
    return pl.pallas_call(
        body,
        out_shape=[jax.ShapeDtypeStruct((b, s, c_hy), F32)] * 3 + [jax.ShapeDtypeStruct((b, s, c_na), BF16)] * 3,
        grid=(b, n_tiles),
        in_specs=[tok(d),
                  pl.BlockSpec((1, ROW_SL, d), lambda bi, i: (bi, jnp.maximum(i * halo_per_tile - 1, 0), 0)),
                  pl.BlockSpec((1, ROW_SL, d),
                               lambda bi, i: (bi, jnp.minimum((i + 1) * halo_per_tile, last_halo), 0)),
                  full2(g), per_b, per_b, resident(w_hy), resident(w_qkv), full2(short_w), full2(short_b)],
        out_specs=[tok(c_hy)] * 3 + [tok(c_na)] * 3,
        compiler_params=_cparams(("parallel", "parallel"), vmem_limit=LARGE_VMEM_LIMIT),
        name="inproj",
    )(x, x, x, g, shift, scale, w_hy, w_qkv, short_w, short_b)


def _ctxkv_body(x_ref, g_ref, sh_ref, sc_ref, w_ref, k_ref, v_ref, *, c_na):
    h = _norm_mod(x_ref[0], g_ref[...], sh_ref[0], sc_ref[0]).astype(BF16)
    z = _dot(h, w_ref[...])
    k_ref[0] = z[:, 0:c_na].astype(BF16)
    v_ref[0] = z[:, c_na:2 * c_na].astype(BF16)


def _ctxkv(ctx, g, shift, scale, w_kv):
    b, n, d = ctx.shape
    c_na = w_kv.shape[1] // 2
    one = pl.BlockSpec((1, 1, d), lambda bi: (0, 0, 0))
    return pl.pallas_call(
        functools.partial(_ctxkv_body, c_na=c_na),
        out_shape=[jax.ShapeDtypeStruct((b, n, c_na), BF16)] * 2,
        grid=(b,),
        in_specs=[pl.BlockSpec((1, n, d), lambda bi: (bi, 0, 0)),
                  pl.BlockSpec(g.shape, lambda bi: (0, 0)), one, one,
                  pl.BlockSpec(w_kv.shape, lambda bi: (0, 0))],
        out_specs=[pl.BlockSpec((1, n, c_na), lambda bi: (bi, 0, 0))] * 2,
        compiler_params=_cparams(("parallel",)),
        name="ctxkv",
    )(ctx, g, shift, scale, w_kv)


NA_HEADS_PER_BLK = 8
NA_ROWS_PER_STEP = 8


def _na_bias_body(r_ref, e_ref, ok_ref, o_ref):
    t = jnp.dot(r_ref[...], e_ref[...], precision=HIGHEST, preferred_element_type=F32)
    o_ref[...] = jnp.where(ok_ref[...] > 0.5, t, NEG_INF)


def _na_bias_table(rpb):
    w = GRID_W
    h, nr, nc = rpb.shape
    col = np.arange(w)[:, None]
    kc = np.arange(w)[None, :]
    c_start = np.clip(col - NA_WIN_C // 2, 0, w - NA_WIN_C)
    valid = ((kc >= c_start) & (kc < c_start + NA_WIN_C)).reshape(1, w * w)
    nc_pad = -(-nc // ROW_SL) * ROW_SL
    expand = (np.arange(nc_pad)[:, None, None] == (kc - col + NA_WIN_C - 1)[None]).reshape(nc_pad, w * w)
    rp = jnp.pad(rpb.astype(F32).reshape(h * nr, nc), ((0, 0), (0, nc_pad - nc)))
    full = lambda a: pl.BlockSpec(a.shape, lambda: (0,) * a.ndim)
    expand = jnp.asarray(expand, dtype=F32)
    ok = jnp.asarray(valid, dtype=F32)
    toep = pl.pallas_call(
        _na_bias_body,
        out_shape=jax.ShapeDtypeStruct((h * nr, w * w), F32),
        in_specs=[full(rp), full(expand), full(ok)],
        out_specs=pl.BlockSpec((h * nr, w * w), lambda: (0, 0)),
        name="na_bias",
    )(rp, expand, ok)
    t2 = toep.reshape(h, nr, w, w).transpose(0, 2, 1, 3).reshape(h, w, nr * w)
    slabs = jnp.stack([t2[:, :, (NA_WIN_R - 1 - off) * w:(2 * NA_WIN_R - 1 - off) * w]
                       for off in range(NA_WIN_R)], axis=1)
    hpb = NA_HEADS_PER_BLK
    slabs = slabs.reshape(h // hpb, hpb, NA_WIN_R, w, NA_WIN_R * w).transpose(0, 2, 1, 3, 4)
    return slabs.reshape(h // hpb, NA_WIN_R, hpb * w, NA_WIN_R * w)


def _natt_body(q_ref, k_ref, v_ref, kc_ref, vc_ref, bias_ref, o_ref, *, rows):
    w = GRID_W
    hpb = NA_HEADS_PER_BLK
    nloc = NA_WIN_R * w
    lane = lax.broadcasted_iota(jnp.int32, (1, hpb * NA_HEAD_DIM), 1)
    in_head = [(lane >= NA_HEAD_DIM * hh) & (lane < NA_HEAD_DIM * (hh + 1)) for hh in range(hpb)]
    kcx = kc_ref[0]
    vcx = vc_ref[0]
    nt = (((1,), (1,)), ((), ()))

    def one_row(r):
        r0 = jnp.clip(r - NA_WIN_R // 2, 0, rows - NA_WIN_R)
        off = r - r0
        qs = q_ref[0, pl.ds(pl.multiple_of(r * w, w), w), :]
        kw = k_ref[0, pl.ds(pl.multiple_of(r0 * w, w), nloc), :]
        vw = v_ref[0, pl.ds(pl.multiple_of(r0 * w, w), nloc), :]
        zero = jnp.zeros_like(qs)
        qst = jnp.concatenate([jnp.where(m, qs, zero) for m in in_head], axis=0)
        s_loc = lax.dot_general(qst, kw, nt, preferred_element_type=F32) + bias_ref[0, off]
        s_ctx = lax.dot_general(qst, kcx, nt, preferred_element_type=F32)
        m = jnp.maximum(jnp.max(s_loc, axis=-1, keepdims=True), jnp.max(s_ctx, axis=-1, keepdims=True))
        p_loc = jnp.exp(s_loc - m)
        p_ctx = jnp.exp(s_ctx - m)
        den = jnp.sum(p_loc, axis=-1, keepdims=True) + jnp.sum(p_ctx, axis=-1, keepdims=True)
        o = (_dot(p_loc.astype(BF16), vw) + _dot(p_ctx.astype(BF16), vcx)) * (1.0 / den)
        acc = jnp.where(in_head[0], o[0:w], 0.0)
        for hh in range(1, hpb):
            acc = acc + jnp.where(in_head[hh], o[hh * w:(hh + 1) * w], 0.0)
        o_ref[0, pl.ds(pl.multiple_of(r * w, w), w), :] = acc.astype(BF16)

    def row_group(i, carry):
        for r in range(NA_ROWS_PER_STEP):
            one_row(NA_ROWS_PER_STEP * i + r)
        return carry

    lax.fori_loop(0, rows // NA_ROWS_PER_STEP, row_group, 0)


def _natt(q, k, v, kc, vc, bias):
    b, s, c = q.shape
    nctx = kc.shape[1]
    lw = NA_HEADS_PER_BLK * NA_HEAD_DIM
    rows = s // GRID_W
    seq = pl.BlockSpec((1, s, lw), lambda bi, g: (bi, 0, g))
    cx = pl.BlockSpec((1, nctx, lw), lambda bi, g: (bi, 0, g))
    return pl.pallas_call(
        functools.partial(_natt_body, rows=rows),
        out_shape=jax.ShapeDtypeStruct((b, s, c), BF16),
        grid=(b, c // lw),
        in_specs=[seq, seq, seq, cx, cx,
                  pl.BlockSpec((1,) + bias.shape[1:], lambda bi, g: (g, 0, 0, 0))],
        out_specs=seq,
        compiler_params=_cparams(("parallel", "parallel")),
        name="natt",
    )(q, k, v, kc, vc, bias)


def _hyena_feats(seq_len):
    t = jnp.linspace(0.0, 1.0, seq_len, dtype=F32)[:, None]
    bands = jnp.linspace(1e-4, HYENA_BANDS - 1, HYENA_BANDS, dtype=F32)
    ang = (2.0 * math.pi / seq_len) * jnp.arange(seq_len, dtype=F32)[:, None] * bands[None, :]
    feats = jnp.concatenate([t, jnp.cos(ang), -jnp.sin(ang)], axis=-1)
    return jnp.pad(feats, ((0, 0), (0, LANES - HYENA_EMB)))


def _filt_body(feat_ref, w1_ref, b1_ref, w2_ref, b2_ref, w3_ref, fr_ref, dl_ref, o_ref, l1_ref, h_scr,
               *, halves):
    j = pl.program_id(0)
    hp = functools.partial(jnp.dot, precision=HIGHEST, preferred_element_type=F32)
    feats = feat_ref[...]

    @pl.when(j == 0)
    def _():
        fr = fr_ref[...]
        h = jnp.sin(fr[0:1] * (hp(feats, w1_ref[...]) + b1_ref[...]))
        h_scr[...] = jnp.sin(fr[1:2] * (hp(h, w2_ref[...]) + b2_ref[...]))

    h2 = h_scr[...]
    h_hi = h2.astype(BF16)
    h_lo = (h2 - h_hi.astype(F32)).astype(BF16)
    w3 = w3_ref[...]
    w_hi = w3.astype(BF16)
    w_lo = (w3 - w_hi.astype(F32)).astype(BF16)
    hc = (_dot(jnp.concatenate([h_hi, h_lo], axis=1), jnp.concatenate([w_hi, w_hi], axis=0))
          + _dot(h_hi, w_lo))
    t = feats[:, 0:1]
    hc = hc * jnp.exp(-t * dl_ref[...])
    row = lax.broadcasted_iota(jnp.int32, hc.shape, 0)
    hc = jnp.where((row == 0) & ((j // halves) % 2 == 1), 0.0, hc)
    l1_ref[0] = jnp.sum(jnp.abs(hc), axis=0, keepdims=True)
    o_ref[0] = hc


def _hyena_filter_taps(seq_len, f_w1, f_b1, f_w2, f_b2, f_w3, f_freq, c_hy):
    feats = _hyena_feats(seq_len)
    hid = f_w1.shape[1]
    w1 = jnp.pad(f_w1.astype(F32), ((0, LANES - HYENA_EMB), (0, 0)))
    deltas = jnp.abs(jnp.linspace(math.log(HYENA_TARGET) / HYENA_SLOW_DECAY,
                                  math.log(HYENA_TARGET) / HYENA_FAST_DECAY, c_hy, dtype=F32))[None, :]
    nblk = f_w3.shape[1] // c_hy
    halves = 2
    cb = c_hy // halves
    c0 = lambda a: pl.BlockSpec(a.shape, lambda j: (0, 0))
    b1, b2 = f_b1.reshape(1, hid), f_b2.reshape(1, hid)
    return pl.pallas_call(
        functools.partial(_filt_body, halves=halves),
        out_shape=[jax.ShapeDtypeStruct((nblk, seq_len, c_hy), F32),
                   jax.ShapeDtypeStruct((nblk, 1, c_hy), F32)],
        grid=(nblk * halves,),
        in_specs=[c0(feats), c0(w1), c0(b1), c0(f_w2), c0(b2),
                  pl.BlockSpec((hid, cb), lambda j: (0, j)), c0(f_freq),
                  pl.BlockSpec((1, cb), lambda j: (0, j % halves))],
        out_specs=[pl.BlockSpec((1, seq_len, cb), lambda j: (j // halves, 0, j % halves)),
                   pl.BlockSpec((1, 1, cb), lambda j: (j // halves, 0, j % halves))],
        scratch_shapes=[pltpu.VMEM((seq_len, hid), F32)],
        compiler_params=_cparams(("arbitrary",)),
        name="hyena_filter",
    )(feats, w1, b1, f_w2, b2, f_w3, f_freq, deltas)


def _conv_dft_constants():
    a_half = FFT_A // 2
    n = FFT_A * FFT_R
    ka = np.arange(FFT_KA)[:, None]
    a = np.arange(a_half)[None, :]
    ph = 2.0 * np.pi * ka * a / FFT_A
    m_fwd = np.zeros((2 * FFT_KA_PAD, a_half))
    m_fwd[:FFT_KA] = np.cos(ph)
    m_fwd[FFT_KA_PAD:FFT_KA_PAD + FFT_KA] = -np.sin(ph)
    wgt = np.where((ka == 0) | (ka == FFT_A // 2), 1.0, 2.0)
    m_inv = np.zeros((a_half, 2 * FFT_KA_PAD))
    m_inv[:, :FFT_KA] = (wgt * np.cos(ph)).T / n
    m_inv[:, FFT_KA_PAD:FFT_KA_PAD + FFT_KA] = (-wgt * np.sin(ph)).T / n
    kb = np.arange(FFT_R)[None, :, None]
    b = np.arange(FFT_R)[None, None, :]
    kaa = np.arange(FFT_KA)[:, None, None]
    th = 2.0 * np.pi * (b * kb / FFT_R + b * kaa / n)
    gr, gi = np.cos(th), -np.sin(th)
    g2 = np.zeros((FFT_KA_PAD, 2 * FFT_R, 2 * FFT_R))
    g2[:FFT_KA] = np.block([[gr, -gi], [gi, gr]])
    grt, git = gr.transpose(0, 2, 1), gi.transpose(0, 2, 1)
    g2h = np.zeros_like(g2)
    g2h[:FFT_KA] = np.block([[grt, git], [-git, grt]])
    return _mxu_const(m_fwd), _mxu_const(m_inv), _mxu_const(g2), _mxu_const(g2h)


FFT_NB = 32


def _fwd1_body(m_ref, u_ref, o_ref):
    u = jnp.concatenate([u_ref[0, :, bb, :] for bb in range(FFT_NB)], axis=1).astype(BF16)
    res = _dot(m_ref[...], u)
    o_ref[0, 0] = res[0:FFT_KA_PAD]
    o_ref[0, 1] = res[FFT_KA_PAD:2 * FFT_KA_PAD]


def _conv_fwd1(u, m_fwd):
    n, seq, c = u.shape
    a_half = FFT_A // 2
    return pl.pallas_call(
        _fwd1_body,
        out_shape=jax.ShapeDtypeStruct((n, 2, FFT_KA_PAD, FFT_R * c), F32),
        grid=(n, FFT_R // FFT_NB),
        in_specs=[pl.BlockSpec(m_fwd.shape, lambda i, j: (0, 0)),
                  pl.BlockSpec((1, a_half, FFT_NB, c), lambda i, j: (i, 0, j, 0))],
        out_specs=pl.BlockSpec((1, 2, FFT_KA_PAD, FFT_NB * c), lambda i, j: (i, 0, 0, j)),
        compiler_params=_cparams(("parallel", "parallel")),
        name="conv_fwd1",
    )(m_fwd, u.reshape(n, a_half, FFT_R, c))


FFT_KB = 8


def _rows_to_slabs(src_ref, dst_scr, c):
    for part in range(2):
        for b in range(FFT_R):
            dst_scr[part, :, b, :] = src_ref[0, part, :, b * c:(b + 1) * c]


def _slabs_to_rows(src_scr, dst_ref, c):
    for part in range(2):
        for b in range(FFT_R):
            dst_ref[0, part, :, b * c:(b + 1) * c] = src_scr[part, :, b, :]


def _slab(scr, i):
    return jnp.concatenate([scr[0, i], scr[1, i]], axis=0).astype(BF16)


def _per_ka_block(j, work, clear):
    full_blocks = FFT_KA // FFT_KB
    tail = FFT_KA - full_blocks * FFT_KB

    @pl.when(j < full_blocks)
    def _():
        for i in range(FFT_KB):
            work(i)

    @pl.when(j >= full_blocks)
    def _():
        for i in range(FFT_KB):
            (work if i < tail else clear)(i)


def _fwd2f_body(sf_ref, sb_ref, g_ref, l1_ref, kf_ref, f3, b3):
    o = pl.program_id(0)
    j = pl.program_id(1)
    r2 = 2 * FFT_R
    c = kf_ref.shape[-1]
    _rows_to_slabs(sf_ref, f3, c)
    _rows_to_slabs(sb_ref, b3, c)
    inv = 1.0 / (l1_ref[2 * o] + l1_ref[2 * o + 1] + EPS)

    def spectrum(i):
        xf = _dot(g_ref[i], _slab(f3, i))
        xb = _dot(g_ref[i], _slab(b3, i))
        kf_ref[0, i, 0:FFT_R] = (xf[0:FFT_R] + xb[0:FFT_R]) * inv
        kf_ref[0, i, FFT_R:r2] = (xf[FFT_R:r2] - xb[FFT_R:r2]) * inv

    def clear(i):
        kf_ref[0, i] = jnp.zeros((r2, c), F32)

    _per_ka_block(j, spectrum, clear)


def _filter_spectrum(s_filt, l1, g2, c):
    n_ord = s_filt.shape[0] // 2
    cols = s_filt.shape[-1]
    r2 = 2 * FFT_R
    return pl.pallas_call(
        _fwd2f_body,
        out_shape=jax.ShapeDtypeStruct((n_ord, FFT_KA_PAD, r2, c), F32),
        grid=(n_ord, FFT_KA_PAD // FFT_KB),
        in_specs=[pl.BlockSpec((1, 2, FFT_KB, cols), lambda o, j: (2 * o, 0, j, 0)),
                  pl.BlockSpec((1, 2, FFT_KB, cols), lambda o, j: (2 * o + 1, 0, j, 0)),
                  pl.BlockSpec((FFT_KB, r2, r2), lambda o, j: (j, 0, 0)),
                  pl.BlockSpec(l1.shape, lambda o, j: (0, 0, 0))],
        out_specs=pl.BlockSpec((1, FFT_KB, r2, c), lambda o, j: (o, j, 0, 0)),
        scratch_shapes=[pltpu.VMEM((2, FFT_KB, FFT_R, c), F32)] * 2,
        compiler_params=_cparams(("parallel", "parallel")),
        name="filter_spectrum",
    )(s_filt, s_filt, g2, l1)


def _mid_body(s_ref, g_ref, gh_ref, kf_ref, t_ref, s3, t3):
    j = pl.program_id(1)
    r2 = 2 * FFT_R
    c = kf_ref.shape[-1]
    _rows_to_slabs(s_ref, s3, c)

    def convolve(i):
        x = _dot(g_ref[i], _slab(s3, i))
        xr, xi = x[0:FFT_R], x[FFT_R:r2]
        kr, ki = kf_ref[0, i, 0:FFT_R], kf_ref[0, i, FFT_R:r2]
        y = jnp.concatenate([xr * kr - xi * ki, xr * ki + xi * kr], axis=0).astype(BF16)
        t = _dot(gh_ref[i], y)
        t3[0, i] = t[0:FFT_R]
        t3[1, i] = t[FFT_R:r2]

    def clear(i):
        t3[0, i] = jnp.zeros((FFT_R, c), F32)
        t3[1, i] = jnp.zeros((FFT_R, c), F32)

    _per_ka_block(j, convolve, clear)
    _slabs_to_rows(t3, t_ref, c)


def _conv_mid(s, kf, order, g2, g2h, c):
    n, _, _, cols = s.shape
    r2 = 2 * FFT_R
    blk = pl.BlockSpec((1, 2, FFT_KB, cols), lambda i, j: (i, 0, j, 0))
    gspec = pl.BlockSpec((FFT_KB, r2, r2), lambda i, j: (j, 0, 0))
    return pl.pallas_call(
        _mid_body,
        out_shape=jax.ShapeDtypeStruct(s.shape, F32),
        grid=(n, FFT_KA_PAD // FFT_KB),
        in_specs=[blk, gspec, gspec,
                  pl.BlockSpec((1, FFT_KB, r2, c), lambda i, j: (order, j, 0, 0))],
        out_specs=blk,
        scratch_shapes=[pltpu.VMEM((2, FFT_KB, FFT_R, c), F32)] * 2,
        compiler_params=_cparams(("parallel", "parallel")),
        name="conv_mid",
    )(s, g2, g2h, kf)


def _inv1_body(m_ref, mf_ref, t_ref, u_ref, xg_ref, sk_ref, o_ref, *s_ref):
    c = u_ref.shape[-1]
    t2 = t_ref[0].reshape(2 * FFT_KA_PAD, FFT_NB * c).astype(BF16)
    y = _dot(m_ref[...], t2)
    gated = []
    for bb in range(FFT_NB):
        conv = y[:, bb * c:(bb + 1) * c] + u_ref[0, :, bb, :] * sk_ref[...]
        gated.append(xg_ref[0, :, bb, :] * conv)
        o_ref[0, :, bb, :] = gated[-1]
    if s_ref:
        res = _dot(mf_ref[...], jnp.concatenate(gated, axis=1).astype(BF16))
        s_ref[0][0, 0] = res[0:FFT_KA_PAD]
        s_ref[0][0, 1] = res[FFT_KA_PAD:2 * FFT_KA_PAD]


def _conv_inv1(t, u, xg, skip, m_inv, m_fwd, with_spectrum):
    n, seq, c = u.shape
    a_half = FFT_A // 2
    sk = skip.astype(F32).reshape(1, c)
    uspec = pl.BlockSpec((1, a_half, FFT_NB, c), lambda i, j: (i, 0, j, 0))
    sspec = pl.BlockSpec((1, 2, FFT_KA_PAD, FFT_NB * c), lambda i, j: (i, 0, 0, j))
    view = lambda a: a.reshape(n, a_half, FFT_R, c)
    y_shape = jax.ShapeDtypeStruct((n, a_half, FFT_R, c), F32)
    s_shape = jax.ShapeDtypeStruct(t.shape, F32)
    outs = pl.pallas_call(
        _inv1_body,
        out_shape=[y_shape, s_shape] if with_spectrum else [y_shape],
        grid=(n, FFT_R // FFT_NB),
        in_specs=[pl.BlockSpec(m_inv.shape, lambda i, j: (0, 0)),
                  pl.BlockSpec(m_fwd.shape, lambda i, j: (0, 0)),
                  sspec, uspec, uspec,
                  pl.BlockSpec((1, c), lambda i, j: (0, 0))],
        out_specs=[uspec, sspec] if with_spectrum else [uspec],
        compiler_params=_cparams(("parallel", "parallel")),
        name="conv_inv1",
    )(m_inv, m_fwd, t, view(u), view(xg), sk)
    y = outs[0].reshape(n, seq, c)
    return (y, outs[1]) if with_spectrum else (y, None)


def _hyena(v, x1, x2, f_w1, f_b1, f_w2, f_b2, f_w3, f_freq, skip):
    _, seq, c = v.shape
    assert 2 * seq == FFT_A * FFT_R
    m_fwd, m_inv, g2, g2h = _conv_dft_constants()
    taps, l1 = _hyena_filter_taps(seq, f_w1, f_b1, f_w2, f_b2, f_w3, f_freq, c)
    kf = _filter_spectrum(_conv_fwd1(taps, m_fwd), l1, g2, c)
    y, s = v, _conv_fwd1(v, m_fwd)
    gates = (x1, x2)
    for order, xg in enumerate(gates):
        t = _conv_mid(s, kf, order, g2, g2h, c)
        y, s = _conv_inv1(t, y, xg, skip[order], m_inv, m_fwd, with_spectrum=order + 1 < len(gates))
    return y


def _mix_ffn_body(x_ref, a1_ref, a2_ref, wm_ref, gm_ref, g_ref, sh_ref, sc_ref, gt_ref,
                  w1_ref, w3_ref, w2_ref, o_ref, *, fb):
    c1 = a1_ref.shape[-1]
    mixed = _dot(a1_ref[0].astype(BF16), wm_ref[0:c1]) + _dot(a2_ref[0].astype(BF16), wm_ref[c1:])
    xm = x_ref[0] + gm_ref[0] * mixed
    h = _norm_mod(xm, g_ref[...], sh_ref[0], sc_ref[0]).astype(BF16)
    acc = None
    for lo in range(0, w1_ref.shape[1], fb):
        a = _dot(h, w1_ref[:, lo:lo + fb])
        u = (a * jax.nn.sigmoid(a) * _dot(h, w3_ref[:, lo:lo + fb])).astype(BF16)
        part = _dot(u, w2_ref[lo:lo + fb, :])
        acc = part if acc is None else acc + part
    o_ref[0] = xm + gt_ref[0] * acc


def _mix_ffn(x, a1, a2, w_mix, gate_mix, g, shift, scale, gate, w1, w3, w2, tm=512, fb=1408):
    b, s, d = x.shape
    tok = lambda c: pl.BlockSpec((1, tm, c), lambda bi, i: (bi, i, 0))
    per_b = pl.BlockSpec((1, 1, d), lambda bi, i: (bi, 0, 0))
    const = lambda a: pl.BlockSpec(a.shape, lambda bi, i: (0, 0))
    resident = lambda a: pl.BlockSpec(a.shape, lambda bi, i: (0, 0), pipeline_mode=pl.Buffered(1))
    return pl.pallas_call(
        functools.partial(_mix_ffn_body, fb=fb),
        out_shape=jax.ShapeDtypeStruct(x.shape, F32),
        grid=(b, s // tm),
        in_specs=[tok(d), tok(a1.shape[-1]), tok(a2.shape[-1]), resident(w_mix), per_b,
                  const(g), per_b, per_b, per_b, resident(w1), resident(w3), resident(w2)],
        out_specs=tok(d),
        compiler_params=_cparams(("parallel", "parallel"), vmem_limit=LARGE_VMEM_LIMIT),
        name="mix_ffn",
    )(x, a1, a2, w_mix, gate_mix, g, shift, scale, gate, w1, w3, w2)


def _fm_constants(cg):
    j = np.arange(cg)[:, None]
    m = np.arange(cg)[None, :]
    ph = 2.0 * np.pi * j * m / cg
    w_cs = np.concatenate([np.cos(ph), np.sin(ph)], axis=1)
    d = np.arange(FM_A)[:, None]
    a = np.arange(FM_A)[None, :]
    ph = 2.0 * np.pi * d * a / FM_A
    fr, fi = np.cos(ph), -np.sin(ph)
    m1 = np.block([[fr, fi], [fi, -fr]])
    n = FM_A * FM_A
    dd = np.arange(FM_A)[:, None, None]
    c = np.arange(FM_A)[None, :, None]
    b = np.arange(FM_A)[None, None, :]
    th = 2.0 * np.pi * (b * c / FM_A + b * dd / n)
    gcat = np.concatenate([np.cos(th), np.sin(th)], axis=2)
    return _mxu_const(w_cs), _mxu_const(m1), _mxu_const(gcat)


def _fm_front_body(x_ref, g_ref, sh_ref, sc_ref, w_ref, m_ref, o_ref, *, cg, nb):
    d = x_ref.shape[-1]
    xs = jnp.concatenate([x_ref[0, :, bb, :] for bb in range(nb)], axis=0)
    h = _norm_mod(xs, g_ref[...], sh_ref[0], sc_ref[0]).astype(BF16)
    pq = [_dot(h[:, grp * cg:(grp + 1) * cg], w_ref[...]) for grp in range(d // cg)]
    p = jnp.concatenate([t[:, 0:cg] for t in pq], axis=1)
    q = jnp.concatenate([t[:, cg:2 * cg] for t in pq], axis=1)
    for bb in range(nb):
        rows = slice(bb * FM_A, (bb + 1) * FM_A)
        res = _dot(m_ref[...], jnp.concatenate([p[rows], q[rows]], axis=0).astype(BF16))
        o_ref[0, 0, :, bb, :] = res[0:FM_A]
        o_ref[0, 1, :, bb, :] = res[FM_A:2 * FM_A]


def _fm_front(x, g, shift, scale, w_cs, m1, nb=8):
    b, s, d = x.shape
    cg = w_cs.shape[0]
    per_b = pl.BlockSpec((1, 1, d), lambda bi, j: (bi, 0, 0))
    const = lambda a: pl.BlockSpec(a.shape, lambda bi, j: (0, 0))
    return pl.pallas_call(
        functools.partial(_fm_front_body, cg=cg, nb=nb),
        out_shape=jax.ShapeDtypeStruct((b, 2, FM_A, s // FM_A, d), F32),
        grid=(b, s // FM_A // nb),
        in_specs=[pl.BlockSpec((1, FM_A, nb, d), lambda bi, j: (bi, 0, j, 0)),
                  const(g), per_b, per_b, const(w_cs), const(m1)],
        out_specs=pl.BlockSpec((1, 2, FM_A, nb, d), lambda bi, j: (bi, 0, 0, j, 0)),
        compiler_params=_cparams(("parallel", "parallel")),
        name="fm_front",
    )(x.reshape(b, FM_A, s // FM_A, d), g, shift, scale, w_cs, m1)


def _fm_s2_body(s_ref, g_ref, o_ref, *, dblk, scale):
    for i in range(dblk):
        s2 = jnp.concatenate([s_ref[0, 0, i], s_ref[0, 1, i]], axis=0).astype(BF16)
        o_ref[:, i, :] = _dot(g_ref[i], s2) * scale


def _fm_stage2(sv, gcat, seq, d, dblk=8):
    b = sv.shape[0]
    scale = 1.0 / math.sqrt(seq * (d // F_GROUPS))
    out = pl.pallas_call(
        functools.partial(_fm_s2_body, dblk=dblk, scale=scale),
        out_shape=jax.ShapeDtypeStruct((b * FM_A, dblk * (FM_A // dblk), d), F32),
        grid=(b, FM_A // dblk),
        in_specs=[pl.BlockSpec((1, 2, dblk, FM_A, d), lambda bi, j: (bi, 0, j, 0, 0)),
                  pl.BlockSpec((dblk, FM_A, 2 * FM_A), lambda bi, j: (j, 0, 0))],
        out_specs=pl.BlockSpec((FM_A, dblk, d), lambda bi, j: (bi, j, 0)),
        compiler_params=_cparams(("parallel", "parallel")),
        name="fm_stage2",
    )(sv, gcat)
    return out.reshape(b, seq, d)


def _fourier_mix(x, g, shift, scale):
    b, s, d = x.shape
    assert s == FM_A * FM_A
    w_cs, m1, gcat = _fm_constants(d // F_GROUPS)
    return _fm_stage2(_fm_front(x, g, shift, scale, w_cs, m1), gcat, s, d)


MOE_TM = 1024
DMA_WINDOW = 128


def _router_body(x_ref, yf_ref, wf_ref, gf_ref, g_ref, sh_ref, sc_ref, wr_ref, br_ref,
                 xo_ref, h_ref, meta_ref, gw_ref, cnt_ref, carry):
    i = pl.program_id(0)

    @pl.when(i == 0)
    def _():
        carry[...] = jnp.zeros_like(carry)

    xm = x_ref[...] + gf_ref[0] * _dot(yf_ref[...].astype(BF16), wf_ref[...])
    xo_ref[...] = xm
    h = _norm_mod(xm, g_ref[...], sh_ref[0], sc_ref[0])
    _rows_to_tiles(h_ref, h)
    h_hi = h.astype(BF16)
    h_lo = (h - h_hi.astype(F32)).astype(BF16)
    by_hi = _dot(h_hi, wr_ref[...])
    logits = by_hi[:, 0:LANES] + by_hi[:, LANES:] + _dot(h_lo, wr_ref[:, 0:LANES]) + br_ref[...]
    lane = lax.broadcasted_iota(jnp.int32, logits.shape, 1)
    nl = logits.shape[-1]
    m1 = jnp.max(logits, axis=-1, keepdims=True)
    i1 = jnp.min(jnp.where(logits == m1, lane, nl), axis=-1, keepdims=True)
    rest = jnp.where(lane == i1, 3.0 * NEG_INF, logits)
    m2 = jnp.max(rest, axis=-1, keepdims=True)
    i2 = jnp.min(jnp.where(rest == m2, lane, nl), axis=-1, keepdims=True)
    e = jnp.exp(m2 - m1)
    gw_ref[...] = jnp.where(lane == 0, 1.0 / (1.0 + e), jnp.where(lane == 1, e / (1.0 + e), 0.0))
    onehot = jnp.where((lane == i1) | (lane == i2), 1.0, 0.0)
    tm = onehot.shape[0]
    earlier = lax.broadcasted_iota(jnp.int32, (tm, tm), 0) > lax.broadcasted_iota(jnp.int32, (tm, tm), 1)
    excl = _dot(jnp.where(earlier, 1.0, 0.0).astype(BF16), onehot.astype(BF16)) + carry[...]
    r1 = jnp.sum(jnp.where(lane == i1, excl, 0.0), axis=-1, keepdims=True).astype(jnp.int32)
    r2 = jnp.sum(jnp.where(lane == i2, excl, 0.0), axis=-1, keepdims=True).astype(jnp.int32)
    meta = jnp.where(lane == 0, i1, jnp.where(lane == 1, i2, jnp.where(lane == 2, r1, jnp.where(lane == 3, r2, 0))))
    meta_ref[...] = meta.T[0:ROW_SL]
    carry[...] = carry[...] + jnp.sum(onehot, axis=0, keepdims=True)
    cnt_ref[...] = carry[...]


def _router(x, y_f, w_f, gate_f, g, shift, scale, w_router, b_router, tm=512):
    b, s, d = x.shape
    t = b * s
    ne = w_router.shape[1]
    wr = jnp.pad(w_router.astype(F32), ((0, 0), (0, LANES - ne)))
    wr_hi = wr.astype(BF16)
    wr = jnp.concatenate([wr_hi, (wr - wr_hi.astype(F32)).astype(BF16)], axis=1)
    br = jnp.pad(b_router.astype(F32).reshape(1, ne), ((0, 0), (0, LANES - ne)), constant_values=NEG_INF)
    spt = s // tm
    per_b = pl.BlockSpec((1, 1, d), lambda i: (i // spt, 0, 0))
    const = lambda a: pl.BlockSpec(a.shape, lambda i: (0, 0))
    tok = pl.BlockSpec((tm, d), lambda i: (i, 0))
    return pl.pallas_call(
        _router_body,
        out_shape=[jax.ShapeDtypeStruct((t, d), F32),
                   jax.ShapeDtypeStruct((t * ROW_SL, LANES), F32),
                   jax.ShapeDtypeStruct((ROW_SL, t), jnp.int32),
                   jax.ShapeDtypeStruct((t, LANES), F32),
                   jax.ShapeDtypeStruct((1, LANES), F32)],
        grid=(t // tm,),
        in_specs=[tok, tok, const(w_f), per_b, const(g), per_b, per_b, const(wr), const(br)],
        out_specs=[tok,
                   pl.BlockSpec((tm * ROW_SL, LANES), lambda i: (i, 0)),
                   pl.BlockSpec((ROW_SL, tm), lambda i: (0, i)),
                   pl.BlockSpec((tm, LANES), lambda i: (i, 0)),
                   pl.BlockSpec((1, LANES), lambda i: (0, 0))],
        scratch_shapes=[pltpu.VMEM((1, LANES), F32)],
        compiler_params=_cparams(("arbitrary",)),
        name="router",
    )(x.reshape(t, d), y_f.reshape(t, d), w_f, gate_f, g, shift, scale, wr, br)


def _moe_plan(meta, counts, ne, tm):
    i1, i2, r1, r2 = meta[0], meta[1], meta[2], meta[3]
    cnt = counts[0, :ne].astype(jnp.int32)
    padded = ((cnt + tm - 1) // tm) * tm
    ends = jnp.cumsum(padded)
    offs = ends - padded
    pick = lambda idx: sum(jnp.where(idx == e, offs[e], 0) for e in range(ne))
    pos = jnp.concatenate([pick(i1) + r1, pick(i2) + r2]).astype(jnp.int32)
    n_tiles = (2 * i1.shape[0]) // tm + ne
    n_used = (ends[ne - 1] // tm).astype(jnp.int32)
    tile_start = jnp.minimum(jnp.arange(n_tiles, dtype=jnp.int32), n_used - 1) * tm
    tile_expert = jnp.sum(tile_start[:, None] >= ends[None, :], axis=1).astype(jnp.int32)
    group_end = sum(jnp.where(tile_expert == e, offs[e] + cnt[e], 0) for e in range(ne))
    tile_rows = jnp.clip(group_end - tile_start, 0, tm).astype(jnp.int32)
    return pos, offs + cnt, padded - cnt, tile_expert, n_used.reshape(1), tile_rows


def _windowed_copies(n, start_copy, wait_one):
    def body(i, carry):
        @pl.when(i >= DMA_WINDOW)
        def _():
            wait_one()
        start_copy(i)
        return carry

    lax.fori_loop(0, n, body, 0)

    def drain(i, carry):
        wait_one()
        return carry

    lax.fori_loop(0, jnp.minimum(n, DMA_WINDOW), drain, 0)


def _tile_of(ref, row):
    return ref.at[pl.ds(pl.multiple_of(row * ROW_SL, ROW_SL), ROW_SL)]


def _tiles_to_rows(ref, n, first=0):
    return jnp.concatenate([ref[pl.ds(first * ROW_SL + sl, n, stride=ROW_SL), :] for sl in range(ROW_SL)], axis=1)


def _rows_to_tiles(ref, val):
    n = val.shape[0]
    for sl in range(ROW_SL):
        ref[pl.ds(sl, n, stride=ROW_SL), :] = val[:, sl * LANES:(sl + 1) * LANES]


def _dispatch_body(pos_ref, pad_start_ref, pad_n_ref, h_ref, xs_hbm, sem, *, n_tok, ne):
    i = pl.program_id(0)
    td = h_ref.shape[0] // ROW_SL
    base = i * td
    copy = lambda src, dst: pltpu.make_async_copy(_tile_of(h_ref, src), _tile_of(xs_hbm, dst), sem)
    wait_one = lambda: copy(0, 0).wait()

    def start_token(r, carry):
        copy(r, pos_ref[base + r]).start(priority=0)
        copy(r, pos_ref[n_tok + base + r]).start(priority=1)
        return carry

    lax.fori_loop(0, td, start_token, 0, unroll=8)
    whole_tile = pltpu.make_async_copy(h_ref, xs_hbm.at[pl.ds(0, td * ROW_SL)], sem)
    whole_tile.wait()
    whole_tile.wait()

    @pl.when(i == 0)
    def _():
        for e in range(ne):
            first = pad_start_ref[e]
            _windowed_copies(pad_n_ref[e], lambda r: copy(0, first + r).start(), wait_one)


def _moe_dispatch(h3, pos, pad_start, pad_n, n_rows, td=1024):
    n_tok = h3.shape[0] // ROW_SL
    ne = pad_start.shape[0]
    return pl.pallas_call(
        functools.partial(_dispatch_body, n_tok=n_tok, ne=ne),
        out_shape=jax.ShapeDtypeStruct((n_rows * ROW_SL, LANES), h3.dtype),
        grid_spec=pltpu.PrefetchScalarGridSpec(
            num_scalar_prefetch=3, grid=(n_tok // td,),
            in_specs=[pl.BlockSpec((td * ROW_SL, LANES), lambda i, p, ps, pn: (i, 0))],
            out_specs=pl.BlockSpec(memory_space=pl.ANY),
            scratch_shapes=[pltpu.SemaphoreType.DMA(())]),
        compiler_params=_cparams(("arbitrary",)),
        name="moe_dispatch",
    )(pos, pad_start, pad_n, h3)


def _moe_grouped_body(te_ref, nu_ref, tr_ref, xs_ref, w1_ref, w3_ref, w2_ref, y_ref, xb_scr, acc_scr):
    i = pl.program_id(0)
    j = pl.program_id(1)
    tm = xb_scr.shape[0]
    hm = tm // 2

    def expert_rows(nrows):
        h = xb_scr[0:nrows]
        a = _dot(h, w1_ref[0].astype(BF16))
        u = (a * jax.nn.sigmoid(a) * _dot(h, w3_ref[0].astype(BF16))).astype(BF16)
        part = _dot(u, w2_ref[0].astype(BF16))

        @pl.when(j == 0)
        def _():
            acc_scr[0:nrows] = part

        @pl.when(j > 0)
        def _():
            acc_scr[0:nrows] += part

    @pl.when(i < nu_ref[0])
    def _():
        @pl.when(j == 0)
        def _():
            xb_scr[...] = _tiles_to_rows(xs_ref, tm).astype(BF16)

        @pl.when(tr_ref[i] > hm)
        def _():
            expert_rows(tm)

        @pl.when(tr_ref[i] <= hm)
        def _():
            expert_rows(hm)

            @pl.when(j == 0)
            def _():
                acc_scr[hm:tm] = jnp.zeros((tm - hm, acc_scr.shape[1]), F32)

        @pl.when(j == pl.num_programs(1) - 1)
        def _():
            _rows_to_tiles(y_ref, acc_scr[...])


def _moe_grouped(xs, tile_expert, n_used, tile_rows, w1, w3, w2, tm, fb=512):
    ne, d, f = w1.shape
    n_rows = xs.shape[0] // ROW_SL
    nj = f // fb
    row_tile = lambda i, j, te, nu, tr: (jnp.maximum(jnp.minimum(i, nu[0] - 1), 0), 0)
    jj = lambda i, j, nu: jnp.where(i < nu[0], j, nj - 1)
    return pl.pallas_call(
        _moe_grouped_body,
        out_shape=jax.ShapeDtypeStruct(xs.shape, F32),
        grid_spec=pltpu.PrefetchScalarGridSpec(
            num_scalar_prefetch=3, grid=(n_rows // tm, nj),
            in_specs=[pl.BlockSpec((tm * ROW_SL, LANES), row_tile),
                      pl.BlockSpec((1, d, fb), lambda i, j, te, nu, tr: (te[i], 0, jj(i, j, nu))),
                      pl.BlockSpec((1, d, fb), lambda i, j, te, nu, tr: (te[i], 0, jj(i, j, nu))),
                      pl.BlockSpec((1, fb, d), lambda i, j, te, nu, tr: (te[i], jj(i, j, nu), 0))],
            out_specs=pl.BlockSpec((tm * ROW_SL, LANES), row_tile),
            scratch_shapes=[pltpu.VMEM((tm, d), BF16), pltpu.VMEM((tm, d), F32)]),
        compiler_params=_cparams(("arbitrary", "arbitrary")),
        name="moe_grouped",
    )(tile_expert, n_used, tile_rows, xs, w1, w3, w2)


def _moe_final_body(pos_ref, x_ref, y_hbm, gw_ref, gt_ref, fg_ref, o_ref, yg_scr, sem, *, n_tok):
    i = pl.program_id(0)
    tc = x_ref.shape[0]
    slot = i % 2

    def gather_tile(step, into):
        base = step * tc

        def start_token(r, carry):
            dst = yg_scr.at[into]
            pltpu.make_async_copy(_tile_of(y_hbm, pos_ref[base + r]), _tile_of(dst, r),
                                  sem.at[into]).start(priority=0)
            pltpu.make_async_copy(_tile_of(y_hbm, pos_ref[n_tok + base + r]), _tile_of(dst, tc + r),
                                  sem.at[into]).start(priority=1)
            return carry

        lax.fori_loop(0, tc, start_token, 0, unroll=8)

    @pl.when(i == 0)
    def _():
        gather_tile(0, 0)

    @pl.when(i + 1 < pl.num_programs(0))
    def _():
        gather_tile(i + 1, 1 - slot)

    pltpu.make_async_copy(y_hbm.at[pl.ds(0, 2 * tc * ROW_SL)], yg_scr.at[slot], sem.at[slot]).wait()
    gw = gw_ref[...]
    rows = yg_scr.at[slot]
    y = gw[:, 0:1] * _tiles_to_rows(rows, tc) + gw[:, 1:2] * _tiles_to_rows(rows, tc, first=tc)
    xo = x_ref[...] + gt_ref[0] * y
    ms = jnp.mean(xo * xo, axis=-1, keepdims=True)
    o_ref[...] = xo * lax.rsqrt(ms + EPS) * fg_ref[...]


def _moe_final(x, y, pos, gw, gt, final_g, tc=512):
    b, s, d = x.shape
    t = b * s
    spt = s // tc
    out = pl.pallas_call(
        functools.partial(_moe_final_body, n_tok=t),
        out_shape=jax.ShapeDtypeStruct((t, d), F32),
        grid_spec=pltpu.PrefetchScalarGridSpec(
            num_scalar_prefetch=1, grid=(t // tc,),
            in_specs=[pl.BlockSpec((tc, d), lambda i, p: (i, 0)),
                      pl.BlockSpec(memory_space=pl.ANY),
                      pl.BlockSpec((tc, LANES), lambda i, p: (i, 0)),
                      pl.BlockSpec((1, 1, d), lambda i, p: (i // spt, 0, 0)),
                      pl.BlockSpec(final_g.shape, lambda i, p: (0, 0))],
            out_specs=pl.BlockSpec((tc, d), lambda i, p: (i, 0)),
            scratch_shapes=[pltpu.VMEM((2, 2 * tc * ROW_SL, LANES), F32), pltpu.SemaphoreType.DMA((2,))]),
        compiler_params=_cparams(("arbitrary",)),
        name="moe_final",
    )(pos, x.reshape(t, d), y, gw, gt, final_g)
    return out.reshape(b, s, d)


def _moe_routed(x, y_f, w_f, gate_f, g, shift, scale, gt, final_g, w_router, b_router, w1, w3, w2):
    ne = w1.shape[0]
    tm = MOE_TM
    x1, h3, meta, gw, counts = _router(x, y_f, w_f, gate_f, g, shift, scale, w_router, b_router)
    pos, pad_start, pad_n, tile_expert, n_used, tile_rows = _moe_plan(meta, counts, ne, tm)
    assert x.shape[-1] == ROW_SL * LANES
    n_rows = (2 * (h3.shape[0] // ROW_SL) // tm + ne) * tm
    xs = _moe_dispatch(h3, pos, pad_start, pad_n, n_rows)
    y = _moe_grouped(xs, tile_expert, n_used, tile_rows, w1, w3, w2, tm)
    return _moe_final(x1.reshape(x.shape), y, pos, gw, gt, final_g)


def kernel(x, c, ctx, c_ctx, w_ada, b_ada, norm_g, w_in, hy_short_w, hy_short_b, hy_f_w1, hy_f_b1, hy_f_w2, hy_f_b2, hy_f_w3, hy_f_freq, hy_skip, na_rpb, w_mix_out, ffn_w1, ffn_w3, ffn_w2, w_fourier, w_router, b_router, moe_w1, moe_w3, moe_w2, final_g):
    b, s, d = x.shape
    depth = w_ada.shape[0]
    assert depth == 2, "layer 0 mixes with Hyena/attention, layer 1 with Fourier/MoE"
    c_hy = hy_skip.shape[-1]
    c_na = d - c_hy

    cvec = jnp.concatenate([c, c_ctx[None, :], jnp.zeros((8 - b - 1, d), F32)], axis=0)
    mods = _ada(cvec, w_ada, b_ada)

    def mod(layer, idx, ctx_row=False):
        m = mods[layer, :, idx * d:(idx + 1) * d]
        return m[b:b + 1, None, :] if ctx_row else m[0:b, None, :]

    row = lambda a: a.reshape(1, -1)

    w_in0 = w_in[0].astype(BF16)
    w_hy, w_qkv = w_in0[:, 0:3 * c_hy], w_in0[:, 3 * c_hy:]
    v, x1, x2, q, k, va = _inproj(x, row(norm_g[0, 0]), mod(0, 0), mod(0, 1), w_hy, w_qkv,
                                  hy_short_w[0], row(hy_short_b[0]))
    kc, vc = _ctxkv(ctx, row(norm_g[0, 0]), mod(0, 0, True), mod(0, 1, True), w_qkv[:, c_na:])
    y_na = _natt(q, k, va, kc, vc, _na_bias_table(na_rpb[0]))
    y_hy = _hyena(v, x1, x2, hy_f_w1[0], hy_f_b1[0], hy_f_w2[0], hy_f_b2[0], hy_f_w3[0],
                  hy_f_freq[0], hy_skip[0])
    x = _mix_ffn(x, y_hy, y_na, w_mix_out[0].astype(BF16), mod(0, 2),
                 row(norm_g[0, 1]), mod(0, 3), mod(0, 4), mod(0, 5),
                 ffn_w1[0].astype(BF16), ffn_w3[0].astype(BF16), ffn_w2[0].astype(BF16))

    y_f = _fourier_mix(x, row(norm_g[1, 0]), mod(1, 0), mod(1, 1))
    return _moe_routed(x, y_f, w_fourier[0].astype(BF16), mod(1, 2),
                       row(norm_g[1, 1]), mod(1, 3), mod(1, 4), mod(1, 5), row(final_g),
                       w_router[0], b_router[0],
                       moe_w1[0], moe_w3[0], moe_w2[0])
```

```python
import functools
import math

import numpy as np
import jax
import jax.numpy as jnp
from jax import lax
from jax.experimental import pallas as pl
from jax.experimental.pallas import tpu as pltpu

F32 = jnp.float32
BF16 = jnp.bfloat16
HIGHEST = lax.Precision.HIGHEST

GRID_W = 64
NA_HEAD_DIM = 32
NA_WIN_R = 8
NA_WIN_C = 16
HYENA_EMB = 33
HYENA_BANDS = (HYENA_EMB - 1) // 2
HYENA_FAST_DECAY = 0.3
HYENA_SLOW_DECAY = 1.5
HYENA_TARGET = 1e-2
F_GROUPS = 4
N_MOD = 6
EPS = 1e-6
NEG_INF = -1e30

FFT_A = 64
FFT_R = 128
FFT_KA = FFT_A // 2 + 1
FFT_KA_PAD = 40
FM_A = 64

LANES = 128
ROW_SL = 8
VMEM_LIMIT = 48 * 1024 * 1024
LARGE_VMEM_LIMIT = 56 * 1024 * 1024


def _cparams(sem, vmem_limit=VMEM_LIMIT):
    return pltpu.CompilerParams(dimension_semantics=sem, vmem_limit_bytes=vmem_limit)


def _dot(a, b):
    return jnp.dot(a, b, preferred_element_type=F32)


def _mxu_const(m):
    return jnp.asarray(m, dtype=F32).astype(BF16)


def _norm_mod(x, g, shift, scale):
    ms = jnp.mean(x * x, axis=-1, keepdims=True)
    y = x * lax.rsqrt(ms + EPS) * g
    return y * (1.0 + scale) + shift


def _ada_body(c_ref, w_ref, b_ref, o_ref):
    cv = c_ref[...]
    s = cv * jax.nn.sigmoid(cv)
    o_ref[0] = jnp.dot(s, w_ref[0], precision=HIGHEST, preferred_element_type=F32) + b_ref[0]


def _ada(cvec, w_ada, b_ada):
    depth, d, n = w_ada.shape
    rows = cvec.shape[0]
    bn = n // 4
    return pl.pallas_call(
        _ada_body,
        out_shape=jax.ShapeDtypeStruct((depth, rows, n), F32),
        grid=(depth, n // bn),
        in_specs=[pl.BlockSpec((rows, d), lambda l, j: (0, 0)),
                  pl.BlockSpec((1, d, bn), lambda l, j: (l, 0, j)),
                  pl.BlockSpec((1, 1, bn), lambda l, j: (l, 0, j))],
        out_specs=pl.BlockSpec((1, rows, bn), lambda l, j: (l, 0, j)),
        compiler_params=_cparams(("parallel", "parallel")),
        name="ada",
    )(cvec, w_ada, b_ada.reshape(depth, 1, n))


def _inproj_body(x_ref, xp_ref, xn_ref, g_ref, sh_ref, sc_ref, why_ref, wqkv_ref, sw_ref, sb_ref,
                 v_ref, x1_ref, x2_ref, q_ref, k_ref, va_ref, *, n_tiles, q_scale, c_hy, c_na):
    i = pl.program_id(1)
    g, sh, sc = g_ref[...], sh_ref[0], sc_ref[0]
    hf = _norm_mod(x_ref[0], g, sh, sc)
    h = hf.astype(BF16)
    tm = hf.shape[0]
    hx = jnp.concatenate([_norm_mod(xp_ref[0], g, sh, sc), hf, _norm_mod(xn_ref[0], g, sh, sc)],
                         axis=0).astype(BF16)
    row = lax.broadcasted_iota(jnp.int32, (tm, c_hy), 0)
    sw = sw_ref[...]
    sb = sb_ref[...]
    lo, hi = ROW_SL, ROW_SL + tm
    for ci, out_ref in enumerate((v_ref, x1_ref, x2_ref)):
        cols = slice(ci * c_hy, (ci + 1) * c_hy)
        zx = _dot(hx, why_ref[:, cols])
        zh = zx[lo:hi]
        zp = jnp.where(i > 0, zx[lo - 1:lo], 0.0)
        zn = jnp.where(i < n_tiles - 1, zx[hi:hi + 1], 0.0)
        z_m1 = jnp.where(row == 0, zp, pltpu.roll(zh, 1, 0))
        z_p1 = jnp.where(row == tm - 1, zn, pltpu.roll(zh, tm - 1, 0))
        out_ref[0] = z_m1 * sw[0:1, cols] + zh * sw[1:2, cols] + z_p1 * sw[2:3, cols] + sb[:, cols]
    for ci, (out_ref, mult) in enumerate(((q_ref, q_scale), (k_ref, None), (va_ref, None))):
        z = _dot(h, wqkv_ref[:, ci * c_na:(ci + 1) * c_na])
        out_ref[0] = (z if mult is None else z * mult).astype(BF16)


def _inproj(x, g, shift, scale, w_hy, w_qkv, short_w, short_b, tm=1024):
    b, s, d = x.shape
    c_hy = w_hy.shape[1] // 3
    c_na = w_qkv.shape[1] // 3
    n_tiles = s // tm
    halo_per_tile = tm // ROW_SL
    last_halo = s // ROW_SL - 1
    body = functools.partial(_inproj_body, n_tiles=n_tiles, q_scale=NA_HEAD_DIM ** -0.5,
                             c_hy=c_hy, c_na=c_na)
    tok = lambda c: pl.BlockSpec((1, tm, c), lambda bi, i: (bi, i, 0))
    full2 = lambda a: pl.BlockSpec(a.shape, lambda bi, i: (0, 0))
    resident = lambda a: pl.BlockSpec(a.shape, lambda bi, i: (0, 0), pipeline_mode=pl.Buffered(1))
    per_b = pl.BlockSpec((1, 1, d), lambda bi, i: (bi, 0, 0))
    return pl.pallas_call(
        body,
        out_shape=[jax.ShapeDtypeStruct((b, s, c_hy), F32)] * 3 + [jax.ShapeDtypeStruct((b, s, c_na), BF16)] * 3,
        grid=(b, n_tiles),
        in_specs=[tok(d),
                  pl.BlockSpec((1, ROW_SL, d), lambda bi, i: (bi, jnp.maximum(i * halo_per_tile - 1, 0), 0)),
                  pl.BlockSpec((1, ROW_SL, d),
                               lambda bi, i: (bi, jnp.minimum((i + 1) * halo_per_tile, last_halo), 0)),
                  full2(g), per_b, per_b, resident(w_hy), resident(w_qkv), full2(short_w), full2(short_b)],
        out_specs=[tok(c_hy)] * 3 + [tok(c_na)] * 3,
        compiler_params=_cparams(("parallel", "parallel"), vmem_limit=LARGE_VMEM_LIMIT),
        name="inproj",
    )(x, x, x, g, shift, scale, w_hy, w_qkv, short_w, short_b)


def _ctxkv_body(x_ref, g_ref, sh_ref, sc_ref, w_ref, k_ref, v_ref, *, c_na):
    h = _norm_mod(x_ref[0], g_ref[...], sh_ref[0], sc_ref[0]).astype(BF16)
    z = _dot(h, w_ref[...])
    k_ref[0] = z[:, 0:c_na].astype(BF16)
    v_ref[0] = z[:, c_na:2 * c_na].astype(BF16)


def _ctxkv(ctx, g, shift, scale, w_kv):
    b, n, d = ctx.shape
    c_na = w_kv.shape[1] // 2
    one = pl.BlockSpec((1, 1, d), lambda bi: (0, 0, 0))
    return pl.pallas_call(
        functools.partial(_ctxkv_body, c_na=c_na),
        out_shape=[jax.ShapeDtypeStruct((b, n, c_na), BF16)] * 2,
        grid=(b,),
        in_specs=[pl.BlockSpec((1, n, d), lambda bi: (bi, 0, 0)),
                  pl.BlockSpec(g.shape, lambda bi: (0, 0)), one, one,
                  pl.BlockSpec(w_kv.shape, lambda bi: (0, 0))],
        out_specs=[pl.BlockSpec((1, n, c_na), lambda bi: (bi, 0, 0))] * 2,
        compiler_params=_cparams(("parallel",)),
        name="ctxkv",
    )(ctx, g, shift, scale, w_kv)


NA_HEADS_PER_BLK = 8
NA_ROWS_PER_STEP = 8


def _na_bias_body(r_ref, ok_ref, o_ref):
    w = GRID_W
    lane = lax.broadcasted_iota(jnp.int32, (w, LANES), 1)
    ok = ok_ref[...] > 0.5

    def toeplitz(ri):
        row = jnp.broadcast_to(r_ref[0, ri:ri + 1, :], (w, LANES))
        return pltpu.roll(row, LANES - (NA_WIN_C - 1), 1, stride=1, stride_axis=0)

    blocks = [toeplitz(ri) for ri in range(2 * NA_WIN_R - 1)]
    for off in range(NA_WIN_R):
        first = NA_WIN_R - 1 - off
        pairs = []
        for p in range(NA_WIN_R // 2):
            a, b = blocks[first + 2 * p], blocks[first + 2 * p + 1]
            pairs.append(jnp.where(ok, jnp.where(lane < w, a, pltpu.roll(b, w, 1)), NEG_INF))
        o_ref[0, off] = jnp.concatenate(pairs, axis=1)


def _na_bias_table(rpb):
    w = GRID_W
    h, nr, nc = rpb.shape
    assert 2 * w == LANES and nr == 2 * NA_WIN_R - 1 and nc == 2 * NA_WIN_C - 1
    hpb = NA_HEADS_PER_BLK
    col = np.arange(w)[:, None]
    kc = np.arange(LANES)[None, :] % w
    c_start = np.clip(col - NA_WIN_C // 2, 0, w - NA_WIN_C)
    ok = jnp.asarray((kc >= c_start) & (kc < c_start + NA_WIN_C), dtype=F32)
    rp = jnp.pad(rpb.astype(F32), ((0, 0), (0, 2 * ROW_SL - nr), (0, LANES - nc)))
    return pl.pallas_call(
        _na_bias_body,
        out_shape=jax.ShapeDtypeStruct((h // hpb, NA_WIN_R, hpb * w, NA_WIN_R * w), F32),
        grid=(h,),
        in_specs=[pl.BlockSpec((1,) + rp.shape[1:], lambda i: (i, 0, 0)),
                  pl.BlockSpec(ok.shape, lambda i: (0, 0))],
        out_specs=pl.BlockSpec((1, NA_WIN_R, w, NA_WIN_R * w), lambda i: (i // hpb, 0, i % hpb, 0)),
        compiler_params=_cparams(("parallel",)),
        name="na_bias",
    )(rp, ok)


def _natt_body(q_ref, k_ref, v_ref, kc_ref, vc_ref, bias_ref, o_ref, *, rows):
    w = GRID_W
    hpb = NA_HEADS_PER_BLK
    nloc = NA_WIN_R * w
    lane = lax.broadcasted_iota(jnp.int32, (1, hpb * NA_HEAD_DIM), 1)
    in_head = [(lane >= NA_HEAD_DIM * hh) & (lane < NA_HEAD_DIM * (hh + 1)) for hh in range(hpb)]
    kcx = kc_ref[0]
    vcx = vc_ref[0]
    nt = (((1,), (1,)), ((), ()))

    def one_row(r):
        r0 = jnp.clip(r - NA_WIN_R // 2, 0, rows - NA_WIN_R)
        off = r - r0
        qs = q_ref[0, pl.ds(pl.multiple_of(r * w, w), w), :]
        kw = k_ref[0, pl.ds(pl.multiple_of(r0 * w, w), nloc), :]
        vw = v_ref[0, pl.ds(pl.multiple_of(r0 * w, w), nloc), :]
        zero = jnp.zeros_like(qs)
        qst = jnp.concatenate([jnp.where(m, qs, zero) for m in in_head], axis=0)
        s_loc = lax.dot_general(qst, kw, nt, preferred_element_type=F32) + bias_ref[0, off]
        s_ctx = lax.dot_general(qst, kcx, nt, preferred_element_type=F32)
        m = jnp.maximum(jnp.max(s_loc, axis=-1, keepdims=True), jnp.max(s_ctx, axis=-1, keepdims=True))
        p_loc = jnp.exp(s_loc - m)
        p_ctx = jnp.exp(s_ctx - m)
        den = jnp.sum(p_loc, axis=-1, keepdims=True) + jnp.sum(p_ctx, axis=-1, keepdims=True)
        o = (_dot(p_loc.astype(BF16), vw) + _dot(p_ctx.astype(BF16), vcx)) * (1.0 / den)
        acc = jnp.where(in_head[0], o[0:w], 0.0)
        for hh in range(1, hpb):
            acc = acc + jnp.where(in_head[hh], o[hh * w:(hh + 1) * w], 0.0)
        o_ref[0, pl.ds(pl.multiple_of(r * w, w), w), :] = acc.astype(BF16)

    def row_group(i, carry):
        for r in range(NA_ROWS_PER_STEP):
            one_row(NA_ROWS_PER_STEP * i + r)
        return carry

    lax.fori_loop(0, rows // NA_ROWS_PER_STEP, row_group, 0)


def _natt(q, k, v, kc, vc, bias):
    b, s, c = q.shape
    nctx = kc.shape[1]
    lw = NA_HEADS_PER_BLK * NA_HEAD_DIM
    rows = s // GRID_W
    seq = pl.BlockSpec((1, s, lw), lambda bi, g: (bi, 0, g))
    cx = pl.BlockSpec((1, nctx, lw), lambda bi, g: (bi, 0, g))
    return pl.pallas_call(
        functools.partial(_natt_body, rows=rows),
        out_shape=jax.ShapeDtypeStruct((b, s, c), BF16),
        grid=(b, c // lw),
        in_specs=[seq, seq, seq, cx, cx,
                  pl.BlockSpec((1,) + bias.shape[1:], lambda bi, g: (g, 0, 0, 0))],
        out_specs=seq,
        compiler_params=_cparams(("parallel", "parallel")),
        name="natt",
    )(q, k, v, kc, vc, bias)


def _hyena_feats(seq_len):
    t = jnp.linspace(0.0, 1.0, seq_len, dtype=F32)[:, None]
    bands = jnp.linspace(1e-4, HYENA_BANDS - 1, HYENA_BANDS, dtype=F32)
    ang = (2.0 * math.pi / seq_len) * jnp.arange(seq_len, dtype=F32)[:, None] * bands[None, :]
    feats = jnp.concatenate([t, jnp.cos(ang), -jnp.sin(ang)], axis=-1)
    return jnp.pad(feats, ((0, 0), (0, LANES - HYENA_EMB)))


def _filt_body(feat_ref, w1_ref, b1_ref, w2_ref, b2_ref, w3_ref, fr_ref, dl_ref, o_ref, l1_ref, h_scr,
               *, halves):
    j = pl.program_id(0)
    hp = functools.partial(jnp.dot, precision=HIGHEST, preferred_element_type=F32)
    feats = feat_ref[...]

    @pl.when(j == 0)
    def _():
        fr = fr_ref[...]
        h = jnp.sin(fr[0:1] * (hp(feats, w1_ref[...]) + b1_ref[...]))
        h_scr[...] = jnp.sin(fr[1:2] * (hp(h, w2_ref[...]) + b2_ref[...]))

    h2 = h_scr[...]
    h_hi = h2.astype(BF16)
    h_lo = (h2 - h_hi.astype(F32)).astype(BF16)
    w3 = w3_ref[...]
    w_hi = w3.astype(BF16)
    w_lo = (w3 - w_hi.astype(F32)).astype(BF16)
    hc = (_dot(jnp.concatenate([h_hi, h_lo], axis=1), jnp.concatenate([w_hi, w_hi], axis=0))
          + _dot(h_hi, w_lo))
    t = feats[:, 0:1]
    hc = hc * jnp.exp(-t * dl_ref[...])
    row = lax.broadcasted_iota(jnp.int32, hc.shape, 0)
    hc = jnp.where((row == 0) & ((j // halves) % 2 == 1), 0.0, hc)
    l1_ref[0] = jnp.sum(jnp.abs(hc), axis=0, keepdims=True)
    o_ref[0] = hc


def _hyena_filter_taps(seq_len, f_w1, f_b1, f_w2, f_b2, f_w3, f_freq, c_hy):
    feats = _hyena_feats(seq_len)
    hid = f_w1.shape[1]
    w1 = jnp.pad(f_w1.astype(F32), ((0, LANES - HYENA_EMB), (0, 0)))
    deltas = jnp.abs(jnp.linspace(math.log(HYENA_TARGET) / HYENA_SLOW_DECAY,
                                  math.log(HYENA_TARGET) / HYENA_FAST_DECAY, c_hy, dtype=F32))[None, :]
    nblk = f_w3.shape[1] // c_hy
    halves = 2
    cb = c_hy // halves
    c0 = lambda a: pl.BlockSpec(a.shape, lambda j: (0, 0))
    b1, b2 = f_b1.reshape(1, hid), f_b2.reshape(1, hid)
    return pl.pallas_call(
        functools.partial(_filt_body, halves=halves),
        out_shape=[jax.ShapeDtypeStruct((nblk, seq_len, c_hy), F32),
                   jax.ShapeDtypeStruct((nblk, 1, c_hy), F32)],
        grid=(nblk * halves,),
        in_specs=[c0(feats), c0(w1), c0(b1), c0(f_w2), c0(b2),
                  pl.BlockSpec((hid, cb), lambda j: (0, j)), c0(f_freq),
                  pl.BlockSpec((1, cb), lambda j: (0, j % halves))],
        out_specs=[pl.BlockSpec((1, seq_len, cb), lambda j: (j // halves, 0, j % halves)),
                   pl.BlockSpec((1, 1, cb), lambda j: (j // halves, 0, j % halves))],
        scratch_shapes=[pltpu.VMEM((seq_len, hid), F32)],
        compiler_params=_cparams(("arbitrary",)),
        name="hyena_filter",
    )(feats, w1, b1, f_w2, b2, f_w3, f_freq, deltas)


def _conv_dft_constants():
    a_half = FFT_A // 2
    n = FFT_A * FFT_R
    ka = np.arange(FFT_KA)[:, None]
    a = np.arange(a_half)[None, :]
    ph = 2.0 * np.pi * ka * a / FFT_A
    m_fwd = np.zeros((2 * FFT_KA_PAD, a_half))
    m_fwd[:FFT_KA] = np.cos(ph)
    m_fwd[FFT_KA_PAD:FFT_KA_PAD + FFT_KA] = -np.sin(ph)
    wgt = np.where((ka == 0) | (ka == FFT_A // 2), 1.0, 2.0)
    m_inv = np.zeros((a_half, 2 * FFT_KA_PAD))
    m_inv[:, :FFT_KA] = (wgt * np.cos(ph)).T / n
    m_inv[:, FFT_KA_PAD:FFT_KA_PAD + FFT_KA] = (-wgt * np.sin(ph)).T / n
    kb = np.arange(FFT_R)[None, :, None]
    b = np.arange(FFT_R)[None, None, :]
    kaa = np.arange(FFT_KA)[:, None, None]
    th = 2.0 * np.pi * (b * kb / FFT_R + b * kaa / n)
    gr, gi = np.cos(th), -np.sin(th)
    g2 = np.zeros((FFT_KA_PAD, 2 * FFT_R, 2 * FFT_R))
    g2[:FFT_KA] = np.block([[gr, -gi], [gi, gr]])
    grt, git = gr.transpose(0, 2, 1), gi.transpose(0, 2, 1)
    g2h = np.zeros_like(g2)
    g2h[:FFT_KA] = np.block([[grt, git], [-git, grt]])
    return _mxu_const(m_fwd), _mxu_const(m_inv), _mxu_const(g2), _mxu_const(g2h)


FFT_NB = 32


def _fwd1_body(m_ref, u_ref, o_ref):
    u = jnp.concatenate([u_ref[0, :, bb, :] for bb in range(FFT_NB)], axis=1).astype(BF16)
    res = _dot(m_ref[...], u)
    o_ref[0, 0] = res[0:FFT_KA_PAD]
    o_ref[0, 1] = res[FFT_KA_PAD:2 * FFT_KA_PAD]


def _conv_fwd1(u, m_fwd):
    n, seq, c = u.shape
    a_half = FFT_A // 2
    return pl.pallas_call(
        _fwd1_body,
        out_shape=jax.ShapeDtypeStruct((n, 2, FFT_KA_PAD, FFT_R * c), F32),
        grid=(n, FFT_R // FFT_NB),
        in_specs=[pl.BlockSpec(m_fwd.shape, lambda i, j: (0, 0)),
                  pl.BlockSpec((1, a_half, FFT_NB, c), lambda i, j: (i, 0, j, 0))],
        out_specs=pl.BlockSpec((1, 2, FFT_KA_PAD, FFT_NB * c), lambda i, j: (i, 0, 0, j)),
        compiler_params=_cparams(("parallel", "parallel")),
        name="conv_fwd1",
    )(m_fwd, u.reshape(n, a_half, FFT_R, c))


FFT_KB = 8


def _rows_to_slabs(src_ref, dst_scr, c):
    for part in range(2):
        for b in range(FFT_R):
            dst_scr[part, :, b, :] = src_ref[0, part, :, b * c:(b + 1) * c]


def _slabs_to_rows(src_scr, dst_ref, c):
    for part in range(2):
        for b in range(FFT_R):
            dst_ref[0, part, :, b * c:(b + 1) * c] = src_scr[part, :, b, :]


def _slab(scr, i):
    return jnp.concatenate([scr[0, i], scr[1, i]], axis=0).astype(BF16)


def _per_ka_block(j, work, clear):
    full_blocks = FFT_KA // FFT_KB
    tail = FFT_KA - full_blocks * FFT_KB

    @pl.when(j < full_blocks)
    def _():
        for i in range(FFT_KB):
            work(i)

    @pl.when(j >= full_blocks)
    def _():
        for i in range(FFT_KB):
            (work if i < tail else clear)(i)


def _fwd2f_body(sf_ref, sb_ref, g_ref, l1_ref, kf_ref, f3, b3):
    o = pl.program_id(0)
    j = pl.program_id(1)
    r2 = 2 * FFT_R
    c = kf_ref.shape[-1]
    _rows_to_slabs(sf_ref, f3, c)
    _rows_to_slabs(sb_ref, b3, c)
    inv = 1.0 / (l1_ref[2 * o] + l1_ref[2 * o + 1] + EPS)

    def spectrum(i):
        xf = _dot(g_ref[i], _slab(f3, i))
        xb = _dot(g_ref[i], _slab(b3, i))
        kf_ref[0, i, 0:FFT_R] = (xf[0:FFT_R] + xb[0:FFT_R]) * inv
        kf_ref[0, i, FFT_R:r2] = (xf[FFT_R:r2] - xb[FFT_R:r2]) * inv

    def clear(i):
        kf_ref[0, i] = jnp.zeros((r2, c), F32)

    _per_ka_block(j, spectrum, clear)


def _filter_spectrum(s_filt, l1, g2, c):
    n_ord = s_filt.shape[0] // 2
    cols = s_filt.shape[-1]
    r2 = 2 * FFT_R
    return pl.pallas_call(
        _fwd2f_body,
        out_shape=jax.ShapeDtypeStruct((n_ord, FFT_KA_PAD, r2, c), F32),
        grid=(n_ord, FFT_KA_PAD // FFT_KB),
        in_specs=[pl.BlockSpec((1, 2, FFT_KB, cols), lambda o, j: (2 * o, 0, j, 0)),
                  pl.BlockSpec((1, 2, FFT_KB, cols), lambda o, j: (2 * o + 1, 0, j, 0)),
                  pl.BlockSpec((FFT_KB, r2, r2), lambda o, j: (j, 0, 0)),
                  pl.BlockSpec(l1.shape, lambda o, j: (0, 0, 0))],
        out_specs=pl.BlockSpec((1, FFT_KB, r2, c), lambda o, j: (o, j, 0, 0)),
        scratch_shapes=[pltpu.VMEM((2, FFT_KB, FFT_R, c), F32)] * 2,
        compiler_params=_cparams(("parallel", "parallel")),
        name="filter_spectrum",
    )(s_filt, s_filt, g2, l1)


def _mid_body(s_ref, g_ref, gh_ref, kf_ref, t_ref, s3, t3):
    j = pl.program_id(1)
    r2 = 2 * FFT_R
    c = kf_ref.shape[-1]
    _rows_to_slabs(s_ref, s3, c)

    def convolve(i):
        x = _dot(g_ref[i], _slab(s3, i))
        xr, xi = x[0:FFT_R], x[FFT_R:r2]
        kr, ki = kf_ref[0, i, 0:FFT_R], kf_ref[0, i, FFT_R:r2]
        y = jnp.concatenate([xr * kr - xi * ki, xr * ki + xi * kr], axis=0).astype(BF16)
        t = _dot(gh_ref[i], y)
        t3[0, i] = t[0:FFT_R]
        t3[1, i] = t[FFT_R:r2]

    def clear(i):
        t3[0, i] = jnp.zeros((FFT_R, c), F32)
        t3[1, i] = jnp.zeros((FFT_R, c), F32)

    _per_ka_block(j, convolve, clear)
    _slabs_to_rows(t3, t_ref, c)


def _conv_mid(s, kf, order, g2, g2h, c):
    n, _, _, cols = s.shape
    r2 = 2 * FFT_R
    blk = pl.BlockSpec((1, 2, FFT_KB, cols), lambda i, j: (i, 0, j, 0))
    gspec = pl.BlockSpec((FFT_KB, r2, r2), lambda i, j: (j, 0, 0))
    return pl.pallas_call(
        _mid_body,
        out_shape=jax.ShapeDtypeStruct(s.shape, F32),
        grid=(n, FFT_KA_PAD // FFT_KB),
        in_specs=[blk, gspec, gspec,
                  pl.BlockSpec((1, FFT_KB, r2, c), lambda i, j: (order, j, 0, 0))],
        out_specs=blk,
        scratch_shapes=[pltpu.VMEM((2, FFT_KB, FFT_R, c), F32)] * 2,
        compiler_params=_cparams(("parallel", "parallel")),
        name="conv_mid",
    )(s, g2, g2h, kf)


def _inv1_body(m_ref, mf_ref, t_ref, u_ref, xg_ref, sk_ref, o_ref, *s_ref):
    c = u_ref.shape[-1]
    t2 = t_ref[0].reshape(2 * FFT_KA_PAD, FFT_NB * c).astype(BF16)
    y = _dot(m_ref[...], t2)
    gated = []
    for bb in range(FFT_NB):
        conv = y[:, bb * c:(bb + 1) * c] + u_ref[0, :, bb, :] * sk_ref[...]
        gated.append(xg_ref[0, :, bb, :] * conv)
        o_ref[0, :, bb, :] = gated[-1]
    if s_ref:
        res = _dot(mf_ref[...], jnp.concatenate(gated, axis=1).astype(BF16))
        s_ref[0][0, 0] = res[0:FFT_KA_PAD]
        s_ref[0][0, 1] = res[FFT_KA_PAD:2 * FFT_KA_PAD]


def _conv_inv1(t, u, xg, skip, m_inv, m_fwd, with_spectrum):
    n, seq, c = u.shape
    a_half = FFT_A // 2
    sk = skip.astype(F32).reshape(1, c)
    uspec = pl.BlockSpec((1, a_half, FFT_NB, c), lambda i, j: (i, 0, j, 0))
    sspec = pl.BlockSpec((1, 2, FFT_KA_PAD, FFT_NB * c), lambda i, j: (i, 0, 0, j))
    view = lambda a: a.reshape(n, a_half, FFT_R, c)
    y_shape = jax.ShapeDtypeStruct((n, a_half, FFT_R, c), F32)
    s_shape = jax.ShapeDtypeStruct(t.shape, F32)
    outs = pl.pallas_call(
        _inv1_body,
        out_shape=[y_shape, s_shape] if with_spectrum else [y_shape],
        grid=(n, FFT_R // FFT_NB),
        in_specs=[pl.BlockSpec(m_inv.shape, lambda i, j: (0, 0)),
                  pl.BlockSpec(m_fwd.shape, lambda i, j: (0, 0)),
                  sspec, uspec, uspec,
                  pl.BlockSpec((1, c), lambda i, j: (0, 0))],
        out_specs=[uspec, sspec] if with_spectrum else [uspec],
        compiler_params=_cparams(("parallel", "parallel")),
        name="conv_inv1",
    )(m_inv, m_fwd, t, view(u), view(xg), sk)
    y = outs[0].reshape(n, seq, c)
    return (y, outs[1]) if with_spectrum else (y, None)


def _hyena(v, x1, x2, f_w1, f_b1, f_w2, f_b2, f_w3, f_freq, skip):
    _, seq, c = v.shape
    assert 2 * seq == FFT_A * FFT_R
    m_fwd, m_inv, g2, g2h = _conv_dft_constants()
    taps, l1 = _hyena_filter_taps(seq, f_w1, f_b1, f_w2, f_b2, f_w3, f_freq, c)
    kf = _filter_spectrum(_conv_fwd1(taps, m_fwd), l1, g2, c)
    y, s = v, _conv_fwd1(v, m_fwd)
    gates = (x1, x2)
    for order, xg in enumerate(gates):
        t = _conv_mid(s, kf, order, g2, g2h, c)
        y, s = _conv_inv1(t, y, xg, skip[order], m_inv, m_fwd, with_spectrum=order + 1 < len(gates))
    return y


def _mix_ffn_body(x_ref, a1_ref, a2_ref, wm_ref, gm_ref, g_ref, sh_ref, sc_ref, gt_ref,
                  w1_ref, w3_ref, w2_ref, o_ref, *, fb):
    c1 = a1_ref.shape[-1]
    mixed = _dot(a1_ref[0].astype(BF16), wm_ref[0:c1]) + _dot(a2_ref[0].astype(BF16), wm_ref[c1:])
    xm = x_ref[0] + gm_ref[0] * mixed
    h = _norm_mod(xm, g_ref[...], sh_ref[0], sc_ref[0]).astype(BF16)
    acc = None
    for lo in range(0, w1_ref.shape[1], fb):
        a = _dot(h, w1_ref[:, lo:lo + fb])
        u = (a * jax.nn.sigmoid(a) * _dot(h, w3_ref[:, lo:lo + fb])).astype(BF16)
        part = _dot(u, w2_ref[lo:lo + fb, :])
        acc = part if acc is None else acc + part
    o_ref[0] = xm + gt_ref[0] * acc


def _mix_ffn(x, a1, a2, w_mix, gate_mix, g, shift, scale, gate, w1, w3, w2, tm=512, fb=1408):
    b, s, d = x.shape
    tok = lambda c: pl.BlockSpec((1, tm, c), lambda bi, i: (bi, i, 0))
    per_b = pl.BlockSpec((1, 1, d), lambda bi, i: (bi, 0, 0))
    const = lambda a: pl.BlockSpec(a.shape, lambda bi, i: (0, 0))
    resident = lambda a: pl.BlockSpec(a.shape, lambda bi, i: (0, 0), pipeline_mode=pl.Buffered(1))
    return pl.pallas_call(
        functools.partial(_mix_ffn_body, fb=fb),
        out_shape=jax.ShapeDtypeStruct(x.shape, F32),
        grid=(b, s // tm),
        in_specs=[tok(d), tok(a1.shape[-1]), tok(a2.shape[-1]), resident(w_mix), per_b,
                  const(g), per_b, per_b, per_b, resident(w1), resident(w3), resident(w2)],
        out_specs=tok(d),
        compiler_params=_cparams(("parallel", "parallel"), vmem_limit=LARGE_VMEM_LIMIT),
        name="mix_ffn",
    )(x, a1, a2, w_mix, gate_mix, g, shift, scale, gate, w1, w3, w2)


def _fm_constants(cg):
    j = np.arange(cg)[:, None]
    m = np.arange(cg)[None, :]
    ph = 2.0 * np.pi * j * m / cg
    w_cs = np.concatenate([np.cos(ph), np.sin(ph)], axis=1)
    d = np.arange(FM_A)[:, None]
    a = np.arange(FM_A)[None, :]
    ph = 2.0 * np.pi * d * a / FM_A
    fr, fi = np.cos(ph), -np.sin(ph)
    m1 = np.block([[fr, fi], [fi, -fr]])
    n = FM_A * FM_A
    dd = np.arange(FM_A)[:, None, None]
    c = np.arange(FM_A)[None, :, None]
    b = np.arange(FM_A)[None, None, :]
    th = 2.0 * np.pi * (b * c / FM_A + b * dd / n)
    gcat = np.concatenate([np.cos(th), np.sin(th)], axis=2)
    return _mxu_const(w_cs), _mxu_const(m1), _mxu_const(gcat)


def _fm_front_body(x_ref, g_ref, sh_ref, sc_ref, w_ref, m_ref, o_ref, *, cg, nb):
    d = x_ref.shape[-1]
    xs = jnp.concatenate([x_ref[0, :, bb, :] for bb in range(nb)], axis=0)
    h = _norm_mod(xs, g_ref[...], sh_ref[0], sc_ref[0]).astype(BF16)
    pq = [_dot(h[:, grp * cg:(grp + 1) * cg], w_ref[...]) for grp in range(d // cg)]
    p = jnp.concatenate([t[:, 0:cg] for t in pq], axis=1)
    q = jnp.concatenate([t[:, cg:2 * cg] for t in pq], axis=1)
    for bb in range(nb):
        rows = slice(bb * FM_A, (bb + 1) * FM_A)
        res = _dot(m_ref[...], jnp.concatenate([p[rows], q[rows]], axis=0).astype(BF16))
        o_ref[0, 0, :, bb, :] = res[0:FM_A]
        o_ref[0, 1, :, bb, :] = res[FM_A:2 * FM_A]


def _fm_front(x, g, shift, scale, w_cs, m1, nb=8):
    b, s, d = x.shape
    cg = w_cs.shape[0]
    per_b = pl.BlockSpec((1, 1, d), lambda bi, j: (bi, 0, 0))
    const = lambda a: pl.BlockSpec(a.shape, lambda bi, j: (0, 0))
    return pl.pallas_call(
        functools.partial(_fm_front_body, cg=cg, nb=nb),
        out_shape=jax.ShapeDtypeStruct((b, 2, FM_A, s // FM_A, d), F32),
        grid=(b, s // FM_A // nb),
        in_specs=[pl.BlockSpec((1, FM_A, nb, d), lambda bi, j: (bi, 0, j, 0)),
                  const(g), per_b, per_b, const(w_cs), const(m1)],
        out_specs=pl.BlockSpec((1, 2, FM_A, nb, d), lambda bi, j: (bi, 0, 0, j, 0)),
        compiler_params=_cparams(("parallel", "parallel")),
        name="fm_front",
    )(x.reshape(b, FM_A, s // FM_A, d), g, shift, scale, w_cs, m1)


def _fm_s2_body(s_ref, g_ref, o_ref, *, dblk, scale):
    for i in range(dblk):
        s2 = jnp.concatenate([s_ref[0, 0, i], s_ref[0, 1, i]], axis=0).astype(BF16)
        o_ref[:, i, :] = _dot(g_ref[i], s2) * scale


def _fm_stage2(sv, gcat, seq, d, dblk=8):
    b = sv.shape[0]
    scale = 1.0 / math.sqrt(seq * (d // F_GROUPS))
    out = pl.pallas_call(
        functools.partial(_fm_s2_body, dblk=dblk, scale=scale),
        out_shape=jax.ShapeDtypeStruct((b * FM_A, dblk * (FM_A // dblk), d), F32),
        grid=(b, FM_A // dblk),
        in_specs=[pl.BlockSpec((1, 2, dblk, FM_A, d), lambda bi, j: (bi, 0, j, 0, 0)),
                  pl.BlockSpec((dblk, FM_A, 2 * FM_A), lambda bi, j: (j, 0, 0))],
        out_specs=pl.BlockSpec((FM_A, dblk, d), lambda bi, j: (bi, j, 0)),
        compiler_params=_cparams(("parallel", "parallel")),
        name="fm_stage2",
    )(sv, gcat)
    return out.reshape(b, seq, d)


def _fourier_mix(x, g, shift, scale):
    b, s, d = x.shape
    assert s == FM_A * FM_A
    w_cs, m1, gcat = _fm_constants(d // F_GROUPS)
    return _fm_stage2(_fm_front(x, g, shift, scale, w_cs, m1), gcat, s, d)


MOE_TM = 1024
DMA_WINDOW = 128


def _router_body(x_ref, yf_ref, wf_ref, gf_ref, g_ref, sh_ref, sc_ref, wr_ref, br_ref,
                 xo_ref, h_ref, meta_ref, gw_ref, cnt_ref, carry):
    i = pl.program_id(0)

    @pl.when(i == 0)
    def _():
        carry[...] = jnp.zeros_like(carry)

    xm = x_ref[...] + gf_ref[0] * _dot(yf_ref[...].astype(BF16), wf_ref[...])
    xo_ref[...] = xm
    h = _norm_mod(xm, g_ref[...], sh_ref[0], sc_ref[0])
    _rows_to_tiles(h_ref, h)
    h_hi = h.astype(BF16)
    h_lo = (h - h_hi.astype(F32)).astype(BF16)
    by_hi = _dot(h_hi, wr_ref[...])
    logits = by_hi[:, 0:LANES] + by_hi[:, LANES:] + _dot(h_lo, wr_ref[:, 0:LANES]) + br_ref[...]
    lane = lax.broadcasted_iota(jnp.int32, logits.shape, 1)
    nl = logits.shape[-1]
    m1 = jnp.max(logits, axis=-1, keepdims=True)
    i1 = jnp.min(jnp.where(logits == m1, lane, nl), axis=-1, keepdims=True)
    rest = jnp.where(lane == i1, 3.0 * NEG_INF, logits)
    m2 = jnp.max(rest, axis=-1, keepdims=True)
    i2 = jnp.min(jnp.where(rest == m2, lane, nl), axis=-1, keepdims=True)
    e = jnp.exp(m2 - m1)
    gw_ref[...] = jnp.where(lane == 0, 1.0 / (1.0 + e), jnp.where(lane == 1, e / (1.0 + e), 0.0))
    onehot = jnp.where((lane == i1) | (lane == i2), 1.0, 0.0)
    tm = onehot.shape[0]
    earlier = lax.broadcasted_iota(jnp.int32, (tm, tm), 0) > lax.broadcasted_iota(jnp.int32, (tm, tm), 1)
    excl = _dot(jnp.where(earlier, 1.0, 0.0).astype(BF16), onehot.astype(BF16)) + carry[...]
    r1 = jnp.sum(jnp.where(lane == i1, excl, 0.0), axis=-1, keepdims=True).astype(jnp.int32)
    r2 = jnp.sum(jnp.where(lane == i2, excl, 0.0), axis=-1, keepdims=True).astype(jnp.int32)
    meta = jnp.where(lane == 0, i1, jnp.where(lane == 1, i2, jnp.where(lane == 2, r1, jnp.where(lane == 3, r2, 0))))
    meta_ref[...] = meta.T[0:ROW_SL]
    carry[...] = carry[...] + jnp.sum(onehot, axis=0, keepdims=True)
    cnt_ref[...] = carry[...]


def _router(x, y_f, w_f, gate_f, g, shift, scale, w_router, b_router, tm=512):
    b, s, d = x.shape
    t = b * s
    ne = w_router.shape[1]
    wr = jnp.pad(w_router.astype(F32), ((0, 0), (0, LANES - ne)))
    wr_hi = wr.astype(BF16)
    wr = jnp.concatenate([wr_hi, (wr - wr_hi.astype(F32)).astype(BF16)], axis=1)
    br = jnp.pad(b_router.astype(F32).reshape(1, ne), ((0, 0), (0, LANES - ne)), constant_values=NEG_INF)
    spt = s // tm
    per_b = pl.BlockSpec((1, 1, d), lambda i: (i // spt, 0, 0))
    const = lambda a: pl.BlockSpec(a.shape, lambda i: (0, 0))
    tok = pl.BlockSpec((tm, d), lambda i: (i, 0))
    return pl.pallas_call(
        _router_body,
        out_shape=[jax.ShapeDtypeStruct((t, d), F32),
                   jax.ShapeDtypeStruct((t * ROW_SL, LANES), F32),
                   jax.ShapeDtypeStruct((ROW_SL, t), jnp.int32),
                   jax.ShapeDtypeStruct((t, LANES), F32),
                   jax.ShapeDtypeStruct((1, LANES), F32)],
        grid=(t // tm,),
        in_specs=[tok, tok, const(w_f), per_b, const(g), per_b, per_b, const(wr), const(br)],
        out_specs=[tok,
                   pl.BlockSpec((tm * ROW_SL, LANES), lambda i: (i, 0)),
                   pl.BlockSpec((ROW_SL, tm), lambda i: (0, i)),
                   pl.BlockSpec((tm, LANES), lambda i: (i, 0)),
                   pl.BlockSpec((1, LANES), lambda i: (0, 0))],
        scratch_shapes=[pltpu.VMEM((1, LANES), F32)],
        compiler_params=_cparams(("arbitrary",)),
        name="router",
    )(x.reshape(t, d), y_f.reshape(t, d), w_f, gate_f, g, shift, scale, wr, br)


def _moe_plan(meta, counts, ne, tm):
    i1, i2, r1, r2 = meta[0], meta[1], meta[2], meta[3]
    cnt = counts[0, :ne].astype(jnp.int32)
    padded = ((cnt + tm - 1) // tm) * tm
    ends = jnp.cumsum(padded)
    offs = ends - padded
    pick = lambda idx: sum(jnp.where(idx == e, offs[e], 0) for e in range(ne))
    pos = jnp.concatenate([pick(i1) + r1, pick(i2) + r2]).astype(jnp.int32)
    n_tiles = (2 * i1.shape[0]) // tm + ne
    n_used = (ends[ne - 1] // tm).astype(jnp.int32)
    tile_start = jnp.minimum(jnp.arange(n_tiles, dtype=jnp.int32), n_used - 1) * tm
    tile_expert = jnp.sum(tile_start[:, None] >= ends[None, :], axis=1).astype(jnp.int32)
    group_end = sum(jnp.where(tile_expert == e, offs[e] + cnt[e], 0) for e in range(ne))
    tile_rows = jnp.clip(group_end - tile_start, 0, tm).astype(jnp.int32)
    return pos, offs + cnt, padded - cnt, tile_expert, n_used.reshape(1), tile_rows


def _windowed_copies(n, start_copy, wait_one):
    def body(i, carry):
        @pl.when(i >= DMA_WINDOW)
        def _():
            wait_one()
        start_copy(i)
        return carry

    lax.fori_loop(0, n, body, 0)

    def drain(i, carry):
        wait_one()
        return carry

    lax.fori_loop(0, jnp.minimum(n, DMA_WINDOW), drain, 0)


def _tile_of(ref, row):
    return ref.at[pl.ds(pl.multiple_of(row * ROW_SL, ROW_SL), ROW_SL)]


def _tiles_to_rows(ref, n, first=0):
    return jnp.concatenate([ref[pl.ds(first * ROW_SL + sl, n, stride=ROW_SL), :] for sl in range(ROW_SL)], axis=1)


def _rows_to_tiles(ref, val):
    n = val.shape[0]
    for sl in range(ROW_SL):
        ref[pl.ds(sl, n, stride=ROW_SL), :] = val[:, sl * LANES:(sl + 1) * LANES]


def _dispatch_body(pos_ref, pad_start_ref, pad_n_ref, h_ref, xs_hbm, sem, *, n_tok, ne):
    i = pl.program_id(0)
    td = h_ref.shape[0] // ROW_SL
    base = i * td
    copy = lambda src, dst: pltpu.make_async_copy(_tile_of(h_ref, src), _tile_of(xs_hbm, dst), sem)
    wait_one = lambda: copy(0, 0).wait()

    def start_token(r, carry):
        copy(r, pos_ref[base + r]).start(priority=0)
        copy(r, pos_ref[n_tok + base + r]).start(priority=1)
        return carry

    lax.fori_loop(0, td, start_token, 0, unroll=8)
    whole_tile = pltpu.make_async_copy(h_ref, xs_hbm.at[pl.ds(0, td * ROW_SL)], sem)
    whole_tile.wait()
    whole_tile.wait()

    @pl.when(i == 0)
    def _():
        for e in range(ne):
            first = pad_start_ref[e]
            _windowed_copies(pad_n_ref[e], lambda r: copy(0, first + r).start(), wait_one)


def _moe_dispatch(h3, pos, pad_start, pad_n, n_rows, td=1024):
    n_tok = h3.shape[0] // ROW_SL
    ne = pad_start.shape[0]
    return pl.pallas_call(
        functools.partial(_dispatch_body, n_tok=n_tok, ne=ne),
        out_shape=jax.ShapeDtypeStruct((n_rows * ROW_SL, LANES), h3.dtype),
        grid_spec=pltpu.PrefetchScalarGridSpec(
            num_scalar_prefetch=3, grid=(n_tok // td,),
            in_specs=[pl.BlockSpec((td * ROW_SL, LANES), lambda i, p, ps, pn: (i, 0))],
            out_specs=pl.BlockSpec(memory_space=pl.ANY),
            scratch_shapes=[pltpu.SemaphoreType.DMA(())]),
        compiler_params=_cparams(("arbitrary",)),
        name="moe_dispatch",
    )(pos, pad_start, pad_n, h3)


def _moe_grouped_body(te_ref, nu_ref, tr_ref, xs_ref, w1_ref, w3_ref, w2_ref, y_ref, xb_scr, acc_scr):
    i = pl.program_id(0)
    j = pl.program_id(1)
    tm = xb_scr.shape[0]
    hm = tm // 2

    def expert_rows(nrows):
        h = xb_scr[0:nrows]
        a = _dot(h, w1_ref[0].astype(BF16))
        u = (a * jax.nn.sigmoid(a) * _dot(h, w3_ref[0].astype(BF16))).astype(BF16)
        part = _dot(u, w2_ref[0].astype(BF16))

        @pl.when(j == 0)
        def _():
            acc_scr[0:nrows] = part

        @pl.when(j > 0)
        def _():
            acc_scr[0:nrows] += part

    @pl.when(i < nu_ref[0])
    def _():
        @pl.when(j == 0)
        def _():
            xb_scr[...] = _tiles_to_rows(xs_ref, tm).astype(BF16)

        @pl.when(tr_ref[i] > hm)
        def _():
            expert_rows(tm)

        @pl.when(tr_ref[i] <= hm)
        def _():
            expert_rows(hm)

            @pl.when(j == 0)
            def _():
                acc_scr[hm:tm] = jnp.zeros((tm - hm, acc_scr.shape[1]), F32)

        @pl.when(j == pl.num_programs(1) - 1)
        def _():
            _rows_to_tiles(y_ref, acc_scr[...])


def _moe_grouped(xs, tile_expert, n_used, tile_rows, w1, w3, w2, tm, fb=512):
    ne, d, f = w1.shape
    n_rows = xs.shape[0] // ROW_SL
    nj = f // fb
    row_tile = lambda i, j, te, nu, tr: (jnp.maximum(jnp.minimum(i, nu[0] - 1), 0), 0)
    jj = lambda i, j, nu: jnp.where(i < nu[0], j, nj - 1)
    return pl.pallas_call(
        _moe_grouped_body,
        out_shape=jax.ShapeDtypeStruct(xs.shape, F32),
        grid_spec=pltpu.PrefetchScalarGridSpec(
            num_scalar_prefetch=3, grid=(n_rows // tm, nj),
            in_specs=[pl.BlockSpec((tm * ROW_SL, LANES), row_tile),
                      pl.BlockSpec((1, d, fb), lambda i, j, te, nu, tr: (te[i], 0, jj(i, j, nu))),
                      pl.BlockSpec((1, d, fb), lambda i, j, te, nu, tr: (te[i], 0, jj(i, j, nu))),
                      pl.BlockSpec((1, fb, d), lambda i, j, te, nu, tr: (te[i], jj(i, j, nu), 0))],
            out_specs=pl.BlockSpec((tm * ROW_SL, LANES), row_tile),
            scratch_shapes=[pltpu.VMEM((tm, d), BF16), pltpu.VMEM((tm, d), F32)]),
        compiler_params=_cparams(("arbitrary", "arbitrary")),
        name="moe_grouped",
    )(tile_expert, n_used, tile_rows, xs, w1, w3, w2)


def _moe_final_body(pos_ref, x_ref, y_hbm, gw_ref, gt_ref, fg_ref, o_ref, yg_scr, sem, *, n_tok):
    i = pl.program_id(0)
    tc = x_ref.shape[0]
    slot = i % 2

    def gather_tile(step, into):
        base = step * tc

        def start_token(r, carry):
            dst = yg_scr.at[into]
            pltpu.make_async_copy(_tile_of(y_hbm, pos_ref[base + r]), _tile_of(dst, r),
                                  sem.at[into]).start(priority=0)
            pltpu.make_async_copy(_tile_of(y_hbm, pos_ref[n_tok + base + r]), _tile_of(dst, tc + r),
                                  sem.at[into]).start(priority=1)
            return carry

        lax.fori_loop(0, tc, start_token, 0, unroll=8)

    @pl.when(i == 0)
    def _():
        gather_tile(0, 0)

    @pl.when(i + 1 < pl.num_programs(0))
    def _():
        gather_tile(i + 1, 1 - slot)

    pltpu.make_async_copy(y_hbm.at[pl.ds(0, 2 * tc * ROW_SL)], yg_scr.at[slot], sem.at[slot]).wait()
    gw = gw_ref[...]
    rows = yg_scr.at[slot]
    y = gw[:, 0:1] * _tiles_to_rows(rows, tc) + gw[:, 1:2] * _tiles_to_rows(rows, tc, first=tc)
    xo = x_ref[...] + gt_ref[0] * y
    ms = jnp.mean(xo * xo, axis=-1, keepdims=True)
    o_ref[...] = xo * lax.rsqrt(ms + EPS) * fg_ref[...]


def _moe_final(x, y, pos, gw, gt, final_g, tc=512):
    b, s, d = x.shape
    t = b * s
    spt = s // tc
    out = pl.pallas_call(
        functools.partial(_moe_final_body, n_tok=t),
        out_shape=jax.ShapeDtypeStruct((t, d), F32),
        grid_spec=pltpu.PrefetchScalarGridSpec(
            num_scalar_prefetch=1, grid=(t // tc,),
            in_specs=[pl.BlockSpec((tc, d), lambda i, p: (i, 0)),
                      pl.BlockSpec(memory_space=pl.ANY),
                      pl.BlockSpec((tc, LANES), lambda i, p: (i, 0)),
                      pl.BlockSpec((1, 1, d), lambda i, p: (i // spt, 0, 0)),
                      pl.BlockSpec(final_g.shape, lambda i, p: (0, 0))],
            out_specs=pl.BlockSpec((tc, d), lambda i, p: (i, 0)),
            scratch_shapes=[pltpu.VMEM((2, 2 * tc * ROW_SL, LANES), F32), pltpu.SemaphoreType.DMA((2,))]),
        compiler_params=_cparams(("arbitrary",)),
        name="moe_final",
    )(pos, x.reshape(t, d), y, gw, gt, final_g)
    return out.reshape(b, s, d)


def _moe_routed(x, y_f, w_f, gate_f, g, shift, scale, gt, final_g, w_router, b_router, w1, w3, w2):
    ne = w1.shape[0]
    tm = MOE_TM
    x1, h3, meta, gw, counts = _router(x, y_f, w_f, gate_f, g, shift, scale, w_router, b_router)
    pos, pad_start, pad_n, tile_expert, n_used, tile_rows = _moe_plan(meta, counts, ne, tm)
    assert x.shape[-1] == ROW_SL * LANES
    n_rows = (2 * (h3.shape[0] // ROW_SL) // tm + ne) * tm
    xs = _moe_dispatch(h3, pos, pad_start, pad_n, n_rows)
    y = _moe_grouped(xs, tile_expert, n_used, tile_rows, w1, w3, w2, tm)
    return _moe_final(x1.reshape(x.shape), y, pos, gw, gt, final_g)


def kernel(x, c, ctx, c_ctx, w_ada, b_ada, norm_g, w_in, hy_short_w, hy_short_b, hy_f_w1, hy_f_b1, hy_f_w2, hy_f_b2, hy_f_w3, hy_f_freq, hy_skip, na_rpb, w_mix_out, ffn_w1, ffn_w3, ffn_w2, w_fourier, w_router, b_router, moe_w1, moe_w3, moe_w2, final_g):
    b, s, d = x.shape
    depth = w_ada.shape[0]
    assert depth == 2, "layer 0 mixes with Hyena/attention, layer 1 with Fourier/MoE"
    c_hy = hy_skip.shape[-1]
    c_na = d - c_hy

    cvec = jnp.concatenate([c, c_ctx[None, :], jnp.zeros((8 - b - 1, d), F32)], axis=0)
    mods = _ada(cvec, w_ada, b_ada)

    def mod(layer, idx, ctx_row=False):
        m = mods[layer, :, idx * d:(idx + 1) * d]
        return m[b:b + 1, None, :] if ctx_row else m[0:b, None, :]

    row = lambda a: a.reshape(1, -1)

    w_in0 = w_in[0].astype(BF16)
    w_hy, w_qkv = w_in0[:, 0:3 * c_hy], w_in0[:, 3 * c_hy:]
    v, x1, x2, q, k, va = _inproj(x, row(norm_g[0, 0]), mod(0, 0), mod(0, 1), w_hy, w_qkv,
                                  hy_short_w[0], row(hy_short_b[0]))
    kc, vc = _ctxkv(ctx, row(norm_g[0, 0]), mod(0, 0, True), mod(0, 1, True), w_qkv[:, c_na:])
    y_na = _natt(q, k, va, kc, vc, _na_bias_table(na_rpb[0]))
    y_hy = _hyena(v, x1, x2, hy_f_w1[0], hy_f_b1[0], hy_f_w2[0], hy_f_b2[0], hy_f_w3[0],
                  hy_f_freq[0], hy_skip[0])
    x = _mix_ffn(x, y_hy, y_na, w_mix_out[0].astype(BF16), mod(0, 2),
                 row(norm_g[0, 1]), mod(0, 3), mod(0, 4), mod(0, 5),
                 ffn_w1[0].astype(BF16), ffn_w3[0].astype(BF16), ffn_w2[0].astype(BF16))

    y_f = _fourier_mix(x, row(norm_g[1, 0]), mod(1, 0), mod(1, 1))
    return _moe_routed(x, y_f, w_fourier[0].astype(BF16), mod(1, 2),
                       row(norm_g[1, 1]), mod(1, 3), mod(1, 4), mod(1, 5), row(final_g),
                       w_router[0], b_router[0],
                       moe_w1[0], moe_w3[0], moe_w2[0])
```

```python
import functools
import math

import numpy as np
import jax
import jax.numpy as jnp
from jax import lax
from jax.experimental import pallas as pl
from jax.experimental.pallas import tpu as pltpu

F32 = jnp.float32
BF16 = jnp.bfloat16
HIGHEST = lax.Precision.HIGHEST

GRID_W = 64
NA_HEAD_DIM = 32
NA_WIN_R = 8
NA_WIN_C = 16
HYENA_EMB = 33
HYENA_BANDS = (HYENA_EMB - 1) // 2
HYENA_FAST_DECAY = 0.3
HYENA_SLOW_DECAY = 1.5
HYENA_TARGET = 1e-2
F_GROUPS = 4
N_MOD = 6
EPS = 1e-6
NEG_INF = -1e30

FFT_A = 64
FFT_R = 128
FFT_KA = FFT_A // 2 + 1
FFT_KA_PAD = 40
FM_A = 64

LANES = 128
ROW_SL = 8
VMEM_LIMIT = 48 * 1024 * 1024
LARGE_VMEM_LIMIT = 56 * 1024 * 1024


def _cparams(sem, vmem_limit=VMEM_LIMIT):
    return pltpu.CompilerParams(dimension_semantics=sem, vmem_limit_bytes=vmem_limit)


def _dot(a, b):
    return jnp.dot(a, b, preferred_element_type=F32)


def _mxu_const(m):
    return jnp.asarray(m, dtype=F32).astype(BF16)


def _norm_mod(x, g, shift, scale):
    ms = jnp.mean(x * x, axis=-1, keepdims=True)
    y = x * lax.rsqrt(ms + EPS) * g
    return y * (1.0 + scale) + shift


def _ada_body(c_ref, w_ref, b_ref, o_ref):
    cv = c_ref[...]
    s = cv * jax.nn.sigmoid(cv)
    o_ref[0] = jnp.dot(s, w_ref[0], precision=HIGHEST, preferred_element_type=F32) + b_ref[0]


def _ada(cvec, w_ada, b_ada):
    depth, d, n = w_ada.shape
    rows = cvec.shape[0]
    bn = n // 4
    return pl.pallas_call(
        _ada_body,
        out_shape=jax.ShapeDtypeStruct((depth, rows, n), F32),
        grid=(depth, n // bn),
        in_specs=[pl.BlockSpec((rows, d), lambda l, j: (0, 0)),
                  pl.BlockSpec((1, d, bn), lambda l, j: (l, 0, j)),
                  pl.BlockSpec((1, 1, bn), lambda l, j: (l, 0, j))],
        out_specs=pl.BlockSpec((1, rows, bn), lambda l, j: (l, 0, j)),
        compiler_params=_cparams(("parallel", "parallel")),
        name="ada",
    )(cvec, w_ada, b_ada.reshape(depth, 1, n))


def _inproj_body(x_ref, xp_ref, xn_ref, g_ref, sh_ref, sc_ref, why_ref, wqkv_ref, sw_ref, sb_ref,
                 v_ref, x1_ref, x2_ref, q_ref, k_ref, va_ref, *, n_tiles, q_scale, c_hy, c_na):
    i = pl.program_id(1)
    g, sh, sc = g_ref[...], sh_ref[0], sc_ref[0]
    hf = _norm_mod(x_ref[0], g, sh, sc)
    h = hf.astype(BF16)
    tm = hf.shape[0]
    hx = jnp.concatenate([_norm_mod(xp_ref[0], g, sh, sc), hf, _norm_mod(xn_ref[0], g, sh, sc)],
                         axis=0).astype(BF16)
    row = lax.broadcasted_iota(jnp.int32, (tm, c_hy), 0)
    sw = sw_ref[...]
    sb = sb_ref[...]
    lo, hi = ROW_SL, ROW_SL + tm
    for ci, out_ref in enumerate((v_ref, x1_ref, x2_ref)):
        cols = slice(ci * c_hy, (ci + 1) * c_hy)
        zx = _dot(hx, why_ref[:, cols])
        zh = zx[lo:hi]
        zp = jnp.where(i > 0, zx[lo - 1:lo], 0.0)
        zn = jnp.where(i < n_tiles - 1, zx[hi:hi + 1], 0.0)
        z_m1 = jnp.where(row == 0, zp, pltpu.roll(zh, 1, 0))
        z_p1 = jnp.where(row == tm - 1, zn, pltpu.roll(zh, tm - 1, 0))
        out_ref[0] = z_m1 * sw[0:1, cols] + zh * sw[1:2, cols] + z_p1 * sw[2:3, cols] + sb[:, cols]
    for ci, (out_ref, mult) in enumerate(((q_ref, q_scale), (k_ref, None), (va_ref, None))):
        z = _dot(h, wqkv_ref[:, ci * c_na:(ci + 1) * c_na])
        out_ref[0] = (z if mult is None else z * mult).astype(BF16)


def _inproj(x, g, shift, scale, w_hy, w_qkv, short_w, short_b, tm=1024):
    b, s, d = x.shape
    c_hy = w_hy.shape[1] // 3
    c_na = w_qkv.shape[1] // 3
    n_tiles = s // tm
    halo_per_tile = tm // ROW_SL
    last_halo = s // ROW_SL - 1
    body = functools.partial(_inproj_body, n_tiles=n_tiles, q_scale=NA_HEAD_DIM ** -0.5,
                             c_hy=c_hy, c_na=c_na)
    tok = lambda c: pl.BlockSpec((1, tm, c), lambda bi, i: (bi, i, 0))
    full2 = lambda a: pl.BlockSpec(a.shape, lambda bi, i: (0, 0))
    resident = lambda a: pl.BlockSpec(a.shape, lambda bi, i: (0, 0), pipeline_mode=pl.Buffered(1))
    per_b = pl.BlockSpec((1, 1, d), lambda bi, i: (bi, 0, 0))
    return pl.pallas_call(
        body,
        out_shape=[jax.ShapeDtypeStruct((b, s, c_hy), F32)] * 3 + [jax.ShapeDtypeStruct((b, s, c_na), BF16)] * 3,
        grid=(b, n_tiles),
        in_specs=[tok(d),
                  pl.BlockSpec((1, ROW_SL, d), lambda bi, i: (bi, jnp.maximum(i * halo_per_tile - 1, 0), 0)),
                  pl.BlockSpec((1, ROW_SL, d),
                               lambda bi, i: (bi, jnp.minimum((i + 1) * halo_per_tile, last_halo), 0)),
                  full2(g), per_b, per_b, resident(w_hy), resident(w_qkv), full2(short_w), full2(short_b)],
        out_specs=[tok(c_hy)] * 3 + [tok(c_na)] * 3,
        compiler_params=_cparams(("parallel", "parallel"), vmem_limit=LARGE_VMEM_LIMIT),
        name="inproj",
    )(x, x, x, g, shift, scale, w_hy, w_qkv, short_w, short_b)


def _ctxkv_body(x_ref, g_ref, sh_ref, sc_ref, w_ref, k_ref, v_ref, *, c_na):
    h = _norm_mod(x_ref[0], g_ref[...], sh_ref[0], sc_ref[0]).astype(BF16)
    z = _dot(h, w_ref[...])
    k_ref[0] = z[:, 0:c_na].astype(BF16)
    v_ref[0] = z[:, c_na:2 * c_na].astype(BF16)


def _ctxkv(ctx, g, shift, scale, w_kv):
    b, n, d = ctx.shape
    c_na = w_kv.shape[1] // 2
    one = pl.BlockSpec((1, 1, d), lambda bi: (0, 0, 0))
    return pl.pallas_call(
        functools.partial(_ctxkv_body, c_na=c_na),
        out_shape=[jax.ShapeDtypeStruct((b, n, c_na), BF16)] * 2,
        grid=(b,),
        in_specs=[pl.BlockSpec((1, n, d), lambda bi: (bi, 0, 0)),
                  pl.BlockSpec(g.shape, lambda bi: (0, 0)), one, one,
                  pl.BlockSpec(w_kv.shape, lambda bi: (0, 0))],
        out_specs=[pl.BlockSpec((1, n, c_na), lambda bi: (bi, 0, 0))] * 2,
        compiler_params=_cparams(("parallel",)),
        name="ctxkv",
    )(ctx, g, shift, scale, w_kv)


NA_HEADS_PER_BLK = 8
NA_ROWS_PER_STEP = 8


def _na_bias_body(r_ref, ok_ref, o_ref):
    w = GRID_W
    lane = lax.broadcasted_iota(jnp.int32, (w, LANES), 1)
    ok = ok_ref[...] > 0.5

    def toeplitz(ri):
        row = jnp.broadcast_to(r_ref[0, ri:ri + 1, :], (w, LANES))
        return pltpu.roll(row, LANES - (NA_WIN_C - 1), 1, stride=1, stride_axis=0)

    blocks = [toeplitz(ri) for ri in range(2 * NA_WIN_R - 1)]
    for off in range(NA_WIN_R):
        first = NA_WIN_R - 1 - off
        pairs = []
        for p in range(NA_WIN_R // 2):
            a, b = blocks[first + 2 * p], blocks[first + 2 * p + 1]
            pairs.append(jnp.where(ok, jnp.where(lane < w, a, pltpu.roll(b, w, 1)), NEG_INF))
        o_ref[0, off] = jnp.concatenate(pairs, axis=1)


def _na_bias_table(rpb):
    w = GRID_W
    h, nr, nc = rpb.shape
    assert 2 * w == LANES and nr == 2 * NA_WIN_R - 1 and nc == 2 * NA_WIN_C - 1
    hpb = NA_HEADS_PER_BLK
    col = np.arange(w)[:, None]
    kc = np.arange(LANES)[None, :] % w
    c_start = np.clip(col - NA_WIN_C // 2, 0, w - NA_WIN_C)
    ok = jnp.asarray((kc >= c_start) & (kc < c_start + NA_WIN_C), dtype=F32)
    rp = jnp.pad(rpb.astype(F32), ((0, 0), (0, 2 * ROW_SL - nr), (0, LANES - nc)))
    return pl.pallas_call(
        _na_bias_body,
        out_shape=jax.ShapeDtypeStruct((h // hpb, NA_WIN_R, hpb * w, NA_WIN_R * w), F32),
        grid=(h,),
        in_specs=[pl.BlockSpec((1,) + rp.shape[1:], lambda i: (i, 0, 0)),
                  pl.BlockSpec(ok.shape, lambda i: (0, 0))],
        out_specs=pl.BlockSpec((1, NA_WIN_R, w, NA_WIN_R * w), lambda i: (i // hpb, 0, i % hpb, 0)),
        compiler_params=_cparams(("parallel",)),
        name="na_bias",
    )(rp, ok)


def _natt_body(q_ref, k_ref, v_ref, kc_ref, vc_ref, bias_ref, o_ref, *, rows):
    w = GRID_W
    hpb = NA_HEADS_PER_BLK
    nloc = NA_WIN_R * w
    lane = lax.broadcasted_iota(jnp.int32, (1, hpb * NA_HEAD_DIM), 1)
    in_head = [(lane >= NA_HEAD_DIM * hh) & (lane < NA_HEAD_DIM * (hh + 1)) for hh in range(hpb)]
    kcx = kc_ref[0]
    vcx = vc_ref[0]
    nt = (((1,), (1,)), ((), ()))

    def one_row(r):
        r0 = jnp.clip(r - NA_WIN_R // 2, 0, rows - NA_WIN_R)
        off = r - r0
        qs = q_ref[0, pl.ds(pl.multiple_of(r * w, w), w), :]
        kw = k_ref[0, pl.ds(pl.multiple_of(r0 * w, w), nloc), :]
        vw = v_ref[0, pl.ds(pl.multiple_of(r0 * w, w), nloc), :]
        zero = jnp.zeros_like(qs)
        qst = jnp.concatenate([jnp.where(m, qs, zero) for m in in_head], axis=0)
        s_loc = lax.dot_general(qst, kw, nt, preferred_element_type=F32) + bias_ref[0, off]
        s_ctx = lax.dot_general(qst, kcx, nt, preferred_element_type=F32)
        m = jnp.maximum(jnp.max(s_loc, axis=-1, keepdims=True), jnp.max(s_ctx, axis=-1, keepdims=True))
        p_loc = jnp.exp(s_loc - m)
        p_ctx = jnp.exp(s_ctx - m)
        den = jnp.sum(p_loc, axis=-1, keepdims=True) + jnp.sum(p_ctx, axis=-1, keepdims=True)
        o = (_dot(p_loc.astype(BF16), vw) + _dot(p_ctx.astype(BF16), vcx)) * (1.0 / den)
        acc = jnp.where(in_head[0], o[0:w], 0.0)
        for hh in range(1, hpb):
            acc = acc + jnp.where(in_head[hh], o[hh * w:(hh + 1) * w], 0.0)
        o_ref[0, pl.ds(pl.multiple_of(r * w, w), w), :] = acc.astype(BF16)

    def row_group(i, carry):
        for r in range(NA_ROWS_PER_STEP):
            one_row(NA_ROWS_PER_STEP * i + r)
        return carry

    lax.fori_loop(0, rows // NA_ROWS_PER_STEP, row_group, 0)


def _natt(q, k, v, kc, vc, bias):
    b, s, c = q.shape
    nctx = kc.shape[1]
    lw = NA_HEADS_PER_BLK * NA_HEAD_DIM
    rows = s // GRID_W
    seq = pl.BlockSpec((1, s, lw), lambda bi, g: (bi, 0, g))
    cx = pl.BlockSpec((1, nctx, lw), lambda bi, g: (bi, 0, g))
    return pl.pallas_call(
        functools.partial(_natt_body, rows=rows),
        out_shape=jax.ShapeDtypeStruct((b, s, c), BF16),
        grid=(b, c // lw),
        in_specs=[seq, seq, seq, cx, cx,
                  pl.BlockSpec((1,) + bias.shape[1:], lambda bi, g: (g, 0, 0, 0))],
        out_specs=seq,
        compiler_params=_cparams(("parallel", "parallel")),
        name="natt",
    )(q, k, v, kc, vc, bias)


def _hyena_feats(seq_len):
    t = jnp.linspace(0.0, 1.0, seq_len, dtype=F32)[:, None]
    bands = jnp.linspace(1e-4, HYENA_BANDS - 1, HYENA_BANDS, dtype=F32)
    ang = (2.0 * math.pi / seq_len) * jnp.arange(seq_len, dtype=F32)[:, None] * bands[None, :]
    feats = jnp.concatenate([t, jnp.cos(ang), -jnp.sin(ang)], axis=-1)
    return jnp.pad(feats, ((0, 0), (0, LANES - HYENA_EMB)))


def _filt_body(feat_ref, w1_ref, b1_ref, w2_ref, b2_ref, w3_ref, fr_ref, dl_ref, o_ref, l1_ref, h_scr,
               *, halves):
    j = pl.program_id(0)
    hp = functools.partial(jnp.dot, precision=HIGHEST, preferred_element_type=F32)
    feats = feat_ref[...]

    @pl.when(j == 0)
    def _():
        fr = fr_ref[...]
        h = jnp.sin(fr[0:1] * (hp(feats, w1_ref[...]) + b1_ref[...]))
        h_scr[...] = jnp.sin(fr[1:2] * (hp(h, w2_ref[...]) + b2_ref[...]))

    h2 = h_scr[...]
    h_hi = h2.astype(BF16)
    h_lo = (h2 - h_hi.astype(F32)).astype(BF16)
    w3 = w3_ref[...]
    w_hi = w3.astype(BF16)
    w_lo = (w3 - w_hi.astype(F32)).astype(BF16)
    hc = (_dot(jnp.concatenate([h_hi, h_lo], axis=1), jnp.concatenate([w_hi, w_hi], axis=0))
          + _dot(h_hi, w_lo))
    t = feats[:, 0:1]
    hc = hc * jnp.exp(-t * dl_ref[...])
    row = lax.broadcasted_iota(jnp.int32, hc.shape, 0)
    hc = jnp.where((row == 0) & ((j // halves) % 2 == 1), 0.0, hc)
    l1_ref[0] = jnp.sum(jnp.abs(hc), axis=0, keepdims=True)
    o_ref[0] = hc


def _hyena_filter_taps(seq_len, f_w1, f_b1, f_w2, f_b2, f_w3, f_freq, c_hy):
    feats = _hyena_feats(seq_len)
    hid = f_w1.shape[1]
    w1 = jnp.pad(f_w1.astype(F32), ((0, LANES - HYENA_EMB), (0, 0)))
    deltas = jnp.abs(jnp.linspace(math.log(HYENA_TARGET) / HYENA_SLOW_DECAY,
                                  math.log(HYENA_TARGET) / HYENA_FAST_DECAY, c_hy, dtype=F32))[None, :]
    nblk = f_w3.shape[1] // c_hy
    halves = 2
    cb = c_hy // halves
    c0 = lambda a: pl.BlockSpec(a.shape, lambda j: (0, 0))
    b1, b2 = f_b1.reshape(1, hid), f_b2.reshape(1, hid)
    return pl.pallas_call(
        functools.partial(_filt_body, halves=halves),
        out_shape=[jax.ShapeDtypeStruct((nblk, seq_len, c_hy), F32),
                   jax.ShapeDtypeStruct((nblk, 1, c_hy), F32)],
        grid=(nblk * halves,),
        in_specs=[c0(feats), c0(w1), c0(b1), c0(f_w2), c0(b2),
                  pl.BlockSpec((hid, cb), lambda j: (0, j)), c0(f_freq),
                  pl.BlockSpec((1, cb), lambda j: (0, j % halves))],
        out_specs=[pl.BlockSpec((1, seq_len, cb), lambda j: (j // halves, 0, j % halves)),
                   pl.BlockSpec((1, 1, cb), lambda j: (j // halves, 0, j % halves))],
        scratch_shapes=[pltpu.VMEM((seq_len, hid), F32)],
        compiler_params=_cparams(("arbitrary",)),
        name="hyena_filter",
    )(feats, w1, b1, f_w2, b2, f_w3, f_freq, deltas)


def _conv_dft_constants():
    a_half = FFT_A // 2
    n = FFT_A * FFT_R
    ka = np.arange(FFT_KA)[:, None]
    a = np.arange(a_half)[None, :]
    ph = 2.0 * np.pi * ka * a / FFT_A
    m_fwd = np.zeros((2 * FFT_KA_PAD, a_half))
    m_fwd[:FFT_KA] = np.cos(ph)
    m_fwd[FFT_KA_PAD:FFT_KA_PAD + FFT_KA] = -np.sin(ph)
    wgt = np.where((ka == 0) | (ka == FFT_A // 2), 1.0, 2.0)
    m_inv = np.zeros((a_half, 2 * FFT_KA_PAD))
    m_inv[:, :FFT_KA] = (wgt * np.cos(ph)).T / n
    m_inv[:, FFT_KA_PAD:FFT_KA_PAD + FFT_KA] = (-wgt * np.sin(ph)).T / n
    kb = np.arange(FFT_R)[None, :, None]
    b = np.arange(FFT_R)[None, None, :]
    kaa = np.arange(FFT_KA)[:, None, None]
    th = 2.0 * np.pi * (b * kb / FFT_R + b * kaa / n)
    gr, gi = np.cos(th), -np.sin(th)
    g2 = np.zeros((FFT_KA_PAD, 2 * FFT_R, 2 * FFT_R))
    g2[:FFT_KA] = np.block([[gr, -gi], [gi, gr]])
    grt, git = gr.transpose(0, 2, 1), gi.transpose(0, 2, 1)
    g2h = np.zeros_like(g2)
    g2h[:FFT_KA] = np.block([[grt, git], [-git, grt]])
    return _mxu_const(m_fwd), _mxu_const(m_inv), _mxu_const(g2), _mxu_const(g2h)


FFT_NB = 32


def _fwd1_body(m_ref, u_ref, o_ref):
    u = jnp.concatenate([u_ref[0, :, bb, :] for bb in range(FFT_NB)], axis=1).astype(BF16)
    res = _dot(m_ref[...], u)
    o_ref[0, 0] = res[0:FFT_KA_PAD]
    o_ref[0, 1] = res[FFT_KA_PAD:2 * FFT_KA_PAD]


def _conv_fwd1(u, m_fwd):
    n, seq, c = u.shape
    a_half = FFT_A // 2
    return pl.pallas_call(
        _fwd1_body,
        out_shape=jax.ShapeDtypeStruct((n, 2, FFT_KA_PAD, FFT_R * c), F32),
        grid=(n, FFT_R // FFT_NB),
        in_specs=[pl.BlockSpec(m_fwd.shape, lambda i, j: (0, 0)),
                  pl.BlockSpec((1, a_half, FFT_NB, c), lambda i, j: (i, 0, j, 0))],
        out_specs=pl.BlockSpec((1, 2, FFT_KA_PAD, FFT_NB * c), lambda i, j: (i, 0, 0, j)),
        compiler_params=_cparams(("parallel", "parallel")),
        name="conv_fwd1",
    )(m_fwd, u.reshape(n, a_half, FFT_R, c))


FFT_KB = 8


def _rows_to_slabs(src_ref, dst_scr, c):
    for part in range(2):
        for b in range(FFT_R):
            dst_scr[part, :, b, :] = src_ref[0, part, :, b * c:(b + 1) * c]


def _slabs_to_rows(src_scr, dst_ref, c):
    for part in range(2):
        for b in range(FFT_R):
            dst_ref[0, part, :, b * c:(b + 1) * c] = src_scr[part, :, b, :]


def _slab(scr, i):
    return jnp.concatenate([scr[0, i], scr[1, i]], axis=0).astype(BF16)


def _per_ka_block(j, work, clear):
    full_blocks = FFT_KA // FFT_KB
    tail = FFT_KA - full_blocks * FFT_KB

    @pl.when(j < full_blocks)
    def _():
        for i in range(FFT_KB):
            work(i)

    @pl.when(j >= full_blocks)
    def _():
        for i in range(FFT_KB):
            (work if i < tail else clear)(i)


def _fwd2f_body(sf_ref, sb_ref, g_ref, l1_ref, kf_ref, f3, b3):
    o = pl.program_id(0)
    j = pl.program_id(1)
    r2 = 2 * FFT_R
    c = kf_ref.shape[-1]
    _rows_to_slabs(sf_ref, f3, c)
    _rows_to_slabs(sb_ref, b3, c)
    inv = 1.0 / (l1_ref[2 * o] + l1_ref[2 * o + 1] + EPS)

    def spectrum(i):
        xf = _dot(g_ref[i], _slab(f3, i))
        xb = _dot(g_ref[i], _slab(b3, i))
        kf_ref[0, i, 0:FFT_R] = (xf[0:FFT_R] + xb[0:FFT_R]) * inv
        kf_ref[0, i, FFT_R:r2] = (xf[FFT_R:r2] - xb[FFT_R:r2]) * inv

    def clear(i):
        kf_ref[0, i] = jnp.zeros((r2, c), F32)

    _per_ka_block(j, spectrum, clear)


def _filter_spectrum(s_filt, l1, g2, c):
    n_ord = s_filt.shape[0] // 2
    cols = s_filt.shape[-1]
    r2 = 2 * FFT_R
    return pl.pallas_call(
        _fwd2f_body,
        out_shape=jax.ShapeDtypeStruct((n_ord, FFT_KA_PAD, r2, c), F32),
        grid=(n_ord, FFT_KA_PAD // FFT_KB),
        in_specs=[pl.BlockSpec((1, 2, FFT_KB, cols), lambda o, j: (2 * o, 0, j, 0)),
                  pl.BlockSpec((1, 2, FFT_KB, cols), lambda o, j: (2 * o + 1, 0, j, 0)),
                  pl.BlockSpec((FFT_KB, r2, r2), lambda o, j: (j, 0, 0)),
                  pl.BlockSpec(l1.shape, lambda o, j: (0, 0, 0))],
        out_specs=pl.BlockSpec((1, FFT_KB, r2, c), lambda o, j: (o, j, 0, 0)),
        scratch_shapes=[pltpu.VMEM((2, FFT_KB, FFT_R, c), F32)] * 2,
        compiler_params=_cparams(("parallel", "parallel")),
        name="filter_spectrum",
    )(s_filt, s_filt, g2, l1)


def _mid_body(s_ref, g_ref, gh_ref, kf_ref, t_ref, s3, t3):
    j = pl.program_id(1)
    r2 = 2 * FFT_R
    c = kf_ref.shape[-1]
    _rows_to_slabs(s_ref, s3, c)

    def convolve(i):
        x = _dot(g_ref[i], _slab(s3, i))
        xr, xi = x[0:FFT_R], x[FFT_R:r2]
        kr, ki = kf_ref[0, i, 0:FFT_R], kf_ref[0, i, FFT_R:r2]
        y = jnp.concatenate([xr * kr - xi * ki, xr * ki + xi * kr], axis=0).astype(BF16)
        t = _dot(gh_ref[i], y)
        t3[0, i] = t[0:FFT_R]
        t3[1, i] = t[FFT_R:r2]

    def clear(i):
        t3[0, i] = jnp.zeros((FFT_R, c), F32)
        t3[1, i] = jnp.zeros((FFT_R, c), F32)

    _per_ka_block(j, convolve, clear)
    _slabs_to_rows(t3, t_ref, c)


def _conv_mid(s, kf, order, g2, g2h, c):
    n, _, _, cols = s.shape
    r2 = 2 * FFT_R
    blk = pl.BlockSpec((1, 2, FFT_KB, cols), lambda i, j: (i, 0, j, 0))
    gspec = pl.BlockSpec((FFT_KB, r2, r2), lambda i, j: (j, 0, 0))
    return pl.pallas_call(
        _mid_body,
        out_shape=jax.ShapeDtypeStruct(s.shape, F32),
        grid=(n, FFT_KA_PAD // FFT_KB),
        in_specs=[blk, gspec, gspec,
                  pl.BlockSpec((1, FFT_KB, r2, c), lambda i, j: (order, j, 0, 0))],
        out_specs=blk,
        scratch_shapes=[pltpu.VMEM((2, FFT_KB, FFT_R, c), F32)] * 2,
        compiler_params=_cparams(("parallel", "parallel")),
        name="conv_mid",
    )(s, g2, g2h, kf)


def _inv1_body(m_ref, mf_ref, t_ref, u_ref, xg_ref, sk_ref, o_ref, *s_ref):
    c = u_ref.shape[-1]
    t2 = t_ref[0].reshape(2 * FFT_KA_PAD, FFT_NB * c).astype(BF16)
    y = _dot(m_ref[...], t2)
    gated = []
    for bb in range(FFT_NB):
        conv = y[:, bb * c:(bb + 1) * c] + u_ref[0, :, bb, :] * sk_ref[...]
        gated.append(xg_ref[0, :, bb, :] * conv)
        o_ref[0, :, bb, :] = gated[-1]
    if s_ref:
        res = _dot(mf_ref[...], jnp.concatenate(gated, axis=1).astype(BF16))
        s_ref[0][0, 0] = res[0:FFT_KA_PAD]
        s_ref[0][0, 1] = res[FFT_KA_PAD:2 * FFT_KA_PAD]


def _conv_inv1(t, u, xg, skip, m_inv, m_fwd, with_spectrum):
    n, seq, c = u.shape
    a_half = FFT_A // 2
    sk = skip.astype(F32).reshape(1, c)
    uspec = pl.BlockSpec((1, a_half, FFT_NB, c), lambda i, j: (i, 0, j, 0))
    sspec = pl.BlockSpec((1, 2, FFT_KA_PAD, FFT_NB * c), lambda i, j: (i, 0, 0, j))
    view = lambda a: a.reshape(n, a_half, FFT_R, c)
    y_shape = jax.ShapeDtypeStruct((n, a_half, FFT_R, c), F32)
    s_shape = jax.ShapeDtypeStruct(t.shape, F32)
    outs = pl.pallas_call(
        _inv1_body,
        out_shape=[y_shape, s_shape] if with_spectrum else [y_shape],
        grid=(n, FFT_R // FFT_NB),
        in_specs=[pl.BlockSpec(m_inv.shape, lambda i, j: (0, 0)),
                  pl.BlockSpec(m_fwd.shape, lambda i, j: (0, 0)),
                  sspec, uspec, uspec,
                  pl.BlockSpec((1, c), lambda i, j: (0, 0))],
        out_specs=[uspec, sspec] if with_spectrum else [uspec],
        compiler_params=_cparams(("parallel", "parallel")),
        name="conv_inv1",
    )(m_inv, m_fwd, t, view(u), view(xg), sk)
    y = outs[0].reshape(n, seq, c)
    return (y, outs[1]) if with_spectrum else (y, None)


def _hyena(v, x1, x2, f_w1, f_b1, f_w2, f_b2, f_w3, f_freq, skip):
    _, seq, c = v.shape
    assert 2 * seq == FFT_A * FFT_R
    m_fwd, m_inv, g2, g2h = _conv_dft_constants()
    taps, l1 = _hyena_filter_taps(seq, f_w1, f_b1, f_w2, f_b2, f_w3, f_freq, c)
    kf = _filter_spectrum(_conv_fwd1(taps, m_fwd), l1, g2, c)
    y, s = v, _conv_fwd1(v, m_fwd)
    gates = (x1, x2)
    for order, xg in enumerate(gates):
        t = _conv_mid(s, kf, order, g2, g2h, c)
        y, s = _conv_inv1(t, y, xg, skip[order], m_inv, m_fwd, with_spectrum=order + 1 < len(gates))
    return y


def _mix_ffn_body(x_ref, a1_ref, a2_ref, wm_ref, gm_ref, g_ref, sh_ref, sc_ref, gt_ref,
                  w1_ref, w3_ref, w2_ref, o_ref, *, fb):
    c1 = a1_ref.shape[-1]
    mixed = _dot(a1_ref[0].astype(BF16), wm_ref[0:c1]) + _dot(a2_ref[0].astype(BF16), wm_ref[c1:])
    xm = x_ref[0] + gm_ref[0] * mixed
    h = _norm_mod(xm, g_ref[...], sh_ref[0], sc_ref[0]).astype(BF16)
    acc = None
    for lo in range(0, w1_ref.shape[1], fb):
        a = _dot(h, w1_ref[:, lo:lo + fb])
        u = (a * jax.nn.sigmoid(a) * _dot(h, w3_ref[:, lo:lo + fb])).astype(BF16)
        part = _dot(u, w2_ref[lo:lo + fb, :])
        acc = part if acc is None else acc + part
    o_ref[0] = xm + gt_ref[0] * acc


def _mix_ffn(x, a1, a2, w_mix, gate_mix, g, shift, scale, gate, w1, w3, w2, tm=512, fb=2816):
    b, s, d = x.shape
    tok = lambda c: pl.BlockSpec((1, tm, c), lambda bi, i: (bi, i, 0))
    per_b = pl.BlockSpec((1, 1, d), lambda bi, i: (bi, 0, 0))
    const = lambda a: pl.BlockSpec(a.shape, lambda bi, i: (0, 0))
    resident = lambda a: pl.BlockSpec(a.shape, lambda bi, i: (0, 0), pipeline_mode=pl.Buffered(1))
    return pl.pallas_call(
        functools.partial(_mix_ffn_body, fb=fb),
        out_shape=jax.ShapeDtypeStruct(x.shape, F32),
        grid=(b, s // tm),
        in_specs=[tok(d), tok(a1.shape[-1]), tok(a2.shape[-1]), resident(w_mix), per_b,
                  const(g), per_b, per_b, per_b, resident(w1), resident(w3), resident(w2)],
        out_specs=tok(d),
        compiler_params=_cparams(("parallel", "parallel"), vmem_limit=LARGE_VMEM_LIMIT),
        name="mix_ffn",
    )(x, a1, a2, w_mix, gate_mix, g, shift, scale, gate, w1, w3, w2)


def _fm_constants(cg):
    j = np.arange(cg)[:, None]
    m = np.arange(cg)[None, :]
    ph = 2.0 * np.pi * j * m / cg
    w_cs = np.concatenate([np.cos(ph), np.sin(ph)], axis=1)
    d = np.arange(FM_A)[:, None]
    a = np.arange(FM_A)[None, :]
    ph = 2.0 * np.pi * d * a / FM_A
    fr, fi = np.cos(ph), -np.sin(ph)
    m1 = np.block([[fr, fi], [fi, -fr]])
    n = FM_A * FM_A
    dd = np.arange(FM_A)[:, None, None]
    c = np.arange(FM_A)[None, :, None]
    b = np.arange(FM_A)[None, None, :]
    th = 2.0 * np.pi * (b * c / FM_A + b * dd / n)
    gcat = np.concatenate([np.cos(th), np.sin(th)], axis=2)
    return _mxu_const(w_cs), _mxu_const(m1), _mxu_const(gcat)


def _fm_front_body(x_ref, g_ref, sh_ref, sc_ref, w_ref, m_ref, o_ref, *, cg, nb):
    d = x_ref.shape[-1]
    xs = jnp.concatenate([x_ref[0, :, bb, :] for bb in range(nb)], axis=0)
    h = _norm_mod(xs, g_ref[...], sh_ref[0], sc_ref[0]).astype(BF16)
    pq = [_dot(h[:, grp * cg:(grp + 1) * cg], w_ref[...]) for grp in range(d // cg)]
    p = jnp.concatenate([t[:, 0:cg] for t in pq], axis=1)
    q = jnp.concatenate([t[:, cg:2 * cg] for t in pq], axis=1)
    for bb in range(nb):
        rows = slice(bb * FM_A, (bb + 1) * FM_A)
        res = _dot(m_ref[...], jnp.concatenate([p[rows], q[rows]], axis=0).astype(BF16))
        o_ref[0, 0, :, bb, :] = res[0:FM_A]
        o_ref[0, 1, :, bb, :] = res[FM_A:2 * FM_A]


def _fm_front(x, g, shift, scale, w_cs, m1, nb=8):
    b, s, d = x.shape
    cg = w_cs.shape[0]
    per_b = pl.BlockSpec((1, 1, d), lambda bi, j: (bi, 0, 0))
    const = lambda a: pl.BlockSpec(a.shape, lambda bi, j: (0, 0))
    return pl.pallas_call(
        functools.partial(_fm_front_body, cg=cg, nb=nb),
        out_shape=jax.ShapeDtypeStruct((b, 2, FM_A, s // FM_A, d), F32),
        grid=(b, s // FM_A // nb),
        in_specs=[pl.BlockSpec((1, FM_A, nb, d), lambda bi, j: (bi, 0, j, 0)),
                  const(g), per_b, per_b, const(w_cs), const(m1)],
        out_specs=pl.BlockSpec((1, 2, FM_A, nb, d), lambda bi, j: (bi, 0, 0, j, 0)),
        compiler_params=_cparams(("parallel", "parallel")),
        name="fm_front",
    )(x.reshape(b, FM_A, s // FM_A, d), g, shift, scale, w_cs, m1)


def _fm_s2_body(s_ref, g_ref, o_ref, *, dblk, scale):
    for i in range(dblk):
        s2 = jnp.concatenate([s_ref[0, 0, i], s_ref[0, 1, i]], axis=0).astype(BF16)
        o_ref[:, i, :] = _dot(g_ref[i], s2) * scale


def _fm_stage2(sv, gcat, seq, d, dblk=8):
    b = sv.shape[0]
    scale = 1.0 / math.sqrt(seq * (d // F_GROUPS))
    out = pl.pallas_call(
        functools.partial(_fm_s2_body, dblk=dblk, scale=scale),
        out_shape=jax.ShapeDtypeStruct((b * FM_A, dblk * (FM_A // dblk), d), F32),
        grid=(b, FM_A // dblk),
        in_specs=[pl.BlockSpec((1, 2, dblk, FM_A, d), lambda bi, j: (bi, 0, j, 0, 0)),
                  pl.BlockSpec((dblk, FM_A, 2 * FM_A), lambda bi, j: (j, 0, 0))],
        out_specs=pl.BlockSpec((FM_A, dblk, d), lambda bi, j: (bi, j, 0)),
        compiler_params=_cparams(("parallel", "parallel")),
        name="fm_stage2",
    )(sv, gcat)
    return out.reshape(b, seq, d)


def _fourier_mix(x, g, shift, scale):
    b, s, d = x.shape
    assert s == FM_A * FM_A
    w_cs, m1, gcat = _fm_constants(d // F_GROUPS)
    return _fm_stage2(_fm_front(x, g, shift, scale, w_cs, m1), gcat, s, d)


MOE_TM = 1024
DMA_WINDOW = 128


def _router_body(x_ref, yf_ref, wf_ref, gf_ref, g_ref, sh_ref, sc_ref, wr_ref, br_ref,
                 xo_ref, h_ref, meta_ref, gw_ref, cnt_ref, carry):
    i = pl.program_id(0)

    @pl.when(i == 0)
    def _():
        carry[...] = jnp.zeros_like(carry)

    xm = x_ref[...] + gf_ref[0] * _dot(yf_ref[...].astype(BF16), wf_ref[...])
    xo_ref[...] = xm
    h = _norm_mod(xm, g_ref[...], sh_ref[0], sc_ref[0])
    _rows_to_tiles(h_ref, h)
    h_hi = h.astype(BF16)
    h_lo = (h - h_hi.astype(F32)).astype(BF16)
    by_hi = _dot(h_hi, wr_ref[...])
    logits = by_hi[:, 0:LANES] + by_hi[:, LANES:] + _dot(h_lo, wr_ref[:, 0:LANES]) + br_ref[...]
    lane = lax.broadcasted_iota(jnp.int32, logits.shape, 1)
    nl = logits.shape[-1]
    m1 = jnp.max(logits, axis=-1, keepdims=True)
    i1 = jnp.min(jnp.where(logits == m1, lane, nl), axis=-1, keepdims=True)
    rest = jnp.where(lane == i1, 3.0 * NEG_INF, logits)
    m2 = jnp.max(rest, axis=-1, keepdims=True)
    i2 = jnp.min(jnp.where(rest == m2, lane, nl), axis=-1, keepdims=True)
    e = jnp.exp(m2 - m1)
    gw_ref[...] = jnp.where(lane == 0, 1.0 / (1.0 + e), jnp.where(lane == 1, e / (1.0 + e), 0.0))
    onehot = jnp.where((lane == i1) | (lane == i2), 1.0, 0.0)
    tm = onehot.shape[0]
    earlier = lax.broadcasted_iota(jnp.int32, (tm, tm), 0) > lax.broadcasted_iota(jnp.int32, (tm, tm), 1)
    excl = _dot(jnp.where(earlier, 1.0, 0.0).astype(BF16), onehot.astype(BF16)) + carry[...]
    r1 = jnp.sum(jnp.where(lane == i1, excl, 0.0), axis=-1, keepdims=True).astype(jnp.int32)
    r2 = jnp.sum(jnp.where(lane == i2, excl, 0.0), axis=-1, keepdims=True).astype(jnp.int32)
    meta = jnp.where(lane == 0, i1, jnp.where(lane == 1, i2, jnp.where(lane == 2, r1, jnp.where(lane == 3, r2, 0))))
    meta_ref[...] = meta.T[0:ROW_SL]
    carry[...] = carry[...] + jnp.sum(onehot, axis=0, keepdims=True)
    cnt_ref[...] = carry[...]


def _router(x, y_f, w_f, gate_f, g, shift, scale, w_router, b_router, tm=512):
    b, s, d = x.shape
    t = b * s
    ne = w_router.shape[1]
    wr = jnp.pad(w_router.astype(F32), ((0, 0), (0, LANES - ne)))
    wr_hi = wr.astype(BF16)
    wr = jnp.concatenate([wr_hi, (wr - wr_hi.astype(F32)).astype(BF16)], axis=1)
    br = jnp.pad(b_router.astype(F32).reshape(1, ne), ((0, 0), (0, LANES - ne)), constant_values=NEG_INF)
    spt = s // tm
    per_b = pl.BlockSpec((1, 1, d), lambda i: (i // spt, 0, 0))
    const = lambda a: pl.BlockSpec(a.shape, lambda i: (0, 0))
    tok = pl.BlockSpec((tm, d), lambda i: (i, 0))
    return pl.pallas_call(
        _router_body,
        out_shape=[jax.ShapeDtypeStruct((t, d), F32),
                   jax.ShapeDtypeStruct((t * ROW_SL, LANES), F32),
                   jax.ShapeDtypeStruct((ROW_SL, t), jnp.int32),
                   jax.ShapeDtypeStruct((t, LANES), F32),
                   jax.ShapeDtypeStruct((1, LANES), F32)],
        grid=(t // tm,),
        in_specs=[tok, tok, const(w_f), per_b, const(g), per_b, per_b, const(wr), const(br)],
        out_specs=[tok,
                   pl.BlockSpec((tm * ROW_SL, LANES), lambda i: (i, 0)),
                   pl.BlockSpec((ROW_SL, tm), lambda i: (0, i)),
                   pl.BlockSpec((tm, LANES), lambda i: (i, 0)),
                   pl.BlockSpec((1, LANES), lambda i: (0, 0))],
        scratch_shapes=[pltpu.VMEM((1, LANES), F32)],
        compiler_params=_cparams(("arbitrary",)),
        name="router",
    )(x.reshape(t, d), y_f.reshape(t, d), w_f, gate_f, g, shift, scale, wr, br)


def _moe_plan(meta, counts, ne, tm):
    i1, i2, r1, r2 = meta[0], meta[1], meta[2], meta[3]
    cnt = counts[0, :ne].astype(jnp.int32)
    padded = ((cnt + tm - 1) // tm) * tm
    ends = jnp.cumsum(padded)
    offs = ends - padded
    pick = lambda idx: sum(jnp.where(idx == e, offs[e], 0) for e in range(ne))
    pos = jnp.concatenate([pick(i1) + r1, pick(i2) + r2]).astype(jnp.int32)
    n_tiles = (2 * i1.shape[0]) // tm + ne
    n_used = (ends[ne - 1] // tm).astype(jnp.int32)
    tile_start = jnp.minimum(jnp.arange(n_tiles, dtype=jnp.int32), n_used - 1) * tm
    tile_expert = jnp.sum(tile_start[:, None] >= ends[None, :], axis=1).astype(jnp.int32)
    group_end = sum(jnp.where(tile_expert == e, offs[e] + cnt[e], 0) for e in range(ne))
    tile_rows = jnp.clip(group_end - tile_start, 0, tm).astype(jnp.int32)
    return pos, offs + cnt, padded - cnt, tile_expert, n_used.reshape(1), tile_rows


def _windowed_copies(n, start_copy, wait_one):
    def body(i, carry):
        @pl.when(i >= DMA_WINDOW)
        def _():
            wait_one()
        start_copy(i)
        return carry

    lax.fori_loop(0, n, body, 0)

    def drain(i, carry):
        wait_one()
        return carry

    lax.fori_loop(0, jnp.minimum(n, DMA_WINDOW), drain, 0)


def _tile_of(ref, row):
    return ref.at[pl.ds(pl.multiple_of(row * ROW_SL, ROW_SL), ROW_SL)]


def _tiles_to_rows(ref, n, first=0):
    return jnp.concatenate([ref[pl.ds(first * ROW_SL + sl, n, stride=ROW_SL), :] for sl in range(ROW_SL)], axis=1)


def _rows_to_tiles(ref, val):
    n = val.shape[0]
    for sl in range(ROW_SL):
        ref[pl.ds(sl, n, stride=ROW_SL), :] = val[:, sl * LANES:(sl + 1) * LANES]


def _dispatch_body(pos_ref, pad_start_ref, pad_n_ref, h_ref, xs_hbm, sem, *, n_tok, ne):
    i = pl.program_id(0)
    td = h_ref.shape[0] // ROW_SL
    base = i * td
    copy = lambda src, dst: pltpu.make_async_copy(_tile_of(h_ref, src), _tile_of(xs_hbm, dst), sem)
    wait_one = lambda: copy(0, 0).wait()

    def start_token(r, carry):
        copy(r, pos_ref[base + r]).start(priority=0)
        copy(r, pos_ref[n_tok + base + r]).start(priority=1)
        return carry

    lax.fori_loop(0, td, start_token, 0, unroll=8)
    whole_tile = pltpu.make_async_copy(h_ref, xs_hbm.at[pl.ds(0, td * ROW_SL)], sem)
    whole_tile.wait()
    whole_tile.wait()

    @pl.when(i == 0)
    def _():
        for e in range(ne):
            first = pad_start_ref[e]
            _windowed_copies(pad_n_ref[e], lambda r: copy(0, first + r).start(), wait_one)


def _moe_dispatch(h3, pos, pad_start, pad_n, n_rows, td=1024):
    n_tok = h3.shape[0] // ROW_SL
    ne = pad_start.shape[0]
    return pl.pallas_call(
        functools.partial(_dispatch_body, n_tok=n_tok, ne=ne),
        out_shape=jax.ShapeDtypeStruct((n_rows * ROW_SL, LANES), h3.dtype),
        grid_spec=pltpu.PrefetchScalarGridSpec(
            num_scalar_prefetch=3, grid=(n_tok // td,),
            in_specs=[pl.BlockSpec((td * ROW_SL, LANES), lambda i, p, ps, pn: (i, 0))],
            out_specs=pl.BlockSpec(memory_space=pl.ANY),
            scratch_shapes=[pltpu.SemaphoreType.DMA(())]),
        compiler_params=_cparams(("arbitrary",)),
        name="moe_dispatch",
    )(pos, pad_start, pad_n, h3)


def _moe_grouped_body(te_ref, nu_ref, tr_ref, xs_ref, w1_ref, w3_ref, w2_ref, y_ref, xb_scr, acc_scr):
    i = pl.program_id(0)
    j = pl.program_id(1)
    tm = xb_scr.shape[0]
    hm = tm // 2

    def expert_rows(nrows):
        h = xb_scr[0:nrows]
        a = _dot(h, w1_ref[0].astype(BF16))
        u = (a * jax.nn.sigmoid(a) * _dot(h, w3_ref[0].astype(BF16))).astype(BF16)
        part = _dot(u, w2_ref[0].astype(BF16))

        @pl.when(j == 0)
        def _():
            acc_scr[0:nrows] = part

        @pl.when(j > 0)
        def _():
            acc_scr[0:nrows] += part

    @pl.when(i < nu_ref[0])
    def _():
        @pl.when(j == 0)
        def _():
            xb_scr[...] = _tiles_to_rows(xs_ref, tm).astype(BF16)

        @pl.when(tr_ref[i] > hm)
        def _():
            expert_rows(tm)

        @pl.when(tr_ref[i] <= hm)
        def _():
            expert_rows(hm)

            @pl.when(j == 0)
            def _():
                acc_scr[hm:tm] = jnp.zeros((tm - hm, acc_scr.shape[1]), F32)

        @pl.when(j == pl.num_programs(1) - 1)
        def _():
            _rows_to_tiles(y_ref, acc_scr[...])


def _moe_grouped(xs, tile_expert, n_used, tile_rows, w1, w3, w2, tm, fb=512):
    ne, d, f = w1.shape
    n_rows = xs.shape[0] // ROW_SL
    nj = f // fb
    row_tile = lambda i, j, te, nu, tr: (jnp.maximum(jnp.minimum(i, nu[0] - 1), 0), 0)
    jj = lambda i, j, nu: jnp.where(i < nu[0], j, nj - 1)
    return pl.pallas_call(
        _moe_grouped_body,
        out_shape=jax.ShapeDtypeStruct(xs.shape, F32),
        grid_spec=pltpu.PrefetchScalarGridSpec(
            num_scalar_prefetch=3, grid=(n_rows // tm, nj),
            in_specs=[pl.BlockSpec((tm * ROW_SL, LANES), row_tile),
                      pl.BlockSpec((1, d, fb), lambda i, j, te, nu, tr: (te[i], 0, jj(i, j, nu))),
                      pl.BlockSpec((1, d, fb), lambda i, j, te, nu, tr: (te[i], 0, jj(i, j, nu))),
                      pl.BlockSpec((1, fb, d), lambda i, j, te, nu, tr: (te[i], jj(i, j, nu), 0))],
            out_specs=pl.BlockSpec((tm * ROW_SL, LANES), row_tile),
            scratch_shapes=[pltpu.VMEM((tm, d), BF16), pltpu.VMEM((tm, d), F32)]),
        compiler_params=_cparams(("arbitrary", "arbitrary")),
        name="moe_grouped",
    )(tile_expert, n_used, tile_rows, xs, w1, w3, w2)


def _moe_final_body(pos_ref, x_ref, y_hbm, gw_ref, gt_ref, fg_ref, o_ref, yg_scr, sem, *, n_tok):
    i = pl.program_id(0)
    tc = x_ref.shape[0]
    slot = i % 2

    def gather_tile(step, into):
        base = step * tc

        def start_token(r, carry):
            dst = yg_scr.at[into]
            pltpu.make_async_copy(_tile_of(y_hbm, pos_ref[base + r]), _tile_of(dst, r),
                                  sem.at[into]).start(priority=0)
            pltpu.make_async_copy(_tile_of(y_hbm, pos_ref[n_tok + base + r]), _tile_of(dst, tc + r),
                                  sem.at[into]).start(priority=1)
            return carry

        lax.fori_loop(0, tc, start_token, 0, unroll=8)

    @pl.when(i == 0)
    def _():
        gather_tile(0, 0)

    @pl.when(i + 1 < pl.num_programs(0))
    def _():
        gather_tile(i + 1, 1 - slot)

    pltpu.make_async_copy(y_hbm.at[pl.ds(0, 2 * tc * ROW_SL)], yg_scr.at[slot], sem.at[slot]).wait()
    gw = gw_ref[...]
    rows = yg_scr.at[slot]
    y = gw[:, 0:1] * _tiles_to_rows(rows, tc) + gw[:, 1:2] * _tiles_to_rows(rows, tc, first=tc)
    xo = x_ref[...] + gt_ref[0] * y
    ms = jnp.mean(xo * xo, axis=-1, keepdims=True)
    o_ref[...] = xo * lax.rsqrt(ms + EPS) * fg_ref[...]


def _moe_final(x, y, pos, gw, gt, final_g, tc=512):
    b, s, d = x.shape
    t = b * s
    spt = s // tc
    out = pl.pallas_call(
        functools.partial(_moe_final_body, n_tok=t),
        out_shape=jax.ShapeDtypeStruct((t, d), F32),
        grid_spec=pltpu.PrefetchScalarGridSpec(
            num_scalar_prefetch=1, grid=(t // tc,),
            in_specs=[pl.BlockSpec((tc, d), lambda i, p: (i, 0)),
                      pl.BlockSpec(memory_space=pl.ANY),
                      pl.BlockSpec((tc, LANES), lambda i, p: (i, 0)),
                      pl.BlockSpec((1, 1, d), lambda i, p: (i // spt, 0, 0)),
                      pl.BlockSpec(final_g.shape, lambda i, p: (0, 0))],
            out_specs=pl.BlockSpec((tc, d), lambda i, p: (i, 0)),
            scratch_shapes=[pltpu.VMEM((2, 2 * tc * ROW_SL, LANES), F32), pltpu.SemaphoreType.DMA((2,))]),
        compiler_params=_cparams(("arbitrary",)),
        name="moe_final",
    )(pos, x.reshape(t, d), y, gw, gt, final_g)
    return out.reshape(b, s, d)


def _moe_routed(x, y_f, w_f, gate_f, g, shift, scale, gt, final_g, w_router, b_router, w1, w3, w2):
    ne = w1.shape[0]
    tm = MOE_TM
    x1, h3, meta, gw, counts = _router(x, y_f, w_f, gate_f, g, shift, scale, w_router, b_router)
    pos, pad_start, pad_n, tile_expert, n_used, tile_rows = _moe_plan(meta, counts, ne, tm)
    assert x.shape[-1] == ROW_SL * LANES
    n_rows = (2 * (h3.shape[0] // ROW_SL) // tm + ne) * tm
    xs = _moe_dispatch(h3, pos, pad_start, pad_n, n_rows)
    y = _moe_grouped(xs, tile_expert, n_used, tile_rows, w1, w3, w2, tm)
    return _moe_final(x1.reshape(x.shape), y, pos, gw, gt, final_g)


def kernel(x, c, ctx, c_ctx, w_ada, b_ada, norm_g, w_in, hy_short_w, hy_short_b, hy_f_w1, hy_f_b1, hy_f_w2, hy_f_b2, hy_f_w3, hy_f_freq, hy_skip, na_rpb, w_mix_out, ffn_w1, ffn_w3, ffn_w2, w_fourier, w_router, b_router, moe_w1, moe_w3, moe_w2, final_g):
    b, s, d = x.shape
    depth = w_ada.shape[0]
    assert depth == 2, "layer 0 mixes with Hyena/attention, layer 1 with Fourier/MoE"
    c_hy = hy_skip.shape[-1]
    c_na = d - c_hy

    cvec = jnp.concatenate([c, c_ctx[None, :], jnp.zeros((8 - b - 1, d), F32)], axis=0)
    mods = _ada(cvec, w_ada, b_ada)

    def mod(layer, idx, ctx_row=False):
        m = mods[layer, :, idx * d:(idx + 1) * d]
        return m[b:b + 1, None, :] if ctx_row else m[0:b, None, :]

    row = lambda a: a.reshape(1, -1)

    w_in0 = w_in[0].astype(BF16)
    w_hy, w_qkv = w_in0[:, 0:3 * c_hy], w_in0[:, 3 * c_hy:]
    v, x1, x2, q, k, va = _inproj(x, row(norm_g[0, 0]), mod(0, 0), mod(0, 1), w_hy, w_qkv,
                                  hy_short_w[0], row(hy_short_b[0]))
    kc, vc = _ctxkv(ctx, row(norm_g[0, 0]), mod(0, 0, True), mod(0, 1, True), w_qkv[:, c_na:])
    y_na = _natt(q, k, va, kc, vc, _na_bias_table(na_rpb[0]))
    y_hy = _hyena(v, x1, x2, hy_f_w1[0], hy_f_b1[0], hy_f_w2[0], hy_f_b2[0], hy_f_w3[0],
                  hy_f_freq[0], hy_skip[0])
    x = _mix_ffn(x, y_hy, y_na, w_mix_out[0].astype(BF16), mod(0, 2),
                 row(norm_g[0, 1]), mod(0, 3), mod(0, 4), mod(0, 5),
                 ffn_w1[0].astype(BF16), ffn_w3[0].astype(BF16), ffn_w2[0].astype(BF16))

    y_f = _fourier_mix(x, row(norm_g[1, 0]), mod(1, 0), mod(1, 1))
    return _moe_routed(x, y_f, w_fourier[0].astype(BF16), mod(1, 2),
                       row(norm_g[1, 1]), mod(1, 3), mod(1, 4), mod(1, 5), row(final_g),
                       w_router[0], b_router[0],
                       moe_w1[0], moe_w3[0], moe_w2[0])
```

```python
import functools
import math

import numpy as np
import jax
import jax.numpy as jnp
from jax import lax
from jax.experimental import pallas as pl
from jax.experimental.pallas import tpu as pltpu

F32 = jnp.float32
BF16 = jnp.bfloat16
HIGHEST = lax.Precision.HIGHEST

GRID_W = 64
NA_HEAD_DIM = 32
NA_WIN_R = 8
NA_WIN_C = 16
HYENA_EMB = 33
HYENA_BANDS = (HYENA_EMB - 1) // 2
HYENA_FAST_DECAY = 0.3
HYENA_SLOW_DECAY = 1.5
HYENA_TARGET = 1e-2
F_GROUPS = 4
N_MOD = 6
EPS = 1e-6
NEG_INF = -1e30

FFT_A = 64
FFT_R = 128
FFT_KA = FFT_A // 2 + 1
FFT_KA_PAD = 40
FM_A = 64

LANES = 128
ROW_SL = 8
VMEM_LIMIT = 48 * 1024 * 1024
LARGE_VMEM_LIMIT = 56 * 1024 * 1024


def _cparams(sem, vmem_limit=VMEM_LIMIT):
    return pltpu.CompilerParams(dimension_semantics=sem, vmem_limit_bytes=vmem_limit)


def _dot(a, b):
    return jnp.dot(a, b, preferred_element_type=F32)


def _mxu_const(m):
    return jnp.asarray(m, dtype=F32).astype(BF16)


def _norm_mod(x, g, shift, scale):
    ms = jnp.mean(x * x, axis=-1, keepdims=True)
    y = x * lax.rsqrt(ms + EPS) * g
    return y * (1.0 + scale) + shift


def _ada_body(c_ref, w_ref, b_ref, o_ref):
    cv = c_ref[...]
    s = cv * jax.nn.sigmoid(cv)
    o_ref[0] = jnp.dot(s, w_ref[0], precision=HIGHEST, preferred_element_type=F32) + b_ref[0]


def _ada(cvec, w_ada, b_ada):
    depth, d, n = w_ada.shape
    rows = cvec.shape[0]
    bn = n // 4
    return pl.pallas_call(
        _ada_body,
        out_shape=jax.ShapeDtypeStruct((depth, rows, n), F32),
        grid=(depth, n // bn),
        in_specs=[pl.BlockSpec((rows, d), lambda l, j: (0, 0)),
                  pl.BlockSpec((1, d, bn), lambda l, j: (l, 0, j)),
                  pl.BlockSpec((1, 1, bn), lambda l, j: (l, 0, j))],
        out_specs=pl.BlockSpec((1, rows, bn), lambda l, j: (l, 0, j)),
        compiler_params=_cparams(("parallel", "parallel")),
        name="ada",
    )(cvec, w_ada, b_ada.reshape(depth, 1, n))


def _inproj_body(x_ref, xp_ref, xn_ref, g_ref, sh_ref, sc_ref, why_ref, wqkv_ref, sw_ref, sb_ref,
                 v_ref, x1_ref, x2_ref, q_ref, k_ref, va_ref, *, n_tiles, q_scale, c_hy, c_na):
    i = pl.program_id(1)
    g, sh, sc = g_ref[...], sh_ref[0], sc_ref[0]
    hf = _norm_mod(x_ref[0], g, sh, sc)
    h = hf.astype(BF16)
    tm = hf.shape[0]
    hx = jnp.concatenate([_norm_mod(xp_ref[0], g, sh, sc), hf, _norm_mod(xn_ref[0], g, sh, sc)],
                         axis=0).astype(BF16)
    row = lax.broadcasted_iota(jnp.int32, (tm, c_hy), 0)
    sw = sw_ref[...]
    sb = sb_ref[...]
    lo, hi = ROW_SL, ROW_SL + tm
    for ci, out_ref in enumerate((v_ref, x1_ref, x2_ref)):
        cols = slice(ci * c_hy, (ci + 1) * c_hy)
        zx = _dot(hx, why_ref[:, cols])
        zh = zx[lo:hi]
        zp = jnp.where(i > 0, zx[lo - 1:lo], 0.0)
        zn = jnp.where(i < n_tiles - 1, zx[hi:hi + 1], 0.0)
        z_m1 = jnp.where(row == 0, zp, pltpu.roll(zh, 1, 0))
        z_p1 = jnp.where(row == tm - 1, zn, pltpu.roll(zh, tm - 1, 0))
        out_ref[0] = z_m1 * sw[0:1, cols] + zh * sw[1:2, cols] + z_p1 * sw[2:3, cols] + sb[:, cols]
    for ci, (out_ref, mult) in enumerate(((q_ref, q_scale), (k_ref, None), (va_ref, None))):
        z = _dot(h, wqkv_ref[:, ci * c_na:(ci + 1) * c_na])
        out_ref[0] = (z if mult is None else z * mult).astype(BF16)


def _inproj(x, g, shift, scale, w_hy, w_qkv, short_w, short_b, tm=1024):
    b, s, d = x.shape
    c_hy = w_hy.shape[1] // 3
    c_na = w_qkv.shape[1] // 3
    n_tiles = s // tm
    halo_per_tile = tm // ROW_SL
    last_halo = s // ROW_SL - 1
    body = functools.partial(_inproj_body, n_tiles=n_tiles, q_scale=NA_HEAD_DIM ** -0.5,
                             c_hy=c_hy, c_na=c_na)
    tok = lambda c: pl.BlockSpec((1, tm, c), lambda bi, i: (bi, i, 0))
    full2 = lambda a: pl.BlockSpec(a.shape, lambda bi, i: (0, 0))
    resident = lambda a: pl.BlockSpec(a.shape, lambda bi, i: (0, 0), pipeline_mode=pl.Buffered(1))
    per_b = pl.BlockSpec((1, 1, d), lambda bi, i: (bi, 0, 0))
    return pl.pallas_call(
        body,
        out_shape=[jax.ShapeDtypeStruct((b, s, c_hy), F32)] * 3 + [jax.ShapeDtypeStruct((b, s, c_na), BF16)] * 3,
        grid=(b, n_tiles),
        in_specs=[tok(d),
                  pl.BlockSpec((1, ROW_SL, d), lambda bi, i: (bi, jnp.maximum(i * halo_per_tile - 1, 0), 0)),
                  pl.BlockSpec((1, ROW_SL, d),
                               lambda bi, i: (bi, jnp.minimum((i + 1) * halo_per_tile, last_halo), 0)),
                  full2(g), per_b, per_b, resident(w_hy), resident(w_qkv), full2(short_w), full2(short_b)],
        out_specs=[tok(c_hy)] * 3 + [tok(c_na)] * 3,
        compiler_params=_cparams(("parallel", "parallel"), vmem_limit=LARGE_VMEM_LIMIT),
        name="inproj",
    )(x, x, x, g, shift, scale, w_hy, w_qkv, short_w, short_b)


def _ctxkv_body(x_ref, g_ref, sh_ref, sc_ref, w_ref, k_ref, v_ref, *, c_na):
    h = _norm_mod(x_ref[0], g_ref[...], sh_ref[0], sc_ref[0]).astype(BF16)
    z = _dot(h, w_ref[...])
    k_ref[0] = z[:, 0:c_na].astype(BF16)
    v_ref[0] = z[:, c_na:2 * c_na].astype(BF16)


def _ctxkv(ctx, g, shift, scale, w_kv):
    b, n, d = ctx.shape
    c_na = w_kv.shape[1] // 2
    one = pl.BlockSpec((1, 1, d), lambda bi: (0, 0, 0))
    return pl.pallas_call(
        functools.partial(_ctxkv_body, c_na=c_na),
        out_shape=[jax.ShapeDtypeStruct((b, n, c_na), BF16)] * 2,
        grid=(b,),
        in_specs=[pl.BlockSpec((1, n, d), lambda bi: (bi, 0, 0)),
                  pl.BlockSpec(g.shape, lambda bi: (0, 0)), one, one,
                  pl.BlockSpec(w_kv.shape, lambda bi: (0, 0))],
        out_specs=[pl.BlockSpec((1, n, c_na), lambda bi: (bi, 0, 0))] * 2,
        compiler_params=_cparams(("parallel",)),
        name="ctxkv",
    )(ctx, g, shift, scale, w_kv)


NA_HEADS_PER_BLK = 8
NA_ROWS_PER_STEP = 8


def _na_bias_body(r_ref, ok_ref, o_ref):
    w = GRID_W
    lane = lax.broadcasted_iota(jnp.int32, (w, LANES), 1)
    ok = ok_ref[...] > 0.5

    def toeplitz(ri):
        row = jnp.broadcast_to(r_ref[0, ri:ri + 1, :], (w, LANES))
        return pltpu.roll(row, LANES - (NA_WIN_C - 1), 1, stride=1, stride_axis=0)

    blocks = [toeplitz(ri) for ri in range(2 * NA_WIN_R - 1)]
    for off in range(NA_WIN_R):
        first = NA_WIN_R - 1 - off
        pairs = []
        for p in range(NA_WIN_R // 2):
            a, b = blocks[first + 2 * p], blocks[first + 2 * p + 1]
            pairs.append(jnp.where(ok, jnp.where(lane < w, a, pltpu.roll(b, w, 1)), NEG_INF))
        o_ref[0, off] = jnp.concatenate(pairs, axis=1)


def _na_bias_table(rpb):
    w = GRID_W
    h, nr, nc = rpb.shape
    assert 2 * w == LANES and nr == 2 * NA_WIN_R - 1 and nc == 2 * NA_WIN_C - 1
    hpb = NA_HEADS_PER_BLK
    col = np.arange(w)[:, None]
    kc = np.arange(LANES)[None, :] % w
    c_start = np.clip(col - NA_WIN_C // 2, 0, w - NA_WIN_C)
    ok = jnp.asarray((kc >= c_start) & (kc < c_start + NA_WIN_C), dtype=F32)
    rp = jnp.pad(rpb.astype(F32), ((0, 0), (0, 2 * ROW_SL - nr), (0, LANES - nc)))
    return pl.pallas_call(
        _na_bias_body,
        out_shape=jax.ShapeDtypeStruct((h // hpb, NA_WIN_R, hpb * w, NA_WIN_R * w), F32),
        grid=(h,),
        in_specs=[pl.BlockSpec((1,) + rp.shape[1:], lambda i: (i, 0, 0)),
                  pl.BlockSpec(ok.shape, lambda i: (0, 0))],
        out_specs=pl.BlockSpec((1, NA_WIN_R, w, NA_WIN_R * w), lambda i: (i // hpb, 0, i % hpb, 0)),
        compiler_params=_cparams(("parallel",)),
        name="na_bias",
    )(rp, ok)


def _natt_body(q_ref, k_ref, v_ref, kc_ref, vc_ref, bias_ref, o_ref, *, rows):
    w = GRID_W
    hpb = NA_HEADS_PER_BLK
    nloc = NA_WIN_R * w
    lane = lax.broadcasted_iota(jnp.int32, (1, hpb * NA_HEAD_DIM), 1)
    in_head = [(lane >= NA_HEAD_DIM * hh) & (lane < NA_HEAD_DIM * (hh + 1)) for hh in range(hpb)]
    kcx = kc_ref[0]
    vcx = vc_ref[0]
    nt = (((1,), (1,)), ((), ()))

    def one_row(r):
        r0 = jnp.clip(r - NA_WIN_R // 2, 0, rows - NA_WIN_R)
        off = r - r0
        qs = q_ref[0, pl.ds(pl.multiple_of(r * w, w), w), :]
        kw = k_ref[0, pl.ds(pl.multiple_of(r0 * w, w), nloc), :]
        vw = v_ref[0, pl.ds(pl.multiple_of(r0 * w, w), nloc), :]
        zero = jnp.zeros_like(qs)
        acc = None
        half = hpb // 2
        for part in range(2):
            heads = range(part * half, (part + 1) * half)
            qst = jnp.concatenate([jnp.where(in_head[hh], qs, zero) for hh in heads], axis=0)
            bias = bias_ref[0, off, part * half * w:(part + 1) * half * w, :]
            s_loc = lax.dot_general(qst, kw, nt, preferred_element_type=F32) + bias
            s_ctx = lax.dot_general(qst, kcx, nt, preferred_element_type=F32)
            m = jnp.maximum(jnp.max(s_loc, axis=-1, keepdims=True), jnp.max(s_ctx, axis=-1, keepdims=True))
            p_loc = jnp.exp(s_loc - m)
            p_ctx = jnp.exp(s_ctx - m)
            den = jnp.sum(p_loc, axis=-1, keepdims=True) + jnp.sum(p_ctx, axis=-1, keepdims=True)
            o = (_dot(p_loc.astype(BF16), vw) + _dot(p_ctx.astype(BF16), vcx)) * (1.0 / den)
            for n, hh in enumerate(heads):
                piece = jnp.where(in_head[hh], o[n * w:(n + 1) * w], 0.0)
                acc = piece if acc is None else acc + piece
        o_ref[0, pl.ds(pl.multiple_of(r * w, w), w), :] = acc.astype(BF16)

    def row_group(i, carry):
        for r in range(NA_ROWS_PER_STEP):
            one_row(NA_ROWS_PER_STEP * i + r)
        return carry

    lax.fori_loop(0, rows // NA_ROWS_PER_STEP, row_group, 0)


def _natt(q, k, v, kc, vc, bias):
    b, s, c = q.shape
    nctx = kc.shape[1]
    lw = NA_HEADS_PER_BLK * NA_HEAD_DIM
    rows = s // GRID_W
    seq = pl.BlockSpec((1, s, lw), lambda bi, g: (bi, 0, g))
    cx = pl.BlockSpec((1, nctx, lw), lambda bi, g: (bi, 0, g))
    return pl.pallas_call(
        functools.partial(_natt_body, rows=rows),
        out_shape=jax.ShapeDtypeStruct((b, s, c), BF16),
        grid=(b, c // lw),
        in_specs=[seq, seq, seq, cx, cx,
                  pl.BlockSpec((1,) + bias.shape[1:], lambda bi, g: (g, 0, 0, 0))],
        out_specs=seq,
        compiler_params=_cparams(("parallel", "parallel")),
        name="natt",
    )(q, k, v, kc, vc, bias)


def _hyena_feats(seq_len):
    t = jnp.linspace(0.0, 1.0, seq_len, dtype=F32)[:, None]
    bands = jnp.linspace(1e-4, HYENA_BANDS - 1, HYENA_BANDS, dtype=F32)
    ang = (2.0 * math.pi / seq_len) * jnp.arange(seq_len, dtype=F32)[:, None] * bands[None, :]
    feats = jnp.concatenate([t, jnp.cos(ang), -jnp.sin(ang)], axis=-1)
    return jnp.pad(feats, ((0, 0), (0, LANES - HYENA_EMB)))


def _filt_body(feat_ref, w1_ref, b1_ref, w2_ref, b2_ref, w3_ref, fr_ref, dl_ref, o_ref, l1_ref, h_scr,
               *, halves):
    j = pl.program_id(0)
    hp = functools.partial(jnp.dot, precision=HIGHEST, preferred_element_type=F32)
    feats = feat_ref[...]

    @pl.when(j == 0)
    def _():
        fr = fr_ref[...]
        h = jnp.sin(fr[0:1] * (hp(feats, w1_ref[...]) + b1_ref[...]))
        h_scr[...] = jnp.sin(fr[1:2] * (hp(h, w2_ref[...]) + b2_ref[...]))

    h2 = h_scr[...]
    h_hi = h2.astype(BF16)
    h_lo = (h2 - h_hi.astype(F32)).astype(BF16)
    w3 = w3_ref[...]
    w_hi = w3.astype(BF16)
    w_lo = (w3 - w_hi.astype(F32)).astype(BF16)
    hc = (_dot(jnp.concatenate([h_hi, h_lo], axis=1), jnp.concatenate([w_hi, w_hi], axis=0))
          + _dot(h_hi, w_lo))
    t = feats[:, 0:1]
    hc = hc * jnp.exp(-t * dl_ref[...])
    row = lax.broadcasted_iota(jnp.int32, hc.shape, 0)
    hc = jnp.where((row == 0) & ((j // halves) % 2 == 1), 0.0, hc)
    l1_ref[0] = jnp.sum(jnp.abs(hc), axis=0, keepdims=True)
    o_ref[0] = hc


def _hyena_filter_taps(seq_len, f_w1, f_b1, f_w2, f_b2, f_w3, f_freq, c_hy):
    feats = _hyena_feats(seq_len)
    hid = f_w1.shape[1]
    w1 = jnp.pad(f_w1.astype(F32), ((0, LANES - HYENA_EMB), (0, 0)))
    deltas = jnp.abs(jnp.linspace(math.log(HYENA_TARGET) / HYENA_SLOW_DECAY,
                                  math.log(HYENA_TARGET) / HYENA_FAST_DECAY, c_hy, dtype=F32))[None, :]
    nblk = f_w3.shape[1] // c_hy
    halves = 2
    cb = c_hy // halves
    c0 = lambda a: pl.BlockSpec(a.shape, lambda j: (0, 0))
    b1, b2 = f_b1.reshape(1, hid), f_b2.reshape(1, hid)
    return pl.pallas_call(
        functools.partial(_filt_body, halves=halves),
        out_shape=[jax.ShapeDtypeStruct((nblk, seq_len, c_hy), F32),
                   jax.ShapeDtypeStruct((nblk, 1, c_hy), F32)],
        grid=(nblk * halves,),
        in_specs=[c0(feats), c0(w1), c0(b1), c0(f_w2), c0(b2),
                  pl.BlockSpec((hid, cb), lambda j: (0, j)), c0(f_freq),
                  pl.BlockSpec((1, cb), lambda j: (0, j % halves))],
        out_specs=[pl.BlockSpec((1, seq_len, cb), lambda j: (j // halves, 0, j % halves)),
                   pl.BlockSpec((1, 1, cb), lambda j: (j // halves, 0, j % halves))],
        scratch_shapes=[pltpu.VMEM((seq_len, hid), F32)],
        compiler_params=_cparams(("arbitrary",)),
        name="hyena_filter",
    )(feats, w1, b1, f_w2, b2, f_w3, f_freq, deltas)


def _conv_dft_constants():
    a_half = FFT_A // 2
    n = FFT_A * FFT_R
    ka = np.arange(FFT_KA)[:, None]
    a = np.arange(a_half)[None, :]
    ph = 2.0 * np.pi * ka * a / FFT_A
    m_fwd = np.zeros((2 * FFT_KA_PAD, a_half))
    m_fwd[:FFT_KA] = np.cos(ph)
    m_fwd[FFT_KA_PAD:FFT_KA_PAD + FFT_KA] = -np.sin(ph)
    wgt = np.where((ka == 0) | (ka == FFT_A // 2), 1.0, 2.0)
    m_inv = np.zeros((a_half, 2 * FFT_KA_PAD))
    m_inv[:, :FFT_KA] = (wgt * np.cos(ph)).T / n
    m_inv[:, FFT_KA_PAD:FFT_KA_PAD + FFT_KA] = (-wgt * np.sin(ph)).T / n
    kb = np.arange(FFT_R)[None, :, None]
    b = np.arange(FFT_R)[None, None, :]
    kaa = np.arange(FFT_KA)[:, None, None]
    th = 2.0 * np.pi * (b * kb / FFT_R + b * kaa / n)
    gr, gi = np.cos(th), -np.sin(th)
    g2 = np.zeros((FFT_KA_PAD, 2 * FFT_R, 2 * FFT_R))
    g2[:FFT_KA] = np.block([[gr, -gi], [gi, gr]])
    grt, git = gr.transpose(0, 2, 1), gi.transpose(0, 2, 1)
    g2h = np.zeros_like(g2)
    g2h[:FFT_KA] = np.block([[grt, git], [-git, grt]])
    return _mxu_const(m_fwd), _mxu_const(m_inv), _mxu_const(g2), _mxu_const(g2h)


FFT_NB = 32


def _fwd1_body(m_ref, u_ref, o_ref):
    u = jnp.concatenate([u_ref[0, :, bb, :] for bb in range(FFT_NB)], axis=1).astype(BF16)
    res = _dot(m_ref[...], u)
    o_ref[0, 0] = res[0:FFT_KA_PAD]
    o_ref[0, 1] = res[FFT_KA_PAD:2 * FFT_KA_PAD]


def _conv_fwd1(u, m_fwd):
    n, seq, c = u.shape
    a_half = FFT_A // 2
    return pl.pallas_call(
        _fwd1_body,
        out_shape=jax.ShapeDtypeStruct((n, 2, FFT_KA_PAD, FFT_R * c), F32),
        grid=(n, FFT_R // FFT_NB),
        in_specs=[pl.BlockSpec(m_fwd.shape, lambda i, j: (0, 0)),
                  pl.BlockSpec((1, a_half, FFT_NB, c), lambda i, j: (i, 0, j, 0))],
        out_specs=pl.BlockSpec((1, 2, FFT_KA_PAD, FFT_NB * c), lambda i, j: (i, 0, 0, j)),
        compiler_params=_cparams(("parallel", "parallel")),
        name="conv_fwd1",
    )(m_fwd, u.reshape(n, a_half, FFT_R, c))


FFT_KB = 8


def _rows_to_slabs(src_ref, dst_scr, c):
    for part in range(2):
        for b in range(FFT_R):
            dst_scr[part, :, b, :] = src_ref[0, part, :, b * c:(b + 1) * c]


def _slabs_to_rows(src_scr, dst_ref, c):
    for part in range(2):
        for b in range(FFT_R):
            dst_ref[0, part, :, b * c:(b + 1) * c] = src_scr[part, :, b, :]


def _slab(scr, i):
    return jnp.concatenate([scr[0, i], scr[1, i]], axis=0).astype(BF16)


def _per_ka_block(j, work, clear):
    full_blocks = FFT_KA // FFT_KB
    tail = FFT_KA - full_blocks * FFT_KB

    @pl.when(j < full_blocks)
    def _():
        for i in range(FFT_KB):
            work(i)

    @pl.when(j >= full_blocks)
    def _():
        for i in range(FFT_KB):
            (work if i < tail else clear)(i)


def _fwd2f_body(sf_ref, sb_ref, g_ref, l1_ref, kf_ref, f3, b3):
    o = pl.program_id(0)
    j = pl.program_id(1)
    r2 = 2 * FFT_R
    c = kf_ref.shape[-1]
    _rows_to_slabs(sf_ref, f3, c)
    _rows_to_slabs(sb_ref, b3, c)
    inv = 1.0 / (l1_ref[2 * o] + l1_ref[2 * o + 1] + EPS)

    def spectrum(i):
        xf = _dot(g_ref[i], _slab(f3, i))
        xb = _dot(g_ref[i], _slab(b3, i))
        kf_ref[0, i, 0:FFT_R] = (xf[0:FFT_R] + xb[0:FFT_R]) * inv
        kf_ref[0, i, FFT_R:r2] = (xf[FFT_R:r2] - xb[FFT_R:r2]) * inv

    def clear(i):
        kf_ref[0, i] = jnp.zeros((r2, c), F32)

    _per_ka_block(j, spectrum, clear)


def _filter_spectrum(s_filt, l1, g2, c):
    n_ord = s_filt.shape[0] // 2
    cols = s_filt.shape[-1]
    r2 = 2 * FFT_R
    return pl.pallas_call(
        _fwd2f_body,
        out_shape=jax.ShapeDtypeStruct((n_ord, FFT_KA_PAD, r2, c), F32),
        grid=(n_ord, FFT_KA_PAD // FFT_KB),
        in_specs=[pl.BlockSpec((1, 2, FFT_KB, cols), lambda o, j: (2 * o, 0, j, 0)),
                  pl.BlockSpec((1, 2, FFT_KB, cols), lambda o, j: (2 * o + 1, 0, j, 0)),
                  pl.BlockSpec((FFT_KB, r2, r2), lambda o, j: (j, 0, 0)),
                  pl.BlockSpec(l1.shape, lambda o, j: (0, 0, 0))],
        out_specs=pl.BlockSpec((1, FFT_KB, r2, c), lambda o, j: (o, j, 0, 0)),
        scratch_shapes=[pltpu.VMEM((2, FFT_KB, FFT_R, c), F32)] * 2,
        compiler_params=_cparams(("parallel", "parallel")),
        name="filter_spectrum",
    )(s_filt, s_filt, g2, l1)


def _mid_body(s_ref, g_ref, gh_ref, kf_ref, t_ref, s3, t3):
    j = pl.program_id(1)
    r2 = 2 * FFT_R
    c = kf_ref.shape[-1]
    _rows_to_slabs(s_ref, s3, c)

    def convolve(i):
        x = _dot(g_ref[i], _slab(s3, i))
        xr, xi = x[0:FFT_R], x[FFT_R:r2]
        kr, ki = kf_ref[0, i, 0:FFT_R], kf_ref[0, i, FFT_R:r2]
        y = jnp.concatenate([xr * kr - xi * ki, xr * ki + xi * kr], axis=0).astype(BF16)
        t = _dot(gh_ref[i], y)
        t3[0, i] = t[0:FFT_R]
        t3[1, i] = t[FFT_R:r2]

    def clear(i):
        t3[0, i] = jnp.zeros((FFT_R, c), F32)
        t3[1, i] = jnp.zeros((FFT_R, c), F32)

    _per_ka_block(j, convolve, clear)
    _slabs_to_rows(t3, t_ref, c)


def _conv_mid(s, kf, order, g2, g2h, c):
    n, _, _, cols = s.shape
    r2 = 2 * FFT_R
    blk = pl.BlockSpec((1, 2, FFT_KB, cols), lambda i, j: (i, 0, j, 0))
    gspec = pl.BlockSpec((FFT_KB, r2, r2), lambda i, j: (j, 0, 0))
    return pl.pallas_call(
        _mid_body,
        out_shape=jax.ShapeDtypeStruct(s.shape, F32),
        grid=(n, FFT_KA_PAD // FFT_KB),
        in_specs=[blk, gspec, gspec,
                  pl.BlockSpec((1, FFT_KB, r2, c), lambda i, j: (order, j, 0, 0))],
        out_specs=blk,
        scratch_shapes=[pltpu.VMEM((2, FFT_KB, FFT_R, c), F32)] * 2,
        compiler_params=_cparams(("parallel", "parallel")),
        name="conv_mid",
    )(s, g2, g2h, kf)


def _inv1_body(m_ref, mf_ref, t_ref, u_ref, xg_ref, sk_ref, o_ref, *s_ref):
    c = u_ref.shape[-1]
    t2 = t_ref[0].reshape(2 * FFT_KA_PAD, FFT_NB * c).astype(BF16)
    y = _dot(m_ref[...], t2)
    gated = []
    for bb in range(FFT_NB):
        conv = y[:, bb * c:(bb + 1) * c] + u_ref[0, :, bb, :] * sk_ref[...]
        gated.append(xg_ref[0, :, bb, :] * conv)
        o_ref[0, :, bb, :] = gated[-1]
    if s_ref:
        res = _dot(mf_ref[...], jnp.concatenate(gated, axis=1).astype(BF16))
        s_ref[0][0, 0] = res[0:FFT_KA_PAD]
        s_ref[0][0, 1] = res[FFT_KA_PAD:2 * FFT_KA_PAD]


def _conv_inv1(t, u, xg, skip, m_inv, m_fwd, with_spectrum):
    n, seq, c = u.shape
    a_half = FFT_A // 2
    sk = skip.astype(F32).reshape(1, c)
    uspec = pl.BlockSpec((1, a_half, FFT_NB, c), lambda i, j: (i, 0, j, 0))
    sspec = pl.BlockSpec((1, 2, FFT_KA_PAD, FFT_NB * c), lambda i, j: (i, 0, 0, j))
    view = lambda a: a.reshape(n, a_half, FFT_R, c)
    y_shape = jax.ShapeDtypeStruct((n, a_half, FFT_R, c), F32)
    s_shape = jax.ShapeDtypeStruct(t.shape, F32)
    outs = pl.pallas_call(
        _inv1_body,
        out_shape=[y_shape, s_shape] if with_spectrum else [y_shape],
        grid=(n, FFT_R // FFT_NB),
        in_specs=[pl.BlockSpec(m_inv.shape, lambda i, j: (0, 0)),
                  pl.BlockSpec(m_fwd.shape, lambda i, j: (0, 0)),
                  sspec, uspec, uspec,
                  pl.BlockSpec((1, c), lambda i, j: (0, 0))],
        out_specs=[uspec, sspec] if with_spectrum else [uspec],
        compiler_params=_cparams(("parallel", "parallel")),
        name="conv_inv1",
    )(m_inv, m_fwd, t, view(u), view(xg), sk)
    y = outs[0].reshape(n, seq, c)
    return (y, outs[1]) if with_spectrum else (y, None)


def _hyena(v, x1, x2, f_w1, f_b1, f_w2, f_b2, f_w3, f_freq, skip):
    _, seq, c = v.shape
    assert 2 * seq == FFT_A * FFT_R
    m_fwd, m_inv, g2, g2h = _conv_dft_constants()
    taps, l1 = _hyena_filter_taps(seq, f_w1, f_b1, f_w2, f_b2, f_w3, f_freq, c)
    kf = _filter_spectrum(_conv_fwd1(taps, m_fwd), l1, g2, c)
    y, s = v, _conv_fwd1(v, m_fwd)
    gates = (x1, x2)
    for order, xg in enumerate(gates):
        t = _conv_mid(s, kf, order, g2, g2h, c)
        y, s = _conv_inv1(t, y, xg, skip[order], m_inv, m_fwd, with_spectrum=order + 1 < len(gates))
    return y


def _mix_ffn_body(x_ref, a1_ref, a2_ref, wm_ref, gm_ref, g_ref, sh_ref, sc_ref, gt_ref,
                  w1_ref, w3_ref, w2_ref, o_ref, *, fb):
    c1 = a1_ref.shape[-1]
    mixed = _dot(a1_ref[0].astype(BF16), wm_ref[0:c1]) + _dot(a2_ref[0].astype(BF16), wm_ref[c1:])
    xm = x_ref[0] + gm_ref[0] * mixed
    h = _norm_mod(xm, g_ref[...], sh_ref[0], sc_ref[0]).astype(BF16)
    acc = None
    for lo in range(0, w1_ref.shape[1], fb):
        a = _dot(h, w1_ref[:, lo:lo + fb])
        u = (a * jax.nn.sigmoid(a) * _dot(h, w3_ref[:, lo:lo + fb])).astype(BF16)
        part = _dot(u, w2_ref[lo:lo + fb, :])
        acc = part if acc is None else acc + part
    o_ref[0] = xm + gt_ref[0] * acc


def _mix_ffn(x, a1, a2, w_mix, gate_mix, g, shift, scale, gate, w1, w3, w2, tm=512, fb=2816):
    b, s, d = x.shape
    tok = lambda c: pl.BlockSpec((1, tm, c), lambda bi, i: (bi, i, 0))
    per_b = pl.BlockSpec((1, 1, d), lambda bi, i: (bi, 0, 0))
    const = lambda a: pl.BlockSpec(a.shape, lambda bi, i: (0, 0))
    resident = lambda a: pl.BlockSpec(a.shape, lambda bi, i: (0, 0), pipeline_mode=pl.Buffered(1))
    return pl.pallas_call(
        functools.partial(_mix_ffn_body, fb=fb),
        out_shape=jax.ShapeDtypeStruct(x.shape, F32),
        grid=(b, s // tm),
        in_specs=[tok(d), tok(a1.shape[-1]), tok(a2.shape[-1]), resident(w_mix), per_b,
                  const(g), per_b, per_b, per_b, resident(w1), resident(w3), resident(w2)],
        out_specs=tok(d),
        compiler_params=_cparams(("parallel", "parallel"), vmem_limit=LARGE_VMEM_LIMIT),
        name="mix_ffn",
    )(x, a1, a2, w_mix, gate_mix, g, shift, scale, gate, w1, w3, w2)


def _fm_constants(cg):
    j = np.arange(cg)[:, None]
    m = np.arange(cg)[None, :]
    ph = 2.0 * np.pi * j * m / cg
    w_cs = np.concatenate([np.cos(ph), np.sin(ph)], axis=1)
    d = np.arange(FM_A)[:, None]
    a = np.arange(FM_A)[None, :]
    ph = 2.0 * np.pi * d * a / FM_A
    fr, fi = np.cos(ph), -np.sin(ph)
    m1 = np.block([[fr, fi], [fi, -fr]])
    n = FM_A * FM_A
    dd = np.arange(FM_A)[:, None, None]
    c = np.arange(FM_A)[None, :, None]
    b = np.arange(FM_A)[None, None, :]
    th = 2.0 * np.pi * (b * c / FM_A + b * dd / n)
    gcat = np.concatenate([np.cos(th), np.sin(th)], axis=2)
    return _mxu_const(w_cs), _mxu_const(m1), _mxu_const(gcat)


def _fm_front_body(x_ref, g_ref, sh_ref, sc_ref, w_ref, m_ref, o_ref, *, cg, nb):
    d = x_ref.shape[-1]
    xs = jnp.concatenate([x_ref[0, :, bb, :] for bb in range(nb)], axis=0)
    h = _norm_mod(xs, g_ref[...], sh_ref[0], sc_ref[0]).astype(BF16)
    pq = [_dot(h[:, grp * cg:(grp + 1) * cg], w_ref[...]) for grp in range(d // cg)]
    p = jnp.concatenate([t[:, 0:cg] for t in pq], axis=1)
    q = jnp.concatenate([t[:, cg:2 * cg] for t in pq], axis=1)
    for bb in range(nb):
        rows = slice(bb * FM_A, (bb + 1) * FM_A)
        res = _dot(m_ref[...], jnp.concatenate([p[rows], q[rows]], axis=0).astype(BF16))
        o_ref[0, 0, :, bb, :] = res[0:FM_A]
        o_ref[0, 1, :, bb, :] = res[FM_A:2 * FM_A]


def _fm_front(x, g, shift, scale, w_cs, m1, nb=8):
    b, s, d = x.shape
    cg = w_cs.shape[0]
    per_b = pl.BlockSpec((1, 1, d), lambda bi, j: (bi, 0, 0))
    const = lambda a: pl.BlockSpec(a.shape, lambda bi, j: (0, 0))
    return pl.pallas_call(
        functools.partial(_fm_front_body, cg=cg, nb=nb),
        out_shape=jax.ShapeDtypeStruct((b, 2, FM_A, s // FM_A, d), F32),
        grid=(b, s // FM_A // nb),
        in_specs=[pl.BlockSpec((1, FM_A, nb, d), lambda bi, j: (bi, 0, j, 0)),
                  const(g), per_b, per_b, const(w_cs), const(m1)],
        out_specs=pl.BlockSpec((1, 2, FM_A, nb, d), lambda bi, j: (bi, 0, 0, j, 0)),
        compiler_params=_cparams(("parallel", "parallel")),
        name="fm_front",
    )(x.reshape(b, FM_A, s // FM_A, d), g, shift, scale, w_cs, m1)


def _fm_s2_body(s_ref, g_ref, o_ref, *, dblk, scale):
    for i in range(dblk):
        s2 = jnp.concatenate([s_ref[0, 0, i], s_ref[0, 1, i]], axis=0).astype(BF16)
        o_ref[:, i, :] = _dot(g_ref[i], s2) * scale


def _fm_stage2(sv, gcat, seq, d, dblk=8):
    b = sv.shape[0]
    scale = 1.0 / math.sqrt(seq * (d // F_GROUPS))
    out = pl.pallas_call(
        functools.partial(_fm_s2_body, dblk=dblk, scale=scale),
        out_shape=jax.ShapeDtypeStruct((b * FM_A, dblk * (FM_A // dblk), d), F32),
        grid=(b, FM_A // dblk),
        in_specs=[pl.BlockSpec((1, 2, dblk, FM_A, d), lambda bi, j: (bi, 0, j, 0, 0)),
                  pl.BlockSpec((dblk, FM_A, 2 * FM_A), lambda bi, j: (j, 0, 0))],
        out_specs=pl.BlockSpec((FM_A, dblk, d), lambda bi, j: (bi, j, 0)),
        compiler_params=_cparams(("parallel", "parallel")),
        name="fm_stage2",
    )(sv, gcat)
    return out.reshape(b, seq, d)


def _fourier_mix(x, g, shift, scale):
    b, s, d = x.shape
    assert s == FM_A * FM_A
    w_cs, m1, gcat = _fm_constants(d // F_GROUPS)
    return _fm_stage2(_fm_front(x, g, shift, scale, w_cs, m1), gcat, s, d)


MOE_TM = 1024
DMA_WINDOW = 128


def _router_body(x_ref, yf_ref, wf_ref, gf_ref, g_ref, sh_ref, sc_ref, wr_ref, br_ref,
                 xo_ref, h_ref, meta_ref, gw_ref, cnt_ref, carry):
    i = pl.program_id(0)

    @pl.when(i == 0)
    def _():
        carry[...] = jnp.zeros_like(carry)

    xm = x_ref[...] + gf_ref[0] * _dot(yf_ref[...].astype(BF16), wf_ref[...])
    xo_ref[...] = xm
    h = _norm_mod(xm, g_ref[...], sh_ref[0], sc_ref[0])
    _rows_to_tiles(h_ref, h)
    h_hi = h.astype(BF16)
    h_lo = (h - h_hi.astype(F32)).astype(BF16)
    by_hi = _dot(h_hi, wr_ref[...])
    logits = by_hi[:, 0:LANES] + by_hi[:, LANES:] + _dot(h_lo, wr_ref[:, 0:LANES]) + br_ref[...]
    lane = lax.broadcasted_iota(jnp.int32, logits.shape, 1)
    nl = logits.shape[-1]
    m1 = jnp.max(logits, axis=-1, keepdims=True)
    i1 = jnp.min(jnp.where(logits == m1, lane, nl), axis=-1, keepdims=True)
    rest = jnp.where(lane == i1, 3.0 * NEG_INF, logits)
    m2 = jnp.max(rest, axis=-1, keepdims=True)
    i2 = jnp.min(jnp.where(rest == m2, lane, nl), axis=-1, keepdims=True)
    e = jnp.exp(m2 - m1)
    gw_ref[...] = jnp.where(lane == 0, 1.0 / (1.0 + e), jnp.where(lane == 1, e / (1.0 + e), 0.0))
    onehot = jnp.where((lane == i1) | (lane == i2), 1.0, 0.0)
    tm = onehot.shape[0]
    earlier = lax.broadcasted_iota(jnp.int32, (tm, tm), 0) > lax.broadcasted_iota(jnp.int32, (tm, tm), 1)
    excl = _dot(jnp.where(earlier, 1.0, 0.0).astype(BF16), onehot.astype(BF16)) + carry[...]
    r1 = jnp.sum(jnp.where(lane == i1, excl, 0.0), axis=-1, keepdims=True).astype(jnp.int32)
    r2 = jnp.sum(jnp.where(lane == i2, excl, 0.0), axis=-1, keepdims=True).astype(jnp.int32)
    meta = jnp.where(lane == 0, i1, jnp.where(lane == 1, i2, jnp.where(lane == 2, r1, jnp.where(lane == 3, r2, 0))))
    meta_ref[...] = meta.T[0:ROW_SL]
    carry[...] = carry[...] + jnp.sum(onehot, axis=0, keepdims=True)
    cnt_ref[...] = carry[...]


def _router(x, y_f, w_f, gate_f, g, shift, scale, w_router, b_router, tm=512):
    b, s, d = x.shape
    t = b * s
    ne = w_router.shape[1]
    wr = jnp.pad(w_router.astype(F32), ((0, 0), (0, LANES - ne)))
    wr_hi = wr.astype(BF16)
    wr = jnp.concatenate([wr_hi, (wr - wr_hi.astype(F32)).astype(BF16)], axis=1)
    br = jnp.pad(b_router.astype(F32).reshape(1, ne), ((0, 0), (0, LANES - ne)), constant_values=NEG_INF)
    spt = s // tm
    per_b = pl.BlockSpec((1, 1, d), lambda i: (i // spt, 0, 0))
    const = lambda a: pl.BlockSpec(a.shape, lambda i: (0, 0))
    tok = pl.BlockSpec((tm, d), lambda i: (i, 0))
    return pl.pallas_call(
        _router_body,
        out_shape=[jax.ShapeDtypeStruct((t, d), F32),
                   jax.ShapeDtypeStruct((t * ROW_SL, LANES), F32),
                   jax.ShapeDtypeStruct((ROW_SL, t), jnp.int32),
                   jax.ShapeDtypeStruct((t, LANES), F32),
                   jax.ShapeDtypeStruct((1, LANES), F32)],
        grid=(t // tm,),
        in_specs=[tok, tok, const(w_f), per_b, const(g), per_b, per_b, const(wr), const(br)],
        out_specs=[tok,
                   pl.BlockSpec((tm * ROW_SL, LANES), lambda i: (i, 0)),
                   pl.BlockSpec((ROW_SL, tm), lambda i: (0, i)),
                   pl.BlockSpec((tm, LANES), lambda i: (i, 0)),
                   pl.BlockSpec((1, LANES), lambda i: (0, 0))],
        scratch_shapes=[pltpu.VMEM((1, LANES), F32)],
        compiler_params=_cparams(("arbitrary",)),
        name="router",
    )(x.reshape(t, d), y_f.reshape(t, d), w_f, gate_f, g, shift, scale, wr, br)


def _moe_plan(meta, counts, ne, tm):
    i1, i2, r1, r2 = meta[0], meta[1], meta[2], meta[3]
    cnt = counts[0, :ne].astype(jnp.int32)
    padded = ((cnt + tm - 1) // tm) * tm
    ends = jnp.cumsum(padded)
    offs = ends - padded
    pick = lambda idx: sum(jnp.where(idx == e, offs[e], 0) for e in range(ne))
    pos = jnp.concatenate([pick(i1) + r1, pick(i2) + r2]).astype(jnp.int32)
    n_tiles = (2 * i1.shape[0]) // tm + ne
    n_used = (ends[ne - 1] // tm).astype(jnp.int32)
    tile_start = jnp.minimum(jnp.arange(n_tiles, dtype=jnp.int32), n_used - 1) * tm
    tile_expert = jnp.sum(tile_start[:, None] >= ends[None, :], axis=1).astype(jnp.int32)
    group_end = sum(jnp.where(tile_expert == e, offs[e] + cnt[e], 0) for e in range(ne))
    tile_rows = jnp.clip(group_end - tile_start, 0, tm).astype(jnp.int32)
    return pos, offs + cnt, padded - cnt, tile_expert, n_used.reshape(1), tile_rows


def _windowed_copies(n, start_copy, wait_one):
    def body(i, carry):
        @pl.when(i >= DMA_WINDOW)
        def _():
            wait_one()
        start_copy(i)
        return carry

    lax.fori_loop(0, n, body, 0)

    def drain(i, carry):
        wait_one()
        return carry

    lax.fori_loop(0, jnp.minimum(n, DMA_WINDOW), drain, 0)


def _tile_of(ref, row):
    return ref.at[pl.ds(pl.multiple_of(row * ROW_SL, ROW_SL), ROW_SL)]


def _tiles_to_rows(ref, n, first=0):
    return jnp.concatenate([ref[pl.ds(first * ROW_SL + sl, n, stride=ROW_SL), :] for sl in range(ROW_SL)], axis=1)


def _rows_to_tiles(ref, val):
    n = val.shape[0]
    for sl in range(ROW_SL):
        ref[pl.ds(sl, n, stride=ROW_SL), :] = val[:, sl * LANES:(sl + 1) * LANES]


def _dispatch_body(pos_ref, pad_start_ref, pad_n_ref, h_ref, xs_hbm, sem, *, n_tok, ne):
    i = pl.program_id(0)
    td = h_ref.shape[0] // ROW_SL
    base = i * td
    copy = lambda src, dst: pltpu.make_async_copy(_tile_of(h_ref, src), _tile_of(xs_hbm, dst), sem)
    wait_one = lambda: copy(0, 0).wait()

    def start_token(r, carry):
        copy(r, pos_ref[base + r]).start(priority=0)
        copy(r, pos_ref[n_tok + base + r]).start(priority=1)
        return carry

    lax.fori_loop(0, td, start_token, 0, unroll=8)
    whole_tile = pltpu.make_async_copy(h_ref, xs_hbm.at[pl.ds(0, td * ROW_SL)], sem)
    whole_tile.wait()
    whole_tile.wait()

    @pl.when(i == 0)
    def _():
        for e in range(ne):
            first = pad_start_ref[e]
            _windowed_copies(pad_n_ref[e], lambda r: copy(0, first + r).start(), wait_one)


def _moe_dispatch(h3, pos, pad_start, pad_n, n_rows, td=1024):
    n_tok = h3.shape[0] // ROW_SL
    ne = pad_start.shape[0]
    return pl.pallas_call(
        functools.partial(_dispatch_body, n_tok=n_tok, ne=ne),
        out_shape=jax.ShapeDtypeStruct((n_rows * ROW_SL, LANES), h3.dtype),
        grid_spec=pltpu.PrefetchScalarGridSpec(
            num_scalar_prefetch=3, grid=(n_tok // td,),
            in_specs=[pl.BlockSpec((td * ROW_SL, LANES), lambda i, p, ps, pn: (i, 0))],
            out_specs=pl.BlockSpec(memory_space=pl.ANY),
            scratch_shapes=[pltpu.SemaphoreType.DMA(())]),
        compiler_params=_cparams(("arbitrary",)),
        name="moe_dispatch",
    )(pos, pad_start, pad_n, h3)


def _moe_grouped_body(te_ref, nu_ref, tr_ref, xs_ref, w1_ref, w3_ref, w2_ref, y_ref, xb_scr, acc_scr):
    i = pl.program_id(0)
    j = pl.program_id(1)
    tm = xb_scr.shape[0]
    hm = tm // 2

    def expert_rows(nrows):
        h = xb_scr[0:nrows]
        a = _dot(h, w1_ref[0].astype(BF16))
        u = (a * jax.nn.sigmoid(a) * _dot(h, w3_ref[0].astype(BF16))).astype(BF16)
        part = _dot(u, w2_ref[0].astype(BF16))

        @pl.when(j == 0)
        def _():
            acc_scr[0:nrows] = part

        @pl.when(j > 0)
        def _():
            acc_scr[0:nrows] += part

    @pl.when(i < nu_ref[0])
    def _():
        @pl.when(j == 0)
        def _():
            xb_scr[...] = _tiles_to_rows(xs_ref, tm).astype(BF16)

        @pl.when(tr_ref[i] > hm)
        def _():
            expert_rows(tm)

        @pl.when(tr_ref[i] <= hm)
        def _():
            expert_rows(hm)

            @pl.when(j == 0)
            def _():
                acc_scr[hm:tm] = jnp.zeros((tm - hm, acc_scr.shape[1]), F32)

        @pl.when(j == pl.num_programs(1) - 1)
        def _():
            _rows_to_tiles(y_ref, acc_scr[...])


def _moe_grouped(xs, tile_expert, n_used, tile_rows, w1, w3, w2, tm, fb=512):
    ne, d, f = w1.shape
    n_rows = xs.shape[0] // ROW_SL
    nj = f // fb
    row_tile = lambda i, j, te, nu, tr: (jnp.maximum(jnp.minimum(i, nu[0] - 1), 0), 0)
    jj = lambda i, j, nu: jnp.where(i < nu[0], j, nj - 1)
    return pl.pallas_call(
        _moe_grouped_body,
        out_shape=jax.ShapeDtypeStruct(xs.shape, F32),
        grid_spec=pltpu.PrefetchScalarGridSpec(
            num_scalar_prefetch=3, grid=(n_rows // tm, nj),
            in_specs=[pl.BlockSpec((tm * ROW_SL, LANES), row_tile),
                      pl.BlockSpec((1, d, fb), lambda i, j, te, nu, tr: (te[i], 0, jj(i, j, nu))),
                      pl.BlockSpec((1, d, fb), lambda i, j, te, nu, tr: (te[i], 0, jj(i, j, nu))),
                      pl.BlockSpec((1, fb, d), lambda i, j, te, nu, tr: (te[i], jj(i, j, nu), 0))],
            out_specs=pl.BlockSpec((tm * ROW_SL, LANES), row_tile),
            scratch_shapes=[pltpu.VMEM((tm, d), BF16), pltpu.VMEM((tm, d), F32)]),
        compiler_params=_cparams(("arbitrary", "arbitrary")),
        name="moe_grouped",
    )(tile_expert, n_used, tile_rows, xs, w1, w3, w2)


def _moe_final_body(pos_ref, x_ref, y_hbm, gw_ref, gt_ref, fg_ref, o_ref, yg_scr, sem, *, n_tok):
    i = pl.program_id(0)
    tc = x_ref.shape[0]
    slot = i % 2

    def gather_tile(step, into):
        base = step * tc

        def start_token(r, carry):
            dst = yg_scr.at[into]
            pltpu.make_async_copy(_tile_of(y_hbm, pos_ref[base + r]), _tile_of(dst, r),
                                  sem.at[into]).start(priority=0)
            pltpu.make_async_copy(_tile_of(y_hbm, pos_ref[n_tok + base + r]), _tile_of(dst, tc + r),
                                  sem.at[into]).start(priority=1)
            return carry

        lax.fori_loop(0, tc, start_token, 0, unroll=8)

    @pl.when(i == 0)
    def _():
        gather_tile(0, 0)

    @pl.when(i + 1 < pl.num_programs(0))
    def _():
        gather_tile(i + 1, 1 - slot)

    pltpu.make_async_copy(y_hbm.at[pl.ds(0, 2 * tc * ROW_SL)], yg_scr.at[slot], sem.at[slot]).wait()
    gw = gw_ref[...]
    rows = yg_scr.at[slot]
    y = gw[:, 0:1] * _tiles_to_rows(rows, tc) + gw[:, 1:2] * _tiles_to_rows(rows, tc, first=tc)
    xo = x_ref[...] + gt_ref[0] * y
    ms = jnp.mean(xo * xo, axis=-1, keepdims=True)
    o_ref[...] = xo * lax.rsqrt(ms + EPS) * fg_ref[...]


def _moe_final(x, y, pos, gw, gt, final_g, tc=512):
    b, s, d = x.shape
    t = b * s
    spt = s // tc
    out = pl.pallas_call(
        functools.partial(_moe_final_body, n_tok=t),
        out_shape=jax.ShapeDtypeStruct((t, d), F32),
        grid_spec=pltpu.PrefetchScalarGridSpec(
            num_scalar_prefetch=1, grid=(t // tc,),
            in_specs=[pl.BlockSpec((tc, d), lambda i, p: (i, 0)),
                      pl.BlockSpec(memory_space=pl.ANY),
                      pl.BlockSpec((tc, LANES), lambda i, p: (i, 0)),
                      pl.BlockSpec((1, 1, d), lambda i, p: (i // spt, 0, 0)),
                      pl.BlockSpec(final_g.shape, lambda i, p: (0, 0))],
            out_specs=pl.BlockSpec((tc, d), lambda i, p: (i, 0)),
            scratch_shapes=[pltpu.VMEM((2, 2 * tc * ROW_SL, LANES), F32), pltpu.SemaphoreType.DMA((2,))]),
        compiler_params=_cparams(("arbitrary",)),
        name="moe_final",
    )(pos, x.reshape(t, d), y, gw, gt, final_g)
    return out.reshape(b, s, d)


def _moe_routed(x, y_f, w_f, gate_f, g, shift, scale, gt, final_g, w_router, b_router, w1, w3, w2):
    ne = w1.shape[0]
    tm = MOE_TM
    x1, h3, meta, gw, counts = _router(x, y_f, w_f, gate_f, g, shift, scale, w_router, b_router)
    pos, pad_start, pad_n, tile_expert, n_used, tile_rows = _moe_plan(meta, counts, ne, tm)
    assert x.shape[-1] == ROW_SL * LANES
    n_rows = (2 * (h3.shape[0] // ROW_SL) // tm + ne) * tm
    xs = _moe_dispatch(h3, pos, pad_start, pad_n, n_rows)
    y = _moe_grouped(xs, tile_expert, n_used, tile_rows, w1, w3, w2, tm)
    return _moe_final(x1.reshape(x.shape), y, pos, gw, gt, final_g)


def kernel(x, c, ctx, c_ctx, w_ada, b_ada, norm_g, w_in, hy_short_w, hy_short_b, hy_f_w1, hy_f_b1, hy_f_w2, hy_f_b2, hy_f_w3, hy_f_freq, hy_skip, na_rpb, w_mix_out, ffn_w1, ffn_w3, ffn_w2, w_fourier, w_router, b_router, moe_w1, moe_w3, moe_w2, final_g):
    b, s, d = x.shape
    depth = w_ada.shape[0]
    assert depth == 2, "layer 0 mixes with Hyena/attention, layer 1 with Fourier/MoE"
    c_hy = hy_skip.shape[-1]
    c_na = d - c_hy

    cvec = jnp.concatenate([c, c_ctx[None, :], jnp.zeros((8 - b - 1, d), F32)], axis=0)
    mods = _ada(cvec, w_ada, b_ada)

    def mod(layer, idx, ctx_row=False):
        m = mods[layer, :, idx * d:(idx + 1) * d]
        return m[b:b + 1, None, :] if ctx_row else m[0:b, None, :]

    row = lambda a: a.reshape(1, -1)

    w_in0 = w_in[0].astype(BF16)
    w_hy, w_qkv = w_in0[:, 0:3 * c_hy], w_in0[:, 3 * c_hy:]
    v, x1, x2, q, k, va = _inproj(x, row(norm_g[0, 0]), mod(0, 0), mod(0, 1), w_hy, w_qkv,
                                  hy_short_w[0], row(hy_short_b[0]))
    kc, vc = _ctxkv(ctx, row(norm_g[0, 0]), mod(0, 0, True), mod(0, 1, True), w_qkv[:, c_na:])
    y_na = _natt(q, k, va, kc, vc, _na_bias_table(na_rpb[0]))
    y_hy = _hyena(v, x1, x2, hy_f_w1[0], hy_f_b1[0], hy_f_w2[0], hy_f_b2[0], hy_f_w3[0],
                  hy_f_freq[0], hy_skip[0])
    x = _mix_ffn(x, y_hy, y_na, w_mix_out[0].astype(BF16), mod(0, 2),
                 row(norm_g[0, 1]), mod(0, 3), mod(0, 4), mod(0, 5),
                 ffn_w1[0].astype(BF16), ffn_w3[0].astype(BF16), ffn_w2[0].astype(BF16))

    y_f = _fourier_mix(x, row(norm_g[1, 0]), mod(1, 0), mod(1, 1))
    return _moe_routed(x, y_f, w_fourier[0].astype(BF16), mod(1, 2),
                       row(norm_g[1, 1]), mod(1, 3), mod(1, 4), mod(1, 5), row(final_g),
                       w_router[0], b_router[0],
                       moe_w1[0], moe_w3[0], moe_w2[0])
```
